```python
import jax, jax.numpy as jnp
from jax import lax
import numpy as np

D_MODEL = 2048
BATCH = 8
SEQ = 4096
DEPTH = 1

CHUNK = 64
D_MIX = D_MODEL
D_MLSTM = D_MIX // 2
MLSTM_HEADS = 4
MLSTM_HEAD_DIM = D_MLSTM // MLSTM_HEADS
D_LRU = D_MIX - D_MLSTM
LRU_BLOCKS = 8
LRU_BLOCK_DIM = D_LRU // LRU_BLOCKS
LRU_CONV = 4
LRU_C = 8.0
D_FF = 5632
FFN_CONV = 3
EPS = 1e-6
PROJ_COLS = 4 * D_MLSTM + 2 * MLSTM_HEADS + 2 * D_LRU

kernel_name = "hymba_mlstm_rglru_convffn"


def rmsnorm(x, g):
    xf = x.astype(jnp.float32)
    r = xf * lax.rsqrt(jnp.mean(xf * xf, axis=-1, keepdims=True) + EPS)
    return (r * g.astype(jnp.float32)).astype(x.dtype)


def causal_dwconv(x, w, b):
    K = w.shape[0]
    S = x.shape[1]
    xp = jnp.pad(x, ((0, 0), (K - 1, 0), (0, 0)))
    out = b + xp[:, 0:S, :] * w[0]
    for j in range(1, K):
        out = out + xp[:, j:j + S, :] * w[j]
    return out


def mlstm_chunkwise(q, k, v, i_pre, f_pre, head_g):
    B, S, H, dh = q.shape
    NC, L = S // CHUNK, CHUNK
    f32 = jnp.float32

    def to_chunks(t):
        t = t.astype(f32).reshape((B, NC, L, H) + t.shape[3:])
        return jnp.moveaxis(t, 3, 1)

    q = to_chunks(q) * (MLSTM_HEAD_DIM ** -0.5)
    k, v = to_chunks(k), to_chunks(v)
    ig = to_chunks(i_pre)
    logf = jax.nn.log_sigmoid(to_chunks(f_pre))
    b = jnp.cumsum(logf, axis=-1)
    b_tot = b[..., -1]

    w_end = b_tot[..., None] - b + ig
    m_loc = jnp.max(w_end, axis=-1)
    e_end = jnp.exp(w_end - m_loc[..., None])
    C_loc = jnp.einsum('bhcl,bhclk,bhclv->bhckv', e_end, k, v)
    n_loc = jnp.einsum('bhcl,bhclk->bhck', e_end, k)

    def step(carry, inp):
        C, n, m = carry
        bt, ml, Cl, nl = inp
        m_new = jnp.maximum(bt + m, ml)
        a = jnp.exp(bt + m - m_new)
        c = jnp.exp(ml - m_new)
        C_new = a[..., None, None] * C + c[..., None, None] * Cl
        n_new = a[..., None] * n + c[..., None] * nl
        return (C_new, n_new, m_new), (C, n, m)

    init = (jnp.zeros((B, H, dh, dh), f32), jnp.zeros((B, H, dh), f32), jnp.zeros((B, H), f32))
    xs = (jnp.moveaxis(b_tot, 2, 0), jnp.moveaxis(m_loc, 2, 0),
          jnp.moveaxis(C_loc, 2, 0), jnp.moveaxis(n_loc, 2, 0))
    _, (C_prev, n_prev, m_prev) = lax.scan(step, init, xs)
    C_prev = jnp.moveaxis(C_prev, 0, 2)
    n_prev = jnp.moveaxis(n_prev, 0, 2)
    m_prev = jnp.moveaxis(m_prev, 0, 2)

    causal = jnp.tril(jnp.ones((L, L), dtype=bool))
    D = b[..., :, None] - b[..., None, :] + ig[..., None, :]
    D = jnp.where(causal, D, -jnp.inf)
    m_inter = b + m_prev[..., None]
    m_t = jnp.maximum(m_inter, jnp.max(D, axis=-1))
    Wqk = jnp.einsum('bhctk,bhcsk->bhcts', q, k) * jnp.exp(D - m_t[..., None])
    e_inter = jnp.exp(m_inter - m_t)
    num = (jnp.einsum('bhcts,bhcsv->bhctv', Wqk, v)
           + e_inter[..., None] * jnp.einsum('bhctk,bhckv->bhctv', q, C_prev))
    den = jnp.sum(Wqk, axis=-1) + e_inter * jnp.einsum('bhctk,bhck->bhct', q, n_prev)
    h = num / jnp.maximum(jnp.abs(den), jnp.exp(-m_t))[..., None]

    h = h * lax.rsqrt(jnp.mean(h * h, axis=-1, keepdims=True) + EPS)
    h = h * head_g.astype(f32)[None, :, None, None, :]
    return jnp.moveaxis(h, 1, 3).reshape(B, S, H * dh)


def rglru(x_r, conv_w, conv_b, wa, ba, wx, bx, lam):
    B, S, _ = x_r.shape
    f32 = jnp.float32
    xc = causal_dwconv(x_r.astype(f32), conv_w.astype(f32), conv_b.astype(f32))
    xb = xc.reshape(B, S, LRU_BLOCKS, LRU_BLOCK_DIM)
    r = jax.nn.sigmoid(jnp.einsum('bsni,nij->bsnj', xb, wa.astype(f32)).reshape(B, S, D_LRU) + ba)
    i = jax.nn.sigmoid(jnp.einsum('bsni,nij->bsnj', xb, wx.astype(f32)).reshape(B, S, D_LRU) + bx)
    log_a = -LRU_C * r * jax.nn.softplus(-lam.astype(f32))
    a = jnp.exp(log_a)
    u = jnp.sqrt(-jnp.expm1(2.0 * log_a)) * (i * xc)

    def combine(e1, e2):
        a1, b1 = e1
        a2, b2 = e2
        return a1 * a2, a2 * b1 + b2

    _, h = lax.associative_scan(combine, (a, u), axis=1)
    return h


def _fwd_setup_inputs(seed: int = 0) -> dict:
    key = jax.random.key(seed)
    ks = jax.random.split(key, 24)
    f32 = jnp.float32

    def nrm(k, shape, scale):
        return jax.random.normal(k, shape, f32) * scale

    x = jax.random.normal(ks[0], (BATCH, SEQ, D_MODEL), f32)
    norm_mix_g = 1.0 + nrm(ks[1], (DEPTH, D_MODEL), 0.02)
    w_in = nrm(ks[2], (DEPTH, D_MODEL, PROJ_COLS), D_MODEL ** -0.5)
    i_bias = nrm(ks[3], (DEPTH, MLSTM_HEADS), 0.1)
    f_bias = jnp.linspace(3.0, 6.0, MLSTM_HEADS, dtype=f32)[None, :] + nrm(ks[4], (DEPTH, MLSTM_HEADS), 0.1)
    b_gate_m = jnp.concatenate([i_bias, f_bias], axis=-1)
    mlstm_norm_g = 1.0 + nrm(ks[5], (DEPTH, MLSTM_HEADS, MLSTM_HEAD_DIM), 0.02)
    lru_conv_w = nrm(ks[6], (DEPTH, LRU_CONV, D_LRU), LRU_CONV ** -0.5)
    lru_conv_b = nrm(ks[7], (DEPTH, D_LRU), 0.01)
    lru_wa = nrm(ks[8], (DEPTH, LRU_BLOCKS, LRU_BLOCK_DIM, LRU_BLOCK_DIM), LRU_BLOCK_DIM ** -0.5)
    lru_ba = nrm(ks[9], (DEPTH, D_LRU), 0.01)
    lru_wx = nrm(ks[10], (DEPTH, LRU_BLOCKS, LRU_BLOCK_DIM, LRU_BLOCK_DIM), LRU_BLOCK_DIM ** -0.5)
    lru_bx = nrm(ks[11], (DEPTH, D_LRU), 0.01)
    u = jax.random.uniform(ks[12], (DEPTH, D_LRU), f32, 0.9, 0.999)
    a_base = u ** (1.0 / LRU_C)
    lru_lambda = jnp.log(a_base) - jnp.log1p(-a_base)
    w_out = nrm(ks[13], (DEPTH, D_MIX, D_MODEL), D_MIX ** -0.5)
    norm_ffn_g = 1.0 + nrm(ks[14], (DEPTH, D_MODEL), 0.02)
    w_up = nrm(ks[15], (DEPTH, D_MODEL, 2 * D_FF), D_MODEL ** -0.5)
    ffn_conv_w = nrm(ks[16], (DEPTH, FFN_CONV, D_FF), FFN_CONV ** -0.5)
    ffn_conv_b = nrm(ks[17], (DEPTH, D_FF), 0.01)
    w_down = nrm(ks[18], (DEPTH, D_FF, D_MODEL), D_FF ** -0.5)
    norm_final_g = 1.0 + nrm(ks[19], (D_MODEL,), 0.02)
    return {"x": x, "norm_mix_g": norm_mix_g, "w_in": w_in, "b_gate_m": b_gate_m,
            "mlstm_norm_g": mlstm_norm_g, "lru_conv_w": lru_conv_w, "lru_conv_b": lru_conv_b,
            "lru_wa": lru_wa, "lru_ba": lru_ba, "lru_wx": lru_wx, "lru_bx": lru_bx,
            "lru_lambda": lru_lambda, "w_out": w_out, "norm_ffn_g": norm_ffn_g, "w_up": w_up,
            "ffn_conv_w": ffn_conv_w, "ffn_conv_b": ffn_conv_b, "w_down": w_down,
            "norm_final_g": norm_final_g}


def _fwd_reference(x, norm_mix_g, w_in, b_gate_m, mlstm_norm_g, lru_conv_w, lru_conv_b,
              lru_wa, lru_ba, lru_wx, lru_bx, lru_lambda, w_out, norm_ffn_g, w_up,
              ffn_conv_w, ffn_conv_b, w_down, norm_final_g):
    B, S, _ = x.shape
    H, dh = MLSTM_HEADS, MLSTM_HEAD_DIM
    split_at = [D_MLSTM, 2 * D_MLSTM, 3 * D_MLSTM, 4 * D_MLSTM,
                4 * D_MLSTM + 2 * H, 4 * D_MLSTM + 2 * H + D_LRU]
    for l in range(DEPTH):
        n1 = rmsnorm(x, norm_mix_g[l])
        proj = n1 @ w_in[l]
        q, k, v, o_pre, if_pre, x_r, g_r = jnp.split(proj, split_at, axis=-1)
        if_pre = if_pre.astype(jnp.float32) + b_gate_m[l].astype(jnp.float32)
        i_pre, f_pre = if_pre[..., :H], if_pre[..., H:]
        h_m = mlstm_chunkwise(q.reshape(B, S, H, dh), k.reshape(B, S, H, dh),
                              v.reshape(B, S, H, dh), i_pre, f_pre, mlstm_norm_g[l])
        h_m = jax.nn.sigmoid(o_pre.astype(jnp.float32)) * h_m
        h_r = rglru(x_r, lru_conv_w[l], lru_conv_b[l], lru_wa[l], lru_ba[l],
                    lru_wx[l], lru_bx[l], lru_lambda[l])
        h_r = h_r * jax.nn.gelu(g_r.astype(jnp.float32))
        mix = jnp.concatenate([h_m, h_r], axis=-1).astype(x.dtype)
        x = x + mix @ w_out[l]
        n2 = rmsnorm(x, norm_ffn_g[l])
        gate, up = jnp.split(n2 @ w_up[l], 2, axis=-1)
        gate = causal_dwconv(gate, ffn_conv_w[l], ffn_conv_b[l])
        x = x + (jax.nn.silu(gate) * up) @ w_down[l]
    return rmsnorm(x, norm_final_g)


import jax as _jax
import jax.numpy as _jnp

TWIN_FORMAT = 'train_step'
FWD_PARAMS = ['x', 'norm_mix_g', 'w_in', 'b_gate_m', 'mlstm_norm_g', 'lru_conv_w', 'lru_conv_b', 'lru_wa', 'lru_ba', 'lru_wx', 'lru_bx', 'lru_lambda', 'w_out', 'norm_ffn_g', 'w_up', 'ffn_conv_w', 'ffn_conv_b', 'w_down', 'norm_final_g']
TWIN_WEIGHTS = ['norm_mix_g', 'w_in', 'b_gate_m', 'mlstm_norm_g', 'lru_conv_w', 'lru_conv_b', 'lru_wa', 'lru_ba', 'lru_wx', 'lru_bx', 'lru_lambda', 'w_out', 'norm_ffn_g', 'w_up', 'ffn_conv_w', 'ffn_conv_b', 'w_down', 'norm_final_g']
TWIN_DIFF_INPUT = 'x'
TWIN_INPUTS = ['x', 'norm_mix_g', 'w_in', 'b_gate_m', 'mlstm_norm_g', 'lru_conv_w', 'lru_conv_b', 'lru_wa', 'lru_ba', 'lru_wx', 'lru_bx', 'lru_lambda', 'w_out', 'norm_ffn_g', 'w_up', 'ffn_conv_w', 'ffn_conv_b', 'w_down', 'norm_final_g', 'loss_target', 'm_norm_mix_g', 'm_w_in', 'm_b_gate_m', 'm_mlstm_norm_g', 'm_lru_conv_w', 'm_lru_conv_b', 'm_lru_wa', 'm_lru_ba', 'm_lru_wx', 'm_lru_bx', 'm_lru_lambda', 'm_w_out', 'm_norm_ffn_g', 'm_w_up', 'm_ffn_conv_w', 'm_ffn_conv_b', 'm_w_down', 'm_norm_final_g', 'v_norm_mix_g', 'v_w_in', 'v_b_gate_m', 'v_mlstm_norm_g', 'v_lru_conv_w', 'v_lru_conv_b', 'v_lru_wa', 'v_lru_ba', 'v_lru_wx', 'v_lru_bx', 'v_lru_lambda', 'v_w_out', 'v_norm_ffn_g', 'v_w_up', 'v_ffn_conv_w', 'v_ffn_conv_b', 'v_w_down', 'v_norm_final_g']
TWIN_OUTPUTS = ['loss', 'grad_x', 'grad_norm_mix_g', 'grad_w_in', 'grad_b_gate_m', 'grad_mlstm_norm_g', 'grad_lru_conv_w', 'grad_lru_conv_b', 'grad_lru_wa', 'grad_lru_ba', 'grad_lru_wx', 'grad_lru_bx', 'grad_lru_lambda', 'grad_w_out', 'grad_norm_ffn_g', 'grad_w_up', 'grad_ffn_conv_w', 'grad_ffn_conv_b', 'grad_w_down', 'grad_norm_final_g', 'delta_norm_mix_g', 'delta_w_in', 'delta_b_gate_m', 'delta_mlstm_norm_g', 'delta_lru_conv_w', 'delta_lru_conv_b', 'delta_lru_wa', 'delta_lru_ba', 'delta_lru_wx', 'delta_lru_bx', 'delta_lru_lambda', 'delta_w_out', 'delta_norm_ffn_g', 'delta_w_up', 'delta_ffn_conv_w', 'delta_ffn_conv_b', 'delta_w_down', 'delta_norm_final_g', 'new_m_norm_mix_g', 'new_m_w_in', 'new_m_b_gate_m', 'new_m_mlstm_norm_g', 'new_m_lru_conv_w', 'new_m_lru_conv_b', 'new_m_lru_wa', 'new_m_lru_ba', 'new_m_lru_wx', 'new_m_lru_bx', 'new_m_lru_lambda', 'new_m_w_out', 'new_m_norm_ffn_g', 'new_m_w_up', 'new_m_ffn_conv_w', 'new_m_ffn_conv_b', 'new_m_w_down', 'new_m_norm_final_g', 'new_v_norm_mix_g', 'new_v_w_in', 'new_v_b_gate_m', 'new_v_mlstm_norm_g', 'new_v_lru_conv_w', 'new_v_lru_conv_b', 'new_v_lru_wa', 'new_v_lru_ba', 'new_v_lru_wx', 'new_v_lru_bx', 'new_v_lru_lambda', 'new_v_w_out', 'new_v_norm_ffn_g', 'new_v_w_up', 'new_v_ffn_conv_w', 'new_v_ffn_conv_b', 'new_v_w_down', 'new_v_norm_final_g']
TWIN_LEAF_KINDS = {'loss': 'loss', 'grad_x': 'grad_x', 'grad_norm_mix_g': 'grad_w', 'grad_w_in': 'grad_w', 'grad_b_gate_m': 'grad_w', 'grad_mlstm_norm_g': 'grad_w', 'grad_lru_conv_w': 'grad_w', 'grad_lru_conv_b': 'grad_w', 'grad_lru_wa': 'grad_w', 'grad_lru_ba': 'grad_w', 'grad_lru_wx': 'grad_w', 'grad_lru_bx': 'grad_w', 'grad_lru_lambda': 'grad_w', 'grad_w_out': 'grad_w', 'grad_norm_ffn_g': 'grad_w', 'grad_w_up': 'grad_w', 'grad_ffn_conv_w': 'grad_w', 'grad_ffn_conv_b': 'grad_w', 'grad_w_down': 'grad_w', 'grad_norm_final_g': 'grad_w', 'delta_norm_mix_g': 'delta_w', 'delta_w_in': 'delta_w', 'delta_b_gate_m': 'delta_w', 'delta_mlstm_norm_g': 'delta_w', 'delta_lru_conv_w': 'delta_w', 'delta_lru_conv_b': 'delta_w', 'delta_lru_wa': 'delta_w', 'delta_lru_ba': 'delta_w', 'delta_lru_wx': 'delta_w', 'delta_lru_bx': 'delta_w', 'delta_lru_lambda': 'delta_w', 'delta_w_out': 'delta_w', 'delta_norm_ffn_g': 'delta_w', 'delta_w_up': 'delta_w', 'delta_ffn_conv_w': 'delta_w', 'delta_ffn_conv_b': 'delta_w', 'delta_w_down': 'delta_w', 'delta_norm_final_g': 'delta_w', 'new_m_norm_mix_g': 'new_m', 'new_m_w_in': 'new_m', 'new_m_b_gate_m': 'new_m', 'new_m_mlstm_norm_g': 'new_m', 'new_m_lru_conv_w': 'new_m', 'new_m_lru_conv_b': 'new_m', 'new_m_lru_wa': 'new_m', 'new_m_lru_ba': 'new_m', 'new_m_lru_wx': 'new_m', 'new_m_lru_bx': 'new_m', 'new_m_lru_lambda': 'new_m', 'new_m_w_out': 'new_m', 'new_m_norm_ffn_g': 'new_m', 'new_m_w_up': 'new_m', 'new_m_ffn_conv_w': 'new_m', 'new_m_ffn_conv_b': 'new_m', 'new_m_w_down': 'new_m', 'new_m_norm_final_g': 'new_m', 'new_v_norm_mix_g': 'new_v', 'new_v_w_in': 'new_v', 'new_v_b_gate_m': 'new_v', 'new_v_mlstm_norm_g': 'new_v', 'new_v_lru_conv_w': 'new_v', 'new_v_lru_conv_b': 'new_v', 'new_v_lru_wa': 'new_v', 'new_v_lru_ba': 'new_v', 'new_v_lru_wx': 'new_v', 'new_v_lru_bx': 'new_v', 'new_v_lru_lambda': 'new_v', 'new_v_w_out': 'new_v', 'new_v_norm_ffn_g': 'new_v', 'new_v_w_up': 'new_v', 'new_v_ffn_conv_w': 'new_v', 'new_v_ffn_conv_b': 'new_v', 'new_v_w_down': 'new_v', 'new_v_norm_final_g': 'new_v'}


def _forward(args):
    return _fwd_reference(*[args[k] for k in FWD_PARAMS])


def _output_shape():
    def fwd():
        inp = _fwd_setup_inputs(0)
        return _fwd_reference(*[inp[k] for k in FWD_PARAMS])
    out = _jax.eval_shape(fwd)
    return out.shape, out.dtype

N_MICROBATCH = 1
ADAM_LR = 0.001
ADAM_B1 = 0.9
ADAM_B2 = 0.999
ADAM_EPS = 1e-08
ADAM_WD = 0.01
ADAM_STEP = 10
PER_EXAMPLE_BATCH_AXIS = {'x': 0, 'loss_target': 0}
SHARED_INPUTS = []
_WEIGHT_DTYPES = {'norm_mix_g': _jnp.float32, 'w_in': _jnp.float32, 'b_gate_m': _jnp.float32, 'mlstm_norm_g': _jnp.float32, 'lru_conv_w': _jnp.float32, 'lru_conv_b': _jnp.float32, 'lru_wa': _jnp.float32, 'lru_ba': _jnp.float32, 'lru_wx': _jnp.float32, 'lru_bx': _jnp.float32, 'lru_lambda': _jnp.float32, 'w_out': _jnp.float32, 'norm_ffn_g': _jnp.float32, 'w_up': _jnp.float32, 'ffn_conv_w': _jnp.float32, 'ffn_conv_b': _jnp.float32, 'w_down': _jnp.float32, 'norm_final_g': _jnp.float32}
MOMENT_SCALE = {'norm_mix_g': 8.001464e-02, 'w_in': 4.575438e-02, 'b_gate_m': 2.082878e-01, 'mlstm_norm_g': 5.106421e-02, 'lru_conv_w': 3.873497e-02, 'lru_conv_b': 4.328388e-01, 'lru_wa': 1.259520e-02, 'lru_ba': 9.992526e-03, 'lru_wx': 2.272367e-02, 'lru_bx': 1.422778e-02, 'lru_lambda': 1.911475e-02, 'w_out': 4.309808e-02, 'norm_ffn_g': 6.431078e-02, 'w_up': 2.635806e-02, 'ffn_conv_w': 2.657708e-02, 'ffn_conv_b': 2.522937e-02, 'w_down': 4.298639e-02, 'norm_final_g': 1.598704e+01}


def _to_microbatches(a, axis):
    t = _jnp.moveaxis(a, axis, 0)
    t = t.reshape((N_MICROBATCH, t.shape[0] // N_MICROBATCH) + t.shape[1:])
    return _jnp.moveaxis(t, 1, axis + 1)


def setup_inputs(seed: int = 0) -> dict:
    inp = _fwd_setup_inputs(seed)
    key = _jax.random.fold_in(_jax.random.key(seed), 7919)
    shape, _ = _output_shape()
    out = dict(inp)
    out["loss_target"] = _jax.random.normal(_jax.random.fold_in(key, 0), shape, _jnp.float32)
    for i, name in enumerate(TWIN_WEIGHTS):
        w = inp[name].astype(_jnp.float32)
        if MOMENT_SCALE is None:
            s = _jnp.sqrt(_jnp.mean(_jnp.square(w)) + 1e-30)
        else:
            s = MOMENT_SCALE[name]
        km, kv = _jax.random.split(_jax.random.fold_in(key, i + 1))
        out[name] = w
        out["m_" + name] = s * _jax.random.normal(km, w.shape, _jnp.float32)
        out["v_" + name] = (s * s) * _jax.random.uniform(kv, w.shape, _jnp.float32, 0.5, 1.5)
    if N_MICROBATCH > 1:
        for name, axis in PER_EXAMPLE_BATCH_AXIS.items():
            out[name] = _to_microbatches(out[name], axis)
    return {'x': out['x'], 'norm_mix_g': out['norm_mix_g'], 'w_in': out['w_in'], 'b_gate_m': out['b_gate_m'], 'mlstm_norm_g': out['mlstm_norm_g'], 'lru_conv_w': out['lru_conv_w'], 'lru_conv_b': out['lru_conv_b'], 'lru_wa': out['lru_wa'], 'lru_ba': out['lru_ba'], 'lru_wx': out['lru_wx'], 'lru_bx': out['lru_bx'], 'lru_lambda': out['lru_lambda'], 'w_out': out['w_out'], 'norm_ffn_g': out['norm_ffn_g'], 'w_up': out['w_up'], 'ffn_conv_w': out['ffn_conv_w'], 'ffn_conv_b': out['ffn_conv_b'], 'w_down': out['w_down'], 'norm_final_g': out['norm_final_g'], 'loss_target': out['loss_target'], 'm_norm_mix_g': out['m_norm_mix_g'], 'm_w_in': out['m_w_in'], 'm_b_gate_m': out['m_b_gate_m'], 'm_mlstm_norm_g': out['m_mlstm_norm_g'], 'm_lru_conv_w': out['m_lru_conv_w'], 'm_lru_conv_b': out['m_lru_conv_b'], 'm_lru_wa': out['m_lru_wa'], 'm_lru_ba': out['m_lru_ba'], 'm_lru_wx': out['m_lru_wx'], 'm_lru_bx': out['m_lru_bx'], 'm_lru_lambda': out['m_lru_lambda'], 'm_w_out': out['m_w_out'], 'm_norm_ffn_g': out['m_norm_ffn_g'], 'm_w_up': out['m_w_up'], 'm_ffn_conv_w': out['m_ffn_conv_w'], 'm_ffn_conv_b': out['m_ffn_conv_b'], 'm_w_down': out['m_w_down'], 'm_norm_final_g': out['m_norm_final_g'], 'v_norm_mix_g': out['v_norm_mix_g'], 'v_w_in': out['v_w_in'], 'v_b_gate_m': out['v_b_gate_m'], 'v_mlstm_norm_g': out['v_mlstm_norm_g'], 'v_lru_conv_w': out['v_lru_conv_w'], 'v_lru_conv_b': out['v_lru_conv_b'], 'v_lru_wa': out['v_lru_wa'], 'v_lru_ba': out['v_lru_ba'], 'v_lru_wx': out['v_lru_wx'], 'v_lru_bx': out['v_lru_bx'], 'v_lru_lambda': out['v_lru_lambda'], 'v_w_out': out['v_w_out'], 'v_norm_ffn_g': out['v_norm_ffn_g'], 'v_w_up': out['v_w_up'], 'v_ffn_conv_w': out['v_ffn_conv_w'], 'v_ffn_conv_b': out['v_ffn_conv_b'], 'v_w_down': out['v_w_down'], 'v_norm_final_g': out['v_norm_final_g']}


def _loss(weights, diff, rest, loss_target):
    with _jax.named_scope("forward"):
        args = {**rest, TWIN_DIFF_INPUT: diff, **{k: w.astype(_WEIGHT_DTYPES[k]) for k, w in weights.items()}}
        y = _forward(args)
    with _jax.named_scope("loss_head"):
        err = _jnp.square(y.astype(_jnp.float32) - loss_target)
        return 0.5 * _jnp.sum(_jnp.mean(err, axis=-1)) if err.ndim else 0.5 * err


def _adamw(w, g, m, v):
    m = ADAM_B1 * m + (1.0 - ADAM_B1) * g
    v = ADAM_B2 * v + (1.0 - ADAM_B2) * _jnp.square(g)
    m_hat = m / (1.0 - ADAM_B1 ** ADAM_STEP)
    v_hat = v / (1.0 - ADAM_B2 ** ADAM_STEP)
    delta = -ADAM_LR * (m_hat / (_jnp.sqrt(v_hat) + ADAM_EPS) + ADAM_WD * w)
    return delta, m, v


def reference(x, norm_mix_g, w_in, b_gate_m, mlstm_norm_g, lru_conv_w, lru_conv_b, lru_wa, lru_ba, lru_wx, lru_bx, lru_lambda, w_out, norm_ffn_g, w_up, ffn_conv_w, ffn_conv_b, w_down, norm_final_g, loss_target, m_norm_mix_g, m_w_in, m_b_gate_m, m_mlstm_norm_g, m_lru_conv_w, m_lru_conv_b, m_lru_wa, m_lru_ba, m_lru_wx, m_lru_bx, m_lru_lambda, m_w_out, m_norm_ffn_g, m_w_up, m_ffn_conv_w, m_ffn_conv_b, m_w_down, m_norm_final_g, v_norm_mix_g, v_w_in, v_b_gate_m, v_mlstm_norm_g, v_lru_conv_w, v_lru_conv_b, v_lru_wa, v_lru_ba, v_lru_wx, v_lru_bx, v_lru_lambda, v_w_out, v_norm_ffn_g, v_w_up, v_ffn_conv_w, v_ffn_conv_b, v_w_down, v_norm_final_g):
    given = dict(x=x, norm_mix_g=norm_mix_g, w_in=w_in, b_gate_m=b_gate_m, mlstm_norm_g=mlstm_norm_g, lru_conv_w=lru_conv_w, lru_conv_b=lru_conv_b, lru_wa=lru_wa, lru_ba=lru_ba, lru_wx=lru_wx, lru_bx=lru_bx, lru_lambda=lru_lambda, w_out=w_out, norm_ffn_g=norm_ffn_g, w_up=w_up, ffn_conv_w=ffn_conv_w, ffn_conv_b=ffn_conv_b, w_down=w_down, norm_final_g=norm_final_g, loss_target=loss_target, m_norm_mix_g=m_norm_mix_g, m_w_in=m_w_in, m_b_gate_m=m_b_gate_m, m_mlstm_norm_g=m_mlstm_norm_g, m_lru_conv_w=m_lru_conv_w, m_lru_conv_b=m_lru_conv_b, m_lru_wa=m_lru_wa, m_lru_ba=m_lru_ba, m_lru_wx=m_lru_wx, m_lru_bx=m_lru_bx, m_lru_lambda=m_lru_lambda, m_w_out=m_w_out, m_norm_ffn_g=m_norm_ffn_g, m_w_up=m_w_up, m_ffn_conv_w=m_ffn_conv_w, m_ffn_conv_b=m_ffn_conv_b, m_w_down=m_w_down, m_norm_final_g=m_norm_final_g, v_norm_mix_g=v_norm_mix_g, v_w_in=v_w_in, v_b_gate_m=v_b_gate_m, v_mlstm_norm_g=v_mlstm_norm_g, v_lru_conv_w=v_lru_conv_w, v_lru_conv_b=v_lru_conv_b, v_lru_wa=v_lru_wa, v_lru_ba=v_lru_ba, v_lru_wx=v_lru_wx, v_lru_bx=v_lru_bx, v_lru_lambda=v_lru_lambda, v_w_out=v_w_out, v_norm_ffn_g=v_norm_ffn_g, v_w_up=v_w_up, v_ffn_conv_w=v_ffn_conv_w, v_ffn_conv_b=v_ffn_conv_b, v_w_down=v_w_down, v_norm_final_g=v_norm_final_g)
    weights = {n: given[n] for n in TWIN_WEIGHTS}
    shared = {n: given[n] for n in SHARED_INPUTS}
    per_example = {n: given[n] for n in ['x']}
    grad_fn = _jax.value_and_grad(_loss, argnums=(0, 1))

    def one_microbatch(ex, loss_target):
        ex = dict(ex)
        diff = ex.pop(TWIN_DIFF_INPUT)
        return grad_fn(weights, diff, {**shared, **ex}, loss_target)

    if N_MICROBATCH == 1:
        loss, (grad_w, grad_x) = one_microbatch(per_example, given["loss_target"])
    else:
        def body(carry, xs):
            loss_sum, grad_sum = carry
            l_k, (gw_k, gx_k) = one_microbatch(xs[0], xs[1])
            with _jax.named_scope("update"):
                return (loss_sum + l_k, _jax.tree.map(_jnp.add, grad_sum, gw_k)), gx_k

        init = (_jnp.zeros((), _jnp.float32), _jax.tree.map(_jnp.zeros_like, weights))
        (loss, grad_w), grad_x = _jax.lax.scan(body, init, (per_example, given["loss_target"]))
    with _jax.named_scope("update"):
        delta_w, new_m, new_v = {}, {}, {}
        for n in TWIN_WEIGHTS:
            delta_w[n], new_m[n], new_v[n] = _adamw(weights[n], grad_w[n], given["m_" + n], given["v_" + n])
    return (loss, grad_x, *[grad_w[n] for n in TWIN_WEIGHTS], *[delta_w[n] for n in TWIN_WEIGHTS],
            *[new_m[n] for n in TWIN_WEIGHTS], *[new_v[n] for n in TWIN_WEIGHTS])
```

```python
import functools

import jax
import jax.numpy as jnp
from jax import lax
from jax.experimental import pallas as pl
from jax.experimental.pallas import tpu as pltpu

F32 = jnp.float32
BF16 = jnp.bfloat16
MESH = pl.DeviceIdType.MESH

EPS = 1e-6
CHUNK = 64
HEADS = 4
HEAD_DIM = 256
D_MLSTM = HEADS * HEAD_DIM
LRU_BLOCKS = 8
LRU_BLOCK_DIM = 128
D_LRU = LRU_BLOCKS * LRU_BLOCK_DIM
LRU_C = 8.0
LRU_CONV = 4
FFN_CONV = 3
ADAM_LR = 0.001
ADAM_B1 = 0.9
ADAM_B2 = 0.999
ADAM_EPS = 1e-08
ADAM_WD = 0.01
ADAM_STEP = 10

N_CHIPS = 4
N_DEV = 8
LANES = 128
HALO = 8
PROJ_GATE_PAD = LANES
VMEM_LIMIT = 48 * 1024 * 1024


def _params(sem, vmem=VMEM_LIMIT):
    return pltpu.CompilerParams(dimension_semantics=sem, vmem_limit_bytes=vmem)


def _matmul(name, a, b, grid, a_spec, b_spec, o_spec, out_sds, contract, res=None, res_spec=None):
    nk = grid[2]
    acc_shape = tuple(d for d in o_spec.block_shape if d is not None)

    def body(*refs):
        if res is None:
            a_ref, b_ref, o_ref, acc_ref = refs
            r_ref = None
        else:
            a_ref, b_ref, r_ref, o_ref, acc_ref = refs
        k = pl.program_id(2)

        @pl.when(k == 0)
        def _():
            acc_ref[...] = jnp.zeros_like(acc_ref)

        acc_ref[...] += lax.dot_general(a_ref[...], b_ref[...], (contract, ((), ())),
                                        preferred_element_type=F32)

        @pl.when(k == nk - 1)
        def _():
            r = acc_ref[...]
            if r_ref is not None:
                r = r_ref[...] + r
            o_ref[...] = r.astype(o_ref.dtype)

    in_specs = [a_spec, b_spec] + ([] if res is None else [res_spec])
    args = (a, b) + (() if res is None else (res,))
    return pl.pallas_call(
        body, out_shape=out_sds, grid=grid, in_specs=in_specs, out_specs=o_spec,
        scratch_shapes=[pltpu.VMEM(acc_shape, F32)], name=name,
        compiler_params=_params(("parallel", "parallel", "arbitrary")),
    )(*args)


NN = ((1,), (0,))
NT = ((1,), (1,))
TN = ((0,), (0,))


def _mm_nn(name, a, b, tm, tn, tk, out_dtype=F32, res=None):
    m, k = a.shape
    n = b.shape[1]
    return _matmul(name, a, b, (m // tm, n // tn, k // tk),
                   pl.BlockSpec((tm, tk), lambda i, j, kk: (i, kk)),
                   pl.BlockSpec((tk, tn), lambda i, j, kk: (kk, j)),
                   pl.BlockSpec((tm, tn), lambda i, j, kk: (i, j)),
                   jax.ShapeDtypeStruct((m, n), out_dtype), NN,
                   res=res, res_spec=pl.BlockSpec((tm, tn), lambda i, j, kk: (i, j)))


def _mm_nt(name, a, b, tm, tn, tk, out_dtype=F32, res=None):
    m, k = a.shape
    n = b.shape[0]
    return _matmul(name, a, b, (m // tm, n // tn, k // tk),
                   pl.BlockSpec((tm, tk), lambda i, j, kk: (i, kk)),
                   pl.BlockSpec((tn, tk), lambda i, j, kk: (j, kk)),
                   pl.BlockSpec((tm, tn), lambda i, j, kk: (i, j)),
                   jax.ShapeDtypeStruct((m, n), out_dtype), NT,
                   res=res, res_spec=pl.BlockSpec((tm, tn), lambda i, j, kk: (i, j)))


def _mm_tn(name, a, b, tm, tn, tk, out_dtype=F32):
    k, m = a.shape
    n = b.shape[1]
    tk = min(tk, k)
    return _matmul(name, a, b, (m // tm, n // tn, k // tk),
                   pl.BlockSpec((tk, tm), lambda i, j, kk: (kk, i)),
                   pl.BlockSpec((tk, tn), lambda i, j, kk: (kk, j)),
                   pl.BlockSpec((tm, tn), lambda i, j, kk: (i, j)),
                   jax.ShapeDtypeStruct((m, n), out_dtype), TN)


def _up_shard(n):
    return 2 * (n % 2) + (n // 2) // 2, (n // 2) % 2


def _mm_up_fwd(name, a, wg_up, tm, tk):
    m, k = a.shape
    _, _, cols = wg_up.shape
    tn = cols // 2
    return _matmul(name, a, wg_up, (m // tm, 2 * N_CHIPS, k // tk),
                   pl.BlockSpec((tm, tk), lambda i, j, kk: (i, kk)),
                   pl.BlockSpec((None, tk, tn), lambda i, j, kk: (_up_shard(j)[0], kk, _up_shard(j)[1])),
                   pl.BlockSpec((tm, tn), lambda i, j, kk: (i, j)),
                   jax.ShapeDtypeStruct((m, 2 * N_CHIPS * tn), F32), NN)


def _mm_up_bwd_x(name, dgu, wg_up, tm, tn):
    m, _ = dgu.shape
    _, d, cols = wg_up.shape
    tk = cols // 2
    return _matmul(name, dgu, wg_up, (m // tm, d // tn, 2 * N_CHIPS),
                   pl.BlockSpec((tm, tk), lambda i, j, kk: (i, kk)),
                   pl.BlockSpec((None, tn, tk), lambda i, j, kk: (_up_shard(kk)[0], j, _up_shard(kk)[1])),
                   pl.BlockSpec((tm, tn), lambda i, j, kk: (i, j)),
                   jax.ShapeDtypeStruct((m, d), F32), NT)


def _mm_up_bwd_w(name, n2, dgu, tm, tk):
    s, d = n2.shape
    tk = min(tk, s)
    tn = dgu.shape[1] // (2 * N_CHIPS)
    return _matmul(name, n2, dgu, (d // tm, 2 * N_CHIPS, s // tk),
                   pl.BlockSpec((tk, tm), lambda i, j, kk: (kk, i)),
                   pl.BlockSpec((tk, tn), lambda i, j, kk: (kk, j)),
                   pl.BlockSpec((None, tm, tn), lambda i, j, kk: (_up_shard(j)[0], i, _up_shard(j)[1])),
                   jax.ShapeDtypeStruct((N_CHIPS, d, 2 * tn), F32), TN)


def _rmsnorm_fwd(name, x, g, tm=256):
    s, d = x.shape

    def body(x_ref, g_ref, n_ref, r_ref):
        xf = x_ref[...]
        r = lax.rsqrt(jnp.mean(xf * xf, axis=-1, keepdims=True) + EPS)
        n_ref[...] = ((xf * r) * g_ref[...]).astype(BF16)
        r_ref[...] = r

    return pl.pallas_call(
        body, grid=(s // tm,), name=name,
        in_specs=[pl.BlockSpec((tm, d), lambda i: (i, 0)), pl.BlockSpec((1, d), lambda i: (0, 0))],
        out_specs=[pl.BlockSpec((tm, d), lambda i: (i, 0)), pl.BlockSpec((tm, 1), lambda i: (i, 0))],
        out_shape=[jax.ShapeDtypeStruct((s, d), BF16), jax.ShapeDtypeStruct((s, 1), F32)],
        compiler_params=_params(("parallel",)),
    )(x, g)


def _rmsnorm_bwd(name, x, rstd, g, dn, dres, tm=256):
    s, d = x.shape

    def body(x_ref, r_ref, g_ref, dn_ref, dres_ref, dx_ref, dxb_ref, dg_ref):
        @pl.when(pl.program_id(0) == 0)
        def _():
            dg_ref[...] = jnp.zeros_like(dg_ref)

        r = r_ref[...]
        xhat = x_ref[...] * r
        dn_v = dn_ref[...]
        dxhat = dn_v * g_ref[...]
        dx = dres_ref[...] + r * (dxhat - xhat * jnp.mean(dxhat * xhat, axis=-1, keepdims=True))
        dx_ref[...] = dx
        dxb_ref[...] = dx.astype(BF16)
        dg_ref[...] += jnp.sum(dn_v * xhat, axis=0, keepdims=True)

    row = pl.BlockSpec((tm, d), lambda i: (i, 0))
    vec = pl.BlockSpec((1, d), lambda i: (0, 0))
    return pl.pallas_call(
        body, grid=(s // tm,), name=name,
        in_specs=[row, pl.BlockSpec((tm, 1), lambda i: (i, 0)), vec, row, row],
        out_specs=[row, row, vec],
        out_shape=[jax.ShapeDtypeStruct((s, d), F32), jax.ShapeDtypeStruct((s, d), BF16),
                   jax.ShapeDtypeStruct((1, d), F32)],
        compiler_params=_params(("arbitrary",)),
    )(x, rstd, g, dn, dres)


def _loss_head(name, x, g, target, tm=256):
    s, d = x.shape

    def body(x_ref, g_ref, t_ref, loss_ref, dx_ref, dxb_ref, dg_ref):
        @pl.when(pl.program_id(0) == 0)
        def _():
            dg_ref[...] = jnp.zeros_like(dg_ref)
            loss_ref[...] = jnp.zeros_like(loss_ref)

        xf = x_ref[...]
        gv = g_ref[...]
        r = lax.rsqrt(jnp.mean(xf * xf, axis=-1, keepdims=True) + EPS)
        xhat = xf * r
        err = xhat * gv - t_ref[...]
        loss_ref[...] += 0.5 * jnp.sum(jnp.mean(err * err, axis=-1, keepdims=True), axis=0, keepdims=True)
        dy = err * (1.0 / d)
        dxhat = dy * gv
        dx = r * (dxhat - xhat * jnp.mean(dxhat * xhat, axis=-1, keepdims=True))
        dx_ref[...] = dx
        dxb_ref[...] = dx.astype(BF16)
        dg_ref[...] += jnp.sum(dy * xhat, axis=0, keepdims=True)

    row = pl.BlockSpec((tm, d), lambda i: (i, 0))
    vec = pl.BlockSpec((1, d), lambda i: (0, 0))
    return pl.pallas_call(
        body, grid=(s // tm,), name=name,
        in_specs=[row, vec, row],
        out_specs=[pl.BlockSpec((1, 1), lambda i: (0, 0)), row, row, vec],
        out_shape=[jax.ShapeDtypeStruct((1, 1), F32), jax.ShapeDtypeStruct((s, d), F32),
                   jax.ShapeDtypeStruct((s, d), BF16), jax.ShapeDtypeStruct((1, d), F32)],
        compiler_params=_params(("arbitrary",)),
    )(x, g, target)


def _sigmoid(v):
    return 1.0 / (1.0 + jnp.exp(-v))


def _log_sigmoid(v):
    return jnp.minimum(v, 0.0) - jnp.log1p(jnp.exp(-jnp.abs(v)))


def _softplus(v):
    return jnp.maximum(v, 0.0) + jnp.log1p(jnp.exp(-jnp.abs(v)))


def _one_minus_exp(z):
    series = -z * (1.0 + z * (0.5 + z * (1.0 / 6.0 + z * (1.0 / 24.0 + z * (1.0 / 120.0)))))
    return jnp.where(z > -0.1, series, 1.0 - jnp.exp(z))


_GELU_K = 0.7978845608028654
_GELU_C = 0.044715


def _gelu(v):
    return 0.5 * v * (1.0 + jnp.tanh(_GELU_K * (v + _GELU_C * v * v * v)))


def _gelu_grad(v):
    t = jnp.tanh(_GELU_K * (v + _GELU_C * v * v * v))
    return 0.5 * (1.0 + t) + 0.5 * v * (1.0 - t * t) * _GELU_K * (1.0 + 3.0 * _GELU_C * v * v)


def _rows(shape):
    return lax.broadcasted_iota(jnp.int32, shape, 0)


def _cols(shape):
    return lax.broadcasted_iota(jnp.int32, shape, 1)


def _shift_down(v, prev, d):
    if d == 0:
        return v
    rolled = pltpu.roll(v, d, axis=0)
    head = jnp.where(_rows((HALO, v.shape[1])) >= d, rolled[:HALO], pltpu.roll(prev, d, axis=0))
    if v.shape[0] == HALO:
        return head
    return jnp.concatenate([head, rolled[HALO:]], axis=0)


def _shift_up(v, nxt, d):
    if d == 0:
        return v
    n = v.shape[0]
    rolled = pltpu.roll(v, n - d, axis=0)
    tail = jnp.where(_rows((HALO, v.shape[1])) < HALO - d, rolled[n - HALO:], pltpu.roll(nxt, HALO - d, axis=0))
    if n == HALO:
        return tail
    return jnp.concatenate([rolled[:n - HALO], tail], axis=0)


def _dot(a, b, contract):
    return lax.dot_general(a.astype(BF16), b.astype(BF16), (contract, ((), ())), preferred_element_type=F32)


def _mlstm_chunk_common(h, q_ref, k_ref, v_ref, gcol_ref, grow_ref, brow_ref, bcol_ref, m_prev):
    L = CHUNK
    sl = slice(h * HEAD_DIM, (h + 1) * HEAD_DIM)
    qh = q_ref[:, sl]
    kh = k_ref[:, sl]
    vh = v_ref[:, sl]
    qs = qh * (HEAD_DIM ** -0.5)
    gates = gcol_ref[...] + brow_ref[...]
    lane = _cols(gates.shape)
    ic = jnp.sum(jnp.where(lane == h, gates, 0.0), axis=1, keepdims=True)
    fc = jnp.sum(jnp.where(lane == HEADS + h, gates, 0.0), axis=1, keepdims=True)
    ir = grow_ref[h:h + 1, :] + bcol_ref[h:h + 1, :]
    fr = grow_ref[HEADS + h:HEADS + h + 1, :] + bcol_ref[HEADS + h:HEADS + h + 1, :]
    logf_c = _log_sigmoid(fc)
    logf_r = _log_sigmoid(fr)
    t_i = _rows((L, L))
    s_i = _cols((L, L))
    tri = t_i >= s_i
    b_c = jnp.sum(jnp.where(tri, logf_r, 0.0), axis=1, keepdims=True)
    b_r = jnp.sum(jnp.where(t_i <= s_i, logf_c, 0.0), axis=0, keepdims=True)
    btot = jnp.sum(logf_r, axis=1, keepdims=True)
    dmat = jnp.where(tri, b_c - b_r + ir, -jnp.inf)
    m_inter = b_c + m_prev
    m_t = jnp.maximum(m_inter, jnp.max(dmat, axis=1, keepdims=True))
    e_mat = jnp.exp(dmat - m_t)
    e_inter = jnp.exp(m_inter - m_t)
    wqk = _dot(qs, kh, NT) * e_mat
    w_end_r = btot - b_r + ir
    m_loc = jnp.max(w_end_r, axis=1, keepdims=True)
    e_end_c = jnp.exp(btot - b_c + ic - m_loc)
    m_new = jnp.maximum(btot + m_prev, m_loc)
    a_dec = jnp.exp(btot + m_prev - m_new)
    c_inj = jnp.exp(m_loc - m_new)
    return dict(qh=qh, kh=kh, vh=vh, qs=qs, fc=fc, tri=tri, t_i=t_i, s_i=s_i, m_t=m_t, e_mat=e_mat,
                e_inter=e_inter, wqk=wqk, e_end_c=e_end_c, m_new=m_new, a_dec=a_dec, c_inj=c_inj)


def _mlstm_fwd(proj, gates_t, bias_row, bias_col, head_g):
    s = proj.shape[0]
    nc = s // CHUNK
    L = CHUNK

    def body(q_ref, k_ref, v_ref, o_ref, gcol_ref, grow_ref, brow_ref, bcol_ref, hg_ref,
             out_ref, cprev_ref, nprev_ref, mprev_ref, c_scr, n_scr, m_scr):
        @pl.when(pl.program_id(0) == 0)
        def _():
            c_scr[...] = jnp.zeros_like(c_scr)
            n_scr[...] = jnp.zeros_like(n_scr)
            m_scr[...] = jnp.zeros_like(m_scr)

        for h in range(HEADS):
            sl = slice(h * HEAD_DIM, (h + 1) * HEAD_DIM)
            m_prev = m_scr[h:h + 1, 0:1]
            n_prev = n_scr[h:h + 1, :]
            c_prev = c_scr[h].astype(BF16)
            q = _mlstm_chunk_common(h, q_ref, k_ref, v_ref, gcol_ref, grow_ref, brow_ref, bcol_ref, m_prev)
            num = _dot(q["wqk"], q["vh"], NN) + q["e_inter"] * _dot(q["qs"], c_prev, NN)
            den = (jnp.sum(q["wqk"], axis=1, keepdims=True)
                   + q["e_inter"] * jnp.sum(q["qs"] * n_prev, axis=1, keepdims=True))
            hh = num / jnp.maximum(jnp.abs(den), jnp.exp(-q["m_t"]))
            hn = hh * lax.rsqrt(jnp.mean(hh * hh, axis=1, keepdims=True) + EPS) * hg_ref[h:h + 1, :]
            out_ref[:, sl] = (_sigmoid(o_ref[:, sl]) * hn).astype(BF16)
            cprev_ref[h] = c_prev
            nprev_ref[h:h + 1, :] = n_prev
            mprev_ref[h:h + 1, :] = jnp.broadcast_to(m_prev, (1, LANES))
            c_loc = _dot(q["kh"], q["e_end_c"] * q["vh"], TN)
            n_loc = jnp.sum(q["e_end_c"] * q["kh"], axis=0, keepdims=True)
            c_scr[h] = q["a_dec"] * c_scr[h] + q["c_inj"] * c_loc
            n_scr[h:h + 1, :] = q["a_dec"] * n_prev + q["c_inj"] * n_loc
            m_scr[h:h + 1, :] = jnp.broadcast_to(q["m_new"], (1, LANES))

    blk = lambda j: pl.BlockSpec((L, D_MLSTM), lambda c, j=j: (c, j))
    full = lambda shp: pl.BlockSpec(shp, lambda c: tuple(0 for _ in shp))
    return pl.pallas_call(
        body, grid=(nc,), name="mlstm_fwd",
        in_specs=[blk(0), blk(1), blk(2), blk(3),
                  pl.BlockSpec((L, LANES), lambda c: (c, (4 * D_MLSTM + 2 * D_LRU) // LANES)),
                  pl.BlockSpec((None, 2 * HEADS, L), lambda c: (c, 0, 0)),
                  full((1, LANES)), full((2 * HEADS, 1)), full((HEADS, HEAD_DIM))],
        out_specs=[pl.BlockSpec((L, D_MLSTM), lambda c: (c, 0)),
                   pl.BlockSpec((None, HEADS, HEAD_DIM, HEAD_DIM), lambda c: (c, 0, 0, 0)),
                   pl.BlockSpec((None, HEADS, HEAD_DIM), lambda c: (c, 0, 0)),
                   pl.BlockSpec((None, HEADS, LANES), lambda c: (c, 0, 0))],
        out_shape=[jax.ShapeDtypeStruct((s, D_MLSTM), BF16),
                   jax.ShapeDtypeStruct((nc, HEADS, HEAD_DIM, HEAD_DIM), BF16),
                   jax.ShapeDtypeStruct((nc, HEADS, HEAD_DIM), F32),
                   jax.ShapeDtypeStruct((nc, HEADS, LANES), F32)],
        scratch_shapes=[pltpu.VMEM((HEADS, HEAD_DIM, HEAD_DIM), F32), pltpu.VMEM((HEADS, HEAD_DIM), F32),
                        pltpu.VMEM((HEADS, LANES), F32)],
        compiler_params=_params(("arbitrary",)),
    )(proj, proj, proj, proj, proj, gates_t, bias_row, bias_col, head_g)


def _mlstm_bwd(proj, gates_t, bias_row, bias_col, head_g, cprev, nprev, mprev, dmix):
    s = proj.shape[0]
    nc = s // CHUNK
    L = CHUNK

    def body(q_ref, k_ref, v_ref, o_ref, gcol_ref, grow_ref, brow_ref, bcol_ref, hg_ref,
             cprev_ref, nprev_ref, mprev_ref, dmix_ref,
             dqkvo_ref, dgate_ref, dhg_ref, g_scr, gn_scr):
        @pl.when(pl.program_id(0) == 0)
        def _():
            g_scr[...] = jnp.zeros_like(g_scr)
            gn_scr[...] = jnp.zeros_like(gn_scr)
            dhg_ref[...] = jnp.zeros_like(dhg_ref)

        lane = _cols((L, LANES))
        dgate = jnp.zeros((L, LANES), F32)
        for h in range(HEADS):
            sl = slice(h * HEAD_DIM, (h + 1) * HEAD_DIM)
            m_prev = mprev_ref[h:h + 1, 0:1]
            n_prev = nprev_ref[h:h + 1, :]
            c_prev = cprev_ref[h]
            q = _mlstm_chunk_common(h, q_ref, k_ref, v_ref, gcol_ref, grow_ref, brow_ref, bcol_ref, m_prev)
            qh, kh, vh, qs, wqk, e_inter = q["qh"], q["kh"], q["vh"], q["qs"], q["wqk"], q["e_inter"]
            num_state = e_inter * _dot(qs, c_prev, NN)
            den_state = e_inter * jnp.sum(qs * n_prev, axis=1, keepdims=True)
            num = _dot(wqk, vh, NN) + num_state
            den = jnp.sum(wqk, axis=1, keepdims=True) + den_state
            floor = jnp.exp(-q["m_t"])
            denom = jnp.maximum(jnp.abs(den), floor)
            hh = num / denom
            rn = lax.rsqrt(jnp.mean(hh * hh, axis=1, keepdims=True) + EPS)
            hn_pre = hh * rn
            hg = hg_ref[h:h + 1, :]
            sg = _sigmoid(o_ref[:, sl])
            dout = dmix_ref[:, sl]
            d_o = dout * (hn_pre * hg) * sg * (1.0 - sg)
            dhn = dout * sg
            dhg_ref[h:h + 1, :] += jnp.sum(dhn * hn_pre, axis=0, keepdims=True)
            dhn_pre = dhn * hg
            dhh = rn * (dhn_pre - hn_pre * jnp.mean(dhn_pre * hn_pre, axis=1, keepdims=True))
            dnum = dhh / denom
            dden = jnp.where(jnp.abs(den) >= floor,
                             -jnp.sum(hh * dhh, axis=1, keepdims=True) / denom * jnp.sign(den), 0.0)
            dwqk = _dot(dnum, vh, NT) + dden
            dv = _dot(wqk, dnum, TN)
            dp = dwqk * q["e_mat"]
            dqs = _dot(dp, kh, NN) + e_inter * (_dot(dnum, c_prev, NT) + dden * n_prev)
            dk = _dot(dp, qs, TN)
            g_next = g_scr[h]
            gn_next = gn_scr[h:h + 1, :]
            w_state = q["e_end_c"] * q["c_inj"]
            dk_state = w_state * (_dot(vh, g_next, NT) + gn_next)
            dk = dk + dk_state
            dv = dv + w_state * _dot(kh, g_next, NN)
            dq = dqs * (HEAD_DIM ** -0.5)
            eye = q["t_i"] == q["s_i"]
            to_row = lambda col: jnp.sum(jnp.where(eye, col, 0.0), axis=0, keepdims=True)
            to_col = lambda row: jnp.sum(jnp.where(eye, row, 0.0), axis=1, keepdims=True)
            g_pair = dwqk * wqk
            rs_in = jnp.sum(g_pair, axis=1, keepdims=True)
            cs_in_r = jnp.sum(g_pair, axis=0, keepdims=True)
            rs_state = (jnp.sum(dnum * num_state, axis=1, keepdims=True) + dden * den_state)
            cs_state = jnp.sum(kh * dk_state, axis=1, keepdims=True)
            di_c = to_col(cs_in_r) + cs_state
            through = q["a_dec"] * (jnp.sum(jnp.sum(g_next * c_prev.astype(F32), axis=1, keepdims=True),
                                            axis=0, keepdims=True)
                                    + jnp.sum(gn_next * n_prev, axis=1, keepdims=True))
            ends_here = to_row(rs_in + rs_state) - cs_in_r
            da_c = (jnp.sum(jnp.where(q["s_i"] >= q["t_i"], ends_here, 0.0), axis=1, keepdims=True)
                    + jnp.sum(jnp.where(q["s_i"] < q["t_i"], to_row(cs_state), 0.0), axis=1, keepdims=True)
                    + through)
            df_c = da_c * _sigmoid(-q["fc"])
            dgate = dgate + jnp.where(lane == h, di_c, 0.0) + jnp.where(lane == HEADS + h, df_c, 0.0)
            dqkvo_ref[:, sl] = dq.astype(BF16)
            dqkvo_ref[:, D_MLSTM + h * HEAD_DIM:D_MLSTM + (h + 1) * HEAD_DIM] = dk.astype(BF16)
            dqkvo_ref[:, 2 * D_MLSTM + h * HEAD_DIM:2 * D_MLSTM + (h + 1) * HEAD_DIM] = dv.astype(BF16)
            dqkvo_ref[:, 3 * D_MLSTM + h * HEAD_DIM:3 * D_MLSTM + (h + 1) * HEAD_DIM] = d_o.astype(BF16)
            g_scr[h] = q["a_dec"] * g_next + _dot(e_inter * qs, dnum, TN)
            gn_scr[h:h + 1, :] = q["a_dec"] * gn_next + jnp.sum(e_inter * qs * dden, axis=0, keepdims=True)
        dgate_ref[...] = dgate

    rev = lambda c: nc - 1 - c
    blk = lambda j: pl.BlockSpec((L, D_MLSTM), lambda c, j=j: (rev(c), j))
    full = lambda shp: pl.BlockSpec(shp, lambda c: tuple(0 for _ in shp))
    return pl.pallas_call(
        body, grid=(nc,), name="mlstm_bwd",
        in_specs=[blk(0), blk(1), blk(2), blk(3),
                  pl.BlockSpec((L, LANES), lambda c: (rev(c), (4 * D_MLSTM + 2 * D_LRU) // LANES)),
                  pl.BlockSpec((None, 2 * HEADS, L), lambda c: (rev(c), 0, 0)),
                  full((1, LANES)), full((2 * HEADS, 1)), full((HEADS, HEAD_DIM)),
                  pl.BlockSpec((None, HEADS, HEAD_DIM, HEAD_DIM), lambda c: (rev(c), 0, 0, 0)),
                  pl.BlockSpec((None, HEADS, HEAD_DIM), lambda c: (rev(c), 0, 0)),
                  pl.BlockSpec((None, HEADS, LANES), lambda c: (rev(c), 0, 0)),
                  pl.BlockSpec((L, D_MLSTM), lambda c: (rev(c), 0))],
        out_specs=[pl.BlockSpec((L, 4 * D_MLSTM), lambda c: (rev(c), 0)),
                   pl.BlockSpec((L, LANES), lambda c: (rev(c), 0)),
                   full((HEADS, HEAD_DIM))],
        out_shape=[jax.ShapeDtypeStruct((s, 4 * D_MLSTM), BF16),
                   jax.ShapeDtypeStruct((s, LANES), F32),
                   jax.ShapeDtypeStruct((HEADS, HEAD_DIM), F32)],
        scratch_shapes=[pltpu.VMEM((HEADS, HEAD_DIM, HEAD_DIM), F32), pltpu.VMEM((HEADS, HEAD_DIM), F32)],
        compiler_params=_params(("arbitrary",)),
    )(proj, proj, proj, proj, proj, gates_t, bias_row, bias_col, head_g, cprev, nprev, mprev, dmix)


def _lru_gates(xc, wa_ref, wx_ref, ba, bx, lam):
    r = _sigmoid(_dot(xc, wa_ref[...], NN) + ba)
    ig = _sigmoid(_dot(xc, wx_ref[...], NN) + bx)
    sp = _softplus(-lam)
    log_a = (-LRU_C * r) * sp
    a = jnp.exp(log_a)
    mult = jnp.sqrt(_one_minus_exp(2.0 * log_a))
    return r, ig, sp, a, mult


def _lru_conv(xr, prev, w_ref, b):
    xc = b + _shift_down(xr, prev, 3) * w_ref[0:1, :]
    for j in range(1, LRU_CONV):
        xc = xc + _shift_down(xr, prev, LRU_CONV - 1 - j) * w_ref[j:j + 1, :]
    return xc


def _lru_fwd(proj, conv_w, conv_b, wa, wx, ba, bx, lam, tt=512):
    s = proj.shape[0]
    tt = min(tt, s)
    nt = s // tt
    B = LRU_BLOCK_DIM
    xr_col = 4 * D_MLSTM // B
    gr_col = (4 * D_MLSTM + D_LRU) // B

    def body(xr_ref, gr_ref, cw_ref, cb_ref, wa_ref, wx_ref, ba_ref, bx_ref, lam_ref,
             out_ref, h_ref, prev_scr, hcar_scr):
        @pl.when(pl.program_id(1) == 0)
        def _():
            prev_scr[...] = jnp.zeros_like(prev_scr)
            hcar_scr[...] = jnp.zeros_like(hcar_scr)

        xr = xr_ref[...]
        xc = _lru_conv(xr, prev_scr[...], cw_ref, cb_ref[...])
        prev_scr[...] = xr[tt - HALO:, :]
        _, ig, _, a, mult = _lru_gates(xc, wa_ref, wx_ref, ba_ref[...], bx_ref[...], lam_ref[...])
        u = mult * (ig * xc)
        rows = _rows((tt, B))
        acc_a, acc_b = a, u
        d = 1
        while d < tt:
            keep = rows >= d
            sh_a = jnp.where(keep, pltpu.roll(acc_a, d, axis=0), 1.0)
            sh_b = jnp.where(keep, pltpu.roll(acc_b, d, axis=0), 0.0)
            acc_b = acc_a * sh_b + acc_b
            acc_a = acc_a * sh_a
            d *= 2
        hv = acc_b + acc_a * hcar_scr[0:1, :]
        hcar_scr[...] = jnp.broadcast_to(hv[tt - 1:tt, :], hcar_scr.shape)
        h_ref[...] = hv
        out_ref[...] = (hv * _gelu(gr_ref[...])).astype(BF16)

    chan = lambda rws: pl.BlockSpec((rws, B), lambda n, i: (0, n))
    return pl.pallas_call(
        body, grid=(LRU_BLOCKS, nt), name="lru_fwd",
        in_specs=[pl.BlockSpec((tt, B), lambda n, i: (i, xr_col + n)),
                  pl.BlockSpec((tt, B), lambda n, i: (i, gr_col + n)),
                  chan(LRU_CONV), chan(1),
                  pl.BlockSpec((None, B, B), lambda n, i: (n, 0, 0)),
                  pl.BlockSpec((None, B, B), lambda n, i: (n, 0, 0)),
                  chan(1), chan(1), chan(1)],
        out_specs=[pl.BlockSpec((tt, B), lambda n, i: (i, n)), pl.BlockSpec((tt, B), lambda n, i: (i, n))],
        out_shape=[jax.ShapeDtypeStruct((s, D_LRU), BF16), jax.ShapeDtypeStruct((s, D_LRU), F32)],
        scratch_shapes=[pltpu.VMEM((HALO, B), F32), pltpu.VMEM((HALO, B), F32)],
        compiler_params=_params(("parallel", "arbitrary")),
    )(proj, proj, conv_w, conv_b, wa, wx, ba, bx, lam)


def _lru_bwd(proj, hsave, dmix, conv_w, conv_b, wa, wx, ba, bx, lam, tt=512):
    s = proj.shape[0]
    tt = min(tt, s)
    nt = s // tt
    B = LRU_BLOCK_DIM
    xr_col = 4 * D_MLSTM // B
    gr_col = (4 * D_MLSTM + D_LRU) // B
    dmix_col = D_MLSTM // B
    hpb = tt // HALO

    def body(xr_ref, xprev_ref, gr_ref, h_ref, hprev_ref, dmix_ref, cw_ref, cb_ref, wa_ref, wx_ref,
             ba_ref, bx_ref, lam_ref,
             dxr_ref, dgr_ref, dcw_ref, dcb_ref, dwa_ref, dwx_ref, dba_ref, dbx_ref, dlam_ref,
             gcar_scr, acar_scr, dxc_scr):
        i = pl.program_id(1)
        first_tile = i == nt - 1

        @pl.when(i == 0)
        def _():
            gcar_scr[...] = jnp.zeros_like(gcar_scr)
            acar_scr[...] = jnp.zeros_like(acar_scr)
            dxc_scr[...] = jnp.zeros_like(dxc_scr)
            for ref in (dcw_ref, dcb_ref, dwa_ref, dwx_ref, dba_ref, dbx_ref, dlam_ref):
                ref[...] = jnp.zeros_like(ref)

        xr = xr_ref[...]
        xprev = jnp.where(first_tile, 0.0, xprev_ref[...])
        hprev = jnp.where(first_tile, 0.0, hprev_ref[...])
        lam = lam_ref[...]
        xc = _lru_conv(xr, xprev, cw_ref, cb_ref[...])
        r, ig, sp, a, mult = _lru_gates(xc, wa_ref, wx_ref, ba_ref[...], bx_ref[...], lam)
        gr = gr_ref[...]
        hv = h_ref[...]
        dout = dmix_ref[...]
        dgr_ref[...] = (dout * hv * _gelu_grad(gr)).astype(BF16)
        dh = dout * _gelu(gr)
        rows = _rows((tt, B))
        acc_a = _shift_up(a, acar_scr[...], 1)
        acc_b = dh
        d = 1
        while d < tt:
            keep = rows < tt - d
            sh_a = jnp.where(keep, pltpu.roll(acc_a, tt - d, axis=0), 1.0)
            sh_b = jnp.where(keep, pltpu.roll(acc_b, tt - d, axis=0), 0.0)
            acc_b = acc_a * sh_b + acc_b
            acc_a = acc_a * sh_a
            d *= 2
        gv = acc_b + acc_a * gcar_scr[0:1, :]
        gcar_scr[...] = jnp.broadcast_to(gv[0:1, :], gcar_scr.shape)
        acar_scr[...] = jnp.broadcast_to(a[0:1, :], acar_scr.shape)
        h_before = _shift_down(hv, hprev, 1)
        da = gv * h_before
        dmult = gv * (ig * xc)
        dig = gv * mult * xc
        dxc = gv * mult * ig
        dlog_a = da * a - dmult * (a * a) / mult
        dr = dlog_a * (-LRU_C * sp)
        dlam_ref[...] += jnp.sum(dlog_a * (-LRU_C * r), axis=0, keepdims=True) * (-_sigmoid(-lam))
        dpre_r = dr * r * (1.0 - r)
        dpre_i = dig * ig * (1.0 - ig)
        dba_ref[...] += jnp.sum(dpre_r, axis=0, keepdims=True)
        dbx_ref[...] += jnp.sum(dpre_i, axis=0, keepdims=True)
        dwa_ref[...] += _dot(xc, dpre_r, TN)
        dwx_ref[...] += _dot(xc, dpre_i, TN)
        dxc = dxc + _dot(dpre_r, wa_ref[...], NT) + _dot(dpre_i, wx_ref[...], NT)
        dcb_ref[...] += jnp.sum(dxc, axis=0, keepdims=True)
        nxt = dxc_scr[...]
        dxr = jnp.zeros((tt, B), F32)
        for j in range(LRU_CONV):
            sft = LRU_CONV - 1 - j
            dcw_ref[j:j + 1, :] += jnp.sum(dxc * _shift_down(xr, xprev, sft), axis=0, keepdims=True)
            dxr = dxr + _shift_up(dxc, nxt, sft) * cw_ref[j:j + 1, :]
        dxc_scr[...] = dxc[:HALO, :]
        dxr_ref[...] = dxr.astype(BF16)

    rev = lambda i: nt - 1 - i
    tile = lambda col: pl.BlockSpec((tt, B), lambda n, i, col=col: (rev(i), col + n))
    halo = lambda col: pl.BlockSpec((HALO, B), lambda n, i, col=col: (jnp.maximum(rev(i) * hpb - 1, 0), col + n))
    chan = lambda rws: pl.BlockSpec((rws, B), lambda n, i: (0, n))
    wblk = pl.BlockSpec((None, B, B), lambda n, i: (n, 0, 0))
    return pl.pallas_call(
        body, grid=(LRU_BLOCKS, nt), name="lru_bwd",
        in_specs=[tile(xr_col), halo(xr_col), tile(gr_col), tile(0), halo(0), tile(dmix_col),
                  chan(LRU_CONV), chan(1), wblk, wblk, chan(1), chan(1), chan(1)],
        out_specs=[tile(0), tile(0), chan(LRU_CONV), chan(1), wblk, wblk, chan(1), chan(1), chan(1)],
        out_shape=[jax.ShapeDtypeStruct((s, D_LRU), BF16), jax.ShapeDtypeStruct((s, D_LRU), BF16),
                   jax.ShapeDtypeStruct((LRU_CONV, D_LRU), F32), jax.ShapeDtypeStruct((1, D_LRU), F32),
                   jax.ShapeDtypeStruct((LRU_BLOCKS, B, B), F32), jax.ShapeDtypeStruct((LRU_BLOCKS, B, B), F32),
                   jax.ShapeDtypeStruct((1, D_LRU), F32), jax.ShapeDtypeStruct((1, D_LRU), F32),
                   jax.ShapeDtypeStruct((1, D_LRU), F32)],
        scratch_shapes=[pltpu.VMEM((HALO, B), F32), pltpu.VMEM((HALO, B), F32), pltpu.VMEM((HALO, B), F32)],
        compiler_params=_params(("parallel", "arbitrary")),
    )(proj, proj, proj, hsave, hsave, dmix, conv_w, conv_b, wa, wx, ba, bx, lam)


def _ffn_conv(gp, prev, w_ref, b):
    g = b + _shift_down(gp, prev, 2) * w_ref[0:1, :]
    for j in range(1, FFN_CONV):
        g = g + _shift_down(gp, prev, FFN_CONV - 1 - j) * w_ref[j:j + 1, :]
    return g


def _ffn_act_fwd(gu, conv_w, conv_b, tt=256):
    s = gu.shape[0]
    tt = min(tt, s)
    d_ff = conv_w.shape[1]
    tc = d_ff // N_CHIPS
    hpb = tt // HALO

    def body(g_ref, gprev_ref, u_ref, w_ref, b_ref, act_ref):
        prev = jnp.where(pl.program_id(0) == 0, 0.0, gprev_ref[...])
        gate = _ffn_conv(g_ref[...], prev, w_ref, b_ref[...])
        act_ref[...] = (gate * _sigmoid(gate) * u_ref[...]).astype(BF16)

    return pl.pallas_call(
        body, grid=(s // tt, N_CHIPS), name="ffn_act_fwd",
        in_specs=[pl.BlockSpec((tt, tc), lambda i, j: (i, 2 * j)),
                  pl.BlockSpec((HALO, tc), lambda i, j: (jnp.maximum(i * hpb - 1, 0), 2 * j)),
                  pl.BlockSpec((tt, tc), lambda i, j: (i, 2 * j + 1)),
                  pl.BlockSpec((FFN_CONV, tc), lambda i, j: (0, j)),
                  pl.BlockSpec((1, tc), lambda i, j: (0, j))],
        out_specs=pl.BlockSpec((tt, tc), lambda i, j: (i, j)),
        out_shape=jax.ShapeDtypeStruct((s, d_ff), BF16),
        compiler_params=_params(("parallel", "parallel")),
    )(gu, gu, gu, conv_w, conv_b)


def _ffn_act_bwd(gu, dact, conv_w, conv_b, tt=256):
    s = gu.shape[0]
    tt = min(tt, s)
    nt = s // tt
    d_ff = conv_w.shape[1]
    tc = d_ff // N_CHIPS
    hpb = tt // HALO

    def dgate_of(gate, up, da):
        sg = _sigmoid(gate)
        return da * up * (sg * (1.0 + gate * (1.0 - sg))), da * (gate * sg)

    def body(g_ref, gprev_ref, gnext_ref, u_ref, unext_ref, da_ref, danext_ref, w_ref, b_ref,
             dgu_ref, dw_ref, db_ref):
        i = pl.program_id(1)

        @pl.when(i == 0)
        def _():
            dw_ref[...] = jnp.zeros_like(dw_ref)
            db_ref[...] = jnp.zeros_like(db_ref)

        gp = g_ref[...]
        prev = jnp.where(i == 0, 0.0, gprev_ref[...])
        bias = b_ref[...]
        gate = _ffn_conv(gp, prev, w_ref, bias)
        dgate, dup = dgate_of(gate, u_ref[...], da_ref[...])
        gate_n = _ffn_conv(gnext_ref[...], gp[tt - HALO:, :], w_ref, bias)
        dgate_n, _ = dgate_of(gate_n, unext_ref[...], danext_ref[...])
        dgate_n = jnp.where(i == nt - 1, 0.0, dgate_n)
        db_ref[...] += jnp.sum(dgate, axis=0, keepdims=True)
        dgp = jnp.zeros((tt, tc), F32)
        for j in range(FFN_CONV):
            sft = FFN_CONV - 1 - j
            dw_ref[j:j + 1, :] += jnp.sum(dgate * _shift_down(gp, prev, sft), axis=0, keepdims=True)
            dgp = dgp + _shift_up(dgate, dgate_n, sft) * w_ref[j:j + 1, :]
        dgu_ref[:, :tc] = dgp.astype(BF16)
        dgu_ref[:, tc:] = dup.astype(BF16)

    tile = lambda half: pl.BlockSpec((tt, tc), lambda j, i, half=half: (i, 2 * j + half))
    hprev = lambda half: pl.BlockSpec((HALO, tc), lambda j, i, half=half: (jnp.maximum(i * hpb - 1, 0), 2 * j + half))
    hnext = lambda half: pl.BlockSpec(
        (HALO, tc), lambda j, i, half=half: (jnp.minimum((i + 1) * hpb, nt * hpb - 1), 2 * j + half))
    return pl.pallas_call(
        body, grid=(N_CHIPS, nt), name="ffn_act_bwd",
        in_specs=[tile(0), hprev(0), hnext(0), tile(1), hnext(1),
                  pl.BlockSpec((tt, tc), lambda j, i: (i, j)),
                  pl.BlockSpec((HALO, tc), lambda j, i: (jnp.minimum((i + 1) * hpb, nt * hpb - 1), j)),
                  pl.BlockSpec((FFN_CONV, tc), lambda j, i: (0, j)),
                  pl.BlockSpec((1, tc), lambda j, i: (0, j))],
        out_specs=[pl.BlockSpec((tt, 2 * tc), lambda j, i: (i, j)),
                   pl.BlockSpec((FFN_CONV, tc), lambda j, i: (0, j)),
                   pl.BlockSpec((1, tc), lambda j, i: (0, j))],
        out_shape=[jax.ShapeDtypeStruct((s, 2 * d_ff), BF16),
                   jax.ShapeDtypeStruct((FFN_CONV, d_ff), F32), jax.ShapeDtypeStruct((1, d_ff), F32)],
        compiler_params=_params(("parallel", "arbitrary")),
    )(gu, gu, gu, gu, gu, dact, dact, conv_w, conv_b)


def _colsum(name, a, tm=512):
    s, n = a.shape
    tm = min(tm, s)

    def body(a_ref, o_ref):
        @pl.when(pl.program_id(0) == 0)
        def _():
            o_ref[...] = jnp.zeros_like(o_ref)
        o_ref[...] += jnp.sum(a_ref[...], axis=0, keepdims=True)

    return pl.pallas_call(
        body, grid=(s // tm,), name=name,
        in_specs=[pl.BlockSpec((tm, n), lambda i: (i, 0))],
        out_specs=pl.BlockSpec((1, n), lambda i: (0, 0)),
        out_shape=jax.ShapeDtypeStruct((1, n), F32),
        compiler_params=_params(("arbitrary",)),
    )(a)


def _pick(n, *cands):
    for c in cands:
        if n % c == 0:
            return c
    raise ValueError(f"no tile for {n}")


def _local_step(x, target, w):
    s, d = x.shape
    nc = s // CHUNK
    tm = _pick(s, 1024, 512, 256)
    n_proj = w["w_in"].shape[1]
    tn_proj = _pick(n_proj, 896)
    d_ff = w["w_down"].shape[0]
    gate_col = 4 * D_MLSTM + 2 * D_LRU

    n1, rstd1 = _rmsnorm_fwd("norm_mix_fwd", x, w["norm_mix_g"])
    proj = _mm_nn("proj_fwd", n1, w["w_in"], tm, tn_proj, 512)
    gates = proj[:, gate_col:gate_col + 2 * HEADS]
    gates_t = gates.reshape(nc, CHUNK, 2 * HEADS).transpose(0, 2, 1)
    bias_row = jnp.pad(w["b_gate_m"], ((0, 0), (0, LANES - 2 * HEADS)))
    bias_col = w["b_gate_m"].reshape(2 * HEADS, 1)
    hm, cprev, nprev, mprev = _mlstm_fwd(proj, gates_t, bias_row, bias_col, w["mlstm_norm_g"])
    hr, hsave = _lru_fwd(proj, w["lru_conv_w"], w["lru_conv_b"], w["lru_wa"], w["lru_wx"],
                         w["lru_ba"], w["lru_bx"], w["lru_lambda"])
    mix = jnp.concatenate([hm, hr], axis=1)
    x1 = _mm_nn("out_fwd", mix, w["w_out"], tm, 1024, 512, res=x)
    n2, rstd2 = _rmsnorm_fwd("norm_ffn_fwd", x1, w["norm_ffn_g"])
    gu = _mm_up_fwd("up_fwd", n2, w["w_up"], tm, 512)
    act = _ffn_act_fwd(gu, w["ffn_conv_w"], w["ffn_conv_b"])
    x2 = _mm_nn("down_fwd", act, w["w_down"], tm, 1024, _pick(d_ff, 512), res=x1)
    loss, dx2, dx2b, g_norm_final = _loss_head("loss_head", x2, w["norm_final_g"], target)

    grads = {"norm_final_g": g_norm_final}
    dact = _mm_nt("down_bwd_x", dx2b, w["w_down"], tm, _pick(d_ff, 1408, 512), 1024)
    grads["w_down"] = _mm_tn("down_bwd_w", act, dx2b, _pick(d_ff, 1408, 512), 1024, 512)
    dgu, grads["ffn_conv_w"], grads["ffn_conv_b"] = _ffn_act_bwd(gu, dact, w["ffn_conv_w"], w["ffn_conv_b"])
    dn2 = _mm_up_bwd_x("up_bwd_x", dgu, w["w_up"], tm, 1024)
    grads["w_up"] = _mm_up_bwd_w("up_bwd_w", n2, dgu, 1024, 512)
    dx1, dx1b, grads["norm_ffn_g"] = _rmsnorm_bwd("norm_ffn_bwd", x1, rstd2, w["norm_ffn_g"], dn2, dx2)
    dmix = _mm_nt("out_bwd_x", dx1b, w["w_out"], tm, 1024, 1024)
    grads["w_out"] = _mm_tn("out_bwd_w", mix, dx1b, 1024, 1024, 512)
    dqkvo, dgate, grads["mlstm_norm_g"] = _mlstm_bwd(proj, gates_t, bias_row, bias_col, w["mlstm_norm_g"],
                                                     cprev, nprev, mprev, dmix)
    (dxr, dgr, grads["lru_conv_w"], grads["lru_conv_b"], grads["lru_wa"], grads["lru_wx"],
     grads["lru_ba"], grads["lru_bx"], grads["lru_lambda"]) = _lru_bwd(
        proj, hsave, dmix, w["lru_conv_w"], w["lru_conv_b"], w["lru_wa"], w["lru_wx"],
        w["lru_ba"], w["lru_bx"], w["lru_lambda"])
    grads["b_gate_m"] = _colsum("gate_bias_grad", dgate)[:, :2 * HEADS]
    dproj = jnp.concatenate([dqkvo, dxr, dgr, dgate.astype(BF16)], axis=1)
    dn1 = _mm_nt("proj_bwd_x", dproj, w["w_in"], tm, 1024, tn_proj)
    grads["w_in"] = _mm_tn("proj_bwd_w", n1, dproj, 1024, tn_proj, 512)
    grad_x, _, grads["norm_mix_g"] = _rmsnorm_bwd("norm_mix_bwd", x, rstd1, w["norm_mix_g"], dn1, dx1)
    return loss, grad_x, grads


_QKVO = 4 * D_MLSTM
_N_GATES = 2 * HEADS
_PROJ_COLS = _QKVO + _N_GATES + 2 * D_LRU
_PROJ_PAD = _QKVO + 2 * D_LRU + PROJ_GATE_PAD

WEIGHT_NAMES = ("norm_mix_g", "w_in", "b_gate_m", "mlstm_norm_g", "lru_conv_w", "lru_conv_b", "lru_wa", "lru_ba",
                "lru_wx", "lru_bx", "lru_lambda", "w_out", "norm_ffn_g", "w_up", "ffn_conv_w", "ffn_conv_b",
                "w_down", "norm_final_g")
BIG = ("w_in", "w_out", "w_up", "w_down")
SMALL_SHARDED = ("mlstm_norm_g", "lru_conv_w", "ffn_conv_w")
SMALL = tuple(n for n in WEIGHT_NAMES if n not in BIG)
SMALL_REPLICATED = tuple(n for n in SMALL if n not in SMALL_SHARDED)


def _w_in_to_local(w):
    pad = jnp.zeros((w.shape[0], PROJ_GATE_PAD - _N_GATES), w.dtype)
    return jnp.concatenate([w[:, :_QKVO], w[:, _QKVO + _N_GATES:], w[:, _QKVO:_QKVO + _N_GATES], pad], axis=1)


def _w_in_to_global(w):
    return jnp.concatenate([w[:, :_QKVO], w[:, _QKVO + 2 * D_LRU:_QKVO + 2 * D_LRU + _N_GATES],
                            w[:, _QKVO:_QKVO + 2 * D_LRU]], axis=1)


def _pack(arrs, rows):
    flat = jnp.concatenate([a.reshape(-1).astype(F32) for a in arrs])
    return jnp.pad(flat, (0, rows * LANES - flat.shape[0])).reshape(rows, LANES)


def _unpack(buf, shapes):
    flat = buf.reshape(-1)
    out, at = [], 0
    for shp in shapes:
        n = 1
        for d in shp:
            n *= d
        out.append(flat[at:at + n].reshape(shp))
        at += n
    return out


def _pack_rows(shapes):
    n = sum(functools.reduce(lambda a, b: a * b, shp, 1) for shp in shapes)
    return -(-n // (HALO * LANES)) * HALO


def _assemble_weights(g_in, g_out, g_up, g_down, small_sharded, replicated):
    w = dict(replicated)
    w["w_in"] = _w_in_to_local(jnp.concatenate([g_in[j] for j in range(N_CHIPS)], axis=1))
    w["w_out"] = g_out.reshape(-1, g_out.shape[-1])
    w["w_up"] = g_up
    w["w_down"] = g_down.reshape(-1, g_down.shape[-1])
    for name, v in small_sharded.items():
        w[name] = jnp.concatenate([v[j] for j in range(N_CHIPS)], axis=1)
    return w


def _full_weights_from_global(weights):
    shard = lambda a, axis: jnp.stack(jnp.split(a, N_CHIPS, axis=axis))
    rep = {n: weights[n].reshape(1, -1) if weights[n].ndim <= 2 and n != "b_gate_m" else weights[n]
           for n in SMALL_REPLICATED}
    rep["b_gate_m"] = weights["b_gate_m"].reshape(1, -1)
    return _assemble_weights(shard(weights["w_in"], 1).astype(BF16), shard(weights["w_out"], 0).astype(BF16),
                             shard(weights["w_up"], 1).astype(BF16), shard(weights["w_down"], 0).astype(BF16),
                             {n: shard(weights[n], 1) for n in SMALL_SHARDED}, rep)


def _grads_to_global(grads):
    g = dict(grads)
    g["w_in"] = _w_in_to_global(grads["w_in"])
    g["w_up"] = jnp.concatenate([grads["w_up"][j] for j in range(N_CHIPS)], axis=1)
    return g


ANY = pl.BlockSpec(memory_space=pl.ANY)


def _place():
    x, y, c = lax.axis_index("x"), lax.axis_index("y"), lax.axis_index("c")
    chips = [(1 - x, y), (x, 1 - y), (1 - x, 1 - y)]
    return x, y, c, 2 * x + y, chips


def _half_rows(n_rows, which):
    half = n_rows // 2
    return pl.ds(pl.multiple_of(which * half, 16), half)


def _rcopy(src, dst, send_sem, recv_sem, to):
    return pltpu.make_async_remote_copy(src_ref=src, dst_ref=dst, send_sem=send_sem, recv_sem=recv_sem,
                                        device_id=to, device_id_type=MESH)


def _gather_weights(shards, small):
    nb = len(shards)

    def body(*refs):
        srcs, small_ref = refs[:nb], refs[nb]
        dsts, small_out = refs[nb + 1:2 * nb + 1], refs[2 * nb + 1]
        send_sems, recv_sems, local_sems = refs[2 * nb + 2:]
        x, y, c, me, chips = _place()
        sibling = (x, y, 1 - c)
        mine = [_half_rows(s.shape[0], c) for s in srcs]
        other = [_half_rows(s.shape[0], 1 - c) for s in srcs]

        local = [pltpu.make_async_copy(srcs[w], dsts[w].at[me], local_sems.at[w]) for w in range(nb)]
        local.append(pltpu.make_async_copy(small_ref, small_out.at[me], local_sems.at[nb]))
        for cp in local:
            cp.start()
        sends = []
        for w in range(nb):
            for k, chip in enumerate(chips):
                sends.append(_rcopy(srcs[w].at[mine[w]], dsts[w].at[me, mine[w]],
                                    send_sems.at[w, k], recv_sems.at[w, k], (*chip, c)))
        for k, chip in enumerate(chips):
            sends.append(_rcopy(small_ref, small_out.at[me], send_sems.at[nb, k], recv_sems.at[nb, k], (*chip, c)))
        for cp in sends:
            cp.start()
        passed = []
        for w in range(nb):
            for k, chip in enumerate(chips):
                cid = 2 * chip[0] + chip[1]
                landed = dsts[w].at[cid, mine[w]]
                _rcopy(landed, landed, send_sems.at[w, k], recv_sems.at[w, k], (*chip, c)).wait_recv()
                fwd = _rcopy(landed, landed, send_sems.at[w, 3 + k], recv_sems.at[w, 3 + k], sibling)
                fwd.start()
                passed.append(fwd)
        for k, chip in enumerate(chips):
            cid = 2 * chip[0] + chip[1]
            _rcopy(small_ref, small_out.at[cid], send_sems.at[nb, k], recv_sems.at[nb, k], (*chip, c)).wait_recv()
        for w in range(nb):
            for k, chip in enumerate(chips):
                cid = 2 * chip[0] + chip[1]
                landed = dsts[w].at[cid, other[w]]
                _rcopy(landed, landed, send_sems.at[w, 3 + k], recv_sems.at[w, 3 + k], sibling).wait_recv()
        for cp in sends + passed:
            cp.wait_send()
        for cp in local:
            cp.wait()

    out_shape = [jax.ShapeDtypeStruct((N_CHIPS,) + s.shape, s.dtype) for s in shards]
    out_shape.append(jax.ShapeDtypeStruct((N_CHIPS,) + small.shape, small.dtype))
    return pl.pallas_call(
        body, name="gather_weights", out_shape=out_shape,
        in_specs=[ANY] * (nb + 1), out_specs=[ANY] * (nb + 1),
        scratch_shapes=[pltpu.SemaphoreType.DMA((nb + 1, 6)), pltpu.SemaphoreType.DMA((nb + 1, 6)),
                        pltpu.SemaphoreType.DMA((nb + 1,))],
    )(*shards, small)


def _pair_exchange(grads, small):
    nb = len(grads)

    def body(*refs):
        srcs, small_ref = refs[:nb], refs[nb]
        dsts, small_out = refs[nb + 1:2 * nb + 1], refs[2 * nb + 1]
        send_sems, recv_sems, small_send, small_recv, local_sem = refs[2 * nb + 2:]
        x, y, c, _, _ = _place()
        sibling = (x, y, 1 - c)
        my_id = 4 * x + 2 * y + c
        local = pltpu.make_async_copy(small_ref, small_out.at[my_id], local_sem)
        local.start()
        sends = []
        for w in range(nb):
            other = _half_rows(srcs[w].shape[1], 1 - c)
            sends.append(_rcopy(srcs[w].at[:, other], dsts[w], send_sems.at[w], recv_sems.at[w], sibling))
        for r in range(1, N_DEV):
            to = (1 - x if r & 4 else x, 1 - y if r & 2 else y, 1 - c if r & 1 else c)
            sends.append(_rcopy(small_ref, small_out.at[my_id], small_send.at[r - 1], small_recv.at[r - 1], to))
        for cp in sends:
            cp.start()
        for w in range(nb):
            _rcopy(dsts[w], dsts[w], send_sems.at[w], recv_sems.at[w], sibling).wait_recv()
        for r in range(1, N_DEV):
            frm = (1 - x if r & 4 else x, 1 - y if r & 2 else y, 1 - c if r & 1 else c)
            frm_id = 4 * frm[0] + 2 * frm[1] + frm[2]
            _rcopy(small_ref, small_out.at[frm_id], small_send.at[r - 1], small_recv.at[r - 1], frm).wait_recv()
        for cp in sends:
            cp.wait_send()
        local.wait()

    out_shape = [jax.ShapeDtypeStruct((g.shape[0], g.shape[1] // 2, g.shape[2]), g.dtype) for g in grads]
    out_shape.append(jax.ShapeDtypeStruct((N_DEV,) + small.shape, small.dtype))
    return pl.pallas_call(
        body, name="pair_exchange", out_shape=out_shape,
        in_specs=[ANY] * (nb + 1), out_specs=[ANY] * (nb + 1),
        scratch_shapes=[pltpu.SemaphoreType.DMA((nb,)), pltpu.SemaphoreType.DMA((nb,)),
                        pltpu.SemaphoreType.DMA((N_DEV - 1,)), pltpu.SemaphoreType.DMA((N_DEV - 1,)),
                        pltpu.SemaphoreType.DMA(())],
    )(*grads, small)


def _chip_exchange(partials):
    nb = len(partials)

    def body(*refs):
        srcs, dsts = refs[:nb], refs[nb:2 * nb]
        send_sems, recv_sems = refs[2 * nb:]
        _, _, c, me, chips = _place()
        sends = []
        for w in range(nb):
            for k, chip in enumerate(chips):
                cid = 2 * chip[0] + chip[1]
                sends.append(_rcopy(srcs[w].at[cid], dsts[w].at[me], send_sems.at[w, k], recv_sems.at[w, k],
                                    (*chip, c)))
        for cp in sends:
            cp.start()
        for w in range(nb):
            for k, chip in enumerate(chips):
                cid = 2 * chip[0] + chip[1]
                _rcopy(srcs[w].at[cid], dsts[w].at[cid], send_sems.at[w, k], recv_sems.at[w, k],
                       (*chip, c)).wait_recv()
        for cp in sends:
            cp.wait_send()

    return pl.pallas_call(
        body, name="chip_exchange", out_shape=[jax.ShapeDtypeStruct(p.shape, p.dtype) for p in partials],
        in_specs=[ANY] * nb, out_specs=[ANY] * nb,
        scratch_shapes=[pltpu.SemaphoreType.DMA((nb, 3)), pltpu.SemaphoreType.DMA((nb, 3))],
    )(*partials)


def _pair_share(halves):
    nb = len(halves)

    def body(*refs):
        srcs, dsts = refs[:nb], refs[nb:2 * nb]
        send_sems, recv_sems, local_sems = refs[2 * nb:]
        x, y, c, _, _ = _place()
        sibling = (x, y, 1 - c)
        copies = []
        for w in range(nb):
            mine = _half_rows(dsts[w].shape[0], c)
            loc = pltpu.make_async_copy(srcs[w], dsts[w].at[mine], local_sems.at[w])
            loc.start()
            snd = _rcopy(srcs[w], dsts[w].at[mine], send_sems.at[w], recv_sems.at[w], sibling)
            snd.start()
            copies.append((loc, snd))
        for w in range(nb):
            other = _half_rows(dsts[w].shape[0], 1 - c)
            _rcopy(srcs[w], dsts[w].at[other], send_sems.at[w], recv_sems.at[w], sibling).wait_recv()
        for loc, snd in copies:
            snd.wait_send()
            loc.wait()

    return pl.pallas_call(
        body, name="pair_share",
        out_shape=[jax.ShapeDtypeStruct((2 * h.shape[0], h.shape[1]), h.dtype) for h in halves],
        in_specs=[ANY] * nb, out_specs=[ANY] * nb,
        scratch_shapes=[pltpu.SemaphoreType.DMA((nb,)), pltpu.SemaphoreType.DMA((nb,)),
                        pltpu.SemaphoreType.DMA((nb,))],
    )(*halves)


def _row_tile(rows):
    return _pick(rows, 128, 64, 16, 8)


def _pair_sum(name, idx, grad, recv):
    n, half, cols = recv.shape
    tr = _row_tile(half)
    nrb = half // tr

    def body(idx_ref, g_ref, r_ref, o_ref):
        o_ref[...] = (g_ref[...] + r_ref[...]).astype(BF16)

    return pl.pallas_call(
        body, name=name, out_shape=jax.ShapeDtypeStruct(recv.shape, BF16),
        grid_spec=pltpu.PrefetchScalarGridSpec(
            num_scalar_prefetch=1, grid=(n, nrb),
            in_specs=[pl.BlockSpec((None, tr, cols), lambda j, i, s: (j, s[0] * nrb + i, 0)),
                      pl.BlockSpec((None, tr, cols), lambda j, i, s: (j, i, 0))],
            out_specs=pl.BlockSpec((None, tr, cols), lambda j, i, s: (j, i, 0))),
        compiler_params=_params(("parallel", "parallel")),
    )(idx, grad, recv)


def _final_sum(name, idx, grad, recv, chip_sums):
    _, half, cols = recv.shape
    tr = _row_tile(half)
    nrb = half // tr

    def body(idx_ref, g_ref, r_ref, p1_ref, p2_ref, p3_ref, o_ref):
        acc = g_ref[...] + r_ref[...]
        for p_ref in (p1_ref, p2_ref, p3_ref):
            acc = acc + p_ref[...].astype(F32)
        o_ref[...] = acc

    slot = lambda which: pl.BlockSpec((None, tr, cols), lambda i, s, which=which: (s[which], i, 0))
    return pl.pallas_call(
        body, name=name, out_shape=jax.ShapeDtypeStruct((half, cols), F32),
        grid_spec=pltpu.PrefetchScalarGridSpec(
            num_scalar_prefetch=1, grid=(nrb,),
            in_specs=[pl.BlockSpec((None, tr, cols), lambda i, s: (s[1], s[0] * nrb + i, 0)),
                      slot(1), slot(2), slot(3), slot(4)],
            out_specs=pl.BlockSpec((tr, cols), lambda i, s: (i, 0))),
        compiler_params=_params(("parallel",)),
    )(idx, grad, recv, chip_sums, chip_sums, chip_sums)


def _small_sum(packs):
    n, rows, _ = packs.shape

    def body(p_ref, o_ref):
        acc = p_ref[0]
        for k in range(1, n):
            acc = acc + p_ref[k]
        o_ref[...] = acc

    return pl.pallas_call(
        body, name="small_sum", out_shape=jax.ShapeDtypeStruct((rows, LANES), F32),
        in_specs=[pl.BlockSpec(memory_space=pltpu.VMEM)], out_specs=pl.BlockSpec(memory_space=pltpu.VMEM),
        compiler_params=pltpu.CompilerParams(vmem_limit_bytes=VMEM_LIMIT),
    )(packs)


def _adamw(name, w, g, m, v):
    rows, cols = w.shape
    tr = _row_tile(rows)

    def body(w_ref, g_ref, m_ref, v_ref, d_ref, nm_ref, nv_ref):
        gv = g_ref[...]
        m_new = ADAM_B1 * m_ref[...] + (1.0 - ADAM_B1) * gv
        v_new = ADAM_B2 * v_ref[...] + (1.0 - ADAM_B2) * (gv * gv)
        m_hat = m_new / (1.0 - ADAM_B1 ** ADAM_STEP)
        v_hat = v_new / (1.0 - ADAM_B2 ** ADAM_STEP)
        d_ref[...] = -ADAM_LR * (m_hat / (jnp.sqrt(v_hat) + ADAM_EPS) + ADAM_WD * w_ref[...])
        nm_ref[...] = m_new
        nv_ref[...] = v_new

    blk = pl.BlockSpec((tr, cols), lambda i: (i, 0))
    sds = jax.ShapeDtypeStruct((rows, cols), F32)
    return pl.pallas_call(
        body, name=name, grid=(rows // tr,), in_specs=[blk] * 4, out_specs=[blk] * 3, out_shape=[sds] * 3,
        compiler_params=_params(("parallel",)),
    )(w, g, m, v)


def _train_step(x, target, W, M, V):
    xi, yi, ci = lax.axis_index("x"), lax.axis_index("y"), lax.axis_index("c")
    me = 2 * xi + yi
    big = {n: W[n][0] for n in BIG}

    sharded_shapes = [W[n].shape[1:] for n in SMALL_SHARDED]
    small_rows = _pack_rows(sharded_shapes)
    g_big_and_small = _gather_weights([big[n].astype(BF16) for n in BIG],
                                      _pack([W[n][0] for n in SMALL_SHARDED], small_rows))
    g_in, g_out, g_up, g_down, g_small = g_big_and_small
    per_chip = [_unpack(g_small[j], sharded_shapes) for j in range(N_CHIPS)]
    small_sharded = {n: jnp.stack([per_chip[j][i] for j in range(N_CHIPS)]) for i, n in enumerate(SMALL_SHARDED)}
    replicated = {n: (W[n].reshape(1, -1) if W[n].ndim <= 2 else W[n][0]) for n in SMALL_REPLICATED}
    full = _assemble_weights(g_in, g_out, g_up, g_down, small_sharded, replicated)

    loss, grad_x, grads = _local_step(x[0], target[0], full)
    loss = lax.psum(loss[0, 0], ("x", "y", "c"))

    g_in_l = _w_in_to_global(grads["w_in"])
    shard_major = [jnp.stack(jnp.split(g_in_l, N_CHIPS, axis=1)),
                   grads["w_out"].reshape((N_CHIPS, -1) + grads["w_out"].shape[1:]),
                   grads["w_up"],
                   grads["w_down"].reshape((N_CHIPS, -1) + grads["w_down"].shape[1:])]
    small_shapes = [W[n].shape[1:] if n not in SMALL_SHARDED else (W[n].shape[1], W[n].shape[2] * N_CHIPS)
                    for n in SMALL]
    small_shapes = [tuple(W[n].shape) if n == "norm_final_g" else shp for n, shp in zip(SMALL, small_shapes)]
    small_pack = _pack([grads[n] for n in SMALL], _pack_rows(small_shapes))
    *recv, small_all = _pair_exchange(shard_major, small_pack)
    others = [jnp.where(jnp.int32(i) >= me, i + 1, i) for i in range(N_CHIPS - 1)]
    idx = jnp.stack([ci, me] + others).astype(jnp.int32)
    partial = [_pair_sum(f"pair_sum_{n}", idx, g, r) for n, g, r in zip(BIG, shard_major, recv)]
    chip_sums = _chip_exchange(partial)
    halves = [_final_sum(f"final_sum_{n}", idx, g, r, p) for n, g, r, p in zip(BIG, shard_major, recv, chip_sums)]
    big_grads = dict(zip(BIG, _pair_share(halves)))

    small_grads = dict(zip(SMALL, _unpack(_small_sum(small_all), small_shapes)))
    for n in SMALL_SHARDED:
        width = W[n].shape[2]
        small_grads[n] = lax.dynamic_slice_in_dim(small_grads[n], me * width, width, axis=1)

    out_g, out_d, out_m, out_v = {}, {}, {}, {}
    for n in BIG:
        out_g[n] = big_grads[n][None]
        d, nm, nv = _adamw(f"adamw_{n}", big[n], big_grads[n], M[n][0], V[n][0])
        out_d[n], out_m[n], out_v[n] = d[None], nm[None], nv[None]
    local_shapes = [tuple(W[n].shape) for n in SMALL]
    rows = _pack_rows(local_shapes)
    packed = [_pack([src[n] for n in SMALL], rows) for src in (W, small_grads, M, V)]
    d, nm, nv = _adamw("adamw_small", *packed)
    for dst, buf in ((out_d, d), (out_m, nm), (out_v, nv)):
        dst.update(zip(SMALL, _unpack(buf, local_shapes)))
    for n in SMALL:
        out_g[n] = small_grads[n].reshape(W[n].shape)
    return (loss, grad_x[None], *[out_g[n] for n in WEIGHT_NAMES], *[out_d[n] for n in WEIGHT_NAMES],
            *[out_m[n] for n in WEIGHT_NAMES], *[out_v[n] for n in WEIGHT_NAMES])


def kernel(x, norm_mix_g, w_in, b_gate_m, mlstm_norm_g, lru_conv_w, lru_conv_b, lru_wa, lru_ba, lru_wx, lru_bx, lru_lambda, w_out, norm_ffn_g, w_up, ffn_conv_w, ffn_conv_b, w_down, norm_final_g, loss_target, m_norm_mix_g, m_w_in, m_b_gate_m, m_mlstm_norm_g, m_lru_conv_w, m_lru_conv_b, m_lru_wa, m_lru_ba, m_lru_wx, m_lru_bx, m_lru_lambda, m_w_out, m_norm_ffn_g, m_w_up, m_ffn_conv_w, m_ffn_conv_b, m_w_down, m_norm_final_g, v_norm_mix_g, v_w_in, v_b_gate_m, v_mlstm_norm_g, v_lru_conv_w, v_lru_conv_b, v_lru_wa, v_lru_ba, v_lru_wx, v_lru_bx, v_lru_lambda, v_w_out, v_norm_ffn_g, v_w_up, v_ffn_conv_w, v_ffn_conv_b, v_w_down, v_norm_final_g):
    W = dict(zip(WEIGHT_NAMES, (norm_mix_g, w_in, b_gate_m, mlstm_norm_g, lru_conv_w, lru_conv_b, lru_wa, lru_ba,
                                lru_wx, lru_bx, lru_lambda, w_out, norm_ffn_g, w_up, ffn_conv_w, ffn_conv_b,
                                w_down, norm_final_g)))
    M = dict(zip(WEIGHT_NAMES, (m_norm_mix_g, m_w_in, m_b_gate_m, m_mlstm_norm_g, m_lru_conv_w, m_lru_conv_b,
                                m_lru_wa, m_lru_ba, m_lru_wx, m_lru_bx, m_lru_lambda, m_w_out, m_norm_ffn_g,
                                m_w_up, m_ffn_conv_w, m_ffn_conv_b, m_w_down, m_norm_final_g)))
    V = dict(zip(WEIGHT_NAMES, (v_norm_mix_g, v_w_in, v_b_gate_m, v_mlstm_norm_g, v_lru_conv_w, v_lru_conv_b,
                                v_lru_wa, v_lru_ba, v_lru_wx, v_lru_bx, v_lru_lambda, v_w_out, v_norm_ffn_g,
                                v_w_up, v_ffn_conv_w, v_ffn_conv_b, v_w_down, v_norm_final_g)))
    return _train_step(x, loss_target, W, M, V)
```

```python
import functools

import jax
import jax.numpy as jnp
from jax import lax
from jax.experimental import pallas as pl
from jax.experimental.pallas import tpu as pltpu

F32 = jnp.float32
BF16 = jnp.bfloat16
MESH = pl.DeviceIdType.MESH

EPS = 1e-6
CHUNK = 64
HEADS = 4
HEAD_DIM = 256
D_MLSTM = HEADS * HEAD_DIM
LRU_BLOCKS = 8
LRU_BLOCK_DIM = 128
D_LRU = LRU_BLOCKS * LRU_BLOCK_DIM
LRU_C = 8.0
LRU_CONV = 4
FFN_CONV = 3
ADAM_LR = 0.001
ADAM_B1 = 0.9
ADAM_B2 = 0.999
ADAM_EPS = 1e-08
ADAM_WD = 0.01
ADAM_STEP = 10

N_CHIPS = 4
N_DEV = 8
LANES = 128
HALO = 8
PROJ_GATE_PAD = LANES
_QKVO = 4 * D_MLSTM
_N_GATES = 2 * HEADS
_PROJ_COLS = _QKVO + _N_GATES + 2 * D_LRU
_PROJ_PAD = _QKVO + 2 * D_LRU + PROJ_GATE_PAD
VMEM_LIMIT = 48 * 1024 * 1024
ANY = pl.BlockSpec(memory_space=pl.ANY)


def _params(sem, vmem=VMEM_LIMIT):
    return pltpu.CompilerParams(dimension_semantics=sem, vmem_limit_bytes=vmem)


def _matmul(name, a, b, grid, a_spec, b_spec, o_spec, out_sds, contract, res=None, res_spec=None):
    nk = grid[2]
    acc_shape = tuple(d for d in o_spec.block_shape if d is not None)

    def body(*refs):
        if res is None:
            a_ref, b_ref, o_ref, acc_ref = refs
            r_ref = None
        else:
            a_ref, b_ref, r_ref, o_ref, acc_ref = refs
        k = pl.program_id(2)

        @pl.when(k == 0)
        def _():
            acc_ref[...] = jnp.zeros_like(acc_ref)

        acc_ref[...] += lax.dot_general(a_ref[...], b_ref[...], (contract, ((), ())),
                                        preferred_element_type=F32)

        @pl.when(k == nk - 1)
        def _():
            r = acc_ref[...]
            if r_ref is not None:
                r = r_ref[...] + r
            o_ref[...] = r.astype(o_ref.dtype)

    in_specs = [a_spec, b_spec] + ([] if res is None else [res_spec])
    args = (a, b) + (() if res is None else (res,))
    return pl.pallas_call(
        body, out_shape=out_sds, grid=grid, in_specs=in_specs, out_specs=o_spec,
        scratch_shapes=[pltpu.VMEM(acc_shape, F32)], name=name,
        compiler_params=_params(("parallel", "parallel", "arbitrary")),
    )(*args)


NN = ((1,), (0,))
NT = ((1,), (1,))
TN = ((0,), (0,))


def _mm_nn(name, a, b, tm, tn, tk, out_dtype=F32, res=None):
    m, k = a.shape
    n = b.shape[1]
    return _matmul(name, a, b, (m // tm, n // tn, k // tk),
                   pl.BlockSpec((tm, tk), lambda i, j, kk: (i, kk)),
                   pl.BlockSpec((tk, tn), lambda i, j, kk: (kk, j)),
                   pl.BlockSpec((tm, tn), lambda i, j, kk: (i, j)),
                   jax.ShapeDtypeStruct((m, n), out_dtype), NN,
                   res=res, res_spec=pl.BlockSpec((tm, tn), lambda i, j, kk: (i, j)))


def _mm_nt(name, a, b, tm, tn, tk, out_dtype=F32, res=None):
    m, k = a.shape
    n = b.shape[0]
    return _matmul(name, a, b, (m // tm, n // tn, k // tk),
                   pl.BlockSpec((tm, tk), lambda i, j, kk: (i, kk)),
                   pl.BlockSpec((tn, tk), lambda i, j, kk: (j, kk)),
                   pl.BlockSpec((tm, tn), lambda i, j, kk: (i, j)),
                   jax.ShapeDtypeStruct((m, n), out_dtype), NT,
                   res=res, res_spec=pl.BlockSpec((tm, tn), lambda i, j, kk: (i, j)))


def _mm_tn(name, a, b, tm, tn, tk, out_dtype=F32):
    k, m = a.shape
    n = b.shape[1]
    tk = min(tk, k)
    return _matmul(name, a, b, (m // tm, n // tn, k // tk),
                   pl.BlockSpec((tk, tm), lambda i, j, kk: (kk, i)),
                   pl.BlockSpec((tk, tn), lambda i, j, kk: (kk, j)),
                   pl.BlockSpec((tm, tn), lambda i, j, kk: (i, j)),
                   jax.ShapeDtypeStruct((m, n), out_dtype), TN)


def _up_shard(n):
    return 2 * (n % 2) + (n // 2) // 2, (n // 2) % 2


def _mm_up_fwd(name, a, wg_up, tm, tk):
    m, k = a.shape
    _, _, cols = wg_up.shape
    tn = cols // 2
    return _matmul(name, a, wg_up, (m // tm, 2 * N_CHIPS, k // tk),
                   pl.BlockSpec((tm, tk), lambda i, j, kk: (i, kk)),
                   pl.BlockSpec((None, tk, tn), lambda i, j, kk: (_up_shard(j)[0], kk, _up_shard(j)[1])),
                   pl.BlockSpec((tm, tn), lambda i, j, kk: (i, j)),
                   jax.ShapeDtypeStruct((m, 2 * N_CHIPS * tn), F32), NN)


def _mm_up_bwd_x(name, dgu, wg_up, tm, tn):
    m, _ = dgu.shape
    _, d, cols = wg_up.shape
    tk = cols // 2
    return _matmul(name, dgu, wg_up, (m // tm, d // tn, 2 * N_CHIPS),
                   pl.BlockSpec((tm, tk), lambda i, j, kk: (i, kk)),
                   pl.BlockSpec((None, tn, tk), lambda i, j, kk: (_up_shard(kk)[0], j, _up_shard(kk)[1])),
                   pl.BlockSpec((tm, tn), lambda i, j, kk: (i, j)),
                   jax.ShapeDtypeStruct((m, d), F32), NT)


def _mm_up_bwd_w(name, n2, dgu, tm, tk):
    s, d = n2.shape
    tk = min(tk, s)
    tn = dgu.shape[1] // (2 * N_CHIPS)
    return _matmul(name, n2, dgu, (d // tm, 2 * N_CHIPS, s // tk),
                   pl.BlockSpec((tk, tm), lambda i, j, kk: (kk, i)),
                   pl.BlockSpec((tk, tn), lambda i, j, kk: (kk, j)),
                   pl.BlockSpec((None, tm, tn), lambda i, j, kk: (_up_shard(j)[0], i, _up_shard(j)[1])),
                   jax.ShapeDtypeStruct((N_CHIPS, d, 2 * tn), F32), TN)


def _rmsnorm_fwd(name, x, g, tm=256):
    s, d = x.shape

    def body(x_ref, g_ref, n_ref, r_ref):
        xf = x_ref[...]
        r = lax.rsqrt(jnp.mean(xf * xf, axis=-1, keepdims=True) + EPS)
        n_ref[...] = ((xf * r) * g_ref[...]).astype(BF16)
        r_ref[...] = r

    return pl.pallas_call(
        body, grid=(s // tm,), name=name,
        in_specs=[pl.BlockSpec((tm, d), lambda i: (i, 0)), pl.BlockSpec((1, d), lambda i: (0, 0))],
        out_specs=[pl.BlockSpec((tm, d), lambda i: (i, 0)), pl.BlockSpec((tm, 1), lambda i: (i, 0))],
        out_shape=[jax.ShapeDtypeStruct((s, d), BF16), jax.ShapeDtypeStruct((s, 1), F32)],
        compiler_params=_params(("parallel",)),
    )(x, g)


def _rmsnorm_bwd(name, x, rstd, g, dn, dres, tm=256):
    s, d = x.shape

    def body(x_ref, r_ref, g_ref, dn_ref, dres_ref, dx_ref, dxb_ref, dg_ref):
        @pl.when(pl.program_id(0) == 0)
        def _():
            dg_ref[...] = jnp.zeros_like(dg_ref)

        r = r_ref[...]
        xhat = x_ref[...] * r
        dn_v = dn_ref[...]
        dxhat = dn_v * g_ref[...]
        dx = dres_ref[...] + r * (dxhat - xhat * jnp.mean(dxhat * xhat, axis=-1, keepdims=True))
        dx_ref[...] = dx
        dxb_ref[...] = dx.astype(BF16)
        dg_ref[...] += jnp.sum(dn_v * xhat, axis=0, keepdims=True)

    row = pl.BlockSpec((tm, d), lambda i: (i, 0))
    vec = pl.BlockSpec((1, d), lambda i: (0, 0))
    return pl.pallas_call(
        body, grid=(s // tm,), name=name,
        in_specs=[row, pl.BlockSpec((tm, 1), lambda i: (i, 0)), vec, row, row],
        out_specs=[row, row, vec],
        out_shape=[jax.ShapeDtypeStruct((s, d), F32), jax.ShapeDtypeStruct((s, d), BF16),
                   jax.ShapeDtypeStruct((1, d), F32)],
        compiler_params=_params(("arbitrary",)),
    )(x, rstd, g, dn, dres)


def _loss_head(name, x, g, target, tm=256):
    s, d = x.shape

    def body(x_ref, g_ref, t_ref, loss_ref, dx_ref, dxb_ref, dg_ref):
        @pl.when(pl.program_id(0) == 0)
        def _():
            dg_ref[...] = jnp.zeros_like(dg_ref)
            loss_ref[...] = jnp.zeros_like(loss_ref)

        xf = x_ref[...]
        gv = g_ref[...]
        r = lax.rsqrt(jnp.mean(xf * xf, axis=-1, keepdims=True) + EPS)
        xhat = xf * r
        err = xhat * gv - t_ref[...]
        loss_ref[...] += 0.5 * jnp.sum(jnp.mean(err * err, axis=-1, keepdims=True), axis=0, keepdims=True)
        dy = err * (1.0 / d)
        dxhat = dy * gv
        dx = r * (dxhat - xhat * jnp.mean(dxhat * xhat, axis=-1, keepdims=True))
        dx_ref[...] = dx
        dxb_ref[...] = dx.astype(BF16)
        dg_ref[...] += jnp.sum(dy * xhat, axis=0, keepdims=True)

    row = pl.BlockSpec((tm, d), lambda i: (i, 0))
    vec = pl.BlockSpec((1, d), lambda i: (0, 0))
    return pl.pallas_call(
        body, grid=(s // tm,), name=name,
        in_specs=[row, vec, row],
        out_specs=[pl.BlockSpec((1, 1), lambda i: (0, 0)), row, row, vec],
        out_shape=[jax.ShapeDtypeStruct((1, 1), F32), jax.ShapeDtypeStruct((s, d), F32),
                   jax.ShapeDtypeStruct((s, d), BF16), jax.ShapeDtypeStruct((1, d), F32)],
        compiler_params=_params(("arbitrary",)),
    )(x, g, target)


def _sigmoid(v):
    return 1.0 / (1.0 + jnp.exp(-v))


def _log_sigmoid(v):
    return jnp.minimum(v, 0.0) - jnp.log1p(jnp.exp(-jnp.abs(v)))


def _softplus(v):
    return jnp.maximum(v, 0.0) + jnp.log1p(jnp.exp(-jnp.abs(v)))


def _one_minus_exp(z):
    series = -z * (1.0 + z * (0.5 + z * (1.0 / 6.0 + z * (1.0 / 24.0 + z * (1.0 / 120.0)))))
    return jnp.where(z > -0.1, series, 1.0 - jnp.exp(z))


_GELU_K = 0.7978845608028654
_GELU_C = 0.044715


def _gelu(v):
    return 0.5 * v * (1.0 + jnp.tanh(_GELU_K * (v + _GELU_C * v * v * v)))


def _gelu_grad(v):
    t = jnp.tanh(_GELU_K * (v + _GELU_C * v * v * v))
    return 0.5 * (1.0 + t) + 0.5 * v * (1.0 - t * t) * _GELU_K * (1.0 + 3.0 * _GELU_C * v * v)


def _rows(shape):
    return lax.broadcasted_iota(jnp.int32, shape, 0)


def _cols(shape):
    return lax.broadcasted_iota(jnp.int32, shape, 1)


def _shift_down(v, prev, d):
    if d == 0:
        return v
    rolled = pltpu.roll(v, d, axis=0)
    head = jnp.where(_rows((HALO, v.shape[1])) >= d, rolled[:HALO], pltpu.roll(prev, d, axis=0))
    if v.shape[0] == HALO:
        return head
    return jnp.concatenate([head, rolled[HALO:]], axis=0)


def _shift_up(v, nxt, d):
    if d == 0:
        return v
    n = v.shape[0]
    rolled = pltpu.roll(v, n - d, axis=0)
    tail = jnp.where(_rows((HALO, v.shape[1])) < HALO - d, rolled[n - HALO:], pltpu.roll(nxt, HALO - d, axis=0))
    if n == HALO:
        return tail
    return jnp.concatenate([rolled[:n - HALO], tail], axis=0)


def _dot(a, b, contract):
    return lax.dot_general(a.astype(BF16), b.astype(BF16), (contract, ((), ())), preferred_element_type=F32)


def _mlstm_chunk_common(h, q_ref, k_ref, v_ref, gcol_ref, grow_ref, brow_ref, bcol_ref, m_prev):
    L = CHUNK
    sl = slice(h * HEAD_DIM, (h + 1) * HEAD_DIM)
    qh = q_ref[:, sl]
    kh = k_ref[:, sl]
    vh = v_ref[:, sl]
    qs = qh * (HEAD_DIM ** -0.5)
    gates = gcol_ref[...] + brow_ref[...]
    lane = _cols(gates.shape)
    ic = jnp.sum(jnp.where(lane == h, gates, 0.0), axis=1, keepdims=True)
    fc = jnp.sum(jnp.where(lane == HEADS + h, gates, 0.0), axis=1, keepdims=True)
    ir = grow_ref[h:h + 1, :] + bcol_ref[h:h + 1, :]
    fr = grow_ref[HEADS + h:HEADS + h + 1, :] + bcol_ref[HEADS + h:HEADS + h + 1, :]
    logf_c = _log_sigmoid(fc)
    logf_r = _log_sigmoid(fr)
    t_i = _rows((L, L))
    s_i = _cols((L, L))
    tri = t_i >= s_i
    b_c = jnp.sum(jnp.where(tri, logf_r, 0.0), axis=1, keepdims=True)
    b_r = jnp.sum(jnp.where(t_i <= s_i, logf_c, 0.0), axis=0, keepdims=True)
    btot = jnp.sum(logf_r, axis=1, keepdims=True)
    dmat = jnp.where(tri, b_c - b_r + ir, -jnp.inf)
    m_inter = b_c + m_prev
    m_t = jnp.maximum(m_inter, jnp.max(dmat, axis=1, keepdims=True))
    e_mat = jnp.exp(dmat - m_t)
    e_inter = jnp.exp(m_inter - m_t)
    wqk = _dot(qs, kh, NT) * e_mat
    w_end_r = btot - b_r + ir
    m_loc = jnp.max(w_end_r, axis=1, keepdims=True)
    e_end_c = jnp.exp(btot - b_c + ic - m_loc)
    m_new = jnp.maximum(btot + m_prev, m_loc)
    a_dec = jnp.exp(btot + m_prev - m_new)
    c_inj = jnp.exp(m_loc - m_new)
    return dict(qh=qh, kh=kh, vh=vh, qs=qs, fc=fc, tri=tri, t_i=t_i, s_i=s_i, m_t=m_t, e_mat=e_mat,
                e_inter=e_inter, wqk=wqk, e_end_c=e_end_c, m_new=m_new, a_dec=a_dec, c_inj=c_inj)


def _mlstm_fwd(proj, gates_t, bias_row, bias_col, head_g):
    s = proj.shape[0]
    nc = s // CHUNK
    L = CHUNK

    def body(q_ref, k_ref, v_ref, o_ref, gcol_ref, grow_ref, brow_ref, bcol_ref, hg_ref,
             out_ref, cprev_ref, nprev_ref, mprev_ref, c_scr, n_scr, m_scr):
        @pl.when(pl.program_id(0) == 0)
        def _():
            c_scr[...] = jnp.zeros_like(c_scr)
            n_scr[...] = jnp.zeros_like(n_scr)
            m_scr[...] = jnp.zeros_like(m_scr)

        for h in range(HEADS):
            sl = slice(h * HEAD_DIM, (h + 1) * HEAD_DIM)
            m_prev = m_scr[h:h + 1, 0:1]
            n_prev = n_scr[h:h + 1, :]
            c_prev = c_scr[h].astype(BF16)
            q = _mlstm_chunk_common(h, q_ref, k_ref, v_ref, gcol_ref, grow_ref, brow_ref, bcol_ref, m_prev)
            num = _dot(q["wqk"], q["vh"], NN) + q["e_inter"] * _dot(q["qs"], c_prev, NN)
            den = (jnp.sum(q["wqk"], axis=1, keepdims=True)
                   + q["e_inter"] * jnp.sum(q["qs"] * n_prev, axis=1, keepdims=True))
            hh = num / jnp.maximum(jnp.abs(den), jnp.exp(-q["m_t"]))
            hn = hh * lax.rsqrt(jnp.mean(hh * hh, axis=1, keepdims=True) + EPS) * hg_ref[h:h + 1, :]
            out_ref[:, sl] = (_sigmoid(o_ref[:, sl]) * hn).astype(BF16)
            cprev_ref[h] = c_prev
            nprev_ref[h:h + 1, :] = n_prev
            mprev_ref[h:h + 1, :] = jnp.broadcast_to(m_prev, (1, LANES))
            c_loc = _dot(q["kh"], q["e_end_c"] * q["vh"], TN)
            n_loc = jnp.sum(q["e_end_c"] * q["kh"], axis=0, keepdims=True)
            c_scr[h] = q["a_dec"] * c_scr[h] + q["c_inj"] * c_loc
            n_scr[h:h + 1, :] = q["a_dec"] * n_prev + q["c_inj"] * n_loc
            m_scr[h:h + 1, :] = jnp.broadcast_to(q["m_new"], (1, LANES))

    blk = lambda j: pl.BlockSpec((L, D_MLSTM), lambda c, j=j: (c, j))
    full = lambda shp: pl.BlockSpec(shp, lambda c: tuple(0 for _ in shp))
    return pl.pallas_call(
        body, grid=(nc,), name="mlstm_fwd",
        in_specs=[blk(0), blk(1), blk(2), blk(3),
                  pl.BlockSpec((L, LANES), lambda c: (c, (4 * D_MLSTM + 2 * D_LRU) // LANES)),
                  pl.BlockSpec((None, 2 * HEADS, L), lambda c: (c, 0, 0)),
                  full((1, LANES)), full((2 * HEADS, 1)), full((HEADS, HEAD_DIM))],
        out_specs=[pl.BlockSpec((L, D_MLSTM), lambda c: (c, 0)),
                   pl.BlockSpec((None, HEADS, HEAD_DIM, HEAD_DIM), lambda c: (c, 0, 0, 0)),
                   pl.BlockSpec((None, HEADS, HEAD_DIM), lambda c: (c, 0, 0)),
                   pl.BlockSpec((None, HEADS, LANES), lambda c: (c, 0, 0))],
        out_shape=[jax.ShapeDtypeStruct((s, D_MLSTM + D_LRU), BF16),
                   jax.ShapeDtypeStruct((nc, HEADS, HEAD_DIM, HEAD_DIM), BF16),
                   jax.ShapeDtypeStruct((nc, HEADS, HEAD_DIM), F32),
                   jax.ShapeDtypeStruct((nc, HEADS, LANES), F32)],
        scratch_shapes=[pltpu.VMEM((HEADS, HEAD_DIM, HEAD_DIM), F32), pltpu.VMEM((HEADS, HEAD_DIM), F32),
                        pltpu.VMEM((HEADS, LANES), F32)],
        compiler_params=_params(("arbitrary",)),
    )(proj, proj, proj, proj, proj, gates_t, bias_row, bias_col, head_g)


def _mlstm_bwd(proj, gates_t, bias_row, bias_col, head_g, cprev, nprev, mprev, dmix):
    s = proj.shape[0]
    nc = s // CHUNK
    L = CHUNK

    def body(q_ref, k_ref, v_ref, o_ref, gcol_ref, grow_ref, brow_ref, bcol_ref, hg_ref,
             cprev_ref, nprev_ref, mprev_ref, dmix_ref,
             dqkvo_ref, dgate_ref, dhg_ref, g_scr, gn_scr):
        @pl.when(pl.program_id(0) == 0)
        def _():
            g_scr[...] = jnp.zeros_like(g_scr)
            gn_scr[...] = jnp.zeros_like(gn_scr)
            dhg_ref[...] = jnp.zeros_like(dhg_ref)

        lane = _cols((L, LANES))
        dgate = jnp.zeros((L, LANES), F32)
        for h in range(HEADS):
            sl = slice(h * HEAD_DIM, (h + 1) * HEAD_DIM)
            m_prev = mprev_ref[h:h + 1, 0:1]
            n_prev = nprev_ref[h:h + 1, :]
            c_prev = cprev_ref[h]
            q = _mlstm_chunk_common(h, q_ref, k_ref, v_ref, gcol_ref, grow_ref, brow_ref, bcol_ref, m_prev)
            qh, kh, vh, qs, wqk, e_inter = q["qh"], q["kh"], q["vh"], q["qs"], q["wqk"], q["e_inter"]
            num_state = e_inter * _dot(qs, c_prev, NN)
            den_state = e_inter * jnp.sum(qs * n_prev, axis=1, keepdims=True)
            num = _dot(wqk, vh, NN) + num_state
            den = jnp.sum(wqk, axis=1, keepdims=True) + den_state
            floor = jnp.exp(-q["m_t"])
            denom = jnp.maximum(jnp.abs(den), floor)
            hh = num / denom
            rn = lax.rsqrt(jnp.mean(hh * hh, axis=1, keepdims=True) + EPS)
            hn_pre = hh * rn
            hg = hg_ref[h:h + 1, :]
            sg = _sigmoid(o_ref[:, sl])
            dout = dmix_ref[:, sl]
            d_o = dout * (hn_pre * hg) * sg * (1.0 - sg)
            dhn = dout * sg
            dhg_ref[h:h + 1, :] += jnp.sum(dhn * hn_pre, axis=0, keepdims=True)
            dhn_pre = dhn * hg
            dhh = rn * (dhn_pre - hn_pre * jnp.mean(dhn_pre * hn_pre, axis=1, keepdims=True))
            dnum = dhh / denom
            dden = jnp.where(jnp.abs(den) >= floor,
                             -jnp.sum(hh * dhh, axis=1, keepdims=True) / denom * jnp.sign(den), 0.0)
            dwqk = _dot(dnum, vh, NT) + dden
            dv = _dot(wqk, dnum, TN)
            dp = dwqk * q["e_mat"]
            dqs = _dot(dp, kh, NN) + e_inter * (_dot(dnum, c_prev, NT) + dden * n_prev)
            dk = _dot(dp, qs, TN)
            g_next = g_scr[h]
            gn_next = gn_scr[h:h + 1, :]
            w_state = q["e_end_c"] * q["c_inj"]
            dk_state = w_state * (_dot(vh, g_next, NT) + gn_next)
            dk = dk + dk_state
            dv = dv + w_state * _dot(kh, g_next, NN)
            dq = dqs * (HEAD_DIM ** -0.5)
            eye = q["t_i"] == q["s_i"]
            to_row = lambda col: jnp.sum(jnp.where(eye, col, 0.0), axis=0, keepdims=True)
            to_col = lambda row: jnp.sum(jnp.where(eye, row, 0.0), axis=1, keepdims=True)
            g_pair = dwqk * wqk
            rs_in = jnp.sum(g_pair, axis=1, keepdims=True)
            cs_in_r = jnp.sum(g_pair, axis=0, keepdims=True)
            rs_state = (jnp.sum(dnum * num_state, axis=1, keepdims=True) + dden * den_state)
            cs_state = jnp.sum(kh * dk_state, axis=1, keepdims=True)
            di_c = to_col(cs_in_r) + cs_state
            through = q["a_dec"] * (jnp.sum(jnp.sum(g_next * c_prev.astype(F32), axis=1, keepdims=True),
                                            axis=0, keepdims=True)
                                    + jnp.sum(gn_next * n_prev, axis=1, keepdims=True))
            ends_here = to_row(rs_in + rs_state) - cs_in_r
            da_c = (jnp.sum(jnp.where(q["s_i"] >= q["t_i"], ends_here, 0.0), axis=1, keepdims=True)
                    + jnp.sum(jnp.where(q["s_i"] < q["t_i"], to_row(cs_state), 0.0), axis=1, keepdims=True)
                    + through)
            df_c = da_c * _sigmoid(-q["fc"])
            dgate = dgate + jnp.where(lane == h, di_c, 0.0) + jnp.where(lane == HEADS + h, df_c, 0.0)
            dqkvo_ref[:, sl] = dq.astype(BF16)
            dqkvo_ref[:, D_MLSTM + h * HEAD_DIM:D_MLSTM + (h + 1) * HEAD_DIM] = dk.astype(BF16)
            dqkvo_ref[:, 2 * D_MLSTM + h * HEAD_DIM:2 * D_MLSTM + (h + 1) * HEAD_DIM] = dv.astype(BF16)
            dqkvo_ref[:, 3 * D_MLSTM + h * HEAD_DIM:3 * D_MLSTM + (h + 1) * HEAD_DIM] = d_o.astype(BF16)
            g_scr[h] = q["a_dec"] * g_next + _dot(e_inter * qs, dnum, TN)
            gn_scr[h:h + 1, :] = q["a_dec"] * gn_next + jnp.sum(e_inter * qs * dden, axis=0, keepdims=True)
        dgate_ref[...] = dgate

    rev = lambda c: nc - 1 - c
    blk = lambda j: pl.BlockSpec((L, D_MLSTM), lambda c, j=j: (rev(c), j))
    full = lambda shp: pl.BlockSpec(shp, lambda c: tuple(0 for _ in shp))
    return pl.pallas_call(
        body, grid=(nc,), name="mlstm_bwd",
        in_specs=[blk(0), blk(1), blk(2), blk(3),
                  pl.BlockSpec((L, LANES), lambda c: (rev(c), (4 * D_MLSTM + 2 * D_LRU) // LANES)),
                  pl.BlockSpec((None, 2 * HEADS, L), lambda c: (rev(c), 0, 0)),
                  full((1, LANES)), full((2 * HEADS, 1)), full((HEADS, HEAD_DIM)),
                  pl.BlockSpec((None, HEADS, HEAD_DIM, HEAD_DIM), lambda c: (rev(c), 0, 0, 0)),
                  pl.BlockSpec((None, HEADS, HEAD_DIM), lambda c: (rev(c), 0, 0)),
                  pl.BlockSpec((None, HEADS, LANES), lambda c: (rev(c), 0, 0)),
                  pl.BlockSpec((L, D_MLSTM), lambda c: (rev(c), 0))],
        out_specs=[pl.BlockSpec((L, 4 * D_MLSTM), lambda c: (rev(c), 0)),
                   pl.BlockSpec((L, LANES), lambda c: (rev(c), 0)),
                   full((HEADS, HEAD_DIM))],
        out_shape=[jax.ShapeDtypeStruct((s, _PROJ_PAD), BF16),
                   jax.ShapeDtypeStruct((s, LANES), F32),
                   jax.ShapeDtypeStruct((HEADS, HEAD_DIM), F32)],
        scratch_shapes=[pltpu.VMEM((HEADS, HEAD_DIM, HEAD_DIM), F32), pltpu.VMEM((HEADS, HEAD_DIM), F32)],
        compiler_params=_params(("arbitrary",)),
    )(proj, proj, proj, proj, proj, gates_t, bias_row, bias_col, head_g, cprev, nprev, mprev, dmix)


def _lru_gates(xc, wa_ref, wx_ref, ba, bx, lam):
    r = _sigmoid(_dot(xc, wa_ref[...], NN) + ba)
    ig = _sigmoid(_dot(xc, wx_ref[...], NN) + bx)
    sp = _softplus(-lam)
    log_a = (-LRU_C * r) * sp
    a = jnp.exp(log_a)
    mult = jnp.sqrt(_one_minus_exp(2.0 * log_a))
    return r, ig, sp, a, mult


def _lru_conv(xr, prev, w_ref, b):
    xc = b + _shift_down(xr, prev, 3) * w_ref[0:1, :]
    for j in range(1, LRU_CONV):
        xc = xc + _shift_down(xr, prev, LRU_CONV - 1 - j) * w_ref[j:j + 1, :]
    return xc


def _lru_fwd(proj, mix, conv_w, conv_b, wa, wx, ba, bx, lam, tt=512):
    s = proj.shape[0]
    tt = min(tt, s)
    nt = s // tt
    B = LRU_BLOCK_DIM
    lru_col = 4 * D_MLSTM // B
    mix_col = D_MLSTM // B

    def body(xr_ref, gr_ref, cw_ref, cb_ref, wa_ref, wx_ref, ba_ref, bx_ref, lam_ref, mix_in_ref,
             out_ref, h_ref, prev_scr, hcar_scr):
        @pl.when(pl.program_id(1) == 0)
        def _():
            prev_scr[...] = jnp.zeros_like(prev_scr)
            hcar_scr[...] = jnp.zeros_like(hcar_scr)

        xr = xr_ref[...]
        xc = _lru_conv(xr, prev_scr[...], cw_ref, cb_ref[...])
        prev_scr[...] = xr[tt - HALO:, :]
        _, ig, _, a, mult = _lru_gates(xc, wa_ref, wx_ref, ba_ref[...], bx_ref[...], lam_ref[...])
        u = mult * (ig * xc)
        rows = _rows((tt, B))
        acc_a, acc_b = a, u
        d = 1
        while d < tt:
            keep = rows >= d
            sh_a = jnp.where(keep, pltpu.roll(acc_a, d, axis=0), 1.0)
            sh_b = jnp.where(keep, pltpu.roll(acc_b, d, axis=0), 0.0)
            acc_b = acc_a * sh_b + acc_b
            acc_a = acc_a * sh_a
            d *= 2
        hv = acc_b + acc_a * hcar_scr[0:1, :]
        hcar_scr[...] = jnp.broadcast_to(hv[tt - 1:tt, :], hcar_scr.shape)
        h_ref[...] = hv
        out_ref[...] = (hv * _gelu(gr_ref[...])).astype(BF16)

    chan = lambda rws: pl.BlockSpec((rws, B), lambda n, i: (0, n))
    return pl.pallas_call(
        body, grid=(LRU_BLOCKS, nt), name="lru_fwd",
        in_specs=[pl.BlockSpec((tt, B), lambda n, i: (i, lru_col + 2 * n)),
                  pl.BlockSpec((tt, B), lambda n, i: (i, lru_col + 2 * n + 1)),
                  chan(LRU_CONV), chan(1),
                  pl.BlockSpec((None, B, B), lambda n, i: (n, 0, 0)),
                  pl.BlockSpec((None, B, B), lambda n, i: (n, 0, 0)),
                  chan(1), chan(1), chan(1), ANY],
        out_specs=[pl.BlockSpec((tt, B), lambda n, i: (i, mix_col + n)), pl.BlockSpec((tt, B), lambda n, i: (i, n))],
        out_shape=[jax.ShapeDtypeStruct(mix.shape, BF16), jax.ShapeDtypeStruct((s, D_LRU), F32)],
        scratch_shapes=[pltpu.VMEM((HALO, B), F32), pltpu.VMEM((HALO, B), F32)],
        input_output_aliases={9: 0},
        compiler_params=_params(("parallel", "arbitrary")),
    )(proj, proj, conv_w, conv_b, wa, wx, ba, bx, lam, mix)


def _lru_bwd(proj, hsave, dmix, dproj, conv_w, conv_b, wa, wx, ba, bx, lam, tt=512):
    s = proj.shape[0]
    tt = min(tt, s)
    nt = s // tt
    B = LRU_BLOCK_DIM
    lru_col = 4 * D_MLSTM // B
    dmix_col = D_MLSTM // B
    hpb = tt // HALO

    def body(xr_ref, xprev_ref, gr_ref, h_ref, hprev_ref, dmix_ref, cw_ref, cb_ref, wa_ref, wx_ref,
             ba_ref, bx_ref, lam_ref, dproj_in_ref,
             dxg_ref, dcw_ref, dcb_ref, dwa_ref, dwx_ref, dba_ref, dbx_ref, dlam_ref,
             gcar_scr, acar_scr, dxc_scr):
        i = pl.program_id(1)
        first_tile = i == nt - 1

        @pl.when(i == 0)
        def _():
            gcar_scr[...] = jnp.zeros_like(gcar_scr)
            acar_scr[...] = jnp.zeros_like(acar_scr)
            dxc_scr[...] = jnp.zeros_like(dxc_scr)
            for ref in (dcw_ref, dcb_ref, dwa_ref, dwx_ref, dba_ref, dbx_ref, dlam_ref):
                ref[...] = jnp.zeros_like(ref)

        xr = xr_ref[...]
        xprev = jnp.where(first_tile, 0.0, xprev_ref[...])
        hprev = jnp.where(first_tile, 0.0, hprev_ref[...])
        lam = lam_ref[...]
        xc = _lru_conv(xr, xprev, cw_ref, cb_ref[...])
        r, ig, sp, a, mult = _lru_gates(xc, wa_ref, wx_ref, ba_ref[...], bx_ref[...], lam)
        gr = gr_ref[...]
        hv = h_ref[...]
        dout = dmix_ref[...]
        dxg_ref[:, B:] = (dout * hv * _gelu_grad(gr)).astype(BF16)
        dh = dout * _gelu(gr)
        rows = _rows((tt, B))
        acc_a = _shift_up(a, acar_scr[...], 1)
        acc_b = dh
        d = 1
        while d < tt:
            keep = rows < tt - d
            sh_a = jnp.where(keep, pltpu.roll(acc_a, tt - d, axis=0), 1.0)
            sh_b = jnp.where(keep, pltpu.roll(acc_b, tt - d, axis=0), 0.0)
            acc_b = acc_a * sh_b + acc_b
            acc_a = acc_a * sh_a
            d *= 2
        gv = acc_b + acc_a * gcar_scr[0:1, :]
        gcar_scr[...] = jnp.broadcast_to(gv[0:1, :], gcar_scr.shape)
        acar_scr[...] = jnp.broadcast_to(a[0:1, :], acar_scr.shape)
        h_before = _shift_down(hv, hprev, 1)
        da = gv * h_before
        dmult = gv * (ig * xc)
        dig = gv * mult * xc
        dxc = gv * mult * ig
        dlog_a = da * a - dmult * (a * a) / mult
        dr = dlog_a * (-LRU_C * sp)
        dlam_ref[...] += jnp.sum(dlog_a * (-LRU_C * r), axis=0, keepdims=True) * (-_sigmoid(-lam))
        dpre_r = dr * r * (1.0 - r)
        dpre_i = dig * ig * (1.0 - ig)
        dba_ref[...] += jnp.sum(dpre_r, axis=0, keepdims=True)
        dbx_ref[...] += jnp.sum(dpre_i, axis=0, keepdims=True)
        dwa_ref[...] += _dot(xc, dpre_r, TN)
        dwx_ref[...] += _dot(xc, dpre_i, TN)
        dxc = dxc + _dot(dpre_r, wa_ref[...], NT) + _dot(dpre_i, wx_ref[...], NT)
        dcb_ref[...] += jnp.sum(dxc, axis=0, keepdims=True)
        nxt = dxc_scr[...]
        dxr = jnp.zeros((tt, B), F32)
        for j in range(LRU_CONV):
            sft = LRU_CONV - 1 - j
            dcw_ref[j:j + 1, :] += jnp.sum(dxc * _shift_down(xr, xprev, sft), axis=0, keepdims=True)
            dxr = dxr + _shift_up(dxc, nxt, sft) * cw_ref[j:j + 1, :]
        dxc_scr[...] = dxc[:HALO, :]
        dxg_ref[:, :B] = dxr.astype(BF16)

    rev = lambda i: nt - 1 - i
    tile = lambda col, step: pl.BlockSpec((tt, B), lambda n, i: (rev(i), col + step * n))
    halo = lambda col, step: pl.BlockSpec(
        (HALO, B), lambda n, i: (jnp.maximum(rev(i) * hpb - 1, 0), col + step * n))
    chan = lambda rws: pl.BlockSpec((rws, B), lambda n, i: (0, n))
    wblk = pl.BlockSpec((None, B, B), lambda n, i: (n, 0, 0))
    return pl.pallas_call(
        body, grid=(LRU_BLOCKS, nt), name="lru_bwd",
        in_specs=[tile(lru_col, 2), halo(lru_col, 2), tile(lru_col + 1, 2), tile(0, 1), halo(0, 1),
                  tile(dmix_col, 1), chan(LRU_CONV), chan(1), wblk, wblk, chan(1), chan(1), chan(1), ANY],
        out_specs=[pl.BlockSpec((tt, 2 * B), lambda n, i: (rev(i), lru_col // 2 + n)),
                   chan(LRU_CONV), chan(1), wblk, wblk, chan(1), chan(1), chan(1)],
        out_shape=[jax.ShapeDtypeStruct(dproj.shape, BF16),
                   jax.ShapeDtypeStruct((LRU_CONV, D_LRU), F32), jax.ShapeDtypeStruct((1, D_LRU), F32),
                   jax.ShapeDtypeStruct((LRU_BLOCKS, B, B), F32), jax.ShapeDtypeStruct((LRU_BLOCKS, B, B), F32),
                   jax.ShapeDtypeStruct((1, D_LRU), F32), jax.ShapeDtypeStruct((1, D_LRU), F32),
                   jax.ShapeDtypeStruct((1, D_LRU), F32)],
        scratch_shapes=[pltpu.VMEM((HALO, B), F32), pltpu.VMEM((HALO, B), F32), pltpu.VMEM((HALO, B), F32)],
        input_output_aliases={13: 0},
        compiler_params=_params(("parallel", "arbitrary")),
    )(proj, proj, proj, hsave, hsave, dmix, conv_w, conv_b, wa, wx, ba, bx, lam, dproj)


def _ffn_conv(gp, prev, w_ref, b):
    g = b + _shift_down(gp, prev, 2) * w_ref[0:1, :]
    for j in range(1, FFN_CONV):
        g = g + _shift_down(gp, prev, FFN_CONV - 1 - j) * w_ref[j:j + 1, :]
    return g


def _ffn_act_fwd(gu, conv_w, conv_b, tt=256):
    s = gu.shape[0]
    tt = min(tt, s)
    d_ff = conv_w.shape[1]
    tc = d_ff // N_CHIPS
    hpb = tt // HALO

    def body(g_ref, gprev_ref, u_ref, w_ref, b_ref, act_ref):
        prev = jnp.where(pl.program_id(0) == 0, 0.0, gprev_ref[...])
        gate = _ffn_conv(g_ref[...], prev, w_ref, b_ref[...])
        act_ref[...] = (gate * _sigmoid(gate) * u_ref[...]).astype(BF16)

    return pl.pallas_call(
        body, grid=(s // tt, N_CHIPS), name="ffn_act_fwd",
        in_specs=[pl.BlockSpec((tt, tc), lambda i, j: (i, 2 * j)),
                  pl.BlockSpec((HALO, tc), lambda i, j: (jnp.maximum(i * hpb - 1, 0), 2 * j)),
                  pl.BlockSpec((tt, tc), lambda i, j: (i, 2 * j + 1)),
                  pl.BlockSpec((FFN_CONV, tc), lambda i, j: (0, j)),
                  pl.BlockSpec((1, tc), lambda i, j: (0, j))],
        out_specs=pl.BlockSpec((tt, tc), lambda i, j: (i, j)),
        out_shape=jax.ShapeDtypeStruct((s, d_ff), BF16),
        compiler_params=_params(("parallel", "parallel")),
    )(gu, gu, gu, conv_w, conv_b)


def _ffn_act_bwd(gu, dact, conv_w, conv_b, tt=256):
    s = gu.shape[0]
    tt = min(tt, s)
    nt = s // tt
    d_ff = conv_w.shape[1]
    tc = d_ff // N_CHIPS
    hpb = tt // HALO

    def dgate_of(gate, up, da):
        sg = _sigmoid(gate)
        return da * up * (sg * (1.0 + gate * (1.0 - sg))), da * (gate * sg)

    def body(g_ref, gprev_ref, gnext_ref, u_ref, unext_ref, da_ref, danext_ref, w_ref, b_ref,
             dgu_ref, dw_ref, db_ref):
        i = pl.program_id(1)

        @pl.when(i == 0)
        def _():
            dw_ref[...] = jnp.zeros_like(dw_ref)
            db_ref[...] = jnp.zeros_like(db_ref)

        gp = g_ref[...]
        prev = jnp.where(i == 0, 0.0, gprev_ref[...])
        bias = b_ref[...]
        gate = _ffn_conv(gp, prev, w_ref, bias)
        dgate, dup = dgate_of(gate, u_ref[...], da_ref[...])
        gate_n = _ffn_conv(gnext_ref[...], gp[tt - HALO:, :], w_ref, bias)
        dgate_n, _ = dgate_of(gate_n, unext_ref[...], danext_ref[...])
        dgate_n = jnp.where(i == nt - 1, 0.0, dgate_n)
        db_ref[...] += jnp.sum(dgate, axis=0, keepdims=True)
        dgp = jnp.zeros((tt, tc), F32)
        for j in range(FFN_CONV):
            sft = FFN_CONV - 1 - j
            dw_ref[j:j + 1, :] += jnp.sum(dgate * _shift_down(gp, prev, sft), axis=0, keepdims=True)
            dgp = dgp + _shift_up(dgate, dgate_n, sft) * w_ref[j:j + 1, :]
        dgu_ref[:, :tc] = dgp.astype(BF16)
        dgu_ref[:, tc:] = dup.astype(BF16)

    tile = lambda half: pl.BlockSpec((tt, tc), lambda j, i, half=half: (i, 2 * j + half))
    hprev = lambda half: pl.BlockSpec((HALO, tc), lambda j, i, half=half: (jnp.maximum(i * hpb - 1, 0), 2 * j + half))
    hnext = lambda half: pl.BlockSpec(
        (HALO, tc), lambda j, i, half=half: (jnp.minimum((i + 1) * hpb, nt * hpb - 1), 2 * j + half))
    return pl.pallas_call(
        body, grid=(N_CHIPS, nt), name="ffn_act_bwd",
        in_specs=[tile(0), hprev(0), hnext(0), tile(1), hnext(1),
                  pl.BlockSpec((tt, tc), lambda j, i: (i, j)),
                  pl.BlockSpec((HALO, tc), lambda j, i: (jnp.minimum((i + 1) * hpb, nt * hpb - 1), j)),
                  pl.BlockSpec((FFN_CONV, tc), lambda j, i: (0, j)),
                  pl.BlockSpec((1, tc), lambda j, i: (0, j))],
        out_specs=[pl.BlockSpec((tt, 2 * tc), lambda j, i: (i, j)),
                   pl.BlockSpec((FFN_CONV, tc), lambda j, i: (0, j)),
                   pl.BlockSpec((1, tc), lambda j, i: (0, j))],
        out_shape=[jax.ShapeDtypeStruct((s, 2 * d_ff), BF16),
                   jax.ShapeDtypeStruct((FFN_CONV, d_ff), F32), jax.ShapeDtypeStruct((1, d_ff), F32)],
        compiler_params=_params(("parallel", "arbitrary")),
    )(gu, gu, gu, gu, gu, dact, dact, conv_w, conv_b)


def _gate_grads(dgate, dproj, tm=512):
    s, n = dgate.shape
    tm = min(tm, s)

    def body(a_ref, dproj_in_ref, o_ref, dproj_ref):
        @pl.when(pl.program_id(0) == 0)
        def _():
            o_ref[...] = jnp.zeros_like(o_ref)
        a = a_ref[...]
        o_ref[...] += jnp.sum(a, axis=0, keepdims=True)
        dproj_ref[...] = a.astype(BF16)

    return pl.pallas_call(
        body, grid=(s // tm,), name="gate_grads",
        in_specs=[pl.BlockSpec((tm, n), lambda i: (i, 0)), ANY],
        out_specs=[pl.BlockSpec((1, n), lambda i: (0, 0)),
                   pl.BlockSpec((tm, n), lambda i: (i, (_QKVO + 2 * D_LRU) // LANES))],
        out_shape=[jax.ShapeDtypeStruct((1, n), F32), jax.ShapeDtypeStruct(dproj.shape, BF16)],
        input_output_aliases={1: 1},
        compiler_params=_params(("arbitrary",)),
    )(dgate, dproj)


def _pick(n, *cands):
    for c in cands:
        if n % c == 0:
            return c
    raise ValueError(f"no tile for {n}")


def _local_step(x, target, w):
    s, d = x.shape
    nc = s // CHUNK
    tm = _pick(s, 1024, 512, 256)
    n_proj = w["w_in"].shape[1]
    tn_proj = _pick(n_proj, 896)
    d_ff = w["w_down"].shape[0]
    gate_col = 4 * D_MLSTM + 2 * D_LRU

    n1, rstd1 = _rmsnorm_fwd("norm_mix_fwd", x, w["norm_mix_g"])
    proj = _mm_nn("proj_fwd", n1, w["w_in"], tm, tn_proj, 512)
    gates = proj[:, gate_col:gate_col + 2 * HEADS]
    gates_t = gates.reshape(nc, CHUNK, 2 * HEADS).transpose(0, 2, 1)
    bias_row = jnp.pad(w["b_gate_m"], ((0, 0), (0, LANES - 2 * HEADS)))
    bias_col = w["b_gate_m"].reshape(2 * HEADS, 1)
    mix, cprev, nprev, mprev = _mlstm_fwd(proj, gates_t, bias_row, bias_col, w["mlstm_norm_g"])
    mix, hsave = _lru_fwd(proj, mix, w["lru_conv_w"], w["lru_conv_b"], w["lru_wa"], w["lru_wx"],
                          w["lru_ba"], w["lru_bx"], w["lru_lambda"])
    x1 = _mm_nn("out_fwd", mix, w["w_out"], tm, 1024, 512, res=x)
    n2, rstd2 = _rmsnorm_fwd("norm_ffn_fwd", x1, w["norm_ffn_g"])
    gu = _mm_up_fwd("up_fwd", n2, w["w_up"], tm, 512)
    act = _ffn_act_fwd(gu, w["ffn_conv_w"], w["ffn_conv_b"])
    x2 = _mm_nn("down_fwd", act, w["w_down"], tm, 1024, _pick(d_ff, 512), res=x1)
    loss, dx2, dx2b, g_norm_final = _loss_head("loss_head", x2, w["norm_final_g"], target)

    grads = {"norm_final_g": g_norm_final}
    dact = _mm_nt("down_bwd_x", dx2b, w["w_down"], tm, _pick(d_ff, 1408, 512), 1024)
    grads["w_down"] = _mm_tn("down_bwd_w", act, dx2b, _pick(d_ff, 1408, 512), 1024, 512)
    dgu, grads["ffn_conv_w"], grads["ffn_conv_b"] = _ffn_act_bwd(gu, dact, w["ffn_conv_w"], w["ffn_conv_b"])
    dn2 = _mm_up_bwd_x("up_bwd_x", dgu, w["w_up"], tm, 1024)
    grads["w_up"] = _mm_up_bwd_w("up_bwd_w", n2, dgu, 1024, 512)
    dx1, dx1b, grads["norm_ffn_g"] = _rmsnorm_bwd("norm_ffn_bwd", x1, rstd2, w["norm_ffn_g"], dn2, dx2)
    dmix = _mm_nt("out_bwd_x", dx1b, w["w_out"], tm, 1024, 1024)
    grads["w_out"] = _mm_tn("out_bwd_w", mix, dx1b, 1024, 1024, 512)
    dproj, dgate, grads["mlstm_norm_g"] = _mlstm_bwd(proj, gates_t, bias_row, bias_col, w["mlstm_norm_g"],
                                                     cprev, nprev, mprev, dmix)
    (dproj, grads["lru_conv_w"], grads["lru_conv_b"], grads["lru_wa"], grads["lru_wx"],
     grads["lru_ba"], grads["lru_bx"], grads["lru_lambda"]) = _lru_bwd(
        proj, hsave, dmix, dproj, w["lru_conv_w"], w["lru_conv_b"], w["lru_wa"], w["lru_wx"],
        w["lru_ba"], w["lru_bx"], w["lru_lambda"])
    gate_bias_grad, dproj = _gate_grads(dgate, dproj)
    grads["b_gate_m"] = gate_bias_grad[:, :2 * HEADS]
    dn1 = _mm_nt("proj_bwd_x", dproj, w["w_in"], tm, 1024, tn_proj)
    grads["w_in"] = _mm_tn("proj_bwd_w", n1, dproj, 1024, tn_proj, 512)
    grad_x, _, grads["norm_mix_g"] = _rmsnorm_bwd("norm_mix_bwd", x, rstd1, w["norm_mix_g"], dn1, dx1)
    return loss, grad_x, grads


WEIGHT_NAMES = ("norm_mix_g", "w_in", "b_gate_m", "mlstm_norm_g", "lru_conv_w", "lru_conv_b", "lru_wa", "lru_ba",
                "lru_wx", "lru_bx", "lru_lambda", "w_out", "norm_ffn_g", "w_up", "ffn_conv_w", "ffn_conv_b",
                "w_down", "norm_final_g")
BIG = ("w_in", "w_out", "w_up", "w_down")
SMALL_SHARDED = ("mlstm_norm_g", "lru_conv_w", "ffn_conv_w")
SMALL = tuple(n for n in WEIGHT_NAMES if n not in BIG)
SMALL_REPLICATED = tuple(n for n in SMALL if n not in SMALL_SHARDED)


def _proj_segments():
    segs = [(0, 0, _QKVO), (_QKVO, _QKVO + 2 * D_LRU, _N_GATES)]
    for n in range(LRU_BLOCKS):
        segs.append((_QKVO + _N_GATES + n * LRU_BLOCK_DIM, _QKVO + 2 * n * LRU_BLOCK_DIM, LRU_BLOCK_DIM))
        segs.append((_QKVO + _N_GATES + D_LRU + n * LRU_BLOCK_DIM, _QKVO + (2 * n + 1) * LRU_BLOCK_DIM,
                     LRU_BLOCK_DIM))
    return segs


def _w_in_shards_to_local(shards):
    width = shards.shape[2]
    pieces = []
    for g0, _, n in sorted(_proj_segments(), key=lambda s: s[1]):
        at = g0
        while at < g0 + n:
            j = at // width
            stop = min(g0 + n, (j + 1) * width)
            pieces.append(shards[j][:, at - j * width:stop - j * width])
            at = stop
    pieces.append(jnp.zeros((shards.shape[1], PROJ_GATE_PAD - _N_GATES), shards.dtype))
    return jnp.concatenate(pieces, axis=1)


def _w_in_local_to_shards(w):
    width = _PROJ_COLS // N_CHIPS
    shards = []
    for j in range(N_CHIPS):
        pieces = []
        for g0, l0, n in sorted(_proj_segments()):
            lo, hi = max(g0, j * width), min(g0 + n, (j + 1) * width)
            if lo < hi:
                pieces.append(w[:, l0 + lo - g0:l0 + hi - g0])
        shards.append(jnp.concatenate(pieces, axis=1))
    return jnp.stack(shards)


def _w_in_to_global(w):
    sh = _w_in_local_to_shards(w)
    return jnp.concatenate([sh[j] for j in range(N_CHIPS)], axis=1)


def _pack(arrs, rows):
    flat = jnp.concatenate([a.reshape(-1).astype(F32) for a in arrs])
    return jnp.pad(flat, (0, rows * LANES - flat.shape[0])).reshape(rows, LANES)


def _unpack(buf, shapes):
    flat = buf.reshape(-1)
    out, at = [], 0
    for shp in shapes:
        n = 1
        for d in shp:
            n *= d
        out.append(flat[at:at + n].reshape(shp))
        at += n
    return out


def _pack_rows(shapes):
    n = sum(functools.reduce(lambda a, b: a * b, shp, 1) for shp in shapes)
    return -(-n // (HALO * LANES)) * HALO


def _assemble_weights(g_in, g_out, g_up, g_down, small_sharded, replicated):
    w = dict(replicated)
    w["w_in"] = _w_in_shards_to_local(g_in)
    w["w_out"] = g_out.reshape(-1, g_out.shape[-1])
    w["w_up"] = g_up
    w["w_down"] = g_down.reshape(-1, g_down.shape[-1])
    for name, v in small_sharded.items():
        w[name] = jnp.concatenate([v[j] for j in range(N_CHIPS)], axis=1)
    return w


def _full_weights_from_global(weights):
    shard = lambda a, axis: jnp.stack(jnp.split(a, N_CHIPS, axis=axis))
    rep = {n: weights[n].reshape(1, -1) if weights[n].ndim <= 2 and n != "b_gate_m" else weights[n]
           for n in SMALL_REPLICATED}
    rep["b_gate_m"] = weights["b_gate_m"].reshape(1, -1)
    return _assemble_weights(shard(weights["w_in"], 1).astype(BF16), shard(weights["w_out"], 0).astype(BF16),
                             shard(weights["w_up"], 1).astype(BF16), shard(weights["w_down"], 0).astype(BF16),
                             {n: shard(weights[n], 1) for n in SMALL_SHARDED}, rep)


def _grads_to_global(grads):
    g = dict(grads)
    g["w_in"] = _w_in_to_global(grads["w_in"])
    g["w_up"] = jnp.concatenate([grads["w_up"][j] for j in range(N_CHIPS)], axis=1)
    return g


def _place():
    x, y, c = lax.axis_index("x"), lax.axis_index("y"), lax.axis_index("c")
    chips = [(1 - x, y), (x, 1 - y), (1 - x, 1 - y)]
    return x, y, c, 2 * x + y, chips


def _half_rows(n_rows, which):
    half = n_rows // 2
    return pl.ds(pl.multiple_of(which * half, 16), half)


def _rcopy(src, dst, send_sem, recv_sem, to):
    return pltpu.make_async_remote_copy(src_ref=src, dst_ref=dst, send_sem=send_sem, recv_sem=recv_sem,
                                        device_id=to, device_id_type=MESH)


def _gather_weights(shards, small):
    nb = len(shards)

    def body(*refs):
        srcs, small_ref = refs[:nb], refs[nb]
        dsts, small_out = refs[nb + 1:2 * nb + 1], refs[2 * nb + 1]
        send_sems, recv_sems, local_sems = refs[2 * nb + 2:]
        x, y, c, me, chips = _place()
        sibling = (x, y, 1 - c)
        mine = [_half_rows(s.shape[0], c) for s in srcs]
        other = [_half_rows(s.shape[0], 1 - c) for s in srcs]

        local = [pltpu.make_async_copy(srcs[w], dsts[w].at[me], local_sems.at[w]) for w in range(nb)]
        local.append(pltpu.make_async_copy(small_ref, small_out.at[me], local_sems.at[nb]))
        for cp in local:
            cp.start()
        sends = []
        for w in range(nb):
            for k, chip in enumerate(chips):
                sends.append(_rcopy(srcs[w].at[mine[w]], dsts[w].at[me, mine[w]],
                                    send_sems.at[w, k], recv_sems.at[w, k], (*chip, c)))
        for k, chip in enumerate(chips):
            sends.append(_rcopy(small_ref, small_out.at[me], send_sems.at[nb, k], recv_sems.at[nb, k], (*chip, c)))
        for cp in sends:
            cp.start()
        passed = []
        for w in range(nb):
            for k, chip in enumerate(chips):
                cid = 2 * chip[0] + chip[1]
                landed = dsts[w].at[cid, mine[w]]
                _rcopy(landed, landed, send_sems.at[w, k], recv_sems.at[w, k], (*chip, c)).wait_recv()
                fwd = _rcopy(landed, landed, send_sems.at[w, 3 + k], recv_sems.at[w, 3 + k], sibling)
                fwd.start()
                passed.append(fwd)
        for k, chip in enumerate(chips):
            cid = 2 * chip[0] + chip[1]
            _rcopy(small_ref, small_out.at[cid], send_sems.at[nb, k], recv_sems.at[nb, k], (*chip, c)).wait_recv()
        for w in range(nb):
            for k, chip in enumerate(chips):
                cid = 2 * chip[0] + chip[1]
                landed = dsts[w].at[cid, other[w]]
                _rcopy(landed, landed, send_sems.at[w, 3 + k], recv_sems.at[w, 3 + k], sibling).wait_recv()
        for cp in sends + passed:
            cp.wait_send()
        for cp in local:
            cp.wait()

    out_shape = [jax.ShapeDtypeStruct((N_CHIPS,) + s.shape, s.dtype) for s in shards]
    out_shape.append(jax.ShapeDtypeStruct((N_CHIPS,) + small.shape, small.dtype))
    return pl.pallas_call(
        body, name="gather_weights", out_shape=out_shape,
        in_specs=[ANY] * (nb + 1), out_specs=[ANY] * (nb + 1),
        scratch_shapes=[pltpu.SemaphoreType.DMA((nb + 1, 6)), pltpu.SemaphoreType.DMA((nb + 1, 6)),
                        pltpu.SemaphoreType.DMA((nb + 1,))],
    )(*shards, small)


def _pair_exchange(grads, small):
    nb = len(grads)

    def body(*refs):
        srcs, small_ref = refs[:nb], refs[nb]
        dsts, small_out = refs[nb + 1:2 * nb + 1], refs[2 * nb + 1]
        send_sems, recv_sems, small_send, small_recv, local_sem = refs[2 * nb + 2:]
        x, y, c, _, _ = _place()
        sibling = (x, y, 1 - c)
        my_id = 4 * x + 2 * y + c
        local = pltpu.make_async_copy(small_ref, small_out.at[my_id], local_sem)
        local.start()
        sends = []
        for w in range(nb):
            other = _half_rows(srcs[w].shape[1], 1 - c)
            sends.append(_rcopy(srcs[w].at[:, other], dsts[w], send_sems.at[w], recv_sems.at[w], sibling))
        for r in range(1, N_DEV):
            to = (1 - x if r & 4 else x, 1 - y if r & 2 else y, 1 - c if r & 1 else c)
            sends.append(_rcopy(small_ref, small_out.at[my_id], small_send.at[r - 1], small_recv.at[r - 1], to))
        for cp in sends:
            cp.start()
        for w in range(nb):
            _rcopy(dsts[w], dsts[w], send_sems.at[w], recv_sems.at[w], sibling).wait_recv()
        for r in range(1, N_DEV):
            frm = (1 - x if r & 4 else x, 1 - y if r & 2 else y, 1 - c if r & 1 else c)
            frm_id = 4 * frm[0] + 2 * frm[1] + frm[2]
            _rcopy(small_ref, small_out.at[frm_id], small_send.at[r - 1], small_recv.at[r - 1], frm).wait_recv()
        for cp in sends:
            cp.wait_send()
        local.wait()

    out_shape = [jax.ShapeDtypeStruct((g.shape[0], g.shape[1] // 2, g.shape[2]), g.dtype) for g in grads]
    out_shape.append(jax.ShapeDtypeStruct((N_DEV,) + small.shape, small.dtype))
    return pl.pallas_call(
        body, name="pair_exchange", out_shape=out_shape,
        in_specs=[ANY] * (nb + 1), out_specs=[ANY] * (nb + 1),
        scratch_shapes=[pltpu.SemaphoreType.DMA((nb,)), pltpu.SemaphoreType.DMA((nb,)),
                        pltpu.SemaphoreType.DMA((N_DEV - 1,)), pltpu.SemaphoreType.DMA((N_DEV - 1,)),
                        pltpu.SemaphoreType.DMA(())],
    )(*grads, small)


def _chip_exchange(partials):
    nb = len(partials)

    def body(*refs):
        srcs, dsts = refs[:nb], refs[nb:2 * nb]
        send_sems, recv_sems = refs[2 * nb:]
        _, _, c, me, chips = _place()
        sends = []
        for w in range(nb):
            for k, chip in enumerate(chips):
                cid = 2 * chip[0] + chip[1]
                sends.append(_rcopy(srcs[w].at[cid], dsts[w].at[me], send_sems.at[w, k], recv_sems.at[w, k],
                                    (*chip, c)))
        for cp in sends:
            cp.start()
        for w in range(nb):
            for k, chip in enumerate(chips):
                cid = 2 * chip[0] + chip[1]
                _rcopy(srcs[w].at[cid], dsts[w].at[cid], send_sems.at[w, k], recv_sems.at[w, k],
                       (*chip, c)).wait_recv()
        for cp in sends:
            cp.wait_send()

    return pl.pallas_call(
        body, name="chip_exchange", out_shape=[jax.ShapeDtypeStruct(p.shape, p.dtype) for p in partials],
        in_specs=[ANY] * nb, out_specs=[ANY] * nb,
        scratch_shapes=[pltpu.SemaphoreType.DMA((nb, 3)), pltpu.SemaphoreType.DMA((nb, 3))],
    )(*partials)


def _pair_share(shards):
    nb = len(shards)

    def body(*refs):
        srcs, dsts = refs[:nb], refs[nb:2 * nb]
        send_sems, recv_sems = refs[2 * nb:]
        x, y, c, _, _ = _place()
        sibling = (x, y, 1 - c)
        sends = []
        for w in range(nb):
            mine = _half_rows(dsts[w].shape[0], c)
            sends.append(_rcopy(srcs[w].at[mine], dsts[w].at[mine], send_sems.at[w], recv_sems.at[w], sibling))
        for cp in sends:
            cp.start()
        for w in range(nb):
            other = _half_rows(dsts[w].shape[0], 1 - c)
            _rcopy(srcs[w].at[other], dsts[w].at[other], send_sems.at[w], recv_sems.at[w], sibling).wait_recv()
        for cp in sends:
            cp.wait_send()

    return pl.pallas_call(
        body, name="pair_share", out_shape=[jax.ShapeDtypeStruct(h.shape, h.dtype) for h in shards],
        in_specs=[ANY] * nb, out_specs=[ANY] * nb,
        scratch_shapes=[pltpu.SemaphoreType.DMA((nb,)), pltpu.SemaphoreType.DMA((nb,))],
        input_output_aliases={w: w for w in range(nb)},
    )(*shards)


def _row_tile(rows):
    return _pick(rows, 128, 64, 16, 8)


def _pair_sum(name, idx, grad, recv):
    n, half, cols = recv.shape
    tr = _row_tile(half)
    nrb = half // tr

    def body(idx_ref, g_ref, r_ref, o_ref):
        o_ref[...] = (g_ref[...] + r_ref[...]).astype(BF16)

    return pl.pallas_call(
        body, name=name, out_shape=jax.ShapeDtypeStruct(recv.shape, BF16),
        grid_spec=pltpu.PrefetchScalarGridSpec(
            num_scalar_prefetch=1, grid=(n, nrb),
            in_specs=[pl.BlockSpec((None, tr, cols), lambda j, i, s: (j, s[0] * nrb + i, 0)),
                      pl.BlockSpec((None, tr, cols), lambda j, i, s: (j, i, 0))],
            out_specs=pl.BlockSpec((None, tr, cols), lambda j, i, s: (j, i, 0))),
        compiler_params=_params(("parallel", "parallel")),
    )(idx, grad, recv)


def _final_sum(name, idx, grad, recv, chip_sums):
    _, half, cols = recv.shape
    tr = _row_tile(half)
    nrb = half // tr

    def body(idx_ref, g_ref, r_ref, p1_ref, p2_ref, p3_ref, o_ref):
        acc = g_ref[...] + r_ref[...]
        for p_ref in (p1_ref, p2_ref, p3_ref):
            acc = acc + p_ref[...].astype(F32)
        o_ref[...] = acc

    slot = lambda which: pl.BlockSpec((None, tr, cols), lambda i, s, which=which: (s[which], i, 0))
    return pl.pallas_call(
        body, name=name, out_shape=jax.ShapeDtypeStruct((2 * half, cols), F32),
        grid_spec=pltpu.PrefetchScalarGridSpec(
            num_scalar_prefetch=1, grid=(nrb,),
            in_specs=[pl.BlockSpec((None, tr, cols), lambda i, s: (s[1], s[0] * nrb + i, 0)),
                      slot(1), slot(2), slot(3), slot(4)],
            out_specs=pl.BlockSpec((tr, cols), lambda i, s: (s[0] * nrb + i, 0))),
        compiler_params=_params(("parallel",)),
    )(idx, grad, recv, chip_sums, chip_sums, chip_sums)


def _small_sum(packs):
    n, rows, _ = packs.shape

    def body(p_ref, o_ref):
        acc = p_ref[0]
        for k in range(1, n):
            acc = acc + p_ref[k]
        o_ref[...] = acc

    return pl.pallas_call(
        body, name="small_sum", out_shape=jax.ShapeDtypeStruct((rows, LANES), F32),
        in_specs=[pl.BlockSpec(memory_space=pltpu.VMEM)], out_specs=pl.BlockSpec(memory_space=pltpu.VMEM),
        compiler_params=pltpu.CompilerParams(vmem_limit_bytes=VMEM_LIMIT),
    )(packs)


def _adamw(name, w, g, m, v):
    rows, cols = w.shape
    tr = rows if rows * cols * 4 <= (2 << 20) else _row_tile(rows)

    def body(w_ref, g_ref, m_ref, v_ref, g_out_ref, d_ref, nm_ref, nv_ref):
        gv = g_ref[...]
        g_out_ref[...] = gv
        m_new = ADAM_B1 * m_ref[...] + (1.0 - ADAM_B1) * gv
        v_new = ADAM_B2 * v_ref[...] + (1.0 - ADAM_B2) * (gv * gv)
        m_hat = m_new / (1.0 - ADAM_B1 ** ADAM_STEP)
        v_hat = v_new / (1.0 - ADAM_B2 ** ADAM_STEP)
        d_ref[...] = -ADAM_LR * (m_hat / (jnp.sqrt(v_hat) + ADAM_EPS) + ADAM_WD * w_ref[...])
        nm_ref[...] = m_new
        nv_ref[...] = v_new

    blk = pl.BlockSpec((tr, cols), lambda i: (i, 0))
    sds = jax.ShapeDtypeStruct((rows, cols), F32)
    return pl.pallas_call(
        body, name=name, grid=(rows // tr,), in_specs=[blk] * 4, out_specs=[blk] * 4, out_shape=[sds] * 4,
        compiler_params=_params(("parallel",)),
    )(w, g, m, v)


def _train_step(x, target, W, M, V):
    xi, yi, ci = lax.axis_index("x"), lax.axis_index("y"), lax.axis_index("c")
    me = 2 * xi + yi
    big = {n: W[n][0] for n in BIG}

    sharded_shapes = [W[n].shape[1:] for n in SMALL_SHARDED]
    small_rows = _pack_rows(sharded_shapes)
    g_big_and_small = _gather_weights([big[n].astype(BF16) for n in BIG],
                                      _pack([W[n][0] for n in SMALL_SHARDED], small_rows))
    g_in, g_out, g_up, g_down, g_small = g_big_and_small
    per_chip = [_unpack(g_small[j], sharded_shapes) for j in range(N_CHIPS)]
    small_sharded = {n: jnp.stack([per_chip[j][i] for j in range(N_CHIPS)]) for i, n in enumerate(SMALL_SHARDED)}
    replicated = {n: (W[n].reshape(1, -1) if W[n].ndim <= 2 else W[n][0]) for n in SMALL_REPLICATED}
    full = _assemble_weights(g_in, g_out, g_up, g_down, small_sharded, replicated)

    loss, grad_x, grads = _local_step(x[0], target[0], full)
    loss = lax.psum(loss[0, 0], ("x", "y", "c"))

    shard_major = [_w_in_local_to_shards(grads["w_in"]),
                   grads["w_out"].reshape((N_CHIPS, -1) + grads["w_out"].shape[1:]),
                   grads["w_up"],
                   grads["w_down"].reshape((N_CHIPS, -1) + grads["w_down"].shape[1:])]
    small_shapes = [W[n].shape[1:] if n not in SMALL_SHARDED else (W[n].shape[1], W[n].shape[2] * N_CHIPS)
                    for n in SMALL]
    small_shapes = [tuple(W[n].shape) if n == "norm_final_g" else shp for n, shp in zip(SMALL, small_shapes)]
    small_pack = _pack([grads[n] for n in SMALL], _pack_rows(small_shapes))
    *recv, small_all = _pair_exchange(shard_major, small_pack)
    others = [jnp.where(jnp.int32(i) >= me, i + 1, i) for i in range(N_CHIPS - 1)]
    idx = jnp.stack([ci, me] + others).astype(jnp.int32)
    partial = [_pair_sum(f"pair_sum_{n}", idx, g, r) for n, g, r in zip(BIG, shard_major, recv)]
    chip_sums = _chip_exchange(partial)
    halves = [_final_sum(f"final_sum_{n}", idx, g, r, p) for n, g, r, p in zip(BIG, shard_major, recv, chip_sums)]
    big_grads = dict(zip(BIG, _pair_share(halves)))

    small_grads = dict(zip(SMALL, _unpack(_small_sum(small_all), small_shapes)))
    for n in SMALL_SHARDED:
        width = W[n].shape[2]
        small_grads[n] = lax.dynamic_slice_in_dim(small_grads[n], me * width, width, axis=1)

    out_g, out_d, out_m, out_v = {}, {}, {}, {}
    for n in BIG:
        g, d, nm, nv = _adamw(f"adamw_{n}", big[n], big_grads[n], M[n][0], V[n][0])
        out_g[n], out_d[n], out_m[n], out_v[n] = g[None], d[None], nm[None], nv[None]
    local_shapes = [tuple(W[n].shape) for n in SMALL]
    rows = _pack_rows(local_shapes)
    packed = [_pack([src[n] for n in SMALL], rows) for src in (W, small_grads, M, V)]
    _, d, nm, nv = _adamw("adamw_small", *packed)
    for dst, buf in ((out_d, d), (out_m, nm), (out_v, nv)):
        dst.update(zip(SMALL, _unpack(buf, local_shapes)))
    for n in SMALL:
        out_g[n] = small_grads[n].reshape(W[n].shape)
    return (loss, grad_x[None], *[out_g[n] for n in WEIGHT_NAMES], *[out_d[n] for n in WEIGHT_NAMES],
            *[out_m[n] for n in WEIGHT_NAMES], *[out_v[n] for n in WEIGHT_NAMES])


def kernel(x, norm_mix_g, w_in, b_gate_m, mlstm_norm_g, lru_conv_w, lru_conv_b, lru_wa, lru_ba, lru_wx, lru_bx, lru_lambda, w_out, norm_ffn_g, w_up, ffn_conv_w, ffn_conv_b, w_down, norm_final_g, loss_target, m_norm_mix_g, m_w_in, m_b_gate_m, m_mlstm_norm_g, m_lru_conv_w, m_lru_conv_b, m_lru_wa, m_lru_ba, m_lru_wx, m_lru_bx, m_lru_lambda, m_w_out, m_norm_ffn_g, m_w_up, m_ffn_conv_w, m_ffn_conv_b, m_w_down, m_norm_final_g, v_norm_mix_g, v_w_in, v_b_gate_m, v_mlstm_norm_g, v_lru_conv_w, v_lru_conv_b, v_lru_wa, v_lru_ba, v_lru_wx, v_lru_bx, v_lru_lambda, v_w_out, v_norm_ffn_g, v_w_up, v_ffn_conv_w, v_ffn_conv_b, v_w_down, v_norm_final_g):
    W = dict(zip(WEIGHT_NAMES, (norm_mix_g, w_in, b_gate_m, mlstm_norm_g, lru_conv_w, lru_conv_b, lru_wa, lru_ba,
                                lru_wx, lru_bx, lru_lambda, w_out, norm_ffn_g, w_up, ffn_conv_w, ffn_conv_b,
                                w_down, norm_final_g)))
    M = dict(zip(WEIGHT_NAMES, (m_norm_mix_g, m_w_in, m_b_gate_m, m_mlstm_norm_g, m_lru_conv_w, m_lru_conv_b,
                                m_lru_wa, m_lru_ba, m_lru_wx, m_lru_bx, m_lru_lambda, m_w_out, m_norm_ffn_g,
                                m_w_up, m_ffn_conv_w, m_ffn_conv_b, m_w_down, m_norm_final_g)))
    V = dict(zip(WEIGHT_NAMES, (v_norm_mix_g, v_w_in, v_b_gate_m, v_mlstm_norm_g, v_lru_conv_w, v_lru_conv_b,
                                v_lru_wa, v_lru_ba, v_lru_wx, v_lru_bx, v_lru_lambda, v_w_out, v_norm_ffn_g,
                                v_w_up, v_ffn_conv_w, v_ffn_conv_b, v_w_down, v_norm_final_g)))
    return _train_step(x, loss_target, W, M, V)
```

```python
import functools

import jax
import jax.numpy as jnp
from jax import lax
from jax.experimental import pallas as pl
from jax.experimental.pallas import tpu as pltpu

F32 = jnp.float32
BF16 = jnp.bfloat16
MESH = pl.DeviceIdType.MESH

EPS = 1e-6
CHUNK = 64
HEADS = 4
HEAD_DIM = 256
D_MLSTM = HEADS * HEAD_DIM
LRU_BLOCKS = 8
LRU_BLOCK_DIM = 128
D_LRU = LRU_BLOCKS * LRU_BLOCK_DIM
LRU_C = 8.0
LRU_CONV = 4
FFN_CONV = 3
ADAM_LR = 0.001
ADAM_B1 = 0.9
ADAM_B2 = 0.999
ADAM_EPS = 1e-08
ADAM_WD = 0.01
ADAM_STEP = 10

N_CHIPS = 4
N_DEV = 8
LANES = 128
HALO = 8
PROJ_GATE_PAD = LANES
_QKVO = 4 * D_MLSTM
_N_GATES = 2 * HEADS
_PROJ_COLS = _QKVO + _N_GATES + 2 * D_LRU
_PROJ_PAD = _QKVO + 2 * D_LRU + PROJ_GATE_PAD
VMEM_LIMIT = 48 * 1024 * 1024
ANY = pl.BlockSpec(memory_space=pl.ANY)


def _params(sem, vmem=VMEM_LIMIT):
    return pltpu.CompilerParams(dimension_semantics=sem, vmem_limit_bytes=vmem)


def _matmul(name, a, b, grid, a_spec, b_spec, o_spec, out_sds, contract, res=None, res_spec=None):
    nk = grid[2]
    acc_shape = tuple(d for d in o_spec.block_shape if d is not None)

    def body(*refs):
        if res is None:
            a_ref, b_ref, o_ref, acc_ref = refs
            r_ref = None
        else:
            a_ref, b_ref, r_ref, o_ref, acc_ref = refs
        k = pl.program_id(2)

        @pl.when(k == 0)
        def _():
            acc_ref[...] = jnp.zeros_like(acc_ref)

        acc_ref[...] += lax.dot_general(a_ref[...], b_ref[...], (contract, ((), ())),
                                        preferred_element_type=F32)

        @pl.when(k == nk - 1)
        def _():
            r = acc_ref[...]
            if r_ref is not None:
                r = r_ref[...] + r
            o_ref[...] = r.astype(o_ref.dtype)

    in_specs = [a_spec, b_spec] + ([] if res is None else [res_spec])
    args = (a, b) + (() if res is None else (res,))
    return pl.pallas_call(
        body, out_shape=out_sds, grid=grid, in_specs=in_specs, out_specs=o_spec,
        scratch_shapes=[pltpu.VMEM(acc_shape, F32)], name=name,
        compiler_params=_params(("parallel", "parallel", "arbitrary")),
    )(*args)


NN = ((1,), (0,))
NT = ((1,), (1,))
TN = ((0,), (0,))


def _mm_nn(name, a, b, tm, tn, tk, out_dtype=F32, res=None):
    m, k = a.shape
    n = b.shape[1]
    return _matmul(name, a, b, (m // tm, n // tn, k // tk),
                   pl.BlockSpec((tm, tk), lambda i, j, kk: (i, kk)),
                   pl.BlockSpec((tk, tn), lambda i, j, kk: (kk, j)),
                   pl.BlockSpec((tm, tn), lambda i, j, kk: (i, j)),
                   jax.ShapeDtypeStruct((m, n), out_dtype), NN,
                   res=res, res_spec=pl.BlockSpec((tm, tn), lambda i, j, kk: (i, j)))


def _mm_nt(name, a, b, tm, tn, tk, out_dtype=F32, res=None):
    m, k = a.shape
    n = b.shape[0]
    return _matmul(name, a, b, (m // tm, n // tn, k // tk),
                   pl.BlockSpec((tm, tk), lambda i, j, kk: (i, kk)),
                   pl.BlockSpec((tn, tk), lambda i, j, kk: (j, kk)),
                   pl.BlockSpec((tm, tn), lambda i, j, kk: (i, j)),
                   jax.ShapeDtypeStruct((m, n), out_dtype), NT,
                   res=res, res_spec=pl.BlockSpec((tm, tn), lambda i, j, kk: (i, j)))


def _mm_tn(name, a, b, tm, tn, tk, out_dtype=F32):
    k, m = a.shape
    n = b.shape[1]
    tk = min(tk, k)
    return _matmul(name, a, b, (m // tm, n // tn, k // tk),
                   pl.BlockSpec((tk, tm), lambda i, j, kk: (kk, i)),
                   pl.BlockSpec((tk, tn), lambda i, j, kk: (kk, j)),
                   pl.BlockSpec((tm, tn), lambda i, j, kk: (i, j)),
                   jax.ShapeDtypeStruct((m, n), out_dtype), TN)


def _up_shard(n):
    return 2 * (n % 2) + (n // 2) // 2, (n // 2) % 2


def _mm_up_fwd(name, a, wg_up, tm, tk):
    m, k = a.shape
    _, _, cols = wg_up.shape
    tn = cols // 2
    return _matmul(name, a, wg_up, (m // tm, 2 * N_CHIPS, k // tk),
                   pl.BlockSpec((tm, tk), lambda i, j, kk: (i, kk)),
                   pl.BlockSpec((None, tk, tn), lambda i, j, kk: (_up_shard(j)[0], kk, _up_shard(j)[1])),
                   pl.BlockSpec((tm, tn), lambda i, j, kk: (i, j)),
                   jax.ShapeDtypeStruct((m, 2 * N_CHIPS * tn), F32), NN)


def _mm_up_bwd_x(name, dgu, wg_up, tm, tn):
    m, _ = dgu.shape
    _, d, cols = wg_up.shape
    tk = cols // 2
    return _matmul(name, dgu, wg_up, (m // tm, d // tn, 2 * N_CHIPS),
                   pl.BlockSpec((tm, tk), lambda i, j, kk: (i, kk)),
                   pl.BlockSpec((None, tn, tk), lambda i, j, kk: (_up_shard(kk)[0], j, _up_shard(kk)[1])),
                   pl.BlockSpec((tm, tn), lambda i, j, kk: (i, j)),
                   jax.ShapeDtypeStruct((m, d), F32), NT)


def _mm_up_bwd_w(name, n2, dgu, tm, tk):
    s, d = n2.shape
    tk = min(tk, s)
    tn = dgu.shape[1] // (2 * N_CHIPS)
    return _matmul(name, n2, dgu, (d // tm, 2 * N_CHIPS, s // tk),
                   pl.BlockSpec((tk, tm), lambda i, j, kk: (kk, i)),
                   pl.BlockSpec((tk, tn), lambda i, j, kk: (kk, j)),
                   pl.BlockSpec((None, tm, tn), lambda i, j, kk: (_up_shard(j)[0], i, _up_shard(j)[1])),
                   jax.ShapeDtypeStruct((N_CHIPS, d, 2 * tn), F32), TN)


def _rmsnorm_fwd(name, x, g, tm=256):
    s, d = x.shape

    def body(x_ref, g_ref, n_ref, r_ref):
        xf = x_ref[...]
        r = lax.rsqrt(jnp.mean(xf * xf, axis=-1, keepdims=True) + EPS)
        n_ref[...] = ((xf * r) * g_ref[...]).astype(BF16)
        r_ref[...] = r

    return pl.pallas_call(
        body, grid=(s // tm,), name=name,
        in_specs=[pl.BlockSpec((tm, d), lambda i: (i, 0)), pl.BlockSpec((1, d), lambda i: (0, 0))],
        out_specs=[pl.BlockSpec((tm, d), lambda i: (i, 0)), pl.BlockSpec((tm, 1), lambda i: (i, 0))],
        out_shape=[jax.ShapeDtypeStruct((s, d), BF16), jax.ShapeDtypeStruct((s, 1), F32)],
        compiler_params=_params(("parallel",)),
    )(x, g)


def _rmsnorm_bwd(name, x, rstd, g, dn, dres, tm=256):
    s, d = x.shape

    def body(x_ref, r_ref, g_ref, dn_ref, dres_ref, dx_ref, dxb_ref, dg_ref):
        @pl.when(pl.program_id(0) == 0)
        def _():
            dg_ref[...] = jnp.zeros_like(dg_ref)

        r = r_ref[...]
        xhat = x_ref[...] * r
        dn_v = dn_ref[...]
        dxhat = dn_v * g_ref[...]
        dx = dres_ref[...] + r * (dxhat - xhat * jnp.mean(dxhat * xhat, axis=-1, keepdims=True))
        dx_ref[...] = dx
        dxb_ref[...] = dx.astype(BF16)
        dg_ref[...] += jnp.sum(dn_v * xhat, axis=0, keepdims=True)

    row = pl.BlockSpec((tm, d), lambda i: (i, 0))
    vec = pl.BlockSpec((1, d), lambda i: (0, 0))
    return pl.pallas_call(
        body, grid=(s // tm,), name=name,
        in_specs=[row, pl.BlockSpec((tm, 1), lambda i: (i, 0)), vec, row, row],
        out_specs=[row, row, vec],
        out_shape=[jax.ShapeDtypeStruct((s, d), F32), jax.ShapeDtypeStruct((s, d), BF16),
                   jax.ShapeDtypeStruct((1, d), F32)],
        compiler_params=_params(("arbitrary",)),
    )(x, rstd, g, dn, dres)


def _loss_head(name, x, g, target, tm=256):
    s, d = x.shape

    def body(x_ref, g_ref, t_ref, loss_ref, dx_ref, dxb_ref, dg_ref):
        @pl.when(pl.program_id(0) == 0)
        def _():
            dg_ref[...] = jnp.zeros_like(dg_ref)
            loss_ref[...] = jnp.zeros_like(loss_ref)

        xf = x_ref[...]
        gv = g_ref[...]
        r = lax.rsqrt(jnp.mean(xf * xf, axis=-1, keepdims=True) + EPS)
        xhat = xf * r
        err = xhat * gv - t_ref[...]
        loss_ref[...] += 0.5 * jnp.sum(jnp.mean(err * err, axis=-1, keepdims=True), axis=0, keepdims=True)
        dy = err * (1.0 / d)
        dxhat = dy * gv
        dx = r * (dxhat - xhat * jnp.mean(dxhat * xhat, axis=-1, keepdims=True))
        dx_ref[...] = dx
        dxb_ref[...] = dx.astype(BF16)
        dg_ref[...] += jnp.sum(dy * xhat, axis=0, keepdims=True)

    row = pl.BlockSpec((tm, d), lambda i: (i, 0))
    vec = pl.BlockSpec((1, d), lambda i: (0, 0))
    return pl.pallas_call(
        body, grid=(s // tm,), name=name,
        in_specs=[row, vec, row],
        out_specs=[pl.BlockSpec((1, 1), lambda i: (0, 0)), row, row, vec],
        out_shape=[jax.ShapeDtypeStruct((1, 1), F32), jax.ShapeDtypeStruct((s, d), F32),
                   jax.ShapeDtypeStruct((s, d), BF16), jax.ShapeDtypeStruct((1, d), F32)],
        compiler_params=_params(("arbitrary",)),
    )(x, g, target)


def _sigmoid(v):
    return 1.0 / (1.0 + jnp.exp(-v))


def _log_sigmoid(v):
    return jnp.minimum(v, 0.0) - jnp.log1p(jnp.exp(-jnp.abs(v)))


def _softplus(v):
    return jnp.maximum(v, 0.0) + jnp.log1p(jnp.exp(-jnp.abs(v)))


def _one_minus_exp(z):
    series = -z * (1.0 + z * (0.5 + z * (1.0 / 6.0 + z * (1.0 / 24.0 + z * (1.0 / 120.0)))))
    return jnp.where(z > -0.1, series, 1.0 - jnp.exp(z))


_GELU_K = 0.7978845608028654
_GELU_C = 0.044715


def _gelu(v):
    return 0.5 * v * (1.0 + jnp.tanh(_GELU_K * (v + _GELU_C * v * v * v)))


def _gelu_grad(v):
    t = jnp.tanh(_GELU_K * (v + _GELU_C * v * v * v))
    return 0.5 * (1.0 + t) + 0.5 * v * (1.0 - t * t) * _GELU_K * (1.0 + 3.0 * _GELU_C * v * v)


def _rows(shape):
    return lax.broadcasted_iota(jnp.int32, shape, 0)


def _cols(shape):
    return lax.broadcasted_iota(jnp.int32, shape, 1)


def _shift_down(v, prev, d):
    if d == 0:
        return v
    rolled = pltpu.roll(v, d, axis=0)
    head = jnp.where(_rows((HALO, v.shape[1])) >= d, rolled[:HALO], pltpu.roll(prev, d, axis=0))
    if v.shape[0] == HALO:
        return head
    return jnp.concatenate([head, rolled[HALO:]], axis=0)


def _shift_up(v, nxt, d):
    if d == 0:
        return v
    n = v.shape[0]
    rolled = pltpu.roll(v, n - d, axis=0)
    tail = jnp.where(_rows((HALO, v.shape[1])) < HALO - d, rolled[n - HALO:], pltpu.roll(nxt, HALO - d, axis=0))
    if n == HALO:
        return tail
    return jnp.concatenate([rolled[:n - HALO], tail], axis=0)


def _dot(a, b, contract):
    return lax.dot_general(a.astype(BF16), b.astype(BF16), (contract, ((), ())), preferred_element_type=F32)


def _mlstm_chunk_common(h, q_ref, k_ref, v_ref, gcol_ref, grow_ref, brow_ref, bcol_ref, m_prev):
    L = CHUNK
    sl = slice(h * HEAD_DIM, (h + 1) * HEAD_DIM)
    qh = q_ref[:, sl]
    kh = k_ref[:, sl]
    vh = v_ref[:, sl]
    qs = qh * (HEAD_DIM ** -0.5)
    gates = gcol_ref[...] + brow_ref[...]
    lane = _cols(gates.shape)
    ic = jnp.sum(jnp.where(lane == h, gates, 0.0), axis=1, keepdims=True)
    fc = jnp.sum(jnp.where(lane == HEADS + h, gates, 0.0), axis=1, keepdims=True)
    ir = grow_ref[h:h + 1, :] + bcol_ref[h:h + 1, :]
    fr = grow_ref[HEADS + h:HEADS + h + 1, :] + bcol_ref[HEADS + h:HEADS + h + 1, :]
    logf_c = _log_sigmoid(fc)
    logf_r = _log_sigmoid(fr)
    t_i = _rows((L, L))
    s_i = _cols((L, L))
    tri = t_i >= s_i
    b_c = jnp.sum(jnp.where(tri, logf_r, 0.0), axis=1, keepdims=True)
    b_r = jnp.sum(jnp.where(t_i <= s_i, logf_c, 0.0), axis=0, keepdims=True)
    btot = jnp.sum(logf_r, axis=1, keepdims=True)
    dmat = jnp.where(tri, b_c - b_r + ir, -jnp.inf)
    m_inter = b_c + m_prev
    m_t = jnp.maximum(m_inter, jnp.max(dmat, axis=1, keepdims=True))
    e_mat = jnp.exp(dmat - m_t)
    e_inter = jnp.exp(m_inter - m_t)
    wqk = _dot(qs, kh, NT) * e_mat
    w_end_r = btot - b_r + ir
    m_loc = jnp.max(w_end_r, axis=1, keepdims=True)
    e_end_c = jnp.exp(btot - b_c + ic - m_loc)
    m_new = jnp.maximum(btot + m_prev, m_loc)
    a_dec = jnp.exp(btot + m_prev - m_new)
    c_inj = jnp.exp(m_loc - m_new)
    return dict(qh=qh, kh=kh, vh=vh, qs=qs, fc=fc, tri=tri, t_i=t_i, s_i=s_i, m_t=m_t, e_mat=e_mat,
                e_inter=e_inter, wqk=wqk, e_end_c=e_end_c, m_new=m_new, a_dec=a_dec, c_inj=c_inj)


def _mlstm_fwd(proj, gates_t, bias_row, bias_col, head_g):
    s = proj.shape[0]
    nc = s // CHUNK
    L = CHUNK

    def body(q_ref, k_ref, v_ref, o_ref, gcol_ref, grow_ref, brow_ref, bcol_ref, hg_ref,
             out_ref, cprev_ref, nprev_ref, mprev_ref, c_scr, n_scr, m_scr):
        @pl.when(pl.program_id(0) == 0)
        def _():
            c_scr[...] = jnp.zeros_like(c_scr)
            n_scr[...] = jnp.zeros_like(n_scr)
            m_scr[...] = jnp.zeros_like(m_scr)

        for h in range(HEADS):
            sl = slice(h * HEAD_DIM, (h + 1) * HEAD_DIM)
            m_prev = m_scr[h:h + 1, 0:1]
            n_prev = n_scr[h:h + 1, :]
            c_prev = c_scr[h].astype(BF16)
            q = _mlstm_chunk_common(h, q_ref, k_ref, v_ref, gcol_ref, grow_ref, brow_ref, bcol_ref, m_prev)
            num = _dot(q["wqk"], q["vh"], NN) + q["e_inter"] * _dot(q["qs"], c_prev, NN)
            den = (jnp.sum(q["wqk"], axis=1, keepdims=True)
                   + q["e_inter"] * jnp.sum(q["qs"] * n_prev, axis=1, keepdims=True))
            hh = num / jnp.maximum(jnp.abs(den), jnp.exp(-q["m_t"]))
            hn = hh * lax.rsqrt(jnp.mean(hh * hh, axis=1, keepdims=True) + EPS) * hg_ref[h:h + 1, :]
            out_ref[:, sl] = (_sigmoid(o_ref[:, sl]) * hn).astype(BF16)
            cprev_ref[h] = c_prev
            nprev_ref[h:h + 1, :] = n_prev
            mprev_ref[h:h + 1, :] = jnp.broadcast_to(m_prev, (1, LANES))
            c_loc = _dot(q["kh"], q["e_end_c"] * q["vh"], TN)
            n_loc = jnp.sum(q["e_end_c"] * q["kh"], axis=0, keepdims=True)
            c_scr[h] = q["a_dec"] * c_scr[h] + q["c_inj"] * c_loc
            n_scr[h:h + 1, :] = q["a_dec"] * n_prev + q["c_inj"] * n_loc
            m_scr[h:h + 1, :] = jnp.broadcast_to(q["m_new"], (1, LANES))

    blk = lambda j: pl.BlockSpec((L, D_MLSTM), lambda c, j=j: (c, j))
    full = lambda shp: pl.BlockSpec(shp, lambda c: tuple(0 for _ in shp))
    return pl.pallas_call(
        body, grid=(nc,), name="mlstm_fwd",
        in_specs=[blk(0), blk(1), blk(2), blk(3),
                  pl.BlockSpec((L, LANES), lambda c: (c, (4 * D_MLSTM + 2 * D_LRU) // LANES)),
                  pl.BlockSpec((None, 2 * HEADS, L), lambda c: (c, 0, 0)),
                  full((1, LANES)), full((2 * HEADS, 1)), full((HEADS, HEAD_DIM))],
        out_specs=[pl.BlockSpec((L, D_MLSTM), lambda c: (c, 0)),
                   pl.BlockSpec((None, HEADS, HEAD_DIM, HEAD_DIM), lambda c: (c, 0, 0, 0)),
                   pl.BlockSpec((None, HEADS, HEAD_DIM), lambda c: (c, 0, 0)),
                   pl.BlockSpec((None, HEADS, LANES), lambda c: (c, 0, 0))],
        out_shape=[jax.ShapeDtypeStruct((s, D_MLSTM + D_LRU), BF16),
                   jax.ShapeDtypeStruct((nc, HEADS, HEAD_DIM, HEAD_DIM), BF16),
                   jax.ShapeDtypeStruct((nc, HEADS, HEAD_DIM), F32),
                   jax.ShapeDtypeStruct((nc, HEADS, LANES), F32)],
        scratch_shapes=[pltpu.VMEM((HEADS, HEAD_DIM, HEAD_DIM), F32), pltpu.VMEM((HEADS, HEAD_DIM), F32),
                        pltpu.VMEM((HEADS, LANES), F32)],
        compiler_params=_params(("arbitrary",)),
    )(proj, proj, proj, proj, proj, gates_t, bias_row, bias_col, head_g)


def _mlstm_bwd(proj, gates_t, bias_row, bias_col, head_g, cprev, nprev, mprev, dmix):
    s = proj.shape[0]
    nc = s // CHUNK
    L = CHUNK

    def body(q_ref, k_ref, v_ref, o_ref, gcol_ref, grow_ref, brow_ref, bcol_ref, hg_ref,
             cprev_ref, nprev_ref, mprev_ref, dmix_ref,
             dqkvo_ref, dgate_ref, dhg_ref, g_scr, gn_scr):
        @pl.when(pl.program_id(0) == 0)
        def _():
            g_scr[...] = jnp.zeros_like(g_scr)
            gn_scr[...] = jnp.zeros_like(gn_scr)
            dhg_ref[...] = jnp.zeros_like(dhg_ref)

        lane = _cols((L, LANES))
        dgate = jnp.zeros((L, LANES), F32)
        for h in range(HEADS):
            sl = slice(h * HEAD_DIM, (h + 1) * HEAD_DIM)
            m_prev = mprev_ref[h:h + 1, 0:1]
            n_prev = nprev_ref[h:h + 1, :]
            c_prev = cprev_ref[h]
            q = _mlstm_chunk_common(h, q_ref, k_ref, v_ref, gcol_ref, grow_ref, brow_ref, bcol_ref, m_prev)
            qh, kh, vh, qs, wqk, e_inter = q["qh"], q["kh"], q["vh"], q["qs"], q["wqk"], q["e_inter"]
            num_state = e_inter * _dot(qs, c_prev, NN)
            den_state = e_inter * jnp.sum(qs * n_prev, axis=1, keepdims=True)
            num = _dot(wqk, vh, NN) + num_state
            den = jnp.sum(wqk, axis=1, keepdims=True) + den_state
            floor = jnp.exp(-q["m_t"])
            denom = jnp.maximum(jnp.abs(den), floor)
            hh = num / denom
            rn = lax.rsqrt(jnp.mean(hh * hh, axis=1, keepdims=True) + EPS)
            hn_pre = hh * rn
            hg = hg_ref[h:h + 1, :]
            sg = _sigmoid(o_ref[:, sl])
            dout = dmix_ref[:, sl]
            d_o = dout * (hn_pre * hg) * sg * (1.0 - sg)
            dhn = dout * sg
            dhg_ref[h:h + 1, :] += jnp.sum(dhn * hn_pre, axis=0, keepdims=True)
            dhn_pre = dhn * hg
            dhh = rn * (dhn_pre - hn_pre * jnp.mean(dhn_pre * hn_pre, axis=1, keepdims=True))
            dnum = dhh / denom
            dden = jnp.where(jnp.abs(den) >= floor,
                             -jnp.sum(hh * dhh, axis=1, keepdims=True) / denom * jnp.sign(den), 0.0)
            dwqk = _dot(dnum, vh, NT) + dden
            dv = _dot(wqk, dnum, TN)
            dp = dwqk * q["e_mat"]
            dqs = _dot(dp, kh, NN) + e_inter * (_dot(dnum, c_prev, NT) + dden * n_prev)
            dk = _dot(dp, qs, TN)
            g_next = g_scr[h]
            gn_next = gn_scr[h:h + 1, :]
            w_state = q["e_end_c"] * q["c_inj"]
            dk_state = w_state * (_dot(vh, g_next, NT) + gn_next)
            dk = dk + dk_state
            dv = dv + w_state * _dot(kh, g_next, NN)
            dq = dqs * (HEAD_DIM ** -0.5)
            eye = q["t_i"] == q["s_i"]
            to_row = lambda col: jnp.sum(jnp.where(eye, col, 0.0), axis=0, keepdims=True)
            to_col = lambda row: jnp.sum(jnp.where(eye, row, 0.0), axis=1, keepdims=True)
            g_pair = dwqk * wqk
            rs_in = jnp.sum(g_pair, axis=1, keepdims=True)
            cs_in_r = jnp.sum(g_pair, axis=0, keepdims=True)
            rs_state = (jnp.sum(dnum * num_state, axis=1, keepdims=True) + dden * den_state)
            cs_state = jnp.sum(kh * dk_state, axis=1, keepdims=True)
            di_c = to_col(cs_in_r) + cs_state
            through = q["a_dec"] * (jnp.sum(jnp.sum(g_next * c_prev.astype(F32), axis=1, keepdims=True),
                                            axis=0, keepdims=True)
                                    + jnp.sum(gn_next * n_prev, axis=1, keepdims=True))
            ends_here = to_row(rs_in + rs_state) - cs_in_r
            da_c = (jnp.sum(jnp.where(q["s_i"] >= q["t_i"], ends_here, 0.0), axis=1, keepdims=True)
                    + jnp.sum(jnp.where(q["s_i"] < q["t_i"], to_row(cs_state), 0.0), axis=1, keepdims=True)
                    + through)
            df_c = da_c * _sigmoid(-q["fc"])
            dgate = dgate + jnp.where(lane == h, di_c, 0.0) + jnp.where(lane == HEADS + h, df_c, 0.0)
            dqkvo_ref[:, sl] = dq.astype(BF16)
            dqkvo_ref[:, D_MLSTM + h * HEAD_DIM:D_MLSTM + (h + 1) * HEAD_DIM] = dk.astype(BF16)
            dqkvo_ref[:, 2 * D_MLSTM + h * HEAD_DIM:2 * D_MLSTM + (h + 1) * HEAD_DIM] = dv.astype(BF16)
            dqkvo_ref[:, 3 * D_MLSTM + h * HEAD_DIM:3 * D_MLSTM + (h + 1) * HEAD_DIM] = d_o.astype(BF16)
            g_scr[h] = q["a_dec"] * g_next + _dot(e_inter * qs, dnum, TN)
            gn_scr[h:h + 1, :] = q["a_dec"] * gn_next + jnp.sum(e_inter * qs * dden, axis=0, keepdims=True)
        dgate_ref[...] = dgate

    rev = lambda c: nc - 1 - c
    blk = lambda j: pl.BlockSpec((L, D_MLSTM), lambda c, j=j: (rev(c), j))
    full = lambda shp: pl.BlockSpec(shp, lambda c: tuple(0 for _ in shp))
    return pl.pallas_call(
        body, grid=(nc,), name="mlstm_bwd",
        in_specs=[blk(0), blk(1), blk(2), blk(3),
                  pl.BlockSpec((L, LANES), lambda c: (rev(c), (4 * D_MLSTM + 2 * D_LRU) // LANES)),
                  pl.BlockSpec((None, 2 * HEADS, L), lambda c: (rev(c), 0, 0)),
                  full((1, LANES)), full((2 * HEADS, 1)), full((HEADS, HEAD_DIM)),
                  pl.BlockSpec((None, HEADS, HEAD_DIM, HEAD_DIM), lambda c: (rev(c), 0, 0, 0)),
                  pl.BlockSpec((None, HEADS, HEAD_DIM), lambda c: (rev(c), 0, 0)),
                  pl.BlockSpec((None, HEADS, LANES), lambda c: (rev(c), 0, 0)),
                  pl.BlockSpec((L, D_MLSTM), lambda c: (rev(c), 0))],
        out_specs=[pl.BlockSpec((L, 4 * D_MLSTM), lambda c: (rev(c), 0)),
                   pl.BlockSpec((L, LANES), lambda c: (rev(c), 0)),
                   full((HEADS, HEAD_DIM))],
        out_shape=[jax.ShapeDtypeStruct((s, _PROJ_PAD), BF16),
                   jax.ShapeDtypeStruct((s, LANES), F32),
                   jax.ShapeDtypeStruct((HEADS, HEAD_DIM), F32)],
        scratch_shapes=[pltpu.VMEM((HEADS, HEAD_DIM, HEAD_DIM), F32), pltpu.VMEM((HEADS, HEAD_DIM), F32)],
        compiler_params=_params(("arbitrary",)),
    )(proj, proj, proj, proj, proj, gates_t, bias_row, bias_col, head_g, cprev, nprev, mprev, dmix)


def _lru_gates(xc, wa_ref, wx_ref, ba, bx, lam):
    r = _sigmoid(_dot(xc, wa_ref[...], NN) + ba)
    ig = _sigmoid(_dot(xc, wx_ref[...], NN) + bx)
    sp = _softplus(-lam)
    log_a = (-LRU_C * r) * sp
    a = jnp.exp(log_a)
    mult = jnp.sqrt(_one_minus_exp(2.0 * log_a))
    return r, ig, sp, a, mult


def _lru_conv(xr, prev, w_ref, b):
    xc = b + _shift_down(xr, prev, 3) * w_ref[0:1, :]
    for j in range(1, LRU_CONV):
        xc = xc + _shift_down(xr, prev, LRU_CONV - 1 - j) * w_ref[j:j + 1, :]
    return xc


def _lru_fwd(proj, mix, conv_w, conv_b, wa, wx, ba, bx, lam, tt=512):
    s = proj.shape[0]
    tt = min(tt, s)
    nt = s // tt
    B = LRU_BLOCK_DIM
    lru_col = 4 * D_MLSTM // B
    mix_col = D_MLSTM // B

    def body(xr_ref, gr_ref, cw_ref, cb_ref, wa_ref, wx_ref, ba_ref, bx_ref, lam_ref, mix_in_ref,
             out_ref, h_ref, prev_scr, hcar_scr):
        @pl.when(pl.program_id(1) == 0)
        def _():
            prev_scr[...] = jnp.zeros_like(prev_scr)
            hcar_scr[...] = jnp.zeros_like(hcar_scr)

        xr = xr_ref[...]
        xc = _lru_conv(xr, prev_scr[...], cw_ref, cb_ref[...])
        prev_scr[...] = xr[tt - HALO:, :]
        _, ig, _, a, mult = _lru_gates(xc, wa_ref, wx_ref, ba_ref[...], bx_ref[...], lam_ref[...])
        u = mult * (ig * xc)
        rows = _rows((tt, B))
        acc_a, acc_b = a, u
        d = 1
        while d < tt:
            keep = rows >= d
            sh_a = jnp.where(keep, pltpu.roll(acc_a, d, axis=0), 1.0)
            sh_b = jnp.where(keep, pltpu.roll(acc_b, d, axis=0), 0.0)
            acc_b = acc_a * sh_b + acc_b
            acc_a = acc_a * sh_a
            d *= 2
        hv = acc_b + acc_a * hcar_scr[0:1, :]
        hcar_scr[...] = jnp.broadcast_to(hv[tt - 1:tt, :], hcar_scr.shape)
        h_ref[...] = hv
        out_ref[...] = (hv * _gelu(gr_ref[...])).astype(BF16)

    chan = lambda rws: pl.BlockSpec((rws, B), lambda n, i: (0, n))
    return pl.pallas_call(
        body, grid=(LRU_BLOCKS, nt), name="lru_fwd",
        in_specs=[pl.BlockSpec((tt, B), lambda n, i: (i, lru_col + 2 * n)),
                  pl.BlockSpec((tt, B), lambda n, i: (i, lru_col + 2 * n + 1)),
                  chan(LRU_CONV), chan(1),
                  pl.BlockSpec((None, B, B), lambda n, i: (n, 0, 0)),
                  pl.BlockSpec((None, B, B), lambda n, i: (n, 0, 0)),
                  chan(1), chan(1), chan(1), ANY],
        out_specs=[pl.BlockSpec((tt, B), lambda n, i: (i, mix_col + n)), pl.BlockSpec((tt, B), lambda n, i: (i, n))],
        out_shape=[jax.ShapeDtypeStruct(mix.shape, BF16), jax.ShapeDtypeStruct((s, D_LRU), F32)],
        scratch_shapes=[pltpu.VMEM((HALO, B), F32), pltpu.VMEM((HALO, B), F32)],
        input_output_aliases={9: 0},
        compiler_params=_params(("parallel", "arbitrary")),
    )(proj, proj, conv_w, conv_b, wa, wx, ba, bx, lam, mix)


def _lru_bwd(proj, hsave, dmix, dproj, conv_w, conv_b, wa, wx, ba, bx, lam, tt=512):
    s = proj.shape[0]
    tt = min(tt, s)
    nt = s // tt
    B = LRU_BLOCK_DIM
    lru_col = 4 * D_MLSTM // B
    dmix_col = D_MLSTM // B
    hpb = tt // HALO

    def body(xr_ref, xprev_ref, gr_ref, h_ref, hprev_ref, dmix_ref, cw_ref, cb_ref, wa_ref, wx_ref,
             ba_ref, bx_ref, lam_ref, dproj_in_ref,
             dxg_ref, dcw_ref, dcb_ref, dwa_ref, dwx_ref, dba_ref, dbx_ref, dlam_ref,
             gcar_scr, acar_scr, dxc_scr):
        i = pl.program_id(1)
        first_tile = i == nt - 1

        @pl.when(i == 0)
        def _():
            gcar_scr[...] = jnp.zeros_like(gcar_scr)
            acar_scr[...] = jnp.zeros_like(acar_scr)
            dxc_scr[...] = jnp.zeros_like(dxc_scr)
            for ref in (dcw_ref, dcb_ref, dwa_ref, dwx_ref, dba_ref, dbx_ref, dlam_ref):
                ref[...] = jnp.zeros_like(ref)

        xr = xr_ref[...]
        xprev = jnp.where(first_tile, 0.0, xprev_ref[...])
        hprev = jnp.where(first_tile, 0.0, hprev_ref[...])
        lam = lam_ref[...]
        xc = _lru_conv(xr, xprev, cw_ref, cb_ref[...])
        r, ig, sp, a, mult = _lru_gates(xc, wa_ref, wx_ref, ba_ref[...], bx_ref[...], lam)
        gr = gr_ref[...]
        hv = h_ref[...]
        dout = dmix_ref[...]
        dxg_ref[:, B:] = (dout * hv * _gelu_grad(gr)).astype(BF16)
        dh = dout * _gelu(gr)
        rows = _rows((tt, B))
        acc_a = _shift_up(a, acar_scr[...], 1)
        acc_b = dh
        d = 1
        while d < tt:
            keep = rows < tt - d
            sh_a = jnp.where(keep, pltpu.roll(acc_a, tt - d, axis=0), 1.0)
            sh_b = jnp.where(keep, pltpu.roll(acc_b, tt - d, axis=0), 0.0)
            acc_b = acc_a * sh_b + acc_b
            acc_a = acc_a * sh_a
            d *= 2
        gv = acc_b + acc_a * gcar_scr[0:1, :]
        gcar_scr[...] = jnp.broadcast_to(gv[0:1, :], gcar_scr.shape)
        acar_scr[...] = jnp.broadcast_to(a[0:1, :], acar_scr.shape)
        h_before = _shift_down(hv, hprev, 1)
        da = gv * h_before
        dmult = gv * (ig * xc)
        dig = gv * mult * xc
        dxc = gv * mult * ig
        dlog_a = da * a - dmult * (a * a) / mult
        dr = dlog_a * (-LRU_C * sp)
        dlam_ref[...] += jnp.sum(dlog_a * (-LRU_C * r), axis=0, keepdims=True) * (-_sigmoid(-lam))
        dpre_r = dr * r * (1.0 - r)
        dpre_i = dig * ig * (1.0 - ig)
        dba_ref[...] += jnp.sum(dpre_r, axis=0, keepdims=True)
        dbx_ref[...] += jnp.sum(dpre_i, axis=0, keepdims=True)
        dwa_ref[...] += _dot(xc, dpre_r, TN)
        dwx_ref[...] += _dot(xc, dpre_i, TN)
        dxc = dxc + _dot(dpre_r, wa_ref[...], NT) + _dot(dpre_i, wx_ref[...], NT)
        dcb_ref[...] += jnp.sum(dxc, axis=0, keepdims=True)
        nxt = dxc_scr[...]
        dxr = jnp.zeros((tt, B), F32)
        for j in range(LRU_CONV):
            sft = LRU_CONV - 1 - j
            dcw_ref[j:j + 1, :] += jnp.sum(dxc * _shift_down(xr, xprev, sft), axis=0, keepdims=True)
            dxr = dxr + _shift_up(dxc, nxt, sft) * cw_ref[j:j + 1, :]
        dxc_scr[...] = dxc[:HALO, :]
        dxg_ref[:, :B] = dxr.astype(BF16)

    rev = lambda i: nt - 1 - i
    tile = lambda col, step: pl.BlockSpec((tt, B), lambda n, i: (rev(i), col + step * n))
    halo = lambda col, step: pl.BlockSpec(
        (HALO, B), lambda n, i: (jnp.maximum(rev(i) * hpb - 1, 0), col + step * n))
    chan = lambda rws: pl.BlockSpec((rws, B), lambda n, i: (0, n))
    wblk = pl.BlockSpec((None, B, B), lambda n, i: (n, 0, 0))
    return pl.pallas_call(
        body, grid=(LRU_BLOCKS, nt), name="lru_bwd",
        in_specs=[tile(lru_col, 2), halo(lru_col, 2), tile(lru_col + 1, 2), tile(0, 1), halo(0, 1),
                  tile(dmix_col, 1), chan(LRU_CONV), chan(1), wblk, wblk, chan(1), chan(1), chan(1), ANY],
        out_specs=[pl.BlockSpec((tt, 2 * B), lambda n, i: (rev(i), lru_col // 2 + n)),
                   chan(LRU_CONV), chan(1), wblk, wblk, chan(1), chan(1), chan(1)],
        out_shape=[jax.ShapeDtypeStruct(dproj.shape, BF16),
                   jax.ShapeDtypeStruct((LRU_CONV, D_LRU), F32), jax.ShapeDtypeStruct((1, D_LRU), F32),
                   jax.ShapeDtypeStruct((LRU_BLOCKS, B, B), F32), jax.ShapeDtypeStruct((LRU_BLOCKS, B, B), F32),
                   jax.ShapeDtypeStruct((1, D_LRU), F32), jax.ShapeDtypeStruct((1, D_LRU), F32),
                   jax.ShapeDtypeStruct((1, D_LRU), F32)],
        scratch_shapes=[pltpu.VMEM((HALO, B), F32), pltpu.VMEM((HALO, B), F32), pltpu.VMEM((HALO, B), F32)],
        input_output_aliases={13: 0},
        compiler_params=_params(("parallel", "arbitrary")),
    )(proj, proj, proj, hsave, hsave, dmix, conv_w, conv_b, wa, wx, ba, bx, lam, dproj)


def _ffn_conv(gp, prev, w_ref, b):
    g = b + _shift_down(gp, prev, 2) * w_ref[0:1, :]
    for j in range(1, FFN_CONV):
        g = g + _shift_down(gp, prev, FFN_CONV - 1 - j) * w_ref[j:j + 1, :]
    return g


def _ffn_act_fwd(gu, conv_w, conv_b, tt=256):
    s = gu.shape[0]
    tt = min(tt, s)
    d_ff = conv_w.shape[1]
    tc = d_ff // N_CHIPS
    hpb = tt // HALO

    def body(g_ref, gprev_ref, u_ref, w_ref, b_ref, act_ref):
        prev = jnp.where(pl.program_id(0) == 0, 0.0, gprev_ref[...])
        gate = _ffn_conv(g_ref[...], prev, w_ref, b_ref[...])
        act_ref[...] = (gate * _sigmoid(gate) * u_ref[...]).astype(BF16)

    return pl.pallas_call(
        body, grid=(s // tt, N_CHIPS), name="ffn_act_fwd",
        in_specs=[pl.BlockSpec((tt, tc), lambda i, j: (i, 2 * j)),
                  pl.BlockSpec((HALO, tc), lambda i, j: (jnp.maximum(i * hpb - 1, 0), 2 * j)),
                  pl.BlockSpec((tt, tc), lambda i, j: (i, 2 * j + 1)),
                  pl.BlockSpec((FFN_CONV, tc), lambda i, j: (0, j)),
                  pl.BlockSpec((1, tc), lambda i, j: (0, j))],
        out_specs=pl.BlockSpec((tt, tc), lambda i, j: (i, j)),
        out_shape=jax.ShapeDtypeStruct((s, d_ff), BF16),
        compiler_params=_params(("parallel", "parallel")),
    )(gu, gu, gu, conv_w, conv_b)


def _ffn_act_bwd(gu, dact, conv_w, conv_b, tt=256):
    s = gu.shape[0]
    tt = min(tt, s)
    nt = s // tt
    d_ff = conv_w.shape[1]
    tc = d_ff // N_CHIPS
    hpb = tt // HALO

    def dgate_of(gate, up, da):
        sg = _sigmoid(gate)
        return da * up * (sg * (1.0 + gate * (1.0 - sg))), da * (gate * sg)

    def body(g_ref, gprev_ref, gnext_ref, u_ref, unext_ref, da_ref, danext_ref, w_ref, b_ref,
             dgu_ref, dw_ref, db_ref):
        i = pl.program_id(1)

        @pl.when(i == 0)
        def _():
            dw_ref[...] = jnp.zeros_like(dw_ref)
            db_ref[...] = jnp.zeros_like(db_ref)

        gp = g_ref[...]
        prev = jnp.where(i == 0, 0.0, gprev_ref[...])
        bias = b_ref[...]
        gate = _ffn_conv(gp, prev, w_ref, bias)
        dgate, dup = dgate_of(gate, u_ref[...], da_ref[...])
        gate_n = _ffn_conv(gnext_ref[...], gp[tt - HALO:, :], w_ref, bias)
        dgate_n, _ = dgate_of(gate_n, unext_ref[...], danext_ref[...])
        dgate_n = jnp.where(i == nt - 1, 0.0, dgate_n)
        db_ref[...] += jnp.sum(dgate, axis=0, keepdims=True)
        dgp = jnp.zeros((tt, tc), F32)
        for j in range(FFN_CONV):
            sft = FFN_CONV - 1 - j
            dw_ref[j:j + 1, :] += jnp.sum(dgate * _shift_down(gp, prev, sft), axis=0, keepdims=True)
            dgp = dgp + _shift_up(dgate, dgate_n, sft) * w_ref[j:j + 1, :]
        dgu_ref[:, :tc] = dgp.astype(BF16)
        dgu_ref[:, tc:] = dup.astype(BF16)

    tile = lambda half: pl.BlockSpec((tt, tc), lambda j, i, half=half: (i, 2 * j + half))
    hprev = lambda half: pl.BlockSpec((HALO, tc), lambda j, i, half=half: (jnp.maximum(i * hpb - 1, 0), 2 * j + half))
    hnext = lambda half: pl.BlockSpec(
        (HALO, tc), lambda j, i, half=half: (jnp.minimum((i + 1) * hpb, nt * hpb - 1), 2 * j + half))
    return pl.pallas_call(
        body, grid=(N_CHIPS, nt), name="ffn_act_bwd",
        in_specs=[tile(0), hprev(0), hnext(0), tile(1), hnext(1),
                  pl.BlockSpec((tt, tc), lambda j, i: (i, j)),
                  pl.BlockSpec((HALO, tc), lambda j, i: (jnp.minimum((i + 1) * hpb, nt * hpb - 1), j)),
                  pl.BlockSpec((FFN_CONV, tc), lambda j, i: (0, j)),
                  pl.BlockSpec((1, tc), lambda j, i: (0, j))],
        out_specs=[pl.BlockSpec((tt, 2 * tc), lambda j, i: (i, j)),
                   pl.BlockSpec((FFN_CONV, tc), lambda j, i: (0, j)),
                   pl.BlockSpec((1, tc), lambda j, i: (0, j))],
        out_shape=[jax.ShapeDtypeStruct((s, 2 * d_ff), BF16),
                   jax.ShapeDtypeStruct((FFN_CONV, d_ff), F32), jax.ShapeDtypeStruct((1, d_ff), F32)],
        compiler_params=_params(("parallel", "arbitrary")),
    )(gu, gu, gu, gu, gu, dact, dact, conv_w, conv_b)


def _gate_grads(dgate, dproj, tm=512):
    s, n = dgate.shape
    tm = min(tm, s)

    def body(a_ref, dproj_in_ref, o_ref, dproj_ref):
        @pl.when(pl.program_id(0) == 0)
        def _():
            o_ref[...] = jnp.zeros_like(o_ref)
        a = a_ref[...]
        o_ref[...] += jnp.sum(a, axis=0, keepdims=True)
        dproj_ref[...] = a.astype(BF16)

    return pl.pallas_call(
        body, grid=(s // tm,), name="gate_grads",
        in_specs=[pl.BlockSpec((tm, n), lambda i: (i, 0)), ANY],
        out_specs=[pl.BlockSpec((1, n), lambda i: (0, 0)),
                   pl.BlockSpec((tm, n), lambda i: (i, (_QKVO + 2 * D_LRU) // LANES))],
        out_shape=[jax.ShapeDtypeStruct((1, n), F32), jax.ShapeDtypeStruct(dproj.shape, BF16)],
        input_output_aliases={1: 1},
        compiler_params=_params(("arbitrary",)),
    )(dgate, dproj)


def _pick(n, *cands):
    for c in cands:
        if n % c == 0:
            return c
    raise ValueError(f"no tile for {n}")


def _behind(a, token):
    return a if token is None else a + token[0:1, 0:1].astype(a.dtype).reshape((1,) * a.ndim)


class _Gathered:
    def __init__(self, w):
        self.w = w

    def begin(self):
        return None

    def mid(self, grp, after):
        return None

    def end(self, grp, after):
        return self.w


def _local_step(x, target, w, comm):
    s, d = x.shape
    nc = s // CHUNK
    tm = _pick(s, 1024, 512, 256)
    tn_proj = _pick(_PROJ_PAD, 896)
    gate_col = 4 * D_MLSTM + 2 * D_LRU
    w = dict(w)

    token = comm.begin()
    n1, rstd1 = _rmsnorm_fwd("norm_mix_fwd", x, _behind(w["norm_mix_g"], token))
    comm.mid(0, n1)
    w.update(comm.end(0, None))
    proj = _mm_nn("proj_fwd", n1, w["w_in"], tm, tn_proj, 512)
    token = comm.mid(1, proj)
    gates = proj[:, gate_col:gate_col + 2 * HEADS]
    gates_t = gates.reshape(nc, CHUNK, 2 * HEADS).transpose(0, 2, 1)
    bias_row = _behind(jnp.pad(w["b_gate_m"], ((0, 0), (0, LANES - 2 * HEADS))), token)
    bias_col = w["b_gate_m"].reshape(2 * HEADS, 1)
    mix, cprev, nprev, mprev = _mlstm_fwd(proj, gates_t, bias_row, bias_col, w["mlstm_norm_g"])
    mix, hsave = _lru_fwd(proj, mix, w["lru_conv_w"], w["lru_conv_b"], w["lru_wa"], w["lru_wx"],
                          w["lru_ba"], w["lru_bx"], w["lru_lambda"])
    w.update(comm.end(1, hsave))
    x1 = _mm_nn("out_fwd", mix, w["w_out"], tm, 1024, 512, res=x)
    n2, rstd2 = _rmsnorm_fwd("norm_ffn_fwd", x1, w["norm_ffn_g"])
    token = comm.mid(2, n2)
    gu = _mm_up_fwd("up_fwd", n2, w["w_up"], tm, 512)
    act = _ffn_act_fwd(gu, w["ffn_conv_w"], _behind(w["ffn_conv_b"], token))
    w.update(comm.end(2, act))
    d_ff = w["w_down"].shape[0]
    x2 = _mm_nn("down_fwd", act, w["w_down"], tm, 1024, _pick(d_ff, 512), res=x1)
    loss, dx2, dx2b, g_norm_final = _loss_head("loss_head", x2, w["norm_final_g"], target)

    grads = {"norm_final_g": g_norm_final}
    dact = _mm_nt("down_bwd_x", dx2b, w["w_down"], tm, _pick(d_ff, 1408, 512), 1024)
    grads["w_down"] = _mm_tn("down_bwd_w", act, dx2b, _pick(d_ff, 1408, 512), 1024, 512)
    dgu, grads["ffn_conv_w"], grads["ffn_conv_b"] = _ffn_act_bwd(gu, dact, w["ffn_conv_w"], w["ffn_conv_b"])
    dn2 = _mm_up_bwd_x("up_bwd_x", dgu, w["w_up"], tm, 1024)
    grads["w_up"] = _mm_up_bwd_w("up_bwd_w", n2, dgu, 1024, 512)
    dx1, dx1b, grads["norm_ffn_g"] = _rmsnorm_bwd("norm_ffn_bwd", x1, rstd2, w["norm_ffn_g"], dn2, dx2)
    dmix = _mm_nt("out_bwd_x", dx1b, w["w_out"], tm, 1024, 1024)
    grads["w_out"] = _mm_tn("out_bwd_w", mix, dx1b, 1024, 1024, 512)
    dproj, dgate, grads["mlstm_norm_g"] = _mlstm_bwd(proj, gates_t, bias_row, bias_col, w["mlstm_norm_g"],
                                                     cprev, nprev, mprev, dmix)
    (dproj, grads["lru_conv_w"], grads["lru_conv_b"], grads["lru_wa"], grads["lru_wx"],
     grads["lru_ba"], grads["lru_bx"], grads["lru_lambda"]) = _lru_bwd(
        proj, hsave, dmix, dproj, w["lru_conv_w"], w["lru_conv_b"], w["lru_wa"], w["lru_wx"],
        w["lru_ba"], w["lru_bx"], w["lru_lambda"])
    gate_bias_grad, dproj = _gate_grads(dgate, dproj)
    grads["b_gate_m"] = gate_bias_grad[:, :2 * HEADS]
    dn1 = _mm_nt("proj_bwd_x", dproj, w["w_in"], tm, 1024, tn_proj)
    grads["w_in"] = _mm_tn("proj_bwd_w", n1, dproj, 1024, tn_proj, 512)
    grad_x, _, grads["norm_mix_g"] = _rmsnorm_bwd("norm_mix_bwd", x, rstd1, w["norm_mix_g"], dn1, dx1)
    return loss, grad_x, grads


WEIGHT_NAMES = ("norm_mix_g", "w_in", "b_gate_m", "mlstm_norm_g", "lru_conv_w", "lru_conv_b", "lru_wa", "lru_ba",
                "lru_wx", "lru_bx", "lru_lambda", "w_out", "norm_ffn_g", "w_up", "ffn_conv_w", "ffn_conv_b",
                "w_down", "norm_final_g")
BIG = ("w_in", "w_out", "w_up", "w_down")
SMALL_SHARDED = ("mlstm_norm_g", "lru_conv_w", "ffn_conv_w")
SMALL = tuple(n for n in WEIGHT_NAMES if n not in BIG)
SMALL_REPLICATED = tuple(n for n in SMALL if n not in SMALL_SHARDED)


def _proj_segments():
    segs = [(0, 0, _QKVO), (_QKVO, _QKVO + 2 * D_LRU, _N_GATES)]
    for n in range(LRU_BLOCKS):
        segs.append((_QKVO + _N_GATES + n * LRU_BLOCK_DIM, _QKVO + 2 * n * LRU_BLOCK_DIM, LRU_BLOCK_DIM))
        segs.append((_QKVO + _N_GATES + D_LRU + n * LRU_BLOCK_DIM, _QKVO + (2 * n + 1) * LRU_BLOCK_DIM,
                     LRU_BLOCK_DIM))
    return segs


def _w_in_shards_to_local(shards):
    width = shards.shape[2]
    pieces = []
    for g0, _, n in sorted(_proj_segments(), key=lambda s: s[1]):
        at = g0
        while at < g0 + n:
            j = at // width
            stop = min(g0 + n, (j + 1) * width)
            pieces.append(shards[j][:, at - j * width:stop - j * width])
            at = stop
    pieces.append(jnp.zeros((shards.shape[1], PROJ_GATE_PAD - _N_GATES), shards.dtype))
    return jnp.concatenate(pieces, axis=1)


def _w_in_local_to_shards(w):
    width = _PROJ_COLS // N_CHIPS
    shards = []
    for j in range(N_CHIPS):
        pieces = []
        for g0, l0, n in sorted(_proj_segments()):
            lo, hi = max(g0, j * width), min(g0 + n, (j + 1) * width)
            if lo < hi:
                pieces.append(w[:, l0 + lo - g0:l0 + hi - g0])
        shards.append(jnp.concatenate(pieces, axis=1))
    return jnp.stack(shards)


def _w_in_to_global(w):
    sh = _w_in_local_to_shards(w)
    return jnp.concatenate([sh[j] for j in range(N_CHIPS)], axis=1)


def _pack(arrs, rows):
    flat = jnp.concatenate([a.reshape(-1).astype(F32) for a in arrs])
    return jnp.pad(flat, (0, rows * LANES - flat.shape[0])).reshape(rows, LANES)


def _unpack(buf, shapes):
    flat = buf.reshape(-1)
    out, at = [], 0
    for shp in shapes:
        n = 1
        for d in shp:
            n *= d
        out.append(flat[at:at + n].reshape(shp))
        at += n
    return out


def _pack_rows(shapes):
    n = sum(functools.reduce(lambda a, b: a * b, shp, 1) for shp in shapes)
    return -(-n // (HALO * LANES)) * HALO


def _assemble_weights(g_in, g_out, g_up, g_down, small_sharded, replicated):
    w = dict(replicated)
    w["w_in"] = _w_in_shards_to_local(g_in)
    w["w_out"] = g_out.reshape(-1, g_out.shape[-1])
    w["w_up"] = g_up
    w["w_down"] = g_down.reshape(-1, g_down.shape[-1])
    for name, v in small_sharded.items():
        w[name] = jnp.concatenate([v[j] for j in range(N_CHIPS)], axis=1)
    return w


def _full_weights_from_global(weights):
    shard = lambda a, axis: jnp.stack(jnp.split(a, N_CHIPS, axis=axis))
    rep = {n: weights[n].reshape(1, -1) if weights[n].ndim <= 2 and n != "b_gate_m" else weights[n]
           for n in SMALL_REPLICATED}
    rep["b_gate_m"] = weights["b_gate_m"].reshape(1, -1)
    return _assemble_weights(shard(weights["w_in"], 1).astype(BF16), shard(weights["w_out"], 0).astype(BF16),
                             shard(weights["w_up"], 1).astype(BF16), shard(weights["w_down"], 0).astype(BF16),
                             {n: shard(weights[n], 1) for n in SMALL_SHARDED}, rep)


def _grads_to_global(grads):
    g = dict(grads)
    g["w_in"] = _w_in_to_global(grads["w_in"])
    g["w_up"] = jnp.concatenate([grads["w_up"][j] for j in range(N_CHIPS)], axis=1)
    return g


def _place():
    x, y, c = lax.axis_index("x"), lax.axis_index("y"), lax.axis_index("c")
    chips = [(1 - x, y), (x, 1 - y), (1 - x, 1 - y)]
    return x, y, c, 2 * x + y, chips


def _half_rows(n_rows, which):
    half = n_rows // 2
    return pl.ds(pl.multiple_of(which * half, 16), half)


def _rcopy(src, dst, send_sem, recv_sem, to):
    return pltpu.make_async_remote_copy(src_ref=src, dst_ref=dst, send_sem=send_sem, recv_sem=recv_sem,
                                        device_id=to, device_id_type=MESH)


HBM_SPEC = pl.BlockSpec(memory_space=pltpu.HBM)
SEM_SPEC = pl.BlockSpec(memory_space=pltpu.SEMAPHORE)
TOKEN_SHAPE = (8, LANES)


def _split_call(name, bufs, sems_in, sems_out_shapes, body_fn, after=None):
    nb, ni, no = len(bufs), len(sems_in), len(sems_out_shapes)

    def body(*refs):
        buf_refs = refs[:nb]
        sem_in_refs = refs[nb:nb + ni]
        outs = refs[nb + ni + (0 if after is None else 1):]
        sem_out_refs = outs[:no]
        token_ref = outs[no + nb]
        body_fn(buf_refs, sem_in_refs, sem_out_refs)
        token_ref[...] = jnp.zeros_like(token_ref)

    out_shape = ([pltpu.SemaphoreType.DMA(shp) for shp in sems_out_shapes]
                 + [pltpu.HBM(b.shape, b.dtype) for b in bufs] + [jax.ShapeDtypeStruct(TOKEN_SHAPE, F32)])
    res = pl.pallas_call(
        body, name=name, out_shape=out_shape,
        in_specs=[HBM_SPEC] * nb + [SEM_SPEC] * ni + ([] if after is None else [ANY]),
        out_specs=[SEM_SPEC] * no + [HBM_SPEC] * nb + [pl.BlockSpec(memory_space=pltpu.VMEM)],
        input_output_aliases={i: no + i for i in range(nb)},
        compiler_params=pltpu.CompilerParams(has_side_effects=pltpu.SideEffectType.DATAFLOW_SIDE_EFFECTING),
    )(*[pltpu.with_memory_space_constraint(b, pltpu.HBM) for b in bufs], *sems_in,
      *(() if after is None else (after,)))
    return list(res[:no]), list(res[no:no + nb]), res[no + nb]


def _place_own_shard(name, idx, shard):
    rows, cols = shard.shape
    tr = _row_tile(rows)

    def body(idx_ref, s_ref, o_ref):
        o_ref[...] = s_ref[...].astype(BF16)

    return pl.pallas_call(
        body, name=name, out_shape=jax.ShapeDtypeStruct((N_CHIPS, rows, cols), BF16),
        grid_spec=pltpu.PrefetchScalarGridSpec(
            num_scalar_prefetch=1, grid=(rows // tr,),
            in_specs=[pl.BlockSpec((tr, cols), lambda i, s: (i, 0))],
            out_specs=pl.BlockSpec((None, tr, cols), lambda i, s: (s[1], i, 0))),
        compiler_params=_params(("parallel",)),
    )(idx, shard)


GATHER_GROUPS = ((0, 4), (1, 2), (3,))


def _gather_start(lands):
    order = [w for grp in GATHER_GROUPS for w in grp]

    def starts(bufs, _, sems):
        x, y, c, me, chips = _place()
        for w in order:
            grp = [g for g, members in enumerate(GATHER_GROUPS) if w in members][0]
            pos = GATHER_GROUPS[grp].index(w)
            part = bufs[w].at[me] if w == 4 else bufs[w].at[me, _half_rows(bufs[w].shape[1], c)]
            for k, chip in enumerate(chips):
                _rcopy(part, part, sems[2 * grp].at[3 * pos + k], sems[2 * grp + 1].at[3 * pos + k],
                       (*chip, c)).start()

    shapes = []
    for members in GATHER_GROUPS:
        shapes += [(3 * len(members),), (3 * len(members),)]
    sems, lands, token = _split_call("gather_start", lands, [], shapes, starts)
    return [(sems[2 * g], sems[2 * g + 1]) for g in range(len(GATHER_GROUPS))], lands, token


def _gather_mid(grp, lands, sems, after):
    members = GATHER_GROUPS[grp]
    big = [w for w in members if w != 4]

    def mid(bufs, sems_in, sems_out):
        x, y, c, me, chips = _place()
        send_sems, recv_sems = sems_in
        for pos, w in enumerate(members):
            for k, chip in enumerate(chips):
                cid = 2 * chip[0] + chip[1]
                buf = bufs[pos]
                mine = buf.at[me] if w == 4 else buf.at[me, _half_rows(buf.shape[1], c)]
                theirs = buf.at[cid] if w == 4 else buf.at[cid, _half_rows(buf.shape[1], c)]
                arrival = _rcopy(mine, theirs, send_sems.at[3 * pos + k], recv_sems.at[3 * pos + k], (*chip, c))
                arrival.wait_recv()
                arrival.wait_send()
                if w != 4:
                    _rcopy(theirs, theirs, sems_out[0].at[3 * big.index(w) + k],
                           sems_out[1].at[3 * big.index(w) + k], (x, y, 1 - c)).start()

    new_sems, bufs, token = _split_call(f"gather_mid_{grp}", [lands[w] for w in members], list(sems),
                                        [(3 * len(big),), (3 * len(big),)], mid, after=after)
    return new_sems, bufs, token


def _gather_end(grp, bufs, sems, after):
    members = GATHER_GROUPS[grp]
    big = [w for w in members if w != 4]

    def end(refs, sems_in, _):
        x, y, c, me, chips = _place()
        send_sems, recv_sems = sems_in
        for pos, w in enumerate(members):
            if w == 4:
                continue
            for k, chip in enumerate(chips):
                cid = 2 * chip[0] + chip[1]
                buf = refs[pos]
                sent = buf.at[cid, _half_rows(buf.shape[1], c)]
                landed = buf.at[cid, _half_rows(buf.shape[1], 1 - c)]
                fwd = _rcopy(sent, landed, send_sems.at[3 * big.index(w) + k], recv_sems.at[3 * big.index(w) + k],
                             (x, y, 1 - c))
                fwd.wait_recv()
                fwd.wait_send()

    _, bufs, token = _split_call(f"gather_end_{grp}", bufs, list(sems), [], end, after=after)
    return bufs, token


def _gather_weights(shards, small):
    nb = len(shards)

    def body(*refs):
        srcs, small_ref = refs[:nb], refs[nb]
        dsts, small_out = refs[nb + 1:2 * nb + 1], refs[2 * nb + 1]
        send_sems, recv_sems, local_sems = refs[2 * nb + 2:]
        x, y, c, me, chips = _place()
        sibling = (x, y, 1 - c)
        mine = [_half_rows(s.shape[0], c) for s in srcs]
        other = [_half_rows(s.shape[0], 1 - c) for s in srcs]

        local = [pltpu.make_async_copy(srcs[w], dsts[w].at[me], local_sems.at[w]) for w in range(nb)]
        local.append(pltpu.make_async_copy(small_ref, small_out.at[me], local_sems.at[nb]))
        for cp in local:
            cp.start()
        sends = []
        for w in range(nb):
            for k, chip in enumerate(chips):
                sends.append(_rcopy(srcs[w].at[mine[w]], dsts[w].at[me, mine[w]],
                                    send_sems.at[w, k], recv_sems.at[w, k], (*chip, c)))
        for k, chip in enumerate(chips):
            sends.append(_rcopy(small_ref, small_out.at[me], send_sems.at[nb, k], recv_sems.at[nb, k], (*chip, c)))
        for cp in sends:
            cp.start()
        passed = []
        for w in range(nb):
            for k, chip in enumerate(chips):
                cid = 2 * chip[0] + chip[1]
                landed = dsts[w].at[cid, mine[w]]
                _rcopy(landed, landed, send_sems.at[w, k], recv_sems.at[w, k], (*chip, c)).wait_recv()
                fwd = _rcopy(landed, landed, send_sems.at[w, 3 + k], recv_sems.at[w, 3 + k], sibling)
                fwd.start()
                passed.append(fwd)
        for k, chip in enumerate(chips):
            cid = 2 * chip[0] + chip[1]
            _rcopy(small_ref, small_out.at[cid], send_sems.at[nb, k], recv_sems.at[nb, k], (*chip, c)).wait_recv()
        for w in range(nb):
            for k, chip in enumerate(chips):
                cid = 2 * chip[0] + chip[1]
                landed = dsts[w].at[cid, other[w]]
                _rcopy(landed, landed, send_sems.at[w, 3 + k], recv_sems.at[w, 3 + k], sibling).wait_recv()
        for cp in sends + passed:
            cp.wait_send()
        for cp in local:
            cp.wait()

    out_shape = [jax.ShapeDtypeStruct((N_CHIPS,) + s.shape, s.dtype) for s in shards]
    out_shape.append(jax.ShapeDtypeStruct((N_CHIPS,) + small.shape, small.dtype))
    return pl.pallas_call(
        body, name="gather_weights", out_shape=out_shape,
        in_specs=[ANY] * (nb + 1), out_specs=[ANY] * (nb + 1),
        scratch_shapes=[pltpu.SemaphoreType.DMA((nb + 1, 6)), pltpu.SemaphoreType.DMA((nb + 1, 6)),
                        pltpu.SemaphoreType.DMA((nb + 1,))],
    )(*shards, small)


def _pair_exchange(grads, small):
    nb = len(grads)

    def body(*refs):
        srcs, small_ref = refs[:nb], refs[nb]
        dsts, small_out = refs[nb + 1:2 * nb + 1], refs[2 * nb + 1]
        send_sems, recv_sems, small_send, small_recv, local_sem = refs[2 * nb + 2:]
        x, y, c, _, _ = _place()
        sibling = (x, y, 1 - c)
        my_id = 4 * x + 2 * y + c
        local = pltpu.make_async_copy(small_ref, small_out.at[my_id], local_sem)
        local.start()
        sends = []
        for w in range(nb):
            other = _half_rows(srcs[w].shape[1], 1 - c)
            sends.append(_rcopy(srcs[w].at[:, other], dsts[w], send_sems.at[w], recv_sems.at[w], sibling))
        for r in range(1, N_DEV):
            to = (1 - x if r & 4 else x, 1 - y if r & 2 else y, 1 - c if r & 1 else c)
            sends.append(_rcopy(small_ref, small_out.at[my_id], small_send.at[r - 1], small_recv.at[r - 1], to))
        for cp in sends:
            cp.start()
        for w in range(nb):
            _rcopy(dsts[w], dsts[w], send_sems.at[w], recv_sems.at[w], sibling).wait_recv()
        for r in range(1, N_DEV):
            frm = (1 - x if r & 4 else x, 1 - y if r & 2 else y, 1 - c if r & 1 else c)
            frm_id = 4 * frm[0] + 2 * frm[1] + frm[2]
            _rcopy(small_ref, small_out.at[frm_id], small_send.at[r - 1], small_recv.at[r - 1], frm).wait_recv()
        for cp in sends:
            cp.wait_send()
        local.wait()

    out_shape = [jax.ShapeDtypeStruct((g.shape[0], g.shape[1] // 2, g.shape[2]), g.dtype) for g in grads]
    out_shape.append(jax.ShapeDtypeStruct((N_DEV,) + small.shape, small.dtype))
    return pl.pallas_call(
        body, name="pair_exchange", out_shape=out_shape,
        in_specs=[ANY] * (nb + 1), out_specs=[ANY] * (nb + 1),
        scratch_shapes=[pltpu.SemaphoreType.DMA((nb,)), pltpu.SemaphoreType.DMA((nb,)),
                        pltpu.SemaphoreType.DMA((N_DEV - 1,)), pltpu.SemaphoreType.DMA((N_DEV - 1,)),
                        pltpu.SemaphoreType.DMA(())],
    )(*grads, small)


def _chip_exchange(partials):
    nb = len(partials)

    def body(*refs):
        srcs, dsts = refs[:nb], refs[nb:2 * nb]
        send_sems, recv_sems = refs[2 * nb:]
        _, _, c, me, chips = _place()
        sends = []
        for w in range(nb):
            for k, chip in enumerate(chips):
                cid = 2 * chip[0] + chip[1]
                sends.append(_rcopy(srcs[w].at[cid], dsts[w].at[me], send_sems.at[w, k], recv_sems.at[w, k],
                                    (*chip, c)))
        for cp in sends:
            cp.start()
        for w in range(nb):
            for k, chip in enumerate(chips):
                cid = 2 * chip[0] + chip[1]
                _rcopy(srcs[w].at[cid], dsts[w].at[cid], send_sems.at[w, k], recv_sems.at[w, k],
                       (*chip, c)).wait_recv()
        for cp in sends:
            cp.wait_send()

    return pl.pallas_call(
        body, name="chip_exchange", out_shape=[jax.ShapeDtypeStruct(p.shape, p.dtype) for p in partials],
        in_specs=[ANY] * nb, out_specs=[ANY] * nb,
        scratch_shapes=[pltpu.SemaphoreType.DMA((nb, 3)), pltpu.SemaphoreType.DMA((nb, 3))],
    )(*partials)


def _pair_share(shards):
    nb = len(shards)

    def body(*refs):
        srcs, dsts = refs[:nb], refs[nb:2 * nb]
        send_sems, recv_sems = refs[2 * nb:]
        x, y, c, _, _ = _place()
        sibling = (x, y, 1 - c)
        sends = []
        for w in range(nb):
            mine = _half_rows(dsts[w].shape[0], c)
            sends.append(_rcopy(srcs[w].at[mine], dsts[w].at[mine], send_sems.at[w], recv_sems.at[w], sibling))
        for cp in sends:
            cp.start()
        for w in range(nb):
            other = _half_rows(dsts[w].shape[0], 1 - c)
            _rcopy(srcs[w].at[other], dsts[w].at[other], send_sems.at[w], recv_sems.at[w], sibling).wait_recv()
        for cp in sends:
            cp.wait_send()

    return pl.pallas_call(
        body, name="pair_share", out_shape=[jax.ShapeDtypeStruct(h.shape, h.dtype) for h in shards],
        in_specs=[ANY] * nb, out_specs=[ANY] * nb,
        scratch_shapes=[pltpu.SemaphoreType.DMA((nb,)), pltpu.SemaphoreType.DMA((nb,))],
        input_output_aliases={w: w for w in range(nb)},
    )(*shards)


def _row_tile(rows):
    return _pick(rows, 128, 64, 16, 8)


def _pair_sum(name, idx, grad, recv):
    n, half, cols = recv.shape
    tr = _row_tile(half)
    nrb = half // tr

    def body(idx_ref, g_ref, r_ref, o_ref):
        o_ref[...] = (g_ref[...] + r_ref[...]).astype(BF16)

    return pl.pallas_call(
        body, name=name, out_shape=jax.ShapeDtypeStruct(recv.shape, BF16),
        grid_spec=pltpu.PrefetchScalarGridSpec(
            num_scalar_prefetch=1, grid=(n, nrb),
            in_specs=[pl.BlockSpec((None, tr, cols), lambda j, i, s: (j, s[0] * nrb + i, 0)),
                      pl.BlockSpec((None, tr, cols), lambda j, i, s: (j, i, 0))],
            out_specs=pl.BlockSpec((None, tr, cols), lambda j, i, s: (j, i, 0))),
        compiler_params=_params(("parallel", "parallel")),
    )(idx, grad, recv)


def _final_sum(name, idx, grad, recv, chip_sums):
    _, half, cols = recv.shape
    tr = _row_tile(half)
    nrb = half // tr

    def body(idx_ref, g_ref, r_ref, p1_ref, p2_ref, p3_ref, o_ref):
        acc = g_ref[...] + r_ref[...]
        for p_ref in (p1_ref, p2_ref, p3_ref):
            acc = acc + p_ref[...].astype(F32)
        o_ref[...] = acc

    slot = lambda which: pl.BlockSpec((None, tr, cols), lambda i, s, which=which: (s[which], i, 0))
    return pl.pallas_call(
        body, name=name, out_shape=jax.ShapeDtypeStruct((2 * half, cols), F32),
        grid_spec=pltpu.PrefetchScalarGridSpec(
            num_scalar_prefetch=1, grid=(nrb,),
            in_specs=[pl.BlockSpec((None, tr, cols), lambda i, s: (s[1], s[0] * nrb + i, 0)),
                      slot(1), slot(2), slot(3), slot(4)],
            out_specs=pl.BlockSpec((tr, cols), lambda i, s: (s[0] * nrb + i, 0))),
        compiler_params=_params(("parallel",)),
    )(idx, grad, recv, chip_sums, chip_sums, chip_sums)


def _small_sum(packs):
    n, rows, _ = packs.shape

    def body(p_ref, o_ref):
        acc = p_ref[0]
        for k in range(1, n):
            acc = acc + p_ref[k]
        o_ref[...] = acc

    return pl.pallas_call(
        body, name="small_sum", out_shape=jax.ShapeDtypeStruct((rows, LANES), F32),
        in_specs=[pl.BlockSpec(memory_space=pltpu.VMEM)], out_specs=pl.BlockSpec(memory_space=pltpu.VMEM),
        compiler_params=pltpu.CompilerParams(vmem_limit_bytes=VMEM_LIMIT),
    )(packs)


def _adamw(name, w, g, m, v):
    rows, cols = w.shape
    tr = rows if rows * cols * 4 <= (2 << 20) else _row_tile(rows)

    def body(w_ref, g_ref, m_ref, v_ref, g_out_ref, d_ref, nm_ref, nv_ref):
        gv = g_ref[...]
        g_out_ref[...] = gv
        m_new = ADAM_B1 * m_ref[...] + (1.0 - ADAM_B1) * gv
        v_new = ADAM_B2 * v_ref[...] + (1.0 - ADAM_B2) * (gv * gv)
        m_hat = m_new / (1.0 - ADAM_B1 ** ADAM_STEP)
        v_hat = v_new / (1.0 - ADAM_B2 ** ADAM_STEP)
        d_ref[...] = -ADAM_LR * (m_hat / (jnp.sqrt(v_hat) + ADAM_EPS) + ADAM_WD * w_ref[...])
        nm_ref[...] = m_new
        nv_ref[...] = v_new

    blk = pl.BlockSpec((tr, cols), lambda i: (i, 0))
    sds = jax.ShapeDtypeStruct((rows, cols), F32)
    return pl.pallas_call(
        body, name=name, grid=(rows // tr,), in_specs=[blk] * 4, out_specs=[blk] * 4, out_shape=[sds] * 4,
        compiler_params=_params(("parallel",)),
    )(w, g, m, v)


def _train_step(x, target, W, M, V):
    xi, yi, ci = lax.axis_index("x"), lax.axis_index("y"), lax.axis_index("c")
    me = 2 * xi + yi
    big = {n: W[n][0] for n in BIG}

    others = [jnp.where(jnp.int32(i) >= me, i + 1, i) for i in range(N_CHIPS - 1)]
    idx = jnp.stack([ci, me] + others).astype(jnp.int32)

    sharded_shapes = [W[n].shape[1:] for n in SMALL_SHARDED]
    small_pack = _pack([W[n][0] for n in SMALL_SHARDED], _pack_rows(sharded_shapes))
    lands = [_place_own_shard(f"place_{n}", idx, big[n]) for n in BIG]
    lands.append(lax.dynamic_update_slice(jnp.zeros((N_CHIPS,) + small_pack.shape, F32), small_pack[None],
                                          (me, 0, 0)))
    replicated = {n: (W[n].reshape(1, -1) if W[n].ndim <= 2 else W[n][0]) for n in SMALL_REPLICATED}

    class _SplitGather:
        def begin(self):
            self.sems, self.lands, token = _gather_start(lands)
            return token

        def mid(self, grp, after):
            self.pending = _gather_mid(grp, self.lands, self.sems[grp], after)
            return self.pending[2]

        def end(self, grp, after):
            sems, bufs, _ = self.pending
            bufs, _ = _gather_end(grp, bufs, sems, after)
            if grp == 0:
                per_chip = [_unpack(bufs[1][j], sharded_shapes) for j in range(N_CHIPS)]
                out = {n: jnp.concatenate([per_chip[j][i] for j in range(N_CHIPS)], axis=1)
                       for i, n in enumerate(SMALL_SHARDED)}
                out["w_in"] = _w_in_shards_to_local(bufs[0])
                return out
            if grp == 1:
                return {"w_out": bufs[0].reshape(-1, bufs[0].shape[-1]), "w_up": bufs[1]}
            return {"w_down": bufs[0].reshape(-1, bufs[0].shape[-1])}

    loss, grad_x, grads = _local_step(x[0], target[0], replicated, _SplitGather())
    loss = lax.psum(loss[0, 0], ("x", "y", "c"))

    shard_major = [_w_in_local_to_shards(grads["w_in"]),
                   grads["w_out"].reshape((N_CHIPS, -1) + grads["w_out"].shape[1:]),
                   grads["w_up"],
                   grads["w_down"].reshape((N_CHIPS, -1) + grads["w_down"].shape[1:])]
    small_shapes = [W[n].shape[1:] if n not in SMALL_SHARDED else (W[n].shape[1], W[n].shape[2] * N_CHIPS)
                    for n in SMALL]
    small_shapes = [tuple(W[n].shape) if n == "norm_final_g" else shp for n, shp in zip(SMALL, small_shapes)]
    small_pack = _pack([grads[n] for n in SMALL], _pack_rows(small_shapes))
    *recv, small_all = _pair_exchange(shard_major, small_pack)
    partial =[_pair_sum(f"pair_sum_{n}", idx, g, r) for n, g, r in zip(BIG, shard_major, recv)]
    chip_sums = _chip_exchange(partial)
    halves = [_final_sum(f"final_sum_{n}", idx, g, r, p) for n, g, r, p in zip(BIG, shard_major, recv, chip_sums)]
    big_grads = dict(zip(BIG, _pair_share(halves)))

    small_grads = dict(zip(SMALL, _unpack(_small_sum(small_all), small_shapes)))
    for n in SMALL_SHARDED:
        width = W[n].shape[2]
        small_grads[n] = lax.dynamic_slice_in_dim(small_grads[n], me * width, width, axis=1)

    out_g, out_d, out_m, out_v = {}, {}, {}, {}
    for n in BIG:
        g, d, nm, nv = _adamw(f"adamw_{n}", big[n], big_grads[n], M[n][0], V[n][0])
        out_g[n], out_d[n], out_m[n], out_v[n] = g[None], d[None], nm[None], nv[None]
    local_shapes = [tuple(W[n].shape) for n in SMALL]
    rows = _pack_rows(local_shapes)
    packed = [_pack([src[n] for n in SMALL], rows) for src in (W, small_grads, M, V)]
    _, d, nm, nv = _adamw("adamw_small", *packed)
    for dst, buf in ((out_d, d), (out_m, nm), (out_v, nv)):
        dst.update(zip(SMALL, _unpack(buf, local_shapes)))
    for n in SMALL:
        out_g[n] = small_grads[n].reshape(W[n].shape)
    return (loss, grad_x[None], *[out_g[n] for n in WEIGHT_NAMES], *[out_d[n] for n in WEIGHT_NAMES],
            *[out_m[n] for n in WEIGHT_NAMES], *[out_v[n] for n in WEIGHT_NAMES])


def kernel(x, norm_mix_g, w_in, b_gate_m, mlstm_norm_g, lru_conv_w, lru_conv_b, lru_wa, lru_ba, lru_wx, lru_bx, lru_lambda, w_out, norm_ffn_g, w_up, ffn_conv_w, ffn_conv_b, w_down, norm_final_g, loss_target, m_norm_mix_g, m_w_in, m_b_gate_m, m_mlstm_norm_g, m_lru_conv_w, m_lru_conv_b, m_lru_wa, m_lru_ba, m_lru_wx, m_lru_bx, m_lru_lambda, m_w_out, m_norm_ffn_g, m_w_up, m_ffn_conv_w, m_ffn_conv_b, m_w_down, m_norm_final_g, v_norm_mix_g, v_w_in, v_b_gate_m, v_mlstm_norm_g, v_lru_conv_w, v_lru_conv_b, v_lru_wa, v_lru_ba, v_lru_wx, v_lru_bx, v_lru_lambda, v_w_out, v_norm_ffn_g, v_w_up, v_ffn_conv_w, v_ffn_conv_b, v_w_down, v_norm_final_g):
    W = dict(zip(WEIGHT_NAMES, (norm_mix_g, w_in, b_gate_m, mlstm_norm_g, lru_conv_w, lru_conv_b, lru_wa, lru_ba,
                                lru_wx, lru_bx, lru_lambda, w_out, norm_ffn_g, w_up, ffn_conv_w, ffn_conv_b,
                                w_down, norm_final_g)))
    M = dict(zip(WEIGHT_NAMES, (m_norm_mix_g, m_w_in, m_b_gate_m, m_mlstm_norm_g, m_lru_conv_w, m_lru_conv_b,
                                m_lru_wa, m_lru_ba, m_lru_wx, m_lru_bx, m_lru_lambda, m_w_out, m_norm_ffn_g,
                                m_w_up, m_ffn_conv_w, m_ffn_conv_b, m_w_down, m_norm_final_g)))
    V = dict(zip(WEIGHT_NAMES, (v_norm_mix_g, v_w_in, v_b_gate_m, v_mlstm_norm_g, v_lru_conv_w, v_lru_conv_b,
                                v_lru_wa, v_lru_ba, v_lru_wx, v_lru_bx, v_lru_lambda, v_w_out, v_norm_ffn_g,
                                v_w_up, v_ffn_conv_w, v_ffn_conv_b, v_w_down, v_norm_final_g)))
    return _train_step(x, loss_target, W, M, V)
```

```python
import functools

import jax
import jax.numpy as jnp
from jax import lax
from jax.experimental import pallas as pl
from jax.experimental.pallas import tpu as pltpu

F32 = jnp.float32
BF16 = jnp.bfloat16
MESH = pl.DeviceIdType.MESH

EPS = 1e-6
CHUNK = 64
HEADS = 4
HEAD_DIM = 256
D_MLSTM = HEADS * HEAD_DIM
LRU_BLOCKS = 8
LRU_BLOCK_DIM = 128
D_LRU = LRU_BLOCKS * LRU_BLOCK_DIM
LRU_C = 8.0
LRU_CONV = 4
FFN_CONV = 3
ADAM_LR = 0.001
ADAM_B1 = 0.9
ADAM_B2 = 0.999
ADAM_EPS = 1e-08
ADAM_WD = 0.01
ADAM_STEP = 10

N_CHIPS = 4
N_DEV = 8
LANES = 128
HALO = 8
PROJ_GATE_PAD = LANES
_QKVO = 4 * D_MLSTM
_N_GATES = 2 * HEADS
_PROJ_COLS = _QKVO + _N_GATES + 2 * D_LRU
_PROJ_PAD = _QKVO + 2 * D_LRU + PROJ_GATE_PAD
VMEM_LIMIT = 48 * 1024 * 1024
ANY = pl.BlockSpec(memory_space=pl.ANY)


def _params(sem, vmem=VMEM_LIMIT):
    return pltpu.CompilerParams(dimension_semantics=sem, vmem_limit_bytes=vmem)


def _matmul(name, a, b, grid, a_spec, b_spec, o_spec, out_sds, contract, res=None, res_spec=None):
    nk = grid[2]
    acc_shape = tuple(d for d in o_spec.block_shape if d is not None)

    def body(*refs):
        if res is None:
            a_ref, b_ref, o_ref, acc_ref = refs
            r_ref = None
        else:
            a_ref, b_ref, r_ref, o_ref, acc_ref = refs
        k = pl.program_id(2)

        @pl.when(k == 0)
        def _():
            acc_ref[...] = jnp.zeros_like(acc_ref)

        acc_ref[...] += lax.dot_general(a_ref[...], b_ref[...], (contract, ((), ())),
                                        preferred_element_type=F32)

        @pl.when(k == nk - 1)
        def _():
            r = acc_ref[...]
            if r_ref is not None:
                r = r_ref[...] + r
            o_ref[...] = r.astype(o_ref.dtype)

    in_specs = [a_spec, b_spec] + ([] if res is None else [res_spec])
    args = (a, b) + (() if res is None else (res,))
    return pl.pallas_call(
        body, out_shape=out_sds, grid=grid, in_specs=in_specs, out_specs=o_spec,
        scratch_shapes=[pltpu.VMEM(acc_shape, F32)], name=name,
        compiler_params=_params(("parallel", "parallel", "arbitrary")),
    )(*args)


NN = ((1,), (0,))
NT = ((1,), (1,))
TN = ((0,), (0,))


def _mm_nn(name, a, b, tm, tn, tk, out_dtype=F32, res=None):
    m, k = a.shape
    n = b.shape[1]
    return _matmul(name, a, b, (m // tm, n // tn, k // tk),
                   pl.BlockSpec((tm, tk), lambda i, j, kk: (i, kk)),
                   pl.BlockSpec((tk, tn), lambda i, j, kk: (kk, j)),
                   pl.BlockSpec((tm, tn), lambda i, j, kk: (i, j)),
                   jax.ShapeDtypeStruct((m, n), out_dtype), NN,
                   res=res, res_spec=pl.BlockSpec((tm, tn), lambda i, j, kk: (i, j)))


def _mm_nt(name, a, b, tm, tn, tk, out_dtype=F32, res=None):
    m, k = a.shape
    n = b.shape[0]
    return _matmul(name, a, b, (m // tm, n // tn, k // tk),
                   pl.BlockSpec((tm, tk), lambda i, j, kk: (i, kk)),
                   pl.BlockSpec((tn, tk), lambda i, j, kk: (j, kk)),
                   pl.BlockSpec((tm, tn), lambda i, j, kk: (i, j)),
                   jax.ShapeDtypeStruct((m, n), out_dtype), NT,
                   res=res, res_spec=pl.BlockSpec((tm, tn), lambda i, j, kk: (i, j)))


def _mm_tn(name, a, b, tm, tn, tk, out_dtype=F32):
    k, m = a.shape
    n = b.shape[1]
    tk = min(tk, k)
    return _matmul(name, a, b, (m // tm, n // tn, k // tk),
                   pl.BlockSpec((tk, tm), lambda i, j, kk: (kk, i)),
                   pl.BlockSpec((tk, tn), lambda i, j, kk: (kk, j)),
                   pl.BlockSpec((tm, tn), lambda i, j, kk: (i, j)),
                   jax.ShapeDtypeStruct((m, n), out_dtype), TN)


def _up_shard(n):
    return 2 * (n % 2) + (n // 2) // 2, (n // 2) % 2


def _mm_up_fwd(name, a, wg_up, tm, tk):
    m, k = a.shape
    _, _, cols = wg_up.shape
    tn = cols // 2
    return _matmul(name, a, wg_up, (m // tm, 2 * N_CHIPS, k // tk),
                   pl.BlockSpec((tm, tk), lambda i, j, kk: (i, kk)),
                   pl.BlockSpec((None, tk, tn), lambda i, j, kk: (_up_shard(j)[0], kk, _up_shard(j)[1])),
                   pl.BlockSpec((tm, tn), lambda i, j, kk: (i, j)),
                   jax.ShapeDtypeStruct((m, 2 * N_CHIPS * tn), F32), NN)


def _mm_up_bwd_x(name, dgu, wg_up, tm, tn):
    m, _ = dgu.shape
    _, d, cols = wg_up.shape
    tk = cols // 2
    return _matmul(name, dgu, wg_up, (m // tm, d // tn, 2 * N_CHIPS),
                   pl.BlockSpec((tm, tk), lambda i, j, kk: (i, kk)),
                   pl.BlockSpec((None, tn, tk), lambda i, j, kk: (_up_shard(kk)[0], j, _up_shard(kk)[1])),
                   pl.BlockSpec((tm, tn), lambda i, j, kk: (i, j)),
                   jax.ShapeDtypeStruct((m, d), F32), NT)


def _mm_up_bwd_w(name, n2, dgu, tm, tk):
    s, d = n2.shape
    tk = min(tk, s)
    tn = dgu.shape[1] // (2 * N_CHIPS)
    return _matmul(name, n2, dgu, (d // tm, 2 * N_CHIPS, s // tk),
                   pl.BlockSpec((tk, tm), lambda i, j, kk: (kk, i)),
                   pl.BlockSpec((tk, tn), lambda i, j, kk: (kk, j)),
                   pl.BlockSpec((None, tm, tn), lambda i, j, kk: (_up_shard(j)[0], i, _up_shard(j)[1])),
                   jax.ShapeDtypeStruct((N_CHIPS, d, 2 * tn), F32), TN)


def _rmsnorm_fwd(name, x, g, tm=256):
    s, d = x.shape

    def body(x_ref, g_ref, n_ref, r_ref):
        xf = x_ref[...]
        r = lax.rsqrt(jnp.mean(xf * xf, axis=-1, keepdims=True) + EPS)
        n_ref[...] = ((xf * r) * g_ref[...]).astype(BF16)
        r_ref[...] = r

    return pl.pallas_call(
        body, grid=(s // tm,), name=name,
        in_specs=[pl.BlockSpec((tm, d), lambda i: (i, 0)), pl.BlockSpec((1, d), lambda i: (0, 0))],
        out_specs=[pl.BlockSpec((tm, d), lambda i: (i, 0)), pl.BlockSpec((tm, 1), lambda i: (i, 0))],
        out_shape=[jax.ShapeDtypeStruct((s, d), BF16), jax.ShapeDtypeStruct((s, 1), F32)],
        compiler_params=_params(("parallel",)),
    )(x, g)


def _rmsnorm_bwd(name, x, rstd, g, dn, dres, tm=256):
    s, d = x.shape

    def body(x_ref, r_ref, g_ref, dn_ref, dres_ref, dx_ref, dxb_ref, dg_ref):
        @pl.when(pl.program_id(0) == 0)
        def _():
            dg_ref[...] = jnp.zeros_like(dg_ref)

        r = r_ref[...]
        xhat = x_ref[...] * r
        dn_v = dn_ref[...]
        dxhat = dn_v * g_ref[...]
        dx = dres_ref[...] + r * (dxhat - xhat * jnp.mean(dxhat * xhat, axis=-1, keepdims=True))
        dx_ref[...] = dx
        dxb_ref[...] = dx.astype(BF16)
        dg_ref[...] += jnp.sum(dn_v * xhat, axis=0, keepdims=True)

    row = pl.BlockSpec((tm, d), lambda i: (i, 0))
    vec = pl.BlockSpec((1, d), lambda i: (0, 0))
    return pl.pallas_call(
        body, grid=(s // tm,), name=name,
        in_specs=[row, pl.BlockSpec((tm, 1), lambda i: (i, 0)), vec, row, row],
        out_specs=[row, row, vec],
        out_shape=[jax.ShapeDtypeStruct((s, d), F32), jax.ShapeDtypeStruct((s, d), BF16),
                   jax.ShapeDtypeStruct((1, d), F32)],
        compiler_params=_params(("arbitrary",)),
    )(x, rstd, g, dn, dres)


def _loss_head(name, x, g, target, tm=256):
    s, d = x.shape

    def body(x_ref, g_ref, t_ref, loss_ref, dx_ref, dxb_ref, dg_ref):
        @pl.when(pl.program_id(0) == 0)
        def _():
            dg_ref[...] = jnp.zeros_like(dg_ref)
            loss_ref[...] = jnp.zeros_like(loss_ref)

        xf = x_ref[...]
        gv = g_ref[...]
        r = lax.rsqrt(jnp.mean(xf * xf, axis=-1, keepdims=True) + EPS)
        xhat = xf * r
        err = xhat * gv - t_ref[...]
        loss_ref[...] += 0.5 * jnp.sum(jnp.mean(err * err, axis=-1, keepdims=True), axis=0, keepdims=True)
        dy = err * (1.0 / d)
        dxhat = dy * gv
        dx = r * (dxhat - xhat * jnp.mean(dxhat * xhat, axis=-1, keepdims=True))
        dx_ref[...] = dx
        dxb_ref[...] = dx.astype(BF16)
        dg_ref[...] += jnp.sum(dy * xhat, axis=0, keepdims=True)

    row = pl.BlockSpec((tm, d), lambda i: (i, 0))
    vec = pl.BlockSpec((1, d), lambda i: (0, 0))
    return pl.pallas_call(
        body, grid=(s // tm,), name=name,
        in_specs=[row, vec, row],
        out_specs=[pl.BlockSpec((1, 1), lambda i: (0, 0)), row, row, vec],
        out_shape=[jax.ShapeDtypeStruct((1, 1), F32), jax.ShapeDtypeStruct((s, d), F32),
                   jax.ShapeDtypeStruct((s, d), BF16), jax.ShapeDtypeStruct((1, d), F32)],
        compiler_params=_params(("arbitrary",)),
    )(x, g, target)


def _sigmoid(v):
    return 1.0 / (1.0 + jnp.exp(-v))


def _log_sigmoid(v):
    return jnp.minimum(v, 0.0) - jnp.log1p(jnp.exp(-jnp.abs(v)))


def _softplus(v):
    return jnp.maximum(v, 0.0) + jnp.log1p(jnp.exp(-jnp.abs(v)))


def _one_minus_exp(z):
    series = -z * (1.0 + z * (0.5 + z * (1.0 / 6.0 + z * (1.0 / 24.0 + z * (1.0 / 120.0)))))
    return jnp.where(z > -0.1, series, 1.0 - jnp.exp(z))


_GELU_K = 0.7978845608028654
_GELU_C = 0.044715


def _gelu(v):
    return 0.5 * v * (1.0 + jnp.tanh(_GELU_K * (v + _GELU_C * v * v * v)))


def _gelu_grad(v):
    t = jnp.tanh(_GELU_K * (v + _GELU_C * v * v * v))
    return 0.5 * (1.0 + t) + 0.5 * v * (1.0 - t * t) * _GELU_K * (1.0 + 3.0 * _GELU_C * v * v)


def _rows(shape):
    return lax.broadcasted_iota(jnp.int32, shape, 0)


def _cols(shape):
    return lax.broadcasted_iota(jnp.int32, shape, 1)


def _shift_down(v, prev, d):
    if d == 0:
        return v
    rolled = pltpu.roll(v, d, axis=0)
    head = jnp.where(_rows((HALO, v.shape[1])) >= d, rolled[:HALO], pltpu.roll(prev, d, axis=0))
    if v.shape[0] == HALO:
        return head
    return jnp.concatenate([head, rolled[HALO:]], axis=0)


def _shift_up(v, nxt, d):
    if d == 0:
        return v
    n = v.shape[0]
    rolled = pltpu.roll(v, n - d, axis=0)
    tail = jnp.where(_rows((HALO, v.shape[1])) < HALO - d, rolled[n - HALO:], pltpu.roll(nxt, HALO - d, axis=0))
    if n == HALO:
        return tail
    return jnp.concatenate([rolled[:n - HALO], tail], axis=0)


def _dot(a, b, contract):
    return lax.dot_general(a.astype(BF16), b.astype(BF16), (contract, ((), ())), preferred_element_type=F32)


def _mlstm_chunk_common(h, q_ref, k_ref, v_ref, gcol_ref, grow_ref, brow_ref, bcol_ref, m_prev):
    L = CHUNK
    sl = slice(h * HEAD_DIM, (h + 1) * HEAD_DIM)
    qh = q_ref[:, sl]
    kh = k_ref[:, sl]
    vh = v_ref[:, sl]
    qs = qh * (HEAD_DIM ** -0.5)
    gates = gcol_ref[...] + brow_ref[...]
    lane = _cols(gates.shape)
    ic = jnp.sum(jnp.where(lane == h, gates, 0.0), axis=1, keepdims=True)
    fc = jnp.sum(jnp.where(lane == HEADS + h, gates, 0.0), axis=1, keepdims=True)
    ir = grow_ref[h:h + 1, :] + bcol_ref[h:h + 1, :]
    fr = grow_ref[HEADS + h:HEADS + h + 1, :] + bcol_ref[HEADS + h:HEADS + h + 1, :]
    logf_c = _log_sigmoid(fc)
    logf_r = _log_sigmoid(fr)
    t_i = _rows((L, L))
    s_i = _cols((L, L))
    tri = t_i >= s_i
    b_c = jnp.sum(jnp.where(tri, logf_r, 0.0), axis=1, keepdims=True)
    b_r = jnp.sum(jnp.where(t_i <= s_i, logf_c, 0.0), axis=0, keepdims=True)
    btot = jnp.sum(logf_r, axis=1, keepdims=True)
    dmat = jnp.where(tri, b_c - b_r + ir, -jnp.inf)
    m_inter = b_c + m_prev
    m_t = jnp.maximum(m_inter, jnp.max(dmat, axis=1, keepdims=True))
    e_mat = jnp.exp(dmat - m_t)
    e_inter = jnp.exp(m_inter - m_t)
    wqk = _dot(qs, kh, NT) * e_mat
    w_end_r = btot - b_r + ir
    m_loc = jnp.max(w_end_r, axis=1, keepdims=True)
    e_end_c = jnp.exp(btot - b_c + ic - m_loc)
    m_new = jnp.maximum(btot + m_prev, m_loc)
    a_dec = jnp.exp(btot + m_prev - m_new)
    c_inj = jnp.exp(m_loc - m_new)
    return dict(qh=qh, kh=kh, vh=vh, qs=qs, fc=fc, tri=tri, t_i=t_i, s_i=s_i, m_t=m_t, e_mat=e_mat,
                e_inter=e_inter, wqk=wqk, e_end_c=e_end_c, m_new=m_new, a_dec=a_dec, c_inj=c_inj)


def _mlstm_fwd(proj, gates_t, bias_row, bias_col, head_g):
    s = proj.shape[0]
    nc = s // CHUNK
    L = CHUNK

    def body(q_ref, k_ref, v_ref, o_ref, gcol_ref, grow_ref, brow_ref, bcol_ref, hg_ref,
             out_ref, cprev_ref, nprev_ref, mprev_ref, c_scr, n_scr, m_scr):
        @pl.when(pl.program_id(0) == 0)
        def _():
            c_scr[...] = jnp.zeros_like(c_scr)
            n_scr[...] = jnp.zeros_like(n_scr)
            m_scr[...] = jnp.zeros_like(m_scr)

        for h in range(HEADS):
            sl = slice(h * HEAD_DIM, (h + 1) * HEAD_DIM)
            m_prev = m_scr[h:h + 1, 0:1]
            n_prev = n_scr[h:h + 1, :]
            c_prev = c_scr[h].astype(BF16)
            q = _mlstm_chunk_common(h, q_ref, k_ref, v_ref, gcol_ref, grow_ref, brow_ref, bcol_ref, m_prev)
            num = _dot(q["wqk"], q["vh"], NN) + q["e_inter"] * _dot(q["qs"], c_prev, NN)
            den = (jnp.sum(q["wqk"], axis=1, keepdims=True)
                   + q["e_inter"] * jnp.sum(q["qs"] * n_prev, axis=1, keepdims=True))
            hh = num / jnp.maximum(jnp.abs(den), jnp.exp(-q["m_t"]))
            hn = hh * lax.rsqrt(jnp.mean(hh * hh, axis=1, keepdims=True) + EPS) * hg_ref[h:h + 1, :]
            out_ref[:, sl] = (_sigmoid(o_ref[:, sl]) * hn).astype(BF16)
            cprev_ref[h] = c_prev
            nprev_ref[h:h + 1, :] = n_prev
            mprev_ref[h:h + 1, :] = jnp.broadcast_to(m_prev, (1, LANES))
            c_loc = _dot(q["kh"], q["e_end_c"] * q["vh"], TN)
            n_loc = jnp.sum(q["e_end_c"] * q["kh"], axis=0, keepdims=True)
            c_scr[h] = q["a_dec"] * c_scr[h] + q["c_inj"] * c_loc
            n_scr[h:h + 1, :] = q["a_dec"] * n_prev + q["c_inj"] * n_loc
            m_scr[h:h + 1, :] = jnp.broadcast_to(q["m_new"], (1, LANES))

    blk = lambda j: pl.BlockSpec((L, D_MLSTM), lambda c, j=j: (c, j))
    full = lambda shp: pl.BlockSpec(shp, lambda c: tuple(0 for _ in shp))
    return pl.pallas_call(
        body, grid=(nc,), name="mlstm_fwd",
        in_specs=[blk(0), blk(1), blk(2), blk(3),
                  pl.BlockSpec((L, LANES), lambda c: (c, (4 * D_MLSTM + 2 * D_LRU) // LANES)),
                  pl.BlockSpec((None, 2 * HEADS, L), lambda c: (c, 0, 0)),
                  full((1, LANES)), full((2 * HEADS, 1)), full((HEADS, HEAD_DIM))],
        out_specs=[pl.BlockSpec((L, D_MLSTM), lambda c: (c, 0)),
                   pl.BlockSpec((None, HEADS, HEAD_DIM, HEAD_DIM), lambda c: (c, 0, 0, 0)),
                   pl.BlockSpec((None, HEADS, HEAD_DIM), lambda c: (c, 0, 0)),
                   pl.BlockSpec((None, HEADS, LANES), lambda c: (c, 0, 0))],
        out_shape=[jax.ShapeDtypeStruct((s, D_MLSTM + D_LRU), BF16),
                   jax.ShapeDtypeStruct((nc, HEADS, HEAD_DIM, HEAD_DIM), BF16),
                   jax.ShapeDtypeStruct((nc, HEADS, HEAD_DIM), F32),
                   jax.ShapeDtypeStruct((nc, HEADS, LANES), F32)],
        scratch_shapes=[pltpu.VMEM((HEADS, HEAD_DIM, HEAD_DIM), F32), pltpu.VMEM((HEADS, HEAD_DIM), F32),
                        pltpu.VMEM((HEADS, LANES), F32)],
        compiler_params=_params(("arbitrary",)),
    )(proj, proj, proj, proj, proj, gates_t, bias_row, bias_col, head_g)


def _mlstm_bwd(proj, gates_t, bias_row, bias_col, head_g, cprev, nprev, mprev, dmix):
    s = proj.shape[0]
    nc = s // CHUNK
    L = CHUNK

    def body(q_ref, k_ref, v_ref, o_ref, gcol_ref, grow_ref, brow_ref, bcol_ref, hg_ref,
             cprev_ref, nprev_ref, mprev_ref, dmix_ref,
             dqkvo_ref, dgate_ref, dhg_ref, g_scr, gn_scr):
        @pl.when(pl.program_id(0) == 0)
        def _():
            g_scr[...] = jnp.zeros_like(g_scr)
            gn_scr[...] = jnp.zeros_like(gn_scr)
            dhg_ref[...] = jnp.zeros_like(dhg_ref)

        lane = _cols((L, LANES))
        dgate = jnp.zeros((L, LANES), F32)
        for h in range(HEADS):
            sl = slice(h * HEAD_DIM, (h + 1) * HEAD_DIM)
            m_prev = mprev_ref[h:h + 1, 0:1]
            n_prev = nprev_ref[h:h + 1, :]
            c_prev = cprev_ref[h]
            q = _mlstm_chunk_common(h, q_ref, k_ref, v_ref, gcol_ref, grow_ref, brow_ref, bcol_ref, m_prev)
            qh, kh, vh, qs, wqk, e_inter = q["qh"], q["kh"], q["vh"], q["qs"], q["wqk"], q["e_inter"]
            num_state = e_inter * _dot(qs, c_prev, NN)
            den_state = e_inter * jnp.sum(qs * n_prev, axis=1, keepdims=True)
            num = _dot(wqk, vh, NN) + num_state
            den = jnp.sum(wqk, axis=1, keepdims=True) + den_state
            floor = jnp.exp(-q["m_t"])
            denom = jnp.maximum(jnp.abs(den), floor)
            hh = num / denom
            rn = lax.rsqrt(jnp.mean(hh * hh, axis=1, keepdims=True) + EPS)
            hn_pre = hh * rn
            hg = hg_ref[h:h + 1, :]
            sg = _sigmoid(o_ref[:, sl])
            dout = dmix_ref[:, sl]
            d_o = dout * (hn_pre * hg) * sg * (1.0 - sg)
            dhn = dout * sg
            dhg_ref[h:h + 1, :] += jnp.sum(dhn * hn_pre, axis=0, keepdims=True)
            dhn_pre = dhn * hg
            dhh = rn * (dhn_pre - hn_pre * jnp.mean(dhn_pre * hn_pre, axis=1, keepdims=True))
            dnum = dhh / denom
            dden = jnp.where(jnp.abs(den) >= floor,
                             -jnp.sum(hh * dhh, axis=1, keepdims=True) / denom * jnp.sign(den), 0.0)
            dwqk = _dot(dnum, vh, NT) + dden
            dv = _dot(wqk, dnum, TN)
            dp = dwqk * q["e_mat"]
            dqs = _dot(dp, kh, NN) + e_inter * (_dot(dnum, c_prev, NT) + dden * n_prev)
            dk = _dot(dp, qs, TN)
            g_next = g_scr[h]
            gn_next = gn_scr[h:h + 1, :]
            w_state = q["e_end_c"] * q["c_inj"]
            dk_state = w_state * (_dot(vh, g_next, NT) + gn_next)
            dk = dk + dk_state
            dv = dv + w_state * _dot(kh, g_next, NN)
            dq = dqs * (HEAD_DIM ** -0.5)
            eye = q["t_i"] == q["s_i"]
            to_row = lambda col: jnp.sum(jnp.where(eye, col, 0.0), axis=0, keepdims=True)
            to_col = lambda row: jnp.sum(jnp.where(eye, row, 0.0), axis=1, keepdims=True)
            g_pair = dwqk * wqk
            rs_in = jnp.sum(g_pair, axis=1, keepdims=True)
            cs_in_r = jnp.sum(g_pair, axis=0, keepdims=True)
            rs_state = (jnp.sum(dnum * num_state, axis=1, keepdims=True) + dden * den_state)
            cs_state = jnp.sum(kh * dk_state, axis=1, keepdims=True)
            di_c = to_col(cs_in_r) + cs_state
            through = q["a_dec"] * (jnp.sum(jnp.sum(g_next * c_prev.astype(F32), axis=1, keepdims=True),
                                            axis=0, keepdims=True)
                                    + jnp.sum(gn_next * n_prev, axis=1, keepdims=True))
            ends_here = to_row(rs_in + rs_state) - cs_in_r
            da_c = (jnp.sum(jnp.where(q["s_i"] >= q["t_i"], ends_here, 0.0), axis=1, keepdims=True)
                    + jnp.sum(jnp.where(q["s_i"] < q["t_i"], to_row(cs_state), 0.0), axis=1, keepdims=True)
                    + through)
            df_c = da_c * _sigmoid(-q["fc"])
            dgate = dgate + jnp.where(lane == h, di_c, 0.0) + jnp.where(lane == HEADS + h, df_c, 0.0)
            dqkvo_ref[:, sl] = dq.astype(BF16)
            dqkvo_ref[:, D_MLSTM + h * HEAD_DIM:D_MLSTM + (h + 1) * HEAD_DIM] = dk.astype(BF16)
            dqkvo_ref[:, 2 * D_MLSTM + h * HEAD_DIM:2 * D_MLSTM + (h + 1) * HEAD_DIM] = dv.astype(BF16)
            dqkvo_ref[:, 3 * D_MLSTM + h * HEAD_DIM:3 * D_MLSTM + (h + 1) * HEAD_DIM] = d_o.astype(BF16)
            g_scr[h] = q["a_dec"] * g_next + _dot(e_inter * qs, dnum, TN)
            gn_scr[h:h + 1, :] = q["a_dec"] * gn_next + jnp.sum(e_inter * qs * dden, axis=0, keepdims=True)
        dgate_ref[...] = dgate

    rev = lambda c: nc - 1 - c
    blk = lambda j: pl.BlockSpec((L, D_MLSTM), lambda c, j=j: (rev(c), j))
    full = lambda shp: pl.BlockSpec(shp, lambda c: tuple(0 for _ in shp))
    return pl.pallas_call(
        body, grid=(nc,), name="mlstm_bwd",
        in_specs=[blk(0), blk(1), blk(2), blk(3),
                  pl.BlockSpec((L, LANES), lambda c: (rev(c), (4 * D_MLSTM + 2 * D_LRU) // LANES)),
                  pl.BlockSpec((None, 2 * HEADS, L), lambda c: (rev(c), 0, 0)),
                  full((1, LANES)), full((2 * HEADS, 1)), full((HEADS, HEAD_DIM)),
                  pl.BlockSpec((None, HEADS, HEAD_DIM, HEAD_DIM), lambda c: (rev(c), 0, 0, 0)),
                  pl.BlockSpec((None, HEADS, HEAD_DIM), lambda c: (rev(c), 0, 0)),
                  pl.BlockSpec((None, HEADS, LANES), lambda c: (rev(c), 0, 0)),
                  pl.BlockSpec((L, D_MLSTM), lambda c: (rev(c), 0))],
        out_specs=[pl.BlockSpec((L, 4 * D_MLSTM), lambda c: (rev(c), 0)),
                   pl.BlockSpec((L, LANES), lambda c: (rev(c), 0)),
                   full((HEADS, HEAD_DIM))],
        out_shape=[jax.ShapeDtypeStruct((s, _PROJ_PAD), BF16),
                   jax.ShapeDtypeStruct((s, LANES), F32),
                   jax.ShapeDtypeStruct((HEADS, HEAD_DIM), F32)],
        scratch_shapes=[pltpu.VMEM((HEADS, HEAD_DIM, HEAD_DIM), F32), pltpu.VMEM((HEADS, HEAD_DIM), F32)],
        compiler_params=_params(("arbitrary",)),
    )(proj, proj, proj, proj, proj, gates_t, bias_row, bias_col, head_g, cprev, nprev, mprev, dmix)


def _lru_gates(xc, wa_ref, wx_ref, ba, bx, lam):
    r = _sigmoid(_dot(xc, wa_ref[...], NN) + ba)
    ig = _sigmoid(_dot(xc, wx_ref[...], NN) + bx)
    sp = _softplus(-lam)
    log_a = (-LRU_C * r) * sp
    a = jnp.exp(log_a)
    mult = jnp.sqrt(_one_minus_exp(2.0 * log_a))
    return r, ig, sp, a, mult


def _lru_conv(xr, prev, w_ref, b):
    xc = b + _shift_down(xr, prev, 3) * w_ref[0:1, :]
    for j in range(1, LRU_CONV):
        xc = xc + _shift_down(xr, prev, LRU_CONV - 1 - j) * w_ref[j:j + 1, :]
    return xc


def _lru_fwd(proj, mix, conv_w, conv_b, wa, wx, ba, bx, lam, tt=512):
    s = proj.shape[0]
    tt = min(tt, s)
    nt = s // tt
    B = LRU_BLOCK_DIM
    lru_col = 4 * D_MLSTM // B
    mix_col = D_MLSTM // B

    def body(xr_ref, gr_ref, cw_ref, cb_ref, wa_ref, wx_ref, ba_ref, bx_ref, lam_ref, mix_in_ref,
             out_ref, h_ref, prev_scr, hcar_scr):
        @pl.when(pl.program_id(1) == 0)
        def _():
            prev_scr[...] = jnp.zeros_like(prev_scr)
            hcar_scr[...] = jnp.zeros_like(hcar_scr)

        xr = xr_ref[...]
        xc = _lru_conv(xr, prev_scr[...], cw_ref, cb_ref[...])
        prev_scr[...] = xr[tt - HALO:, :]
        _, ig, _, a, mult = _lru_gates(xc, wa_ref, wx_ref, ba_ref[...], bx_ref[...], lam_ref[...])
        u = mult * (ig * xc)
        rows = _rows((tt, B))
        acc_a, acc_b = a, u
        d = 1
        while d < tt:
            keep = rows >= d
            sh_a = jnp.where(keep, pltpu.roll(acc_a, d, axis=0), 1.0)
            sh_b = jnp.where(keep, pltpu.roll(acc_b, d, axis=0), 0.0)
            acc_b = acc_a * sh_b + acc_b
            acc_a = acc_a * sh_a
            d *= 2
        hv = acc_b + acc_a * hcar_scr[0:1, :]
        hcar_scr[...] = jnp.broadcast_to(hv[tt - 1:tt, :], hcar_scr.shape)
        h_ref[...] = hv
        out_ref[...] = (hv * _gelu(gr_ref[...])).astype(BF16)

    chan = lambda rws: pl.BlockSpec((rws, B), lambda n, i: (0, n))
    return pl.pallas_call(
        body, grid=(LRU_BLOCKS, nt), name="lru_fwd",
        in_specs=[pl.BlockSpec((tt, B), lambda n, i: (i, lru_col + 2 * n)),
                  pl.BlockSpec((tt, B), lambda n, i: (i, lru_col + 2 * n + 1)),
                  chan(LRU_CONV), chan(1),
                  pl.BlockSpec((None, B, B), lambda n, i: (n, 0, 0)),
                  pl.BlockSpec((None, B, B), lambda n, i: (n, 0, 0)),
                  chan(1), chan(1), chan(1), ANY],
        out_specs=[pl.BlockSpec((tt, B), lambda n, i: (i, mix_col + n)), pl.BlockSpec((tt, B), lambda n, i: (i, n))],
        out_shape=[jax.ShapeDtypeStruct(mix.shape, BF16), jax.ShapeDtypeStruct((s, D_LRU), F32)],
        scratch_shapes=[pltpu.VMEM((HALO, B), F32), pltpu.VMEM((HALO, B), F32)],
        input_output_aliases={9: 0},
        compiler_params=_params(("parallel", "arbitrary")),
    )(proj, proj, conv_w, conv_b, wa, wx, ba, bx, lam, mix)


def _lru_bwd(proj, hsave, dmix, dproj, conv_w, conv_b, wa, wx, ba, bx, lam, tt=512):
    s = proj.shape[0]
    tt = min(tt, s)
    nt = s // tt
    B = LRU_BLOCK_DIM
    lru_col = 4 * D_MLSTM // B
    dmix_col = D_MLSTM // B
    hpb = tt // HALO

    def body(xr_ref, xprev_ref, gr_ref, h_ref, hprev_ref, dmix_ref, cw_ref, cb_ref, wa_ref, wx_ref,
             ba_ref, bx_ref, lam_ref, dproj_in_ref,
             dxg_ref, dcw_ref, dcb_ref, dwa_ref, dwx_ref, dba_ref, dbx_ref, dlam_ref,
             gcar_scr, acar_scr, dxc_scr):
        i = pl.program_id(1)
        first_tile = i == nt - 1

        @pl.when(i == 0)
        def _():
            gcar_scr[...] = jnp.zeros_like(gcar_scr)
            acar_scr[...] = jnp.zeros_like(acar_scr)
            dxc_scr[...] = jnp.zeros_like(dxc_scr)
            for ref in (dcw_ref, dcb_ref, dwa_ref, dwx_ref, dba_ref, dbx_ref, dlam_ref):
                ref[...] = jnp.zeros_like(ref)

        xr = xr_ref[...]
        xprev = jnp.where(first_tile, 0.0, xprev_ref[...])
        hprev = jnp.where(first_tile, 0.0, hprev_ref[...])
        lam = lam_ref[...]
        xc = _lru_conv(xr, xprev, cw_ref, cb_ref[...])
        r, ig, sp, a, mult = _lru_gates(xc, wa_ref, wx_ref, ba_ref[...], bx_ref[...], lam)
        gr = gr_ref[...]
        hv = h_ref[...]
        dout = dmix_ref[...]
        dxg_ref[:, B:] = (dout * hv * _gelu_grad(gr)).astype(BF16)
        dh = dout * _gelu(gr)
        rows = _rows((tt, B))
        acc_a = _shift_up(a, acar_scr[...], 1)
        acc_b = dh
        d = 1
        while d < tt:
            keep = rows < tt - d
            sh_a = jnp.where(keep, pltpu.roll(acc_a, tt - d, axis=0), 1.0)
            sh_b = jnp.where(keep, pltpu.roll(acc_b, tt - d, axis=0), 0.0)
            acc_b = acc_a * sh_b + acc_b
            acc_a = acc_a * sh_a
            d *= 2
        gv = acc_b + acc_a * gcar_scr[0:1, :]
        gcar_scr[...] = jnp.broadcast_to(gv[0:1, :], gcar_scr.shape)
        acar_scr[...] = jnp.broadcast_to(a[0:1, :], acar_scr.shape)
        h_before = _shift_down(hv, hprev, 1)
        da = gv * h_before
        dmult = gv * (ig * xc)
        dig = gv * mult * xc
        dxc = gv * mult * ig
        dlog_a = da * a - dmult * (a * a) / mult
        dr = dlog_a * (-LRU_C * sp)
        dlam_ref[...] += jnp.sum(dlog_a * (-LRU_C * r), axis=0, keepdims=True) * (-_sigmoid(-lam))
        dpre_r = dr * r * (1.0 - r)
        dpre_i = dig * ig * (1.0 - ig)
        dba_ref[...] += jnp.sum(dpre_r, axis=0, keepdims=True)
        dbx_ref[...] += jnp.sum(dpre_i, axis=0, keepdims=True)
        dwa_ref[...] += _dot(xc, dpre_r, TN)
        dwx_ref[...] += _dot(xc, dpre_i, TN)
        dxc = dxc + _dot(dpre_r, wa_ref[...], NT) + _dot(dpre_i, wx_ref[...], NT)
        dcb_ref[...] += jnp.sum(dxc, axis=0, keepdims=True)
        nxt = dxc_scr[...]
        dxr = jnp.zeros((tt, B), F32)
        for j in range(LRU_CONV):
            sft = LRU_CONV - 1 - j
            dcw_ref[j:j + 1, :] += jnp.sum(dxc * _shift_down(xr, xprev, sft), axis=0, keepdims=True)
            dxr = dxr + _shift_up(dxc, nxt, sft) * cw_ref[j:j + 1, :]
        dxc_scr[...] = dxc[:HALO, :]
        dxg_ref[:, :B] = dxr.astype(BF16)

    rev = lambda i: nt - 1 - i
    tile = lambda col, step: pl.BlockSpec((tt, B), lambda n, i: (rev(i), col + step * n))
    halo = lambda col, step: pl.BlockSpec(
        (HALO, B), lambda n, i: (jnp.maximum(rev(i) * hpb - 1, 0), col + step * n))
    chan = lambda rws: pl.BlockSpec((rws, B), lambda n, i: (0, n))
    wblk = pl.BlockSpec((None, B, B), lambda n, i: (n, 0, 0))
    return pl.pallas_call(
        body, grid=(LRU_BLOCKS, nt), name="lru_bwd",
        in_specs=[tile(lru_col, 2), halo(lru_col, 2), tile(lru_col + 1, 2), tile(0, 1), halo(0, 1),
                  tile(dmix_col, 1), chan(LRU_CONV), chan(1), wblk, wblk, chan(1), chan(1), chan(1), ANY],
        out_specs=[pl.BlockSpec((tt, 2 * B), lambda n, i: (rev(i), lru_col // 2 + n)),
                   chan(LRU_CONV), chan(1), wblk, wblk, chan(1), chan(1), chan(1)],
        out_shape=[jax.ShapeDtypeStruct(dproj.shape, BF16),
                   jax.ShapeDtypeStruct((LRU_CONV, D_LRU), F32), jax.ShapeDtypeStruct((1, D_LRU), F32),
                   jax.ShapeDtypeStruct((LRU_BLOCKS, B, B), F32), jax.ShapeDtypeStruct((LRU_BLOCKS, B, B), F32),
                   jax.ShapeDtypeStruct((1, D_LRU), F32), jax.ShapeDtypeStruct((1, D_LRU), F32),
                   jax.ShapeDtypeStruct((1, D_LRU), F32)],
        scratch_shapes=[pltpu.VMEM((HALO, B), F32), pltpu.VMEM((HALO, B), F32), pltpu.VMEM((HALO, B), F32)],
        input_output_aliases={13: 0},
        compiler_params=_params(("parallel", "arbitrary")),
    )(proj, proj, proj, hsave, hsave, dmix, conv_w, conv_b, wa, wx, ba, bx, lam, dproj)


def _ffn_conv(gp, prev, w_ref, b):
    g = b + _shift_down(gp, prev, 2) * w_ref[0:1, :]
    for j in range(1, FFN_CONV):
        g = g + _shift_down(gp, prev, FFN_CONV - 1 - j) * w_ref[j:j + 1, :]
    return g


def _ffn_act_fwd(gu, conv_w, conv_b, tt=256):
    s = gu.shape[0]
    tt = min(tt, s)
    d_ff = conv_w.shape[1]
    tc = d_ff // N_CHIPS
    hpb = tt // HALO

    def body(g_ref, gprev_ref, u_ref, w_ref, b_ref, act_ref):
        prev = jnp.where(pl.program_id(0) == 0, 0.0, gprev_ref[...])
        gate = _ffn_conv(g_ref[...], prev, w_ref, b_ref[...])
        act_ref[...] = (gate * _sigmoid(gate) * u_ref[...]).astype(BF16)

    return pl.pallas_call(
        body, grid=(s // tt, N_CHIPS), name="ffn_act_fwd",
        in_specs=[pl.BlockSpec((tt, tc), lambda i, j: (i, 2 * j)),
                  pl.BlockSpec((HALO, tc), lambda i, j: (jnp.maximum(i * hpb - 1, 0), 2 * j)),
                  pl.BlockSpec((tt, tc), lambda i, j: (i, 2 * j + 1)),
                  pl.BlockSpec((FFN_CONV, tc), lambda i, j: (0, j)),
                  pl.BlockSpec((1, tc), lambda i, j: (0, j))],
        out_specs=pl.BlockSpec((tt, tc), lambda i, j: (i, j)),
        out_shape=jax.ShapeDtypeStruct((s, d_ff), BF16),
        compiler_params=_params(("parallel", "parallel")),
    )(gu, gu, gu, conv_w, conv_b)


def _ffn_act_bwd(gu, dact, conv_w, conv_b, tt=256):
    s = gu.shape[0]
    tt = min(tt, s)
    nt = s // tt
    d_ff = conv_w.shape[1]
    tc = d_ff // N_CHIPS
    hpb = tt // HALO

    def dgate_of(gate, up, da):
        sg = _sigmoid(gate)
        return da * up * (sg * (1.0 + gate * (1.0 - sg))), da * (gate * sg)

    def body(g_ref, gprev_ref, gnext_ref, u_ref, unext_ref, da_ref, danext_ref, w_ref, b_ref,
             dgu_ref, dw_ref, db_ref):
        i = pl.program_id(1)

        @pl.when(i == 0)
        def _():
            dw_ref[...] = jnp.zeros_like(dw_ref)
            db_ref[...] = jnp.zeros_like(db_ref)

        gp = g_ref[...]
        prev = jnp.where(i == 0, 0.0, gprev_ref[...])
        bias = b_ref[...]
        gate = _ffn_conv(gp, prev, w_ref, bias)
        dgate, dup = dgate_of(gate, u_ref[...], da_ref[...])
        gate_n = _ffn_conv(gnext_ref[...], gp[tt - HALO:, :], w_ref, bias)
        dgate_n, _ = dgate_of(gate_n, unext_ref[...], danext_ref[...])
        dgate_n = jnp.where(i == nt - 1, 0.0, dgate_n)
        db_ref[...] += jnp.sum(dgate, axis=0, keepdims=True)
        dgp = jnp.zeros((tt, tc), F32)
        for j in range(FFN_CONV):
            sft = FFN_CONV - 1 - j
            dw_ref[j:j + 1, :] += jnp.sum(dgate * _shift_down(gp, prev, sft), axis=0, keepdims=True)
            dgp = dgp + _shift_up(dgate, dgate_n, sft) * w_ref[j:j + 1, :]
        dgu_ref[:, :tc] = dgp.astype(BF16)
        dgu_ref[:, tc:] = dup.astype(BF16)

    tile = lambda half: pl.BlockSpec((tt, tc), lambda j, i, half=half: (i, 2 * j + half))
    hprev = lambda half: pl.BlockSpec((HALO, tc), lambda j, i, half=half: (jnp.maximum(i * hpb - 1, 0), 2 * j + half))
    hnext = lambda half: pl.BlockSpec(
        (HALO, tc), lambda j, i, half=half: (jnp.minimum((i + 1) * hpb, nt * hpb - 1), 2 * j + half))
    return pl.pallas_call(
        body, grid=(N_CHIPS, nt), name="ffn_act_bwd",
        in_specs=[tile(0), hprev(0), hnext(0), tile(1), hnext(1),
                  pl.BlockSpec((tt, tc), lambda j, i: (i, j)),
                  pl.BlockSpec((HALO, tc), lambda j, i: (jnp.minimum((i + 1) * hpb, nt * hpb - 1), j)),
                  pl.BlockSpec((FFN_CONV, tc), lambda j, i: (0, j)),
                  pl.BlockSpec((1, tc), lambda j, i: (0, j))],
        out_specs=[pl.BlockSpec((tt, 2 * tc), lambda j, i: (i, j)),
                   pl.BlockSpec((FFN_CONV, tc), lambda j, i: (0, j)),
                   pl.BlockSpec((1, tc), lambda j, i: (0, j))],
        out_shape=[jax.ShapeDtypeStruct((s, 2 * d_ff), BF16),
                   jax.ShapeDtypeStruct((FFN_CONV, d_ff), F32), jax.ShapeDtypeStruct((1, d_ff), F32)],
        compiler_params=_params(("parallel", "arbitrary")),
    )(gu, gu, gu, gu, gu, dact, dact, conv_w, conv_b)


def _gate_grads(dgate, dproj, tm=512):
    s, n = dgate.shape
    tm = min(tm, s)

    def body(a_ref, dproj_in_ref, o_ref, dproj_ref):
        @pl.when(pl.program_id(0) == 0)
        def _():
            o_ref[...] = jnp.zeros_like(o_ref)
        a = a_ref[...]
        o_ref[...] += jnp.sum(a, axis=0, keepdims=True)
        dproj_ref[...] = a.astype(BF16)

    return pl.pallas_call(
        body, grid=(s // tm,), name="gate_grads",
        in_specs=[pl.BlockSpec((tm, n), lambda i: (i, 0)), ANY],
        out_specs=[pl.BlockSpec((1, n), lambda i: (0, 0)),
                   pl.BlockSpec((tm, n), lambda i: (i, (_QKVO + 2 * D_LRU) // LANES))],
        out_shape=[jax.ShapeDtypeStruct((1, n), F32), jax.ShapeDtypeStruct(dproj.shape, BF16)],
        input_output_aliases={1: 1},
        compiler_params=_params(("arbitrary",)),
    )(dgate, dproj)


def _pick(n, *cands):
    for c in cands:
        if n % c == 0:
            return c
    raise ValueError(f"no tile for {n}")


def _behind(a, token):
    return a if token is None else a + token[0:1, 0:1].astype(a.dtype).reshape((1,) * a.ndim)


class _Gathered:
    def __init__(self, w):
        self.w = w

    def begin(self):
        return None

    def mid(self, grp, after):
        return None

    def end(self, grp, after):
        return self.w

    def reduce_early(self, grads):
        return None

    def reduce_early_mid(self, after):
        return None

    def reduce_late(self, grads):
        return None

    def reduce_late_mid(self, after):
        return None


def _local_step(x, target, w, comm):
    s, d = x.shape
    nc = s // CHUNK
    tm = _pick(s, 1024, 512, 256)
    tn_proj = _pick(_PROJ_PAD, 896)
    gate_col = 4 * D_MLSTM + 2 * D_LRU
    w = dict(w)

    token = comm.begin()
    n1, rstd1 = _rmsnorm_fwd("norm_mix_fwd", x, _behind(w["norm_mix_g"], token))
    comm.mid(0, n1)
    w.update(comm.end(0, None))
    proj = _mm_nn("proj_fwd", n1, w["w_in"], tm, tn_proj, 512)
    token = comm.mid(1, proj)
    gates = proj[:, gate_col:gate_col + 2 * HEADS]
    gates_t = gates.reshape(nc, CHUNK, 2 * HEADS).transpose(0, 2, 1)
    bias_row = _behind(jnp.pad(w["b_gate_m"], ((0, 0), (0, LANES - 2 * HEADS))), token)
    bias_col = w["b_gate_m"].reshape(2 * HEADS, 1)
    mix, cprev, nprev, mprev = _mlstm_fwd(proj, gates_t, bias_row, bias_col, w["mlstm_norm_g"])
    mix, hsave = _lru_fwd(proj, mix, w["lru_conv_w"], w["lru_conv_b"], w["lru_wa"], w["lru_wx"],
                          w["lru_ba"], w["lru_bx"], w["lru_lambda"])
    w.update(comm.end(1, hsave))
    x1 = _mm_nn("out_fwd", mix, w["w_out"], tm, 1024, 512, res=x)
    n2, rstd2 = _rmsnorm_fwd("norm_ffn_fwd", x1, w["norm_ffn_g"])
    token = comm.mid(2, n2)
    gu = _mm_up_fwd("up_fwd", n2, w["w_up"], tm, 512)
    act = _ffn_act_fwd(gu, w["ffn_conv_w"], _behind(w["ffn_conv_b"], token))
    w.update(comm.end(2, act))
    d_ff = w["w_down"].shape[0]
    x2 = _mm_nn("down_fwd", act, w["w_down"], tm, 1024, _pick(d_ff, 512), res=x1)
    loss, dx2, dx2b, g_norm_final = _loss_head("loss_head", x2, w["norm_final_g"], target)

    grads = {"norm_final_g": g_norm_final}
    dact = _mm_nt("down_bwd_x", dx2b, w["w_down"], tm, _pick(d_ff, 1408, 512), 1024)
    grads["w_down"] = _mm_tn("down_bwd_w", act, dx2b, _pick(d_ff, 1408, 512), 1024, 512)
    dgu, grads["ffn_conv_w"], grads["ffn_conv_b"] = _ffn_act_bwd(gu, dact, w["ffn_conv_w"], w["ffn_conv_b"])
    dn2 = _mm_up_bwd_x("up_bwd_x", dgu, w["w_up"], tm, 1024)
    grads["w_up"] = _mm_up_bwd_w("up_bwd_w", n2, dgu, 1024, 512)
    dx1, dx1b, grads["norm_ffn_g"] = _rmsnorm_bwd("norm_ffn_bwd", x1, rstd2, w["norm_ffn_g"], dn2, dx2)
    dmix = _mm_nt("out_bwd_x", dx1b, w["w_out"], tm, 1024, 1024)
    grads["w_out"] = _mm_tn("out_bwd_w", mix, dx1b, 1024, 1024, 512)
    token = comm.reduce_early(grads)
    dproj, dgate, grads["mlstm_norm_g"] = _mlstm_bwd(proj, gates_t, _behind(bias_row, token), bias_col,
                                                     w["mlstm_norm_g"], cprev, nprev, mprev, dmix)
    token = comm.reduce_early_mid(dproj)
    (dproj, grads["lru_conv_w"], grads["lru_conv_b"], grads["lru_wa"], grads["lru_wx"],
     grads["lru_ba"], grads["lru_bx"], grads["lru_lambda"]) = _lru_bwd(
        proj, hsave, dmix, dproj, w["lru_conv_w"], _behind(w["lru_conv_b"], token), w["lru_wa"], w["lru_wx"],
        w["lru_ba"], w["lru_bx"], w["lru_lambda"])
    gate_bias_grad, dproj = _gate_grads(dgate, dproj)
    grads["b_gate_m"] = gate_bias_grad[:, :2 * HEADS]
    grads["w_in"] = _mm_tn("proj_bwd_w", n1, dproj, 1024, tn_proj, 512)
    comm.reduce_late(grads)
    dn1 = _mm_nt("proj_bwd_x", dproj, w["w_in"], tm, 1024, tn_proj)
    token = comm.reduce_late_mid(dn1)
    grad_x, _, grads["norm_mix_g"] = _rmsnorm_bwd("norm_mix_bwd", x, rstd1, _behind(w["norm_mix_g"], token),
                                                  dn1, dx1)
    return loss, grad_x, grads


WEIGHT_NAMES = ("norm_mix_g", "w_in", "b_gate_m", "mlstm_norm_g", "lru_conv_w", "lru_conv_b", "lru_wa", "lru_ba",
                "lru_wx", "lru_bx", "lru_lambda", "w_out", "norm_ffn_g", "w_up", "ffn_conv_w", "ffn_conv_b",
                "w_down", "norm_final_g")
BIG = ("w_in", "w_out", "w_up", "w_down")
SMALL_SHARDED = ("mlstm_norm_g", "lru_conv_w", "ffn_conv_w")
SMALL = tuple(n for n in WEIGHT_NAMES if n not in BIG)
SMALL_REPLICATED = tuple(n for n in SMALL if n not in SMALL_SHARDED)


def _proj_segments():
    segs = [(0, 0, _QKVO), (_QKVO, _QKVO + 2 * D_LRU, _N_GATES)]
    for n in range(LRU_BLOCKS):
        segs.append((_QKVO + _N_GATES + n * LRU_BLOCK_DIM, _QKVO + 2 * n * LRU_BLOCK_DIM, LRU_BLOCK_DIM))
        segs.append((_QKVO + _N_GATES + D_LRU + n * LRU_BLOCK_DIM, _QKVO + (2 * n + 1) * LRU_BLOCK_DIM,
                     LRU_BLOCK_DIM))
    return segs


def _w_in_shards_to_local(shards):
    width = shards.shape[2]
    pieces = []
    for g0, _, n in sorted(_proj_segments(), key=lambda s: s[1]):
        at = g0
        while at < g0 + n:
            j = at // width
            stop = min(g0 + n, (j + 1) * width)
            pieces.append(shards[j][:, at - j * width:stop - j * width])
            at = stop
    pieces.append(jnp.zeros((shards.shape[1], PROJ_GATE_PAD - _N_GATES), shards.dtype))
    return jnp.concatenate(pieces, axis=1)


def _w_in_local_to_shards(w):
    width = _PROJ_COLS // N_CHIPS
    shards = []
    for j in range(N_CHIPS):
        pieces = []
        for g0, l0, n in sorted(_proj_segments()):
            lo, hi = max(g0, j * width), min(g0 + n, (j + 1) * width)
            if lo < hi:
                pieces.append(w[:, l0 + lo - g0:l0 + hi - g0])
        shards.append(jnp.concatenate(pieces, axis=1))
    return jnp.stack(shards)


def _w_in_to_global(w):
    sh = _w_in_local_to_shards(w)
    return jnp.concatenate([sh[j] for j in range(N_CHIPS)], axis=1)


def _pack(arrs, rows):
    flat = jnp.concatenate([a.reshape(-1).astype(F32) for a in arrs])
    return jnp.pad(flat, (0, rows * LANES - flat.shape[0])).reshape(rows, LANES)


def _unpack(buf, shapes):
    flat = buf.reshape(-1)
    out, at = [], 0
    for shp in shapes:
        n = 1
        for d in shp:
            n *= d
        out.append(flat[at:at + n].reshape(shp))
        at += n
    return out


def _pack_rows(shapes):
    n = sum(functools.reduce(lambda a, b: a * b, shp, 1) for shp in shapes)
    return -(-n // (HALO * LANES)) * HALO


def _assemble_weights(g_in, g_out, g_up, g_down, small_sharded, replicated):
    w = dict(replicated)
    w["w_in"] = _w_in_shards_to_local(g_in)
    w["w_out"] = g_out.reshape(-1, g_out.shape[-1])
    w["w_up"] = g_up
    w["w_down"] = g_down.reshape(-1, g_down.shape[-1])
    for name, v in small_sharded.items():
        w[name] = jnp.concatenate([v[j] for j in range(N_CHIPS)], axis=1)
    return w


def _full_weights_from_global(weights):
    shard = lambda a, axis: jnp.stack(jnp.split(a, N_CHIPS, axis=axis))
    rep = {n: weights[n].reshape(1, -1) if weights[n].ndim <= 2 and n != "b_gate_m" else weights[n]
           for n in SMALL_REPLICATED}
    rep["b_gate_m"] = weights["b_gate_m"].reshape(1, -1)
    return _assemble_weights(shard(weights["w_in"], 1).astype(BF16), shard(weights["w_out"], 0).astype(BF16),
                             shard(weights["w_up"], 1).astype(BF16), shard(weights["w_down"], 0).astype(BF16),
                             {n: shard(weights[n], 1) for n in SMALL_SHARDED}, rep)


def _grads_to_global(grads):
    g = dict(grads)
    g["w_in"] = _w_in_to_global(grads["w_in"])
    g["w_up"] = jnp.concatenate([grads["w_up"][j] for j in range(N_CHIPS)], axis=1)
    return g


def _place():
    x, y, c = lax.axis_index("x"), lax.axis_index("y"), lax.axis_index("c")
    chips = [(1 - x, y), (x, 1 - y), (1 - x, 1 - y)]
    return x, y, c, 2 * x + y, chips


def _half_rows(n_rows, which):
    half = n_rows // 2
    return pl.ds(pl.multiple_of(which * half, 16), half)


def _rcopy(src, dst, send_sem, recv_sem, to):
    return pltpu.make_async_remote_copy(src_ref=src, dst_ref=dst, send_sem=send_sem, recv_sem=recv_sem,
                                        device_id=to, device_id_type=MESH)


HBM_SPEC = pl.BlockSpec(memory_space=pltpu.HBM)
SEM_SPEC = pl.BlockSpec(memory_space=pltpu.SEMAPHORE)
TOKEN_SHAPE = (8, LANES)


def _split_call(name, bufs, sems_in, sems_out_shapes, body_fn, after=None):
    nb, ni, no = len(bufs), len(sems_in), len(sems_out_shapes)

    def body(*refs):
        buf_refs = refs[:nb]
        sem_in_refs = refs[nb:nb + ni]
        outs = refs[nb + ni + (0 if after is None else 1):]
        sem_out_refs = outs[:no]
        token_ref = outs[no + nb]
        body_fn(buf_refs, sem_in_refs, sem_out_refs)
        token_ref[...] = jnp.zeros_like(token_ref)

    out_shape = ([pltpu.SemaphoreType.DMA(shp) for shp in sems_out_shapes]
                 + [pltpu.HBM(b.shape, b.dtype) for b in bufs] + [jax.ShapeDtypeStruct(TOKEN_SHAPE, F32)])
    res = pl.pallas_call(
        body, name=name, out_shape=out_shape,
        in_specs=[HBM_SPEC] * nb + [SEM_SPEC] * ni + ([] if after is None else [ANY]),
        out_specs=[SEM_SPEC] * no + [HBM_SPEC] * nb + [pl.BlockSpec(memory_space=pltpu.VMEM)],
        input_output_aliases={i: no + i for i in range(nb)},
        compiler_params=pltpu.CompilerParams(has_side_effects=pltpu.SideEffectType.DATAFLOW_SIDE_EFFECTING),
    )(*[pltpu.with_memory_space_constraint(b, pltpu.HBM) for b in bufs], *sems_in,
      *(() if after is None else (after,)))
    return list(res[:no]), list(res[no:no + nb]), res[no + nb]


def _place_own_shard(name, idx, shard):
    rows, cols = shard.shape
    tr = _row_tile(rows)

    def body(idx_ref, s_ref, o_ref):
        o_ref[...] = s_ref[...].astype(BF16)

    return pl.pallas_call(
        body, name=name, out_shape=jax.ShapeDtypeStruct((N_CHIPS, rows, cols), BF16),
        grid_spec=pltpu.PrefetchScalarGridSpec(
            num_scalar_prefetch=1, grid=(rows // tr,),
            in_specs=[pl.BlockSpec((tr, cols), lambda i, s: (i, 0))],
            out_specs=pl.BlockSpec((None, tr, cols), lambda i, s: (s[1], i, 0))),
        compiler_params=_params(("parallel",)),
    )(idx, shard)


GATHER_GROUPS = ((0, 4), (1, 2), (3,))


def _gather_start(lands):
    order = [w for grp in GATHER_GROUPS for w in grp]

    def starts(bufs, _, sems):
        x, y, c, me, chips = _place()
        for w in order:
            grp = [g for g, members in enumerate(GATHER_GROUPS) if w in members][0]
            pos = GATHER_GROUPS[grp].index(w)
            part = bufs[w].at[me] if w == 4 else bufs[w].at[me, _half_rows(bufs[w].shape[1], c)]
            for k, chip in enumerate(chips):
                _rcopy(part, part, sems[2 * grp].at[3 * pos + k], sems[2 * grp + 1].at[3 * pos + k],
                       (*chip, c)).start()

    shapes = []
    for members in GATHER_GROUPS:
        shapes += [(3 * len(members),), (3 * len(members),)]
    sems, lands, token = _split_call("gather_start", lands, [], shapes, starts)
    return [(sems[2 * g], sems[2 * g + 1]) for g in range(len(GATHER_GROUPS))], lands, token


def _gather_mid(grp, lands, sems, after):
    members = GATHER_GROUPS[grp]
    big = [w for w in members if w != 4]

    def mid(bufs, sems_in, sems_out):
        x, y, c, me, chips = _place()
        send_sems, recv_sems = sems_in
        for pos, w in enumerate(members):
            for k, chip in enumerate(chips):
                cid = 2 * chip[0] + chip[1]
                buf = bufs[pos]
                mine = buf.at[me] if w == 4 else buf.at[me, _half_rows(buf.shape[1], c)]
                theirs = buf.at[cid] if w == 4 else buf.at[cid, _half_rows(buf.shape[1], c)]
                arrival = _rcopy(mine, theirs, send_sems.at[3 * pos + k], recv_sems.at[3 * pos + k], (*chip, c))
                arrival.wait_recv()
                arrival.wait_send()
                if w != 4:
                    _rcopy(theirs, theirs, sems_out[0].at[3 * big.index(w) + k],
                           sems_out[1].at[3 * big.index(w) + k], (x, y, 1 - c)).start()

    new_sems, bufs, token = _split_call(f"gather_mid_{grp}", [lands[w] for w in members], list(sems),
                                        [(3 * len(big),), (3 * len(big),)], mid, after=after)
    return new_sems, bufs, token


def _gather_end(grp, bufs, sems, after):
    members = GATHER_GROUPS[grp]
    big = [w for w in members if w != 4]

    def end(refs, sems_in, _):
        x, y, c, me, chips = _place()
        send_sems, recv_sems = sems_in
        for pos, w in enumerate(members):
            if w == 4:
                continue
            for k, chip in enumerate(chips):
                cid = 2 * chip[0] + chip[1]
                buf = refs[pos]
                sent = buf.at[cid, _half_rows(buf.shape[1], c)]
                landed = buf.at[cid, _half_rows(buf.shape[1], 1 - c)]
                fwd = _rcopy(sent, landed, send_sems.at[3 * big.index(w) + k], recv_sems.at[3 * big.index(w) + k],
                             (x, y, 1 - c))
                fwd.wait_recv()
                fwd.wait_send()

    _, bufs, token = _split_call(f"gather_end_{grp}", bufs, list(sems), [], end, after=after)
    return bufs, token


def _pair_start(name, grads, extra=None):
    n = len(grads)
    bufs = list(grads) + [lax.empty((g.shape[0], g.shape[1] // 2, g.shape[2]), g.dtype) for g in grads]
    if extra is not None:
        bufs += [extra, lax.empty(extra.shape, extra.dtype)]

    def starts(refs, _, sems):
        x, y, c, _, _ = _place()
        for w in range(n):
            other = _half_rows(refs[w].shape[1], 1 - c)
            _rcopy(refs[w].at[:, other], refs[n + w], sems[0].at[w], sems[1].at[w], (x, y, 1 - c)).start()
        if extra is not None:
            _rcopy(refs[2 * n], refs[2 * n + 1], sems[0].at[n], sems[1].at[n], (x, y, 1 - c)).start()

    count = n + (extra is not None)
    return _split_call(name, bufs, [], [(count,), (count,)], starts)


def _pair_wait(name, n, bufs, sems, after):
    has_extra = len(bufs) > 2 * n

    def waits(refs, sems_in, _):
        x, y, c, _, _ = _place()
        for w in range(n):
            other = _half_rows(refs[w].shape[1], 1 - c)
            cp = _rcopy(refs[w].at[:, other], refs[n + w], sems_in[0].at[w], sems_in[1].at[w], (x, y, 1 - c))
            cp.wait_recv()
            cp.wait_send()
        if has_extra:
            cp = _rcopy(refs[2 * n], refs[2 * n + 1], sems_in[0].at[n], sems_in[1].at[n], (x, y, 1 - c))
            cp.wait_recv()
            cp.wait_send()

    _, bufs, token = _split_call(name, bufs, list(sems), [], waits, after=after)
    return bufs, token


def _chip_start(name, partials, small=None):
    n = len(partials)
    bufs = list(partials) + [lax.empty(p.shape, p.dtype) for p in partials] + ([] if small is None else [small])

    def starts(refs, _, sems):
        _, _, c, me, chips = _place()
        for w in range(n):
            for k, chip in enumerate(chips):
                cid = 2 * chip[0] + chip[1]
                _rcopy(refs[w].at[cid], refs[n + w].at[me], sems[0].at[3 * w + k], sems[1].at[3 * w + k],
                       (*chip, c)).start()
        if small is not None:
            for k, chip in enumerate(chips):
                _rcopy(refs[2 * n].at[me], refs[2 * n].at[me], sems[0].at[3 * n + k], sems[1].at[3 * n + k],
                       (*chip, c)).start()

    count = 3 * (n + (small is not None))
    return _split_call(name, bufs, [], [(count,), (count,)], starts)


def _chip_wait(name, n, bufs, sems, after):
    has_small = len(bufs) > 2 * n

    def waits(refs, sems_in, _):
        _, _, c, me, chips = _place()
        for w in range(n):
            for k, chip in enumerate(chips):
                cid = 2 * chip[0] + chip[1]
                cp = _rcopy(refs[w].at[cid], refs[n + w].at[cid], sems_in[0].at[3 * w + k],
                            sems_in[1].at[3 * w + k], (*chip, c))
                cp.wait_recv()
                cp.wait_send()
        if has_small:
            for k, chip in enumerate(chips):
                cid = 2 * chip[0] + chip[1]
                cp = _rcopy(refs[2 * n].at[me], refs[2 * n].at[cid], sems_in[0].at[3 * n + k],
                            sems_in[1].at[3 * n + k], (*chip, c))
                cp.wait_recv()
                cp.wait_send()

    _, bufs, token = _split_call(name, bufs, list(sems), [], waits, after=after)
    return bufs, token


def _small_pair_sum(idx, own, recv):
    rows = own.shape[0]

    def body(idx_ref, a_ref, b_ref, o_ref):
        o_ref[...] = a_ref[...] + b_ref[...]

    blk = pl.BlockSpec((rows, LANES), lambda i, s: (0, 0))
    return pl.pallas_call(
        body, name="small_pair_sum", out_shape=jax.ShapeDtypeStruct((N_CHIPS, rows, LANES), F32),
        grid_spec=pltpu.PrefetchScalarGridSpec(
            num_scalar_prefetch=1, grid=(1,), in_specs=[blk, blk],
            out_specs=pl.BlockSpec((None, rows, LANES), lambda i, s: (s[1], 0, 0))),
        compiler_params=_params(("arbitrary",)),
    )(idx, own, recv)


def _gather_weights(shards, small):
    nb = len(shards)

    def body(*refs):
        srcs, small_ref = refs[:nb], refs[nb]
        dsts, small_out = refs[nb + 1:2 * nb + 1], refs[2 * nb + 1]
        send_sems, recv_sems, local_sems = refs[2 * nb + 2:]
        x, y, c, me, chips = _place()
        sibling = (x, y, 1 - c)
        mine = [_half_rows(s.shape[0], c) for s in srcs]
        other = [_half_rows(s.shape[0], 1 - c) for s in srcs]

        local = [pltpu.make_async_copy(srcs[w], dsts[w].at[me], local_sems.at[w]) for w in range(nb)]
        local.append(pltpu.make_async_copy(small_ref, small_out.at[me], local_sems.at[nb]))
        for cp in local:
            cp.start()
        sends = []
        for w in range(nb):
            for k, chip in enumerate(chips):
                sends.append(_rcopy(srcs[w].at[mine[w]], dsts[w].at[me, mine[w]],
                                    send_sems.at[w, k], recv_sems.at[w, k], (*chip, c)))
        for k, chip in enumerate(chips):
            sends.append(_rcopy(small_ref, small_out.at[me], send_sems.at[nb, k], recv_sems.at[nb, k], (*chip, c)))
        for cp in sends:
            cp.start()
        passed = []
        for w in range(nb):
            for k, chip in enumerate(chips):
                cid = 2 * chip[0] + chip[1]
                landed = dsts[w].at[cid, mine[w]]
                _rcopy(landed, landed, send_sems.at[w, k], recv_sems.at[w, k], (*chip, c)).wait_recv()
                fwd = _rcopy(landed, landed, send_sems.at[w, 3 + k], recv_sems.at[w, 3 + k], sibling)
                fwd.start()
                passed.append(fwd)
        for k, chip in enumerate(chips):
            cid = 2 * chip[0] + chip[1]
            _rcopy(small_ref, small_out.at[cid], send_sems.at[nb, k], recv_sems.at[nb, k], (*chip, c)).wait_recv()
        for w in range(nb):
            for k, chip in enumerate(chips):
                cid = 2 * chip[0] + chip[1]
                landed = dsts[w].at[cid, other[w]]
                _rcopy(landed, landed, send_sems.at[w, 3 + k], recv_sems.at[w, 3 + k], sibling).wait_recv()
        for cp in sends + passed:
            cp.wait_send()
        for cp in local:
            cp.wait()

    out_shape = [jax.ShapeDtypeStruct((N_CHIPS,) + s.shape, s.dtype) for s in shards]
    out_shape.append(jax.ShapeDtypeStruct((N_CHIPS,) + small.shape, small.dtype))
    return pl.pallas_call(
        body, name="gather_weights", out_shape=out_shape,
        in_specs=[ANY] * (nb + 1), out_specs=[ANY] * (nb + 1),
        scratch_shapes=[pltpu.SemaphoreType.DMA((nb + 1, 6)), pltpu.SemaphoreType.DMA((nb + 1, 6)),
                        pltpu.SemaphoreType.DMA((nb + 1,))],
    )(*shards, small)


def _pair_exchange(grads, small):
    nb = len(grads)

    def body(*refs):
        srcs, small_ref = refs[:nb], refs[nb]
        dsts, small_out = refs[nb + 1:2 * nb + 1], refs[2 * nb + 1]
        send_sems, recv_sems, small_send, small_recv, local_sem = refs[2 * nb + 2:]
        x, y, c, _, _ = _place()
        sibling = (x, y, 1 - c)
        my_id = 4 * x + 2 * y + c
        local = pltpu.make_async_copy(small_ref, small_out.at[my_id], local_sem)
        local.start()
        sends = []
        for w in range(nb):
            other = _half_rows(srcs[w].shape[1], 1 - c)
            sends.append(_rcopy(srcs[w].at[:, other], dsts[w], send_sems.at[w], recv_sems.at[w], sibling))
        for r in range(1, N_DEV):
            to = (1 - x if r & 4 else x, 1 - y if r & 2 else y, 1 - c if r & 1 else c)
            sends.append(_rcopy(small_ref, small_out.at[my_id], small_send.at[r - 1], small_recv.at[r - 1], to))
        for cp in sends:
            cp.start()
        for w in range(nb):
            _rcopy(dsts[w], dsts[w], send_sems.at[w], recv_sems.at[w], sibling).wait_recv()
        for r in range(1, N_DEV):
            frm = (1 - x if r & 4 else x, 1 - y if r & 2 else y, 1 - c if r & 1 else c)
            frm_id = 4 * frm[0] + 2 * frm[1] + frm[2]
            _rcopy(small_ref, small_out.at[frm_id], small_send.at[r - 1], small_recv.at[r - 1], frm).wait_recv()
        for cp in sends:
            cp.wait_send()
        local.wait()

    out_shape = [jax.ShapeDtypeStruct((g.shape[0], g.shape[1] // 2, g.shape[2]), g.dtype) for g in grads]
    out_shape.append(jax.ShapeDtypeStruct((N_DEV,) + small.shape, small.dtype))
    return pl.pallas_call(
        body, name="pair_exchange", out_shape=out_shape,
        in_specs=[ANY] * (nb + 1), out_specs=[ANY] * (nb + 1),
        scratch_shapes=[pltpu.SemaphoreType.DMA((nb,)), pltpu.SemaphoreType.DMA((nb,)),
                        pltpu.SemaphoreType.DMA((N_DEV - 1,)), pltpu.SemaphoreType.DMA((N_DEV - 1,)),
                        pltpu.SemaphoreType.DMA(())],
    )(*grads, small)


def _chip_exchange(partials):
    nb = len(partials)

    def body(*refs):
        srcs, dsts = refs[:nb], refs[nb:2 * nb]
        send_sems, recv_sems = refs[2 * nb:]
        _, _, c, me, chips = _place()
        sends = []
        for w in range(nb):
            for k, chip in enumerate(chips):
                cid = 2 * chip[0] + chip[1]
                sends.append(_rcopy(srcs[w].at[cid], dsts[w].at[me], send_sems.at[w, k], recv_sems.at[w, k],
                                    (*chip, c)))
        for cp in sends:
            cp.start()
        for w in range(nb):
            for k, chip in enumerate(chips):
                cid = 2 * chip[0] + chip[1]
                _rcopy(srcs[w].at[cid], dsts[w].at[cid], send_sems.at[w, k], recv_sems.at[w, k],
                       (*chip, c)).wait_recv()
        for cp in sends:
            cp.wait_send()

    return pl.pallas_call(
        body, name="chip_exchange", out_shape=[jax.ShapeDtypeStruct(p.shape, p.dtype) for p in partials],
        in_specs=[ANY] * nb, out_specs=[ANY] * nb,
        scratch_shapes=[pltpu.SemaphoreType.DMA((nb, 3)), pltpu.SemaphoreType.DMA((nb, 3))],
    )(*partials)


def _pair_share(shards, late):
    nb = len(shards)

    def body(*refs):
        srcs, late_ref = refs[:nb], refs[nb]
        dsts, late_out = refs[nb + 1:2 * nb + 1], refs[2 * nb + 1]
        send_sems, recv_sems, late_send, late_recv, local_sem = refs[2 * nb + 2:]
        x, y, c, _, _ = _place()
        sibling = (x, y, 1 - c)
        my_id = 4 * x + 2 * y + c
        peer = lambda r: (1 - x if r & 4 else x, 1 - y if r & 2 else y, 1 - c if r & 1 else c)
        local = pltpu.make_async_copy(late_ref, late_out.at[my_id], local_sem)
        local.start()
        sends = []
        for w in range(nb):
            mine = _half_rows(dsts[w].shape[0], c)
            sends.append(_rcopy(srcs[w].at[mine], dsts[w].at[mine], send_sems.at[w], recv_sems.at[w], sibling))
        for r in range(1, N_DEV):
            sends.append(_rcopy(late_ref, late_out.at[my_id], late_send.at[r - 1], late_recv.at[r - 1], peer(r)))
        for cp in sends:
            cp.start()
        for w in range(nb):
            other = _half_rows(dsts[w].shape[0], 1 - c)
            _rcopy(srcs[w].at[other], dsts[w].at[other], send_sems.at[w], recv_sems.at[w], sibling).wait_recv()
        for r in range(1, N_DEV):
            frm = peer(r)
            _rcopy(late_ref, late_out.at[4 * frm[0] + 2 * frm[1] + frm[2]], late_send.at[r - 1],
                   late_recv.at[r - 1], frm).wait_recv()
        for cp in sends:
            cp.wait_send()
        local.wait()

    out_shape = [jax.ShapeDtypeStruct(h.shape, h.dtype) for h in shards]
    out_shape.append(jax.ShapeDtypeStruct((N_DEV,) + late.shape, late.dtype))
    return pl.pallas_call(
        body, name="pair_share", out_shape=out_shape,
        in_specs=[ANY] * (nb + 1), out_specs=[ANY] * (nb + 1),
        scratch_shapes=[pltpu.SemaphoreType.DMA((nb,)), pltpu.SemaphoreType.DMA((nb,)),
                        pltpu.SemaphoreType.DMA((N_DEV - 1,)), pltpu.SemaphoreType.DMA((N_DEV - 1,)),
                        pltpu.SemaphoreType.DMA(())],
        input_output_aliases={w: w for w in range(nb)},
    )(*shards, late)


def _row_tile(rows):
    return _pick(rows, 128, 64, 16, 8)


def _pair_sum(name, idx, grad, recv):
    n, half, cols = recv.shape
    tr = _row_tile(half)
    nrb = half // tr

    def body(idx_ref, g_ref, r_ref, o_ref):
        o_ref[...] = (g_ref[...] + r_ref[...]).astype(BF16)

    return pl.pallas_call(
        body, name=name, out_shape=jax.ShapeDtypeStruct(recv.shape, BF16),
        grid_spec=pltpu.PrefetchScalarGridSpec(
            num_scalar_prefetch=1, grid=(n, nrb),
            in_specs=[pl.BlockSpec((None, tr, cols), lambda j, i, s: (j, s[0] * nrb + i, 0)),
                      pl.BlockSpec((None, tr, cols), lambda j, i, s: (j, i, 0))],
            out_specs=pl.BlockSpec((None, tr, cols), lambda j, i, s: (j, i, 0))),
        compiler_params=_params(("parallel", "parallel")),
    )(idx, grad, recv)


def _final_sum(name, idx, grad, recv, chip_sums):
    _, half, cols = recv.shape
    tr = _row_tile(half)
    nrb = half // tr

    def body(idx_ref, g_ref, r_ref, p1_ref, p2_ref, p3_ref, o_ref):
        acc = g_ref[...] + r_ref[...]
        for p_ref in (p1_ref, p2_ref, p3_ref):
            acc = acc + p_ref[...].astype(F32)
        o_ref[...] = acc

    slot = lambda which: pl.BlockSpec((None, tr, cols), lambda i, s, which=which: (s[which], i, 0))
    return pl.pallas_call(
        body, name=name, out_shape=jax.ShapeDtypeStruct((2 * half, cols), F32),
        grid_spec=pltpu.PrefetchScalarGridSpec(
            num_scalar_prefetch=1, grid=(nrb,),
            in_specs=[pl.BlockSpec((None, tr, cols), lambda i, s: (s[1], s[0] * nrb + i, 0)),
                      slot(1), slot(2), slot(3), slot(4)],
            out_specs=pl.BlockSpec((tr, cols), lambda i, s: (s[0] * nrb + i, 0))),
        compiler_params=_params(("parallel",)),
    )(idx, grad, recv, chip_sums, chip_sums, chip_sums)


def _small_sum(name, packs):
    n, rows, _ = packs.shape

    def body(p_ref, o_ref):
        acc = p_ref[0]
        for k in range(1, n):
            acc = acc + p_ref[k]
        o_ref[...] = acc

    return pl.pallas_call(
        body, name=name, out_shape=jax.ShapeDtypeStruct((rows, LANES), F32),
        in_specs=[pl.BlockSpec(memory_space=pltpu.VMEM)], out_specs=pl.BlockSpec(memory_space=pltpu.VMEM),
        compiler_params=pltpu.CompilerParams(vmem_limit_bytes=VMEM_LIMIT),
    )(packs)


def _adamw(name, w, g, m, v):
    rows, cols = w.shape
    tr = rows if rows * cols * 4 <= (2 << 20) else _row_tile(rows)

    def body(w_ref, g_ref, m_ref, v_ref, g_out_ref, d_ref, nm_ref, nv_ref):
        gv = g_ref[...]
        g_out_ref[...] = gv
        m_new = ADAM_B1 * m_ref[...] + (1.0 - ADAM_B1) * gv
        v_new = ADAM_B2 * v_ref[...] + (1.0 - ADAM_B2) * (gv * gv)
        m_hat = m_new / (1.0 - ADAM_B1 ** ADAM_STEP)
        v_hat = v_new / (1.0 - ADAM_B2 ** ADAM_STEP)
        d_ref[...] = -ADAM_LR * (m_hat / (jnp.sqrt(v_hat) + ADAM_EPS) + ADAM_WD * w_ref[...])
        nm_ref[...] = m_new
        nv_ref[...] = v_new

    blk = pl.BlockSpec((tr, cols), lambda i: (i, 0))
    sds = jax.ShapeDtypeStruct((rows, cols), F32)
    return pl.pallas_call(
        body, name=name, grid=(rows // tr,), in_specs=[blk] * 4, out_specs=[blk] * 4, out_shape=[sds] * 4,
        compiler_params=_params(("parallel",)),
    )(w, g, m, v)


def _train_step(x, target, W, M, V):
    xi, yi, ci = lax.axis_index("x"), lax.axis_index("y"), lax.axis_index("c")
    me = 2 * xi + yi
    big = {n: W[n][0] for n in BIG}

    others = [jnp.where(jnp.int32(i) >= me, i + 1, i) for i in range(N_CHIPS - 1)]
    idx = jnp.stack([ci, me] + others).astype(jnp.int32)

    sharded_shapes = [W[n].shape[1:] for n in SMALL_SHARDED]
    small_pack = _pack([W[n][0] for n in SMALL_SHARDED], _pack_rows(sharded_shapes))
    lands = [_place_own_shard(f"place_{n}", idx, big[n]) for n in BIG]
    lands.append(lax.dynamic_update_slice(jnp.zeros((N_CHIPS,) + small_pack.shape, F32), small_pack[None],
                                          (me, 0, 0)))
    replicated = {n: (W[n].reshape(1, -1) if W[n].ndim <= 2 else W[n][0]) for n in SMALL_REPLICATED}

    early = ("w_out", "w_up", "w_down")
    small_late = "norm_mix_g"
    small_early = tuple(n for n in SMALL if n != small_late)
    global_shape = lambda n: ((W[n].shape[1], W[n].shape[2] * N_CHIPS) if n in SMALL_SHARDED else
                              tuple(W[n].shape) if W[n].ndim == 1 else tuple(W[n].shape[1:]))
    small_shapes = [global_shape(n) for n in small_early]

    def shard_major(n, g):
        if n == "w_in":
            return _w_in_local_to_shards(g)
        return g if g.ndim == 3 else g.reshape((N_CHIPS, -1) + g.shape[1:])

    class _SplitComm:
        def reduce_early(self, grads):
            self.e_sems, self.e_bufs, token = _pair_start("pair_start_early",
                                                          [shard_major(n, grads[n]) for n in early])
            return token

        def reduce_early_mid(self, after):
            n = len(early)
            bufs, _ = _pair_wait("pair_wait_early", n, self.e_bufs, self.e_sems, after)
            self.e_grads, self.e_recv = bufs[:n], bufs[n:2 * n]
            partial = [_pair_sum(f"pair_sum_{nm}", idx, g, r) for nm, g, r in zip(early, self.e_grads, self.e_recv)]
            self.e_sems, self.e_bufs, token = _chip_start("chip_start_early", partial)
            return token

        def reduce_late(self, grads):
            pack = _pack([grads[n] for n in small_early], _pack_rows(small_shapes))
            self.l_sems, self.l_bufs, token = _pair_start("pair_start_late", [shard_major("w_in", grads["w_in"])],
                                                          extra=pack)
            return token

        def reduce_late_mid(self, after):
            bufs, _ = _pair_wait("pair_wait_late", 1, self.l_bufs, self.l_sems, after)
            self.l_grads, self.l_recv = bufs[:1], bufs[1:2]
            partial = [_pair_sum("pair_sum_w_in", idx, bufs[0], bufs[1])]
            self.l_sems, self.l_bufs, token = _chip_start("chip_start_late", partial,
                                                          small=_small_pair_sum(idx, bufs[2], bufs[3]))
            return token

        def finish(self, after, late):
            n = len(early)
            bufs, _ = _chip_wait("chip_wait_early", n, self.e_bufs, self.e_sems, after)
            halves = {nm: _final_sum(f"final_sum_{nm}", idx, g, r, p)
                      for nm, g, r, p in zip(early, self.e_grads, self.e_recv, bufs[n:2 * n])}
            bufs, _ = _chip_wait("chip_wait_late", 1, self.l_bufs, self.l_sems, halves[early[-1]])
            halves["w_in"] = _final_sum("final_sum_w_in", idx, self.l_grads[0], self.l_recv[0], bufs[1])
            small = dict(zip(small_early, _unpack(_small_sum("small_sum", bufs[2]), small_shapes)))
            *whole, late_all = _pair_share([halves[nm] for nm in BIG], late)
            return dict(zip(BIG, whole)), small, _small_sum("late_sum", late_all)

        def begin(self):
            self.sems, self.lands, token = _gather_start(lands)
            return token

        def mid(self, grp, after):
            self.pending = _gather_mid(grp, self.lands, self.sems[grp], after)
            return self.pending[2]

        def end(self, grp, after):
            sems, bufs, _ = self.pending
            bufs, _ = _gather_end(grp, bufs, sems, after)
            if grp == 0:
                per_chip = [_unpack(bufs[1][j], sharded_shapes) for j in range(N_CHIPS)]
                out = {n: jnp.concatenate([per_chip[j][i] for j in range(N_CHIPS)], axis=1)
                       for i, n in enumerate(SMALL_SHARDED)}
                out["w_in"] = _w_in_shards_to_local(bufs[0])
                return out
            if grp == 1:
                return {"w_out": bufs[0].reshape(-1, bufs[0].shape[-1]), "w_up": bufs[1]}
            return {"w_down": bufs[0].reshape(-1, bufs[0].shape[-1])}

    comm = _SplitComm()
    loss, grad_x, grads = _local_step(x[0], target[0], replicated, comm)
    loss = lax.psum(loss[0, 0], ("x", "y", "c"))
    late = _pack([grads[small_late]], _pack_rows([global_shape(small_late)]))
    big_grads, small_grads, late_sum = comm.finish(grad_x, late)
    small_grads[small_late] = _unpack(late_sum, [global_shape(small_late)])[0]
    for n in SMALL_SHARDED:
        width = W[n].shape[2]
        small_grads[n] = lax.dynamic_slice_in_dim(small_grads[n], me * width, width, axis=1)

    out_g, out_d, out_m, out_v = {}, {}, {}, {}
    for n in BIG:
        g, d, nm, nv = _adamw(f"adamw_{n}", big[n], big_grads[n], M[n][0], V[n][0])
        out_g[n], out_d[n], out_m[n], out_v[n] = g[None], d[None], nm[None], nv[None]
    local_shapes = [tuple(W[n].shape) for n in SMALL]
    rows = _pack_rows(local_shapes)
    packed = [_pack([src[n] for n in SMALL], rows) for src in (W, small_grads, M, V)]
    _, d, nm, nv = _adamw("adamw_small", *packed)
    for dst, buf in ((out_d, d), (out_m, nm), (out_v, nv)):
        dst.update(zip(SMALL, _unpack(buf, local_shapes)))
    for n in SMALL:
        out_g[n] = small_grads[n].reshape(W[n].shape)
    return (loss, grad_x[None], *[out_g[n] for n in WEIGHT_NAMES], *[out_d[n] for n in WEIGHT_NAMES],
            *[out_m[n] for n in WEIGHT_NAMES], *[out_v[n] for n in WEIGHT_NAMES])


def kernel(x, norm_mix_g, w_in, b_gate_m, mlstm_norm_g, lru_conv_w, lru_conv_b, lru_wa, lru_ba, lru_wx, lru_bx, lru_lambda, w_out, norm_ffn_g, w_up, ffn_conv_w, ffn_conv_b, w_down, norm_final_g, loss_target, m_norm_mix_g, m_w_in, m_b_gate_m, m_mlstm_norm_g, m_lru_conv_w, m_lru_conv_b, m_lru_wa, m_lru_ba, m_lru_wx, m_lru_bx, m_lru_lambda, m_w_out, m_norm_ffn_g, m_w_up, m_ffn_conv_w, m_ffn_conv_b, m_w_down, m_norm_final_g, v_norm_mix_g, v_w_in, v_b_gate_m, v_mlstm_norm_g, v_lru_conv_w, v_lru_conv_b, v_lru_wa, v_lru_ba, v_lru_wx, v_lru_bx, v_lru_lambda, v_w_out, v_norm_ffn_g, v_w_up, v_ffn_conv_w, v_ffn_conv_b, v_w_down, v_norm_final_g):
    W = dict(zip(WEIGHT_NAMES, (norm_mix_g, w_in, b_gate_m, mlstm_norm_g, lru_conv_w, lru_conv_b, lru_wa, lru_ba,
                                lru_wx, lru_bx, lru_lambda, w_out, norm_ffn_g, w_up, ffn_conv_w, ffn_conv_b,
                                w_down, norm_final_g)))
    M = dict(zip(WEIGHT_NAMES, (m_norm_mix_g, m_w_in, m_b_gate_m, m_mlstm_norm_g, m_lru_conv_w, m_lru_conv_b,
                                m_lru_wa, m_lru_ba, m_lru_wx, m_lru_bx, m_lru_lambda, m_w_out, m_norm_ffn_g,
                                m_w_up, m_ffn_conv_w, m_ffn_conv_b, m_w_down, m_norm_final_g)))
    V = dict(zip(WEIGHT_NAMES, (v_norm_mix_g, v_w_in, v_b_gate_m, v_mlstm_norm_g, v_lru_conv_w, v_lru_conv_b,
                                v_lru_wa, v_lru_ba, v_lru_wx, v_lru_bx, v_lru_lambda, v_w_out, v_norm_ffn_g,
                                v_w_up, v_ffn_conv_w, v_ffn_conv_b, v_w_down, v_norm_final_g)))
    return _train_step(x, loss_target, W, M, V)
```

```python
import functools

import jax
import jax.numpy as jnp
from jax import lax
from jax.experimental import pallas as pl
from jax.experimental.pallas import tpu as pltpu

F32 = jnp.float32
BF16 = jnp.bfloat16
MESH = pl.DeviceIdType.MESH

EPS = 1e-6
CHUNK = 64
HEADS = 4
HEAD_DIM = 256
D_MLSTM = HEADS * HEAD_DIM
LRU_BLOCKS = 8
LRU_BLOCK_DIM = 128
D_LRU = LRU_BLOCKS * LRU_BLOCK_DIM
LRU_C = 8.0
LRU_CONV = 4
FFN_CONV = 3
ADAM_LR = 0.001
ADAM_B1 = 0.9
ADAM_B2 = 0.999
ADAM_EPS = 1e-08
ADAM_WD = 0.01
ADAM_STEP = 10

N_CHIPS = 4
N_DEV = 8
LANES = 128
HALO = 8
PROJ_GATE_PAD = LANES
_QKVO = 4 * D_MLSTM
_N_GATES = 2 * HEADS
_PROJ_COLS = _QKVO + _N_GATES + 2 * D_LRU
_PROJ_PAD = _QKVO + 2 * D_LRU + PROJ_GATE_PAD
VMEM_LIMIT = 48 * 1024 * 1024
ANY = pl.BlockSpec(memory_space=pl.ANY)


def _params(sem, vmem=VMEM_LIMIT):
    return pltpu.CompilerParams(dimension_semantics=sem, vmem_limit_bytes=vmem)


def _matmul(name, a, b, grid, a_spec, b_spec, o_spec, out_sds, contract, res=None, res_spec=None, after=None):
    nk = grid[2]
    acc_shape = tuple(d for d in o_spec.block_shape if d is not None)

    def body(*refs):
        refs = list(refs)
        a_ref, b_ref = refs[:2]
        r_ref = refs[2] if res is not None else None
        o_ref = refs[-1] if nk == 1 else refs[-2]
        acc_ref = None if nk == 1 else refs[-1]
        k = pl.program_id(2)

        def part():
            return lax.dot_general(a_ref[...], b_ref[...], (contract, ((), ())), preferred_element_type=F32)

        def finish(r):
            if r_ref is not None:
                r = r_ref[...] + r
            o_ref[...] = r.astype(o_ref.dtype)

        if nk == 1:
            finish(part())
            return

        @pl.when(k == 0)
        def _():
            acc_ref[...] = part()

        @pl.when(jnp.logical_and(k > 0, k < nk - 1))
        def _():
            acc_ref[...] += part()

        @pl.when(k == nk - 1)
        def _():
            finish(acc_ref[...] + part())

    in_specs = [a_spec, b_spec] + ([] if res is None else [res_spec]) + ([] if after is None else [ANY])
    args = (a, b) + (() if res is None else (res,)) + (() if after is None else (after,))
    if after is not None:
        inner = body
        body = lambda *refs: inner(*refs[:len(in_specs) - 1], *refs[len(in_specs):])
    return pl.pallas_call(
        body, out_shape=out_sds, grid=grid, in_specs=in_specs, out_specs=o_spec,
        scratch_shapes=[] if nk == 1 else [pltpu.VMEM(acc_shape, F32)], name=name,
        compiler_params=_params(("parallel", "parallel", "arbitrary")),
    )(*args)


NN = ((1,), (0,))
NT = ((1,), (1,))
TN = ((0,), (0,))


def _mm_nn(name, a, b, tm, tn, tk, out_dtype=F32, res=None):
    m, k = a.shape
    n = b.shape[1]
    return _matmul(name, a, b, (m // tm, n // tn, k // tk),
                   pl.BlockSpec((tm, tk), lambda i, j, kk: (i, kk)),
                   pl.BlockSpec((tk, tn), lambda i, j, kk: (kk, j)),
                   pl.BlockSpec((tm, tn), lambda i, j, kk: (i, j)),
                   jax.ShapeDtypeStruct((m, n), out_dtype), NN,
                   res=res, res_spec=pl.BlockSpec((tm, tn), lambda i, j, kk: (i, j)))


def _mm_nt(name, a, b, tm, tn, tk, out_dtype=F32, res=None, after=None):
    m, k = a.shape
    n = b.shape[0]
    return _matmul(name, a, b, (m // tm, n // tn, k // tk),
                   pl.BlockSpec((tm, tk), lambda i, j, kk: (i, kk)),
                   pl.BlockSpec((tn, tk), lambda i, j, kk: (j, kk)),
                   pl.BlockSpec((tm, tn), lambda i, j, kk: (i, j)),
                   jax.ShapeDtypeStruct((m, n), out_dtype), NT,
                   res=res, res_spec=pl.BlockSpec((tm, tn), lambda i, j, kk: (i, j)), after=after)


def _mm_tn(name, a, b, tm, tn, tk, out_dtype=F32):
    k, m = a.shape
    n = b.shape[1]
    tk = min(tk, k)
    return _matmul(name, a, b, (m // tm, n // tn, k // tk),
                   pl.BlockSpec((tk, tm), lambda i, j, kk: (kk, i)),
                   pl.BlockSpec((tk, tn), lambda i, j, kk: (kk, j)),
                   pl.BlockSpec((tm, tn), lambda i, j, kk: (i, j)),
                   jax.ShapeDtypeStruct((m, n), out_dtype), TN)


def _up_shard(n):
    return 2 * (n % 2) + (n // 2) // 2, (n // 2) % 2


def _mm_up_fwd(name, a, wg_up, tm, tk):
    m, k = a.shape
    _, _, cols = wg_up.shape
    tn = cols // 2
    return _matmul(name, a, wg_up, (m // tm, 2 * N_CHIPS, k // tk),
                   pl.BlockSpec((tm, tk), lambda i, j, kk: (i, kk)),
                   pl.BlockSpec((None, tk, tn), lambda i, j, kk: (_up_shard(j)[0], kk, _up_shard(j)[1])),
                   pl.BlockSpec((tm, tn), lambda i, j, kk: (i, j)),
                   jax.ShapeDtypeStruct((m, 2 * N_CHIPS * tn), F32), NN)


def _mm_up_bwd_x(name, dgu, wg_up, tm, tn):
    m, _ = dgu.shape
    _, d, cols = wg_up.shape
    tk = cols // 2
    return _matmul(name, dgu, wg_up, (m // tm, d // tn, 2 * N_CHIPS),
                   pl.BlockSpec((tm, tk), lambda i, j, kk: (i, kk)),
                   pl.BlockSpec((None, tn, tk), lambda i, j, kk: (_up_shard(kk)[0], j, _up_shard(kk)[1])),
                   pl.BlockSpec((tm, tn), lambda i, j, kk: (i, j)),
                   jax.ShapeDtypeStruct((m, d), F32), NT)


def _mm_up_bwd_w(name, n2, dgu, tm, tk):
    s, d = n2.shape
    tk = min(tk, s)
    tn = dgu.shape[1] // (2 * N_CHIPS)
    return _matmul(name, n2, dgu, (d // tm, 2 * N_CHIPS, s // tk),
                   pl.BlockSpec((tk, tm), lambda i, j, kk: (kk, i)),
                   pl.BlockSpec((tk, tn), lambda i, j, kk: (kk, j)),
                   pl.BlockSpec((None, tm, tn), lambda i, j, kk: (_up_shard(j)[0], i, _up_shard(j)[1])),
                   jax.ShapeDtypeStruct((N_CHIPS, d, 2 * tn), F32), TN)


def _rmsnorm_fwd(name, x, g, tm=256):
    s, d = x.shape

    def body(x_ref, g_ref, n_ref, r_ref):
        xf = x_ref[...]
        r = lax.rsqrt(jnp.mean(xf * xf, axis=-1, keepdims=True) + EPS)
        n_ref[...] = ((xf * r) * g_ref[...]).astype(BF16)
        r_ref[...] = r

    return pl.pallas_call(
        body, grid=(s // tm,), name=name,
        in_specs=[pl.BlockSpec((tm, d), lambda i: (i, 0)), pl.BlockSpec((1, d), lambda i: (0, 0))],
        out_specs=[pl.BlockSpec((tm, d), lambda i: (i, 0)), pl.BlockSpec((tm, 1), lambda i: (i, 0))],
        out_shape=[jax.ShapeDtypeStruct((s, d), BF16), jax.ShapeDtypeStruct((s, 1), F32)],
        compiler_params=_params(("parallel",)),
    )(x, g)


def _rmsnorm_bwd(name, x, rstd, g, dn, dres, tm=256):
    s, d = x.shape

    def body(x_ref, r_ref, g_ref, dn_ref, dres_ref, dx_ref, dxb_ref, dg_ref):
        @pl.when(pl.program_id(0) == 0)
        def _():
            dg_ref[...] = jnp.zeros_like(dg_ref)

        r = r_ref[...]
        xhat = x_ref[...] * r
        dn_v = dn_ref[...]
        dxhat = dn_v * g_ref[...]
        dx = dres_ref[...] + r * (dxhat - xhat * jnp.mean(dxhat * xhat, axis=-1, keepdims=True))
        dx_ref[...] = dx
        dxb_ref[...] = dx.astype(BF16)
        dg_ref[...] += jnp.sum(dn_v * xhat, axis=0, keepdims=True)

    row = pl.BlockSpec((tm, d), lambda i: (i, 0))
    vec = pl.BlockSpec((1, d), lambda i: (0, 0))
    return pl.pallas_call(
        body, grid=(s // tm,), name=name,
        in_specs=[row, pl.BlockSpec((tm, 1), lambda i: (i, 0)), vec, row, row],
        out_specs=[row, row, vec],
        out_shape=[jax.ShapeDtypeStruct((s, d), F32), jax.ShapeDtypeStruct((s, d), BF16),
                   jax.ShapeDtypeStruct((1, d), F32)],
        compiler_params=_params(("arbitrary",)),
    )(x, rstd, g, dn, dres)


def _loss_head(name, x, g, target, tm=256):
    s, d = x.shape

    def body(x_ref, g_ref, t_ref, loss_ref, dx_ref, dxb_ref, dg_ref):
        @pl.when(pl.program_id(0) == 0)
        def _():
            dg_ref[...] = jnp.zeros_like(dg_ref)
            loss_ref[...] = jnp.zeros_like(loss_ref)

        xf = x_ref[...]
        gv = g_ref[...]
        r = lax.rsqrt(jnp.mean(xf * xf, axis=-1, keepdims=True) + EPS)
        xhat = xf * r
        err = xhat * gv - t_ref[...]
        loss_ref[...] += 0.5 * jnp.sum(jnp.mean(err * err, axis=-1, keepdims=True), axis=0, keepdims=True)
        dy = err * (1.0 / d)
        dxhat = dy * gv
        dx = r * (dxhat - xhat * jnp.mean(dxhat * xhat, axis=-1, keepdims=True))
        dx_ref[...] = dx
        dxb_ref[...] = dx.astype(BF16)
        dg_ref[...] += jnp.sum(dy * xhat, axis=0, keepdims=True)

    row = pl.BlockSpec((tm, d), lambda i: (i, 0))
    vec = pl.BlockSpec((1, d), lambda i: (0, 0))
    return pl.pallas_call(
        body, grid=(s // tm,), name=name,
        in_specs=[row, vec, row],
        out_specs=[pl.BlockSpec((1, 1), lambda i: (0, 0)), row, row, vec],
        out_shape=[jax.ShapeDtypeStruct((1, 1), F32), jax.ShapeDtypeStruct((s, d), F32),
                   jax.ShapeDtypeStruct((s, d), BF16), jax.ShapeDtypeStruct((1, d), F32)],
        compiler_params=_params(("arbitrary",)),
    )(x, g, target)


def _sigmoid(v):
    return 1.0 / (1.0 + jnp.exp(-v))


def _log_sigmoid(v):
    return jnp.minimum(v, 0.0) - jnp.log1p(jnp.exp(-jnp.abs(v)))


def _softplus(v):
    return jnp.maximum(v, 0.0) + jnp.log1p(jnp.exp(-jnp.abs(v)))


def _one_minus_exp(z):
    series = -z * (1.0 + z * (0.5 + z * (1.0 / 6.0 + z * (1.0 / 24.0 + z * (1.0 / 120.0)))))
    return jnp.where(z > -0.1, series, 1.0 - jnp.exp(z))


_GELU_K = 0.7978845608028654
_GELU_C = 0.044715


def _gelu(v):
    return 0.5 * v * (1.0 + jnp.tanh(_GELU_K * (v + _GELU_C * v * v * v)))


def _gelu_grad(v):
    t = jnp.tanh(_GELU_K * (v + _GELU_C * v * v * v))
    return 0.5 * (1.0 + t) + 0.5 * v * (1.0 - t * t) * _GELU_K * (1.0 + 3.0 * _GELU_C * v * v)


def _rows(shape):
    return lax.broadcasted_iota(jnp.int32, shape, 0)


def _cols(shape):
    return lax.broadcasted_iota(jnp.int32, shape, 1)


def _shift_down(v, prev, d):
    if d == 0:
        return v
    rolled = pltpu.roll(v, d, axis=0)
    head = jnp.where(_rows((HALO, v.shape[1])) >= d, rolled[:HALO], pltpu.roll(prev, d, axis=0))
    if v.shape[0] == HALO:
        return head
    return jnp.concatenate([head, rolled[HALO:]], axis=0)


def _shift_up(v, nxt, d):
    if d == 0:
        return v
    n = v.shape[0]
    rolled = pltpu.roll(v, n - d, axis=0)
    tail = jnp.where(_rows((HALO, v.shape[1])) < HALO - d, rolled[n - HALO:], pltpu.roll(nxt, HALO - d, axis=0))
    if n == HALO:
        return tail
    return jnp.concatenate([rolled[:n - HALO], tail], axis=0)


def _dot(a, b, contract):
    return lax.dot_general(a.astype(BF16), b.astype(BF16), (contract, ((), ())), preferred_element_type=F32)


def _mlstm_chunk_common(h, q_ref, k_ref, v_ref, gcol_ref, grow_ref, brow_ref, bcol_ref, m_prev):
    L = CHUNK
    sl = slice(h * HEAD_DIM, (h + 1) * HEAD_DIM)
    qh = q_ref[:, sl]
    kh = k_ref[:, sl]
    vh = v_ref[:, sl]
    qs = qh * (HEAD_DIM ** -0.5)
    gates = gcol_ref[...] + brow_ref[...]
    lane = _cols(gates.shape)
    ic = jnp.sum(jnp.where(lane == h, gates, 0.0), axis=1, keepdims=True)
    fc = jnp.sum(jnp.where(lane == HEADS + h, gates, 0.0), axis=1, keepdims=True)
    ir = grow_ref[h:h + 1, :] + bcol_ref[h:h + 1, :]
    fr = grow_ref[HEADS + h:HEADS + h + 1, :] + bcol_ref[HEADS + h:HEADS + h + 1, :]
    logf_c = _log_sigmoid(fc)
    logf_r = _log_sigmoid(fr)
    t_i = _rows((L, L))
    s_i = _cols((L, L))
    tri = t_i >= s_i
    b_c = jnp.sum(jnp.where(tri, logf_r, 0.0), axis=1, keepdims=True)
    b_r = jnp.sum(jnp.where(t_i <= s_i, logf_c, 0.0), axis=0, keepdims=True)
    btot = jnp.sum(logf_r, axis=1, keepdims=True)
    dmat = jnp.where(tri, b_c - b_r + ir, -jnp.inf)
    m_inter = b_c + m_prev
    m_t = jnp.maximum(m_inter, jnp.max(dmat, axis=1, keepdims=True))
    e_mat = jnp.exp(dmat - m_t)
    e_inter = jnp.exp(m_inter - m_t)
    wqk = _dot(qs, kh, NT) * e_mat
    w_end_r = btot - b_r + ir
    m_loc = jnp.max(w_end_r, axis=1, keepdims=True)
    e_end_c = jnp.exp(btot - b_c + ic - m_loc)
    m_new = jnp.maximum(btot + m_prev, m_loc)
    a_dec = jnp.exp(btot + m_prev - m_new)
    c_inj = jnp.exp(m_loc - m_new)
    return dict(qh=qh, kh=kh, vh=vh, qs=qs, fc=fc, tri=tri, t_i=t_i, s_i=s_i, m_t=m_t, e_mat=e_mat,
                e_inter=e_inter, wqk=wqk, e_end_c=e_end_c, m_new=m_new, a_dec=a_dec, c_inj=c_inj)


def _mlstm_fwd(proj, gates_t, bias_row, bias_col, head_g):
    s = proj.shape[0]
    nc = s // CHUNK
    L = CHUNK

    def body(q_ref, k_ref, v_ref, o_ref, gcol_ref, grow_ref, brow_ref, bcol_ref, hg_ref,
             out_ref, cprev_ref, nprev_ref, mprev_ref, c_scr, n_scr, m_scr):
        @pl.when(pl.program_id(0) == 0)
        def _():
            c_scr[...] = jnp.zeros_like(c_scr)
            n_scr[...] = jnp.zeros_like(n_scr)
            m_scr[...] = jnp.zeros_like(m_scr)

        for h in range(HEADS):
            sl = slice(h * HEAD_DIM, (h + 1) * HEAD_DIM)
            m_prev = m_scr[h:h + 1, 0:1]
            n_prev = n_scr[h:h + 1, :]
            c_prev = c_scr[h].astype(BF16)
            q = _mlstm_chunk_common(h, q_ref, k_ref, v_ref, gcol_ref, grow_ref, brow_ref, bcol_ref, m_prev)
            num = _dot(q["wqk"], q["vh"], NN) + q["e_inter"] * _dot(q["qs"], c_prev, NN)
            den = (jnp.sum(q["wqk"], axis=1, keepdims=True)
                   + q["e_inter"] * jnp.sum(q["qs"] * n_prev, axis=1, keepdims=True))
            hh = num / jnp.maximum(jnp.abs(den), jnp.exp(-q["m_t"]))
            hn = hh * lax.rsqrt(jnp.mean(hh * hh, axis=1, keepdims=True) + EPS) * hg_ref[h:h + 1, :]
            out_ref[:, sl] = (_sigmoid(o_ref[:, sl]) * hn).astype(BF16)
            cprev_ref[h] = c_prev
            nprev_ref[h:h + 1, :] = n_prev
            mprev_ref[h:h + 1, :] = jnp.broadcast_to(m_prev, (1, LANES))
            c_loc = _dot(q["kh"], q["e_end_c"] * q["vh"], TN)
            n_loc = jnp.sum(q["e_end_c"] * q["kh"], axis=0, keepdims=True)
            c_scr[h] = q["a_dec"] * c_scr[h] + q["c_inj"] * c_loc
            n_scr[h:h + 1, :] = q["a_dec"] * n_prev + q["c_inj"] * n_loc
            m_scr[h:h + 1, :] = jnp.broadcast_to(q["m_new"], (1, LANES))

    blk = lambda j: pl.BlockSpec((L, D_MLSTM), lambda c, j=j: (c, j))
    full = lambda shp: pl.BlockSpec(shp, lambda c: tuple(0 for _ in shp))
    return pl.pallas_call(
        body, grid=(nc,), name="mlstm_fwd",
        in_specs=[blk(0), blk(1), blk(2), blk(3),
                  pl.BlockSpec((L, LANES), lambda c: (c, (4 * D_MLSTM + 2 * D_LRU) // LANES)),
                  pl.BlockSpec((None, 2 * HEADS, L), lambda c: (c, 0, 0)),
                  full((1, LANES)), full((2 * HEADS, 1)), full((HEADS, HEAD_DIM))],
        out_specs=[pl.BlockSpec((L, D_MLSTM), lambda c: (c, 0)),
                   pl.BlockSpec((None, HEADS, HEAD_DIM, HEAD_DIM), lambda c: (c, 0, 0, 0)),
                   pl.BlockSpec((None, HEADS, HEAD_DIM), lambda c: (c, 0, 0)),
                   pl.BlockSpec((None, HEADS, LANES), lambda c: (c, 0, 0))],
        out_shape=[jax.ShapeDtypeStruct((s, D_MLSTM + D_LRU), BF16),
                   jax.ShapeDtypeStruct((nc, HEADS, HEAD_DIM, HEAD_DIM), BF16),
                   jax.ShapeDtypeStruct((nc, HEADS, HEAD_DIM), F32),
                   jax.ShapeDtypeStruct((nc, HEADS, LANES), F32)],
        scratch_shapes=[pltpu.VMEM((HEADS, HEAD_DIM, HEAD_DIM), F32), pltpu.VMEM((HEADS, HEAD_DIM), F32),
                        pltpu.VMEM((HEADS, LANES), F32)],
        compiler_params=_params(("arbitrary",)),
    )(proj, proj, proj, proj, proj, gates_t, bias_row, bias_col, head_g)


def _mlstm_bwd(proj, gates_t, bias_row, bias_col, head_g, cprev, nprev, mprev, dmix):
    s = proj.shape[0]
    nc = s // CHUNK
    L = CHUNK

    def body(q_ref, k_ref, v_ref, o_ref, gcol_ref, grow_ref, brow_ref, bcol_ref, hg_ref,
             cprev_ref, nprev_ref, mprev_ref, dmix_ref,
             dqkvo_ref, dgate_ref, dhg_ref, g_scr, gn_scr):
        @pl.when(pl.program_id(0) == 0)
        def _():
            g_scr[...] = jnp.zeros_like(g_scr)
            gn_scr[...] = jnp.zeros_like(gn_scr)
            dhg_ref[...] = jnp.zeros_like(dhg_ref)

        lane = _cols((L, LANES))
        dgate = jnp.zeros((L, LANES), F32)
        for h in range(HEADS):
            sl = slice(h * HEAD_DIM, (h + 1) * HEAD_DIM)
            m_prev = mprev_ref[h:h + 1, 0:1]
            n_prev = nprev_ref[h:h + 1, :]
            c_prev = cprev_ref[h]
            q = _mlstm_chunk_common(h, q_ref, k_ref, v_ref, gcol_ref, grow_ref, brow_ref, bcol_ref, m_prev)
            qh, kh, vh, qs, wqk, e_inter = q["qh"], q["kh"], q["vh"], q["qs"], q["wqk"], q["e_inter"]
            num_state = e_inter * _dot(qs, c_prev, NN)
            den_state = e_inter * jnp.sum(qs * n_prev, axis=1, keepdims=True)
            num = _dot(wqk, vh, NN) + num_state
            den = jnp.sum(wqk, axis=1, keepdims=True) + den_state
            floor = jnp.exp(-q["m_t"])
            denom = jnp.maximum(jnp.abs(den), floor)
            hh = num / denom
            rn = lax.rsqrt(jnp.mean(hh * hh, axis=1, keepdims=True) + EPS)
            hn_pre = hh * rn
            hg = hg_ref[h:h + 1, :]
            sg = _sigmoid(o_ref[:, sl])
            dout = dmix_ref[:, sl]
            d_o = dout * (hn_pre * hg) * sg * (1.0 - sg)
            dhn = dout * sg
            dhg_ref[h:h + 1, :] += jnp.sum(dhn * hn_pre, axis=0, keepdims=True)
            dhn_pre = dhn * hg
            dhh = rn * (dhn_pre - hn_pre * jnp.mean(dhn_pre * hn_pre, axis=1, keepdims=True))
            dnum = dhh / denom
            dden = jnp.where(jnp.abs(den) >= floor,
                             -jnp.sum(hh * dhh, axis=1, keepdims=True) / denom * jnp.sign(den), 0.0)
            dwqk = _dot(dnum, vh, NT) + dden
            dv = _dot(wqk, dnum, TN)
            dp = dwqk * q["e_mat"]
            dqs = _dot(dp, kh, NN) + e_inter * (_dot(dnum, c_prev, NT) + dden * n_prev)
            dk = _dot(dp, qs, TN)
            g_next = g_scr[h]
            gn_next = gn_scr[h:h + 1, :]
            w_state = q["e_end_c"] * q["c_inj"]
            dk_state = w_state * (_dot(vh, g_next, NT) + gn_next)
            dk = dk + dk_state
            dv = dv + w_state * _dot(kh, g_next, NN)
            dq = dqs * (HEAD_DIM ** -0.5)
            eye = q["t_i"] == q["s_i"]
            to_row = lambda col: jnp.sum(jnp.where(eye, col, 0.0), axis=0, keepdims=True)
            to_col = lambda row: jnp.sum(jnp.where(eye, row, 0.0), axis=1, keepdims=True)
            g_pair = dwqk * wqk
            rs_in = jnp.sum(g_pair, axis=1, keepdims=True)
            cs_in_r = jnp.sum(g_pair, axis=0, keepdims=True)
            rs_state = (jnp.sum(dnum * num_state, axis=1, keepdims=True) + dden * den_state)
            cs_state = jnp.sum(kh * dk_state, axis=1, keepdims=True)
            di_c = to_col(cs_in_r) + cs_state
            through = q["a_dec"] * (jnp.sum(jnp.sum(g_next * c_prev.astype(F32), axis=1, keepdims=True),
                                            axis=0, keepdims=True)
                                    + jnp.sum(gn_next * n_prev, axis=1, keepdims=True))
            ends_here = to_row(rs_in + rs_state) - cs_in_r
            da_c = (jnp.sum(jnp.where(q["s_i"] >= q["t_i"], ends_here, 0.0), axis=1, keepdims=True)
                    + jnp.sum(jnp.where(q["s_i"] < q["t_i"], to_row(cs_state), 0.0), axis=1, keepdims=True)
                    + through)
            df_c = da_c * _sigmoid(-q["fc"])
            dgate = dgate + jnp.where(lane == h, di_c, 0.0) + jnp.where(lane == HEADS + h, df_c, 0.0)
            dqkvo_ref[:, sl] = dq.astype(BF16)
            dqkvo_ref[:, D_MLSTM + h * HEAD_DIM:D_MLSTM + (h + 1) * HEAD_DIM] = dk.astype(BF16)
            dqkvo_ref[:, 2 * D_MLSTM + h * HEAD_DIM:2 * D_MLSTM + (h + 1) * HEAD_DIM] = dv.astype(BF16)
            dqkvo_ref[:, 3 * D_MLSTM + h * HEAD_DIM:3 * D_MLSTM + (h + 1) * HEAD_DIM] = d_o.astype(BF16)
            g_scr[h] = q["a_dec"] * g_next + _dot(e_inter * qs, dnum, TN)
            gn_scr[h:h + 1, :] = q["a_dec"] * gn_next + jnp.sum(e_inter * qs * dden, axis=0, keepdims=True)
        dgate_ref[...] = dgate

    rev = lambda c: nc - 1 - c
    blk = lambda j: pl.BlockSpec((L, D_MLSTM), lambda c, j=j: (rev(c), j))
    full = lambda shp: pl.BlockSpec(shp, lambda c: tuple(0 for _ in shp))
    return pl.pallas_call(
        body, grid=(nc,), name="mlstm_bwd",
        in_specs=[blk(0), blk(1), blk(2), blk(3),
                  pl.BlockSpec((L, LANES), lambda c: (rev(c), (4 * D_MLSTM + 2 * D_LRU) // LANES)),
                  pl.BlockSpec((None, 2 * HEADS, L), lambda c: (rev(c), 0, 0)),
                  full((1, LANES)), full((2 * HEADS, 1)), full((HEADS, HEAD_DIM)),
                  pl.BlockSpec((None, HEADS, HEAD_DIM, HEAD_DIM), lambda c: (rev(c), 0, 0, 0)),
                  pl.BlockSpec((None, HEADS, HEAD_DIM), lambda c: (rev(c), 0, 0)),
                  pl.BlockSpec((None, HEADS, LANES), lambda c: (rev(c), 0, 0)),
                  pl.BlockSpec((L, D_MLSTM), lambda c: (rev(c), 0))],
        out_specs=[pl.BlockSpec((L, 4 * D_MLSTM), lambda c: (rev(c), 0)),
                   pl.BlockSpec((L, LANES), lambda c: (rev(c), 0)),
                   full((HEADS, HEAD_DIM))],
        out_shape=[jax.ShapeDtypeStruct((s, _PROJ_PAD), BF16),
                   jax.ShapeDtypeStruct((s, LANES), F32),
                   jax.ShapeDtypeStruct((HEADS, HEAD_DIM), F32)],
        scratch_shapes=[pltpu.VMEM((HEADS, HEAD_DIM, HEAD_DIM), F32), pltpu.VMEM((HEADS, HEAD_DIM), F32)],
        compiler_params=_params(("arbitrary",)),
    )(proj, proj, proj, proj, proj, gates_t, bias_row, bias_col, head_g, cprev, nprev, mprev, dmix)


def _lru_gates(xc, wa_ref, wx_ref, ba, bx, lam):
    r = _sigmoid(_dot(xc, wa_ref[...], NN) + ba)
    ig = _sigmoid(_dot(xc, wx_ref[...], NN) + bx)
    sp = _softplus(-lam)
    log_a = (-LRU_C * r) * sp
    a = jnp.exp(log_a)
    mult = jnp.sqrt(_one_minus_exp(2.0 * log_a))
    return r, ig, sp, a, mult


def _lru_conv(xr, prev, w_ref, b):
    xc = b + _shift_down(xr, prev, 3) * w_ref[0:1, :]
    for j in range(1, LRU_CONV):
        xc = xc + _shift_down(xr, prev, LRU_CONV - 1 - j) * w_ref[j:j + 1, :]
    return xc


def _lru_fwd(proj, mix, conv_w, conv_b, wa, wx, ba, bx, lam, tt=512):
    s = proj.shape[0]
    tt = min(tt, s)
    nt = s // tt
    B = LRU_BLOCK_DIM
    lru_col = 4 * D_MLSTM // B
    mix_col = D_MLSTM // B

    def body(xr_ref, gr_ref, cw_ref, cb_ref, wa_ref, wx_ref, ba_ref, bx_ref, lam_ref, mix_in_ref,
             out_ref, h_ref, prev_scr, hcar_scr):
        @pl.when(pl.program_id(1) == 0)
        def _():
            prev_scr[...] = jnp.zeros_like(prev_scr)
            hcar_scr[...] = jnp.zeros_like(hcar_scr)

        xr = xr_ref[...]
        xc = _lru_conv(xr, prev_scr[...], cw_ref, cb_ref[...])
        prev_scr[...] = xr[tt - HALO:, :]
        _, ig, _, a, mult = _lru_gates(xc, wa_ref, wx_ref, ba_ref[...], bx_ref[...], lam_ref[...])
        u = mult * (ig * xc)
        rows = _rows((tt, B))
        acc_a, acc_b = a, u
        d = 1
        while d < tt:
            keep = rows >= d
            sh_a = jnp.where(keep, pltpu.roll(acc_a, d, axis=0), 1.0)
            sh_b = jnp.where(keep, pltpu.roll(acc_b, d, axis=0), 0.0)
            acc_b = acc_a * sh_b + acc_b
            acc_a = acc_a * sh_a
            d *= 2
        hv = acc_b + acc_a * hcar_scr[0:1, :]
        hcar_scr[...] = jnp.broadcast_to(hv[tt - 1:tt, :], hcar_scr.shape)
        h_ref[...] = hv
        out_ref[...] = (hv * _gelu(gr_ref[...])).astype(BF16)

    chan = lambda rws: pl.BlockSpec((rws, B), lambda n, i: (0, n))
    return pl.pallas_call(
        body, grid=(LRU_BLOCKS, nt), name="lru_fwd",
        in_specs=[pl.BlockSpec((tt, B), lambda n, i: (i, lru_col + 2 * n)),
                  pl.BlockSpec((tt, B), lambda n, i: (i, lru_col + 2 * n + 1)),
                  chan(LRU_CONV), chan(1),
                  pl.BlockSpec((None, B, B), lambda n, i: (n, 0, 0)),
                  pl.BlockSpec((None, B, B), lambda n, i: (n, 0, 0)),
                  chan(1), chan(1), chan(1), ANY],
        out_specs=[pl.BlockSpec((tt, B), lambda n, i: (i, mix_col + n)), pl.BlockSpec((tt, B), lambda n, i: (i, n))],
        out_shape=[jax.ShapeDtypeStruct(mix.shape, BF16), jax.ShapeDtypeStruct((s, D_LRU), F32)],
        scratch_shapes=[pltpu.VMEM((HALO, B), F32), pltpu.VMEM((HALO, B), F32)],
        input_output_aliases={9: 0},
        compiler_params=_params(("parallel", "arbitrary")),
    )(proj, proj, conv_w, conv_b, wa, wx, ba, bx, lam, mix)


def _lru_bwd(proj, hsave, dmix, dproj, conv_w, conv_b, wa, wx, ba, bx, lam, tt=512):
    s = proj.shape[0]
    tt = min(tt, s)
    nt = s // tt
    B = LRU_BLOCK_DIM
    lru_col = 4 * D_MLSTM // B
    dmix_col = D_MLSTM // B
    hpb = tt // HALO

    def body(xr_ref, xprev_ref, gr_ref, h_ref, hprev_ref, dmix_ref, cw_ref, cb_ref, wa_ref, wx_ref,
             ba_ref, bx_ref, lam_ref, dproj_in_ref,
             dxg_ref, dcw_ref, dcb_ref, dwa_ref, dwx_ref, dba_ref, dbx_ref, dlam_ref,
             gcar_scr, acar_scr, dxc_scr):
        i = pl.program_id(1)
        first_tile = i == nt - 1

        @pl.when(i == 0)
        def _():
            gcar_scr[...] = jnp.zeros_like(gcar_scr)
            acar_scr[...] = jnp.zeros_like(acar_scr)
            dxc_scr[...] = jnp.zeros_like(dxc_scr)
            for ref in (dcw_ref, dcb_ref, dwa_ref, dwx_ref, dba_ref, dbx_ref, dlam_ref):
                ref[...] = jnp.zeros_like(ref)

        xr = xr_ref[...]
        xprev = jnp.where(first_tile, 0.0, xprev_ref[...])
        hprev = jnp.where(first_tile, 0.0, hprev_ref[...])
        lam = lam_ref[...]
        xc = _lru_conv(xr, xprev, cw_ref, cb_ref[...])
        r, ig, sp, a, mult = _lru_gates(xc, wa_ref, wx_ref, ba_ref[...], bx_ref[...], lam)
        gr = gr_ref[...]
        hv = h_ref[...]
        dout = dmix_ref[...]
        dxg_ref[:, B:] = (dout * hv * _gelu_grad(gr)).astype(BF16)
        dh = dout * _gelu(gr)
        rows = _rows((tt, B))
        acc_a = _shift_up(a, acar_scr[...], 1)
        acc_b = dh
        d = 1
        while d < tt:
            keep = rows < tt - d
            sh_a = jnp.where(keep, pltpu.roll(acc_a, tt - d, axis=0), 1.0)
            sh_b = jnp.where(keep, pltpu.roll(acc_b, tt - d, axis=0), 0.0)
            acc_b = acc_a * sh_b + acc_b
            acc_a = acc_a * sh_a
            d *= 2
        gv = acc_b + acc_a * gcar_scr[0:1, :]
        gcar_scr[...] = jnp.broadcast_to(gv[0:1, :], gcar_scr.shape)
        acar_scr[...] = jnp.broadcast_to(a[0:1, :], acar_scr.shape)
        h_before = _shift_down(hv, hprev, 1)
        da = gv * h_before
        dmult = gv * (ig * xc)
        dig = gv * mult * xc
        dxc = gv * mult * ig
        dlog_a = da * a - dmult * (a * a) / mult
        dr = dlog_a * (-LRU_C * sp)
        dlam_ref[...] += jnp.sum(dlog_a * (-LRU_C * r), axis=0, keepdims=True) * (-_sigmoid(-lam))
        dpre_r = dr * r * (1.0 - r)
        dpre_i = dig * ig * (1.0 - ig)
        dba_ref[...] += jnp.sum(dpre_r, axis=0, keepdims=True)
        dbx_ref[...] += jnp.sum(dpre_i, axis=0, keepdims=True)
        dwa_ref[...] += _dot(xc, dpre_r, TN)
        dwx_ref[...] += _dot(xc, dpre_i, TN)
        dxc = dxc + _dot(dpre_r, wa_ref[...], NT) + _dot(dpre_i, wx_ref[...], NT)
        dcb_ref[...] += jnp.sum(dxc, axis=0, keepdims=True)
        nxt = dxc_scr[...]
        dxr = jnp.zeros((tt, B), F32)
        for j in range(LRU_CONV):
            sft = LRU_CONV - 1 - j
            dcw_ref[j:j + 1, :] += jnp.sum(dxc * _shift_down(xr, xprev, sft), axis=0, keepdims=True)
            dxr = dxr + _shift_up(dxc, nxt, sft) * cw_ref[j:j + 1, :]
        dxc_scr[...] = dxc[:HALO, :]
        dxg_ref[:, :B] = dxr.astype(BF16)

    rev = lambda i: nt - 1 - i
    tile = lambda col, step: pl.BlockSpec((tt, B), lambda n, i: (rev(i), col + step * n))
    halo = lambda col, step: pl.BlockSpec(
        (HALO, B), lambda n, i: (jnp.maximum(rev(i) * hpb - 1, 0), col + step * n))
    chan = lambda rws: pl.BlockSpec((rws, B), lambda n, i: (0, n))
    wblk = pl.BlockSpec((None, B, B), lambda n, i: (n, 0, 0))
    return pl.pallas_call(
        body, grid=(LRU_BLOCKS, nt), name="lru_bwd",
        in_specs=[tile(lru_col, 2), halo(lru_col, 2), tile(lru_col + 1, 2), tile(0, 1), halo(0, 1),
                  tile(dmix_col, 1), chan(LRU_CONV), chan(1), wblk, wblk, chan(1), chan(1), chan(1), ANY],
        out_specs=[pl.BlockSpec((tt, 2 * B), lambda n, i: (rev(i), lru_col // 2 + n)),
                   chan(LRU_CONV), chan(1), wblk, wblk, chan(1), chan(1), chan(1)],
        out_shape=[jax.ShapeDtypeStruct(dproj.shape, BF16),
                   jax.ShapeDtypeStruct((LRU_CONV, D_LRU), F32), jax.ShapeDtypeStruct((1, D_LRU), F32),
                   jax.ShapeDtypeStruct((LRU_BLOCKS, B, B), F32), jax.ShapeDtypeStruct((LRU_BLOCKS, B, B), F32),
                   jax.ShapeDtypeStruct((1, D_LRU), F32), jax.ShapeDtypeStruct((1, D_LRU), F32),
                   jax.ShapeDtypeStruct((1, D_LRU), F32)],
        scratch_shapes=[pltpu.VMEM((HALO, B), F32), pltpu.VMEM((HALO, B), F32), pltpu.VMEM((HALO, B), F32)],
        input_output_aliases={13: 0},
        compiler_params=_params(("parallel", "arbitrary")),
    )(proj, proj, proj, hsave, hsave, dmix, conv_w, conv_b, wa, wx, ba, bx, lam, dproj)


def _ffn_conv(gp, prev, w_ref, b):
    g = b + _shift_down(gp, prev, 2) * w_ref[0:1, :]
    for j in range(1, FFN_CONV):
        g = g + _shift_down(gp, prev, FFN_CONV - 1 - j) * w_ref[j:j + 1, :]
    return g


def _ffn_act_fwd(gu, conv_w, conv_b, tt=256):
    s = gu.shape[0]
    tt = min(tt, s)
    d_ff = conv_w.shape[1]
    tc = d_ff // N_CHIPS
    hpb = tt // HALO

    def body(g_ref, gprev_ref, u_ref, w_ref, b_ref, act_ref):
        prev = jnp.where(pl.program_id(0) == 0, 0.0, gprev_ref[...])
        gate = _ffn_conv(g_ref[...], prev, w_ref, b_ref[...])
        act_ref[...] = (gate * _sigmoid(gate) * u_ref[...]).astype(BF16)

    return pl.pallas_call(
        body, grid=(s // tt, N_CHIPS), name="ffn_act_fwd",
        in_specs=[pl.BlockSpec((tt, tc), lambda i, j: (i, 2 * j)),
                  pl.BlockSpec((HALO, tc), lambda i, j: (jnp.maximum(i * hpb - 1, 0), 2 * j)),
                  pl.BlockSpec((tt, tc), lambda i, j: (i, 2 * j + 1)),
                  pl.BlockSpec((FFN_CONV, tc), lambda i, j: (0, j)),
                  pl.BlockSpec((1, tc), lambda i, j: (0, j))],
        out_specs=pl.BlockSpec((tt, tc), lambda i, j: (i, j)),
        out_shape=jax.ShapeDtypeStruct((s, d_ff), BF16),
        compiler_params=_params(("parallel", "parallel")),
    )(gu, gu, gu, conv_w, conv_b)


def _ffn_act_bwd(gu, dact, conv_w, conv_b, tt=256):
    s = gu.shape[0]
    tt = min(tt, s)
    nt = s // tt
    d_ff = conv_w.shape[1]
    tc = d_ff // N_CHIPS
    hpb = tt // HALO

    def dgate_of(gate, up, da):
        sg = _sigmoid(gate)
        return da * up * (sg * (1.0 + gate * (1.0 - sg))), da * (gate * sg)

    def body(g_ref, gprev_ref, gnext_ref, u_ref, unext_ref, da_ref, danext_ref, w_ref, b_ref,
             dgu_ref, dw_ref, db_ref):
        i = pl.program_id(1)

        @pl.when(i == 0)
        def _():
            dw_ref[...] = jnp.zeros_like(dw_ref)
            db_ref[...] = jnp.zeros_like(db_ref)

        gp = g_ref[...]
        prev = jnp.where(i == 0, 0.0, gprev_ref[...])
        bias = b_ref[...]
        gate = _ffn_conv(gp, prev, w_ref, bias)
        dgate, dup = dgate_of(gate, u_ref[...], da_ref[...])
        gate_n = _ffn_conv(gnext_ref[...], gp[tt - HALO:, :], w_ref, bias)
        dgate_n, _ = dgate_of(gate_n, unext_ref[...], danext_ref[...])
        dgate_n = jnp.where(i == nt - 1, 0.0, dgate_n)
        db_ref[...] += jnp.sum(dgate, axis=0, keepdims=True)
        dgp = jnp.zeros((tt, tc), F32)
        for j in range(FFN_CONV):
            sft = FFN_CONV - 1 - j
            dw_ref[j:j + 1, :] += jnp.sum(dgate * _shift_down(gp, prev, sft), axis=0, keepdims=True)
            dgp = dgp + _shift_up(dgate, dgate_n, sft) * w_ref[j:j + 1, :]
        dgu_ref[:, :tc] = dgp.astype(BF16)
        dgu_ref[:, tc:] = dup.astype(BF16)

    tile = lambda half: pl.BlockSpec((tt, tc), lambda j, i, half=half: (i, 2 * j + half))
    hprev = lambda half: pl.BlockSpec((HALO, tc), lambda j, i, half=half: (jnp.maximum(i * hpb - 1, 0), 2 * j + half))
    hnext = lambda half: pl.BlockSpec(
        (HALO, tc), lambda j, i, half=half: (jnp.minimum((i + 1) * hpb, nt * hpb - 1), 2 * j + half))
    return pl.pallas_call(
        body, grid=(N_CHIPS, nt), name="ffn_act_bwd",
        in_specs=[tile(0), hprev(0), hnext(0), tile(1), hnext(1),
                  pl.BlockSpec((tt, tc), lambda j, i: (i, j)),
                  pl.BlockSpec((HALO, tc), lambda j, i: (jnp.minimum((i + 1) * hpb, nt * hpb - 1), j)),
                  pl.BlockSpec((FFN_CONV, tc), lambda j, i: (0, j)),
                  pl.BlockSpec((1, tc), lambda j, i: (0, j))],
        out_specs=[pl.BlockSpec((tt, 2 * tc), lambda j, i: (i, j)),
                   pl.BlockSpec((FFN_CONV, tc), lambda j, i: (0, j)),
                   pl.BlockSpec((1, tc), lambda j, i: (0, j))],
        out_shape=[jax.ShapeDtypeStruct((s, 2 * d_ff), BF16),
                   jax.ShapeDtypeStruct((FFN_CONV, d_ff), F32), jax.ShapeDtypeStruct((1, d_ff), F32)],
        compiler_params=_params(("parallel", "arbitrary")),
    )(gu, gu, gu, gu, gu, dact, dact, conv_w, conv_b)


def _gate_grads(dgate, dproj, tm=512):
    s, n = dgate.shape
    tm = min(tm, s)

    def body(a_ref, dproj_in_ref, o_ref, dproj_ref):
        @pl.when(pl.program_id(0) == 0)
        def _():
            o_ref[...] = jnp.zeros_like(o_ref)
        a = a_ref[...]
        o_ref[...] += jnp.sum(a, axis=0, keepdims=True)
        dproj_ref[...] = a.astype(BF16)

    return pl.pallas_call(
        body, grid=(s // tm,), name="gate_grads",
        in_specs=[pl.BlockSpec((tm, n), lambda i: (i, 0)), ANY],
        out_specs=[pl.BlockSpec((1, n), lambda i: (0, 0)),
                   pl.BlockSpec((tm, n), lambda i: (i, (_QKVO + 2 * D_LRU) // LANES))],
        out_shape=[jax.ShapeDtypeStruct((1, n), F32), jax.ShapeDtypeStruct(dproj.shape, BF16)],
        input_output_aliases={1: 1},
        compiler_params=_params(("arbitrary",)),
    )(dgate, dproj)


def _pick(n, *cands):
    for c in cands:
        if n % c == 0:
            return c
    raise ValueError(f"no tile for {n}")


def _behind(a, token):
    return a if token is None else a + token[0:1, 0:1].astype(a.dtype).reshape((1,) * a.ndim)


class _Gathered:
    def __init__(self, w):
        self.w = w

    def begin(self):
        return None

    def mid(self, grp, after):
        return None

    def end(self, grp, after):
        return self.w

    def reduce_early(self, grads):
        return None

    def reduce_early_mid(self, after):
        return None

    def reduce_late(self, grads):
        return None

    def reduce_late_mid(self, after):
        return None


def _local_step(x, target, w, comm):
    s, d = x.shape
    nc = s // CHUNK
    tm = _pick(s, 1024, 512, 256)
    tn_proj = _pick(_PROJ_PAD, 896)
    gate_col = 4 * D_MLSTM + 2 * D_LRU
    w = dict(w)

    token = comm.begin()
    n1, rstd1 = _rmsnorm_fwd("norm_mix_fwd", x, _behind(w["norm_mix_g"], token))
    comm.mid(0, n1)
    w.update(comm.end(0, None))
    proj = _mm_nn("proj_fwd", n1, w["w_in"], tm, tn_proj, d)
    token = comm.mid(1, proj)
    gates = proj[:, gate_col:gate_col + 2 * HEADS]
    gates_t = gates.reshape(nc, CHUNK, 2 * HEADS).transpose(0, 2, 1)
    bias_row = _behind(jnp.pad(w["b_gate_m"], ((0, 0), (0, LANES - 2 * HEADS))), token)
    bias_col = w["b_gate_m"].reshape(2 * HEADS, 1)
    mix, cprev, nprev, mprev = _mlstm_fwd(proj, gates_t, bias_row, bias_col, w["mlstm_norm_g"])
    mix, hsave = _lru_fwd(proj, mix, w["lru_conv_w"], w["lru_conv_b"], w["lru_wa"], w["lru_wx"],
                          w["lru_ba"], w["lru_bx"], w["lru_lambda"])
    w.update(comm.end(1, hsave))
    x1 = _mm_nn("out_fwd", mix, w["w_out"], tm, 1024, d, res=x)
    n2, rstd2 = _rmsnorm_fwd("norm_ffn_fwd", x1, w["norm_ffn_g"])
    token = comm.mid(2, n2)
    gu = _mm_up_fwd("up_fwd", n2, w["w_up"], tm, d)
    act = _ffn_act_fwd(gu, w["ffn_conv_w"], _behind(w["ffn_conv_b"], token))
    w.update(comm.end(2, act))
    d_ff = w["w_down"].shape[0]
    x2 = _mm_nn("down_fwd", act, w["w_down"], min(tm, 512), 1024, d_ff // 2, res=x1)
    loss, dx2, dx2b, g_norm_final = _loss_head("loss_head", x2, w["norm_final_g"], target)

    grads = {"norm_final_g": g_norm_final}
    dact = _mm_nt("down_bwd_x", dx2b, w["w_down"], tm, d_ff // N_CHIPS, d)
    grads["w_down"] = _mm_tn("down_bwd_w", act, dx2b, d_ff // N_CHIPS, 1024, 2048)
    dgu, grads["ffn_conv_w"], grads["ffn_conv_b"] = _ffn_act_bwd(gu, dact, w["ffn_conv_w"], w["ffn_conv_b"])
    dn2 = _mm_up_bwd_x("up_bwd_x", dgu, w["w_up"], tm, 1024)
    grads["w_up"] = _mm_up_bwd_w("up_bwd_w", n2, dgu, 1024, 2048)
    dx1, dx1b, grads["norm_ffn_g"] = _rmsnorm_bwd("norm_ffn_bwd", x1, rstd2, w["norm_ffn_g"], dn2, dx2)
    dmix = _mm_nt("out_bwd_x", dx1b, w["w_out"], tm, 1024, d)
    grads["w_out"] = _mm_tn("out_bwd_w", mix, dx1b, 1024, 1024, 2048)
    token = comm.reduce_early(grads)
    dproj, dgate, grads["mlstm_norm_g"] = _mlstm_bwd(proj, gates_t, _behind(bias_row, token), bias_col,
                                                     w["mlstm_norm_g"], cprev, nprev, mprev, dmix)
    token = comm.reduce_early_mid(dproj)
    (dproj, grads["lru_conv_w"], grads["lru_conv_b"], grads["lru_wa"], grads["lru_wx"],
     grads["lru_ba"], grads["lru_bx"], grads["lru_lambda"]) = _lru_bwd(
        proj, hsave, dmix, dproj, w["lru_conv_w"], _behind(w["lru_conv_b"], token), w["lru_wa"], w["lru_wx"],
        w["lru_ba"], w["lru_bx"], w["lru_lambda"])
    gate_bias_grad, dproj = _gate_grads(dgate, dproj)
    grads["b_gate_m"] = gate_bias_grad[:, :2 * HEADS]
    grads["w_in"] = _mm_tn("proj_bwd_w", n1, dproj, 1024, tn_proj, 2048)
    token = comm.reduce_late(grads)
    dn1 = _mm_nt("proj_bwd_x", dproj, w["w_in"], tm, 512, _PROJ_PAD, after=token)
    token = comm.reduce_late_mid(dn1)
    grad_x, _, grads["norm_mix_g"] = _rmsnorm_bwd("norm_mix_bwd", x, rstd1, _behind(w["norm_mix_g"], token),
                                                  dn1, dx1)
    return loss, grad_x, grads


WEIGHT_NAMES = ("norm_mix_g", "w_in", "b_gate_m", "mlstm_norm_g", "lru_conv_w", "lru_conv_b", "lru_wa", "lru_ba",
                "lru_wx", "lru_bx", "lru_lambda", "w_out", "norm_ffn_g", "w_up", "ffn_conv_w", "ffn_conv_b",
                "w_down", "norm_final_g")
BIG = ("w_in", "w_out", "w_up", "w_down")
SMALL_SHARDED = ("mlstm_norm_g", "lru_conv_w", "ffn_conv_w")
SMALL = tuple(n for n in WEIGHT_NAMES if n not in BIG)
SMALL_REPLICATED = tuple(n for n in SMALL if n not in SMALL_SHARDED)


def _proj_segments():
    segs = [(0, 0, _QKVO), (_QKVO, _QKVO + 2 * D_LRU, _N_GATES)]
    for n in range(LRU_BLOCKS):
        segs.append((_QKVO + _N_GATES + n * LRU_BLOCK_DIM, _QKVO + 2 * n * LRU_BLOCK_DIM, LRU_BLOCK_DIM))
        segs.append((_QKVO + _N_GATES + D_LRU + n * LRU_BLOCK_DIM, _QKVO + (2 * n + 1) * LRU_BLOCK_DIM,
                     LRU_BLOCK_DIM))
    return segs


def _w_in_shards_to_local(shards):
    width = shards.shape[2]
    pieces = []
    for g0, _, n in sorted(_proj_segments(), key=lambda s: s[1]):
        at = g0
        while at < g0 + n:
            j = at // width
            stop = min(g0 + n, (j + 1) * width)
            pieces.append(shards[j][:, at - j * width:stop - j * width])
            at = stop
    pieces.append(jnp.zeros((shards.shape[1], PROJ_GATE_PAD - _N_GATES), shards.dtype))
    return jnp.concatenate(pieces, axis=1)


def _w_in_local_to_shards(w):
    width = _PROJ_COLS // N_CHIPS
    shards = []
    for j in range(N_CHIPS):
        pieces = []
        for g0, l0, n in sorted(_proj_segments()):
            lo, hi = max(g0, j * width), min(g0 + n, (j + 1) * width)
            if lo < hi:
                pieces.append(w[:, l0 + lo - g0:l0 + hi - g0])
        shards.append(jnp.concatenate(pieces, axis=1))
    return jnp.stack(shards)


def _w_in_to_global(w):
    sh = _w_in_local_to_shards(w)
    return jnp.concatenate([sh[j] for j in range(N_CHIPS)], axis=1)


def _size(shp):
    return functools.reduce(lambda a, b: a * b, shp, 1)


def _lane_dense(shp):
    return len(shp) >= 2 and shp[-1] == LANES and _size(shp) % (HALO * LANES) == 0


def _pack_rows(shapes):
    loose = sum(_size(shp) for shp in shapes if not _lane_dense(shp))
    return sum(_size(shp) // LANES for shp in shapes if _lane_dense(shp)) + -(-loose // (HALO * LANES)) * HALO


def _pack(arrs, rows):
    del rows
    parts = [a.reshape(-1, LANES).astype(F32) for a in arrs if _lane_dense(a.shape)]
    loose = [a.reshape(-1).astype(F32) for a in arrs if not _lane_dense(a.shape)]
    if loose:
        flat = jnp.concatenate(loose)
        n = -(-flat.shape[0] // (HALO * LANES)) * HALO * LANES
        parts.append(jnp.pad(flat, (0, n - flat.shape[0])).reshape(-1, LANES))
    return parts[0] if len(parts) == 1 else jnp.concatenate(parts, axis=0)


def _unpack(buf, shapes):
    out, row = {}, 0
    for i, shp in enumerate(shapes):
        if _lane_dense(shp):
            n = _size(shp) // LANES
            out[i] = buf[row:row + n].reshape(shp)
            row += n
    flat, at = buf[row:].reshape(-1), 0
    for i, shp in enumerate(shapes):
        if not _lane_dense(shp):
            out[i] = flat[at:at + _size(shp)].reshape(shp)
            at += _size(shp)
    return [out[i] for i in range(len(shapes))]


def _assemble_weights(g_in, g_out, g_up, g_down, small_sharded, replicated):
    w = dict(replicated)
    w["w_in"] = _w_in_shards_to_local(g_in)
    w["w_out"] = g_out.reshape(-1, g_out.shape[-1])
    w["w_up"] = g_up
    w["w_down"] = g_down.reshape(-1, g_down.shape[-1])
    for name, v in small_sharded.items():
        w[name] = jnp.concatenate([v[j] for j in range(N_CHIPS)], axis=1)
    return w


def _full_weights_from_global(weights):
    shard = lambda a, axis: jnp.stack(jnp.split(a, N_CHIPS, axis=axis))
    rep = {n: weights[n].reshape(1, -1) if weights[n].ndim <= 2 and n != "b_gate_m" else weights[n]
           for n in SMALL_REPLICATED}
    rep["b_gate_m"] = weights["b_gate_m"].reshape(1, -1)
    return _assemble_weights(shard(weights["w_in"], 1).astype(BF16), shard(weights["w_out"], 0).astype(BF16),
                             shard(weights["w_up"], 1).astype(BF16), shard(weights["w_down"], 0).astype(BF16),
                             {n: shard(weights[n], 1) for n in SMALL_SHARDED}, rep)


def _grads_to_global(grads):
    g = dict(grads)
    g["w_in"] = _w_in_to_global(grads["w_in"])
    g["w_up"] = jnp.concatenate([grads["w_up"][j] for j in range(N_CHIPS)], axis=1)
    return g


def _place():
    x, y, c = lax.axis_index("x"), lax.axis_index("y"), lax.axis_index("c")
    chips = [(1 - x, y), (x, 1 - y), (1 - x, 1 - y)]
    return x, y, c, 2 * x + y, chips


def _half_rows(n_rows, which):
    half = n_rows // 2
    return pl.ds(pl.multiple_of(which * half, 16), half)


def _rcopy(src, dst, send_sem, recv_sem, to):
    return pltpu.make_async_remote_copy(src_ref=src, dst_ref=dst, send_sem=send_sem, recv_sem=recv_sem,
                                        device_id=to, device_id_type=MESH)


HBM_SPEC = pl.BlockSpec(memory_space=pltpu.HBM)
SEM_SPEC = pl.BlockSpec(memory_space=pltpu.SEMAPHORE)
TOKEN_SHAPE = (8, LANES)


def _split_call(name, bufs, sems_in, sems_out_shapes, body_fn, after=None):
    nb, ni, no = len(bufs), len(sems_in), len(sems_out_shapes)

    def body(*refs):
        buf_refs = refs[:nb]
        sem_in_refs = refs[nb:nb + ni]
        outs = refs[nb + ni + (0 if after is None else 1):]
        sem_out_refs = outs[:no]
        token_ref = outs[no + nb]
        body_fn(buf_refs, sem_in_refs, sem_out_refs)
        token_ref[...] = jnp.zeros_like(token_ref)

    out_shape = ([pltpu.SemaphoreType.DMA(shp) for shp in sems_out_shapes]
                 + [pltpu.HBM(b.shape, b.dtype) for b in bufs] + [jax.ShapeDtypeStruct(TOKEN_SHAPE, F32)])
    res = pl.pallas_call(
        body, name=name, out_shape=out_shape,
        in_specs=[HBM_SPEC] * nb + [SEM_SPEC] * ni + ([] if after is None else [ANY]),
        out_specs=[SEM_SPEC] * no + [HBM_SPEC] * nb + [pl.BlockSpec(memory_space=pltpu.VMEM)],
        input_output_aliases={i: no + i for i in range(nb)},
        compiler_params=pltpu.CompilerParams(has_side_effects=pltpu.SideEffectType.DATAFLOW_SIDE_EFFECTING),
    )(*[pltpu.with_memory_space_constraint(b, pltpu.HBM) for b in bufs], *sems_in,
      *(() if after is None else (after,)))
    return list(res[:no]), list(res[no:no + nb]), res[no + nb]


def _place_own_shard(name, idx, shard, after=None):
    rows, cols = shard.shape
    tr = _row_tile(rows)

    def body(idx_ref, s_ref, *rest):
        rest[-1][...] = s_ref[...].astype(BF16)

    return pl.pallas_call(
        body, name=name, out_shape=jax.ShapeDtypeStruct((N_CHIPS, rows, cols), BF16),
        grid_spec=pltpu.PrefetchScalarGridSpec(
            num_scalar_prefetch=1, grid=(rows // tr,),
            in_specs=[pl.BlockSpec((tr, cols), lambda i, s: (i, 0))] + ([] if after is None else [ANY]),
            out_specs=pl.BlockSpec((None, tr, cols), lambda i, s: (s[1], i, 0))),
        compiler_params=_params(("parallel",)),
    )(idx, shard, *(() if after is None else (after,)))


GATHER_GROUPS = ((0, 4), (1, 2), (3,))


def _gather_start(name, lands, groups, after=None):
    members = [w for g in groups for w in GATHER_GROUPS[g]]

    def starts(bufs, _, sems):
        x, y, c, me, chips = _place()
        for gi, g in enumerate(groups):
            for pos, w in enumerate(GATHER_GROUPS[g]):
                buf = bufs[members.index(w)]
                part = buf.at[me] if w == 4 else buf.at[me, _half_rows(buf.shape[1], c)]
                for k, chip in enumerate(chips):
                    _rcopy(part, part, sems[2 * gi].at[3 * pos + k], sems[2 * gi + 1].at[3 * pos + k],
                           (*chip, c)).start()

    shapes = []
    for g in groups:
        shapes += [(3 * len(GATHER_GROUPS[g]),)] * 2
    sems, bufs, token = _split_call(name, [lands[w] for w in members], [], shapes, starts, after=after)
    return ({g: (sems[2 * gi], sems[2 * gi + 1]) for gi, g in enumerate(groups)},
            dict(zip(members, bufs)), token)


def _gather_mid(grp, lands, sems, after):
    members = GATHER_GROUPS[grp]
    big = [w for w in members if w != 4]

    def mid(bufs, sems_in, sems_out):
        x, y, c, me, chips = _place()
        send_sems, recv_sems = sems_in
        for pos, w in enumerate(members):
            for k, chip in enumerate(chips):
                cid = 2 * chip[0] + chip[1]
                buf = bufs[pos]
                mine = buf.at[me] if w == 4 else buf.at[me, _half_rows(buf.shape[1], c)]
                theirs = buf.at[cid] if w == 4 else buf.at[cid, _half_rows(buf.shape[1], c)]
                arrival = _rcopy(mine, theirs, send_sems.at[3 * pos + k], recv_sems.at[3 * pos + k], (*chip, c))
                arrival.wait_recv()
                arrival.wait_send()
                if w != 4:
                    _rcopy(theirs, theirs, sems_out[0].at[3 * big.index(w) + k],
                           sems_out[1].at[3 * big.index(w) + k], (x, y, 1 - c)).start()

    new_sems, bufs, token = _split_call(f"gather_mid_{grp}", [lands[w] for w in members], list(sems),
                                        [(3 * len(big),), (3 * len(big),)], mid, after=after)
    return new_sems, bufs, token


def _gather_end(grp, bufs, sems, after):
    members = GATHER_GROUPS[grp]
    big = [w for w in members if w != 4]

    def end(refs, sems_in, _):
        x, y, c, me, chips = _place()
        send_sems, recv_sems = sems_in
        for pos, w in enumerate(members):
            if w == 4:
                continue
            for k, chip in enumerate(chips):
                cid = 2 * chip[0] + chip[1]
                buf = refs[pos]
                sent = buf.at[cid, _half_rows(buf.shape[1], c)]
                landed = buf.at[cid, _half_rows(buf.shape[1], 1 - c)]
                fwd = _rcopy(sent, landed, send_sems.at[3 * big.index(w) + k], recv_sems.at[3 * big.index(w) + k],
                             (x, y, 1 - c))
                fwd.wait_recv()
                fwd.wait_send()

    _, bufs, token = _split_call(f"gather_end_{grp}", bufs, list(sems), [], end, after=after)
    return bufs, token


def _pair_start(name, grads, extra=None):
    n = len(grads)
    bufs = list(grads) + [lax.empty((g.shape[0], g.shape[1] // 2, g.shape[2]), g.dtype) for g in grads]
    if extra is not None:
        bufs += [extra, lax.empty(extra.shape, extra.dtype)]

    def starts(refs, _, sems):
        x, y, c, _, _ = _place()
        for w in range(n):
            other = _half_rows(refs[w].shape[1], 1 - c)
            _rcopy(refs[w].at[:, other], refs[n + w], sems[0].at[w], sems[1].at[w], (x, y, 1 - c)).start()
        if extra is not None:
            _rcopy(refs[2 * n], refs[2 * n + 1], sems[0].at[n], sems[1].at[n], (x, y, 1 - c)).start()

    count = n + (extra is not None)
    return _split_call(name, bufs, [], [(count,), (count,)], starts)


def _pair_wait(name, n, bufs, sems, after):
    has_extra = len(bufs) > 2 * n

    def waits(refs, sems_in, _):
        x, y, c, _, _ = _place()
        for w in range(n):
            other = _half_rows(refs[w].shape[1], 1 - c)
            cp = _rcopy(refs[w].at[:, other], refs[n + w], sems_in[0].at[w], sems_in[1].at[w], (x, y, 1 - c))
            cp.wait_recv()
            cp.wait_send()
        if has_extra:
            cp = _rcopy(refs[2 * n], refs[2 * n + 1], sems_in[0].at[n], sems_in[1].at[n], (x, y, 1 - c))
            cp.wait_recv()
            cp.wait_send()

    _, bufs, token = _split_call(name, bufs, list(sems), [], waits, after=after)
    return bufs, token


def _chip_start(name, partials, small=None):
    n = len(partials)
    bufs = list(partials) + [lax.empty(p.shape, p.dtype) for p in partials] + ([] if small is None else [small])

    def starts(refs, _, sems):
        _, _, c, me, chips = _place()
        for w in range(n):
            for k, chip in enumerate(chips):
                cid = 2 * chip[0] + chip[1]
                _rcopy(refs[w].at[cid], refs[n + w].at[me], sems[0].at[3 * w + k], sems[1].at[3 * w + k],
                       (*chip, c)).start()
        if small is not None:
            for k, chip in enumerate(chips):
                _rcopy(refs[2 * n].at[me], refs[2 * n].at[me], sems[0].at[3 * n + k], sems[1].at[3 * n + k],
                       (*chip, c)).start()

    count = 3 * (n + (small is not None))
    return _split_call(name, bufs, [], [(count,), (count,)], starts)


def _chip_wait(name, n, bufs, sems, after):
    has_small = len(bufs) > 2 * n

    def waits(refs, sems_in, _):
        _, _, c, me, chips = _place()
        for w in range(n):
            for k, chip in enumerate(chips):
                cid = 2 * chip[0] + chip[1]
                cp = _rcopy(refs[w].at[cid], refs[n + w].at[cid], sems_in[0].at[3 * w + k],
                            sems_in[1].at[3 * w + k], (*chip, c))
                cp.wait_recv()
                cp.wait_send()
        if has_small:
            for k, chip in enumerate(chips):
                cid = 2 * chip[0] + chip[1]
                cp = _rcopy(refs[2 * n].at[me], refs[2 * n].at[cid], sems_in[0].at[3 * n + k],
                            sems_in[1].at[3 * n + k], (*chip, c))
                cp.wait_recv()
                cp.wait_send()

    _, bufs, token = _split_call(name, bufs, list(sems), [], waits, after=after)
    return bufs, token


def _small_pair_sum(idx, own, recv):
    rows = own.shape[0]

    def body(idx_ref, a_ref, b_ref, o_ref):
        o_ref[...] = a_ref[...] + b_ref[...]

    blk = pl.BlockSpec((rows, LANES), lambda i, s: (0, 0))
    return pl.pallas_call(
        body, name="small_pair_sum", out_shape=jax.ShapeDtypeStruct((N_CHIPS, rows, LANES), F32),
        grid_spec=pltpu.PrefetchScalarGridSpec(
            num_scalar_prefetch=1, grid=(1,), in_specs=[blk, blk],
            out_specs=pl.BlockSpec((None, rows, LANES), lambda i, s: (s[1], 0, 0))),
        compiler_params=_params(("arbitrary",)),
    )(idx, own, recv)


def _gather_weights(shards, small):
    nb = len(shards)

    def body(*refs):
        srcs, small_ref = refs[:nb], refs[nb]
        dsts, small_out = refs[nb + 1:2 * nb + 1], refs[2 * nb + 1]
        send_sems, recv_sems, local_sems = refs[2 * nb + 2:]
        x, y, c, me, chips = _place()
        sibling = (x, y, 1 - c)
        mine = [_half_rows(s.shape[0], c) for s in srcs]
        other = [_half_rows(s.shape[0], 1 - c) for s in srcs]

        local = [pltpu.make_async_copy(srcs[w], dsts[w].at[me], local_sems.at[w]) for w in range(nb)]
        local.append(pltpu.make_async_copy(small_ref, small_out.at[me], local_sems.at[nb]))
        for cp in local:
            cp.start()
        sends = []
        for w in range(nb):
            for k, chip in enumerate(chips):
                sends.append(_rcopy(srcs[w].at[mine[w]], dsts[w].at[me, mine[w]],
                                    send_sems.at[w, k], recv_sems.at[w, k], (*chip, c)))
        for k, chip in enumerate(chips):
            sends.append(_rcopy(small_ref, small_out.at[me], send_sems.at[nb, k], recv_sems.at[nb, k], (*chip, c)))
        for cp in sends:
            cp.start()
        passed = []
        for w in range(nb):
            for k, chip in enumerate(chips):
                cid = 2 * chip[0] + chip[1]
                landed = dsts[w].at[cid, mine[w]]
                _rcopy(landed, landed, send_sems.at[w, k], recv_sems.at[w, k], (*chip, c)).wait_recv()
                fwd = _rcopy(landed, landed, send_sems.at[w, 3 + k], recv_sems.at[w, 3 + k], sibling)
                fwd.start()
                passed.append(fwd)
        for k, chip in enumerate(chips):
            cid = 2 * chip[0] + chip[1]
            _rcopy(small_ref, small_out.at[cid], send_sems.at[nb, k], recv_sems.at[nb, k], (*chip, c)).wait_recv()
        for w in range(nb):
            for k, chip in enumerate(chips):
                cid = 2 * chip[0] + chip[1]
                landed = dsts[w].at[cid, other[w]]
                _rcopy(landed, landed, send_sems.at[w, 3 + k], recv_sems.at[w, 3 + k], sibling).wait_recv()
        for cp in sends + passed:
            cp.wait_send()
        for cp in local:
            cp.wait()

    out_shape = [jax.ShapeDtypeStruct((N_CHIPS,) + s.shape, s.dtype) for s in shards]
    out_shape.append(jax.ShapeDtypeStruct((N_CHIPS,) + small.shape, small.dtype))
    return pl.pallas_call(
        body, name="gather_weights", out_shape=out_shape,
        in_specs=[ANY] * (nb + 1), out_specs=[ANY] * (nb + 1),
        scratch_shapes=[pltpu.SemaphoreType.DMA((nb + 1, 6)), pltpu.SemaphoreType.DMA((nb + 1, 6)),
                        pltpu.SemaphoreType.DMA((nb + 1,))],
    )(*shards, small)


def _pair_exchange(grads, small):
    nb = len(grads)

    def body(*refs):
        srcs, small_ref = refs[:nb], refs[nb]
        dsts, small_out = refs[nb + 1:2 * nb + 1], refs[2 * nb + 1]
        send_sems, recv_sems, small_send, small_recv, local_sem = refs[2 * nb + 2:]
        x, y, c, _, _ = _place()
        sibling = (x, y, 1 - c)
        my_id = 4 * x + 2 * y + c
        local = pltpu.make_async_copy(small_ref, small_out.at[my_id], local_sem)
        local.start()
        sends = []
        for w in range(nb):
            other = _half_rows(srcs[w].shape[1], 1 - c)
            sends.append(_rcopy(srcs[w].at[:, other], dsts[w], send_sems.at[w], recv_sems.at[w], sibling))
        for r in range(1, N_DEV):
            to = (1 - x if r & 4 else x, 1 - y if r & 2 else y, 1 - c if r & 1 else c)
            sends.append(_rcopy(small_ref, small_out.at[my_id], small_send.at[r - 1], small_recv.at[r - 1], to))
        for cp in sends:
            cp.start()
        for w in range(nb):
            _rcopy(dsts[w], dsts[w], send_sems.at[w], recv_sems.at[w], sibling).wait_recv()
        for r in range(1, N_DEV):
            frm = (1 - x if r & 4 else x, 1 - y if r & 2 else y, 1 - c if r & 1 else c)
            frm_id = 4 * frm[0] + 2 * frm[1] + frm[2]
            _rcopy(small_ref, small_out.at[frm_id], small_send.at[r - 1], small_recv.at[r - 1], frm).wait_recv()
        for cp in sends:
            cp.wait_send()
        local.wait()

    out_shape = [jax.ShapeDtypeStruct((g.shape[0], g.shape[1] // 2, g.shape[2]), g.dtype) for g in grads]
    out_shape.append(jax.ShapeDtypeStruct((N_DEV,) + small.shape, small.dtype))
    return pl.pallas_call(
        body, name="pair_exchange", out_shape=out_shape,
        in_specs=[ANY] * (nb + 1), out_specs=[ANY] * (nb + 1),
        scratch_shapes=[pltpu.SemaphoreType.DMA((nb,)), pltpu.SemaphoreType.DMA((nb,)),
                        pltpu.SemaphoreType.DMA((N_DEV - 1,)), pltpu.SemaphoreType.DMA((N_DEV - 1,)),
                        pltpu.SemaphoreType.DMA(())],
    )(*grads, small)


def _chip_exchange(partials):
    nb = len(partials)

    def body(*refs):
        srcs, dsts = refs[:nb], refs[nb:2 * nb]
        send_sems, recv_sems = refs[2 * nb:]
        _, _, c, me, chips = _place()
        sends = []
        for w in range(nb):
            for k, chip in enumerate(chips):
                cid = 2 * chip[0] + chip[1]
                sends.append(_rcopy(srcs[w].at[cid], dsts[w].at[me], send_sems.at[w, k], recv_sems.at[w, k],
                                    (*chip, c)))
        for cp in sends:
            cp.start()
        for w in range(nb):
            for k, chip in enumerate(chips):
                cid = 2 * chip[0] + chip[1]
                _rcopy(srcs[w].at[cid], dsts[w].at[cid], send_sems.at[w, k], recv_sems.at[w, k],
                       (*chip, c)).wait_recv()
        for cp in sends:
            cp.wait_send()

    return pl.pallas_call(
        body, name="chip_exchange", out_shape=[jax.ShapeDtypeStruct(p.shape, p.dtype) for p in partials],
        in_specs=[ANY] * nb, out_specs=[ANY] * nb,
        scratch_shapes=[pltpu.SemaphoreType.DMA((nb, 3)), pltpu.SemaphoreType.DMA((nb, 3))],
    )(*partials)


def _pair_share(name, shards, late=None):
    nb = len(shards)
    nl = 0 if late is None else 1

    def body(*refs):
        srcs = refs[:nb]
        dsts = refs[nb + nl:2 * nb + nl]
        send_sems, recv_sems = refs[2 * nb + 2 * nl:2 * nb + 2 * nl + 2]
        x, y, c, _, _ = _place()
        sibling = (x, y, 1 - c)
        sends = []
        for w in range(nb):
            mine = _half_rows(dsts[w].shape[0], c)
            sends.append(_rcopy(srcs[w].at[mine], dsts[w].at[mine], send_sems.at[w], recv_sems.at[w], sibling))
        if nl:
            late_ref, late_out = refs[nb], refs[2 * nb + 1]
            late_send, late_recv, local_sem = refs[2 * nb + 4:]
            my_id = 4 * x + 2 * y + c
            peer = lambda r: (1 - x if r & 4 else x, 1 - y if r & 2 else y, 1 - c if r & 1 else c)
            local = pltpu.make_async_copy(late_ref, late_out.at[my_id], local_sem)
            local.start()
            for r in range(1, N_DEV):
                sends.append(_rcopy(late_ref, late_out.at[my_id], late_send.at[r - 1], late_recv.at[r - 1],
                                    peer(r)))
        for cp in sends:
            cp.start()
        for w in range(nb):
            other = _half_rows(dsts[w].shape[0], 1 - c)
            _rcopy(srcs[w].at[other], dsts[w].at[other], send_sems.at[w], recv_sems.at[w], sibling).wait_recv()
        if nl:
            for r in range(1, N_DEV):
                frm = peer(r)
                _rcopy(late_ref, late_out.at[4 * frm[0] + 2 * frm[1] + frm[2]], late_send.at[r - 1],
                       late_recv.at[r - 1], frm).wait_recv()
        for cp in sends:
            cp.wait_send()
        if nl:
            local.wait()

    out_shape = [jax.ShapeDtypeStruct(h.shape, h.dtype) for h in shards]
    scratch = [pltpu.SemaphoreType.DMA((nb,)), pltpu.SemaphoreType.DMA((nb,))]
    if nl:
        out_shape.append(jax.ShapeDtypeStruct((N_DEV,) + late.shape, late.dtype))
        scratch += [pltpu.SemaphoreType.DMA((N_DEV - 1,)), pltpu.SemaphoreType.DMA((N_DEV - 1,)),
                    pltpu.SemaphoreType.DMA(())]
    return pl.pallas_call(
        body, name=name, out_shape=out_shape,
        in_specs=[ANY] * (nb + nl), out_specs=[ANY] * (nb + nl), scratch_shapes=scratch,
        input_output_aliases={w: w for w in range(nb)},
    )(*shards, *(() if late is None else (late,)))


def _row_tile(rows):
    return _pick(rows, 128, 64, 16, 8)


def _pair_sum(name, idx, grad, recv):
    n, half, cols = recv.shape
    tr = _row_tile(half)
    nrb = half // tr

    def body(idx_ref, g_ref, r_ref, o_ref):
        o_ref[...] = (g_ref[...] + r_ref[...]).astype(BF16)

    return pl.pallas_call(
        body, name=name, out_shape=jax.ShapeDtypeStruct(recv.shape, BF16),
        grid_spec=pltpu.PrefetchScalarGridSpec(
            num_scalar_prefetch=1, grid=(n, nrb),
            in_specs=[pl.BlockSpec((None, tr, cols), lambda j, i, s: (j, s[0] * nrb + i, 0)),
                      pl.BlockSpec((None, tr, cols), lambda j, i, s: (j, i, 0))],
            out_specs=pl.BlockSpec((None, tr, cols), lambda j, i, s: (j, i, 0))),
        compiler_params=_params(("parallel", "parallel")),
    )(idx, grad, recv)


def _final_sum(name, idx, grad, recv, chip_sums):
    _, half, cols = recv.shape
    tr = _row_tile(half)
    nrb = half // tr

    def body(idx_ref, g_ref, r_ref, p1_ref, p2_ref, p3_ref, o_ref):
        acc = g_ref[...] + r_ref[...]
        for p_ref in (p1_ref, p2_ref, p3_ref):
            acc = acc + p_ref[...].astype(F32)
        o_ref[...] = acc

    slot = lambda which: pl.BlockSpec((None, tr, cols), lambda i, s, which=which: (s[which], i, 0))
    return pl.pallas_call(
        body, name=name, out_shape=jax.ShapeDtypeStruct((2 * half, cols), F32),
        grid_spec=pltpu.PrefetchScalarGridSpec(
            num_scalar_prefetch=1, grid=(nrb,),
            in_specs=[pl.BlockSpec((None, tr, cols), lambda i, s: (s[1], s[0] * nrb + i, 0)),
                      slot(1), slot(2), slot(3), slot(4)],
            out_specs=pl.BlockSpec((tr, cols), lambda i, s: (s[0] * nrb + i, 0))),
        compiler_params=_params(("parallel",)),
    )(idx, grad, recv, chip_sums, chip_sums, chip_sums)


def _small_sum(name, packs):
    n, rows, _ = packs.shape

    def body(p_ref, o_ref):
        acc = p_ref[0]
        for k in range(1, n):
            acc = acc + p_ref[k]
        o_ref[...] = acc

    return pl.pallas_call(
        body, name=name, out_shape=jax.ShapeDtypeStruct((rows, LANES), F32),
        in_specs=[pl.BlockSpec(memory_space=pltpu.VMEM)], out_specs=pl.BlockSpec(memory_space=pltpu.VMEM),
        compiler_params=pltpu.CompilerParams(vmem_limit_bytes=VMEM_LIMIT),
    )(packs)


def _adamw(name, w, g, m, v):
    rows, cols = w.shape
    tr = rows if rows * cols * 4 <= (2 << 20) else _row_tile(rows)

    def body(w_ref, g_ref, m_ref, v_ref, g_out_ref, d_ref, nm_ref, nv_ref):
        gv = g_ref[...]
        g_out_ref[...] = gv
        m_new = ADAM_B1 * m_ref[...] + (1.0 - ADAM_B1) * gv
        v_new = ADAM_B2 * v_ref[...] + (1.0 - ADAM_B2) * (gv * gv)
        m_hat = m_new / (1.0 - ADAM_B1 ** ADAM_STEP)
        v_hat = v_new / (1.0 - ADAM_B2 ** ADAM_STEP)
        d_ref[...] = -ADAM_LR * (m_hat / (jnp.sqrt(v_hat) + ADAM_EPS) + ADAM_WD * w_ref[...])
        nm_ref[...] = m_new
        nv_ref[...] = v_new

    blk = pl.BlockSpec((tr, cols), lambda i: (i, 0))
    sds = jax.ShapeDtypeStruct((rows, cols), F32)
    return pl.pallas_call(
        body, name=name, grid=(rows // tr,), in_specs=[blk] * 4, out_specs=[blk] * 4, out_shape=[sds] * 4,
        compiler_params=_params(("parallel",)),
    )(w, g, m, v)


def _train_step(x, target, W, M, V):
    xi, yi, ci = lax.axis_index("x"), lax.axis_index("y"), lax.axis_index("c")
    me = 2 * xi + yi
    big = {n: W[n][0] for n in BIG}

    others = [jnp.where(jnp.int32(i) >= me, i + 1, i) for i in range(N_CHIPS - 1)]
    idx = jnp.stack([ci, me] + others).astype(jnp.int32)

    sharded_shapes = [W[n].shape[1:] for n in SMALL_SHARDED]
    small_pack = _pack([W[n][0] for n in SMALL_SHARDED], _pack_rows(sharded_shapes))
    small_land = lax.dynamic_update_slice(jnp.zeros((N_CHIPS,) + small_pack.shape, F32), small_pack[None],
                                          (me, 0, 0))
    replicated = {n: (W[n].reshape(1, -1) if W[n].ndim <= 2 else W[n][0]) for n in SMALL_REPLICATED}

    early = ("w_out", "w_up", "w_down")
    small_late = "norm_mix_g"
    small_early = tuple(n for n in SMALL if n != small_late)
    global_shape = lambda n: ((W[n].shape[1], W[n].shape[2] * N_CHIPS) if n in SMALL_SHARDED else
                              tuple(W[n].shape) if W[n].ndim == 1 else tuple(W[n].shape[1:]))
    small_shapes = [global_shape(n) for n in small_early]

    def shard_major(n, g):
        if n == "w_in":
            return _w_in_local_to_shards(g)
        return g if g.ndim == 3 else g.reshape((N_CHIPS, -1) + g.shape[1:])

    class _SplitComm:
        def reduce_early(self, grads):
            self.e_sems, self.e_bufs, token = _pair_start("pair_start_early",
                                                          [shard_major(n, grads[n]) for n in early])
            return token

        def reduce_early_mid(self, after):
            n = len(early)
            bufs, _ = _pair_wait("pair_wait_early", n, self.e_bufs, self.e_sems, after)
            self.e_grads, self.e_recv = bufs[:n], bufs[n:2 * n]
            partial = [_pair_sum(f"pair_sum_{nm}", idx, g, r) for nm, g, r in zip(early, self.e_grads, self.e_recv)]
            self.e_sems, self.e_bufs, token = _chip_start("chip_start_early", partial)
            return token

        def reduce_late(self, grads):
            pack = _pack([grads[n] for n in small_early], _pack_rows(small_shapes))
            self.l_sems, self.l_bufs, token = _pair_start("pair_start_late", [shard_major("w_in", grads["w_in"])],
                                                          extra=pack)
            return token

        def reduce_late_mid(self, after):
            bufs, _ = _pair_wait("pair_wait_late", 1, self.l_bufs, self.l_sems, after)
            self.l_grads, self.l_recv = bufs[:1], bufs[1:2]
            partial = [_pair_sum("pair_sum_w_in", idx, bufs[0], bufs[1])]
            self.l_sems, self.l_bufs, token = _chip_start("chip_start_late", partial,
                                                          small=_small_pair_sum(idx, bufs[2], bufs[3]))
            return token

        def finish_early(self, after):
            n = len(early)
            bufs, _ = _chip_wait("chip_wait_early", n, self.e_bufs, self.e_sems, after)
            halves = [_final_sum(f"final_sum_{nm}", idx, g, r, p)
                      for nm, g, r, p in zip(early, self.e_grads, self.e_recv, bufs[n:2 * n])]
            return dict(zip(early, _pair_share("pair_share_early", halves)))

        def finish_late(self, after, late):
            bufs, _ = _chip_wait("chip_wait_late", 1, self.l_bufs, self.l_sems, after)
            half = _final_sum("final_sum_w_in", idx, self.l_grads[0], self.l_recv[0], bufs[1])
            small = dict(zip(small_early, _unpack(_small_sum("small_sum", bufs[2]), small_shapes)))
            whole, late_all = _pair_share("pair_share_late", [half], late)
            return whole, small, _small_sum("late_sum", late_all)

        def begin(self):
            first = {0: _place_own_shard("place_w_in", idx, big["w_in"]), 4: small_land}
            self.sems, self.lands, token = _gather_start("gather_start_0", first, (0,))
            rest = {i: _place_own_shard(f"place_{BIG[i]}", idx, big[BIG[i]], after=token) for i in (1, 2, 3)}
            sems, lands, token = _gather_start("gather_start_1", rest, (1, 2), after=token)
            self.sems.update(sems)
            self.lands.update(lands)
            return token

        def mid(self, grp, after):
            self.pending = _gather_mid(grp, self.lands, self.sems[grp], after)
            return self.pending[2]

        def end(self, grp, after):
            sems, bufs, _ = self.pending
            bufs, _ = _gather_end(grp, bufs, sems, after)
            if grp == 0:
                per_chip = [_unpack(bufs[1][j], sharded_shapes) for j in range(N_CHIPS)]
                out = {n: jnp.concatenate([per_chip[j][i] for j in range(N_CHIPS)], axis=1)
                       for i, n in enumerate(SMALL_SHARDED)}
                out["w_in"] = _w_in_shards_to_local(bufs[0])
                return out
            if grp == 1:
                return {"w_out": bufs[0].reshape(-1, bufs[0].shape[-1]), "w_up": bufs[1]}
            return {"w_down": bufs[0].reshape(-1, bufs[0].shape[-1])}

    comm = _SplitComm()
    loss, grad_x, grads = _local_step(x[0], target[0], replicated, comm)
    loss = lax.psum(loss[0, 0], ("x", "y", "c"))
    out_g, out_d, out_m, out_v = {}, {}, {}, {}

    def update_big(n, grad):
        g, d, nm, nv = _adamw(f"adamw_{n}", big[n], grad, M[n][0], V[n][0])
        out_g[n], out_d[n], out_m[n], out_v[n] = g[None], d[None], nm[None], nv[None]
        return d

    early_grads = comm.finish_early(grad_x)
    for n in early:
        last = update_big(n, early_grads[n])
    late = _pack([grads[small_late]], _pack_rows([global_shape(small_late)]))
    w_in_grad, small_grads, late_sum = comm.finish_late(last, late)
    update_big("w_in", w_in_grad)
    small_grads[small_late] = _unpack(late_sum, [global_shape(small_late)])[0]
    for n in SMALL_SHARDED:
        width = W[n].shape[2]
        small_grads[n] = lax.dynamic_slice_in_dim(small_grads[n], me * width, width, axis=1)

    local_shapes = [tuple(W[n].shape) for n in SMALL]
    rows = _pack_rows(local_shapes)
    packed = [_pack([src[n] for n in SMALL], rows) for src in (W, small_grads, M, V)]
    _, d, nm, nv = _adamw("adamw_small", *packed)
    for dst, buf in ((out_d, d), (out_m, nm), (out_v, nv)):
        dst.update(zip(SMALL, _unpack(buf, local_shapes)))
    for n in SMALL:
        out_g[n] = small_grads[n].reshape(W[n].shape)
    return (loss, grad_x[None], *[out_g[n] for n in WEIGHT_NAMES], *[out_d[n] for n in WEIGHT_NAMES],
            *[out_m[n] for n in WEIGHT_NAMES], *[out_v[n] for n in WEIGHT_NAMES])


def kernel(x, norm_mix_g, w_in, b_gate_m, mlstm_norm_g, lru_conv_w, lru_conv_b, lru_wa, lru_ba, lru_wx, lru_bx, lru_lambda, w_out, norm_ffn_g, w_up, ffn_conv_w, ffn_conv_b, w_down, norm_final_g, loss_target, m_norm_mix_g, m_w_in, m_b_gate_m, m_mlstm_norm_g, m_lru_conv_w, m_lru_conv_b, m_lru_wa, m_lru_ba, m_lru_wx, m_lru_bx, m_lru_lambda, m_w_out, m_norm_ffn_g, m_w_up, m_ffn_conv_w, m_ffn_conv_b, m_w_down, m_norm_final_g, v_norm_mix_g, v_w_in, v_b_gate_m, v_mlstm_norm_g, v_lru_conv_w, v_lru_conv_b, v_lru_wa, v_lru_ba, v_lru_wx, v_lru_bx, v_lru_lambda, v_w_out, v_norm_ffn_g, v_w_up, v_ffn_conv_w, v_ffn_conv_b, v_w_down, v_norm_final_g):
    W = dict(zip(WEIGHT_NAMES, (norm_mix_g, w_in, b_gate_m, mlstm_norm_g, lru_conv_w, lru_conv_b, lru_wa, lru_ba,
                                lru_wx, lru_bx, lru_lambda, w_out, norm_ffn_g, w_up, ffn_conv_w, ffn_conv_b,
                                w_down, norm_final_g)))
    M = dict(zip(WEIGHT_NAMES, (m_norm_mix_g, m_w_in, m_b_gate_m, m_mlstm_norm_g, m_lru_conv_w, m_lru_conv_b,
                                m_lru_wa, m_lru_ba, m_lru_wx, m_lru_bx, m_lru_lambda, m_w_out, m_norm_ffn_g,
                                m_w_up, m_ffn_conv_w, m_ffn_conv_b, m_w_down, m_norm_final_g)))
    V = dict(zip(WEIGHT_NAMES, (v_norm_mix_g, v_w_in, v_b_gate_m, v_mlstm_norm_g, v_lru_conv_w, v_lru_conv_b,
                                v_lru_wa, v_lru_ba, v_lru_wx, v_lru_bx, v_lru_lambda, v_w_out, v_norm_ffn_g,
                                v_w_up, v_ffn_conv_w, v_ffn_conv_b, v_w_down, v_norm_final_g)))
    return _train_step(x, loss_target, W, M, V)
```

```python
import functools

import jax
import jax.numpy as jnp
from jax import lax
from jax.experimental import pallas as pl
from jax.experimental.pallas import tpu as pltpu

F32 = jnp.float32
BF16 = jnp.bfloat16
MESH = pl.DeviceIdType.MESH

EPS = 1e-6
CHUNK = 128
HEADS = 4
HEAD_DIM = 256
D_MLSTM = HEADS * HEAD_DIM
LRU_BLOCKS = 8
LRU_BLOCK_DIM = 128
D_LRU = LRU_BLOCKS * LRU_BLOCK_DIM
LRU_C = 8.0
LRU_CONV = 4
FFN_CONV = 3
ADAM_LR = 0.001
ADAM_B1 = 0.9
ADAM_B2 = 0.999
ADAM_EPS = 1e-08
ADAM_WD = 0.01
ADAM_STEP = 10

N_CHIPS = 4
N_DEV = 8
LANES = 128
HALO = 8
PROJ_GATE_PAD = LANES
_QKVO = 4 * D_MLSTM
_N_GATES = 2 * HEADS
_PROJ_COLS = _QKVO + _N_GATES + 2 * D_LRU
_PROJ_PAD = _QKVO + 2 * D_LRU + PROJ_GATE_PAD
VMEM_LIMIT = 48 * 1024 * 1024
ANY = pl.BlockSpec(memory_space=pl.ANY)


def _params(sem, vmem=VMEM_LIMIT):
    return pltpu.CompilerParams(dimension_semantics=sem, vmem_limit_bytes=vmem)


def _matmul(name, a, b, grid, a_spec, b_spec, o_spec, out_sds, contract, res=None, res_spec=None, after=None):
    nk = grid[2]
    acc_shape = tuple(d for d in o_spec.block_shape if d is not None)

    def body(*refs):
        refs = list(refs)
        a_ref, b_ref = refs[:2]
        r_ref = refs[2] if res is not None else None
        o_ref = refs[-1] if nk == 1 else refs[-2]
        acc_ref = None if nk == 1 else refs[-1]
        k = pl.program_id(2)

        def part():
            return lax.dot_general(a_ref[...], b_ref[...], (contract, ((), ())), preferred_element_type=F32)

        def finish(r):
            if r_ref is not None:
                r = r_ref[...] + r
            o_ref[...] = r.astype(o_ref.dtype)

        if nk == 1:
            finish(part())
            return

        @pl.when(k == 0)
        def _():
            acc_ref[...] = part()

        @pl.when(jnp.logical_and(k > 0, k < nk - 1))
        def _():
            acc_ref[...] += part()

        @pl.when(k == nk - 1)
        def _():
            finish(acc_ref[...] + part())

    in_specs = [a_spec, b_spec] + ([] if res is None else [res_spec]) + ([] if after is None else [ANY])
    args = (a, b) + (() if res is None else (res,)) + (() if after is None else (after,))
    if after is not None:
        inner = body
        body = lambda *refs: inner(*refs[:len(in_specs) - 1], *refs[len(in_specs):])
    return pl.pallas_call(
        body, out_shape=out_sds, grid=grid, in_specs=in_specs, out_specs=o_spec,
        scratch_shapes=[] if nk == 1 else [pltpu.VMEM(acc_shape, F32)], name=name,
        compiler_params=_params(("parallel", "parallel", "arbitrary")),
    )(*args)


NN = ((1,), (0,))
NT = ((1,), (1,))
TN = ((0,), (0,))


def _mm_nn(name, a, b, tm, tn, tk, out_dtype=F32, res=None):
    m, k = a.shape
    n = b.shape[1]
    return _matmul(name, a, b, (m // tm, n // tn, k // tk),
                   pl.BlockSpec((tm, tk), lambda i, j, kk: (i, kk)),
                   pl.BlockSpec((tk, tn), lambda i, j, kk: (kk, j)),
                   pl.BlockSpec((tm, tn), lambda i, j, kk: (i, j)),
                   jax.ShapeDtypeStruct((m, n), out_dtype), NN,
                   res=res, res_spec=pl.BlockSpec((tm, tn), lambda i, j, kk: (i, j)))


def _mm_nt(name, a, b, tm, tn, tk, out_dtype=F32, res=None, after=None):
    m, k = a.shape
    n = b.shape[0]
    return _matmul(name, a, b, (m // tm, n // tn, k // tk),
                   pl.BlockSpec((tm, tk), lambda i, j, kk: (i, kk)),
                   pl.BlockSpec((tn, tk), lambda i, j, kk: (j, kk)),
                   pl.BlockSpec((tm, tn), lambda i, j, kk: (i, j)),
                   jax.ShapeDtypeStruct((m, n), out_dtype), NT,
                   res=res, res_spec=pl.BlockSpec((tm, tn), lambda i, j, kk: (i, j)), after=after)


def _mm_tn(name, a, b, tm, tn, tk, out_dtype=F32):
    k, m = a.shape
    n = b.shape[1]
    tk = min(tk, k)
    return _matmul(name, a, b, (m // tm, n // tn, k // tk),
                   pl.BlockSpec((tk, tm), lambda i, j, kk: (kk, i)),
                   pl.BlockSpec((tk, tn), lambda i, j, kk: (kk, j)),
                   pl.BlockSpec((tm, tn), lambda i, j, kk: (i, j)),
                   jax.ShapeDtypeStruct((m, n), out_dtype), TN)


def _up_shard(n):
    return 2 * (n % 2) + (n // 2) // 2, (n // 2) % 2


def _mm_up_fwd(name, a, wg_up, tm, tk):
    m, k = a.shape
    _, _, cols = wg_up.shape
    tn = cols // 2
    return _matmul(name, a, wg_up, (m // tm, 2 * N_CHIPS, k // tk),
                   pl.BlockSpec((tm, tk), lambda i, j, kk: (i, kk)),
                   pl.BlockSpec((None, tk, tn), lambda i, j, kk: (_up_shard(j)[0], kk, _up_shard(j)[1])),
                   pl.BlockSpec((tm, tn), lambda i, j, kk: (i, j)),
                   jax.ShapeDtypeStruct((m, 2 * N_CHIPS * tn), F32), NN)


def _mm_up_bwd_x(name, dgu, wg_up, tm, tn):
    m, _ = dgu.shape
    _, d, cols = wg_up.shape
    tk = cols // 2
    return _matmul(name, dgu, wg_up, (m // tm, d // tn, 2 * N_CHIPS),
                   pl.BlockSpec((tm, tk), lambda i, j, kk: (i, kk)),
                   pl.BlockSpec((None, tn, tk), lambda i, j, kk: (_up_shard(kk)[0], j, _up_shard(kk)[1])),
                   pl.BlockSpec((tm, tn), lambda i, j, kk: (i, j)),
                   jax.ShapeDtypeStruct((m, d), F32), NT)


def _mm_up_bwd_w(name, n2, dgu, tm, tk):
    s, d = n2.shape
    tk = min(tk, s)
    tn = dgu.shape[1] // (2 * N_CHIPS)
    return _matmul(name, n2, dgu, (d // tm, 2 * N_CHIPS, s // tk),
                   pl.BlockSpec((tk, tm), lambda i, j, kk: (kk, i)),
                   pl.BlockSpec((tk, tn), lambda i, j, kk: (kk, j)),
                   pl.BlockSpec((None, tm, tn), lambda i, j, kk: (_up_shard(j)[0], i, _up_shard(j)[1])),
                   jax.ShapeDtypeStruct((N_CHIPS, d, 2 * tn), F32), TN)


def _rmsnorm_fwd(name, x, g, tm=256):
    s, d = x.shape

    def body(x_ref, g_ref, n_ref, r_ref):
        xf = x_ref[...]
        r = lax.rsqrt(jnp.mean(xf * xf, axis=-1, keepdims=True) + EPS)
        n_ref[...] = ((xf * r) * g_ref[...]).astype(BF16)
        r_ref[...] = r

    return pl.pallas_call(
        body, grid=(s // tm,), name=name,
        in_specs=[pl.BlockSpec((tm, d), lambda i: (i, 0)), pl.BlockSpec((1, d), lambda i: (0, 0))],
        out_specs=[pl.BlockSpec((tm, d), lambda i: (i, 0)), pl.BlockSpec((tm, 1), lambda i: (i, 0))],
        out_shape=[jax.ShapeDtypeStruct((s, d), BF16), jax.ShapeDtypeStruct((s, 1), F32)],
        compiler_params=_params(("parallel",)),
    )(x, g)


def _rmsnorm_bwd(name, x, rstd, g, dn, dres, tm=256):
    s, d = x.shape

    def body(x_ref, r_ref, g_ref, dn_ref, dres_ref, dx_ref, dxb_ref, dg_ref):
        @pl.when(pl.program_id(0) == 0)
        def _():
            dg_ref[...] = jnp.zeros_like(dg_ref)

        r = r_ref[...]
        xhat = x_ref[...] * r
        dn_v = dn_ref[...]
        dxhat = dn_v * g_ref[...]
        dx = dres_ref[...] + r * (dxhat - xhat * jnp.mean(dxhat * xhat, axis=-1, keepdims=True))
        dx_ref[...] = dx
        dxb_ref[...] = dx.astype(BF16)
        dg_ref[...] += jnp.sum(dn_v * xhat, axis=0, keepdims=True)

    row = pl.BlockSpec((tm, d), lambda i: (i, 0))
    vec = pl.BlockSpec((1, d), lambda i: (0, 0))
    return pl.pallas_call(
        body, grid=(s // tm,), name=name,
        in_specs=[row, pl.BlockSpec((tm, 1), lambda i: (i, 0)), vec, row, row],
        out_specs=[row, row, vec],
        out_shape=[jax.ShapeDtypeStruct((s, d), F32), jax.ShapeDtypeStruct((s, d), BF16),
                   jax.ShapeDtypeStruct((1, d), F32)],
        compiler_params=_params(("arbitrary",)),
    )(x, rstd, g, dn, dres)


def _loss_head(name, x, g, target, tm=256):
    s, d = x.shape

    def body(x_ref, g_ref, t_ref, loss_ref, dx_ref, dxb_ref, dg_ref):
        @pl.when(pl.program_id(0) == 0)
        def _():
            dg_ref[...] = jnp.zeros_like(dg_ref)
            loss_ref[...] = jnp.zeros_like(loss_ref)

        xf = x_ref[...]
        gv = g_ref[...]
        r = lax.rsqrt(jnp.mean(xf * xf, axis=-1, keepdims=True) + EPS)
        xhat = xf * r
        err = xhat * gv - t_ref[...]
        loss_ref[...] += 0.5 * jnp.sum(jnp.mean(err * err, axis=-1, keepdims=True), axis=0, keepdims=True)
        dy = err * (1.0 / d)
        dxhat = dy * gv
        dx = r * (dxhat - xhat * jnp.mean(dxhat * xhat, axis=-1, keepdims=True))
        dx_ref[...] = dx
        dxb_ref[...] = dx.astype(BF16)
        dg_ref[...] += jnp.sum(dy * xhat, axis=0, keepdims=True)

    row = pl.BlockSpec((tm, d), lambda i: (i, 0))
    vec = pl.BlockSpec((1, d), lambda i: (0, 0))
    return pl.pallas_call(
        body, grid=(s // tm,), name=name,
        in_specs=[row, vec, row],
        out_specs=[pl.BlockSpec((1, 1), lambda i: (0, 0)), row, row, vec],
        out_shape=[jax.ShapeDtypeStruct((1, 1), F32), jax.ShapeDtypeStruct((s, d), F32),
                   jax.ShapeDtypeStruct((s, d), BF16), jax.ShapeDtypeStruct((1, d), F32)],
        compiler_params=_params(("arbitrary",)),
    )(x, g, target)


def _sigmoid(v):
    return 1.0 / (1.0 + jnp.exp(-v))


def _log_sigmoid(v):
    return jnp.minimum(v, 0.0) - jnp.log1p(jnp.exp(-jnp.abs(v)))


def _softplus(v):
    return jnp.maximum(v, 0.0) + jnp.log1p(jnp.exp(-jnp.abs(v)))


def _one_minus_exp(z):
    series = -z * (1.0 + z * (0.5 + z * (1.0 / 6.0 + z * (1.0 / 24.0 + z * (1.0 / 120.0)))))
    return jnp.where(z > -0.1, series, 1.0 - jnp.exp(z))


_GELU_K = 0.7978845608028654
_GELU_C = 0.044715


def _gelu(v):
    return 0.5 * v * (1.0 + jnp.tanh(_GELU_K * (v + _GELU_C * v * v * v)))


def _gelu_grad(v):
    t = jnp.tanh(_GELU_K * (v + _GELU_C * v * v * v))
    return 0.5 * (1.0 + t) + 0.5 * v * (1.0 - t * t) * _GELU_K * (1.0 + 3.0 * _GELU_C * v * v)


def _rows(shape):
    return lax.broadcasted_iota(jnp.int32, shape, 0)


def _cols(shape):
    return lax.broadcasted_iota(jnp.int32, shape, 1)


def _shift_down(v, prev, d):
    if d == 0:
        return v
    rolled = pltpu.roll(v, d, axis=0)
    head = jnp.where(_rows((HALO, v.shape[1])) >= d, rolled[:HALO], pltpu.roll(prev, d, axis=0))
    if v.shape[0] == HALO:
        return head
    return jnp.concatenate([head, rolled[HALO:]], axis=0)


def _shift_up(v, nxt, d):
    if d == 0:
        return v
    n = v.shape[0]
    rolled = pltpu.roll(v, n - d, axis=0)
    tail = jnp.where(_rows((HALO, v.shape[1])) < HALO - d, rolled[n - HALO:], pltpu.roll(nxt, HALO - d, axis=0))
    if n == HALO:
        return tail
    return jnp.concatenate([rolled[:n - HALO], tail], axis=0)


def _dot(a, b, contract):
    return lax.dot_general(a.astype(BF16), b.astype(BF16), (contract, ((), ())), preferred_element_type=F32)


def _mlstm_chunk_common(h, q_ref, k_ref, v_ref, gcol_ref, grow_ref, brow_ref, bcol_ref, m_prev):
    L = CHUNK
    sl = slice(h * HEAD_DIM, (h + 1) * HEAD_DIM)
    qh = q_ref[:, sl]
    kh = k_ref[:, sl]
    vh = v_ref[:, sl]
    qs = qh * (HEAD_DIM ** -0.5)
    gates = gcol_ref[...] + brow_ref[...]
    lane = _cols(gates.shape)
    ic = jnp.sum(jnp.where(lane == h, gates, 0.0), axis=1, keepdims=True)
    fc = jnp.sum(jnp.where(lane == HEADS + h, gates, 0.0), axis=1, keepdims=True)
    ir = grow_ref[h:h + 1, :] + bcol_ref[h:h + 1, :]
    fr = grow_ref[HEADS + h:HEADS + h + 1, :] + bcol_ref[HEADS + h:HEADS + h + 1, :]
    logf_c = _log_sigmoid(fc)
    logf_r = _log_sigmoid(fr)
    t_i = _rows((L, L))
    s_i = _cols((L, L))
    tri = t_i >= s_i
    b_c = jnp.sum(jnp.where(tri, logf_r, 0.0), axis=1, keepdims=True)
    b_r = jnp.sum(jnp.where(t_i <= s_i, logf_c, 0.0), axis=0, keepdims=True)
    btot = jnp.sum(logf_r, axis=1, keepdims=True)
    dmat = jnp.where(tri, b_c - b_r + ir, -jnp.inf)
    m_inter = b_c + m_prev
    m_t = jnp.maximum(m_inter, jnp.max(dmat, axis=1, keepdims=True))
    e_mat = jnp.exp(dmat - m_t)
    e_inter = jnp.exp(m_inter - m_t)
    wqk = _dot(qs, kh, NT) * e_mat
    w_end_r = btot - b_r + ir
    m_loc = jnp.max(w_end_r, axis=1, keepdims=True)
    e_end_c = jnp.exp(btot - b_c + ic - m_loc)
    m_new = jnp.maximum(btot + m_prev, m_loc)
    a_dec = jnp.exp(btot + m_prev - m_new)
    c_inj = jnp.exp(m_loc - m_new)
    return dict(qh=qh, kh=kh, vh=vh, qs=qs, fc=fc, tri=tri, t_i=t_i, s_i=s_i, m_t=m_t, e_mat=e_mat,
                e_inter=e_inter, wqk=wqk, e_end_c=e_end_c, m_new=m_new, a_dec=a_dec, c_inj=c_inj)


def _mlstm_fwd(proj, gates_t, bias_row, bias_col, head_g):
    s = proj.shape[0]
    nc = s // CHUNK
    L = CHUNK

    def body(q_ref, k_ref, v_ref, o_ref, gcol_ref, grow_ref, brow_ref, bcol_ref, hg_ref,
             out_ref, cprev_ref, nprev_ref, mprev_ref, c_scr, n_scr, m_scr):
        @pl.when(pl.program_id(0) == 0)
        def _():
            c_scr[...] = jnp.zeros_like(c_scr)
            n_scr[...] = jnp.zeros_like(n_scr)
            m_scr[...] = jnp.zeros_like(m_scr)

        for h in range(HEADS):
            sl = slice(h * HEAD_DIM, (h + 1) * HEAD_DIM)
            m_prev = m_scr[h:h + 1, 0:1]
            n_prev = n_scr[h:h + 1, :]
            c_prev = c_scr[h].astype(BF16)
            q = _mlstm_chunk_common(h, q_ref, k_ref, v_ref, gcol_ref, grow_ref, brow_ref, bcol_ref, m_prev)
            num = _dot(q["wqk"], q["vh"], NN) + q["e_inter"] * _dot(q["qs"], c_prev, NN)
            den = (jnp.sum(q["wqk"], axis=1, keepdims=True)
                   + q["e_inter"] * jnp.sum(q["qs"] * n_prev, axis=1, keepdims=True))
            hh = num / jnp.maximum(jnp.abs(den), jnp.exp(-q["m_t"]))
            hn = hh * lax.rsqrt(jnp.mean(hh * hh, axis=1, keepdims=True) + EPS) * hg_ref[h:h + 1, :]
            out_ref[:, sl] = (_sigmoid(o_ref[:, sl]) * hn).astype(BF16)
            cprev_ref[h] = c_prev
            nprev_ref[h:h + 1, :] = n_prev
            mprev_ref[h:h + 1, :] = jnp.broadcast_to(m_prev, (1, LANES))
            c_loc = _dot(q["kh"], q["e_end_c"] * q["vh"], TN)
            n_loc = jnp.sum(q["e_end_c"] * q["kh"], axis=0, keepdims=True)
            c_scr[h] = q["a_dec"] * c_scr[h] + q["c_inj"] * c_loc
            n_scr[h:h + 1, :] = q["a_dec"] * n_prev + q["c_inj"] * n_loc
            m_scr[h:h + 1, :] = jnp.broadcast_to(q["m_new"], (1, LANES))

    blk = lambda j: pl.BlockSpec((L, D_MLSTM), lambda c, j=j: (c, j))
    full = lambda shp: pl.BlockSpec(shp, lambda c: tuple(0 for _ in shp))
    return pl.pallas_call(
        body, grid=(nc,), name="mlstm_fwd",
        in_specs=[blk(0), blk(1), blk(2), blk(3),
                  pl.BlockSpec((L, LANES), lambda c: (c, (4 * D_MLSTM + 2 * D_LRU) // LANES)),
                  pl.BlockSpec((None, 2 * HEADS, L), lambda c: (c, 0, 0)),
                  full((1, LANES)), full((2 * HEADS, 1)), full((HEADS, HEAD_DIM))],
        out_specs=[pl.BlockSpec((L, D_MLSTM), lambda c: (c, 0)),
                   pl.BlockSpec((None, HEADS, HEAD_DIM, HEAD_DIM), lambda c: (c, 0, 0, 0)),
                   pl.BlockSpec((None, HEADS, HEAD_DIM), lambda c: (c, 0, 0)),
                   pl.BlockSpec((None, HEADS, LANES), lambda c: (c, 0, 0))],
        out_shape=[jax.ShapeDtypeStruct((s, D_MLSTM + D_LRU), BF16),
                   jax.ShapeDtypeStruct((nc, HEADS, HEAD_DIM, HEAD_DIM), BF16),
                   jax.ShapeDtypeStruct((nc, HEADS, HEAD_DIM), F32),
                   jax.ShapeDtypeStruct((nc, HEADS, LANES), F32)],
        scratch_shapes=[pltpu.VMEM((HEADS, HEAD_DIM, HEAD_DIM), F32), pltpu.VMEM((HEADS, HEAD_DIM), F32),
                        pltpu.VMEM((HEADS, LANES), F32)],
        compiler_params=_params(("arbitrary",)),
    )(proj, proj, proj, proj, proj, gates_t, bias_row, bias_col, head_g)


def _mlstm_bwd(proj, gates_t, bias_row, bias_col, head_g, cprev, nprev, mprev, dmix):
    s = proj.shape[0]
    nc = s // CHUNK
    L = CHUNK

    def body(q_ref, k_ref, v_ref, o_ref, gcol_ref, grow_ref, brow_ref, bcol_ref, hg_ref,
             cprev_ref, nprev_ref, mprev_ref, dmix_ref,
             dqkvo_ref, dgate_ref, dhg_ref, g_scr, gn_scr):
        @pl.when(pl.program_id(0) == 0)
        def _():
            g_scr[...] = jnp.zeros_like(g_scr)
            gn_scr[...] = jnp.zeros_like(gn_scr)
            dhg_ref[...] = jnp.zeros_like(dhg_ref)

        lane = _cols((L, LANES))
        dgate = jnp.zeros((L, LANES), F32)
        for h in range(HEADS):
            sl = slice(h * HEAD_DIM, (h + 1) * HEAD_DIM)
            m_prev = mprev_ref[h:h + 1, 0:1]
            n_prev = nprev_ref[h:h + 1, :]
            c_prev = cprev_ref[h]
            q = _mlstm_chunk_common(h, q_ref, k_ref, v_ref, gcol_ref, grow_ref, brow_ref, bcol_ref, m_prev)
            qh, kh, vh, qs, wqk, e_inter = q["qh"], q["kh"], q["vh"], q["qs"], q["wqk"], q["e_inter"]
            num_state = e_inter * _dot(qs, c_prev, NN)
            den_state = e_inter * jnp.sum(qs * n_prev, axis=1, keepdims=True)
            num = _dot(wqk, vh, NN) + num_state
            den = jnp.sum(wqk, axis=1, keepdims=True) + den_state
            floor = jnp.exp(-q["m_t"])
            denom = jnp.maximum(jnp.abs(den), floor)
            hh = num / denom
            rn = lax.rsqrt(jnp.mean(hh * hh, axis=1, keepdims=True) + EPS)
            hn_pre = hh * rn
            hg = hg_ref[h:h + 1, :]
            sg = _sigmoid(o_ref[:, sl])
            dout = dmix_ref[:, sl]
            d_o = dout * (hn_pre * hg) * sg * (1.0 - sg)
            dhn = dout * sg
            dhg_ref[h:h + 1, :] += jnp.sum(dhn * hn_pre, axis=0, keepdims=True)
            dhn_pre = dhn * hg
            dhh = rn * (dhn_pre - hn_pre * jnp.mean(dhn_pre * hn_pre, axis=1, keepdims=True))
            dnum = dhh / denom
            dden = jnp.where(jnp.abs(den) >= floor,
                             -jnp.sum(hh * dhh, axis=1, keepdims=True) / denom * jnp.sign(den), 0.0)
            dwqk = _dot(dnum, vh, NT) + dden
            dv = _dot(wqk, dnum, TN)
            dp = dwqk * q["e_mat"]
            dqs = _dot(dp, kh, NN) + e_inter * (_dot(dnum, c_prev, NT) + dden * n_prev)
            dk = _dot(dp, qs, TN)
            g_next = g_scr[h]
            gn_next = gn_scr[h:h + 1, :]
            w_state = q["e_end_c"] * q["c_inj"]
            dk_state = w_state * (_dot(vh, g_next, NT) + gn_next)
            dk = dk + dk_state
            dv = dv + w_state * _dot(kh, g_next, NN)
            dq = dqs * (HEAD_DIM ** -0.5)
            eye = q["t_i"] == q["s_i"]
            to_row = lambda col: jnp.sum(jnp.where(eye, col, 0.0), axis=0, keepdims=True)
            to_col = lambda row: jnp.sum(jnp.where(eye, row, 0.0), axis=1, keepdims=True)
            g_pair = dwqk * wqk
            rs_in = jnp.sum(g_pair, axis=1, keepdims=True)
            cs_in_r = jnp.sum(g_pair, axis=0, keepdims=True)
            rs_state = (jnp.sum(dnum * num_state, axis=1, keepdims=True) + dden * den_state)
            cs_state = jnp.sum(kh * dk_state, axis=1, keepdims=True)
            di_c = to_col(cs_in_r) + cs_state
            through = q["a_dec"] * (jnp.sum(jnp.sum(g_next * c_prev.astype(F32), axis=1, keepdims=True),
                                            axis=0, keepdims=True)
                                    + jnp.sum(gn_next * n_prev, axis=1, keepdims=True))
            ends_here = to_row(rs_in + rs_state) - cs_in_r
            da_c = (jnp.sum(jnp.where(q["s_i"] >= q["t_i"], ends_here, 0.0), axis=1, keepdims=True)
                    + jnp.sum(jnp.where(q["s_i"] < q["t_i"], to_row(cs_state), 0.0), axis=1, keepdims=True)
                    + through)
            df_c = da_c * _sigmoid(-q["fc"])
            dgate = dgate + jnp.where(lane == h, di_c, 0.0) + jnp.where(lane == HEADS + h, df_c, 0.0)
            dqkvo_ref[:, sl] = dq.astype(BF16)
            dqkvo_ref[:, D_MLSTM + h * HEAD_DIM:D_MLSTM + (h + 1) * HEAD_DIM] = dk.astype(BF16)
            dqkvo_ref[:, 2 * D_MLSTM + h * HEAD_DIM:2 * D_MLSTM + (h + 1) * HEAD_DIM] = dv.astype(BF16)
            dqkvo_ref[:, 3 * D_MLSTM + h * HEAD_DIM:3 * D_MLSTM + (h + 1) * HEAD_DIM] = d_o.astype(BF16)
            g_scr[h] = q["a_dec"] * g_next + _dot(e_inter * qs, dnum, TN)
            gn_scr[h:h + 1, :] = q["a_dec"] * gn_next + jnp.sum(e_inter * qs * dden, axis=0, keepdims=True)
        dgate_ref[...] = dgate

    rev = lambda c: nc - 1 - c
    blk = lambda j: pl.BlockSpec((L, D_MLSTM), lambda c, j=j: (rev(c), j))
    full = lambda shp: pl.BlockSpec(shp, lambda c: tuple(0 for _ in shp))
    return pl.pallas_call(
        body, grid=(nc,), name="mlstm_bwd",
        in_specs=[blk(0), blk(1), blk(2), blk(3),
                  pl.BlockSpec((L, LANES), lambda c: (rev(c), (4 * D_MLSTM + 2 * D_LRU) // LANES)),
                  pl.BlockSpec((None, 2 * HEADS, L), lambda c: (rev(c), 0, 0)),
                  full((1, LANES)), full((2 * HEADS, 1)), full((HEADS, HEAD_DIM)),
                  pl.BlockSpec((None, HEADS, HEAD_DIM, HEAD_DIM), lambda c: (rev(c), 0, 0, 0)),
                  pl.BlockSpec((None, HEADS, HEAD_DIM), lambda c: (rev(c), 0, 0)),
                  pl.BlockSpec((None, HEADS, LANES), lambda c: (rev(c), 0, 0)),
                  pl.BlockSpec((L, D_MLSTM), lambda c: (rev(c), 0))],
        out_specs=[pl.BlockSpec((L, 4 * D_MLSTM), lambda c: (rev(c), 0)),
                   pl.BlockSpec((L, LANES), lambda c: (rev(c), 0)),
                   full((HEADS, HEAD_DIM))],
        out_shape=[jax.ShapeDtypeStruct((s, _PROJ_PAD), BF16),
                   jax.ShapeDtypeStruct((s, LANES), F32),
                   jax.ShapeDtypeStruct((HEADS, HEAD_DIM), F32)],
        scratch_shapes=[pltpu.VMEM((HEADS, HEAD_DIM, HEAD_DIM), F32), pltpu.VMEM((HEADS, HEAD_DIM), F32)],
        compiler_params=_params(("arbitrary",)),
    )(proj, proj, proj, proj, proj, gates_t, bias_row, bias_col, head_g, cprev, nprev, mprev, dmix)


def _lru_gates(xc, wa_ref, wx_ref, ba, bx, lam):
    r = _sigmoid(_dot(xc, wa_ref[...], NN) + ba)
    ig = _sigmoid(_dot(xc, wx_ref[...], NN) + bx)
    sp = _softplus(-lam)
    log_a = (-LRU_C * r) * sp
    a = jnp.exp(log_a)
    mult = jnp.sqrt(_one_minus_exp(2.0 * log_a))
    return r, ig, sp, a, mult


def _lru_conv(xr, prev, w_ref, b):
    xc = b + _shift_down(xr, prev, 3) * w_ref[0:1, :]
    for j in range(1, LRU_CONV):
        xc = xc + _shift_down(xr, prev, LRU_CONV - 1 - j) * w_ref[j:j + 1, :]
    return xc


def _lru_fwd(proj, mix, conv_w, conv_b, wa, wx, ba, bx, lam, tt=512):
    s = proj.shape[0]
    tt = min(tt, s)
    nt = s // tt
    B = LRU_BLOCK_DIM
    lru_col = 4 * D_MLSTM // B
    mix_col = D_MLSTM // B

    def body(xr_ref, gr_ref, cw_ref, cb_ref, wa_ref, wx_ref, ba_ref, bx_ref, lam_ref, mix_in_ref,
             out_ref, h_ref, prev_scr, hcar_scr):
        @pl.when(pl.program_id(1) == 0)
        def _():
            prev_scr[...] = jnp.zeros_like(prev_scr)
            hcar_scr[...] = jnp.zeros_like(hcar_scr)

        xr = xr_ref[...]
        xc = _lru_conv(xr, prev_scr[...], cw_ref, cb_ref[...])
        prev_scr[...] = xr[tt - HALO:, :]
        _, ig, _, a, mult = _lru_gates(xc, wa_ref, wx_ref, ba_ref[...], bx_ref[...], lam_ref[...])
        u = mult * (ig * xc)
        rows = _rows((tt, B))
        acc_a, acc_b = a, u
        d = 1
        while d < tt:
            keep = rows >= d
            sh_a = jnp.where(keep, pltpu.roll(acc_a, d, axis=0), 1.0)
            sh_b = jnp.where(keep, pltpu.roll(acc_b, d, axis=0), 0.0)
            acc_b = acc_a * sh_b + acc_b
            acc_a = acc_a * sh_a
            d *= 2
        hv = acc_b + acc_a * hcar_scr[0:1, :]
        hcar_scr[...] = jnp.broadcast_to(hv[tt - 1:tt, :], hcar_scr.shape)
        h_ref[...] = hv
        out_ref[...] = (hv * _gelu(gr_ref[...])).astype(BF16)

    chan = lambda rws: pl.BlockSpec((rws, B), lambda n, i: (0, n))
    return pl.pallas_call(
        body, grid=(LRU_BLOCKS, nt), name="lru_fwd",
        in_specs=[pl.BlockSpec((tt, B), lambda n, i: (i, lru_col + 2 * n)),
                  pl.BlockSpec((tt, B), lambda n, i: (i, lru_col + 2 * n + 1)),
                  chan(LRU_CONV), chan(1),
                  pl.BlockSpec((None, B, B), lambda n, i: (n, 0, 0)),
                  pl.BlockSpec((None, B, B), lambda n, i: (n, 0, 0)),
                  chan(1), chan(1), chan(1), ANY],
        out_specs=[pl.BlockSpec((tt, B), lambda n, i: (i, mix_col + n)), pl.BlockSpec((tt, B), lambda n, i: (i, n))],
        out_shape=[jax.ShapeDtypeStruct(mix.shape, BF16), jax.ShapeDtypeStruct((s, D_LRU), F32)],
        scratch_shapes=[pltpu.VMEM((HALO, B), F32), pltpu.VMEM((HALO, B), F32)],
        input_output_aliases={9: 0},
        compiler_params=_params(("parallel", "arbitrary")),
    )(proj, proj, conv_w, conv_b, wa, wx, ba, bx, lam, mix)


def _lru_bwd(proj, hsave, dmix, dproj, conv_w, conv_b, wa, wx, ba, bx, lam, tt=512):
    s = proj.shape[0]
    tt = min(tt, s)
    nt = s // tt
    B = LRU_BLOCK_DIM
    lru_col = 4 * D_MLSTM // B
    dmix_col = D_MLSTM // B
    hpb = tt // HALO

    def body(xr_ref, xprev_ref, gr_ref, h_ref, hprev_ref, dmix_ref, cw_ref, cb_ref, wa_ref, wx_ref,
             ba_ref, bx_ref, lam_ref, dproj_in_ref,
             dxg_ref, dcw_ref, dcb_ref, dwa_ref, dwx_ref, dba_ref, dbx_ref, dlam_ref,
             gcar_scr, acar_scr, dxc_scr):
        i = pl.program_id(1)
        first_tile = i == nt - 1

        @pl.when(i == 0)
        def _():
            gcar_scr[...] = jnp.zeros_like(gcar_scr)
            acar_scr[...] = jnp.zeros_like(acar_scr)
            dxc_scr[...] = jnp.zeros_like(dxc_scr)
            for ref in (dcw_ref, dcb_ref, dwa_ref, dwx_ref, dba_ref, dbx_ref, dlam_ref):
                ref[...] = jnp.zeros_like(ref)

        xr = xr_ref[...]
        xprev = jnp.where(first_tile, 0.0, xprev_ref[...])
        hprev = jnp.where(first_tile, 0.0, hprev_ref[...])
        lam = lam_ref[...]
        xc = _lru_conv(xr, xprev, cw_ref, cb_ref[...])
        r, ig, sp, a, mult = _lru_gates(xc, wa_ref, wx_ref, ba_ref[...], bx_ref[...], lam)
        gr = gr_ref[...]
        hv = h_ref[...]
        dout = dmix_ref[...]
        dxg_ref[:, B:] = (dout * hv * _gelu_grad(gr)).astype(BF16)
        dh = dout * _gelu(gr)
        rows = _rows((tt, B))
        acc_a = _shift_up(a, acar_scr[...], 1)
        acc_b = dh
        d = 1
        while d < tt:
            keep = rows < tt - d
            sh_a = jnp.where(keep, pltpu.roll(acc_a, tt - d, axis=0), 1.0)
            sh_b = jnp.where(keep, pltpu.roll(acc_b, tt - d, axis=0), 0.0)
            acc_b = acc_a * sh_b + acc_b
            acc_a = acc_a * sh_a
            d *= 2
        gv = acc_b + acc_a * gcar_scr[0:1, :]
        gcar_scr[...] = jnp.broadcast_to(gv[0:1, :], gcar_scr.shape)
        acar_scr[...] = jnp.broadcast_to(a[0:1, :], acar_scr.shape)
        h_before = _shift_down(hv, hprev, 1)
        da = gv * h_before
        dmult = gv * (ig * xc)
        dig = gv * mult * xc
        dxc = gv * mult * ig
        dlog_a = da * a - dmult * (a * a) / mult
        dr = dlog_a * (-LRU_C * sp)
        dlam_ref[...] += jnp.sum(dlog_a * (-LRU_C * r), axis=0, keepdims=True) * (-_sigmoid(-lam))
        dpre_r = dr * r * (1.0 - r)
        dpre_i = dig * ig * (1.0 - ig)
        dba_ref[...] += jnp.sum(dpre_r, axis=0, keepdims=True)
        dbx_ref[...] += jnp.sum(dpre_i, axis=0, keepdims=True)
        dwa_ref[...] += _dot(xc, dpre_r, TN)
        dwx_ref[...] += _dot(xc, dpre_i, TN)
        dxc = dxc + _dot(dpre_r, wa_ref[...], NT) + _dot(dpre_i, wx_ref[...], NT)
        dcb_ref[...] += jnp.sum(dxc, axis=0, keepdims=True)
        nxt = dxc_scr[...]
        dxr = jnp.zeros((tt, B), F32)
        for j in range(LRU_CONV):
            sft = LRU_CONV - 1 - j
            dcw_ref[j:j + 1, :] += jnp.sum(dxc * _shift_down(xr, xprev, sft), axis=0, keepdims=True)
            dxr = dxr + _shift_up(dxc, nxt, sft) * cw_ref[j:j + 1, :]
        dxc_scr[...] = dxc[:HALO, :]
        dxg_ref[:, :B] = dxr.astype(BF16)

    rev = lambda i: nt - 1 - i
    tile = lambda col, step: pl.BlockSpec((tt, B), lambda n, i: (rev(i), col + step * n))
    halo = lambda col, step: pl.BlockSpec(
        (HALO, B), lambda n, i: (jnp.maximum(rev(i) * hpb - 1, 0), col + step * n))
    chan = lambda rws: pl.BlockSpec((rws, B), lambda n, i: (0, n))
    wblk = pl.BlockSpec((None, B, B), lambda n, i: (n, 0, 0))
    return pl.pallas_call(
        body, grid=(LRU_BLOCKS, nt), name="lru_bwd",
        in_specs=[tile(lru_col, 2), halo(lru_col, 2), tile(lru_col + 1, 2), tile(0, 1), halo(0, 1),
                  tile(dmix_col, 1), chan(LRU_CONV), chan(1), wblk, wblk, chan(1), chan(1), chan(1), ANY],
        out_specs=[pl.BlockSpec((tt, 2 * B), lambda n, i: (rev(i), lru_col // 2 + n)),
                   chan(LRU_CONV), chan(1), wblk, wblk, chan(1), chan(1), chan(1)],
        out_shape=[jax.ShapeDtypeStruct(dproj.shape, BF16),
                   jax.ShapeDtypeStruct((LRU_CONV, D_LRU), F32), jax.ShapeDtypeStruct((1, D_LRU), F32),
                   jax.ShapeDtypeStruct((LRU_BLOCKS, B, B), F32), jax.ShapeDtypeStruct((LRU_BLOCKS, B, B), F32),
                   jax.ShapeDtypeStruct((1, D_LRU), F32), jax.ShapeDtypeStruct((1, D_LRU), F32),
                   jax.ShapeDtypeStruct((1, D_LRU), F32)],
        scratch_shapes=[pltpu.VMEM((HALO, B), F32), pltpu.VMEM((HALO, B), F32), pltpu.VMEM((HALO, B), F32)],
        input_output_aliases={13: 0},
        compiler_params=_params(("parallel", "arbitrary")),
    )(proj, proj, proj, hsave, hsave, dmix, conv_w, conv_b, wa, wx, ba, bx, lam, dproj)


def _ffn_conv(gp, prev, w_ref, b):
    g = b + _shift_down(gp, prev, 2) * w_ref[0:1, :]
    for j in range(1, FFN_CONV):
        g = g + _shift_down(gp, prev, FFN_CONV - 1 - j) * w_ref[j:j + 1, :]
    return g


def _ffn_act_fwd(gu, conv_w, conv_b, tt=256):
    s = gu.shape[0]
    tt = min(tt, s)
    d_ff = conv_w.shape[1]
    tc = d_ff // N_CHIPS
    hpb = tt // HALO

    def body(g_ref, gprev_ref, u_ref, w_ref, b_ref, act_ref):
        prev = jnp.where(pl.program_id(0) == 0, 0.0, gprev_ref[...])
        gate = _ffn_conv(g_ref[...], prev, w_ref, b_ref[...])
        act_ref[...] = (gate * _sigmoid(gate) * u_ref[...]).astype(BF16)

    return pl.pallas_call(
        body, grid=(s // tt, N_CHIPS), name="ffn_act_fwd",
        in_specs=[pl.BlockSpec((tt, tc), lambda i, j: (i, 2 * j)),
                  pl.BlockSpec((HALO, tc), lambda i, j: (jnp.maximum(i * hpb - 1, 0), 2 * j)),
                  pl.BlockSpec((tt, tc), lambda i, j: (i, 2 * j + 1)),
                  pl.BlockSpec((FFN_CONV, tc), lambda i, j: (0, j)),
                  pl.BlockSpec((1, tc), lambda i, j: (0, j))],
        out_specs=pl.BlockSpec((tt, tc), lambda i, j: (i, j)),
        out_shape=jax.ShapeDtypeStruct((s, d_ff), BF16),
        compiler_params=_params(("parallel", "parallel")),
    )(gu, gu, gu, conv_w, conv_b)


def _ffn_act_bwd(gu, dact, conv_w, conv_b, tt=256):
    s = gu.shape[0]
    tt = min(tt, s)
    nt = s // tt
    d_ff = conv_w.shape[1]
    tc = d_ff // N_CHIPS
    hpb = tt // HALO

    def dgate_of(gate, up, da):
        sg = _sigmoid(gate)
        return da * up * (sg * (1.0 + gate * (1.0 - sg))), da * (gate * sg)

    def body(g_ref, gprev_ref, gnext_ref, u_ref, unext_ref, da_ref, danext_ref, w_ref, b_ref,
             dgu_ref, dw_ref, db_ref):
        i = pl.program_id(1)

        @pl.when(i == 0)
        def _():
            dw_ref[...] = jnp.zeros_like(dw_ref)
            db_ref[...] = jnp.zeros_like(db_ref)

        gp = g_ref[...]
        prev = jnp.where(i == 0, 0.0, gprev_ref[...])
        bias = b_ref[...]
        gate = _ffn_conv(gp, prev, w_ref, bias)
        dgate, dup = dgate_of(gate, u_ref[...], da_ref[...])
        gate_n = _ffn_conv(gnext_ref[...], gp[tt - HALO:, :], w_ref, bias)
        dgate_n, _ = dgate_of(gate_n, unext_ref[...], danext_ref[...])
        dgate_n = jnp.where(i == nt - 1, 0.0, dgate_n)
        db_ref[...] += jnp.sum(dgate, axis=0, keepdims=True)
        dgp = jnp.zeros((tt, tc), F32)
        for j in range(FFN_CONV):
            sft = FFN_CONV - 1 - j
            dw_ref[j:j + 1, :] += jnp.sum(dgate * _shift_down(gp, prev, sft), axis=0, keepdims=True)
            dgp = dgp + _shift_up(dgate, dgate_n, sft) * w_ref[j:j + 1, :]
        dgu_ref[:, :tc] = dgp.astype(BF16)
        dgu_ref[:, tc:] = dup.astype(BF16)

    tile = lambda half: pl.BlockSpec((tt, tc), lambda j, i, half=half: (i, 2 * j + half))
    hprev = lambda half: pl.BlockSpec((HALO, tc), lambda j, i, half=half: (jnp.maximum(i * hpb - 1, 0), 2 * j + half))
    hnext = lambda half: pl.BlockSpec(
        (HALO, tc), lambda j, i, half=half: (jnp.minimum((i + 1) * hpb, nt * hpb - 1), 2 * j + half))
    return pl.pallas_call(
        body, grid=(N_CHIPS, nt), name="ffn_act_bwd",
        in_specs=[tile(0), hprev(0), hnext(0), tile(1), hnext(1),
                  pl.BlockSpec((tt, tc), lambda j, i: (i, j)),
                  pl.BlockSpec((HALO, tc), lambda j, i: (jnp.minimum((i + 1) * hpb, nt * hpb - 1), j)),
                  pl.BlockSpec((FFN_CONV, tc), lambda j, i: (0, j)),
                  pl.BlockSpec((1, tc), lambda j, i: (0, j))],
        out_specs=[pl.BlockSpec((tt, 2 * tc), lambda j, i: (i, j)),
                   pl.BlockSpec((FFN_CONV, tc), lambda j, i: (0, j)),
                   pl.BlockSpec((1, tc), lambda j, i: (0, j))],
        out_shape=[jax.ShapeDtypeStruct((s, 2 * d_ff), BF16),
                   jax.ShapeDtypeStruct((FFN_CONV, d_ff), F32), jax.ShapeDtypeStruct((1, d_ff), F32)],
        compiler_params=_params(("parallel", "arbitrary")),
    )(gu, gu, gu, gu, gu, dact, dact, conv_w, conv_b)


def _gate_grads(dgate, dproj, tm=512):
    s, n = dgate.shape
    tm = min(tm, s)

    def body(a_ref, dproj_in_ref, o_ref, dproj_ref):
        @pl.when(pl.program_id(0) == 0)
        def _():
            o_ref[...] = jnp.zeros_like(o_ref)
        a = a_ref[...]
        o_ref[...] += jnp.sum(a, axis=0, keepdims=True)
        dproj_ref[...] = a.astype(BF16)

    return pl.pallas_call(
        body, grid=(s // tm,), name="gate_grads",
        in_specs=[pl.BlockSpec((tm, n), lambda i: (i, 0)), ANY],
        out_specs=[pl.BlockSpec((1, n), lambda i: (0, 0)),
                   pl.BlockSpec((tm, n), lambda i: (i, (_QKVO + 2 * D_LRU) // LANES))],
        out_shape=[jax.ShapeDtypeStruct((1, n), F32), jax.ShapeDtypeStruct(dproj.shape, BF16)],
        input_output_aliases={1: 1},
        compiler_params=_params(("arbitrary",)),
    )(dgate, dproj)


def _pick(n, *cands):
    for c in cands:
        if n % c == 0:
            return c
    raise ValueError(f"no tile for {n}")


def _behind(a, token):
    return a if token is None else a + token[0:1, 0:1].astype(a.dtype).reshape((1,) * a.ndim)


class _Gathered:
    def __init__(self, w):
        self.w = w

    def begin(self):
        return None

    def mid(self, grp, after):
        return None

    def end(self, grp, after):
        return self.w

    def reduce_early(self, grads):
        return None

    def reduce_early_mid(self, after):
        return None

    def reduce_late(self, grads):
        return None

    def reduce_late_mid(self, after):
        return None


def _local_step(x, target, w, comm):
    s, d = x.shape
    nc = s // CHUNK
    tm = _pick(s, 1024, 512, 256)
    tn_proj = _pick(_PROJ_PAD, 896)
    gate_col = 4 * D_MLSTM + 2 * D_LRU
    w = dict(w)

    token = comm.begin()
    n1, rstd1 = _rmsnorm_fwd("norm_mix_fwd", x, _behind(w["norm_mix_g"], token))
    comm.mid(0, n1)
    w.update(comm.end(0, None))
    proj = _mm_nn("proj_fwd", n1, w["w_in"], tm, tn_proj, d)
    token = comm.mid(1, proj)
    gates = proj[:, gate_col:gate_col + 2 * HEADS]
    gates_t = gates.reshape(nc, CHUNK, 2 * HEADS).transpose(0, 2, 1)
    bias_row = _behind(jnp.pad(w["b_gate_m"], ((0, 0), (0, LANES - 2 * HEADS))), token)
    bias_col = w["b_gate_m"].reshape(2 * HEADS, 1)
    mix, cprev, nprev, mprev = _mlstm_fwd(proj, gates_t, bias_row, bias_col, w["mlstm_norm_g"])
    mix, hsave = _lru_fwd(proj, mix, w["lru_conv_w"], w["lru_conv_b"], w["lru_wa"], w["lru_wx"],
                          w["lru_ba"], w["lru_bx"], w["lru_lambda"])
    w.update(comm.end(1, hsave))
    token = comm.mid(2, hsave)
    x1 = _mm_nn("out_fwd", mix, w["w_out"], tm, 1024, d, res=x)
    n2, rstd2 = _rmsnorm_fwd("norm_ffn_fwd", x1, _behind(w["norm_ffn_g"], token))
    w.update(comm.end(2, n2))
    token = comm.mid(3, n2)
    gu = _mm_up_fwd("up_fwd", n2, w["w_up"], tm, d)
    act = _ffn_act_fwd(gu, w["ffn_conv_w"], _behind(w["ffn_conv_b"], token))
    w.update(comm.end(3, act))
    d_ff = w["w_down"].shape[0]
    x2 = _mm_nn("down_fwd", act, w["w_down"], min(tm, 512), 1024, d_ff // 2, res=x1)
    loss, dx2, dx2b, g_norm_final = _loss_head("loss_head", x2, w["norm_final_g"], target)

    grads = {"norm_final_g": g_norm_final}
    dact = _mm_nt("down_bwd_x", dx2b, w["w_down"], tm, d_ff // N_CHIPS, d)
    grads["w_down"] = _mm_tn("down_bwd_w", act, dx2b, d_ff // N_CHIPS, 1024, 2048)
    dgu, grads["ffn_conv_w"], grads["ffn_conv_b"] = _ffn_act_bwd(gu, dact, w["ffn_conv_w"], w["ffn_conv_b"])
    dn2 = _mm_up_bwd_x("up_bwd_x", dgu, w["w_up"], tm, 1024)
    grads["w_up"] = _mm_up_bwd_w("up_bwd_w", n2, dgu, 1024, 2048)
    dx1, dx1b, grads["norm_ffn_g"] = _rmsnorm_bwd("norm_ffn_bwd", x1, rstd2, w["norm_ffn_g"], dn2, dx2)
    dmix = _mm_nt("out_bwd_x", dx1b, w["w_out"], tm, 1024, d)
    grads["w_out"] = _mm_tn("out_bwd_w", mix, dx1b, 1024, 1024, 2048)
    token = comm.reduce_early(grads)
    dproj, dgate, grads["mlstm_norm_g"] = _mlstm_bwd(proj, gates_t, _behind(bias_row, token), bias_col,
                                                     w["mlstm_norm_g"], cprev, nprev, mprev, dmix)
    token = comm.reduce_early_mid(dproj)
    (dproj, grads["lru_conv_w"], grads["lru_conv_b"], grads["lru_wa"], grads["lru_wx"],
     grads["lru_ba"], grads["lru_bx"], grads["lru_lambda"]) = _lru_bwd(
        proj, hsave, dmix, dproj, w["lru_conv_w"], _behind(w["lru_conv_b"], token), w["lru_wa"], w["lru_wx"],
        w["lru_ba"], w["lru_bx"], w["lru_lambda"])
    gate_bias_grad, dproj = _gate_grads(dgate, dproj)
    grads["b_gate_m"] = gate_bias_grad[:, :2 * HEADS]
    grads["w_in"] = _mm_tn("proj_bwd_w", n1, dproj, 1024, tn_proj, 2048)
    token = comm.reduce_late(grads)
    dn1 = _mm_nt("proj_bwd_x", dproj, w["w_in"], tm, 512, _PROJ_PAD, after=token)
    token = comm.reduce_late_mid(dn1)
    grad_x, _, grads["norm_mix_g"] = _rmsnorm_bwd("norm_mix_bwd", x, rstd1, _behind(w["norm_mix_g"], token),
                                                  dn1, dx1)
    return loss, grad_x, grads


WEIGHT_NAMES = ("norm_mix_g", "w_in", "b_gate_m", "mlstm_norm_g", "lru_conv_w", "lru_conv_b", "lru_wa", "lru_ba",
                "lru_wx", "lru_bx", "lru_lambda", "w_out", "norm_ffn_g", "w_up", "ffn_conv_w", "ffn_conv_b",
                "w_down", "norm_final_g")
BIG = ("w_in", "w_out", "w_up", "w_down")
SMALL_SHARDED = ("mlstm_norm_g", "lru_conv_w", "ffn_conv_w")
SMALL = tuple(n for n in WEIGHT_NAMES if n not in BIG)
SMALL_REPLICATED = tuple(n for n in SMALL if n not in SMALL_SHARDED)


def _proj_segments():
    segs = [(0, 0, _QKVO), (_QKVO, _QKVO + 2 * D_LRU, _N_GATES)]
    for n in range(LRU_BLOCKS):
        segs.append((_QKVO + _N_GATES + n * LRU_BLOCK_DIM, _QKVO + 2 * n * LRU_BLOCK_DIM, LRU_BLOCK_DIM))
        segs.append((_QKVO + _N_GATES + D_LRU + n * LRU_BLOCK_DIM, _QKVO + (2 * n + 1) * LRU_BLOCK_DIM,
                     LRU_BLOCK_DIM))
    return segs


def _w_in_shards_to_local(shards):
    width = shards.shape[2]
    pieces = []
    for g0, _, n in sorted(_proj_segments(), key=lambda s: s[1]):
        at = g0
        while at < g0 + n:
            j = at // width
            stop = min(g0 + n, (j + 1) * width)
            pieces.append(shards[j][:, at - j * width:stop - j * width])
            at = stop
    pieces.append(jnp.zeros((shards.shape[1], PROJ_GATE_PAD - _N_GATES), shards.dtype))
    return jnp.concatenate(pieces, axis=1)


def _w_in_local_to_shards(w):
    width = _PROJ_COLS // N_CHIPS
    shards = []
    for j in range(N_CHIPS):
        pieces = []
        for g0, l0, n in sorted(_proj_segments()):
            lo, hi = max(g0, j * width), min(g0 + n, (j + 1) * width)
            if lo < hi:
                pieces.append(w[:, l0 + lo - g0:l0 + hi - g0])
        shards.append(jnp.concatenate(pieces, axis=1))
    return jnp.stack(shards)


def _w_in_to_global(w):
    sh = _w_in_local_to_shards(w)
    return jnp.concatenate([sh[j] for j in range(N_CHIPS)], axis=1)


def _size(shp):
    return functools.reduce(lambda a, b: a * b, shp, 1)


def _lane_dense(shp):
    return len(shp) >= 2 and shp[-1] == LANES and _size(shp) % (HALO * LANES) == 0


def _pack_rows(shapes):
    loose = sum(_size(shp) for shp in shapes if not _lane_dense(shp))
    return sum(_size(shp) // LANES for shp in shapes if _lane_dense(shp)) + -(-loose // (HALO * LANES)) * HALO


def _pack(arrs, rows):
    del rows
    parts = [a.reshape(-1, LANES).astype(F32) for a in arrs if _lane_dense(a.shape)]
    loose = [a.reshape(-1).astype(F32) for a in arrs if not _lane_dense(a.shape)]
    if loose:
        flat = jnp.concatenate(loose)
        n = -(-flat.shape[0] // (HALO * LANES)) * HALO * LANES
        parts.append(jnp.pad(flat, (0, n - flat.shape[0])).reshape(-1, LANES))
    return parts[0] if len(parts) == 1 else jnp.concatenate(parts, axis=0)


def _unpack(buf, shapes):
    out, row = {}, 0
    for i, shp in enumerate(shapes):
        if _lane_dense(shp):
            n = _size(shp) // LANES
            out[i] = buf[row:row + n].reshape(shp)
            row += n
    flat, at = buf[row:].reshape(-1), 0
    for i, shp in enumerate(shapes):
        if not _lane_dense(shp):
            out[i] = flat[at:at + _size(shp)].reshape(shp)
            at += _size(shp)
    return [out[i] for i in range(len(shapes))]


def _assemble_weights(g_in, g_out, g_up, g_down, small_sharded, replicated):
    w = dict(replicated)
    w["w_in"] = _w_in_shards_to_local(g_in)
    w["w_out"] = g_out.reshape(-1, g_out.shape[-1])
    w["w_up"] = g_up
    w["w_down"] = g_down.reshape(-1, g_down.shape[-1])
    for name, v in small_sharded.items():
        w[name] = jnp.concatenate([v[j] for j in range(N_CHIPS)], axis=1)
    return w


def _full_weights_from_global(weights):
    shard = lambda a, axis: jnp.stack(jnp.split(a, N_CHIPS, axis=axis))
    rep = {n: weights[n].reshape(1, -1) if weights[n].ndim <= 2 and n != "b_gate_m" else weights[n]
           for n in SMALL_REPLICATED}
    rep["b_gate_m"] = weights["b_gate_m"].reshape(1, -1)
    return _assemble_weights(shard(weights["w_in"], 1).astype(BF16), shard(weights["w_out"], 0).astype(BF16),
                             shard(weights["w_up"], 1).astype(BF16), shard(weights["w_down"], 0).astype(BF16),
                             {n: shard(weights[n], 1) for n in SMALL_SHARDED}, rep)


def _grads_to_global(grads):
    g = dict(grads)
    g["w_in"] = _w_in_to_global(grads["w_in"])
    g["w_up"] = jnp.concatenate([grads["w_up"][j] for j in range(N_CHIPS)], axis=1)
    return g


def _place():
    x, y, c = lax.axis_index("x"), lax.axis_index("y"), lax.axis_index("c")
    chips = [(1 - x, y), (x, 1 - y), (1 - x, 1 - y)]
    return x, y, c, 2 * x + y, chips


def _half_rows(n_rows, which):
    half = n_rows // 2
    return pl.ds(pl.multiple_of(which * half, 16), half)


def _rcopy(src, dst, send_sem, recv_sem, to):
    return pltpu.make_async_remote_copy(src_ref=src, dst_ref=dst, send_sem=send_sem, recv_sem=recv_sem,
                                        device_id=to, device_id_type=MESH)


HBM_SPEC = pl.BlockSpec(memory_space=pltpu.HBM)
SEM_SPEC = pl.BlockSpec(memory_space=pltpu.SEMAPHORE)
TOKEN_SHAPE = (8, LANES)


def _split_call(name, bufs, sems_in, sems_out_shapes, body_fn, after=None):
    nb, ni, no = len(bufs), len(sems_in), len(sems_out_shapes)

    def body(*refs):
        buf_refs = refs[:nb]
        sem_in_refs = refs[nb:nb + ni]
        outs = refs[nb + ni + (0 if after is None else 1):]
        sem_out_refs = outs[:no]
        token_ref = outs[no + nb]
        body_fn(buf_refs, sem_in_refs, sem_out_refs)
        token_ref[...] = jnp.zeros_like(token_ref)

    out_shape = ([pltpu.SemaphoreType.DMA(shp) for shp in sems_out_shapes]
                 + [pltpu.HBM(b.shape, b.dtype) for b in bufs] + [jax.ShapeDtypeStruct(TOKEN_SHAPE, F32)])
    res = pl.pallas_call(
        body, name=name, out_shape=out_shape,
        in_specs=[HBM_SPEC] * nb + [SEM_SPEC] * ni + ([] if after is None else [ANY]),
        out_specs=[SEM_SPEC] * no + [HBM_SPEC] * nb + [pl.BlockSpec(memory_space=pltpu.VMEM)],
        input_output_aliases={i: no + i for i in range(nb)},
        compiler_params=pltpu.CompilerParams(has_side_effects=pltpu.SideEffectType.DATAFLOW_SIDE_EFFECTING),
    )(*[pltpu.with_memory_space_constraint(b, pltpu.HBM) for b in bufs], *sems_in,
      *(() if after is None else (after,)))
    return list(res[:no]), list(res[no:no + nb]), res[no + nb]


def _place_own_shard(name, idx, shard, after=None):
    rows, cols = shard.shape
    tr = _row_tile(rows)

    def body(idx_ref, s_ref, *rest):
        rest[-1][...] = s_ref[...].astype(BF16)

    return pl.pallas_call(
        body, name=name, out_shape=jax.ShapeDtypeStruct((N_CHIPS, rows, cols), BF16),
        grid_spec=pltpu.PrefetchScalarGridSpec(
            num_scalar_prefetch=1, grid=(rows // tr,),
            in_specs=[pl.BlockSpec((tr, cols), lambda i, s: (i, 0))] + ([] if after is None else [ANY]),
            out_specs=pl.BlockSpec((None, tr, cols), lambda i, s: (s[1], i, 0))),
        compiler_params=_params(("parallel",)),
    )(idx, shard, *(() if after is None else (after,)))


GATHER_GROUPS = ((0, 4), (1,), (2,), (3,))


def _gather_start(name, lands, groups, after=None):
    members = [w for g in groups for w in GATHER_GROUPS[g]]

    def starts(bufs, _, sems):
        x, y, c, me, chips = _place()
        for gi, g in enumerate(groups):
            for pos, w in enumerate(GATHER_GROUPS[g]):
                buf = bufs[members.index(w)]
                part = buf.at[me] if w == 4 else buf.at[me, _half_rows(buf.shape[1], c)]
                for k, chip in enumerate(chips):
                    _rcopy(part, part, sems[2 * gi].at[3 * pos + k], sems[2 * gi + 1].at[3 * pos + k],
                           (*chip, c)).start()

    shapes = []
    for g in groups:
        shapes += [(3 * len(GATHER_GROUPS[g]),)] * 2
    sems, bufs, token = _split_call(name, [lands[w] for w in members], [], shapes, starts, after=after)
    return ({g: (sems[2 * gi], sems[2 * gi + 1]) for gi, g in enumerate(groups)},
            dict(zip(members, bufs)), token)


def _gather_mid(grp, lands, sems, after):
    members = GATHER_GROUPS[grp]
    big = [w for w in members if w != 4]

    def mid(bufs, sems_in, sems_out):
        x, y, c, me, chips = _place()
        send_sems, recv_sems = sems_in
        for pos, w in enumerate(members):
            for k, chip in enumerate(chips):
                cid = 2 * chip[0] + chip[1]
                buf = bufs[pos]
                mine = buf.at[me] if w == 4 else buf.at[me, _half_rows(buf.shape[1], c)]
                theirs = buf.at[cid] if w == 4 else buf.at[cid, _half_rows(buf.shape[1], c)]
                arrival = _rcopy(mine, theirs, send_sems.at[3 * pos + k], recv_sems.at[3 * pos + k], (*chip, c))
                arrival.wait_recv()
                arrival.wait_send()
                if w != 4:
                    _rcopy(theirs, theirs, sems_out[0].at[3 * big.index(w) + k],
                           sems_out[1].at[3 * big.index(w) + k], (x, y, 1 - c)).start()

    new_sems, bufs, token = _split_call(f"gather_mid_{grp}", [lands[w] for w in members], list(sems),
                                        [(3 * len(big),), (3 * len(big),)], mid, after=after)
    return new_sems, bufs, token


def _gather_end(grp, bufs, sems, after):
    members = GATHER_GROUPS[grp]
    big = [w for w in members if w != 4]

    def end(refs, sems_in, _):
        x, y, c, me, chips = _place()
        send_sems, recv_sems = sems_in
        for pos, w in enumerate(members):
            if w == 4:
                continue
            for k, chip in enumerate(chips):
                cid = 2 * chip[0] + chip[1]
                buf = refs[pos]
                sent = buf.at[cid, _half_rows(buf.shape[1], c)]
                landed = buf.at[cid, _half_rows(buf.shape[1], 1 - c)]
                fwd = _rcopy(sent, landed, send_sems.at[3 * big.index(w) + k], recv_sems.at[3 * big.index(w) + k],
                             (x, y, 1 - c))
                fwd.wait_recv()
                fwd.wait_send()

    _, bufs, token = _split_call(f"gather_end_{grp}", bufs, list(sems), [], end, after=after)
    return bufs, token


def _pair_start(name, grads, extra=None):
    n = len(grads)
    bufs = list(grads) + [lax.empty((g.shape[0], g.shape[1] // 2, g.shape[2]), g.dtype) for g in grads]
    if extra is not None:
        bufs += [extra, lax.empty(extra.shape, extra.dtype)]

    def starts(refs, _, sems):
        x, y, c, _, _ = _place()
        for w in range(n):
            other = _half_rows(refs[w].shape[1], 1 - c)
            _rcopy(refs[w].at[:, other], refs[n + w], sems[0].at[w], sems[1].at[w], (x, y, 1 - c)).start()
        if extra is not None:
            _rcopy(refs[2 * n], refs[2 * n + 1], sems[0].at[n], sems[1].at[n], (x, y, 1 - c)).start()

    count = n + (extra is not None)
    return _split_call(name, bufs, [], [(count,), (count,)], starts)


def _pair_wait(name, n, bufs, sems, after):
    has_extra = len(bufs) > 2 * n

    def waits(refs, sems_in, _):
        x, y, c, _, _ = _place()
        for w in range(n):
            other = _half_rows(refs[w].shape[1], 1 - c)
            cp = _rcopy(refs[w].at[:, other], refs[n + w], sems_in[0].at[w], sems_in[1].at[w], (x, y, 1 - c))
            cp.wait_recv()
            cp.wait_send()
        if has_extra:
            cp = _rcopy(refs[2 * n], refs[2 * n + 1], sems_in[0].at[n], sems_in[1].at[n], (x, y, 1 - c))
            cp.wait_recv()
            cp.wait_send()

    _, bufs, token = _split_call(name, bufs, list(sems), [], waits, after=after)
    return bufs, token


def _chip_start(name, partials, small=None):
    n = len(partials)
    bufs = list(partials) + [lax.empty(p.shape, p.dtype) for p in partials] + ([] if small is None else [small])

    def starts(refs, _, sems):
        _, _, c, me, chips = _place()
        for w in range(n):
            for k, chip in enumerate(chips):
                cid = 2 * chip[0] + chip[1]
                _rcopy(refs[w].at[cid], refs[n + w].at[me], sems[0].at[3 * w + k], sems[1].at[3 * w + k],
                       (*chip, c)).start()
        if small is not None:
            for k, chip in enumerate(chips):
                _rcopy(refs[2 * n].at[me], refs[2 * n].at[me], sems[0].at[3 * n + k], sems[1].at[3 * n + k],
                       (*chip, c)).start()

    count = 3 * (n + (small is not None))
    return _split_call(name, bufs, [], [(count,), (count,)], starts)


def _chip_wait(name, n, bufs, sems, after):
    has_small = len(bufs) > 2 * n

    def waits(refs, sems_in, _):
        _, _, c, me, chips = _place()
        for w in range(n):
            for k, chip in enumerate(chips):
                cid = 2 * chip[0] + chip[1]
                cp = _rcopy(refs[w].at[cid], refs[n + w].at[cid], sems_in[0].at[3 * w + k],
                            sems_in[1].at[3 * w + k], (*chip, c))
                cp.wait_recv()
                cp.wait_send()
        if has_small:
            for k, chip in enumerate(chips):
                cid = 2 * chip[0] + chip[1]
                cp = _rcopy(refs[2 * n].at[me], refs[2 * n].at[cid], sems_in[0].at[3 * n + k],
                            sems_in[1].at[3 * n + k], (*chip, c))
                cp.wait_recv()
                cp.wait_send()

    _, bufs, token = _split_call(name, bufs, list(sems), [], waits, after=after)
    return bufs, token


def _small_pair_sum(idx, own, recv):
    rows = own.shape[0]

    def body(idx_ref, a_ref, b_ref, o_ref):
        o_ref[...] = a_ref[...] + b_ref[...]

    blk = pl.BlockSpec((rows, LANES), lambda i, s: (0, 0))
    return pl.pallas_call(
        body, name="small_pair_sum", out_shape=jax.ShapeDtypeStruct((N_CHIPS, rows, LANES), F32),
        grid_spec=pltpu.PrefetchScalarGridSpec(
            num_scalar_prefetch=1, grid=(1,), in_specs=[blk, blk],
            out_specs=pl.BlockSpec((None, rows, LANES), lambda i, s: (s[1], 0, 0))),
        compiler_params=_params(("arbitrary",)),
    )(idx, own, recv)


def _gather_weights(shards, small):
    nb = len(shards)

    def body(*refs):
        srcs, small_ref = refs[:nb], refs[nb]
        dsts, small_out = refs[nb + 1:2 * nb + 1], refs[2 * nb + 1]
        send_sems, recv_sems, local_sems = refs[2 * nb + 2:]
        x, y, c, me, chips = _place()
        sibling = (x, y, 1 - c)
        mine = [_half_rows(s.shape[0], c) for s in srcs]
        other = [_half_rows(s.shape[0], 1 - c) for s in srcs]

        local = [pltpu.make_async_copy(srcs[w], dsts[w].at[me], local_sems.at[w]) for w in range(nb)]
        local.append(pltpu.make_async_copy(small_ref, small_out.at[me], local_sems.at[nb]))
        for cp in local:
            cp.start()
        sends = []
        for w in range(nb):
            for k, chip in enumerate(chips):
                sends.append(_rcopy(srcs[w].at[mine[w]], dsts[w].at[me, mine[w]],
                                    send_sems.at[w, k], recv_sems.at[w, k], (*chip, c)))
        for k, chip in enumerate(chips):
            sends.append(_rcopy(small_ref, small_out.at[me], send_sems.at[nb, k], recv_sems.at[nb, k], (*chip, c)))
        for cp in sends:
            cp.start()
        passed = []
        for w in range(nb):
            for k, chip in enumerate(chips):
                cid = 2 * chip[0] + chip[1]
                landed = dsts[w].at[cid, mine[w]]
                _rcopy(landed, landed, send_sems.at[w, k], recv_sems.at[w, k], (*chip, c)).wait_recv()
                fwd = _rcopy(landed, landed, send_sems.at[w, 3 + k], recv_sems.at[w, 3 + k], sibling)
                fwd.start()
                passed.append(fwd)
        for k, chip in enumerate(chips):
            cid = 2 * chip[0] + chip[1]
            _rcopy(small_ref, small_out.at[cid], send_sems.at[nb, k], recv_sems.at[nb, k], (*chip, c)).wait_recv()
        for w in range(nb):
            for k, chip in enumerate(chips):
                cid = 2 * chip[0] + chip[1]
                landed = dsts[w].at[cid, other[w]]
                _rcopy(landed, landed, send_sems.at[w, 3 + k], recv_sems.at[w, 3 + k], sibling).wait_recv()
        for cp in sends + passed:
            cp.wait_send()
        for cp in local:
            cp.wait()

    out_shape = [jax.ShapeDtypeStruct((N_CHIPS,) + s.shape, s.dtype) for s in shards]
    out_shape.append(jax.ShapeDtypeStruct((N_CHIPS,) + small.shape, small.dtype))
    return pl.pallas_call(
        body, name="gather_weights", out_shape=out_shape,
        in_specs=[ANY] * (nb + 1), out_specs=[ANY] * (nb + 1),
        scratch_shapes=[pltpu.SemaphoreType.DMA((nb + 1, 6)), pltpu.SemaphoreType.DMA((nb + 1, 6)),
                        pltpu.SemaphoreType.DMA((nb + 1,))],
    )(*shards, small)


def _pair_exchange(grads, small):
    nb = len(grads)

    def body(*refs):
        srcs, small_ref = refs[:nb], refs[nb]
        dsts, small_out = refs[nb + 1:2 * nb + 1], refs[2 * nb + 1]
        send_sems, recv_sems, small_send, small_recv, local_sem = refs[2 * nb + 2:]
        x, y, c, _, _ = _place()
        sibling = (x, y, 1 - c)
        my_id = 4 * x + 2 * y + c
        local = pltpu.make_async_copy(small_ref, small_out.at[my_id], local_sem)
        local.start()
        sends = []
        for w in range(nb):
            other = _half_rows(srcs[w].shape[1], 1 - c)
            sends.append(_rcopy(srcs[w].at[:, other], dsts[w], send_sems.at[w], recv_sems.at[w], sibling))
        for r in range(1, N_DEV):
            to = (1 - x if r & 4 else x, 1 - y if r & 2 else y, 1 - c if r & 1 else c)
            sends.append(_rcopy(small_ref, small_out.at[my_id], small_send.at[r - 1], small_recv.at[r - 1], to))
        for cp in sends:
            cp.start()
        for w in range(nb):
            _rcopy(dsts[w], dsts[w], send_sems.at[w], recv_sems.at[w], sibling).wait_recv()
        for r in range(1, N_DEV):
            frm = (1 - x if r & 4 else x, 1 - y if r & 2 else y, 1 - c if r & 1 else c)
            frm_id = 4 * frm[0] + 2 * frm[1] + frm[2]
            _rcopy(small_ref, small_out.at[frm_id], small_send.at[r - 1], small_recv.at[r - 1], frm).wait_recv()
        for cp in sends:
            cp.wait_send()
        local.wait()

    out_shape = [jax.ShapeDtypeStruct((g.shape[0], g.shape[1] // 2, g.shape[2]), g.dtype) for g in grads]
    out_shape.append(jax.ShapeDtypeStruct((N_DEV,) + small.shape, small.dtype))
    return pl.pallas_call(
        body, name="pair_exchange", out_shape=out_shape,
        in_specs=[ANY] * (nb + 1), out_specs=[ANY] * (nb + 1),
        scratch_shapes=[pltpu.SemaphoreType.DMA((nb,)), pltpu.SemaphoreType.DMA((nb,)),
                        pltpu.SemaphoreType.DMA((N_DEV - 1,)), pltpu.SemaphoreType.DMA((N_DEV - 1,)),
                        pltpu.SemaphoreType.DMA(())],
    )(*grads, small)


def _chip_exchange(partials):
    nb = len(partials)

    def body(*refs):
        srcs, dsts = refs[:nb], refs[nb:2 * nb]
        send_sems, recv_sems = refs[2 * nb:]
        _, _, c, me, chips = _place()
        sends = []
        for w in range(nb):
            for k, chip in enumerate(chips):
                cid = 2 * chip[0] + chip[1]
                sends.append(_rcopy(srcs[w].at[cid], dsts[w].at[me], send_sems.at[w, k], recv_sems.at[w, k],
                                    (*chip, c)))
        for cp in sends:
            cp.start()
        for w in range(nb):
            for k, chip in enumerate(chips):
                cid = 2 * chip[0] + chip[1]
                _rcopy(srcs[w].at[cid], dsts[w].at[cid], send_sems.at[w, k], recv_sems.at[w, k],
                       (*chip, c)).wait_recv()
        for cp in sends:
            cp.wait_send()

    return pl.pallas_call(
        body, name="chip_exchange", out_shape=[jax.ShapeDtypeStruct(p.shape, p.dtype) for p in partials],
        in_specs=[ANY] * nb, out_specs=[ANY] * nb,
        scratch_shapes=[pltpu.SemaphoreType.DMA((nb, 3)), pltpu.SemaphoreType.DMA((nb, 3))],
    )(*partials)


def _pair_share(name, shards, late=None):
    nb = len(shards)
    nl = 0 if late is None else 1

    def body(*refs):
        srcs = refs[:nb]
        dsts = refs[nb + nl:2 * nb + nl]
        send_sems, recv_sems = refs[2 * nb + 2 * nl:2 * nb + 2 * nl + 2]
        x, y, c, _, _ = _place()
        sibling = (x, y, 1 - c)
        sends = []
        for w in range(nb):
            mine = _half_rows(dsts[w].shape[0], c)
            sends.append(_rcopy(srcs[w].at[mine], dsts[w].at[mine], send_sems.at[w], recv_sems.at[w], sibling))
        if nl:
            late_ref, late_out = refs[nb], refs[2 * nb + 1]
            late_send, late_recv, local_sem = refs[2 * nb + 4:]
            my_id = 4 * x + 2 * y + c
            peer = lambda r: (1 - x if r & 4 else x, 1 - y if r & 2 else y, 1 - c if r & 1 else c)
            local = pltpu.make_async_copy(late_ref, late_out.at[my_id], local_sem)
            local.start()
            for r in range(1, N_DEV):
                sends.append(_rcopy(late_ref, late_out.at[my_id], late_send.at[r - 1], late_recv.at[r - 1],
                                    peer(r)))
        for cp in sends:
            cp.start()
        for w in range(nb):
            other = _half_rows(dsts[w].shape[0], 1 - c)
            _rcopy(srcs[w].at[other], dsts[w].at[other], send_sems.at[w], recv_sems.at[w], sibling).wait_recv()
        if nl:
            for r in range(1, N_DEV):
                frm = peer(r)
                _rcopy(late_ref, late_out.at[4 * frm[0] + 2 * frm[1] + frm[2]], late_send.at[r - 1],
                       late_recv.at[r - 1], frm).wait_recv()
        for cp in sends:
            cp.wait_send()
        if nl:
            local.wait()

    out_shape = [jax.ShapeDtypeStruct(h.shape, h.dtype) for h in shards]
    scratch = [pltpu.SemaphoreType.DMA((nb,)), pltpu.SemaphoreType.DMA((nb,))]
    if nl:
        out_shape.append(jax.ShapeDtypeStruct((N_DEV,) + late.shape, late.dtype))
        scratch += [pltpu.SemaphoreType.DMA((N_DEV - 1,)), pltpu.SemaphoreType.DMA((N_DEV - 1,)),
                    pltpu.SemaphoreType.DMA(())]
    return pl.pallas_call(
        body, name=name, out_shape=out_shape,
        in_specs=[ANY] * (nb + nl), out_specs=[ANY] * (nb + nl), scratch_shapes=scratch,
        input_output_aliases={w: w for w in range(nb)},
    )(*shards, *(() if late is None else (late,)))


def _row_tile(rows):
    return _pick(rows, 128, 64, 16, 8)


def _pair_sum(name, idx, grad, recv):
    n, half, cols = recv.shape
    tr = _row_tile(half)
    nrb = half // tr

    def body(idx_ref, g_ref, r_ref, o_ref):
        o_ref[...] = (g_ref[...] + r_ref[...]).astype(BF16)

    return pl.pallas_call(
        body, name=name, out_shape=jax.ShapeDtypeStruct(recv.shape, BF16),
        grid_spec=pltpu.PrefetchScalarGridSpec(
            num_scalar_prefetch=1, grid=(n, nrb),
            in_specs=[pl.BlockSpec((None, tr, cols), lambda j, i, s: (j, s[0] * nrb + i, 0)),
                      pl.BlockSpec((None, tr, cols), lambda j, i, s: (j, i, 0))],
            out_specs=pl.BlockSpec((None, tr, cols), lambda j, i, s: (j, i, 0))),
        compiler_params=_params(("parallel", "parallel")),
    )(idx, grad, recv)


def _final_sum(name, idx, grad, recv, chip_sums):
    _, half, cols = recv.shape
    tr = _row_tile(half)
    nrb = half // tr

    def body(idx_ref, g_ref, r_ref, p1_ref, p2_ref, p3_ref, o_ref):
        acc = g_ref[...] + r_ref[...]
        for p_ref in (p1_ref, p2_ref, p3_ref):
            acc = acc + p_ref[...].astype(F32)
        o_ref[...] = acc

    slot = lambda which: pl.BlockSpec((None, tr, cols), lambda i, s, which=which: (s[which], i, 0))
    return pl.pallas_call(
        body, name=name, out_shape=jax.ShapeDtypeStruct((2 * half, cols), F32),
        grid_spec=pltpu.PrefetchScalarGridSpec(
            num_scalar_prefetch=1, grid=(nrb,),
            in_specs=[pl.BlockSpec((None, tr, cols), lambda i, s: (s[1], s[0] * nrb + i, 0)),
                      slot(1), slot(2), slot(3), slot(4)],
            out_specs=pl.BlockSpec((tr, cols), lambda i, s: (s[0] * nrb + i, 0))),
        compiler_params=_params(("parallel",)),
    )(idx, grad, recv, chip_sums, chip_sums, chip_sums)


def _small_sum(name, packs):
    n, rows, _ = packs.shape

    def body(p_ref, o_ref):
        acc = p_ref[0]
        for k in range(1, n):
            acc = acc + p_ref[k]
        o_ref[...] = acc

    return pl.pallas_call(
        body, name=name, out_shape=jax.ShapeDtypeStruct((rows, LANES), F32),
        in_specs=[pl.BlockSpec(memory_space=pltpu.VMEM)], out_specs=pl.BlockSpec(memory_space=pltpu.VMEM),
        compiler_params=pltpu.CompilerParams(vmem_limit_bytes=VMEM_LIMIT),
    )(packs)


def _adamw(name, w, g, m, v):
    rows, cols = w.shape
    tr = rows if rows * cols * 4 <= (2 << 20) else _row_tile(rows)

    def body(w_ref, g_ref, m_ref, v_ref, g_out_ref, d_ref, nm_ref, nv_ref):
        gv = g_ref[...]
        g_out_ref[...] = gv
        m_new = ADAM_B1 * m_ref[...] + (1.0 - ADAM_B1) * gv
        v_new = ADAM_B2 * v_ref[...] + (1.0 - ADAM_B2) * (gv * gv)
        m_hat = m_new / (1.0 - ADAM_B1 ** ADAM_STEP)
        v_hat = v_new / (1.0 - ADAM_B2 ** ADAM_STEP)
        d_ref[...] = -ADAM_LR * (m_hat / (jnp.sqrt(v_hat) + ADAM_EPS) + ADAM_WD * w_ref[...])
        nm_ref[...] = m_new
        nv_ref[...] = v_new

    blk = pl.BlockSpec((tr, cols), lambda i: (i, 0))
    sds = jax.ShapeDtypeStruct((rows, cols), F32)
    return pl.pallas_call(
        body, name=name, grid=(rows // tr,), in_specs=[blk] * 4, out_specs=[blk] * 4, out_shape=[sds] * 4,
        compiler_params=_params(("parallel",)),
    )(w, g, m, v)


def _train_step(x, target, W, M, V):
    xi, yi, ci = lax.axis_index("x"), lax.axis_index("y"), lax.axis_index("c")
    me = 2 * xi + yi
    big = {n: W[n][0] for n in BIG}

    others = [jnp.where(jnp.int32(i) >= me, i + 1, i) for i in range(N_CHIPS - 1)]
    idx = jnp.stack([ci, me] + others).astype(jnp.int32)

    sharded_shapes = [W[n].shape[1:] for n in SMALL_SHARDED]
    small_pack = _pack([W[n][0] for n in SMALL_SHARDED], _pack_rows(sharded_shapes))
    small_land = lax.dynamic_update_slice(jnp.zeros((N_CHIPS,) + small_pack.shape, F32), small_pack[None],
                                          (me, 0, 0))
    replicated = {n: (W[n].reshape(1, -1) if W[n].ndim <= 2 else W[n][0]) for n in SMALL_REPLICATED}

    early = ("w_out", "w_up", "w_down")
    small_late = "norm_mix_g"
    small_early = tuple(n for n in SMALL if n != small_late)
    global_shape = lambda n: ((W[n].shape[1], W[n].shape[2] * N_CHIPS) if n in SMALL_SHARDED else
                              tuple(W[n].shape) if W[n].ndim == 1 else tuple(W[n].shape[1:]))
    small_shapes = [global_shape(n) for n in small_early]

    def shard_major(n, g):
        if n == "w_in":
            return _w_in_local_to_shards(g)
        return g if g.ndim == 3 else g.reshape((N_CHIPS, -1) + g.shape[1:])

    class _SplitComm:
        def reduce_early(self, grads):
            self.e_sems, self.e_bufs, token = _pair_start("pair_start_early",
                                                          [shard_major(n, grads[n]) for n in early])
            return token

        def reduce_early_mid(self, after):
            n = len(early)
            bufs, _ = _pair_wait("pair_wait_early", n, self.e_bufs, self.e_sems, after)
            self.e_grads, self.e_recv = bufs[:n], bufs[n:2 * n]
            partial = [_pair_sum(f"pair_sum_{nm}", idx, g, r) for nm, g, r in zip(early, self.e_grads, self.e_recv)]
            self.e_sems, self.e_bufs, token = _chip_start("chip_start_early", partial)
            return token

        def reduce_late(self, grads):
            pack = _pack([grads[n] for n in small_early], _pack_rows(small_shapes))
            self.l_sems, self.l_bufs, token = _pair_start("pair_start_late", [shard_major("w_in", grads["w_in"])],
                                                          extra=pack)
            return token

        def reduce_late_mid(self, after):
            bufs, _ = _pair_wait("pair_wait_late", 1, self.l_bufs, self.l_sems, after)
            self.l_grads, self.l_recv = bufs[:1], bufs[1:2]
            partial = [_pair_sum("pair_sum_w_in", idx, bufs[0], bufs[1])]
            self.l_sems, self.l_bufs, token = _chip_start("chip_start_late", partial,
                                                          small=_small_pair_sum(idx, bufs[2], bufs[3]))
            return token

        def finish_early(self, after):
            n = len(early)
            bufs, _ = _chip_wait("chip_wait_early", n, self.e_bufs, self.e_sems, after)
            halves = [_final_sum(f"final_sum_{nm}", idx, g, r, p)
                      for nm, g, r, p in zip(early, self.e_grads, self.e_recv, bufs[n:2 * n])]
            return dict(zip(early, _pair_share("pair_share_early", halves)))

        def finish_late(self, after, late):
            bufs, _ = _chip_wait("chip_wait_late", 1, self.l_bufs, self.l_sems, after)
            half = _final_sum("final_sum_w_in", idx, self.l_grads[0], self.l_recv[0], bufs[1])
            small = dict(zip(small_early, _unpack(_small_sum("small_sum", bufs[2]), small_shapes)))
            whole, late_all = _pair_share("pair_share_late", [half], late)
            return whole, small, _small_sum("late_sum", late_all)

        def begin(self):
            first = {0: _place_own_shard("place_w_in", idx, big["w_in"]), 4: small_land}
            self.sems, self.lands, token = _gather_start("gather_start_0", first, (0,))
            rest = {i: _place_own_shard(f"place_{BIG[i]}", idx, big[BIG[i]], after=token) for i in (1, 2, 3)}
            sems, lands, token = _gather_start("gather_start_1", rest, (1, 2, 3), after=token)
            self.sems.update(sems)
            self.lands.update(lands)
            return token

        def mid(self, grp, after):
            self.pending = _gather_mid(grp, self.lands, self.sems[grp], after)
            return self.pending[2]

        def end(self, grp, after):
            sems, bufs, _ = self.pending
            bufs, _ = _gather_end(grp, bufs, sems, after)
            if grp == 0:
                per_chip = [_unpack(bufs[1][j], sharded_shapes) for j in range(N_CHIPS)]
                out = {n: jnp.concatenate([per_chip[j][i] for j in range(N_CHIPS)], axis=1)
                       for i, n in enumerate(SMALL_SHARDED)}
                out["w_in"] = _w_in_shards_to_local(bufs[0])
                return out
            if grp == 2:
                return {"w_up": bufs[0]}
            return {("w_out" if grp == 1 else "w_down"): bufs[0].reshape(-1, bufs[0].shape[-1])}

    comm = _SplitComm()
    loss, grad_x, grads = _local_step(x[0], target[0], replicated, comm)
    loss = lax.psum(loss[0, 0], ("x", "y", "c"))
    out_g, out_d, out_m, out_v = {}, {}, {}, {}

    def update_big(n, grad):
        g, d, nm, nv = _adamw(f"adamw_{n}", big[n], grad, M[n][0], V[n][0])
        out_g[n], out_d[n], out_m[n], out_v[n] = g[None], d[None], nm[None], nv[None]
        return d

    early_grads = comm.finish_early(grad_x)
    for n in early:
        last = update_big(n, early_grads[n])
    late = _pack([grads[small_late]], _pack_rows([global_shape(small_late)]))
    w_in_grad, small_grads, late_sum = comm.finish_late(last, late)
    update_big("w_in", w_in_grad)
    small_grads[small_late] = _unpack(late_sum, [global_shape(small_late)])[0]
    for n in SMALL_SHARDED:
        width = W[n].shape[2]
        small_grads[n] = lax.dynamic_slice_in_dim(small_grads[n], me * width, width, axis=1)

    local_shapes = [tuple(W[n].shape) for n in SMALL]
    rows = _pack_rows(local_shapes)
    packed = [_pack([src[n] for n in SMALL], rows) for src in (W, small_grads, M, V)]
    _, d, nm, nv = _adamw("adamw_small", *packed)
    for dst, buf in ((out_d, d), (out_m, nm), (out_v, nv)):
        dst.update(zip(SMALL, _unpack(buf, local_shapes)))
    for n in SMALL:
        out_g[n] = small_grads[n].reshape(W[n].shape)
    return (loss, grad_x[None], *[out_g[n] for n in WEIGHT_NAMES], *[out_d[n] for n in WEIGHT_NAMES],
            *[out_m[n] for n in WEIGHT_NAMES], *[out_v[n] for n in WEIGHT_NAMES])


def kernel(x, norm_mix_g, w_in, b_gate_m, mlstm_norm_g, lru_conv_w, lru_conv_b, lru_wa, lru_ba, lru_wx, lru_bx, lru_lambda, w_out, norm_ffn_g, w_up, ffn_conv_w, ffn_conv_b, w_down, norm_final_g, loss_target, m_norm_mix_g, m_w_in, m_b_gate_m, m_mlstm_norm_g, m_lru_conv_w, m_lru_conv_b, m_lru_wa, m_lru_ba, m_lru_wx, m_lru_bx, m_lru_lambda, m_w_out, m_norm_ffn_g, m_w_up, m_ffn_conv_w, m_ffn_conv_b, m_w_down, m_norm_final_g, v_norm_mix_g, v_w_in, v_b_gate_m, v_mlstm_norm_g, v_lru_conv_w, v_lru_conv_b, v_lru_wa, v_lru_ba, v_lru_wx, v_lru_bx, v_lru_lambda, v_w_out, v_norm_ffn_g, v_w_up, v_ffn_conv_w, v_ffn_conv_b, v_w_down, v_norm_final_g):
    W = dict(zip(WEIGHT_NAMES, (norm_mix_g, w_in, b_gate_m, mlstm_norm_g, lru_conv_w, lru_conv_b, lru_wa, lru_ba,
                                lru_wx, lru_bx, lru_lambda, w_out, norm_ffn_g, w_up, ffn_conv_w, ffn_conv_b,
                                w_down, norm_final_g)))
    M = dict(zip(WEIGHT_NAMES, (m_norm_mix_g, m_w_in, m_b_gate_m, m_mlstm_norm_g, m_lru_conv_w, m_lru_conv_b,
                                m_lru_wa, m_lru_ba, m_lru_wx, m_lru_bx, m_lru_lambda, m_w_out, m_norm_ffn_g,
                                m_w_up, m_ffn_conv_w, m_ffn_conv_b, m_w_down, m_norm_final_g)))
    V = dict(zip(WEIGHT_NAMES, (v_norm_mix_g, v_w_in, v_b_gate_m, v_mlstm_norm_g, v_lru_conv_w, v_lru_conv_b,
                                v_lru_wa, v_lru_ba, v_lru_wx, v_lru_bx, v_lru_lambda, v_w_out, v_norm_ffn_g,
                                v_w_up, v_ffn_conv_w, v_ffn_conv_b, v_w_down, v_norm_final_g)))
    return _train_step(x, loss_target, W, M, V)
```

```python
import functools

import jax
import jax.numpy as jnp
from jax import lax
from jax.experimental import pallas as pl
from jax.experimental.pallas import tpu as pltpu

F32 = jnp.float32
BF16 = jnp.bfloat16
MESH = pl.DeviceIdType.MESH

EPS = 1e-6
CHUNK = 512
HEADS = 4
HEAD_DIM = 256
D_MLSTM = HEADS * HEAD_DIM
LRU_BLOCKS = 8
LRU_BLOCK_DIM = 128
D_LRU = LRU_BLOCKS * LRU_BLOCK_DIM
LRU_C = 8.0
LRU_CONV = 4
FFN_CONV = 3
ADAM_LR = 0.001
ADAM_B1 = 0.9
ADAM_B2 = 0.999
ADAM_EPS = 1e-08
ADAM_WD = 0.01
ADAM_STEP = 10

N_CHIPS = 4
N_DEV = 8
LANES = 128
HALO = 8
PROJ_GATE_PAD = LANES
_QKVO = 4 * D_MLSTM
_N_GATES = 2 * HEADS
_PROJ_COLS = _QKVO + _N_GATES + 2 * D_LRU
_PROJ_PAD = _QKVO + 2 * D_LRU + PROJ_GATE_PAD
VMEM_LIMIT = 48 * 1024 * 1024
ANY = pl.BlockSpec(memory_space=pl.ANY)


def _params(sem, vmem=VMEM_LIMIT):
    return pltpu.CompilerParams(dimension_semantics=sem, vmem_limit_bytes=vmem)


def _matmul(name, a, b, grid, a_spec, b_spec, o_spec, out_sds, contract, res=None, res_spec=None, after=None):
    nk = grid[2]
    acc_shape = tuple(d for d in o_spec.block_shape if d is not None)

    def body(*refs):
        refs = list(refs)
        a_ref, b_ref = refs[:2]
        r_ref = refs[2] if res is not None else None
        o_ref = refs[-1] if nk == 1 else refs[-2]
        acc_ref = None if nk == 1 else refs[-1]
        k = pl.program_id(2)

        def part():
            return lax.dot_general(a_ref[...], b_ref[...], (contract, ((), ())), preferred_element_type=F32)

        def finish(r):
            if r_ref is not None:
                r = r_ref[...] + r
            o_ref[...] = r.astype(o_ref.dtype)

        if nk == 1:
            finish(part())
            return

        @pl.when(k == 0)
        def _():
            acc_ref[...] = part()

        @pl.when(jnp.logical_and(k > 0, k < nk - 1))
        def _():
            acc_ref[...] += part()

        @pl.when(k == nk - 1)
        def _():
            finish(acc_ref[...] + part())

    in_specs = [a_spec, b_spec] + ([] if res is None else [res_spec]) + ([] if after is None else [ANY])
    args = (a, b) + (() if res is None else (res,)) + (() if after is None else (after,))
    if after is not None:
        inner = body
        body = lambda *refs: inner(*refs[:len(in_specs) - 1], *refs[len(in_specs):])
    return pl.pallas_call(
        body, out_shape=out_sds, grid=grid, in_specs=in_specs, out_specs=o_spec,
        scratch_shapes=[] if nk == 1 else [pltpu.VMEM(acc_shape, F32)], name=name,
        compiler_params=_params(("parallel", "parallel", "arbitrary")),
    )(*args)


NN = ((1,), (0,))
NT = ((1,), (1,))
TN = ((0,), (0,))


def _mm_nn(name, a, b, tm, tn, tk, out_dtype=F32, res=None):
    m, k = a.shape
    n = b.shape[1]
    return _matmul(name, a, b, (m // tm, n // tn, k // tk),
                   pl.BlockSpec((tm, tk), lambda i, j, kk: (i, kk)),
                   pl.BlockSpec((tk, tn), lambda i, j, kk: (kk, j)),
                   pl.BlockSpec((tm, tn), lambda i, j, kk: (i, j)),
                   jax.ShapeDtypeStruct((m, n), out_dtype), NN,
                   res=res, res_spec=pl.BlockSpec((tm, tn), lambda i, j, kk: (i, j)))


def _mm_nt(name, a, b, tm, tn, tk, out_dtype=F32, res=None, after=None):
    m, k = a.shape
    n = b.shape[0]
    return _matmul(name, a, b, (m // tm, n // tn, k // tk),
                   pl.BlockSpec((tm, tk), lambda i, j, kk: (i, kk)),
                   pl.BlockSpec((tn, tk), lambda i, j, kk: (j, kk)),
                   pl.BlockSpec((tm, tn), lambda i, j, kk: (i, j)),
                   jax.ShapeDtypeStruct((m, n), out_dtype), NT,
                   res=res, res_spec=pl.BlockSpec((tm, tn), lambda i, j, kk: (i, j)), after=after)


def _mm_tn(name, a, b, tm, tn, tk, out_dtype=F32):
    k, m = a.shape
    n = b.shape[1]
    tk = min(tk, k)
    return _matmul(name, a, b, (m // tm, n // tn, k // tk),
                   pl.BlockSpec((tk, tm), lambda i, j, kk: (kk, i)),
                   pl.BlockSpec((tk, tn), lambda i, j, kk: (kk, j)),
                   pl.BlockSpec((tm, tn), lambda i, j, kk: (i, j)),
                   jax.ShapeDtypeStruct((m, n), out_dtype), TN)


def _up_shard(n):
    return 2 * (n % 2) + (n // 2) // 2, (n // 2) % 2


def _mm_up_fwd(name, a, wg_up, tm, tk):
    m, k = a.shape
    _, _, cols = wg_up.shape
    tn = cols // 2
    return _matmul(name, a, wg_up, (m // tm, 2 * N_CHIPS, k // tk),
                   pl.BlockSpec((tm, tk), lambda i, j, kk: (i, kk)),
                   pl.BlockSpec((None, tk, tn), lambda i, j, kk: (_up_shard(j)[0], kk, _up_shard(j)[1])),
                   pl.BlockSpec((tm, tn), lambda i, j, kk: (i, j)),
                   jax.ShapeDtypeStruct((m, 2 * N_CHIPS * tn), F32), NN)


def _mm_up_bwd_x(name, dgu, wg_up, tm, tn):
    m, _ = dgu.shape
    _, d, cols = wg_up.shape
    tk = cols // 2
    nk = N_CHIPS

    def body(a_ref, bg_ref, bu_ref, o_ref, acc_ref):
        k = pl.program_id(2)

        def part():
            dims = (NT, ((), ()))
            return (lax.dot_general(a_ref[:, :tk], bg_ref[...], dims, preferred_element_type=F32)
                    + lax.dot_general(a_ref[:, tk:], bu_ref[...], dims, preferred_element_type=F32))

        @pl.when(k == 0)
        def _():
            acc_ref[...] = part()

        @pl.when(jnp.logical_and(k > 0, k < nk - 1))
        def _():
            acc_ref[...] += part()

        @pl.when(k == nk - 1)
        def _():
            o_ref[...] = acc_ref[...] + part()

    wspec = lambda half: pl.BlockSpec(
        (None, tn, tk), lambda i, j, kk: (_up_shard(2 * kk + half)[0], j, _up_shard(2 * kk + half)[1]))
    return pl.pallas_call(
        body, name=name, grid=(m // tm, d // tn, nk), out_shape=jax.ShapeDtypeStruct((m, d), F32),
        in_specs=[pl.BlockSpec((tm, 2 * tk), lambda i, j, kk: (i, kk)), wspec(0), wspec(1)],
        out_specs=pl.BlockSpec((tm, tn), lambda i, j, kk: (i, j)),
        scratch_shapes=[pltpu.VMEM((tm, tn), F32)],
        compiler_params=_params(("parallel", "parallel", "arbitrary")),
    )(dgu, wg_up, wg_up)


def _mm_up_bwd_w(name, n2, dgu, tm, tk):
    s, d = n2.shape
    tk = min(tk, s)
    tn = dgu.shape[1] // (2 * N_CHIPS)
    return _matmul(name, n2, dgu, (d // tm, 2 * N_CHIPS, s // tk),
                   pl.BlockSpec((tk, tm), lambda i, j, kk: (kk, i)),
                   pl.BlockSpec((tk, tn), lambda i, j, kk: (kk, j)),
                   pl.BlockSpec((None, tm, tn), lambda i, j, kk: (_up_shard(j)[0], i, _up_shard(j)[1])),
                   jax.ShapeDtypeStruct((N_CHIPS, d, 2 * tn), F32), TN)


def _rmsnorm_fwd(name, x, g, tm=256):
    s, d = x.shape

    def body(x_ref, g_ref, n_ref, r_ref):
        xf = x_ref[...]
        r = lax.rsqrt(jnp.mean(xf * xf, axis=-1, keepdims=True) + EPS)
        n_ref[...] = ((xf * r) * g_ref[...]).astype(BF16)
        r_ref[...] = r

    return pl.pallas_call(
        body, grid=(s // tm,), name=name,
        in_specs=[pl.BlockSpec((tm, d), lambda i: (i, 0)), pl.BlockSpec((1, d), lambda i: (0, 0))],
        out_specs=[pl.BlockSpec((tm, d), lambda i: (i, 0)), pl.BlockSpec((tm, 1), lambda i: (i, 0))],
        out_shape=[jax.ShapeDtypeStruct((s, d), BF16), jax.ShapeDtypeStruct((s, 1), F32)],
        compiler_params=_params(("parallel",)),
    )(x, g)


def _rmsnorm_bwd(name, x, rstd, g, dn, dres, tm=256):
    s, d = x.shape

    def body(x_ref, r_ref, g_ref, dn_ref, dres_ref, dx_ref, dxb_ref, dg_ref):
        @pl.when(pl.program_id(0) == 0)
        def _():
            dg_ref[...] = jnp.zeros_like(dg_ref)

        r = r_ref[...]
        xhat = x_ref[...] * r
        dn_v = dn_ref[...]
        dxhat = dn_v * g_ref[...]
        dx = dres_ref[...] + r * (dxhat - xhat * jnp.mean(dxhat * xhat, axis=-1, keepdims=True))
        dx_ref[...] = dx
        dxb_ref[...] = dx.astype(BF16)
        dg_ref[...] += jnp.sum(dn_v * xhat, axis=0, keepdims=True)

    row = pl.BlockSpec((tm, d), lambda i: (i, 0))
    vec = pl.BlockSpec((1, d), lambda i: (0, 0))
    return pl.pallas_call(
        body, grid=(s // tm,), name=name,
        in_specs=[row, pl.BlockSpec((tm, 1), lambda i: (i, 0)), vec, row, row],
        out_specs=[row, row, vec],
        out_shape=[jax.ShapeDtypeStruct((s, d), F32), jax.ShapeDtypeStruct((s, d), BF16),
                   jax.ShapeDtypeStruct((1, d), F32)],
        compiler_params=_params(("arbitrary",)),
    )(x, rstd, g, dn, dres)


def _loss_head(name, x, g, target, tm=256):
    s, d = x.shape

    def body(x_ref, g_ref, t_ref, loss_ref, dx_ref, dxb_ref, dg_ref):
        @pl.when(pl.program_id(0) == 0)
        def _():
            dg_ref[...] = jnp.zeros_like(dg_ref)
            loss_ref[...] = jnp.zeros_like(loss_ref)

        xf = x_ref[...]
        gv = g_ref[...]
        r = lax.rsqrt(jnp.mean(xf * xf, axis=-1, keepdims=True) + EPS)
        xhat = xf * r
        err = xhat * gv - t_ref[...]
        loss_ref[...] += 0.5 * jnp.sum(jnp.mean(err * err, axis=-1, keepdims=True), axis=0, keepdims=True)
        dy = err * (1.0 / d)
        dxhat = dy * gv
        dx = r * (dxhat - xhat * jnp.mean(dxhat * xhat, axis=-1, keepdims=True))
        dx_ref[...] = dx
        dxb_ref[...] = dx.astype(BF16)
        dg_ref[...] += jnp.sum(dy * xhat, axis=0, keepdims=True)

    row = pl.BlockSpec((tm, d), lambda i: (i, 0))
    vec = pl.BlockSpec((1, d), lambda i: (0, 0))
    return pl.pallas_call(
        body, grid=(s // tm,), name=name,
        in_specs=[row, vec, row],
        out_specs=[pl.BlockSpec((1, 1), lambda i: (0, 0)), row, row, vec],
        out_shape=[jax.ShapeDtypeStruct((1, 1), F32), jax.ShapeDtypeStruct((s, d), F32),
                   jax.ShapeDtypeStruct((s, d), BF16), jax.ShapeDtypeStruct((1, d), F32)],
        compiler_params=_params(("arbitrary",)),
    )(x, g, target)


def _sigmoid(v):
    return 1.0 / (1.0 + jnp.exp(-v))


def _log_sigmoid(v):
    return jnp.minimum(v, 0.0) - jnp.log1p(jnp.exp(-jnp.abs(v)))


def _softplus(v):
    return jnp.maximum(v, 0.0) + jnp.log1p(jnp.exp(-jnp.abs(v)))


def _one_minus_exp(z):
    series = -z * (1.0 + z * (0.5 + z * (1.0 / 6.0 + z * (1.0 / 24.0 + z * (1.0 / 120.0)))))
    return jnp.where(z > -0.1, series, 1.0 - jnp.exp(z))


_GELU_K = 0.7978845608028654
_GELU_C = 0.044715


def _gelu(v):
    return 0.5 * v * (1.0 + jnp.tanh(_GELU_K * (v + _GELU_C * v * v * v)))


def _gelu_grad(v):
    t = jnp.tanh(_GELU_K * (v + _GELU_C * v * v * v))
    return 0.5 * (1.0 + t) + 0.5 * v * (1.0 - t * t) * _GELU_K * (1.0 + 3.0 * _GELU_C * v * v)


def _rows(shape):
    return lax.broadcasted_iota(jnp.int32, shape, 0)


def _cols(shape):
    return lax.broadcasted_iota(jnp.int32, shape, 1)


def _shift_down(v, prev, d):
    if d == 0:
        return v
    rolled = pltpu.roll(v, d, axis=0)
    head = jnp.where(_rows((HALO, v.shape[1])) >= d, rolled[:HALO], pltpu.roll(prev, d, axis=0))
    if v.shape[0] == HALO:
        return head
    return jnp.concatenate([head, rolled[HALO:]], axis=0)


def _shift_up(v, nxt, d):
    if d == 0:
        return v
    n = v.shape[0]
    rolled = pltpu.roll(v, n - d, axis=0)
    tail = jnp.where(_rows((HALO, v.shape[1])) < HALO - d, rolled[n - HALO:], pltpu.roll(nxt, HALO - d, axis=0))
    if n == HALO:
        return tail
    return jnp.concatenate([rolled[:n - HALO], tail], axis=0)


def _dot(a, b, contract):
    return lax.dot_general(a.astype(BF16), b.astype(BF16), (contract, ((), ())), preferred_element_type=F32)


def _mlstm_chunk_common(h, q_ref, k_ref, v_ref, gcol_ref, grow_ref, brow_ref, bcol_ref, m_prev):
    L = CHUNK
    sl = slice(h * HEAD_DIM, (h + 1) * HEAD_DIM)
    qh = q_ref[:, sl]
    kh = k_ref[:, sl]
    vh = v_ref[:, sl]
    qs = qh * (HEAD_DIM ** -0.5)
    gates = gcol_ref[...] + brow_ref[...]
    lane = _cols(gates.shape)
    ic = jnp.sum(jnp.where(lane == h, gates, 0.0), axis=1, keepdims=True)
    fc = jnp.sum(jnp.where(lane == HEADS + h, gates, 0.0), axis=1, keepdims=True)
    ir = grow_ref[h:h + 1, :] + bcol_ref[h:h + 1, :]
    fr = grow_ref[HEADS + h:HEADS + h + 1, :] + bcol_ref[HEADS + h:HEADS + h + 1, :]
    logf_c = _log_sigmoid(fc)
    logf_r = _log_sigmoid(fr)
    t_i = _rows((L, L))
    s_i = _cols((L, L))
    tri = t_i >= s_i
    b_c = jnp.sum(jnp.where(tri, logf_r, 0.0), axis=1, keepdims=True)
    b_r = jnp.sum(jnp.where(t_i <= s_i, logf_c, 0.0), axis=0, keepdims=True)
    btot = jnp.sum(logf_r, axis=1, keepdims=True)
    dmat = jnp.where(tri, b_c - b_r + ir, -jnp.inf)
    m_inter = b_c + m_prev
    m_t = jnp.maximum(m_inter, jnp.max(dmat, axis=1, keepdims=True))
    e_mat = jnp.exp(dmat - m_t)
    e_inter = jnp.exp(m_inter - m_t)
    wqk = _dot(qs, kh, NT) * e_mat
    w_end_r = btot - b_r + ir
    m_loc = jnp.max(w_end_r, axis=1, keepdims=True)
    e_end_c = jnp.exp(btot - b_c + ic - m_loc)
    m_new = jnp.maximum(btot + m_prev, m_loc)
    a_dec = jnp.exp(btot + m_prev - m_new)
    c_inj = jnp.exp(m_loc - m_new)
    return dict(qh=qh, kh=kh, vh=vh, qs=qs, fc=fc, tri=tri, t_i=t_i, s_i=s_i, m_t=m_t, e_mat=e_mat,
                e_inter=e_inter, wqk=wqk, e_end_c=e_end_c, m_new=m_new, a_dec=a_dec, c_inj=c_inj)


def _mlstm_fwd(proj, gates_t, bias_row, bias_col, head_g):
    s = proj.shape[0]
    nc = s // CHUNK
    L = CHUNK

    def body(q_ref, k_ref, v_ref, o_ref, gcol_ref, grow_ref, brow_ref, bcol_ref, hg_ref,
             out_ref, cprev_ref, nprev_ref, mprev_ref, c_scr, n_scr, m_scr):
        @pl.when(pl.program_id(0) == 0)
        def _():
            c_scr[...] = jnp.zeros_like(c_scr)
            n_scr[...] = jnp.zeros_like(n_scr)
            m_scr[...] = jnp.zeros_like(m_scr)

        for h in range(HEADS):
            sl = slice(h * HEAD_DIM, (h + 1) * HEAD_DIM)
            m_prev = m_scr[h:h + 1, 0:1]
            n_prev = n_scr[h:h + 1, :]
            c_prev = c_scr[h].astype(BF16)
            q = _mlstm_chunk_common(h, q_ref, k_ref, v_ref, gcol_ref, grow_ref, brow_ref, bcol_ref, m_prev)
            num = _dot(q["wqk"], q["vh"], NN) + q["e_inter"] * _dot(q["qs"], c_prev, NN)
            den = (jnp.sum(q["wqk"], axis=1, keepdims=True)
                   + q["e_inter"] * jnp.sum(q["qs"] * n_prev, axis=1, keepdims=True))
            hh = num / jnp.maximum(jnp.abs(den), jnp.exp(-q["m_t"]))
            hn = hh * lax.rsqrt(jnp.mean(hh * hh, axis=1, keepdims=True) + EPS) * hg_ref[h:h + 1, :]
            out_ref[:, sl] = (_sigmoid(o_ref[:, sl]) * hn).astype(BF16)
            cprev_ref[h] = c_prev
            nprev_ref[h:h + 1, :] = n_prev
            mprev_ref[h:h + 1, :] = jnp.broadcast_to(m_prev, (1, LANES))
            c_loc = _dot(q["kh"], q["e_end_c"] * q["vh"], TN)
            n_loc = jnp.sum(q["e_end_c"] * q["kh"], axis=0, keepdims=True)
            c_scr[h] = q["a_dec"] * c_scr[h] + q["c_inj"] * c_loc
            n_scr[h:h + 1, :] = q["a_dec"] * n_prev + q["c_inj"] * n_loc
            m_scr[h:h + 1, :] = jnp.broadcast_to(q["m_new"], (1, LANES))

    blk = lambda j: pl.BlockSpec((L, D_MLSTM), lambda c, j=j: (c, j))
    full = lambda shp: pl.BlockSpec(shp, lambda c: tuple(0 for _ in shp))
    return pl.pallas_call(
        body, grid=(nc,), name="mlstm_fwd",
        in_specs=[blk(0), blk(1), blk(2), blk(3),
                  pl.BlockSpec((L, LANES), lambda c: (c, (4 * D_MLSTM + 2 * D_LRU) // LANES)),
                  pl.BlockSpec((None, 2 * HEADS, L), lambda c: (c, 0, 0)),
                  full((1, LANES)), full((2 * HEADS, 1)), full((HEADS, HEAD_DIM))],
        out_specs=[pl.BlockSpec((L, D_MLSTM), lambda c: (c, 0)),
                   pl.BlockSpec((None, HEADS, HEAD_DIM, HEAD_DIM), lambda c: (c, 0, 0, 0)),
                   pl.BlockSpec((None, HEADS, HEAD_DIM), lambda c: (c, 0, 0)),
                   pl.BlockSpec((None, HEADS, LANES), lambda c: (c, 0, 0))],
        out_shape=[jax.ShapeDtypeStruct((s, D_MLSTM + D_LRU), BF16),
                   jax.ShapeDtypeStruct((nc, HEADS, HEAD_DIM, HEAD_DIM), BF16),
                   jax.ShapeDtypeStruct((nc, HEADS, HEAD_DIM), F32),
                   jax.ShapeDtypeStruct((nc, HEADS, LANES), F32)],
        scratch_shapes=[pltpu.VMEM((HEADS, HEAD_DIM, HEAD_DIM), F32), pltpu.VMEM((HEADS, HEAD_DIM), F32),
                        pltpu.VMEM((HEADS, LANES), F32)],
        compiler_params=_params(("arbitrary",)),
    )(proj, proj, proj, proj, proj, gates_t, bias_row, bias_col, head_g)


def _mlstm_bwd(proj, gates_t, bias_row, bias_col, head_g, cprev, nprev, mprev, dmix):
    s = proj.shape[0]
    nc = s // CHUNK
    L = CHUNK

    def body(q_ref, k_ref, v_ref, o_ref, gcol_ref, grow_ref, brow_ref, bcol_ref, hg_ref,
             cprev_ref, nprev_ref, mprev_ref, dmix_ref,
             dqkvo_ref, dgate_ref, dhg_ref, g_scr, gn_scr):
        @pl.when(pl.program_id(0) == 0)
        def _():
            g_scr[...] = jnp.zeros_like(g_scr)
            gn_scr[...] = jnp.zeros_like(gn_scr)
            dhg_ref[...] = jnp.zeros_like(dhg_ref)

        lane = _cols((L, LANES))
        dgate = jnp.zeros((L, LANES), F32)
        for h in range(HEADS):
            sl = slice(h * HEAD_DIM, (h + 1) * HEAD_DIM)
            m_prev = mprev_ref[h:h + 1, 0:1]
            n_prev = nprev_ref[h:h + 1, :]
            c_prev = cprev_ref[h]
            q = _mlstm_chunk_common(h, q_ref, k_ref, v_ref, gcol_ref, grow_ref, brow_ref, bcol_ref, m_prev)
            qh, kh, vh, qs, wqk, e_inter = q["qh"], q["kh"], q["vh"], q["qs"], q["wqk"], q["e_inter"]
            num_state = e_inter * _dot(qs, c_prev, NN)
            den_state = e_inter * jnp.sum(qs * n_prev, axis=1, keepdims=True)
            num = _dot(wqk, vh, NN) + num_state
            den = jnp.sum(wqk, axis=1, keepdims=True) + den_state
            floor = jnp.exp(-q["m_t"])
            denom = jnp.maximum(jnp.abs(den), floor)
            hh = num / denom
            rn = lax.rsqrt(jnp.mean(hh * hh, axis=1, keepdims=True) + EPS)
            hn_pre = hh * rn
            hg = hg_ref[h:h + 1, :]
            sg = _sigmoid(o_ref[:, sl])
            dout = dmix_ref[:, sl]
            d_o = dout * (hn_pre * hg) * sg * (1.0 - sg)
            dhn = dout * sg
            dhg_ref[h:h + 1, :] += jnp.sum(dhn * hn_pre, axis=0, keepdims=True)
            dhn_pre = dhn * hg
            dhh = rn * (dhn_pre - hn_pre * jnp.mean(dhn_pre * hn_pre, axis=1, keepdims=True))
            dnum = dhh / denom
            dden = jnp.where(jnp.abs(den) >= floor,
                             -jnp.sum(hh * dhh, axis=1, keepdims=True) / denom * jnp.sign(den), 0.0)
            dwqk = _dot(dnum, vh, NT) + dden
            dv = _dot(wqk, dnum, TN)
            dp = dwqk * q["e_mat"]
            dqs = _dot(dp, kh, NN) + e_inter * (_dot(dnum, c_prev, NT) + dden * n_prev)
            dk = _dot(dp, qs, TN)
            g_next = g_scr[h]
            gn_next = gn_scr[h:h + 1, :]
            w_state = q["e_end_c"] * q["c_inj"]
            dk_state = w_state * (_dot(vh, g_next, NT) + gn_next)
            dk = dk + dk_state
            dv = dv + w_state * _dot(kh, g_next, NN)
            dq = dqs * (HEAD_DIM ** -0.5)
            eye = q["t_i"] == q["s_i"]
            to_row = lambda col: jnp.sum(jnp.where(eye, col, 0.0), axis=0, keepdims=True)
            to_col = lambda row: jnp.sum(jnp.where(eye, row, 0.0), axis=1, keepdims=True)
            g_pair = dwqk * wqk
            rs_in = jnp.sum(g_pair, axis=1, keepdims=True)
            cs_in_r = jnp.sum(g_pair, axis=0, keepdims=True)
            rs_state = (jnp.sum(dnum * num_state, axis=1, keepdims=True) + dden * den_state)
            cs_state = jnp.sum(kh * dk_state, axis=1, keepdims=True)
            di_c = to_col(cs_in_r) + cs_state
            through = q["a_dec"] * (jnp.sum(jnp.sum(g_next * c_prev.astype(F32), axis=1, keepdims=True),
                                            axis=0, keepdims=True)
                                    + jnp.sum(gn_next * n_prev, axis=1, keepdims=True))
            ends_here = to_row(rs_in + rs_state) - cs_in_r
            da_c = (jnp.sum(jnp.where(q["s_i"] >= q["t_i"], ends_here, 0.0), axis=1, keepdims=True)
                    + jnp.sum(jnp.where(q["s_i"] < q["t_i"], to_row(cs_state), 0.0), axis=1, keepdims=True)
                    + through)
            df_c = da_c * _sigmoid(-q["fc"])
            dgate = dgate + jnp.where(lane == h, di_c, 0.0) + jnp.where(lane == HEADS + h, df_c, 0.0)
            dqkvo_ref[:, sl] = dq.astype(BF16)
            dqkvo_ref[:, D_MLSTM + h * HEAD_DIM:D_MLSTM + (h + 1) * HEAD_DIM] = dk.astype(BF16)
            dqkvo_ref[:, 2 * D_MLSTM + h * HEAD_DIM:2 * D_MLSTM + (h + 1) * HEAD_DIM] = dv.astype(BF16)
            dqkvo_ref[:, 3 * D_MLSTM + h * HEAD_DIM:3 * D_MLSTM + (h + 1) * HEAD_DIM] = d_o.astype(BF16)
            g_scr[h] = q["a_dec"] * g_next + _dot(e_inter * qs, dnum, TN)
            gn_scr[h:h + 1, :] = q["a_dec"] * gn_next + jnp.sum(e_inter * qs * dden, axis=0, keepdims=True)
        dgate_ref[...] = dgate

    rev = lambda c: nc - 1 - c
    blk = lambda j: pl.BlockSpec((L, D_MLSTM), lambda c, j=j: (rev(c), j))
    full = lambda shp: pl.BlockSpec(shp, lambda c: tuple(0 for _ in shp))
    return pl.pallas_call(
        body, grid=(nc,), name="mlstm_bwd",
        in_specs=[blk(0), blk(1), blk(2), blk(3),
                  pl.BlockSpec((L, LANES), lambda c: (rev(c), (4 * D_MLSTM + 2 * D_LRU) // LANES)),
                  pl.BlockSpec((None, 2 * HEADS, L), lambda c: (rev(c), 0, 0)),
                  full((1, LANES)), full((2 * HEADS, 1)), full((HEADS, HEAD_DIM)),
                  pl.BlockSpec((None, HEADS, HEAD_DIM, HEAD_DIM), lambda c: (rev(c), 0, 0, 0)),
                  pl.BlockSpec((None, HEADS, HEAD_DIM), lambda c: (rev(c), 0, 0)),
                  pl.BlockSpec((None, HEADS, LANES), lambda c: (rev(c), 0, 0)),
                  pl.BlockSpec((L, D_MLSTM), lambda c: (rev(c), 0))],
        out_specs=[pl.BlockSpec((L, 4 * D_MLSTM), lambda c: (rev(c), 0)),
                   pl.BlockSpec((L, LANES), lambda c: (rev(c), 0)),
                   full((HEADS, HEAD_DIM))],
        out_shape=[jax.ShapeDtypeStruct((s, _PROJ_PAD), BF16),
                   jax.ShapeDtypeStruct((s, LANES), F32),
                   jax.ShapeDtypeStruct((HEADS, HEAD_DIM), F32)],
        scratch_shapes=[pltpu.VMEM((HEADS, HEAD_DIM, HEAD_DIM), F32), pltpu.VMEM((HEADS, HEAD_DIM), F32)],
        compiler_params=_params(("arbitrary",)),
    )(proj, proj, proj, proj, proj, gates_t, bias_row, bias_col, head_g, cprev, nprev, mprev, dmix)


def _lru_gates(xc, wa_ref, wx_ref, ba, bx, lam):
    r = _sigmoid(_dot(xc, wa_ref[...], NN) + ba)
    ig = _sigmoid(_dot(xc, wx_ref[...], NN) + bx)
    sp = _softplus(-lam)
    log_a = (-LRU_C * r) * sp
    a = jnp.exp(log_a)
    mult = jnp.sqrt(_one_minus_exp(2.0 * log_a))
    return r, ig, sp, a, mult


def _lru_conv(xr, prev, w_ref, b):
    xc = b + _shift_down(xr, prev, 3) * w_ref[0:1, :]
    for j in range(1, LRU_CONV):
        xc = xc + _shift_down(xr, prev, LRU_CONV - 1 - j) * w_ref[j:j + 1, :]
    return xc


def _lru_fwd(proj, mix, conv_w, conv_b, wa, wx, ba, bx, lam, tt=1024):
    s = proj.shape[0]
    tt = min(tt, s)
    nt = s // tt
    B = LRU_BLOCK_DIM
    lru_col = 4 * D_MLSTM // B
    mix_col = D_MLSTM // B

    def body(xr_ref, gr_ref, cw_ref, cb_ref, wa_ref, wx_ref, ba_ref, bx_ref, lam_ref, mix_in_ref,
             out_ref, h_ref, prev_scr, hcar_scr):
        @pl.when(pl.program_id(1) == 0)
        def _():
            prev_scr[...] = jnp.zeros_like(prev_scr)
            hcar_scr[...] = jnp.zeros_like(hcar_scr)

        xr = xr_ref[...]
        xc = _lru_conv(xr, prev_scr[...], cw_ref, cb_ref[...])
        prev_scr[...] = xr[tt - HALO:, :]
        _, ig, _, a, mult = _lru_gates(xc, wa_ref, wx_ref, ba_ref[...], bx_ref[...], lam_ref[...])
        u = mult * (ig * xc)
        rows = _rows((tt, B))
        acc_a, acc_b = a, u
        d = 1
        while d < tt:
            keep = rows >= d
            sh_a = jnp.where(keep, pltpu.roll(acc_a, d, axis=0), 1.0)
            sh_b = jnp.where(keep, pltpu.roll(acc_b, d, axis=0), 0.0)
            acc_b = acc_a * sh_b + acc_b
            acc_a = acc_a * sh_a
            d *= 2
        hv = acc_b + acc_a * hcar_scr[0:1, :]
        hcar_scr[...] = jnp.broadcast_to(hv[tt - 1:tt, :], hcar_scr.shape)
        h_ref[...] = hv
        out_ref[...] = (hv * _gelu(gr_ref[...])).astype(BF16)

    chan = lambda rws: pl.BlockSpec((rws, B), lambda n, i: (0, n))
    return pl.pallas_call(
        body, grid=(LRU_BLOCKS, nt), name="lru_fwd",
        in_specs=[pl.BlockSpec((tt, B), lambda n, i: (i, lru_col + 2 * n)),
                  pl.BlockSpec((tt, B), lambda n, i: (i, lru_col + 2 * n + 1)),
                  chan(LRU_CONV), chan(1),
                  pl.BlockSpec((None, B, B), lambda n, i: (n, 0, 0)),
                  pl.BlockSpec((None, B, B), lambda n, i: (n, 0, 0)),
                  chan(1), chan(1), chan(1), ANY],
        out_specs=[pl.BlockSpec((tt, B), lambda n, i: (i, mix_col + n)), pl.BlockSpec((tt, B), lambda n, i: (i, n))],
        out_shape=[jax.ShapeDtypeStruct(mix.shape, BF16), jax.ShapeDtypeStruct((s, D_LRU), F32)],
        scratch_shapes=[pltpu.VMEM((HALO, B), F32), pltpu.VMEM((HALO, B), F32)],
        input_output_aliases={9: 0},
        compiler_params=_params(("parallel", "arbitrary")),
    )(proj, proj, conv_w, conv_b, wa, wx, ba, bx, lam, mix)


def _lru_bwd(proj, hsave, dmix, dproj, conv_w, conv_b, wa, wx, ba, bx, lam, tt=1024):
    s = proj.shape[0]
    tt = min(tt, s)
    nt = s // tt
    B = LRU_BLOCK_DIM
    lru_col = 4 * D_MLSTM // B
    dmix_col = D_MLSTM // B
    hpb = tt // HALO

    def body(xr_ref, xprev_ref, gr_ref, h_ref, hprev_ref, dmix_ref, cw_ref, cb_ref, wa_ref, wx_ref,
             ba_ref, bx_ref, lam_ref, dproj_in_ref,
             dxg_ref, dcw_ref, dcb_ref, dwa_ref, dwx_ref, dba_ref, dbx_ref, dlam_ref,
             gcar_scr, acar_scr, dxc_scr):
        i = pl.program_id(1)
        first_tile = i == nt - 1

        @pl.when(i == 0)
        def _():
            gcar_scr[...] = jnp.zeros_like(gcar_scr)
            acar_scr[...] = jnp.zeros_like(acar_scr)
            dxc_scr[...] = jnp.zeros_like(dxc_scr)
            for ref in (dcw_ref, dcb_ref, dwa_ref, dwx_ref, dba_ref, dbx_ref, dlam_ref):
                ref[...] = jnp.zeros_like(ref)

        xr = xr_ref[...]
        xprev = jnp.where(first_tile, 0.0, xprev_ref[...])
        hprev = jnp.where(first_tile, 0.0, hprev_ref[...])
        lam = lam_ref[...]
        taps = [_shift_down(xr, xprev, LRU_CONV - 1 - j) for j in range(LRU_CONV)]
        xc = cb_ref[...] + taps[0] * cw_ref[0:1, :]
        for j in range(1, LRU_CONV):
            xc = xc + taps[j] * cw_ref[j:j + 1, :]
        r, ig, sp, a, mult = _lru_gates(xc, wa_ref, wx_ref, ba_ref[...], bx_ref[...], lam)
        gr = gr_ref[...]
        hv = h_ref[...]
        dout = dmix_ref[...]
        dxg_ref[:, B:] = (dout * hv * _gelu_grad(gr)).astype(BF16)
        dh = dout * _gelu(gr)
        rows = _rows((tt, B))
        acc_a = _shift_up(a, acar_scr[...], 1)
        acc_b = dh
        d = 1
        while d < tt:
            keep = rows < tt - d
            sh_a = jnp.where(keep, pltpu.roll(acc_a, tt - d, axis=0), 1.0)
            sh_b = jnp.where(keep, pltpu.roll(acc_b, tt - d, axis=0), 0.0)
            acc_b = acc_a * sh_b + acc_b
            acc_a = acc_a * sh_a
            d *= 2
        gv = acc_b + acc_a * gcar_scr[0:1, :]
        gcar_scr[...] = jnp.broadcast_to(gv[0:1, :], gcar_scr.shape)
        acar_scr[...] = jnp.broadcast_to(a[0:1, :], acar_scr.shape)
        h_before = _shift_down(hv, hprev, 1)
        da = gv * h_before
        dmult = gv * (ig * xc)
        dig = gv * mult * xc
        dxc = gv * mult * ig
        dlog_a = da * a - dmult * (a * a) / mult
        dr = dlog_a * (-LRU_C * sp)
        dlam_ref[...] += jnp.sum(dlog_a * (-LRU_C * r), axis=0, keepdims=True) * (-_sigmoid(-lam))
        dpre_r = dr * r * (1.0 - r)
        dpre_i = dig * ig * (1.0 - ig)
        dba_ref[...] += jnp.sum(dpre_r, axis=0, keepdims=True)
        dbx_ref[...] += jnp.sum(dpre_i, axis=0, keepdims=True)
        dwa_ref[...] += _dot(xc, dpre_r, TN)
        dwx_ref[...] += _dot(xc, dpre_i, TN)
        dxc = dxc + _dot(dpre_r, wa_ref[...], NT) + _dot(dpre_i, wx_ref[...], NT)
        dcb_ref[...] += jnp.sum(dxc, axis=0, keepdims=True)
        nxt = dxc_scr[...]
        dxr = jnp.zeros((tt, B), F32)
        for j in range(LRU_CONV):
            sft = LRU_CONV - 1 - j
            dcw_ref[j:j + 1, :] += jnp.sum(dxc * taps[j], axis=0, keepdims=True)
            dxr = dxr + _shift_up(dxc, nxt, sft) * cw_ref[j:j + 1, :]
        dxc_scr[...] = dxc[:HALO, :]
        dxg_ref[:, :B] = dxr.astype(BF16)

    rev = lambda i: nt - 1 - i
    tile = lambda col, step: pl.BlockSpec((tt, B), lambda n, i: (rev(i), col + step * n))
    halo = lambda col, step: pl.BlockSpec(
        (HALO, B), lambda n, i: (jnp.maximum(rev(i) * hpb - 1, 0), col + step * n))
    chan = lambda rws: pl.BlockSpec((rws, B), lambda n, i: (0, n))
    wblk = pl.BlockSpec((None, B, B), lambda n, i: (n, 0, 0))
    return pl.pallas_call(
        body, grid=(LRU_BLOCKS, nt), name="lru_bwd",
        in_specs=[tile(lru_col, 2), halo(lru_col, 2), tile(lru_col + 1, 2), tile(0, 1), halo(0, 1),
                  tile(dmix_col, 1), chan(LRU_CONV), chan(1), wblk, wblk, chan(1), chan(1), chan(1), ANY],
        out_specs=[pl.BlockSpec((tt, 2 * B), lambda n, i: (rev(i), lru_col // 2 + n)),
                   chan(LRU_CONV), chan(1), wblk, wblk, chan(1), chan(1), chan(1)],
        out_shape=[jax.ShapeDtypeStruct(dproj.shape, BF16),
                   jax.ShapeDtypeStruct((LRU_CONV, D_LRU), F32), jax.ShapeDtypeStruct((1, D_LRU), F32),
                   jax.ShapeDtypeStruct((LRU_BLOCKS, B, B), F32), jax.ShapeDtypeStruct((LRU_BLOCKS, B, B), F32),
                   jax.ShapeDtypeStruct((1, D_LRU), F32), jax.ShapeDtypeStruct((1, D_LRU), F32),
                   jax.ShapeDtypeStruct((1, D_LRU), F32)],
        scratch_shapes=[pltpu.VMEM((HALO, B), F32), pltpu.VMEM((HALO, B), F32), pltpu.VMEM((HALO, B), F32)],
        input_output_aliases={13: 0},
        compiler_params=_params(("parallel", "arbitrary")),
    )(proj, proj, proj, hsave, hsave, dmix, conv_w, conv_b, wa, wx, ba, bx, lam, dproj)


def _ffn_conv(gp, prev, w_ref, b):
    g = b + _shift_down(gp, prev, 2) * w_ref[0:1, :]
    for j in range(1, FFN_CONV):
        g = g + _shift_down(gp, prev, FFN_CONV - 1 - j) * w_ref[j:j + 1, :]
    return g


def _ffn_act_fwd(gu, conv_w, conv_b, tt=256):
    s = gu.shape[0]
    tt = min(tt, s)
    d_ff = conv_w.shape[1]
    tc = d_ff // N_CHIPS
    hpb = tt // HALO

    def body(g_ref, gprev_ref, u_ref, w_ref, b_ref, act_ref):
        prev = jnp.where(pl.program_id(0) == 0, 0.0, gprev_ref[...])
        gate = _ffn_conv(g_ref[...], prev, w_ref, b_ref[...])
        act_ref[...] = (gate * _sigmoid(gate) * u_ref[...]).astype(BF16)

    return pl.pallas_call(
        body, grid=(s // tt, N_CHIPS), name="ffn_act_fwd",
        in_specs=[pl.BlockSpec((tt, tc), lambda i, j: (i, 2 * j)),
                  pl.BlockSpec((HALO, tc), lambda i, j: (jnp.maximum(i * hpb - 1, 0), 2 * j)),
                  pl.BlockSpec((tt, tc), lambda i, j: (i, 2 * j + 1)),
                  pl.BlockSpec((FFN_CONV, tc), lambda i, j: (0, j)),
                  pl.BlockSpec((1, tc), lambda i, j: (0, j))],
        out_specs=pl.BlockSpec((tt, tc), lambda i, j: (i, j)),
        out_shape=jax.ShapeDtypeStruct((s, d_ff), BF16),
        compiler_params=_params(("parallel", "parallel")),
    )(gu, gu, gu, conv_w, conv_b)


def _ffn_act_bwd(gu, dact, conv_w, conv_b, tt=256):
    s = gu.shape[0]
    tt = min(tt, s)
    nt = s // tt
    d_ff = conv_w.shape[1]
    tc = d_ff // N_CHIPS
    hpb = tt // HALO

    def dgate_of(gate, up, da):
        sg = _sigmoid(gate)
        return da * up * (sg * (1.0 + gate * (1.0 - sg))), da * (gate * sg)

    def body(g_ref, gprev_ref, gnext_ref, u_ref, unext_ref, da_ref, danext_ref, w_ref, b_ref,
             dgu_ref, dw_ref, db_ref):
        i = pl.program_id(1)

        @pl.when(i == 0)
        def _():
            dw_ref[...] = jnp.zeros_like(dw_ref)
            db_ref[...] = jnp.zeros_like(db_ref)

        gp = g_ref[...]
        prev = jnp.where(i == 0, 0.0, gprev_ref[...])
        bias = b_ref[...]
        taps = [_shift_down(gp, prev, FFN_CONV - 1 - j) for j in range(FFN_CONV)]
        gate = bias + taps[0] * w_ref[0:1, :]
        for j in range(1, FFN_CONV):
            gate = gate + taps[j] * w_ref[j:j + 1, :]
        dgate, dup = dgate_of(gate, u_ref[...], da_ref[...])
        gate_n = _ffn_conv(gnext_ref[...], gp[tt - HALO:, :], w_ref, bias)
        dgate_n, _ = dgate_of(gate_n, unext_ref[...], danext_ref[...])
        dgate_n = jnp.where(i == nt - 1, 0.0, dgate_n)
        db_ref[...] += jnp.sum(dgate, axis=0, keepdims=True)
        dgp = jnp.zeros((tt, tc), F32)
        for j in range(FFN_CONV):
            dw_ref[j:j + 1, :] += jnp.sum(dgate * taps[j], axis=0, keepdims=True)
            dgp = dgp + _shift_up(dgate, dgate_n, FFN_CONV - 1 - j) * w_ref[j:j + 1, :]
        dgu_ref[:, :tc] = dgp.astype(BF16)
        dgu_ref[:, tc:] = dup.astype(BF16)

    tile = lambda half: pl.BlockSpec((tt, tc), lambda j, i, half=half: (i, 2 * j + half))
    hprev = lambda half: pl.BlockSpec((HALO, tc), lambda j, i, half=half: (jnp.maximum(i * hpb - 1, 0), 2 * j + half))
    hnext = lambda half: pl.BlockSpec(
        (HALO, tc), lambda j, i, half=half: (jnp.minimum((i + 1) * hpb, nt * hpb - 1), 2 * j + half))
    return pl.pallas_call(
        body, grid=(N_CHIPS, nt), name="ffn_act_bwd",
        in_specs=[tile(0), hprev(0), hnext(0), tile(1), hnext(1),
                  pl.BlockSpec((tt, tc), lambda j, i: (i, j)),
                  pl.BlockSpec((HALO, tc), lambda j, i: (jnp.minimum((i + 1) * hpb, nt * hpb - 1), j)),
                  pl.BlockSpec((FFN_CONV, tc), lambda j, i: (0, j)),
                  pl.BlockSpec((1, tc), lambda j, i: (0, j))],
        out_specs=[pl.BlockSpec((tt, 2 * tc), lambda j, i: (i, j)),
                   pl.BlockSpec((FFN_CONV, tc), lambda j, i: (0, j)),
                   pl.BlockSpec((1, tc), lambda j, i: (0, j))],
        out_shape=[jax.ShapeDtypeStruct((s, 2 * d_ff), BF16),
                   jax.ShapeDtypeStruct((FFN_CONV, d_ff), F32), jax.ShapeDtypeStruct((1, d_ff), F32)],
        compiler_params=_params(("parallel", "arbitrary")),
    )(gu, gu, gu, gu, gu, dact, dact, conv_w, conv_b)


def _gate_grads(dgate, dproj, tm=512):
    s, n = dgate.shape
    tm = min(tm, s)

    def body(a_ref, dproj_in_ref, o_ref, dproj_ref):
        @pl.when(pl.program_id(0) == 0)
        def _():
            o_ref[...] = jnp.zeros_like(o_ref)
        a = a_ref[...]
        o_ref[...] += jnp.sum(a, axis=0, keepdims=True)
        dproj_ref[...] = a.astype(BF16)

    return pl.pallas_call(
        body, grid=(s // tm,), name="gate_grads",
        in_specs=[pl.BlockSpec((tm, n), lambda i: (i, 0)), ANY],
        out_specs=[pl.BlockSpec((1, n), lambda i: (0, 0)),
                   pl.BlockSpec((tm, n), lambda i: (i, (_QKVO + 2 * D_LRU) // LANES))],
        out_shape=[jax.ShapeDtypeStruct((1, n), F32), jax.ShapeDtypeStruct(dproj.shape, BF16)],
        input_output_aliases={1: 1},
        compiler_params=_params(("arbitrary",)),
    )(dgate, dproj)


def _pick(n, *cands):
    for c in cands:
        if n % c == 0:
            return c
    raise ValueError(f"no tile for {n}")


def _behind(a, token):
    return a if token is None else a + token[0:1, 0:1].astype(a.dtype).reshape((1,) * a.ndim)


class _Gathered:
    def __init__(self, w):
        self.w = w

    def begin(self):
        return None

    def mid(self, grp, after):
        return None

    def end(self, grp, after):
        return self.w

    def reduce_early(self, grads):
        return None

    def reduce_early_mid(self, after):
        return None

    def reduce_late(self, grads):
        return None

    def reduce_late_mid(self, after):
        return None


def _local_step(x, target, w, comm):
    s, d = x.shape
    nc = s // CHUNK
    tm = _pick(s, 1024, 512, 256)
    tn_proj = _pick(_PROJ_PAD, 896)
    gate_col = 4 * D_MLSTM + 2 * D_LRU
    w = dict(w)

    token = comm.begin()
    n1, rstd1 = _rmsnorm_fwd("norm_mix_fwd", x, _behind(w["norm_mix_g"], token))
    comm.mid(0, n1)
    w.update(comm.end(0, None))
    proj = _mm_nn("proj_fwd", n1, w["w_in"], tm, tn_proj, d)
    token = comm.mid(1, proj)
    gates = proj[:, gate_col:gate_col + 2 * HEADS]
    gates_t = gates.reshape(nc, CHUNK, 2 * HEADS).transpose(0, 2, 1)
    bias_row = _behind(jnp.pad(w["b_gate_m"], ((0, 0), (0, LANES - 2 * HEADS))), token)
    bias_col = w["b_gate_m"].reshape(2 * HEADS, 1)
    mix, cprev, nprev, mprev = _mlstm_fwd(proj, gates_t, bias_row, bias_col, w["mlstm_norm_g"])
    mix, hsave = _lru_fwd(proj, mix, w["lru_conv_w"], w["lru_conv_b"], w["lru_wa"], w["lru_wx"],
                          w["lru_ba"], w["lru_bx"], w["lru_lambda"])
    w.update(comm.end(1, hsave))
    token = comm.mid(2, hsave)
    x1 = _mm_nn("out_fwd", mix, w["w_out"], tm, 1024, d, res=x)
    n2, rstd2 = _rmsnorm_fwd("norm_ffn_fwd", x1, _behind(w["norm_ffn_g"], token))
    w.update(comm.end(2, n2))
    token = comm.mid(3, n2)
    gu = _mm_up_fwd("up_fwd", n2, w["w_up"], tm, d)
    act = _ffn_act_fwd(gu, w["ffn_conv_w"], _behind(w["ffn_conv_b"], token))
    w.update(comm.end(3, act))
    d_ff = w["w_down"].shape[0]
    x2 = _mm_nn("down_fwd", act, w["w_down"], min(tm, 512), 1024, d_ff // 2, res=x1)
    loss, dx2, dx2b, g_norm_final = _loss_head("loss_head", x2, w["norm_final_g"], target)

    grads = {"norm_final_g": g_norm_final}
    dact = _mm_nt("down_bwd_x", dx2b, w["w_down"], tm, d_ff // N_CHIPS, d)
    grads["w_down"] = _mm_tn("down_bwd_w", act, dx2b, d_ff // N_CHIPS, 1024, 2048)
    dgu, grads["ffn_conv_w"], grads["ffn_conv_b"] = _ffn_act_bwd(gu, dact, w["ffn_conv_w"], w["ffn_conv_b"])
    dn2 = _mm_up_bwd_x("up_bwd_x", dgu, w["w_up"], tm, 1024)
    grads["w_up"] = _mm_up_bwd_w("up_bwd_w", n2, dgu, 1024, 2048)
    dx1, dx1b, grads["norm_ffn_g"] = _rmsnorm_bwd("norm_ffn_bwd", x1, rstd2, w["norm_ffn_g"], dn2, dx2)
    dmix = _mm_nt("out_bwd_x", dx1b, w["w_out"], tm, 1024, d)
    grads["w_out"] = _mm_tn("out_bwd_w", mix, dx1b, 1024, 1024, 2048)
    token = comm.reduce_early(grads)
    dproj, dgate, grads["mlstm_norm_g"] = _mlstm_bwd(proj, gates_t, _behind(bias_row, token), bias_col,
                                                     w["mlstm_norm_g"], cprev, nprev, mprev, dmix)
    token = comm.reduce_early_mid(dproj)
    (dproj, grads["lru_conv_w"], grads["lru_conv_b"], grads["lru_wa"], grads["lru_wx"],
     grads["lru_ba"], grads["lru_bx"], grads["lru_lambda"]) = _lru_bwd(
        proj, hsave, dmix, dproj, w["lru_conv_w"], _behind(w["lru_conv_b"], token), w["lru_wa"], w["lru_wx"],
        w["lru_ba"], w["lru_bx"], w["lru_lambda"])
    gate_bias_grad, dproj = _gate_grads(dgate, dproj)
    grads["b_gate_m"] = gate_bias_grad[:, :2 * HEADS]
    grads["w_in"] = _mm_tn("proj_bwd_w", n1, dproj, 1024, tn_proj, 2048)
    token = comm.reduce_late(grads)
    dn1 = _mm_nt("proj_bwd_x", dproj, w["w_in"], tm, 512, _PROJ_PAD, after=token)
    token = comm.reduce_late_mid(dn1)
    grad_x, _, grads["norm_mix_g"] = _rmsnorm_bwd("norm_mix_bwd", x, rstd1, _behind(w["norm_mix_g"], token),
                                                  dn1, dx1)
    return loss, grad_x, grads


WEIGHT_NAMES = ("norm_mix_g", "w_in", "b_gate_m", "mlstm_norm_g", "lru_conv_w", "lru_conv_b", "lru_wa", "lru_ba",
                "lru_wx", "lru_bx", "lru_lambda", "w_out", "norm_ffn_g", "w_up", "ffn_conv_w", "ffn_conv_b",
                "w_down", "norm_final_g")
BIG = ("w_in", "w_out", "w_up", "w_down")
SMALL_SHARDED = ("mlstm_norm_g", "lru_conv_w", "ffn_conv_w")
SMALL = tuple(n for n in WEIGHT_NAMES if n not in BIG)
SMALL_REPLICATED = tuple(n for n in SMALL if n not in SMALL_SHARDED)


def _proj_segments():
    segs = [(0, 0, _QKVO), (_QKVO, _QKVO + 2 * D_LRU, _N_GATES)]
    for n in range(LRU_BLOCKS):
        segs.append((_QKVO + _N_GATES + n * LRU_BLOCK_DIM, _QKVO + 2 * n * LRU_BLOCK_DIM, LRU_BLOCK_DIM))
        segs.append((_QKVO + _N_GATES + D_LRU + n * LRU_BLOCK_DIM, _QKVO + (2 * n + 1) * LRU_BLOCK_DIM,
                     LRU_BLOCK_DIM))
    return segs


def _w_in_shards_to_local(shards):
    width = shards.shape[2]
    pieces = []
    for g0, _, n in sorted(_proj_segments(), key=lambda s: s[1]):
        at = g0
        while at < g0 + n:
            j = at // width
            stop = min(g0 + n, (j + 1) * width)
            pieces.append(shards[j][:, at - j * width:stop - j * width])
            at = stop
    pieces.append(jnp.zeros((shards.shape[1], PROJ_GATE_PAD - _N_GATES), shards.dtype))
    return jnp.concatenate(pieces, axis=1)


def _w_in_local_to_shards(w):
    width = _PROJ_COLS // N_CHIPS
    shards = []
    for j in range(N_CHIPS):
        pieces = []
        for g0, l0, n in sorted(_proj_segments()):
            lo, hi = max(g0, j * width), min(g0 + n, (j + 1) * width)
            if lo < hi:
                pieces.append(w[:, l0 + lo - g0:l0 + hi - g0])
        shards.append(jnp.concatenate(pieces, axis=1))
    return jnp.stack(shards)


def _w_in_to_global(w):
    sh = _w_in_local_to_shards(w)
    return jnp.concatenate([sh[j] for j in range(N_CHIPS)], axis=1)


def _size(shp):
    return functools.reduce(lambda a, b: a * b, shp, 1)


def _lane_dense(shp):
    return len(shp) >= 2 and shp[-1] == LANES and _size(shp) % (HALO * LANES) == 0


def _pack_rows(shapes):
    loose = sum(_size(shp) for shp in shapes if not _lane_dense(shp))
    return sum(_size(shp) // LANES for shp in shapes if _lane_dense(shp)) + -(-loose // (HALO * LANES)) * HALO


def _pack(arrs, rows):
    del rows
    parts = [a.reshape(-1, LANES).astype(F32) for a in arrs if _lane_dense(a.shape)]
    loose = [a.reshape(-1).astype(F32) for a in arrs if not _lane_dense(a.shape)]
    if loose:
        flat = jnp.concatenate(loose)
        n = -(-flat.shape[0] // (HALO * LANES)) * HALO * LANES
        parts.append(jnp.pad(flat, (0, n - flat.shape[0])).reshape(-1, LANES))
    return parts[0] if len(parts) == 1 else jnp.concatenate(parts, axis=0)


def _unpack(buf, shapes):
    out, row = {}, 0
    for i, shp in enumerate(shapes):
        if _lane_dense(shp):
            n = _size(shp) // LANES
            out[i] = buf[row:row + n].reshape(shp)
            row += n
    flat, at = buf[row:].reshape(-1), 0
    for i, shp in enumerate(shapes):
        if not _lane_dense(shp):
            out[i] = flat[at:at + _size(shp)].reshape(shp)
            at += _size(shp)
    return [out[i] for i in range(len(shapes))]


def _assemble_weights(g_in, g_out, g_up, g_down, small_sharded, replicated):
    w = dict(replicated)
    w["w_in"] = _w_in_shards_to_local(g_in)
    w["w_out"] = g_out.reshape(-1, g_out.shape[-1])
    w["w_up"] = g_up
    w["w_down"] = g_down.reshape(-1, g_down.shape[-1])
    for name, v in small_sharded.items():
        w[name] = jnp.concatenate([v[j] for j in range(N_CHIPS)], axis=1)
    return w


def _full_weights_from_global(weights):
    shard = lambda a, axis: jnp.stack(jnp.split(a, N_CHIPS, axis=axis))
    rep = {n: weights[n].reshape(1, -1) if weights[n].ndim <= 2 and n != "b_gate_m" else weights[n]
           for n in SMALL_REPLICATED}
    rep["b_gate_m"] = weights["b_gate_m"].reshape(1, -1)
    return _assemble_weights(shard(weights["w_in"], 1).astype(BF16), shard(weights["w_out"], 0).astype(BF16),
                             shard(weights["w_up"], 1).astype(BF16), shard(weights["w_down"], 0).astype(BF16),
                             {n: shard(weights[n], 1) for n in SMALL_SHARDED}, rep)


def _grads_to_global(grads):
    g = dict(grads)
    g["w_in"] = _w_in_to_global(grads["w_in"])
    g["w_up"] = jnp.concatenate([grads["w_up"][j] for j in range(N_CHIPS)], axis=1)
    return g


def _place():
    x, y, c = lax.axis_index("x"), lax.axis_index("y"), lax.axis_index("c")
    chips = [(1 - x, y), (x, 1 - y), (1 - x, 1 - y)]
    return x, y, c, 2 * x + y, chips


def _half_rows(n_rows, which):
    half = n_rows // 2
    return pl.ds(pl.multiple_of(which * half, 16), half)


def _rcopy(src, dst, send_sem, recv_sem, to):
    return pltpu.make_async_remote_copy(src_ref=src, dst_ref=dst, send_sem=send_sem, recv_sem=recv_sem,
                                        device_id=to, device_id_type=MESH)


HBM_SPEC = pl.BlockSpec(memory_space=pltpu.HBM)
SEM_SPEC = pl.BlockSpec(memory_space=pltpu.SEMAPHORE)
TOKEN_SHAPE = (8, LANES)


def _split_call(name, bufs, sems_in, sems_out_shapes, body_fn, after=None):
    nb, ni, no = len(bufs), len(sems_in), len(sems_out_shapes)
    after = [] if after is None else list(after) if isinstance(after, (list, tuple)) else [after]

    def body(*refs):
        buf_refs = refs[:nb]
        sem_in_refs = refs[nb:nb + ni]
        outs = refs[nb + ni + len(after):]
        sem_out_refs = outs[:no]
        token_ref = outs[no + nb]
        body_fn(buf_refs, sem_in_refs, sem_out_refs)
        token_ref[...] = jnp.zeros_like(token_ref)

    out_shape = ([pltpu.SemaphoreType.DMA(shp) for shp in sems_out_shapes]
                 + [pltpu.HBM(b.shape, b.dtype) for b in bufs] + [jax.ShapeDtypeStruct(TOKEN_SHAPE, F32)])
    res = pl.pallas_call(
        body, name=name, out_shape=out_shape,
        in_specs=[HBM_SPEC] * nb + [SEM_SPEC] * ni + [ANY] * len(after),
        out_specs=[SEM_SPEC] * no + [HBM_SPEC] * nb + [pl.BlockSpec(memory_space=pltpu.VMEM)],
        input_output_aliases={i: no + i for i in range(nb)},
        compiler_params=pltpu.CompilerParams(has_side_effects=pltpu.SideEffectType.DATAFLOW_SIDE_EFFECTING),
    )(*[pltpu.with_memory_space_constraint(b, pltpu.HBM) for b in bufs], *sems_in, *after)
    return list(res[:no]), list(res[no:no + nb]), res[no + nb]


def _place_own_shard(name, idx, shard, after=None):
    rows, cols = shard.shape
    tr = _row_tile(rows)

    def body(idx_ref, s_ref, *rest):
        rest[-1][...] = s_ref[...].astype(BF16)

    return pl.pallas_call(
        body, name=name, out_shape=jax.ShapeDtypeStruct((N_CHIPS, rows, cols), BF16),
        grid_spec=pltpu.PrefetchScalarGridSpec(
            num_scalar_prefetch=1, grid=(rows // tr,),
            in_specs=[pl.BlockSpec((tr, cols), lambda i, s: (i, 0))] + ([] if after is None else [ANY]),
            out_specs=pl.BlockSpec((None, tr, cols), lambda i, s: (s[1], i, 0))),
        compiler_params=_params(("parallel",)),
    )(idx, shard, *(() if after is None else (after,)))


GATHER_GROUPS = ((0, 4), (1,), (2,), (3,))


def _gather_start(name, lands, groups, after=None):
    members = [w for g in groups for w in GATHER_GROUPS[g]]

    def starts(bufs, _, sems):
        x, y, c, me, chips = _place()
        for gi, g in enumerate(groups):
            for pos, w in enumerate(GATHER_GROUPS[g]):
                buf = bufs[members.index(w)]
                part = buf.at[me] if w == 4 else buf.at[me, _half_rows(buf.shape[1], c)]
                for k, chip in enumerate(chips):
                    _rcopy(part, part, sems[2 * gi].at[3 * pos + k], sems[2 * gi + 1].at[3 * pos + k],
                           (*chip, c)).start()

    shapes = []
    for g in groups:
        shapes += [(3 * len(GATHER_GROUPS[g]),)] * 2
    sems, bufs, token = _split_call(name, [lands[w] for w in members], [], shapes, starts, after=after)
    return ({g: (sems[2 * gi], sems[2 * gi + 1]) for gi, g in enumerate(groups)},
            dict(zip(members, bufs)), token)


def _gather_mid(grp, lands, sems, after):
    members = GATHER_GROUPS[grp]
    big = [w for w in members if w != 4]

    def mid(bufs, sems_in, sems_out):
        x, y, c, me, chips = _place()
        send_sems, recv_sems = sems_in
        for pos, w in enumerate(members):
            for k, chip in enumerate(chips):
                cid = 2 * chip[0] + chip[1]
                buf = bufs[pos]
                mine = buf.at[me] if w == 4 else buf.at[me, _half_rows(buf.shape[1], c)]
                theirs = buf.at[cid] if w == 4 else buf.at[cid, _half_rows(buf.shape[1], c)]
                arrival = _rcopy(mine, theirs, send_sems.at[3 * pos + k], recv_sems.at[3 * pos + k], (*chip, c))
                arrival.wait_recv()
                arrival.wait_send()
                if w != 4:
                    _rcopy(theirs, theirs, sems_out[0].at[3 * big.index(w) + k],
                           sems_out[1].at[3 * big.index(w) + k], (x, y, 1 - c)).start()

    new_sems, bufs, token = _split_call(f"gather_mid_{grp}", [lands[w] for w in members], list(sems),
                                        [(3 * len(big),), (3 * len(big),)], mid, after=after)
    return new_sems, bufs, token


def _gather_end(grp, bufs, sems, after):
    members = GATHER_GROUPS[grp]
    big = [w for w in members if w != 4]

    def end(refs, sems_in, _):
        x, y, c, me, chips = _place()
        send_sems, recv_sems = sems_in
        for pos, w in enumerate(members):
            if w == 4:
                continue
            for k, chip in enumerate(chips):
                cid = 2 * chip[0] + chip[1]
                buf = refs[pos]
                sent = buf.at[cid, _half_rows(buf.shape[1], c)]
                landed = buf.at[cid, _half_rows(buf.shape[1], 1 - c)]
                fwd = _rcopy(sent, landed, send_sems.at[3 * big.index(w) + k], recv_sems.at[3 * big.index(w) + k],
                             (x, y, 1 - c))
                fwd.wait_recv()
                fwd.wait_send()

    _, bufs, token = _split_call(f"gather_end_{grp}", bufs, list(sems), [], end, after=after)
    return bufs, token


def _pair_start(name, grads, extra=None):
    n = len(grads)
    bufs = list(grads) + [lax.empty((g.shape[0], g.shape[1] // 2, g.shape[2]), g.dtype) for g in grads]
    if extra is not None:
        bufs += [extra, lax.empty(extra.shape, extra.dtype)]

    def starts(refs, _, sems):
        x, y, c, _, _ = _place()
        for w in range(n):
            other = _half_rows(refs[w].shape[1], 1 - c)
            _rcopy(refs[w].at[:, other], refs[n + w], sems[0].at[w], sems[1].at[w], (x, y, 1 - c)).start()
        if extra is not None:
            _rcopy(refs[2 * n], refs[2 * n + 1], sems[0].at[n], sems[1].at[n], (x, y, 1 - c)).start()

    count = n + (extra is not None)
    return _split_call(name, bufs, [], [(count,), (count,)], starts)


def _pair_wait(name, n, bufs, sems, after):
    has_extra = len(bufs) > 2 * n

    def waits(refs, sems_in, _):
        x, y, c, _, _ = _place()
        for w in range(n):
            other = _half_rows(refs[w].shape[1], 1 - c)
            cp = _rcopy(refs[w].at[:, other], refs[n + w], sems_in[0].at[w], sems_in[1].at[w], (x, y, 1 - c))
            cp.wait_recv()
            cp.wait_send()
        if has_extra:
            cp = _rcopy(refs[2 * n], refs[2 * n + 1], sems_in[0].at[n], sems_in[1].at[n], (x, y, 1 - c))
            cp.wait_recv()
            cp.wait_send()

    _, bufs, token = _split_call(name, bufs, list(sems), [], waits, after=after)
    return bufs, token


def _chip_start(name, partials, small=None):
    n = len(partials)
    bufs = list(partials) + [lax.empty(p.shape, p.dtype) for p in partials] + ([] if small is None else [small])

    def starts(refs, _, sems):
        _, _, c, me, chips = _place()
        for w in range(n):
            for k, chip in enumerate(chips):
                cid = 2 * chip[0] + chip[1]
                _rcopy(refs[w].at[cid], refs[n + w].at[me], sems[0].at[3 * w + k], sems[1].at[3 * w + k],
                       (*chip, c)).start()
        if small is not None:
            for k, chip in enumerate(chips):
                _rcopy(refs[2 * n].at[me], refs[2 * n].at[me], sems[0].at[3 * n + k], sems[1].at[3 * n + k],
                       (*chip, c)).start()

    count = 3 * (n + (small is not None))
    return _split_call(name, bufs, [], [(count,), (count,)], starts)


def _chip_wait(name, n, bufs, sems, after):
    has_small = len(bufs) > 2 * n

    def waits(refs, sems_in, _):
        _, _, c, me, chips = _place()
        for w in range(n):
            for k, chip in enumerate(chips):
                cid = 2 * chip[0] + chip[1]
                cp = _rcopy(refs[w].at[cid], refs[n + w].at[cid], sems_in[0].at[3 * w + k],
                            sems_in[1].at[3 * w + k], (*chip, c))
                cp.wait_recv()
                cp.wait_send()
        if has_small:
            for k, chip in enumerate(chips):
                cid = 2 * chip[0] + chip[1]
                cp = _rcopy(refs[2 * n].at[me], refs[2 * n].at[cid], sems_in[0].at[3 * n + k],
                            sems_in[1].at[3 * n + k], (*chip, c))
                cp.wait_recv()
                cp.wait_send()

    _, bufs, token = _split_call(name, bufs, list(sems), [], waits, after=after)
    return bufs, token


def _small_pair_sum(idx, own, recv):
    rows = own.shape[0]

    def body(idx_ref, a_ref, b_ref, o_ref):
        o_ref[...] = a_ref[...] + b_ref[...]

    blk = pl.BlockSpec((rows, LANES), lambda i, s: (0, 0))
    return pl.pallas_call(
        body, name="small_pair_sum", out_shape=jax.ShapeDtypeStruct((N_CHIPS, rows, LANES), F32),
        grid_spec=pltpu.PrefetchScalarGridSpec(
            num_scalar_prefetch=1, grid=(1,), in_specs=[blk, blk],
            out_specs=pl.BlockSpec((None, rows, LANES), lambda i, s: (s[1], 0, 0))),
        compiler_params=_params(("arbitrary",)),
    )(idx, own, recv)


def _gather_weights(shards, small):
    nb = len(shards)

    def body(*refs):
        srcs, small_ref = refs[:nb], refs[nb]
        dsts, small_out = refs[nb + 1:2 * nb + 1], refs[2 * nb + 1]
        send_sems, recv_sems, local_sems = refs[2 * nb + 2:]
        x, y, c, me, chips = _place()
        sibling = (x, y, 1 - c)
        mine = [_half_rows(s.shape[0], c) for s in srcs]
        other = [_half_rows(s.shape[0], 1 - c) for s in srcs]

        local = [pltpu.make_async_copy(srcs[w], dsts[w].at[me], local_sems.at[w]) for w in range(nb)]
        local.append(pltpu.make_async_copy(small_ref, small_out.at[me], local_sems.at[nb]))
        for cp in local:
            cp.start()
        sends = []
        for w in range(nb):
            for k, chip in enumerate(chips):
                sends.append(_rcopy(srcs[w].at[mine[w]], dsts[w].at[me, mine[w]],
                                    send_sems.at[w, k], recv_sems.at[w, k], (*chip, c)))
        for k, chip in enumerate(chips):
            sends.append(_rcopy(small_ref, small_out.at[me], send_sems.at[nb, k], recv_sems.at[nb, k], (*chip, c)))
        for cp in sends:
            cp.start()
        passed = []
        for w in range(nb):
            for k, chip in enumerate(chips):
                cid = 2 * chip[0] + chip[1]
                landed = dsts[w].at[cid, mine[w]]
                _rcopy(landed, landed, send_sems.at[w, k], recv_sems.at[w, k], (*chip, c)).wait_recv()
                fwd = _rcopy(landed, landed, send_sems.at[w, 3 + k], recv_sems.at[w, 3 + k], sibling)
                fwd.start()
                passed.append(fwd)
        for k, chip in enumerate(chips):
            cid = 2 * chip[0] + chip[1]
            _rcopy(small_ref, small_out.at[cid], send_sems.at[nb, k], recv_sems.at[nb, k], (*chip, c)).wait_recv()
        for w in range(nb):
            for k, chip in enumerate(chips):
                cid = 2 * chip[0] + chip[1]
                landed = dsts[w].at[cid, other[w]]
                _rcopy(landed, landed, send_sems.at[w, 3 + k], recv_sems.at[w, 3 + k], sibling).wait_recv()
        for cp in sends + passed:
            cp.wait_send()
        for cp in local:
            cp.wait()

    out_shape = [jax.ShapeDtypeStruct((N_CHIPS,) + s.shape, s.dtype) for s in shards]
    out_shape.append(jax.ShapeDtypeStruct((N_CHIPS,) + small.shape, small.dtype))
    return pl.pallas_call(
        body, name="gather_weights", out_shape=out_shape,
        in_specs=[ANY] * (nb + 1), out_specs=[ANY] * (nb + 1),
        scratch_shapes=[pltpu.SemaphoreType.DMA((nb + 1, 6)), pltpu.SemaphoreType.DMA((nb + 1, 6)),
                        pltpu.SemaphoreType.DMA((nb + 1,))],
    )(*shards, small)


def _pair_exchange(grads, small):
    nb = len(grads)

    def body(*refs):
        srcs, small_ref = refs[:nb], refs[nb]
        dsts, small_out = refs[nb + 1:2 * nb + 1], refs[2 * nb + 1]
        send_sems, recv_sems, small_send, small_recv, local_sem = refs[2 * nb + 2:]
        x, y, c, _, _ = _place()
        sibling = (x, y, 1 - c)
        my_id = 4 * x + 2 * y + c
        local = pltpu.make_async_copy(small_ref, small_out.at[my_id], local_sem)
        local.start()
        sends = []
        for w in range(nb):
            other = _half_rows(srcs[w].shape[1], 1 - c)
            sends.append(_rcopy(srcs[w].at[:, other], dsts[w], send_sems.at[w], recv_sems.at[w], sibling))
        for r in range(1, N_DEV):
            to = (1 - x if r & 4 else x, 1 - y if r & 2 else y, 1 - c if r & 1 else c)
            sends.append(_rcopy(small_ref, small_out.at[my_id], small_send.at[r - 1], small_recv.at[r - 1], to))
        for cp in sends:
            cp.start()
        for w in range(nb):
            _rcopy(dsts[w], dsts[w], send_sems.at[w], recv_sems.at[w], sibling).wait_recv()
        for r in range(1, N_DEV):
            frm = (1 - x if r & 4 else x, 1 - y if r & 2 else y, 1 - c if r & 1 else c)
            frm_id = 4 * frm[0] + 2 * frm[1] + frm[2]
            _rcopy(small_ref, small_out.at[frm_id], small_send.at[r - 1], small_recv.at[r - 1], frm).wait_recv()
        for cp in sends:
            cp.wait_send()
        local.wait()

    out_shape = [jax.ShapeDtypeStruct((g.shape[0], g.shape[1] // 2, g.shape[2]), g.dtype) for g in grads]
    out_shape.append(jax.ShapeDtypeStruct((N_DEV,) + small.shape, small.dtype))
    return pl.pallas_call(
        body, name="pair_exchange", out_shape=out_shape,
        in_specs=[ANY] * (nb + 1), out_specs=[ANY] * (nb + 1),
        scratch_shapes=[pltpu.SemaphoreType.DMA((nb,)), pltpu.SemaphoreType.DMA((nb,)),
                        pltpu.SemaphoreType.DMA((N_DEV - 1,)), pltpu.SemaphoreType.DMA((N_DEV - 1,)),
                        pltpu.SemaphoreType.DMA(())],
    )(*grads, small)


def _chip_exchange(partials):
    nb = len(partials)

    def body(*refs):
        srcs, dsts = refs[:nb], refs[nb:2 * nb]
        send_sems, recv_sems = refs[2 * nb:]
        _, _, c, me, chips = _place()
        sends = []
        for w in range(nb):
            for k, chip in enumerate(chips):
                cid = 2 * chip[0] + chip[1]
                sends.append(_rcopy(srcs[w].at[cid], dsts[w].at[me], send_sems.at[w, k], recv_sems.at[w, k],
                                    (*chip, c)))
        for cp in sends:
            cp.start()
        for w in range(nb):
            for k, chip in enumerate(chips):
                cid = 2 * chip[0] + chip[1]
                _rcopy(srcs[w].at[cid], dsts[w].at[cid], send_sems.at[w, k], recv_sems.at[w, k],
                       (*chip, c)).wait_recv()
        for cp in sends:
            cp.wait_send()

    return pl.pallas_call(
        body, name="chip_exchange", out_shape=[jax.ShapeDtypeStruct(p.shape, p.dtype) for p in partials],
        in_specs=[ANY] * nb, out_specs=[ANY] * nb,
        scratch_shapes=[pltpu.SemaphoreType.DMA((nb, 3)), pltpu.SemaphoreType.DMA((nb, 3))],
    )(*partials)


def _pair_share(name, shards, late=None):
    nb = len(shards)
    nl = 0 if late is None else 1

    def body(*refs):
        srcs = refs[:nb]
        dsts = refs[nb + nl:2 * nb + nl]
        send_sems, recv_sems = refs[2 * nb + 2 * nl:2 * nb + 2 * nl + 2]
        x, y, c, _, _ = _place()
        sibling = (x, y, 1 - c)
        sends = []
        for w in range(nb):
            mine = _half_rows(dsts[w].shape[0], c)
            sends.append(_rcopy(srcs[w].at[mine], dsts[w].at[mine], send_sems.at[w], recv_sems.at[w], sibling))
        if nl:
            late_ref, late_out = refs[nb], refs[2 * nb + 1]
            late_send, late_recv, local_sem = refs[2 * nb + 4:]
            my_id = 4 * x + 2 * y + c
            peer = lambda r: (1 - x if r & 4 else x, 1 - y if r & 2 else y, 1 - c if r & 1 else c)
            local = pltpu.make_async_copy(late_ref, late_out.at[my_id], local_sem)
            local.start()
            for r in range(1, N_DEV):
                sends.append(_rcopy(late_ref, late_out.at[my_id], late_send.at[r - 1], late_recv.at[r - 1],
                                    peer(r)))
        for cp in sends:
            cp.start()
        for w in range(nb):
            other = _half_rows(dsts[w].shape[0], 1 - c)
            _rcopy(srcs[w].at[other], dsts[w].at[other], send_sems.at[w], recv_sems.at[w], sibling).wait_recv()
        if nl:
            for r in range(1, N_DEV):
                frm = peer(r)
                _rcopy(late_ref, late_out.at[4 * frm[0] + 2 * frm[1] + frm[2]], late_send.at[r - 1],
                       late_recv.at[r - 1], frm).wait_recv()
        for cp in sends:
            cp.wait_send()
        if nl:
            local.wait()

    out_shape = [jax.ShapeDtypeStruct(h.shape, h.dtype) for h in shards]
    scratch = [pltpu.SemaphoreType.DMA((nb,)), pltpu.SemaphoreType.DMA((nb,))]
    if nl:
        out_shape.append(jax.ShapeDtypeStruct((N_DEV,) + late.shape, late.dtype))
        scratch += [pltpu.SemaphoreType.DMA((N_DEV - 1,)), pltpu.SemaphoreType.DMA((N_DEV - 1,)),
                    pltpu.SemaphoreType.DMA(())]
    return pl.pallas_call(
        body, name=name, out_shape=out_shape,
        in_specs=[ANY] * (nb + nl), out_specs=[ANY] * (nb + nl), scratch_shapes=scratch,
        input_output_aliases={w: w for w in range(nb)},
    )(*shards, *(() if late is None else (late,)))


def _row_tile(rows):
    return _pick(rows, 128, 64, 16, 8)


def _pair_sum(name, idx, grad, recv):
    n, half, cols = recv.shape
    tr = _row_tile(half)
    nrb = half // tr

    def body(idx_ref, g_ref, r_ref, o_ref):
        o_ref[...] = (g_ref[...] + r_ref[...]).astype(BF16)

    return pl.pallas_call(
        body, name=name, out_shape=jax.ShapeDtypeStruct(recv.shape, BF16),
        grid_spec=pltpu.PrefetchScalarGridSpec(
            num_scalar_prefetch=1, grid=(n - 1, nrb),
            in_specs=[pl.BlockSpec((None, tr, cols), lambda j, i, s: (s[2 + j], s[0] * nrb + i, 0)),
                      pl.BlockSpec((None, tr, cols), lambda j, i, s: (s[2 + j], i, 0))],
            out_specs=pl.BlockSpec((None, tr, cols), lambda j, i, s: (s[2 + j], i, 0))),
        compiler_params=_params(("parallel", "parallel")),
    )(idx, grad, recv)


def _final_sum(name, idx, grad, recv, chip_sums):
    _, half, cols = recv.shape
    tr = _row_tile(half)
    nrb = half // tr

    def body(idx_ref, g_ref, r_ref, p1_ref, p2_ref, p3_ref, o_ref):
        acc = g_ref[...] + r_ref[...]
        for p_ref in (p1_ref, p2_ref, p3_ref):
            acc = acc + p_ref[...].astype(F32)
        o_ref[...] = acc

    slot = lambda which: pl.BlockSpec((None, tr, cols), lambda i, s, which=which: (s[which], i, 0))
    return pl.pallas_call(
        body, name=name, out_shape=jax.ShapeDtypeStruct((2 * half, cols), F32),
        grid_spec=pltpu.PrefetchScalarGridSpec(
            num_scalar_prefetch=1, grid=(nrb,),
            in_specs=[pl.BlockSpec((None, tr, cols), lambda i, s: (s[1], s[0] * nrb + i, 0)),
                      slot(1), slot(2), slot(3), slot(4)],
            out_specs=pl.BlockSpec((tr, cols), lambda i, s: (s[0] * nrb + i, 0))),
        compiler_params=_params(("parallel",)),
    )(idx, grad, recv, chip_sums, chip_sums, chip_sums)


def _small_sum(name, packs):
    n, rows, _ = packs.shape

    def body(p_ref, o_ref):
        acc = p_ref[0]
        for k in range(1, n):
            acc = acc + p_ref[k]
        o_ref[...] = acc

    return pl.pallas_call(
        body, name=name, out_shape=jax.ShapeDtypeStruct((rows, LANES), F32),
        in_specs=[pl.BlockSpec(memory_space=pltpu.VMEM)], out_specs=pl.BlockSpec(memory_space=pltpu.VMEM),
        compiler_params=pltpu.CompilerParams(vmem_limit_bytes=VMEM_LIMIT),
    )(packs)


def _adamw_math(w, g, m, v):
    m_new = ADAM_B1 * m + (1.0 - ADAM_B1) * g
    v_new = ADAM_B2 * v + (1.0 - ADAM_B2) * (g * g)
    m_hat = m_new / (1.0 - ADAM_B1 ** ADAM_STEP)
    v_hat = v_new / (1.0 - ADAM_B2 ** ADAM_STEP)
    return -ADAM_LR * (m_hat / (jnp.sqrt(v_hat) + ADAM_EPS) + ADAM_WD * w), m_new, v_new


def _adamw_many(name, ws, gs, ms, vs):
    n = len(ws)

    def body(*refs):
        for i in range(n):
            d, m_new, v_new = _adamw_math(refs[i][...], refs[n + i][...], refs[2 * n + i][...],
                                          refs[3 * n + i][...])
            refs[4 * n + i][...] = d
            refs[5 * n + i][...] = m_new
            refs[6 * n + i][...] = v_new

    vmem = pl.BlockSpec(memory_space=pltpu.VMEM)
    res = pl.pallas_call(
        body, name=name, in_specs=[vmem] * (4 * n), out_specs=[vmem] * (3 * n),
        out_shape=[jax.ShapeDtypeStruct(w.shape, F32) for w in ws] * 3,
        compiler_params=pltpu.CompilerParams(vmem_limit_bytes=VMEM_LIMIT),
    )(*ws, *gs, *ms, *vs)
    return res[:n], res[n:2 * n], res[2 * n:]


def _adamw(name, w, g, m, v):
    rows, cols = w.shape
    tr = rows if rows * cols * 4 <= (2 << 20) else _row_tile(rows)

    def body(w_ref, g_ref, m_ref, v_ref, g_out_ref, d_ref, nm_ref, nv_ref):
        gv = g_ref[...]
        g_out_ref[...] = gv
        d_ref[...], nm_ref[...], nv_ref[...] = _adamw_math(w_ref[...], gv, m_ref[...], v_ref[...])

    blk = pl.BlockSpec((tr, cols), lambda i: (i, 0))
    sds = jax.ShapeDtypeStruct((rows, cols), F32)
    return pl.pallas_call(
        body, name=name, grid=(rows // tr,), in_specs=[blk] * 4, out_specs=[blk] * 4, out_shape=[sds] * 4,
        compiler_params=_params(("parallel",)),
    )(w, g, m, v)


def _train_step(x, target, W, M, V):
    xi, yi, ci = lax.axis_index("x"), lax.axis_index("y"), lax.axis_index("c")
    me = 2 * xi + yi
    big = {n: W[n][0] for n in BIG}
    big_m = {n: M[n][0] for n in BIG}
    big_v = {n: V[n][0] for n in BIG}

    others = [jnp.where(jnp.int32(i) >= me, i + 1, i) for i in range(N_CHIPS - 1)]
    idx = jnp.stack([ci, me] + others).astype(jnp.int32)

    sharded_shapes = [W[n].shape[1:] for n in SMALL_SHARDED]
    small_pack = _pack([W[n][0] for n in SMALL_SHARDED], _pack_rows(sharded_shapes))
    small_land = lax.dynamic_update_slice(jnp.zeros((N_CHIPS,) + small_pack.shape, F32), small_pack[None],
                                          (me, 0, 0))
    replicated = {n: (W[n].reshape(1, -1) if W[n].ndim <= 2 else W[n][0]) for n in SMALL_REPLICATED}

    early = ("w_out", "w_up", "w_down")
    small_late = "norm_mix_g"
    small_early = tuple(n for n in SMALL if n != small_late)
    global_shape = lambda n: ((W[n].shape[1], W[n].shape[2] * N_CHIPS) if n in SMALL_SHARDED else
                              tuple(W[n].shape) if W[n].ndim == 1 else tuple(W[n].shape[1:]))
    small_shapes = [global_shape(n) for n in small_early]

    def shard_major(n, g):
        if n == "w_in":
            return _w_in_local_to_shards(g)
        return g if g.ndim == 3 else g.reshape((N_CHIPS, -1) + g.shape[1:])

    class _SplitComm:
        def reduce_early(self, grads):
            self.e_sems, self.e_bufs, token = _pair_start("pair_start_early",
                                                          [shard_major(n, grads[n]) for n in early])
            return token

        def reduce_early_mid(self, after):
            n = len(early)
            bufs, _ = _pair_wait("pair_wait_early", n, self.e_bufs, self.e_sems, after)
            self.e_grads, self.e_recv = bufs[:n], bufs[n:2 * n]
            partial = [_pair_sum(f"pair_sum_{nm}", idx, g, r) for nm, g, r in zip(early, self.e_grads, self.e_recv)]
            self.e_sems, self.e_bufs, token = _chip_start("chip_start_early", partial)
            return token

        def reduce_late(self, grads):
            pack = _pack([grads[n] for n in small_early], _pack_rows(small_shapes))
            self.l_sems, self.l_bufs, token = _pair_start("pair_start_late", [shard_major("w_in", grads["w_in"])],
                                                          extra=pack)
            return token

        def reduce_late_mid(self, after):
            bufs, _ = _pair_wait("pair_wait_late", 1, self.l_bufs, self.l_sems, after)
            self.l_grads, self.l_recv = bufs[:1], bufs[1:2]
            partial = [_pair_sum("pair_sum_w_in", idx, bufs[0], bufs[1])]
            self.l_sems, self.l_bufs, token = _chip_start("chip_start_late", partial,
                                                          small=_small_pair_sum(idx, bufs[2], bufs[3]))
            return token

        def finish_early(self, after):
            n = len(early)
            bufs, _ = _chip_wait("chip_wait_early", n, self.e_bufs, self.e_sems, after)
            halves = [_final_sum(f"final_sum_{nm}", idx, g, r, p)
                      for nm, g, r, p in zip(early, self.e_grads, self.e_recv, bufs[n:2 * n])]
            return dict(zip(early, _pair_share("pair_share_early", halves)))

        def finish_late(self, after, late):
            bufs, _ = _chip_wait("chip_wait_late", 1, self.l_bufs, self.l_sems, after)
            half = _final_sum("final_sum_w_in", idx, self.l_grads[0], self.l_recv[0], bufs[1])
            small = dict(zip(small_early, _unpack(_small_sum("small_sum", bufs[2]), small_shapes)))
            whole, late_all = _pair_share("pair_share_late", [half], late)
            return whole, small, _small_sum("late_sum", late_all)

        def begin(self):
            first = {0: _place_own_shard("place_w_in", idx, big["w_in"]), 4: small_land}
            self.sems, self.lands, token = _gather_start("gather_start_0", first, (0,))
            rest = {i: _place_own_shard(f"place_{BIG[i]}", idx, big[BIG[i]], after=token) for i in (1, 2, 3)}
            sems, lands, token = _gather_start("gather_start_1", rest, (1, 2, 3), after=token)
            self.sems.update(sems)
            self.lands.update(lands)
            return token

        def mid(self, grp, after):
            if grp == 0:
                after = [after, big_m["w_in"], big_v["w_in"]]
            self.pending = _gather_mid(grp, self.lands, self.sems[grp], after)
            return self.pending[2]

        def end(self, grp, after):
            sems, bufs, _ = self.pending
            bufs, _ = _gather_end(grp, bufs, sems, after)
            if grp == 0:
                per_chip = [_unpack(bufs[1][j], sharded_shapes) for j in range(N_CHIPS)]
                out = {n: jnp.concatenate([per_chip[j][i] for j in range(N_CHIPS)], axis=1)
                       for i, n in enumerate(SMALL_SHARDED)}
                out["w_in"] = _w_in_shards_to_local(bufs[0])
                return out
            if grp == 2:
                return {"w_up": bufs[0]}
            return {("w_out" if grp == 1 else "w_down"): bufs[0].reshape(-1, bufs[0].shape[-1])}

    comm = _SplitComm()
    loss, grad_x, grads = _local_step(x[0], target[0], replicated, comm)
    loss = lax.psum(loss[0, 0], ("x", "y", "c"))
    out_g, out_d, out_m, out_v = {}, {}, {}, {}

    def update_big(n, grad):
        g, d, nm, nv = _adamw(f"adamw_{n}", big[n], grad, big_m[n], big_v[n])
        out_g[n], out_d[n], out_m[n], out_v[n] = g[None], d[None], nm[None], nv[None]
        return d

    early_grads = comm.finish_early(grad_x)
    for n in early:
        last = update_big(n, early_grads[n])
    late = _pack([grads[small_late]], _pack_rows([global_shape(small_late)]))
    w_in_grad, small_grads, late_sum = comm.finish_late(last, late)
    update_big("w_in", w_in_grad)
    small_grads[small_late] = _unpack(late_sum, [global_shape(small_late)])[0]
    for n in SMALL_SHARDED:
        width = W[n].shape[2]
        small_grads[n] = lax.dynamic_slice_in_dim(small_grads[n], me * width, width, axis=1)

    for n in SMALL:
        out_g[n] = small_grads[n].reshape(W[n].shape)
    two_d = lambda a: a.reshape(1, -1) if a.ndim == 1 else a
    results = _adamw_many("adamw_small", *[[two_d(src[n]) for n in SMALL] for src in (W, out_g, M, V)])
    for dst, arrs in zip((out_d, out_m, out_v), results):
        dst.update({n: a.reshape(W[n].shape) for n, a in zip(SMALL, arrs)})
    return (loss, grad_x[None], *[out_g[n] for n in WEIGHT_NAMES], *[out_d[n] for n in WEIGHT_NAMES],
            *[out_m[n] for n in WEIGHT_NAMES], *[out_v[n] for n in WEIGHT_NAMES])


def kernel(x, norm_mix_g, w_in, b_gate_m, mlstm_norm_g, lru_conv_w, lru_conv_b, lru_wa, lru_ba, lru_wx, lru_bx, lru_lambda, w_out, norm_ffn_g, w_up, ffn_conv_w, ffn_conv_b, w_down, norm_final_g, loss_target, m_norm_mix_g, m_w_in, m_b_gate_m, m_mlstm_norm_g, m_lru_conv_w, m_lru_conv_b, m_lru_wa, m_lru_ba, m_lru_wx, m_lru_bx, m_lru_lambda, m_w_out, m_norm_ffn_g, m_w_up, m_ffn_conv_w, m_ffn_conv_b, m_w_down, m_norm_final_g, v_norm_mix_g, v_w_in, v_b_gate_m, v_mlstm_norm_g, v_lru_conv_w, v_lru_conv_b, v_lru_wa, v_lru_ba, v_lru_wx, v_lru_bx, v_lru_lambda, v_w_out, v_norm_ffn_g, v_w_up, v_ffn_conv_w, v_ffn_conv_b, v_w_down, v_norm_final_g):
    W = dict(zip(WEIGHT_NAMES, (norm_mix_g, w_in, b_gate_m, mlstm_norm_g, lru_conv_w, lru_conv_b, lru_wa, lru_ba,
                                lru_wx, lru_bx, lru_lambda, w_out, norm_ffn_g, w_up, ffn_conv_w, ffn_conv_b,
                                w_down, norm_final_g)))
    M = dict(zip(WEIGHT_NAMES, (m_norm_mix_g, m_w_in, m_b_gate_m, m_mlstm_norm_g, m_lru_conv_w, m_lru_conv_b,
                                m_lru_wa, m_lru_ba, m_lru_wx, m_lru_bx, m_lru_lambda, m_w_out, m_norm_ffn_g,
                                m_w_up, m_ffn_conv_w, m_ffn_conv_b, m_w_down, m_norm_final_g)))
    V = dict(zip(WEIGHT_NAMES, (v_norm_mix_g, v_w_in, v_b_gate_m, v_mlstm_norm_g, v_lru_conv_w, v_lru_conv_b,
                                v_lru_wa, v_lru_ba, v_lru_wx, v_lru_bx, v_lru_lambda, v_w_out, v_norm_ffn_g,
                                v_w_up, v_ffn_conv_w, v_ffn_conv_b, v_w_down, v_norm_final_g)))
    return _train_step(x, loss_target, W, M, V)
```

```python
import functools

import jax
import jax.numpy as jnp
from jax import lax
from jax.experimental import pallas as pl
from jax.experimental.pallas import tpu as pltpu

F32 = jnp.float32
BF16 = jnp.bfloat16
MESH = pl.DeviceIdType.MESH

EPS = 1e-6
CHUNK = 512
HEADS = 4
HEAD_DIM = 256
D_MLSTM = HEADS * HEAD_DIM
LRU_BLOCKS = 8
LRU_BLOCK_DIM = 128
D_LRU = LRU_BLOCKS * LRU_BLOCK_DIM
LRU_C = 8.0
LRU_CONV = 4
FFN_CONV = 3
ADAM_LR = 0.001
ADAM_B1 = 0.9
ADAM_B2 = 0.999
ADAM_EPS = 1e-08
ADAM_WD = 0.01
ADAM_STEP = 10

N_CHIPS = 4
N_DEV = 8
LANES = 128
HALO = 8
PROJ_GATE_PAD = LANES
_QKVO = 4 * D_MLSTM
_N_GATES = 2 * HEADS
_PROJ_COLS = _QKVO + _N_GATES + 2 * D_LRU
_PROJ_PAD = _QKVO + 2 * D_LRU + PROJ_GATE_PAD
VMEM_LIMIT = 48 * 1024 * 1024
ANY = pl.BlockSpec(memory_space=pl.ANY)


def _params(sem, vmem=VMEM_LIMIT):
    return pltpu.CompilerParams(dimension_semantics=sem, vmem_limit_bytes=vmem)


def _matmul(name, a, b, grid, a_spec, b_spec, o_spec, out_sds, contract, res=None, res_spec=None, after=None):
    nk = grid[2]
    acc_shape = tuple(d for d in o_spec.block_shape if d is not None)

    def body(*refs):
        refs = list(refs)
        a_ref, b_ref = refs[:2]
        r_ref = refs[2] if res is not None else None
        o_ref = refs[-1] if nk == 1 else refs[-2]
        acc_ref = None if nk == 1 else refs[-1]
        k = pl.program_id(2)

        def part():
            return lax.dot_general(a_ref[...], b_ref[...], (contract, ((), ())), preferred_element_type=F32)

        def finish(r):
            if r_ref is not None:
                r = r_ref[...] + r
            o_ref[...] = r.astype(o_ref.dtype)

        if nk == 1:
            finish(part())
            return

        @pl.when(k == 0)
        def _():
            acc_ref[...] = part()

        @pl.when(jnp.logical_and(k > 0, k < nk - 1))
        def _():
            acc_ref[...] += part()

        @pl.when(k == nk - 1)
        def _():
            finish(acc_ref[...] + part())

    in_specs = [a_spec, b_spec] + ([] if res is None else [res_spec]) + ([] if after is None else [ANY])
    args = (a, b) + (() if res is None else (res,)) + (() if after is None else (after,))
    if after is not None:
        inner = body
        body = lambda *refs: inner(*refs[:len(in_specs) - 1], *refs[len(in_specs):])
    return pl.pallas_call(
        body, out_shape=out_sds, grid=grid, in_specs=in_specs, out_specs=o_spec,
        scratch_shapes=[] if nk == 1 else [pltpu.VMEM(acc_shape, F32)], name=name,
        compiler_params=_params(("parallel", "parallel", "arbitrary")),
    )(*args)


NN = ((1,), (0,))
NT = ((1,), (1,))
TN = ((0,), (0,))


def _mm_nn(name, a, b, tm, tn, tk, out_dtype=F32, res=None):
    m, k = a.shape
    n = b.shape[1]
    return _matmul(name, a, b, (m // tm, n // tn, k // tk),
                   pl.BlockSpec((tm, tk), lambda i, j, kk: (i, kk)),
                   pl.BlockSpec((tk, tn), lambda i, j, kk: (kk, j)),
                   pl.BlockSpec((tm, tn), lambda i, j, kk: (i, j)),
                   jax.ShapeDtypeStruct((m, n), out_dtype), NN,
                   res=res, res_spec=pl.BlockSpec((tm, tn), lambda i, j, kk: (i, j)))


def _mm_nt(name, a, b, tm, tn, tk, out_dtype=F32, res=None, after=None):
    m, k = a.shape
    n = b.shape[0]
    return _matmul(name, a, b, (m // tm, n // tn, k // tk),
                   pl.BlockSpec((tm, tk), lambda i, j, kk: (i, kk)),
                   pl.BlockSpec((tn, tk), lambda i, j, kk: (j, kk)),
                   pl.BlockSpec((tm, tn), lambda i, j, kk: (i, j)),
                   jax.ShapeDtypeStruct((m, n), out_dtype), NT,
                   res=res, res_spec=pl.BlockSpec((tm, tn), lambda i, j, kk: (i, j)), after=after)


def _mm_tn(name, a, b, tm, tn, tk, out_dtype=F32):
    k, m = a.shape
    n = b.shape[1]
    tk = min(tk, k)
    return _matmul(name, a, b, (m // tm, n // tn, k // tk),
                   pl.BlockSpec((tk, tm), lambda i, j, kk: (kk, i)),
                   pl.BlockSpec((tk, tn), lambda i, j, kk: (kk, j)),
                   pl.BlockSpec((tm, tn), lambda i, j, kk: (i, j)),
                   jax.ShapeDtypeStruct((m, n), out_dtype), TN)


def _up_shard(n):
    return 2 * (n % 2) + (n // 2) // 2, (n // 2) % 2


def _mm_up_fwd(name, a, wg_up, tm, tk):
    m, k = a.shape
    _, _, cols = wg_up.shape
    tn = cols // 2
    return _matmul(name, a, wg_up, (m // tm, 2 * N_CHIPS, k // tk),
                   pl.BlockSpec((tm, tk), lambda i, j, kk: (i, kk)),
                   pl.BlockSpec((None, tk, tn), lambda i, j, kk: (_up_shard(j)[0], kk, _up_shard(j)[1])),
                   pl.BlockSpec((tm, tn), lambda i, j, kk: (i, j)),
                   jax.ShapeDtypeStruct((m, 2 * N_CHIPS * tn), F32), NN)


def _mm_up_bwd_x(name, dgu, wg_up, tm, tn):
    m, _ = dgu.shape
    _, d, cols = wg_up.shape
    tk = cols // 2
    nk = N_CHIPS

    def body(a_ref, bg_ref, bu_ref, o_ref, acc_ref):
        k = pl.program_id(2)

        def part():
            dims = (NT, ((), ()))
            return (lax.dot_general(a_ref[:, :tk], bg_ref[...], dims, preferred_element_type=F32)
                    + lax.dot_general(a_ref[:, tk:], bu_ref[...], dims, preferred_element_type=F32))

        @pl.when(k == 0)
        def _():
            acc_ref[...] = part()

        @pl.when(jnp.logical_and(k > 0, k < nk - 1))
        def _():
            acc_ref[...] += part()

        @pl.when(k == nk - 1)
        def _():
            o_ref[...] = acc_ref[...] + part()

    wspec = lambda half: pl.BlockSpec(
        (None, tn, tk), lambda i, j, kk: (_up_shard(2 * kk + half)[0], j, _up_shard(2 * kk + half)[1]))
    return pl.pallas_call(
        body, name=name, grid=(m // tm, d // tn, nk), out_shape=jax.ShapeDtypeStruct((m, d), F32),
        in_specs=[pl.BlockSpec((tm, 2 * tk), lambda i, j, kk: (i, kk)), wspec(0), wspec(1)],
        out_specs=pl.BlockSpec((tm, tn), lambda i, j, kk: (i, j)),
        scratch_shapes=[pltpu.VMEM((tm, tn), F32)],
        compiler_params=_params(("parallel", "parallel", "arbitrary")),
    )(dgu, wg_up, wg_up)


def _mm_up_bwd_w(name, n2, dgu, tm, tk):
    s, d = n2.shape
    tk = min(tk, s)
    tn = dgu.shape[1] // (2 * N_CHIPS)
    return _matmul(name, n2, dgu, (d // tm, 2 * N_CHIPS, s // tk),
                   pl.BlockSpec((tk, tm), lambda i, j, kk: (kk, i)),
                   pl.BlockSpec((tk, tn), lambda i, j, kk: (kk, j)),
                   pl.BlockSpec((None, tm, tn), lambda i, j, kk: (_up_shard(j)[0], i, _up_shard(j)[1])),
                   jax.ShapeDtypeStruct((N_CHIPS, d, 2 * tn), F32), TN)


def _rmsnorm_fwd(name, x, g, tm=256):
    s, d = x.shape

    def body(x_ref, g_ref, n_ref, r_ref):
        xf = x_ref[...]
        r = lax.rsqrt(jnp.mean(xf * xf, axis=-1, keepdims=True) + EPS)
        n_ref[...] = ((xf * r) * g_ref[...]).astype(BF16)
        r_ref[...] = r

    return pl.pallas_call(
        body, grid=(s // tm,), name=name,
        in_specs=[pl.BlockSpec((tm, d), lambda i: (i, 0)), pl.BlockSpec((1, d), lambda i: (0, 0))],
        out_specs=[pl.BlockSpec((tm, d), lambda i: (i, 0)), pl.BlockSpec((tm, 1), lambda i: (i, 0))],
        out_shape=[jax.ShapeDtypeStruct((s, d), BF16), jax.ShapeDtypeStruct((s, 1), F32)],
        compiler_params=_params(("parallel",)),
    )(x, g)


def _rmsnorm_bwd(name, x, rstd, g, dn, dres, tm=256):
    s, d = x.shape

    def body(x_ref, r_ref, g_ref, dn_ref, dres_ref, dx_ref, dxb_ref, dg_ref):
        @pl.when(pl.program_id(0) == 0)
        def _():
            dg_ref[...] = jnp.zeros_like(dg_ref)

        r = r_ref[...]
        xhat = x_ref[...] * r
        dn_v = dn_ref[...]
        dxhat = dn_v * g_ref[...]
        dx = dres_ref[...] + r * (dxhat - xhat * jnp.mean(dxhat * xhat, axis=-1, keepdims=True))
        dx_ref[...] = dx
        dxb_ref[...] = dx.astype(BF16)
        dg_ref[...] += jnp.sum(dn_v * xhat, axis=0, keepdims=True)

    row = pl.BlockSpec((tm, d), lambda i: (i, 0))
    vec = pl.BlockSpec((1, d), lambda i: (0, 0))
    return pl.pallas_call(
        body, grid=(s // tm,), name=name,
        in_specs=[row, pl.BlockSpec((tm, 1), lambda i: (i, 0)), vec, row, row],
        out_specs=[row, row, vec],
        out_shape=[jax.ShapeDtypeStruct((s, d), F32), jax.ShapeDtypeStruct((s, d), BF16),
                   jax.ShapeDtypeStruct((1, d), F32)],
        compiler_params=_params(("arbitrary",)),
    )(x, rstd, g, dn, dres)


def _loss_head(name, x, g, target, tm=256):
    s, d = x.shape

    def body(x_ref, g_ref, t_ref, loss_ref, dx_ref, dxb_ref, dg_ref):
        @pl.when(pl.program_id(0) == 0)
        def _():
            dg_ref[...] = jnp.zeros_like(dg_ref)
            loss_ref[...] = jnp.zeros_like(loss_ref)

        xf = x_ref[...]
        gv = g_ref[...]
        r = lax.rsqrt(jnp.mean(xf * xf, axis=-1, keepdims=True) + EPS)
        xhat = xf * r
        err = xhat * gv - t_ref[...]
        loss_ref[...] += 0.5 * jnp.sum(jnp.mean(err * err, axis=-1, keepdims=True), axis=0, keepdims=True)
        dy = err * (1.0 / d)
        dxhat = dy * gv
        dx = r * (dxhat - xhat * jnp.mean(dxhat * xhat, axis=-1, keepdims=True))
        dx_ref[...] = dx
        dxb_ref[...] = dx.astype(BF16)
        dg_ref[...] += jnp.sum(dy * xhat, axis=0, keepdims=True)

    row = pl.BlockSpec((tm, d), lambda i: (i, 0))
    vec = pl.BlockSpec((1, d), lambda i: (0, 0))
    return pl.pallas_call(
        body, grid=(s // tm,), name=name,
        in_specs=[row, vec, row],
        out_specs=[pl.BlockSpec((1, 1), lambda i: (0, 0)), row, row, vec],
        out_shape=[jax.ShapeDtypeStruct((1, 1), F32), jax.ShapeDtypeStruct((s, d), F32),
                   jax.ShapeDtypeStruct((s, d), BF16), jax.ShapeDtypeStruct((1, d), F32)],
        compiler_params=_params(("arbitrary",)),
    )(x, g, target)


def _sigmoid(v):
    return 1.0 / (1.0 + jnp.exp(-v))


def _log_sigmoid(v):
    return jnp.minimum(v, 0.0) - jnp.log1p(jnp.exp(-jnp.abs(v)))


def _softplus(v):
    return jnp.maximum(v, 0.0) + jnp.log1p(jnp.exp(-jnp.abs(v)))


def _one_minus_exp(z):
    series = -z * (1.0 + z * (0.5 + z * (1.0 / 6.0 + z * (1.0 / 24.0 + z * (1.0 / 120.0)))))
    return jnp.where(z > -0.1, series, 1.0 - jnp.exp(z))


_GELU_K = 0.7978845608028654
_GELU_C = 0.044715


def _gelu(v):
    return 0.5 * v * (1.0 + jnp.tanh(_GELU_K * (v + _GELU_C * v * v * v)))


def _gelu_grad(v):
    t = jnp.tanh(_GELU_K * (v + _GELU_C * v * v * v))
    return 0.5 * (1.0 + t) + 0.5 * v * (1.0 - t * t) * _GELU_K * (1.0 + 3.0 * _GELU_C * v * v)


def _rows(shape):
    return lax.broadcasted_iota(jnp.int32, shape, 0)


def _cols(shape):
    return lax.broadcasted_iota(jnp.int32, shape, 1)


def _shift_down(v, prev, d):
    if d == 0:
        return v
    rolled = pltpu.roll(v, d, axis=0)
    head = jnp.where(_rows((HALO, v.shape[1])) >= d, rolled[:HALO], pltpu.roll(prev, d, axis=0))
    if v.shape[0] == HALO:
        return head
    return jnp.concatenate([head, rolled[HALO:]], axis=0)


def _shift_up(v, nxt, d):
    if d == 0:
        return v
    n = v.shape[0]
    rolled = pltpu.roll(v, n - d, axis=0)
    tail = jnp.where(_rows((HALO, v.shape[1])) < HALO - d, rolled[n - HALO:], pltpu.roll(nxt, HALO - d, axis=0))
    if n == HALO:
        return tail
    return jnp.concatenate([rolled[:n - HALO], tail], axis=0)


def _dot(a, b, contract):
    return lax.dot_general(a.astype(BF16), b.astype(BF16), (contract, ((), ())), preferred_element_type=F32)


def _mlstm_chunk_common(h, q_ref, k_ref, v_ref, gcol_ref, grow_ref, brow_ref, bcol_ref, m_prev):
    L = CHUNK
    sl = slice(h * HEAD_DIM, (h + 1) * HEAD_DIM)
    qh = q_ref[:, sl]
    kh = k_ref[:, sl]
    vh = v_ref[:, sl]
    qs = qh * (HEAD_DIM ** -0.5)
    gates = gcol_ref[...] + brow_ref[...]
    lane = _cols(gates.shape)
    ic = jnp.sum(jnp.where(lane == h, gates, 0.0), axis=1, keepdims=True)
    fc = jnp.sum(jnp.where(lane == HEADS + h, gates, 0.0), axis=1, keepdims=True)
    ir = grow_ref[h:h + 1, :] + bcol_ref[h:h + 1, :]
    fr = grow_ref[HEADS + h:HEADS + h + 1, :] + bcol_ref[HEADS + h:HEADS + h + 1, :]
    logf_c = _log_sigmoid(fc)
    logf_r = _log_sigmoid(fr)
    t_i = _rows((L, L))
    s_i = _cols((L, L))
    tri = t_i >= s_i
    b_c = jnp.sum(jnp.where(tri, logf_r, 0.0), axis=1, keepdims=True)
    b_r = jnp.sum(jnp.where(t_i <= s_i, logf_c, 0.0), axis=0, keepdims=True)
    btot = jnp.sum(logf_r, axis=1, keepdims=True)
    dmat = jnp.where(tri, b_c - b_r + ir, -jnp.inf)
    m_inter = b_c + m_prev
    m_t = jnp.maximum(m_inter, jnp.max(dmat, axis=1, keepdims=True))
    e_mat = jnp.exp(dmat - m_t)
    e_inter = jnp.exp(m_inter - m_t)
    wqk = _dot(qs, kh, NT) * e_mat
    w_end_r = btot - b_r + ir
    m_loc = jnp.max(w_end_r, axis=1, keepdims=True)
    e_end_c = jnp.exp(btot - b_c + ic - m_loc)
    m_new = jnp.maximum(btot + m_prev, m_loc)
    a_dec = jnp.exp(btot + m_prev - m_new)
    c_inj = jnp.exp(m_loc - m_new)
    return dict(qh=qh, kh=kh, vh=vh, qs=qs, fc=fc, tri=tri, t_i=t_i, s_i=s_i, m_t=m_t, e_mat=e_mat,
                e_inter=e_inter, wqk=wqk, e_end_c=e_end_c, m_new=m_new, a_dec=a_dec, c_inj=c_inj)


def _mlstm_fwd(proj, gates_t, bias_row, bias_col, head_g):
    s = proj.shape[0]
    nc = s // CHUNK
    L = CHUNK

    def body(q_ref, k_ref, v_ref, o_ref, gcol_ref, grow_ref, brow_ref, bcol_ref, hg_ref,
             out_ref, cprev_ref, nprev_ref, mprev_ref, c_scr, n_scr, m_scr):
        @pl.when(pl.program_id(0) == 0)
        def _():
            c_scr[...] = jnp.zeros_like(c_scr)
            n_scr[...] = jnp.zeros_like(n_scr)
            m_scr[...] = jnp.zeros_like(m_scr)

        for h in range(HEADS):
            sl = slice(h * HEAD_DIM, (h + 1) * HEAD_DIM)
            m_prev = m_scr[h:h + 1, 0:1]
            n_prev = n_scr[h:h + 1, :]
            c_prev = c_scr[h].astype(BF16)
            q = _mlstm_chunk_common(h, q_ref, k_ref, v_ref, gcol_ref, grow_ref, brow_ref, bcol_ref, m_prev)
            num = _dot(q["wqk"], q["vh"], NN) + q["e_inter"] * _dot(q["qs"], c_prev, NN)
            den = (jnp.sum(q["wqk"], axis=1, keepdims=True)
                   + q["e_inter"] * jnp.sum(q["qs"] * n_prev, axis=1, keepdims=True))
            hh = num / jnp.maximum(jnp.abs(den), jnp.exp(-q["m_t"]))
            hn = hh * lax.rsqrt(jnp.mean(hh * hh, axis=1, keepdims=True) + EPS) * hg_ref[h:h + 1, :]
            out_ref[:, sl] = (_sigmoid(o_ref[:, sl]) * hn).astype(BF16)
            cprev_ref[h] = c_prev
            nprev_ref[h:h + 1, :] = n_prev
            mprev_ref[h:h + 1, :] = jnp.broadcast_to(m_prev, (1, LANES))
            c_loc = _dot(q["kh"], q["e_end_c"] * q["vh"], TN)
            n_loc = jnp.sum(q["e_end_c"] * q["kh"], axis=0, keepdims=True)
            c_scr[h] = q["a_dec"] * c_scr[h] + q["c_inj"] * c_loc
            n_scr[h:h + 1, :] = q["a_dec"] * n_prev + q["c_inj"] * n_loc
            m_scr[h:h + 1, :] = jnp.broadcast_to(q["m_new"], (1, LANES))

    blk = lambda j: pl.BlockSpec((L, D_MLSTM), lambda c, j=j: (c, j))
    full = lambda shp: pl.BlockSpec(shp, lambda c: tuple(0 for _ in shp))
    return pl.pallas_call(
        body, grid=(nc,), name="mlstm_fwd",
        in_specs=[blk(0), blk(1), blk(2), blk(3),
                  pl.BlockSpec((L, LANES), lambda c: (c, (4 * D_MLSTM + 2 * D_LRU) // LANES)),
                  pl.BlockSpec((None, 2 * HEADS, L), lambda c: (c, 0, 0)),
                  full((1, LANES)), full((2 * HEADS, 1)), full((HEADS, HEAD_DIM))],
        out_specs=[pl.BlockSpec((L, D_MLSTM), lambda c: (c, 0)),
                   pl.BlockSpec((None, HEADS, HEAD_DIM, HEAD_DIM), lambda c: (c, 0, 0, 0)),
                   pl.BlockSpec((None, HEADS, HEAD_DIM), lambda c: (c, 0, 0)),
                   pl.BlockSpec((None, HEADS, LANES), lambda c: (c, 0, 0))],
        out_shape=[jax.ShapeDtypeStruct((s, D_MLSTM + D_LRU), BF16),
                   jax.ShapeDtypeStruct((nc, HEADS, HEAD_DIM, HEAD_DIM), BF16),
                   jax.ShapeDtypeStruct((nc, HEADS, HEAD_DIM), F32),
                   jax.ShapeDtypeStruct((nc, HEADS, LANES), F32)],
        scratch_shapes=[pltpu.VMEM((HEADS, HEAD_DIM, HEAD_DIM), F32), pltpu.VMEM((HEADS, HEAD_DIM), F32),
                        pltpu.VMEM((HEADS, LANES), F32)],
        compiler_params=_params(("arbitrary",)),
    )(proj, proj, proj, proj, proj, gates_t, bias_row, bias_col, head_g)


def _mlstm_bwd(proj, gates_t, bias_row, bias_col, head_g, cprev, nprev, mprev, dmix):
    s = proj.shape[0]
    nc = s // CHUNK
    L = CHUNK

    def body(q_ref, k_ref, v_ref, o_ref, gcol_ref, grow_ref, brow_ref, bcol_ref, hg_ref,
             cprev_ref, nprev_ref, mprev_ref, dmix_ref,
             dqkvo_ref, dgate_ref, dhg_ref, g_scr, gn_scr):
        @pl.when(pl.program_id(0) == 0)
        def _():
            g_scr[...] = jnp.zeros_like(g_scr)
            gn_scr[...] = jnp.zeros_like(gn_scr)
            dhg_ref[...] = jnp.zeros_like(dhg_ref)

        lane = _cols((L, LANES))
        dgate = jnp.zeros((L, LANES), F32)
        for h in range(HEADS):
            sl = slice(h * HEAD_DIM, (h + 1) * HEAD_DIM)
            m_prev = mprev_ref[h:h + 1, 0:1]
            n_prev = nprev_ref[h:h + 1, :]
            c_prev = cprev_ref[h]
            q = _mlstm_chunk_common(h, q_ref, k_ref, v_ref, gcol_ref, grow_ref, brow_ref, bcol_ref, m_prev)
            qh, kh, vh, qs, wqk, e_inter = q["qh"], q["kh"], q["vh"], q["qs"], q["wqk"], q["e_inter"]
            num_state = e_inter * _dot(qs, c_prev, NN)
            den_state = e_inter * jnp.sum(qs * n_prev, axis=1, keepdims=True)
            num = _dot(wqk, vh, NN) + num_state
            den = jnp.sum(wqk, axis=1, keepdims=True) + den_state
            floor = jnp.exp(-q["m_t"])
            denom = jnp.maximum(jnp.abs(den), floor)
            hh = num / denom
            rn = lax.rsqrt(jnp.mean(hh * hh, axis=1, keepdims=True) + EPS)
            hn_pre = hh * rn
            hg = hg_ref[h:h + 1, :]
            sg = _sigmoid(o_ref[:, sl])
            dout = dmix_ref[:, sl]
            d_o = dout * (hn_pre * hg) * sg * (1.0 - sg)
            dhn = dout * sg
            dhg_ref[h:h + 1, :] += jnp.sum(dhn * hn_pre, axis=0, keepdims=True)
            dhn_pre = dhn * hg
            dhh = rn * (dhn_pre - hn_pre * jnp.mean(dhn_pre * hn_pre, axis=1, keepdims=True))
            dnum = dhh / denom
            dden = jnp.where(jnp.abs(den) >= floor,
                             -jnp.sum(hh * dhh, axis=1, keepdims=True) / denom * jnp.sign(den), 0.0)
            dwqk = _dot(dnum, vh, NT) + dden
            dv = _dot(wqk, dnum, TN)
            dp = dwqk * q["e_mat"]
            dqs = _dot(dp, kh, NN) + e_inter * (_dot(dnum, c_prev, NT) + dden * n_prev)
            dk = _dot(dp, qs, TN)
            g_next = g_scr[h]
            gn_next = gn_scr[h:h + 1, :]
            w_state = q["e_end_c"] * q["c_inj"]
            dk_state = w_state * (_dot(vh, g_next, NT) + gn_next)
            dk = dk + dk_state
            dv = dv + w_state * _dot(kh, g_next, NN)
            dq = dqs * (HEAD_DIM ** -0.5)
            eye = q["t_i"] == q["s_i"]
            to_row = lambda col: jnp.sum(jnp.where(eye, col, 0.0), axis=0, keepdims=True)
            to_col = lambda row: jnp.sum(jnp.where(eye, row, 0.0), axis=1, keepdims=True)
            g_pair = dwqk * wqk
            rs_in = jnp.sum(g_pair, axis=1, keepdims=True)
            cs_in_r = jnp.sum(g_pair, axis=0, keepdims=True)
            rs_state = (jnp.sum(dnum * num_state, axis=1, keepdims=True) + dden * den_state)
            cs_state = jnp.sum(kh * dk_state, axis=1, keepdims=True)
            di_c = to_col(cs_in_r) + cs_state
            through = q["a_dec"] * (jnp.sum(jnp.sum(g_next * c_prev.astype(F32), axis=1, keepdims=True),
                                            axis=0, keepdims=True)
                                    + jnp.sum(gn_next * n_prev, axis=1, keepdims=True))
            ends_here = to_row(rs_in + rs_state) - cs_in_r
            da_c = (jnp.sum(jnp.where(q["s_i"] >= q["t_i"], ends_here, 0.0), axis=1, keepdims=True)
                    + jnp.sum(jnp.where(q["s_i"] < q["t_i"], to_row(cs_state), 0.0), axis=1, keepdims=True)
                    + through)
            df_c = da_c * _sigmoid(-q["fc"])
            dgate = dgate + jnp.where(lane == h, di_c, 0.0) + jnp.where(lane == HEADS + h, df_c, 0.0)
            dqkvo_ref[:, sl] = dq.astype(BF16)
            dqkvo_ref[:, D_MLSTM + h * HEAD_DIM:D_MLSTM + (h + 1) * HEAD_DIM] = dk.astype(BF16)
            dqkvo_ref[:, 2 * D_MLSTM + h * HEAD_DIM:2 * D_MLSTM + (h + 1) * HEAD_DIM] = dv.astype(BF16)
            dqkvo_ref[:, 3 * D_MLSTM + h * HEAD_DIM:3 * D_MLSTM + (h + 1) * HEAD_DIM] = d_o.astype(BF16)
            g_scr[h] = q["a_dec"] * g_next + _dot(e_inter * qs, dnum, TN)
            gn_scr[h:h + 1, :] = q["a_dec"] * gn_next + jnp.sum(e_inter * qs * dden, axis=0, keepdims=True)
        dgate_ref[...] = dgate

    rev = lambda c: nc - 1 - c
    blk = lambda j: pl.BlockSpec((L, D_MLSTM), lambda c, j=j: (rev(c), j))
    full = lambda shp: pl.BlockSpec(shp, lambda c: tuple(0 for _ in shp))
    return pl.pallas_call(
        body, grid=(nc,), name="mlstm_bwd",
        in_specs=[blk(0), blk(1), blk(2), blk(3),
                  pl.BlockSpec((L, LANES), lambda c: (rev(c), (4 * D_MLSTM + 2 * D_LRU) // LANES)),
                  pl.BlockSpec((None, 2 * HEADS, L), lambda c: (rev(c), 0, 0)),
                  full((1, LANES)), full((2 * HEADS, 1)), full((HEADS, HEAD_DIM)),
                  pl.BlockSpec((None, HEADS, HEAD_DIM, HEAD_DIM), lambda c: (rev(c), 0, 0, 0)),
                  pl.BlockSpec((None, HEADS, HEAD_DIM), lambda c: (rev(c), 0, 0)),
                  pl.BlockSpec((None, HEADS, LANES), lambda c: (rev(c), 0, 0)),
                  pl.BlockSpec((L, D_MLSTM), lambda c: (rev(c), 0))],
        out_specs=[pl.BlockSpec((L, 4 * D_MLSTM), lambda c: (rev(c), 0)),
                   pl.BlockSpec((L, LANES), lambda c: (rev(c), 0)),
                   full((HEADS, HEAD_DIM))],
        out_shape=[jax.ShapeDtypeStruct((s, _PROJ_PAD), BF16),
                   jax.ShapeDtypeStruct((s, LANES), F32),
                   jax.ShapeDtypeStruct((HEADS, HEAD_DIM), F32)],
        scratch_shapes=[pltpu.VMEM((HEADS, HEAD_DIM, HEAD_DIM), F32), pltpu.VMEM((HEADS, HEAD_DIM), F32)],
        compiler_params=_params(("arbitrary",)),
    )(proj, proj, proj, proj, proj, gates_t, bias_row, bias_col, head_g, cprev, nprev, mprev, dmix)


def _lru_gates(xc, wa_ref, wx_ref, ba, bx, lam):
    r = _sigmoid(_dot(xc, wa_ref[...], NN) + ba)
    ig = _sigmoid(_dot(xc, wx_ref[...], NN) + bx)
    sp = _softplus(-lam)
    log_a = (-LRU_C * r) * sp
    a = jnp.exp(log_a)
    mult = jnp.sqrt(_one_minus_exp(2.0 * log_a))
    return r, ig, sp, a, mult


def _lru_conv(xr, prev, w_ref, b):
    xc = b + _shift_down(xr, prev, 3) * w_ref[0:1, :]
    for j in range(1, LRU_CONV):
        xc = xc + _shift_down(xr, prev, LRU_CONV - 1 - j) * w_ref[j:j + 1, :]
    return xc


def _lru_fwd(proj, mix, conv_w, conv_b, wa, wx, ba, bx, lam, tt=1024):
    s = proj.shape[0]
    tt = min(tt, s)
    nt = s // tt
    B = LRU_BLOCK_DIM
    lru_col = 4 * D_MLSTM // B
    mix_col = D_MLSTM // B

    def body(xr_ref, gr_ref, cw_ref, cb_ref, wa_ref, wx_ref, ba_ref, bx_ref, lam_ref, mix_in_ref,
             out_ref, h_ref, prev_scr, hcar_scr):
        @pl.when(pl.program_id(1) == 0)
        def _():
            prev_scr[...] = jnp.zeros_like(prev_scr)
            hcar_scr[...] = jnp.zeros_like(hcar_scr)

        xr = xr_ref[...]
        xc = _lru_conv(xr, prev_scr[...], cw_ref, cb_ref[...])
        prev_scr[...] = xr[tt - HALO:, :]
        _, ig, _, a, mult = _lru_gates(xc, wa_ref, wx_ref, ba_ref[...], bx_ref[...], lam_ref[...])
        u = mult * (ig * xc)
        rows = _rows((tt, B))
        acc_a, acc_b = a, u
        d = 1
        while d < tt:
            keep = rows >= d
            sh_a = jnp.where(keep, pltpu.roll(acc_a, d, axis=0), 1.0)
            sh_b = jnp.where(keep, pltpu.roll(acc_b, d, axis=0), 0.0)
            acc_b = acc_a * sh_b + acc_b
            acc_a = acc_a * sh_a
            d *= 2
        hv = acc_b + acc_a * hcar_scr[0:1, :]
        hcar_scr[...] = jnp.broadcast_to(hv[tt - 1:tt, :], hcar_scr.shape)
        h_ref[...] = hv
        out_ref[...] = (hv * _gelu(gr_ref[...])).astype(BF16)

    chan = lambda rws: pl.BlockSpec((rws, B), lambda n, i: (0, n))
    return pl.pallas_call(
        body, grid=(LRU_BLOCKS, nt), name="lru_fwd",
        in_specs=[pl.BlockSpec((tt, B), lambda n, i: (i, lru_col + 2 * n)),
                  pl.BlockSpec((tt, B), lambda n, i: (i, lru_col + 2 * n + 1)),
                  chan(LRU_CONV), chan(1),
                  pl.BlockSpec((None, B, B), lambda n, i: (n, 0, 0)),
                  pl.BlockSpec((None, B, B), lambda n, i: (n, 0, 0)),
                  chan(1), chan(1), chan(1), ANY],
        out_specs=[pl.BlockSpec((tt, B), lambda n, i: (i, mix_col + n)), pl.BlockSpec((tt, B), lambda n, i: (i, n))],
        out_shape=[jax.ShapeDtypeStruct(mix.shape, BF16), jax.ShapeDtypeStruct((s, D_LRU), F32)],
        scratch_shapes=[pltpu.VMEM((HALO, B), F32), pltpu.VMEM((HALO, B), F32)],
        input_output_aliases={9: 0},
        compiler_params=_params(("parallel", "arbitrary")),
    )(proj, proj, conv_w, conv_b, wa, wx, ba, bx, lam, mix)


def _lru_bwd(proj, hsave, dmix, dproj, conv_w, conv_b, wa, wx, ba, bx, lam, tt=1024):
    s = proj.shape[0]
    tt = min(tt, s)
    nt = s // tt
    B = LRU_BLOCK_DIM
    lru_col = 4 * D_MLSTM // B
    dmix_col = D_MLSTM // B
    hpb = tt // HALO

    def body(xr_ref, xprev_ref, gr_ref, h_ref, hprev_ref, dmix_ref, cw_ref, cb_ref, wa_ref, wx_ref,
             ba_ref, bx_ref, lam_ref, dproj_in_ref,
             dxg_ref, dcw_ref, dcb_ref, dwa_ref, dwx_ref, dba_ref, dbx_ref, dlam_ref,
             gcar_scr, acar_scr, dxc_scr):
        i = pl.program_id(1)
        first_tile = i == nt - 1

        @pl.when(i == 0)
        def _():
            gcar_scr[...] = jnp.zeros_like(gcar_scr)
            acar_scr[...] = jnp.zeros_like(acar_scr)
            dxc_scr[...] = jnp.zeros_like(dxc_scr)
            for ref in (dcw_ref, dcb_ref, dwa_ref, dwx_ref, dba_ref, dbx_ref, dlam_ref):
                ref[...] = jnp.zeros_like(ref)

        xr = xr_ref[...]
        xprev = jnp.where(first_tile, 0.0, xprev_ref[...])
        hprev = jnp.where(first_tile, 0.0, hprev_ref[...])
        lam = lam_ref[...]
        taps = [_shift_down(xr, xprev, LRU_CONV - 1 - j) for j in range(LRU_CONV)]
        xc = cb_ref[...] + taps[0] * cw_ref[0:1, :]
        for j in range(1, LRU_CONV):
            xc = xc + taps[j] * cw_ref[j:j + 1, :]
        r, ig, sp, a, mult = _lru_gates(xc, wa_ref, wx_ref, ba_ref[...], bx_ref[...], lam)
        gr = gr_ref[...]
        hv = h_ref[...]
        dout = dmix_ref[...]
        dxg_ref[:, B:] = (dout * hv * _gelu_grad(gr)).astype(BF16)
        dh = dout * _gelu(gr)
        rows = _rows((tt, B))
        acc_a = _shift_up(a, acar_scr[...], 1)
        acc_b = dh
        d = 1
        while d < tt:
            keep = rows < tt - d
            sh_a = jnp.where(keep, pltpu.roll(acc_a, tt - d, axis=0), 1.0)
            sh_b = jnp.where(keep, pltpu.roll(acc_b, tt - d, axis=0), 0.0)
            acc_b = acc_a * sh_b + acc_b
            acc_a = acc_a * sh_a
            d *= 2
        gv = acc_b + acc_a * gcar_scr[0:1, :]
        gcar_scr[...] = jnp.broadcast_to(gv[0:1, :], gcar_scr.shape)
        acar_scr[...] = jnp.broadcast_to(a[0:1, :], acar_scr.shape)
        h_before = _shift_down(hv, hprev, 1)
        da = gv * h_before
        dmult = gv * (ig * xc)
        dig = gv * mult * xc
        dxc = gv * mult * ig
        dlog_a = da * a - dmult * (a * a) / mult
        dr = dlog_a * (-LRU_C * sp)
        dlam_ref[...] += jnp.sum(dlog_a * (-LRU_C * r), axis=0, keepdims=True) * (-_sigmoid(-lam))
        dpre_r = dr * r * (1.0 - r)
        dpre_i = dig * ig * (1.0 - ig)
        dba_ref[...] += jnp.sum(dpre_r, axis=0, keepdims=True)
        dbx_ref[...] += jnp.sum(dpre_i, axis=0, keepdims=True)
        dwa_ref[...] += _dot(xc, dpre_r, TN)
        dwx_ref[...] += _dot(xc, dpre_i, TN)
        dxc = dxc + _dot(dpre_r, wa_ref[...], NT) + _dot(dpre_i, wx_ref[...], NT)
        dcb_ref[...] += jnp.sum(dxc, axis=0, keepdims=True)
        nxt = dxc_scr[...]
        dxr = jnp.zeros((tt, B), F32)
        for j in range(LRU_CONV):
            sft = LRU_CONV - 1 - j
            dcw_ref[j:j + 1, :] += jnp.sum(dxc * taps[j], axis=0, keepdims=True)
            dxr = dxr + _shift_up(dxc, nxt, sft) * cw_ref[j:j + 1, :]
        dxc_scr[...] = dxc[:HALO, :]
        dxg_ref[:, :B] = dxr.astype(BF16)

    rev = lambda i: nt - 1 - i
    tile = lambda col, step: pl.BlockSpec((tt, B), lambda n, i: (rev(i), col + step * n))
    halo = lambda col, step: pl.BlockSpec(
        (HALO, B), lambda n, i: (jnp.maximum(rev(i) * hpb - 1, 0), col + step * n))
    chan = lambda rws: pl.BlockSpec((rws, B), lambda n, i: (0, n))
    wblk = pl.BlockSpec((None, B, B), lambda n, i: (n, 0, 0))
    return pl.pallas_call(
        body, grid=(LRU_BLOCKS, nt), name="lru_bwd",
        in_specs=[tile(lru_col, 2), halo(lru_col, 2), tile(lru_col + 1, 2), tile(0, 1), halo(0, 1),
                  tile(dmix_col, 1), chan(LRU_CONV), chan(1), wblk, wblk, chan(1), chan(1), chan(1), ANY],
        out_specs=[pl.BlockSpec((tt, 2 * B), lambda n, i: (rev(i), lru_col // 2 + n)),
                   chan(LRU_CONV), chan(1), wblk, wblk, chan(1), chan(1), chan(1)],
        out_shape=[jax.ShapeDtypeStruct(dproj.shape, BF16),
                   jax.ShapeDtypeStruct((LRU_CONV, D_LRU), F32), jax.ShapeDtypeStruct((1, D_LRU), F32),
                   jax.ShapeDtypeStruct((LRU_BLOCKS, B, B), F32), jax.ShapeDtypeStruct((LRU_BLOCKS, B, B), F32),
                   jax.ShapeDtypeStruct((1, D_LRU), F32), jax.ShapeDtypeStruct((1, D_LRU), F32),
                   jax.ShapeDtypeStruct((1, D_LRU), F32)],
        scratch_shapes=[pltpu.VMEM((HALO, B), F32), pltpu.VMEM((HALO, B), F32), pltpu.VMEM((HALO, B), F32)],
        input_output_aliases={13: 0},
        compiler_params=_params(("parallel", "arbitrary")),
    )(proj, proj, proj, hsave, hsave, dmix, conv_w, conv_b, wa, wx, ba, bx, lam, dproj)


def _ffn_conv(gp, prev, w_ref, b):
    g = b + _shift_down(gp, prev, 2) * w_ref[0:1, :]
    for j in range(1, FFN_CONV):
        g = g + _shift_down(gp, prev, FFN_CONV - 1 - j) * w_ref[j:j + 1, :]
    return g


def _ffn_act_fwd(gu, conv_w, conv_b, tt=256):
    s = gu.shape[0]
    tt = min(tt, s)
    d_ff = conv_w.shape[1]
    tc = d_ff // N_CHIPS
    hpb = tt // HALO

    def body(g_ref, gprev_ref, u_ref, w_ref, b_ref, act_ref):
        prev = jnp.where(pl.program_id(0) == 0, 0.0, gprev_ref[...])
        gate = _ffn_conv(g_ref[...], prev, w_ref, b_ref[...])
        act_ref[...] = (gate * _sigmoid(gate) * u_ref[...]).astype(BF16)

    return pl.pallas_call(
        body, grid=(s // tt, N_CHIPS), name="ffn_act_fwd",
        in_specs=[pl.BlockSpec((tt, tc), lambda i, j: (i, 2 * j)),
                  pl.BlockSpec((HALO, tc), lambda i, j: (jnp.maximum(i * hpb - 1, 0), 2 * j)),
                  pl.BlockSpec((tt, tc), lambda i, j: (i, 2 * j + 1)),
                  pl.BlockSpec((FFN_CONV, tc), lambda i, j: (0, j)),
                  pl.BlockSpec((1, tc), lambda i, j: (0, j))],
        out_specs=pl.BlockSpec((tt, tc), lambda i, j: (i, j)),
        out_shape=jax.ShapeDtypeStruct((s, d_ff), BF16),
        compiler_params=_params(("parallel", "parallel")),
    )(gu, gu, gu, conv_w, conv_b)


def _ffn_act_bwd(gu, dact, conv_w, conv_b, tt=256):
    s = gu.shape[0]
    tt = min(tt, s)
    nt = s // tt
    d_ff = conv_w.shape[1]
    tc = d_ff // N_CHIPS
    hpb = tt // HALO

    def dgate_of(gate, up, da):
        sg = _sigmoid(gate)
        return da * up * (sg * (1.0 + gate * (1.0 - sg))), da * (gate * sg)

    def body(g_ref, gprev_ref, gnext_ref, u_ref, unext_ref, da_ref, danext_ref, w_ref, b_ref,
             dgu_ref, dw_ref, db_ref):
        i = pl.program_id(1)

        @pl.when(i == 0)
        def _():
            dw_ref[...] = jnp.zeros_like(dw_ref)
            db_ref[...] = jnp.zeros_like(db_ref)

        gp = g_ref[...]
        prev = jnp.where(i == 0, 0.0, gprev_ref[...])
        bias = b_ref[...]
        taps = [_shift_down(gp, prev, FFN_CONV - 1 - j) for j in range(FFN_CONV)]
        gate = bias + taps[0] * w_ref[0:1, :]
        for j in range(1, FFN_CONV):
            gate = gate + taps[j] * w_ref[j:j + 1, :]
        dgate, dup = dgate_of(gate, u_ref[...], da_ref[...])
        gate_n = _ffn_conv(gnext_ref[...], gp[tt - HALO:, :], w_ref, bias)
        dgate_n, _ = dgate_of(gate_n, unext_ref[...], danext_ref[...])
        dgate_n = jnp.where(i == nt - 1, 0.0, dgate_n)
        db_ref[...] += jnp.sum(dgate, axis=0, keepdims=True)
        dgp = jnp.zeros((tt, tc), F32)
        for j in range(FFN_CONV):
            dw_ref[j:j + 1, :] += jnp.sum(dgate * taps[j], axis=0, keepdims=True)
            dgp = dgp + _shift_up(dgate, dgate_n, FFN_CONV - 1 - j) * w_ref[j:j + 1, :]
        dgu_ref[:, :tc] = dgp.astype(BF16)
        dgu_ref[:, tc:] = dup.astype(BF16)

    tile = lambda half: pl.BlockSpec((tt, tc), lambda j, i, half=half: (i, 2 * j + half))
    hprev = lambda half: pl.BlockSpec((HALO, tc), lambda j, i, half=half: (jnp.maximum(i * hpb - 1, 0), 2 * j + half))
    hnext = lambda half: pl.BlockSpec(
        (HALO, tc), lambda j, i, half=half: (jnp.minimum((i + 1) * hpb, nt * hpb - 1), 2 * j + half))
    return pl.pallas_call(
        body, grid=(N_CHIPS, nt), name="ffn_act_bwd",
        in_specs=[tile(0), hprev(0), hnext(0), tile(1), hnext(1),
                  pl.BlockSpec((tt, tc), lambda j, i: (i, j)),
                  pl.BlockSpec((HALO, tc), lambda j, i: (jnp.minimum((i + 1) * hpb, nt * hpb - 1), j)),
                  pl.BlockSpec((FFN_CONV, tc), lambda j, i: (0, j)),
                  pl.BlockSpec((1, tc), lambda j, i: (0, j))],
        out_specs=[pl.BlockSpec((tt, 2 * tc), lambda j, i: (i, j)),
                   pl.BlockSpec((FFN_CONV, tc), lambda j, i: (0, j)),
                   pl.BlockSpec((1, tc), lambda j, i: (0, j))],
        out_shape=[jax.ShapeDtypeStruct((s, 2 * d_ff), BF16),
                   jax.ShapeDtypeStruct((FFN_CONV, d_ff), F32), jax.ShapeDtypeStruct((1, d_ff), F32)],
        compiler_params=_params(("parallel", "arbitrary")),
    )(gu, gu, gu, gu, gu, dact, dact, conv_w, conv_b)


def _gate_grads(dgate, dproj, tm=512):
    s, n = dgate.shape
    tm = min(tm, s)

    def body(a_ref, dproj_in_ref, o_ref, dproj_ref):
        @pl.when(pl.program_id(0) == 0)
        def _():
            o_ref[...] = jnp.zeros_like(o_ref)
        a = a_ref[...]
        o_ref[...] += jnp.sum(a, axis=0, keepdims=True)
        dproj_ref[...] = a.astype(BF16)

    return pl.pallas_call(
        body, grid=(s // tm,), name="gate_grads",
        in_specs=[pl.BlockSpec((tm, n), lambda i: (i, 0)), ANY],
        out_specs=[pl.BlockSpec((1, n), lambda i: (0, 0)),
                   pl.BlockSpec((tm, n), lambda i: (i, (_QKVO + 2 * D_LRU) // LANES))],
        out_shape=[jax.ShapeDtypeStruct((1, n), F32), jax.ShapeDtypeStruct(dproj.shape, BF16)],
        input_output_aliases={1: 1},
        compiler_params=_params(("arbitrary",)),
    )(dgate, dproj)


def _pick(n, *cands):
    for c in cands:
        if n % c == 0:
            return c
    raise ValueError(f"no tile for {n}")


def _behind(a, token):
    return a if token is None else a + token[0:1, 0:1].astype(a.dtype).reshape((1,) * a.ndim)


class _Gathered:
    def __init__(self, w):
        self.w = w

    def begin(self):
        return None

    def mid(self, grp, after):
        return None

    def end(self, grp, after):
        return self.w

    def reduce_early(self, grads):
        return None

    def reduce_early_mid(self, after):
        return None

    def reduce_late(self, grads):
        return None

    def reduce_late_mid(self, after):
        return None


def _local_step(x, target, w, comm):
    s, d = x.shape
    nc = s // CHUNK
    tm = _pick(s, 1024, 512, 256)
    tn_proj = _pick(_PROJ_PAD, 896)
    gate_col = 4 * D_MLSTM + 2 * D_LRU
    w = dict(w)

    token = comm.begin()
    n1, rstd1 = _rmsnorm_fwd("norm_mix_fwd", x, _behind(w["norm_mix_g"], token))
    comm.mid(0, n1)
    w.update(comm.end(0, None))
    proj = _mm_nn("proj_fwd", n1, w["w_in"], tm, tn_proj, d)
    token = comm.mid(1, proj)
    gates = proj[:, gate_col:gate_col + 2 * HEADS]
    gates_t = gates.reshape(nc, CHUNK, 2 * HEADS).transpose(0, 2, 1)
    bias_row = _behind(jnp.pad(w["b_gate_m"], ((0, 0), (0, LANES - 2 * HEADS))), token)
    bias_col = w["b_gate_m"].reshape(2 * HEADS, 1)
    mix, cprev, nprev, mprev = _mlstm_fwd(proj, gates_t, bias_row, bias_col, w["mlstm_norm_g"])
    mix, hsave = _lru_fwd(proj, mix, w["lru_conv_w"], w["lru_conv_b"], w["lru_wa"], w["lru_wx"],
                          w["lru_ba"], w["lru_bx"], w["lru_lambda"])
    w.update(comm.end(1, hsave))
    token = comm.mid(2, hsave)
    x1 = _mm_nn("out_fwd", mix, w["w_out"], tm, 1024, d, res=x)
    n2, rstd2 = _rmsnorm_fwd("norm_ffn_fwd", x1, _behind(w["norm_ffn_g"], token))
    w.update(comm.end(2, n2))
    token = comm.mid(3, n2)
    gu = _mm_up_fwd("up_fwd", n2, w["w_up"], tm, d)
    act = _ffn_act_fwd(gu, w["ffn_conv_w"], _behind(w["ffn_conv_b"], token))
    w.update(comm.end(3, act))
    d_ff = w["w_down"].shape[0]
    x2 = _mm_nn("down_fwd", act, w["w_down"], min(tm, 512), 1024, d_ff // 2, res=x1)
    loss, dx2, dx2b, g_norm_final = _loss_head("loss_head", x2, w["norm_final_g"], target)

    grads = {"norm_final_g": g_norm_final}
    dact = _mm_nt("down_bwd_x", dx2b, w["w_down"], tm, d_ff // N_CHIPS, d)
    grads["w_down"] = _mm_tn("down_bwd_w", act, dx2b, d_ff // N_CHIPS, 1024, 2048)
    dgu, grads["ffn_conv_w"], grads["ffn_conv_b"] = _ffn_act_bwd(gu, dact, w["ffn_conv_w"], w["ffn_conv_b"])
    dn2 = _mm_up_bwd_x("up_bwd_x", dgu, w["w_up"], tm, 1024)
    grads["w_up"] = _mm_up_bwd_w("up_bwd_w", n2, dgu, 1024, 2048)
    dx1, dx1b, grads["norm_ffn_g"] = _rmsnorm_bwd("norm_ffn_bwd", x1, rstd2, w["norm_ffn_g"], dn2, dx2)
    dmix = _mm_nt("out_bwd_x", dx1b, w["w_out"], tm, 1024, d)
    grads["w_out"] = _mm_tn("out_bwd_w", mix, dx1b, 1024, 1024, 2048)
    token = comm.reduce_early(grads)
    dproj, dgate, grads["mlstm_norm_g"] = _mlstm_bwd(proj, gates_t, _behind(bias_row, token), bias_col,
                                                     w["mlstm_norm_g"], cprev, nprev, mprev, dmix)
    token = comm.reduce_early_mid(dproj)
    (dproj, grads["lru_conv_w"], grads["lru_conv_b"], grads["lru_wa"], grads["lru_wx"],
     grads["lru_ba"], grads["lru_bx"], grads["lru_lambda"]) = _lru_bwd(
        proj, hsave, dmix, dproj, w["lru_conv_w"], _behind(w["lru_conv_b"], token), w["lru_wa"], w["lru_wx"],
        w["lru_ba"], w["lru_bx"], w["lru_lambda"])
    gate_bias_grad, dproj = _gate_grads(dgate, dproj)
    grads["b_gate_m"] = gate_bias_grad[:, :2 * HEADS]
    grads["w_in"] = _mm_tn("proj_bwd_w", n1, dproj, 1024, tn_proj, 2048)
    token = comm.reduce_late(grads)
    dn1 = _mm_nt("proj_bwd_x", dproj, w["w_in"], tm, 512, _PROJ_PAD, after=token)
    token = comm.reduce_late_mid(dn1)
    grad_x, _, grads["norm_mix_g"] = _rmsnorm_bwd("norm_mix_bwd", x, rstd1, _behind(w["norm_mix_g"], token),
                                                  dn1, dx1)
    return loss, grad_x, grads


WEIGHT_NAMES = ("norm_mix_g", "w_in", "b_gate_m", "mlstm_norm_g", "lru_conv_w", "lru_conv_b", "lru_wa", "lru_ba",
                "lru_wx", "lru_bx", "lru_lambda", "w_out", "norm_ffn_g", "w_up", "ffn_conv_w", "ffn_conv_b",
                "w_down", "norm_final_g")
BIG = ("w_in", "w_out", "w_up", "w_down")
SMALL_SHARDED = ("mlstm_norm_g", "lru_conv_w", "ffn_conv_w")
SMALL = tuple(n for n in WEIGHT_NAMES if n not in BIG)
SMALL_REPLICATED = tuple(n for n in SMALL if n not in SMALL_SHARDED)


def _proj_segments():
    segs = [(0, 0, _QKVO), (_QKVO, _QKVO + 2 * D_LRU, _N_GATES)]
    for n in range(LRU_BLOCKS):
        segs.append((_QKVO + _N_GATES + n * LRU_BLOCK_DIM, _QKVO + 2 * n * LRU_BLOCK_DIM, LRU_BLOCK_DIM))
        segs.append((_QKVO + _N_GATES + D_LRU + n * LRU_BLOCK_DIM, _QKVO + (2 * n + 1) * LRU_BLOCK_DIM,
                     LRU_BLOCK_DIM))
    return segs


def _w_in_shards_to_local(shards):
    width = shards.shape[2]
    pieces = []
    for g0, _, n in sorted(_proj_segments(), key=lambda s: s[1]):
        at = g0
        while at < g0 + n:
            j = at // width
            stop = min(g0 + n, (j + 1) * width)
            pieces.append(shards[j][:, at - j * width:stop - j * width])
            at = stop
    pieces.append(jnp.zeros((shards.shape[1], PROJ_GATE_PAD - _N_GATES), shards.dtype))
    return jnp.concatenate(pieces, axis=1)


def _w_in_local_to_shards(w):
    width = _PROJ_COLS // N_CHIPS
    shards = []
    for j in range(N_CHIPS):
        pieces = []
        for g0, l0, n in sorted(_proj_segments()):
            lo, hi = max(g0, j * width), min(g0 + n, (j + 1) * width)
            if lo < hi:
                pieces.append(w[:, l0 + lo - g0:l0 + hi - g0])
        shards.append(jnp.concatenate(pieces, axis=1))
    return jnp.stack(shards)


def _w_in_to_global(w):
    sh = _w_in_local_to_shards(w)
    return jnp.concatenate([sh[j] for j in range(N_CHIPS)], axis=1)


def _size(shp):
    return functools.reduce(lambda a, b: a * b, shp, 1)


def _lane_dense(shp):
    return len(shp) >= 2 and shp[-1] == LANES and _size(shp) % (HALO * LANES) == 0


def _pack_rows(shapes):
    loose = sum(_size(shp) for shp in shapes if not _lane_dense(shp))
    return sum(_size(shp) // LANES for shp in shapes if _lane_dense(shp)) + -(-loose // (HALO * LANES)) * HALO


def _pack(arrs, rows):
    del rows
    parts = [a.reshape(-1, LANES).astype(F32) for a in arrs if _lane_dense(a.shape)]
    loose = [a.reshape(-1).astype(F32) for a in arrs if not _lane_dense(a.shape)]
    if loose:
        flat = jnp.concatenate(loose)
        n = -(-flat.shape[0] // (HALO * LANES)) * HALO * LANES
        parts.append(jnp.pad(flat, (0, n - flat.shape[0])).reshape(-1, LANES))
    return parts[0] if len(parts) == 1 else jnp.concatenate(parts, axis=0)


def _unpack(buf, shapes):
    out, row = {}, 0
    for i, shp in enumerate(shapes):
        if _lane_dense(shp):
            n = _size(shp) // LANES
            out[i] = buf[row:row + n].reshape(shp)
            row += n
    flat, at = buf[row:].reshape(-1), 0
    for i, shp in enumerate(shapes):
        if not _lane_dense(shp):
            out[i] = flat[at:at + _size(shp)].reshape(shp)
            at += _size(shp)
    return [out[i] for i in range(len(shapes))]


def _assemble_weights(g_in, g_out, g_up, g_down, small_sharded, replicated):
    w = dict(replicated)
    w["w_in"] = _w_in_shards_to_local(g_in)
    w["w_out"] = g_out.reshape(-1, g_out.shape[-1])
    w["w_up"] = g_up
    w["w_down"] = g_down.reshape(-1, g_down.shape[-1])
    for name, v in small_sharded.items():
        w[name] = jnp.concatenate([v[j] for j in range(N_CHIPS)], axis=1)
    return w


def _full_weights_from_global(weights):
    shard = lambda a, axis: jnp.stack(jnp.split(a, N_CHIPS, axis=axis))
    rep = {n: weights[n].reshape(1, -1) if weights[n].ndim <= 2 and n != "b_gate_m" else weights[n]
           for n in SMALL_REPLICATED}
    rep["b_gate_m"] = weights["b_gate_m"].reshape(1, -1)
    return _assemble_weights(shard(weights["w_in"], 1).astype(BF16), shard(weights["w_out"], 0).astype(BF16),
                             shard(weights["w_up"], 1).astype(BF16), shard(weights["w_down"], 0).astype(BF16),
                             {n: shard(weights[n], 1) for n in SMALL_SHARDED}, rep)


def _grads_to_global(grads):
    g = dict(grads)
    g["w_in"] = _w_in_to_global(grads["w_in"])
    g["w_up"] = jnp.concatenate([grads["w_up"][j] for j in range(N_CHIPS)], axis=1)
    return g


def _place():
    x, y, c = lax.axis_index("x"), lax.axis_index("y"), lax.axis_index("c")
    chips = [(1 - x, y), (x, 1 - y), (1 - x, 1 - y)]
    return x, y, c, 2 * x + y, chips


def _half_rows(n_rows, which):
    half = n_rows // 2
    return pl.ds(pl.multiple_of(which * half, 16), half)


def _rcopy(src, dst, send_sem, recv_sem, to):
    return pltpu.make_async_remote_copy(src_ref=src, dst_ref=dst, send_sem=send_sem, recv_sem=recv_sem,
                                        device_id=to, device_id_type=MESH)


HBM_SPEC = pl.BlockSpec(memory_space=pltpu.HBM)
SEM_SPEC = pl.BlockSpec(memory_space=pltpu.SEMAPHORE)
TOKEN_SHAPE = (8, LANES)


def _split_call(name, bufs, sems_in, sems_out_shapes, body_fn, after=None):
    nb, ni, no = len(bufs), len(sems_in), len(sems_out_shapes)
    after = [] if after is None else list(after) if isinstance(after, (list, tuple)) else [after]

    def body(*refs):
        buf_refs = refs[:nb]
        sem_in_refs = refs[nb:nb + ni]
        outs = refs[nb + ni + len(after):]
        sem_out_refs = outs[:no]
        token_ref = outs[no + nb]
        body_fn(buf_refs, sem_in_refs, sem_out_refs)
        token_ref[...] = jnp.zeros_like(token_ref)

    out_shape = ([pltpu.SemaphoreType.DMA(shp) for shp in sems_out_shapes]
                 + [pltpu.HBM(b.shape, b.dtype) for b in bufs] + [jax.ShapeDtypeStruct(TOKEN_SHAPE, F32)])
    res = pl.pallas_call(
        body, name=name, out_shape=out_shape,
        in_specs=[HBM_SPEC] * nb + [SEM_SPEC] * ni + [ANY] * len(after),
        out_specs=[SEM_SPEC] * no + [HBM_SPEC] * nb + [pl.BlockSpec(memory_space=pltpu.VMEM)],
        input_output_aliases={i: no + i for i in range(nb)},
        compiler_params=pltpu.CompilerParams(has_side_effects=pltpu.SideEffectType.DATAFLOW_SIDE_EFFECTING),
    )(*[pltpu.with_memory_space_constraint(b, pltpu.HBM) for b in bufs], *sems_in, *after)
    return list(res[:no]), list(res[no:no + nb]), res[no + nb]


def _place_own_shard(name, idx, shard, after=None):
    rows, cols = shard.shape
    tr = _row_tile(rows)

    def body(idx_ref, s_ref, *rest):
        rest[-1][...] = s_ref[...].astype(BF16)

    return pl.pallas_call(
        body, name=name, out_shape=jax.ShapeDtypeStruct((N_CHIPS, rows, cols), BF16),
        grid_spec=pltpu.PrefetchScalarGridSpec(
            num_scalar_prefetch=1, grid=(rows // tr,),
            in_specs=[pl.BlockSpec((tr, cols), lambda i, s: (i, 0))] + ([] if after is None else [ANY]),
            out_specs=pl.BlockSpec((None, tr, cols), lambda i, s: (s[1], i, 0))),
        compiler_params=_params(("parallel",)),
    )(idx, shard, *(() if after is None else (after,)))


GATHER_GROUPS = ((0, 4), (1,), (2,), (3,))


def _gather_start(name, lands, groups, after=None):
    members = [w for g in groups for w in GATHER_GROUPS[g]]

    def starts(bufs, _, sems):
        x, y, c, me, chips = _place()
        for gi, g in enumerate(groups):
            for pos, w in enumerate(GATHER_GROUPS[g]):
                buf = bufs[members.index(w)]
                part = buf.at[me] if w == 4 else buf.at[me, _half_rows(buf.shape[1], c)]
                for k, chip in enumerate(chips):
                    _rcopy(part, part, sems[2 * gi].at[3 * pos + k], sems[2 * gi + 1].at[3 * pos + k],
                           (*chip, c)).start()

    shapes = []
    for g in groups:
        shapes += [(3 * len(GATHER_GROUPS[g]),)] * 2
    sems, bufs, token = _split_call(name, [lands[w] for w in members], [], shapes, starts, after=after)
    return ({g: (sems[2 * gi], sems[2 * gi + 1]) for gi, g in enumerate(groups)},
            dict(zip(members, bufs)), token)


def _gather_mid(grp, lands, sems, after):
    members = GATHER_GROUPS[grp]
    big = [w for w in members if w != 4]

    def mid(bufs, sems_in, sems_out):
        x, y, c, me, chips = _place()
        send_sems, recv_sems = sems_in
        for pos, w in enumerate(members):
            for k, chip in enumerate(chips):
                cid = 2 * chip[0] + chip[1]
                buf = bufs[pos]
                mine = buf.at[me] if w == 4 else buf.at[me, _half_rows(buf.shape[1], c)]
                theirs = buf.at[cid] if w == 4 else buf.at[cid, _half_rows(buf.shape[1], c)]
                arrival = _rcopy(mine, theirs, send_sems.at[3 * pos + k], recv_sems.at[3 * pos + k], (*chip, c))
                arrival.wait_recv()
                arrival.wait_send()
                if w != 4:
                    _rcopy(theirs, theirs, sems_out[0].at[3 * big.index(w) + k],
                           sems_out[1].at[3 * big.index(w) + k], (x, y, 1 - c)).start()

    new_sems, bufs, token = _split_call(f"gather_mid_{grp}", [lands[w] for w in members], list(sems),
                                        [(3 * len(big),), (3 * len(big),)], mid, after=after)
    return new_sems, bufs, token


def _gather_end(grp, bufs, sems, after):
    members = GATHER_GROUPS[grp]
    big = [w for w in members if w != 4]

    def end(refs, sems_in, _):
        x, y, c, me, chips = _place()
        send_sems, recv_sems = sems_in
        for pos, w in enumerate(members):
            if w == 4:
                continue
            for k, chip in enumerate(chips):
                cid = 2 * chip[0] + chip[1]
                buf = refs[pos]
                sent = buf.at[cid, _half_rows(buf.shape[1], c)]
                landed = buf.at[cid, _half_rows(buf.shape[1], 1 - c)]
                fwd = _rcopy(sent, landed, send_sems.at[3 * big.index(w) + k], recv_sems.at[3 * big.index(w) + k],
                             (x, y, 1 - c))
                fwd.wait_recv()
                fwd.wait_send()

    _, bufs, token = _split_call(f"gather_end_{grp}", bufs, list(sems), [], end, after=after)
    return bufs, token


def _pair_start(name, grads, extra=None):
    n = len(grads)
    bufs = list(grads) + [lax.empty((g.shape[0], g.shape[1] // 2, g.shape[2]), g.dtype) for g in grads]
    if extra is not None:
        bufs += [extra, lax.empty(extra.shape, extra.dtype)]

    def starts(refs, _, sems):
        x, y, c, _, _ = _place()
        for w in range(n):
            other = _half_rows(refs[w].shape[1], 1 - c)
            _rcopy(refs[w].at[:, other], refs[n + w], sems[0].at[w], sems[1].at[w], (x, y, 1 - c)).start()
        if extra is not None:
            _rcopy(refs[2 * n], refs[2 * n + 1], sems[0].at[n], sems[1].at[n], (x, y, 1 - c)).start()

    count = n + (extra is not None)
    return _split_call(name, bufs, [], [(count,), (count,)], starts)


def _pair_wait(name, n, bufs, sems, after):
    has_extra = len(bufs) > 2 * n

    def waits(refs, sems_in, _):
        x, y, c, _, _ = _place()
        for w in range(n):
            other = _half_rows(refs[w].shape[1], 1 - c)
            cp = _rcopy(refs[w].at[:, other], refs[n + w], sems_in[0].at[w], sems_in[1].at[w], (x, y, 1 - c))
            cp.wait_recv()
            cp.wait_send()
        if has_extra:
            cp = _rcopy(refs[2 * n], refs[2 * n + 1], sems_in[0].at[n], sems_in[1].at[n], (x, y, 1 - c))
            cp.wait_recv()
            cp.wait_send()

    _, bufs, token = _split_call(name, bufs, list(sems), [], waits, after=after)
    return bufs, token


def _chip_start(name, partials, small=None):
    n = len(partials)
    bufs = list(partials) + [lax.empty(p.shape, p.dtype) for p in partials] + ([] if small is None else [small])

    def starts(refs, _, sems):
        _, _, c, me, chips = _place()
        for w in range(n):
            for k, chip in enumerate(chips):
                cid = 2 * chip[0] + chip[1]
                _rcopy(refs[w].at[cid], refs[n + w].at[me], sems[0].at[3 * w + k], sems[1].at[3 * w + k],
                       (*chip, c)).start()
        if small is not None:
            for k, chip in enumerate(chips):
                _rcopy(refs[2 * n].at[me], refs[2 * n].at[me], sems[0].at[3 * n + k], sems[1].at[3 * n + k],
                       (*chip, c)).start()

    count = 3 * (n + (small is not None))
    return _split_call(name, bufs, [], [(count,), (count,)], starts)


def _chip_wait(name, n, bufs, sems, after):
    has_small = len(bufs) > 2 * n

    def waits(refs, sems_in, _):
        _, _, c, me, chips = _place()
        for w in range(n):
            for k, chip in enumerate(chips):
                cid = 2 * chip[0] + chip[1]
                cp = _rcopy(refs[w].at[cid], refs[n + w].at[cid], sems_in[0].at[3 * w + k],
                            sems_in[1].at[3 * w + k], (*chip, c))
                cp.wait_recv()
                cp.wait_send()
        if has_small:
            for k, chip in enumerate(chips):
                cid = 2 * chip[0] + chip[1]
                cp = _rcopy(refs[2 * n].at[me], refs[2 * n].at[cid], sems_in[0].at[3 * n + k],
                            sems_in[1].at[3 * n + k], (*chip, c))
                cp.wait_recv()
                cp.wait_send()

    _, bufs, token = _split_call(name, bufs, list(sems), [], waits, after=after)
    return bufs, token


def _small_pair_sum(idx, own, recv):
    rows = own.shape[0]

    def body(idx_ref, a_ref, b_ref, o_ref):
        o_ref[...] = a_ref[...] + b_ref[...]

    blk = pl.BlockSpec((rows, LANES), lambda i, s: (0, 0))
    return pl.pallas_call(
        body, name="small_pair_sum", out_shape=jax.ShapeDtypeStruct((N_CHIPS, rows, LANES), F32),
        grid_spec=pltpu.PrefetchScalarGridSpec(
            num_scalar_prefetch=1, grid=(1,), in_specs=[blk, blk],
            out_specs=pl.BlockSpec((None, rows, LANES), lambda i, s: (s[1], 0, 0))),
        compiler_params=_params(("arbitrary",)),
    )(idx, own, recv)


def _gather_weights(shards, small):
    nb = len(shards)

    def body(*refs):
        srcs, small_ref = refs[:nb], refs[nb]
        dsts, small_out = refs[nb + 1:2 * nb + 1], refs[2 * nb + 1]
        send_sems, recv_sems, local_sems = refs[2 * nb + 2:]
        x, y, c, me, chips = _place()
        sibling = (x, y, 1 - c)
        mine = [_half_rows(s.shape[0], c) for s in srcs]
        other = [_half_rows(s.shape[0], 1 - c) for s in srcs]

        local = [pltpu.make_async_copy(srcs[w], dsts[w].at[me], local_sems.at[w]) for w in range(nb)]
        local.append(pltpu.make_async_copy(small_ref, small_out.at[me], local_sems.at[nb]))
        for cp in local:
            cp.start()
        sends = []
        for w in range(nb):
            for k, chip in enumerate(chips):
                sends.append(_rcopy(srcs[w].at[mine[w]], dsts[w].at[me, mine[w]],
                                    send_sems.at[w, k], recv_sems.at[w, k], (*chip, c)))
        for k, chip in enumerate(chips):
            sends.append(_rcopy(small_ref, small_out.at[me], send_sems.at[nb, k], recv_sems.at[nb, k], (*chip, c)))
        for cp in sends:
            cp.start()
        passed = []
        for w in range(nb):
            for k, chip in enumerate(chips):
                cid = 2 * chip[0] + chip[1]
                landed = dsts[w].at[cid, mine[w]]
                _rcopy(landed, landed, send_sems.at[w, k], recv_sems.at[w, k], (*chip, c)).wait_recv()
                fwd = _rcopy(landed, landed, send_sems.at[w, 3 + k], recv_sems.at[w, 3 + k], sibling)
                fwd.start()
                passed.append(fwd)
        for k, chip in enumerate(chips):
            cid = 2 * chip[0] + chip[1]
            _rcopy(small_ref, small_out.at[cid], send_sems.at[nb, k], recv_sems.at[nb, k], (*chip, c)).wait_recv()
        for w in range(nb):
            for k, chip in enumerate(chips):
                cid = 2 * chip[0] + chip[1]
                landed = dsts[w].at[cid, other[w]]
                _rcopy(landed, landed, send_sems.at[w, 3 + k], recv_sems.at[w, 3 + k], sibling).wait_recv()
        for cp in sends + passed:
            cp.wait_send()
        for cp in local:
            cp.wait()

    out_shape = [jax.ShapeDtypeStruct((N_CHIPS,) + s.shape, s.dtype) for s in shards]
    out_shape.append(jax.ShapeDtypeStruct((N_CHIPS,) + small.shape, small.dtype))
    return pl.pallas_call(
        body, name="gather_weights", out_shape=out_shape,
        in_specs=[ANY] * (nb + 1), out_specs=[ANY] * (nb + 1),
        scratch_shapes=[pltpu.SemaphoreType.DMA((nb + 1, 6)), pltpu.SemaphoreType.DMA((nb + 1, 6)),
                        pltpu.SemaphoreType.DMA((nb + 1,))],
    )(*shards, small)


def _pair_exchange(grads, small):
    nb = len(grads)

    def body(*refs):
        srcs, small_ref = refs[:nb], refs[nb]
        dsts, small_out = refs[nb + 1:2 * nb + 1], refs[2 * nb + 1]
        send_sems, recv_sems, small_send, small_recv, local_sem = refs[2 * nb + 2:]
        x, y, c, _, _ = _place()
        sibling = (x, y, 1 - c)
        my_id = 4 * x + 2 * y + c
        local = pltpu.make_async_copy(small_ref, small_out.at[my_id], local_sem)
        local.start()
        sends = []
        for w in range(nb):
            other = _half_rows(srcs[w].shape[1], 1 - c)
            sends.append(_rcopy(srcs[w].at[:, other], dsts[w], send_sems.at[w], recv_sems.at[w], sibling))
        for r in range(1, N_DEV):
            to = (1 - x if r & 4 else x, 1 - y if r & 2 else y, 1 - c if r & 1 else c)
            sends.append(_rcopy(small_ref, small_out.at[my_id], small_send.at[r - 1], small_recv.at[r - 1], to))
        for cp in sends:
            cp.start()
        for w in range(nb):
            _rcopy(dsts[w], dsts[w], send_sems.at[w], recv_sems.at[w], sibling).wait_recv()
        for r in range(1, N_DEV):
            frm = (1 - x if r & 4 else x, 1 - y if r & 2 else y, 1 - c if r & 1 else c)
            frm_id = 4 * frm[0] + 2 * frm[1] + frm[2]
            _rcopy(small_ref, small_out.at[frm_id], small_send.at[r - 1], small_recv.at[r - 1], frm).wait_recv()
        for cp in sends:
            cp.wait_send()
        local.wait()

    out_shape = [jax.ShapeDtypeStruct((g.shape[0], g.shape[1] // 2, g.shape[2]), g.dtype) for g in grads]
    out_shape.append(jax.ShapeDtypeStruct((N_DEV,) + small.shape, small.dtype))
    return pl.pallas_call(
        body, name="pair_exchange", out_shape=out_shape,
        in_specs=[ANY] * (nb + 1), out_specs=[ANY] * (nb + 1),
        scratch_shapes=[pltpu.SemaphoreType.DMA((nb,)), pltpu.SemaphoreType.DMA((nb,)),
                        pltpu.SemaphoreType.DMA((N_DEV - 1,)), pltpu.SemaphoreType.DMA((N_DEV - 1,)),
                        pltpu.SemaphoreType.DMA(())],
    )(*grads, small)


def _chip_exchange(partials):
    nb = len(partials)

    def body(*refs):
        srcs, dsts = refs[:nb], refs[nb:2 * nb]
        send_sems, recv_sems = refs[2 * nb:]
        _, _, c, me, chips = _place()
        sends = []
        for w in range(nb):
            for k, chip in enumerate(chips):
                cid = 2 * chip[0] + chip[1]
                sends.append(_rcopy(srcs[w].at[cid], dsts[w].at[me], send_sems.at[w, k], recv_sems.at[w, k],
                                    (*chip, c)))
        for cp in sends:
            cp.start()
        for w in range(nb):
            for k, chip in enumerate(chips):
                cid = 2 * chip[0] + chip[1]
                _rcopy(srcs[w].at[cid], dsts[w].at[cid], send_sems.at[w, k], recv_sems.at[w, k],
                       (*chip, c)).wait_recv()
        for cp in sends:
            cp.wait_send()

    return pl.pallas_call(
        body, name="chip_exchange", out_shape=[jax.ShapeDtypeStruct(p.shape, p.dtype) for p in partials],
        in_specs=[ANY] * nb, out_specs=[ANY] * nb,
        scratch_shapes=[pltpu.SemaphoreType.DMA((nb, 3)), pltpu.SemaphoreType.DMA((nb, 3))],
    )(*partials)


def _pair_share(name, shards, late=None):
    nb = len(shards)
    nl = 0 if late is None else 1

    def body(*refs):
        srcs = refs[:nb]
        dsts = refs[nb + nl:2 * nb + nl]
        send_sems, recv_sems = refs[2 * nb + 2 * nl:2 * nb + 2 * nl + 2]
        x, y, c, _, _ = _place()
        sibling = (x, y, 1 - c)
        sends = []
        for w in range(nb):
            mine = _half_rows(dsts[w].shape[0], c)
            sends.append(_rcopy(srcs[w].at[mine], dsts[w].at[mine], send_sems.at[w], recv_sems.at[w], sibling))
        if nl:
            late_ref, late_out = refs[nb], refs[2 * nb + 1]
            late_send, late_recv, local_sem = refs[2 * nb + 4:]
            my_id = 4 * x + 2 * y + c
            peer = lambda r: (1 - x if r & 4 else x, 1 - y if r & 2 else y, 1 - c if r & 1 else c)
            local = pltpu.make_async_copy(late_ref, late_out.at[my_id], local_sem)
            local.start()
            for r in range(1, N_DEV):
                sends.append(_rcopy(late_ref, late_out.at[my_id], late_send.at[r - 1], late_recv.at[r - 1],
                                    peer(r)))
        for cp in sends:
            cp.start()
        for w in range(nb):
            other = _half_rows(dsts[w].shape[0], 1 - c)
            _rcopy(srcs[w].at[other], dsts[w].at[other], send_sems.at[w], recv_sems.at[w], sibling).wait_recv()
        if nl:
            for r in range(1, N_DEV):
                frm = peer(r)
                _rcopy(late_ref, late_out.at[4 * frm[0] + 2 * frm[1] + frm[2]], late_send.at[r - 1],
                       late_recv.at[r - 1], frm).wait_recv()
        for cp in sends:
            cp.wait_send()
        if nl:
            local.wait()

    out_shape = [jax.ShapeDtypeStruct(h.shape, h.dtype) for h in shards]
    scratch = [pltpu.SemaphoreType.DMA((nb,)), pltpu.SemaphoreType.DMA((nb,))]
    if nl:
        out_shape.append(jax.ShapeDtypeStruct((N_DEV,) + late.shape, late.dtype))
        scratch += [pltpu.SemaphoreType.DMA((N_DEV - 1,)), pltpu.SemaphoreType.DMA((N_DEV - 1,)),
                    pltpu.SemaphoreType.DMA(())]
    return pl.pallas_call(
        body, name=name, out_shape=out_shape,
        in_specs=[ANY] * (nb + nl), out_specs=[ANY] * (nb + nl), scratch_shapes=scratch,
        input_output_aliases={w: w for w in range(nb)},
    )(*shards, *(() if late is None else (late,)))


def _row_tile(rows):
    return _pick(rows, 128, 64, 16, 8)


def _pair_sum(name, idx, grad, recv):
    n, half, cols = recv.shape
    tr = _row_tile(half)
    nrb = half // tr

    def body(idx_ref, g_ref, r_ref, o_ref):
        o_ref[...] = (g_ref[...] + r_ref[...]).astype(BF16)

    return pl.pallas_call(
        body, name=name, out_shape=jax.ShapeDtypeStruct(recv.shape, BF16),
        grid_spec=pltpu.PrefetchScalarGridSpec(
            num_scalar_prefetch=1, grid=(n - 1, nrb),
            in_specs=[pl.BlockSpec((None, tr, cols), lambda j, i, s: (s[2 + j], s[0] * nrb + i, 0)),
                      pl.BlockSpec((None, tr, cols), lambda j, i, s: (s[2 + j], i, 0))],
            out_specs=pl.BlockSpec((None, tr, cols), lambda j, i, s: (s[2 + j], i, 0))),
        compiler_params=_params(("parallel", "parallel")),
    )(idx, grad, recv)


def _final_sum(name, idx, grad, recv, chip_sums):
    _, half, cols = recv.shape
    tr = _row_tile(half)
    nrb = half // tr

    def body(idx_ref, g_ref, r_ref, p1_ref, p2_ref, p3_ref, o_ref):
        acc = g_ref[...] + r_ref[...]
        for p_ref in (p1_ref, p2_ref, p3_ref):
            acc = acc + p_ref[...].astype(F32)
        o_ref[...] = acc

    slot = lambda which: pl.BlockSpec((None, tr, cols), lambda i, s, which=which: (s[which], i, 0))
    return pl.pallas_call(
        body, name=name, out_shape=jax.ShapeDtypeStruct((2 * half, cols), F32),
        grid_spec=pltpu.PrefetchScalarGridSpec(
            num_scalar_prefetch=1, grid=(nrb,),
            in_specs=[pl.BlockSpec((None, tr, cols), lambda i, s: (s[1], s[0] * nrb + i, 0)),
                      slot(1), slot(2), slot(3), slot(4)],
            out_specs=pl.BlockSpec((tr, cols), lambda i, s: (s[0] * nrb + i, 0))),
        compiler_params=_params(("parallel",)),
    )(idx, grad, recv, chip_sums, chip_sums, chip_sums)


def _pair_sum_all(name, idx, grad, recv):
    _, half, cols = recv.shape
    tr = _row_tile(half)
    nrb = half // tr

    def body(idx_ref, g_ref, r_ref, o_ref):
        o_ref[...] = (g_ref[...] + r_ref[...]).astype(BF16)

    return pl.pallas_call(
        body, name=name, out_shape=jax.ShapeDtypeStruct((half, cols), BF16),
        grid_spec=pltpu.PrefetchScalarGridSpec(
            num_scalar_prefetch=1, grid=(nrb,),
            in_specs=[pl.BlockSpec((None, tr, cols), lambda i, s: (0, s[0] * nrb + i, 0)),
                      pl.BlockSpec((None, tr, cols), lambda i, s: (0, i, 0))],
            out_specs=pl.BlockSpec((tr, cols), lambda i, s: (i, 0))),
        compiler_params=_params(("parallel",)),
    )(idx, grad, recv)


def _final_sum_bf16(name, idx, partial, chip_sums):
    _, half, cols = partial.shape
    tr = _row_tile(half)
    nrb = half // tr

    def body(idx_ref, p0_ref, p1_ref, p2_ref, p3_ref, o_ref):
        acc = p0_ref[...].astype(F32)
        for p_ref in (p1_ref, p2_ref, p3_ref):
            acc = acc + p_ref[...].astype(F32)
        o_ref[...] = acc

    slot = lambda which: pl.BlockSpec((None, tr, cols), lambda i, s, which=which: (s[which], i, 0))
    return pl.pallas_call(
        body, name=name, out_shape=jax.ShapeDtypeStruct((2 * half, cols), F32),
        grid_spec=pltpu.PrefetchScalarGridSpec(
            num_scalar_prefetch=1, grid=(nrb,),
            in_specs=[slot(1), slot(2), slot(3), slot(4)],
            out_specs=pl.BlockSpec((tr, cols), lambda i, s: (s[0] * nrb + i, 0))),
        compiler_params=_params(("parallel",)),
    )(idx, partial, chip_sums, chip_sums, chip_sums)


def _small_sum(name, packs):
    n, rows, _ = packs.shape

    def body(p_ref, o_ref):
        acc = p_ref[0]
        for k in range(1, n):
            acc = acc + p_ref[k]
        o_ref[...] = acc

    return pl.pallas_call(
        body, name=name, out_shape=jax.ShapeDtypeStruct((rows, LANES), F32),
        in_specs=[pl.BlockSpec(memory_space=pltpu.VMEM)], out_specs=pl.BlockSpec(memory_space=pltpu.VMEM),
        compiler_params=pltpu.CompilerParams(vmem_limit_bytes=VMEM_LIMIT),
    )(packs)


def _adamw_math(w, g, m, v):
    m_new = ADAM_B1 * m + (1.0 - ADAM_B1) * g
    v_new = ADAM_B2 * v + (1.0 - ADAM_B2) * (g * g)
    m_hat = m_new / (1.0 - ADAM_B1 ** ADAM_STEP)
    v_hat = v_new / (1.0 - ADAM_B2 ** ADAM_STEP)
    return -ADAM_LR * (m_hat / (jnp.sqrt(v_hat) + ADAM_EPS) + ADAM_WD * w), m_new, v_new


def _adamw_many(name, ws, gs, ms, vs):
    n = len(ws)

    def body(*refs):
        for i in range(n):
            d, m_new, v_new = _adamw_math(refs[i][...], refs[n + i][...], refs[2 * n + i][...],
                                          refs[3 * n + i][...])
            refs[4 * n + i][...] = d
            refs[5 * n + i][...] = m_new
            refs[6 * n + i][...] = v_new

    vmem = pl.BlockSpec(memory_space=pltpu.VMEM)
    res = pl.pallas_call(
        body, name=name, in_specs=[vmem] * (4 * n), out_specs=[vmem] * (3 * n),
        out_shape=[jax.ShapeDtypeStruct(w.shape, F32) for w in ws] * 3,
        compiler_params=pltpu.CompilerParams(vmem_limit_bytes=VMEM_LIMIT),
    )(*ws, *gs, *ms, *vs)
    return res[:n], res[n:2 * n], res[2 * n:]


def _adamw(name, w, g, m, v):
    rows, cols = w.shape
    tr = rows if rows * cols * 4 <= (2 << 20) else _row_tile(rows)

    def body(w_ref, g_ref, m_ref, v_ref, g_out_ref, d_ref, nm_ref, nv_ref):
        gv = g_ref[...]
        g_out_ref[...] = gv
        d_ref[...], nm_ref[...], nv_ref[...] = _adamw_math(w_ref[...], gv, m_ref[...], v_ref[...])

    blk = pl.BlockSpec((tr, cols), lambda i: (i, 0))
    sds = jax.ShapeDtypeStruct((rows, cols), F32)
    return pl.pallas_call(
        body, name=name, grid=(rows // tr,), in_specs=[blk] * 4, out_specs=[blk] * 4, out_shape=[sds] * 4,
        compiler_params=_params(("parallel",)),
    )(w, g, m, v)


def _train_step(x, target, W, M, V):
    xi, yi, ci = lax.axis_index("x"), lax.axis_index("y"), lax.axis_index("c")
    me = 2 * xi + yi
    big = {n: W[n][0] for n in BIG}
    big_m = {n: M[n][0] for n in BIG}
    big_v = {n: V[n][0] for n in BIG}

    others = [jnp.where(jnp.int32(i) >= me, i + 1, i) for i in range(N_CHIPS - 1)]
    idx = jnp.stack([ci, me] + others).astype(jnp.int32)

    sharded_shapes = [W[n].shape[1:] for n in SMALL_SHARDED]
    small_pack = _pack([W[n][0] for n in SMALL_SHARDED], _pack_rows(sharded_shapes))
    small_land = lax.dynamic_update_slice(jnp.zeros((N_CHIPS,) + small_pack.shape, F32), small_pack[None],
                                          (me, 0, 0))
    replicated = {n: (W[n].reshape(1, -1) if W[n].ndim <= 2 else W[n][0]) for n in SMALL_REPLICATED}

    early = ("w_out", "w_up", "w_down")
    small_late = "norm_mix_g"
    small_early = tuple(n for n in SMALL if n != small_late)
    global_shape = lambda n: ((W[n].shape[1], W[n].shape[2] * N_CHIPS) if n in SMALL_SHARDED else
                              tuple(W[n].shape) if W[n].ndim == 1 else tuple(W[n].shape[1:]))
    small_shapes = [global_shape(n) for n in small_early]

    def shard_major(n, g):
        if n == "w_in":
            return _w_in_local_to_shards(g)
        return g if g.ndim == 3 else g.reshape((N_CHIPS, -1) + g.shape[1:])

    class _SplitComm:
        def reduce_early(self, grads):
            self.e_sems, self.e_bufs, token = _pair_start("pair_start_early",
                                                          [shard_major(n, grads[n]) for n in early])
            return token

        def reduce_early_mid(self, after):
            n = len(early)
            bufs, _ = _pair_wait("pair_wait_early", n, self.e_bufs, self.e_sems, after)
            self.e_grads, self.e_recv = bufs[:n], bufs[n:2 * n]
            partial = [_pair_sum(f"pair_sum_{nm}", idx, g, r) for nm, g, r in zip(early, self.e_grads, self.e_recv)]
            self.e_sems, self.e_bufs, token = _chip_start("chip_start_early", partial)
            return token

        def reduce_late(self, grads):
            pack = _pack([grads[n] for n in small_early], _pack_rows(small_shapes))
            self.l_sems, self.l_bufs, token = _pair_start("pair_start_late", [grads["w_in"][None]], extra=pack)
            return token

        def reduce_late_mid(self, after):
            bufs, _ = _pair_wait("pair_wait_late", 1, self.l_bufs, self.l_sems, after)
            partial = _w_in_local_to_shards(_pair_sum_all("pair_sum_w_in", idx, bufs[0], bufs[1]))
            self.l_sems, self.l_bufs, token = _chip_start("chip_start_late", [partial],
                                                          small=_small_pair_sum(idx, bufs[2], bufs[3]))
            return token

        def finish_early(self, after):
            n = len(early)
            bufs, _ = _chip_wait("chip_wait_early", n, self.e_bufs, self.e_sems, after)
            halves = [_final_sum(f"final_sum_{nm}", idx, g, r, p)
                      for nm, g, r, p in zip(early, self.e_grads, self.e_recv, bufs[n:2 * n])]
            return dict(zip(early, _pair_share("pair_share_early", halves)))

        def finish_late(self, after, late):
            bufs, _ = _chip_wait("chip_wait_late", 1, self.l_bufs, self.l_sems, after)
            half = _final_sum_bf16("final_sum_w_in", idx, bufs[0], bufs[1])
            small = dict(zip(small_early, _unpack(_small_sum("small_sum", bufs[2]), small_shapes)))
            whole, late_all = _pair_share("pair_share_late", [half], late)
            return whole, small, _small_sum("late_sum", late_all)

        def begin(self):
            first = {0: _place_own_shard("place_w_in", idx, big["w_in"]), 4: small_land}
            self.sems, self.lands, token = _gather_start("gather_start_0", first, (0,))
            rest = {i: _place_own_shard(f"place_{BIG[i]}", idx, big[BIG[i]], after=token) for i in (1, 2, 3)}
            sems, lands, token = _gather_start("gather_start_1", rest, (1, 2, 3), after=token)
            self.sems.update(sems)
            self.lands.update(lands)
            return token

        def mid(self, grp, after):
            if grp == 0:
                after = [after, big_m["w_in"], big_v["w_in"]]
            self.pending = _gather_mid(grp, self.lands, self.sems[grp], after)
            return self.pending[2]

        def end(self, grp, after):
            sems, bufs, _ = self.pending
            bufs, _ = _gather_end(grp, bufs, sems, after)
            if grp == 0:
                per_chip = [_unpack(bufs[1][j], sharded_shapes) for j in range(N_CHIPS)]
                out = {n: jnp.concatenate([per_chip[j][i] for j in range(N_CHIPS)], axis=1)
                       for i, n in enumerate(SMALL_SHARDED)}
                out["w_in"] = _w_in_shards_to_local(bufs[0])
                return out
            if grp == 2:
                return {"w_up": bufs[0]}
            return {("w_out" if grp == 1 else "w_down"): bufs[0].reshape(-1, bufs[0].shape[-1])}

    comm = _SplitComm()
    loss, grad_x, grads = _local_step(x[0], target[0], replicated, comm)
    loss = lax.psum(loss[0, 0], ("x", "y", "c"))
    out_g, out_d, out_m, out_v = {}, {}, {}, {}

    def update_big(n, grad):
        g, d, nm, nv = _adamw(f"adamw_{n}", big[n], grad, big_m[n], big_v[n])
        out_g[n], out_d[n], out_m[n], out_v[n] = g[None], d[None], nm[None], nv[None]
        return d

    early_grads = comm.finish_early(grad_x)
    for n in early:
        last = update_big(n, early_grads[n])
    late = _pack([grads[small_late]], _pack_rows([global_shape(small_late)]))
    w_in_grad, small_grads, late_sum = comm.finish_late(last, late)
    update_big("w_in", w_in_grad)
    small_grads[small_late] = _unpack(late_sum, [global_shape(small_late)])[0]
    for n in SMALL_SHARDED:
        width = W[n].shape[2]
        small_grads[n] = lax.dynamic_slice_in_dim(small_grads[n], me * width, width, axis=1)

    for n in SMALL:
        out_g[n] = small_grads[n].reshape(W[n].shape)
    two_d = lambda a: a.reshape(1, -1) if a.ndim == 1 else a
    results = _adamw_many("adamw_small", *[[two_d(src[n]) for n in SMALL] for src in (W, out_g, M, V)])
    for dst, arrs in zip((out_d, out_m, out_v), results):
        dst.update({n: a.reshape(W[n].shape) for n, a in zip(SMALL, arrs)})
    return (loss, grad_x[None], *[out_g[n] for n in WEIGHT_NAMES], *[out_d[n] for n in WEIGHT_NAMES],
            *[out_m[n] for n in WEIGHT_NAMES], *[out_v[n] for n in WEIGHT_NAMES])


def kernel(x, norm_mix_g, w_in, b_gate_m, mlstm_norm_g, lru_conv_w, lru_conv_b, lru_wa, lru_ba, lru_wx, lru_bx, lru_lambda, w_out, norm_ffn_g, w_up, ffn_conv_w, ffn_conv_b, w_down, norm_final_g, loss_target, m_norm_mix_g, m_w_in, m_b_gate_m, m_mlstm_norm_g, m_lru_conv_w, m_lru_conv_b, m_lru_wa, m_lru_ba, m_lru_wx, m_lru_bx, m_lru_lambda, m_w_out, m_norm_ffn_g, m_w_up, m_ffn_conv_w, m_ffn_conv_b, m_w_down, m_norm_final_g, v_norm_mix_g, v_w_in, v_b_gate_m, v_mlstm_norm_g, v_lru_conv_w, v_lru_conv_b, v_lru_wa, v_lru_ba, v_lru_wx, v_lru_bx, v_lru_lambda, v_w_out, v_norm_ffn_g, v_w_up, v_ffn_conv_w, v_ffn_conv_b, v_w_down, v_norm_final_g):
    W = dict(zip(WEIGHT_NAMES, (norm_mix_g, w_in, b_gate_m, mlstm_norm_g, lru_conv_w, lru_conv_b, lru_wa, lru_ba,
                                lru_wx, lru_bx, lru_lambda, w_out, norm_ffn_g, w_up, ffn_conv_w, ffn_conv_b,
                                w_down, norm_final_g)))
    M = dict(zip(WEIGHT_NAMES, (m_norm_mix_g, m_w_in, m_b_gate_m, m_mlstm_norm_g, m_lru_conv_w, m_lru_conv_b,
                                m_lru_wa, m_lru_ba, m_lru_wx, m_lru_bx, m_lru_lambda, m_w_out, m_norm_ffn_g,
                                m_w_up, m_ffn_conv_w, m_ffn_conv_b, m_w_down, m_norm_final_g)))
    V = dict(zip(WEIGHT_NAMES, (v_norm_mix_g, v_w_in, v_b_gate_m, v_mlstm_norm_g, v_lru_conv_w, v_lru_conv_b,
                                v_lru_wa, v_lru_ba, v_lru_wx, v_lru_bx, v_lru_lambda, v_w_out, v_norm_ffn_g,
                                v_w_up, v_ffn_conv_w, v_ffn_conv_b, v_w_down, v_norm_final_g)))
    return _train_step(x, loss_target, W, M, V)
```

```python
import functools

import jax
import jax.numpy as jnp
from jax import lax
from jax.experimental import pallas as pl
from jax.experimental.pallas import tpu as pltpu

F32 = jnp.float32
BF16 = jnp.bfloat16
MESH = pl.DeviceIdType.MESH

EPS = 1e-6
CHUNK = 512
HEADS = 4
HEAD_DIM = 256
D_MLSTM = HEADS * HEAD_DIM
LRU_BLOCKS = 8
LRU_BLOCK_DIM = 128
D_LRU = LRU_BLOCKS * LRU_BLOCK_DIM
LRU_C = 8.0
LRU_CONV = 4
FFN_CONV = 3
ADAM_LR = 0.001
ADAM_B1 = 0.9
ADAM_B2 = 0.999
ADAM_EPS = 1e-08
ADAM_WD = 0.01
ADAM_STEP = 10

N_CHIPS = 4
N_DEV = 8
LANES = 128
HALO = 8
PROJ_GATE_PAD = LANES
_QKVO = 4 * D_MLSTM
_N_GATES = 2 * HEADS
_PROJ_COLS = _QKVO + _N_GATES + 2 * D_LRU
_PROJ_PAD = _QKVO + 2 * D_LRU + PROJ_GATE_PAD
VMEM_LIMIT = 48 * 1024 * 1024
ANY = pl.BlockSpec(memory_space=pl.ANY)


def _params(sem, vmem=VMEM_LIMIT):
    return pltpu.CompilerParams(dimension_semantics=sem, vmem_limit_bytes=vmem)


def _matmul(name, a, b, grid, a_spec, b_spec, o_spec, out_sds, contract, res=None, res_spec=None, after=None):
    nk = grid[2]
    acc_shape = tuple(d for d in o_spec.block_shape if d is not None)

    def body(*refs):
        refs = list(refs)
        a_ref, b_ref = refs[:2]
        r_ref = refs[2] if res is not None else None
        o_ref = refs[-1] if nk == 1 else refs[-2]
        acc_ref = None if nk == 1 else refs[-1]
        k = pl.program_id(2)

        def part():
            return lax.dot_general(a_ref[...], b_ref[...], (contract, ((), ())), preferred_element_type=F32)

        def finish(r):
            if r_ref is not None:
                r = r_ref[...] + r
            o_ref[...] = r.astype(o_ref.dtype)

        if nk == 1:
            finish(part())
            return

        @pl.when(k == 0)
        def _():
            acc_ref[...] = part()

        @pl.when(jnp.logical_and(k > 0, k < nk - 1))
        def _():
            acc_ref[...] += part()

        @pl.when(k == nk - 1)
        def _():
            finish(acc_ref[...] + part())

    in_specs = [a_spec, b_spec] + ([] if res is None else [res_spec]) + ([] if after is None else [ANY])
    args = (a, b) + (() if res is None else (res,)) + (() if after is None else (after,))
    if after is not None:
        inner = body
        body = lambda *refs: inner(*refs[:len(in_specs) - 1], *refs[len(in_specs):])
    return pl.pallas_call(
        body, out_shape=out_sds, grid=grid, in_specs=in_specs, out_specs=o_spec,
        scratch_shapes=[] if nk == 1 else [pltpu.VMEM(acc_shape, F32)], name=name,
        compiler_params=_params(("parallel", "parallel", "arbitrary")),
    )(*args)


NN = ((1,), (0,))
NT = ((1,), (1,))
TN = ((0,), (0,))


def _mm_nn(name, a, b, tm, tn, tk, out_dtype=F32, res=None):
    m, k = a.shape
    n = b.shape[1]
    return _matmul(name, a, b, (m // tm, n // tn, k // tk),
                   pl.BlockSpec((tm, tk), lambda i, j, kk: (i, kk)),
                   pl.BlockSpec((tk, tn), lambda i, j, kk: (kk, j)),
                   pl.BlockSpec((tm, tn), lambda i, j, kk: (i, j)),
                   jax.ShapeDtypeStruct((m, n), out_dtype), NN,
                   res=res, res_spec=pl.BlockSpec((tm, tn), lambda i, j, kk: (i, j)))


def _mm_nt(name, a, b, tm, tn, tk, out_dtype=F32, res=None, after=None):
    m, k = a.shape
    n = b.shape[0]
    return _matmul(name, a, b, (m // tm, n // tn, k // tk),
                   pl.BlockSpec((tm, tk), lambda i, j, kk: (i, kk)),
                   pl.BlockSpec((tn, tk), lambda i, j, kk: (j, kk)),
                   pl.BlockSpec((tm, tn), lambda i, j, kk: (i, j)),
                   jax.ShapeDtypeStruct((m, n), out_dtype), NT,
                   res=res, res_spec=pl.BlockSpec((tm, tn), lambda i, j, kk: (i, j)), after=after)


def _mm_tn(name, a, b, tm, tn, tk, out_dtype=F32):
    k, m = a.shape
    n = b.shape[1]
    tk = min(tk, k)
    return _matmul(name, a, b, (m // tm, n // tn, k // tk),
                   pl.BlockSpec((tk, tm), lambda i, j, kk: (kk, i)),
                   pl.BlockSpec((tk, tn), lambda i, j, kk: (kk, j)),
                   pl.BlockSpec((tm, tn), lambda i, j, kk: (i, j)),
                   jax.ShapeDtypeStruct((m, n), out_dtype), TN)


def _up_shard(n):
    return 2 * (n % 2) + (n // 2) // 2, (n // 2) % 2


def _mm_up_fwd(name, a, wg_up, tm, tk):
    m, k = a.shape
    _, _, cols = wg_up.shape
    tn = cols // 2
    return _matmul(name, a, wg_up, (m // tm, 2 * N_CHIPS, k // tk),
                   pl.BlockSpec((tm, tk), lambda i, j, kk: (i, kk)),
                   pl.BlockSpec((None, tk, tn), lambda i, j, kk: (_up_shard(j)[0], kk, _up_shard(j)[1])),
                   pl.BlockSpec((tm, tn), lambda i, j, kk: (i, j)),
                   jax.ShapeDtypeStruct((m, 2 * N_CHIPS * tn), F32), NN)


def _mm_up_bwd_x(name, dgu, wg_up, tm, tn):
    m, _ = dgu.shape
    _, d, cols = wg_up.shape
    tk = cols // 2
    nk = N_CHIPS

    def body(a_ref, bg_ref, bu_ref, o_ref, acc_ref):
        k = pl.program_id(2)

        def part():
            dims = (NT, ((), ()))
            return (lax.dot_general(a_ref[:, :tk], bg_ref[...], dims, preferred_element_type=F32)
                    + lax.dot_general(a_ref[:, tk:], bu_ref[...], dims, preferred_element_type=F32))

        @pl.when(k == 0)
        def _():
            acc_ref[...] = part()

        @pl.when(jnp.logical_and(k > 0, k < nk - 1))
        def _():
            acc_ref[...] += part()

        @pl.when(k == nk - 1)
        def _():
            o_ref[...] = acc_ref[...] + part()

    wspec = lambda half: pl.BlockSpec(
        (None, tn, tk), lambda i, j, kk: (_up_shard(2 * kk + half)[0], j, _up_shard(2 * kk + half)[1]))
    return pl.pallas_call(
        body, name=name, grid=(m // tm, d // tn, nk), out_shape=jax.ShapeDtypeStruct((m, d), F32),
        in_specs=[pl.BlockSpec((tm, 2 * tk), lambda i, j, kk: (i, kk)), wspec(0), wspec(1)],
        out_specs=pl.BlockSpec((tm, tn), lambda i, j, kk: (i, j)),
        scratch_shapes=[pltpu.VMEM((tm, tn), F32)],
        compiler_params=_params(("parallel", "parallel", "arbitrary")),
    )(dgu, wg_up, wg_up)


def _mm_up_bwd_w(name, n2, dgu, tm, tk):
    s, d = n2.shape
    tk = min(tk, s)
    tn = dgu.shape[1] // (2 * N_CHIPS)
    return _matmul(name, n2, dgu, (d // tm, 2 * N_CHIPS, s // tk),
                   pl.BlockSpec((tk, tm), lambda i, j, kk: (kk, i)),
                   pl.BlockSpec((tk, tn), lambda i, j, kk: (kk, j)),
                   pl.BlockSpec((None, tm, tn), lambda i, j, kk: (_up_shard(j)[0], i, _up_shard(j)[1])),
                   jax.ShapeDtypeStruct((N_CHIPS, d, 2 * tn), F32), TN)


def _rmsnorm_fwd(name, x, g, tm=256):
    s, d = x.shape

    def body(x_ref, g_ref, n_ref, r_ref):
        xf = x_ref[...]
        r = lax.rsqrt(jnp.mean(xf * xf, axis=-1, keepdims=True) + EPS)
        n_ref[...] = ((xf * r) * g_ref[...]).astype(BF16)
        r_ref[...] = r

    return pl.pallas_call(
        body, grid=(s // tm,), name=name,
        in_specs=[pl.BlockSpec((tm, d), lambda i: (i, 0)), pl.BlockSpec((1, d), lambda i: (0, 0))],
        out_specs=[pl.BlockSpec((tm, d), lambda i: (i, 0)), pl.BlockSpec((tm, 1), lambda i: (i, 0))],
        out_shape=[jax.ShapeDtypeStruct((s, d), BF16), jax.ShapeDtypeStruct((s, 1), F32)],
        compiler_params=_params(("parallel",)),
    )(x, g)


def _rmsnorm_bwd(name, x, rstd, g, dn, dres, tm=256):
    s, d = x.shape

    def body(x_ref, r_ref, g_ref, dn_ref, dres_ref, dx_ref, dxb_ref, dg_ref):
        @pl.when(pl.program_id(0) == 0)
        def _():
            dg_ref[...] = jnp.zeros_like(dg_ref)

        r = r_ref[...]
        xhat = x_ref[...] * r
        dn_v = dn_ref[...]
        dxhat = dn_v * g_ref[...]
        dx = dres_ref[...] + r * (dxhat - xhat * jnp.mean(dxhat * xhat, axis=-1, keepdims=True))
        dx_ref[...] = dx
        dxb_ref[...] = dx.astype(BF16)
        dg_ref[...] += jnp.sum(dn_v * xhat, axis=0, keepdims=True)

    row = pl.BlockSpec((tm, d), lambda i: (i, 0))
    vec = pl.BlockSpec((1, d), lambda i: (0, 0))
    return pl.pallas_call(
        body, grid=(s // tm,), name=name,
        in_specs=[row, pl.BlockSpec((tm, 1), lambda i: (i, 0)), vec, row, row],
        out_specs=[row, row, vec],
        out_shape=[jax.ShapeDtypeStruct((s, d), F32), jax.ShapeDtypeStruct((s, d), BF16),
                   jax.ShapeDtypeStruct((1, d), F32)],
        compiler_params=_params(("arbitrary",)),
    )(x, rstd, g, dn, dres)


def _loss_head(name, x, g, target, tm=256):
    s, d = x.shape

    def body(x_ref, g_ref, t_ref, loss_ref, dx_ref, dxb_ref, dg_ref):
        @pl.when(pl.program_id(0) == 0)
        def _():
            dg_ref[...] = jnp.zeros_like(dg_ref)
            loss_ref[...] = jnp.zeros_like(loss_ref)

        xf = x_ref[...]
        gv = g_ref[...]
        r = lax.rsqrt(jnp.mean(xf * xf, axis=-1, keepdims=True) + EPS)
        xhat = xf * r
        err = xhat * gv - t_ref[...]
        loss_ref[...] += 0.5 * jnp.sum(jnp.mean(err * err, axis=-1, keepdims=True), axis=0, keepdims=True)
        dy = err * (1.0 / d)
        dxhat = dy * gv
        dx = r * (dxhat - xhat * jnp.mean(dxhat * xhat, axis=-1, keepdims=True))
        dx_ref[...] = dx
        dxb_ref[...] = dx.astype(BF16)
        dg_ref[...] += jnp.sum(dy * xhat, axis=0, keepdims=True)

    row = pl.BlockSpec((tm, d), lambda i: (i, 0))
    vec = pl.BlockSpec((1, d), lambda i: (0, 0))
    return pl.pallas_call(
        body, grid=(s // tm,), name=name,
        in_specs=[row, vec, row],
        out_specs=[pl.BlockSpec((1, 1), lambda i: (0, 0)), row, row, vec],
        out_shape=[jax.ShapeDtypeStruct((1, 1), F32), jax.ShapeDtypeStruct((s, d), F32),
                   jax.ShapeDtypeStruct((s, d), BF16), jax.ShapeDtypeStruct((1, d), F32)],
        compiler_params=_params(("arbitrary",)),
    )(x, g, target)


def _sigmoid(v):
    return 1.0 / (1.0 + jnp.exp(-v))


def _log_sigmoid(v):
    return jnp.minimum(v, 0.0) - jnp.log1p(jnp.exp(-jnp.abs(v)))


def _softplus(v):
    return jnp.maximum(v, 0.0) + jnp.log1p(jnp.exp(-jnp.abs(v)))


def _one_minus_exp(z):
    series = -z * (1.0 + z * (0.5 + z * (1.0 / 6.0 + z * (1.0 / 24.0 + z * (1.0 / 120.0)))))
    return jnp.where(z > -0.1, series, 1.0 - jnp.exp(z))


_GELU_K = 0.7978845608028654
_GELU_C = 0.044715


def _gelu(v):
    return 0.5 * v * (1.0 + jnp.tanh(_GELU_K * (v + _GELU_C * v * v * v)))


def _gelu_grad(v):
    t = jnp.tanh(_GELU_K * (v + _GELU_C * v * v * v))
    return 0.5 * (1.0 + t) + 0.5 * v * (1.0 - t * t) * _GELU_K * (1.0 + 3.0 * _GELU_C * v * v)


def _rows(shape):
    return lax.broadcasted_iota(jnp.int32, shape, 0)


def _cols(shape):
    return lax.broadcasted_iota(jnp.int32, shape, 1)


def _shift_down(v, prev, d):
    if d == 0:
        return v
    rolled = pltpu.roll(v, d, axis=0)
    head = jnp.where(_rows((HALO, v.shape[1])) >= d, rolled[:HALO], pltpu.roll(prev, d, axis=0))
    if v.shape[0] == HALO:
        return head
    return jnp.concatenate([head, rolled[HALO:]], axis=0)


def _shift_up(v, nxt, d):
    if d == 0:
        return v
    n = v.shape[0]
    rolled = pltpu.roll(v, n - d, axis=0)
    tail = jnp.where(_rows((HALO, v.shape[1])) < HALO - d, rolled[n - HALO:], pltpu.roll(nxt, HALO - d, axis=0))
    if n == HALO:
        return tail
    return jnp.concatenate([rolled[:n - HALO], tail], axis=0)


def _dot(a, b, contract):
    return lax.dot_general(a.astype(BF16), b.astype(BF16), (contract, ((), ())), preferred_element_type=F32)


def _mlstm_chunk_common(h, q_ref, k_ref, v_ref, gcol_ref, grow_ref, brow_ref, bcol_ref, m_prev):
    L = CHUNK
    sl = slice(h * HEAD_DIM, (h + 1) * HEAD_DIM)
    qh = q_ref[:, sl]
    kh = k_ref[:, sl]
    vh = v_ref[:, sl]
    qs = qh * (HEAD_DIM ** -0.5)
    gates = gcol_ref[...] + brow_ref[...]
    lane = _cols(gates.shape)
    ic = jnp.sum(jnp.where(lane == h, gates, 0.0), axis=1, keepdims=True)
    fc = jnp.sum(jnp.where(lane == HEADS + h, gates, 0.0), axis=1, keepdims=True)
    ir = grow_ref[h:h + 1, :] + bcol_ref[h:h + 1, :]
    fr = grow_ref[HEADS + h:HEADS + h + 1, :] + bcol_ref[HEADS + h:HEADS + h + 1, :]
    logf_c = _log_sigmoid(fc)
    logf_r = _log_sigmoid(fr)
    t_i = _rows((L, L))
    s_i = _cols((L, L))
    tri = t_i >= s_i
    b_c = jnp.sum(jnp.where(tri, logf_r, 0.0), axis=1, keepdims=True)
    b_r = jnp.sum(jnp.where(t_i <= s_i, logf_c, 0.0), axis=0, keepdims=True)
    btot = jnp.sum(logf_r, axis=1, keepdims=True)
    dmat = jnp.where(tri, b_c - b_r + ir, -jnp.inf)
    m_inter = b_c + m_prev
    m_t = jnp.maximum(m_inter, jnp.max(dmat, axis=1, keepdims=True))
    e_mat = jnp.exp(dmat - m_t)
    e_inter = jnp.exp(m_inter - m_t)
    wqk = _dot(qs, kh, NT) * e_mat
    w_end_r = btot - b_r + ir
    m_loc = jnp.max(w_end_r, axis=1, keepdims=True)
    e_end_c = jnp.exp(btot - b_c + ic - m_loc)
    m_new = jnp.maximum(btot + m_prev, m_loc)
    a_dec = jnp.exp(btot + m_prev - m_new)
    c_inj = jnp.exp(m_loc - m_new)
    return dict(qh=qh, kh=kh, vh=vh, qs=qs, fc=fc, tri=tri, t_i=t_i, s_i=s_i, m_t=m_t, e_mat=e_mat,
                e_inter=e_inter, wqk=wqk, e_end_c=e_end_c, m_new=m_new, a_dec=a_dec, c_inj=c_inj)


def _mlstm_fwd(proj, gates_t, bias_row, bias_col, head_g):
    s = proj.shape[0]
    nc = s // CHUNK
    L = CHUNK

    def body(q_ref, k_ref, v_ref, o_ref, gcol_ref, grow_ref, brow_ref, bcol_ref, hg_ref,
             out_ref, cprev_ref, nprev_ref, mprev_ref, c_scr, n_scr, m_scr):
        @pl.when(pl.program_id(0) == 0)
        def _():
            c_scr[...] = jnp.zeros_like(c_scr)
            n_scr[...] = jnp.zeros_like(n_scr)
            m_scr[...] = jnp.zeros_like(m_scr)

        for h in range(HEADS):
            sl = slice(h * HEAD_DIM, (h + 1) * HEAD_DIM)
            m_prev = m_scr[h:h + 1, 0:1]
            n_prev = n_scr[h:h + 1, :]
            c_prev = c_scr[h].astype(BF16)
            q = _mlstm_chunk_common(h, q_ref, k_ref, v_ref, gcol_ref, grow_ref, brow_ref, bcol_ref, m_prev)
            num = _dot(q["wqk"], q["vh"], NN) + q["e_inter"] * _dot(q["qs"], c_prev, NN)
            den = (jnp.sum(q["wqk"], axis=1, keepdims=True)
                   + q["e_inter"] * jnp.sum(q["qs"] * n_prev, axis=1, keepdims=True))
            hh = num / jnp.maximum(jnp.abs(den), jnp.exp(-q["m_t"]))
            hn = hh * lax.rsqrt(jnp.mean(hh * hh, axis=1, keepdims=True) + EPS) * hg_ref[h:h + 1, :]
            out_ref[:, sl] = (_sigmoid(o_ref[:, sl]) * hn).astype(BF16)
            cprev_ref[h] = c_prev
            nprev_ref[h:h + 1, :] = n_prev
            mprev_ref[h:h + 1, :] = jnp.broadcast_to(m_prev, (1, LANES))
            c_loc = _dot(q["kh"], q["e_end_c"] * q["vh"], TN)
            n_loc = jnp.sum(q["e_end_c"] * q["kh"], axis=0, keepdims=True)
            c_scr[h] = q["a_dec"] * c_scr[h] + q["c_inj"] * c_loc
            n_scr[h:h + 1, :] = q["a_dec"] * n_prev + q["c_inj"] * n_loc
            m_scr[h:h + 1, :] = jnp.broadcast_to(q["m_new"], (1, LANES))

    blk = lambda j: pl.BlockSpec((L, D_MLSTM), lambda c, j=j: (c, j))
    full = lambda shp: pl.BlockSpec(shp, lambda c: tuple(0 for _ in shp))
    return pl.pallas_call(
        body, grid=(nc,), name="mlstm_fwd",
        in_specs=[blk(0), blk(1), blk(2), blk(3),
                  pl.BlockSpec((L, LANES), lambda c: (c, (4 * D_MLSTM + 2 * D_LRU) // LANES)),
                  pl.BlockSpec((None, 2 * HEADS, L), lambda c: (c, 0, 0)),
                  full((1, LANES)), full((2 * HEADS, 1)), full((HEADS, HEAD_DIM))],
        out_specs=[pl.BlockSpec((L, D_MLSTM), lambda c: (c, 0)),
                   pl.BlockSpec((None, HEADS, HEAD_DIM, HEAD_DIM), lambda c: (c, 0, 0, 0)),
                   pl.BlockSpec((None, HEADS, HEAD_DIM), lambda c: (c, 0, 0)),
                   pl.BlockSpec((None, HEADS, LANES), lambda c: (c, 0, 0))],
        out_shape=[jax.ShapeDtypeStruct((s, D_MLSTM + D_LRU), BF16),
                   jax.ShapeDtypeStruct((nc, HEADS, HEAD_DIM, HEAD_DIM), BF16),
                   jax.ShapeDtypeStruct((nc, HEADS, HEAD_DIM), F32),
                   jax.ShapeDtypeStruct((nc, HEADS, LANES), F32)],
        scratch_shapes=[pltpu.VMEM((HEADS, HEAD_DIM, HEAD_DIM), F32), pltpu.VMEM((HEADS, HEAD_DIM), F32),
                        pltpu.VMEM((HEADS, LANES), F32)],
        compiler_params=_params(("arbitrary",)),
    )(proj, proj, proj, proj, proj, gates_t, bias_row, bias_col, head_g)


def _mlstm_bwd(proj, gates_t, bias_row, bias_col, head_g, cprev, nprev, mprev, dmix):
    s = proj.shape[0]
    nc = s // CHUNK
    L = CHUNK

    def body(q_ref, k_ref, v_ref, o_ref, gcol_ref, grow_ref, brow_ref, bcol_ref, hg_ref,
             cprev_ref, nprev_ref, mprev_ref, dmix_ref,
             dqkvo_ref, dgate_ref, dhg_ref, g_scr, gn_scr):
        @pl.when(pl.program_id(0) == 0)
        def _():
            g_scr[...] = jnp.zeros_like(g_scr)
            gn_scr[...] = jnp.zeros_like(gn_scr)
            dhg_ref[...] = jnp.zeros_like(dhg_ref)

        lane = _cols((L, LANES))
        dgate = jnp.zeros((L, LANES), F32)
        for h in range(HEADS):
            sl = slice(h * HEAD_DIM, (h + 1) * HEAD_DIM)
            m_prev = mprev_ref[h:h + 1, 0:1]
            n_prev = nprev_ref[h:h + 1, :]
            c_prev = cprev_ref[h]
            q = _mlstm_chunk_common(h, q_ref, k_ref, v_ref, gcol_ref, grow_ref, brow_ref, bcol_ref, m_prev)
            qh, kh, vh, qs, wqk, e_inter = q["qh"], q["kh"], q["vh"], q["qs"], q["wqk"], q["e_inter"]
            num_state = e_inter * _dot(qs, c_prev, NN)
            den_state = e_inter * jnp.sum(qs * n_prev, axis=1, keepdims=True)
            num = _dot(wqk, vh, NN) + num_state
            den = jnp.sum(wqk, axis=1, keepdims=True) + den_state
            floor = jnp.exp(-q["m_t"])
            denom = jnp.maximum(jnp.abs(den), floor)
            hh = num / denom
            rn = lax.rsqrt(jnp.mean(hh * hh, axis=1, keepdims=True) + EPS)
            hn_pre = hh * rn
            hg = hg_ref[h:h + 1, :]
            sg = _sigmoid(o_ref[:, sl])
            dout = dmix_ref[:, sl]
            d_o = dout * (hn_pre * hg) * sg * (1.0 - sg)
            dhn = dout * sg
            dhg_ref[h:h + 1, :] += jnp.sum(dhn * hn_pre, axis=0, keepdims=True)
            dhn_pre = dhn * hg
            dhh = rn * (dhn_pre - hn_pre * jnp.mean(dhn_pre * hn_pre, axis=1, keepdims=True))
            dnum = dhh / denom
            dden = jnp.where(jnp.abs(den) >= floor,
                             -jnp.sum(hh * dhh, axis=1, keepdims=True) / denom * jnp.sign(den), 0.0)
            dwqk = _dot(dnum, vh, NT) + dden
            dv = _dot(wqk, dnum, TN)
            dp = dwqk * q["e_mat"]
            dqs = _dot(dp, kh, NN) + e_inter * (_dot(dnum, c_prev, NT) + dden * n_prev)
            dk = _dot(dp, qs, TN)
            g_next = g_scr[h]
            gn_next = gn_scr[h:h + 1, :]
            w_state = q["e_end_c"] * q["c_inj"]
            dk_state = w_state * (_dot(vh, g_next, NT) + gn_next)
            dk = dk + dk_state
            dv = dv + w_state * _dot(kh, g_next, NN)
            dq = dqs * (HEAD_DIM ** -0.5)
            eye = q["t_i"] == q["s_i"]
            to_row = lambda col: jnp.sum(jnp.where(eye, col, 0.0), axis=0, keepdims=True)
            to_col = lambda row: jnp.sum(jnp.where(eye, row, 0.0), axis=1, keepdims=True)
            g_pair = dwqk * wqk
            rs_in = jnp.sum(g_pair, axis=1, keepdims=True)
            cs_in_r = jnp.sum(g_pair, axis=0, keepdims=True)
            rs_state = (jnp.sum(dnum * num_state, axis=1, keepdims=True) + dden * den_state)
            cs_state = jnp.sum(kh * dk_state, axis=1, keepdims=True)
            di_c = to_col(cs_in_r) + cs_state
            through = q["a_dec"] * (jnp.sum(jnp.sum(g_next * c_prev.astype(F32), axis=1, keepdims=True),
                                            axis=0, keepdims=True)
                                    + jnp.sum(gn_next * n_prev, axis=1, keepdims=True))
            ends_here = to_row(rs_in + rs_state) - cs_in_r
            da_c = (jnp.sum(jnp.where(q["s_i"] >= q["t_i"], ends_here, 0.0), axis=1, keepdims=True)
                    + jnp.sum(jnp.where(q["s_i"] < q["t_i"], to_row(cs_state), 0.0), axis=1, keepdims=True)
                    + through)
            df_c = da_c * _sigmoid(-q["fc"])
            dgate = dgate + jnp.where(lane == h, di_c, 0.0) + jnp.where(lane == HEADS + h, df_c, 0.0)
            dqkvo_ref[:, sl] = dq.astype(BF16)
            dqkvo_ref[:, D_MLSTM + h * HEAD_DIM:D_MLSTM + (h + 1) * HEAD_DIM] = dk.astype(BF16)
            dqkvo_ref[:, 2 * D_MLSTM + h * HEAD_DIM:2 * D_MLSTM + (h + 1) * HEAD_DIM] = dv.astype(BF16)
            dqkvo_ref[:, 3 * D_MLSTM + h * HEAD_DIM:3 * D_MLSTM + (h + 1) * HEAD_DIM] = d_o.astype(BF16)
            g_scr[h] = q["a_dec"] * g_next + _dot(e_inter * qs, dnum, TN)
            gn_scr[h:h + 1, :] = q["a_dec"] * gn_next + jnp.sum(e_inter * qs * dden, axis=0, keepdims=True)
        dgate_ref[...] = dgate

    rev = lambda c: nc - 1 - c
    blk = lambda j: pl.BlockSpec((L, D_MLSTM), lambda c, j=j: (rev(c), j))
    full = lambda shp: pl.BlockSpec(shp, lambda c: tuple(0 for _ in shp))
    return pl.pallas_call(
        body, grid=(nc,), name="mlstm_bwd",
        in_specs=[blk(0), blk(1), blk(2), blk(3),
                  pl.BlockSpec((L, LANES), lambda c: (rev(c), (4 * D_MLSTM + 2 * D_LRU) // LANES)),
                  pl.BlockSpec((None, 2 * HEADS, L), lambda c: (rev(c), 0, 0)),
                  full((1, LANES)), full((2 * HEADS, 1)), full((HEADS, HEAD_DIM)),
                  pl.BlockSpec((None, HEADS, HEAD_DIM, HEAD_DIM), lambda c: (rev(c), 0, 0, 0)),
                  pl.BlockSpec((None, HEADS, HEAD_DIM), lambda c: (rev(c), 0, 0)),
                  pl.BlockSpec((None, HEADS, LANES), lambda c: (rev(c), 0, 0)),
                  pl.BlockSpec((L, D_MLSTM), lambda c: (rev(c), 0))],
        out_specs=[pl.BlockSpec((L, 4 * D_MLSTM), lambda c: (rev(c), 0)),
                   pl.BlockSpec((L, LANES), lambda c: (rev(c), 0)),
                   full((HEADS, HEAD_DIM))],
        out_shape=[jax.ShapeDtypeStruct((s, _PROJ_PAD), BF16),
                   jax.ShapeDtypeStruct((s, LANES), F32),
                   jax.ShapeDtypeStruct((HEADS, HEAD_DIM), F32)],
        scratch_shapes=[pltpu.VMEM((HEADS, HEAD_DIM, HEAD_DIM), F32), pltpu.VMEM((HEADS, HEAD_DIM), F32)],
        compiler_params=_params(("arbitrary",)),
    )(proj, proj, proj, proj, proj, gates_t, bias_row, bias_col, head_g, cprev, nprev, mprev, dmix)


def _lru_gates(xc, wa_ref, wx_ref, ba, bx, lam):
    r = _sigmoid(_dot(xc, wa_ref[...], NN) + ba)
    ig = _sigmoid(_dot(xc, wx_ref[...], NN) + bx)
    sp = _softplus(-lam)
    log_a = (-LRU_C * r) * sp
    a = jnp.exp(log_a)
    mult = jnp.sqrt(_one_minus_exp(2.0 * log_a))
    return r, ig, sp, a, mult


def _lru_conv(xr, prev, w_ref, b):
    xc = b + _shift_down(xr, prev, 3) * w_ref[0:1, :]
    for j in range(1, LRU_CONV):
        xc = xc + _shift_down(xr, prev, LRU_CONV - 1 - j) * w_ref[j:j + 1, :]
    return xc


def _lru_fwd(proj, mix, conv_w, conv_b, wa, wx, ba, bx, lam, tt=1024):
    s = proj.shape[0]
    tt = min(tt, s)
    nt = s // tt
    B = LRU_BLOCK_DIM
    lru_col = 4 * D_MLSTM // B
    mix_col = D_MLSTM // B

    def body(xr_ref, gr_ref, cw_ref, cb_ref, wa_ref, wx_ref, ba_ref, bx_ref, lam_ref, mix_in_ref,
             out_ref, h_ref, prev_scr, hcar_scr):
        @pl.when(pl.program_id(1) == 0)
        def _():
            prev_scr[...] = jnp.zeros_like(prev_scr)
            hcar_scr[...] = jnp.zeros_like(hcar_scr)

        xr = xr_ref[...]
        xc = _lru_conv(xr, prev_scr[...], cw_ref, cb_ref[...])
        prev_scr[...] = xr[tt - HALO:, :]
        _, ig, _, a, mult = _lru_gates(xc, wa_ref, wx_ref, ba_ref[...], bx_ref[...], lam_ref[...])
        u = mult * (ig * xc)
        rows = _rows((tt, B))
        acc_a, acc_b = a, u
        d = 1
        while d < tt:
            keep = rows >= d
            sh_a = jnp.where(keep, pltpu.roll(acc_a, d, axis=0), 1.0)
            sh_b = jnp.where(keep, pltpu.roll(acc_b, d, axis=0), 0.0)
            acc_b = acc_a * sh_b + acc_b
            acc_a = acc_a * sh_a
            d *= 2
        hv = acc_b + acc_a * hcar_scr[0:1, :]
        hcar_scr[...] = jnp.broadcast_to(hv[tt - 1:tt, :], hcar_scr.shape)
        h_ref[...] = hv
        out_ref[...] = (hv * _gelu(gr_ref[...])).astype(BF16)

    chan = lambda rws: pl.BlockSpec((rws, B), lambda n, i: (0, n))
    return pl.pallas_call(
        body, grid=(LRU_BLOCKS, nt), name="lru_fwd",
        in_specs=[pl.BlockSpec((tt, B), lambda n, i: (i, lru_col + 2 * n)),
                  pl.BlockSpec((tt, B), lambda n, i: (i, lru_col + 2 * n + 1)),
                  chan(LRU_CONV), chan(1),
                  pl.BlockSpec((None, B, B), lambda n, i: (n, 0, 0)),
                  pl.BlockSpec((None, B, B), lambda n, i: (n, 0, 0)),
                  chan(1), chan(1), chan(1), ANY],
        out_specs=[pl.BlockSpec((tt, B), lambda n, i: (i, mix_col + n)), pl.BlockSpec((tt, B), lambda n, i: (i, n))],
        out_shape=[jax.ShapeDtypeStruct(mix.shape, BF16), jax.ShapeDtypeStruct((s, D_LRU), F32)],
        scratch_shapes=[pltpu.VMEM((HALO, B), F32), pltpu.VMEM((HALO, B), F32)],
        input_output_aliases={9: 0},
        compiler_params=_params(("parallel", "arbitrary")),
    )(proj, proj, conv_w, conv_b, wa, wx, ba, bx, lam, mix)


def _lru_bwd(proj, hsave, dmix, dproj, conv_w, conv_b, wa, wx, ba, bx, lam, tt=1024):
    s = proj.shape[0]
    tt = min(tt, s)
    nt = s // tt
    B = LRU_BLOCK_DIM
    lru_col = 4 * D_MLSTM // B
    dmix_col = D_MLSTM // B
    hpb = tt // HALO

    def body(xr_ref, xprev_ref, gr_ref, h_ref, hprev_ref, dmix_ref, cw_ref, cb_ref, wa_ref, wx_ref,
             ba_ref, bx_ref, lam_ref, dproj_in_ref,
             dxg_ref, dcw_ref, dcb_ref, dwa_ref, dwx_ref, dba_ref, dbx_ref, dlam_ref,
             gcar_scr, acar_scr, dxc_scr):
        i = pl.program_id(1)
        first_tile = i == nt - 1

        @pl.when(i == 0)
        def _():
            gcar_scr[...] = jnp.zeros_like(gcar_scr)
            acar_scr[...] = jnp.zeros_like(acar_scr)
            dxc_scr[...] = jnp.zeros_like(dxc_scr)
            for ref in (dcw_ref, dcb_ref, dwa_ref, dwx_ref, dba_ref, dbx_ref, dlam_ref):
                ref[...] = jnp.zeros_like(ref)

        xr = xr_ref[...]
        xprev = jnp.where(first_tile, 0.0, xprev_ref[...])
        hprev = jnp.where(first_tile, 0.0, hprev_ref[...])
        lam = lam_ref[...]
        taps = [_shift_down(xr, xprev, LRU_CONV - 1 - j) for j in range(LRU_CONV)]
        xc = cb_ref[...] + taps[0] * cw_ref[0:1, :]
        for j in range(1, LRU_CONV):
            xc = xc + taps[j] * cw_ref[j:j + 1, :]
        r, ig, sp, a, mult = _lru_gates(xc, wa_ref, wx_ref, ba_ref[...], bx_ref[...], lam)
        gr = gr_ref[...]
        hv = h_ref[...]
        dout = dmix_ref[...]
        dxg_ref[:, B:] = (dout * hv * _gelu_grad(gr)).astype(BF16)
        dh = dout * _gelu(gr)
        rows = _rows((tt, B))
        acc_a = _shift_up(a, acar_scr[...], 1)
        acc_b = dh
        d = 1
        while d < tt:
            keep = rows < tt - d
            sh_a = jnp.where(keep, pltpu.roll(acc_a, tt - d, axis=0), 1.0)
            sh_b = jnp.where(keep, pltpu.roll(acc_b, tt - d, axis=0), 0.0)
            acc_b = acc_a * sh_b + acc_b
            acc_a = acc_a * sh_a
            d *= 2
        gv = acc_b + acc_a * gcar_scr[0:1, :]
        gcar_scr[...] = jnp.broadcast_to(gv[0:1, :], gcar_scr.shape)
        acar_scr[...] = jnp.broadcast_to(a[0:1, :], acar_scr.shape)
        h_before = _shift_down(hv, hprev, 1)
        da = gv * h_before
        dmult = gv * (ig * xc)
        dig = gv * mult * xc
        dxc = gv * mult * ig
        dlog_a = da * a - dmult * (a * a) / mult
        dr = dlog_a * (-LRU_C * sp)
        dlam_ref[...] += jnp.sum(dlog_a * (-LRU_C * r), axis=0, keepdims=True) * (-_sigmoid(-lam))
        dpre_r = dr * r * (1.0 - r)
        dpre_i = dig * ig * (1.0 - ig)
        dba_ref[...] += jnp.sum(dpre_r, axis=0, keepdims=True)
        dbx_ref[...] += jnp.sum(dpre_i, axis=0, keepdims=True)
        dwa_ref[...] += _dot(xc, dpre_r, TN)
        dwx_ref[...] += _dot(xc, dpre_i, TN)
        dxc = dxc + _dot(dpre_r, wa_ref[...], NT) + _dot(dpre_i, wx_ref[...], NT)
        dcb_ref[...] += jnp.sum(dxc, axis=0, keepdims=True)
        nxt = dxc_scr[...]
        dxr = jnp.zeros((tt, B), F32)
        for j in range(LRU_CONV):
            sft = LRU_CONV - 1 - j
            dcw_ref[j:j + 1, :] += jnp.sum(dxc * taps[j], axis=0, keepdims=True)
            dxr = dxr + _shift_up(dxc, nxt, sft) * cw_ref[j:j + 1, :]
        dxc_scr[...] = dxc[:HALO, :]
        dxg_ref[:, :B] = dxr.astype(BF16)

    rev = lambda i: nt - 1 - i
    tile = lambda col, step: pl.BlockSpec((tt, B), lambda n, i: (rev(i), col + step * n))
    halo = lambda col, step: pl.BlockSpec(
        (HALO, B), lambda n, i: (jnp.maximum(rev(i) * hpb - 1, 0), col + step * n))
    chan = lambda rws: pl.BlockSpec((rws, B), lambda n, i: (0, n))
    wblk = pl.BlockSpec((None, B, B), lambda n, i: (n, 0, 0))
    return pl.pallas_call(
        body, grid=(LRU_BLOCKS, nt), name="lru_bwd",
        in_specs=[tile(lru_col, 2), halo(lru_col, 2), tile(lru_col + 1, 2), tile(0, 1), halo(0, 1),
                  tile(dmix_col, 1), chan(LRU_CONV), chan(1), wblk, wblk, chan(1), chan(1), chan(1), ANY],
        out_specs=[pl.BlockSpec((tt, 2 * B), lambda n, i: (rev(i), lru_col // 2 + n)),
                   chan(LRU_CONV), chan(1), wblk, wblk, chan(1), chan(1), chan(1)],
        out_shape=[jax.ShapeDtypeStruct(dproj.shape, BF16),
                   jax.ShapeDtypeStruct((LRU_CONV, D_LRU), F32), jax.ShapeDtypeStruct((1, D_LRU), F32),
                   jax.ShapeDtypeStruct((LRU_BLOCKS, B, B), F32), jax.ShapeDtypeStruct((LRU_BLOCKS, B, B), F32),
                   jax.ShapeDtypeStruct((1, D_LRU), F32), jax.ShapeDtypeStruct((1, D_LRU), F32),
                   jax.ShapeDtypeStruct((1, D_LRU), F32)],
        scratch_shapes=[pltpu.VMEM((HALO, B), F32), pltpu.VMEM((HALO, B), F32), pltpu.VMEM((HALO, B), F32)],
        input_output_aliases={13: 0},
        compiler_params=_params(("parallel", "arbitrary")),
    )(proj, proj, proj, hsave, hsave, dmix, conv_w, conv_b, wa, wx, ba, bx, lam, dproj)


def _ffn_conv(gp, prev, w_ref, b):
    g = b + _shift_down(gp, prev, 2) * w_ref[0:1, :]
    for j in range(1, FFN_CONV):
        g = g + _shift_down(gp, prev, FFN_CONV - 1 - j) * w_ref[j:j + 1, :]
    return g


def _ffn_act_fwd(gu, conv_w, conv_b, tt=256):
    s = gu.shape[0]
    tt = min(tt, s)
    d_ff = conv_w.shape[1]
    tc = d_ff // N_CHIPS
    hpb = tt // HALO

    def body(g_ref, gprev_ref, u_ref, w_ref, b_ref, act_ref):
        prev = jnp.where(pl.program_id(0) == 0, 0.0, gprev_ref[...])
        gate = _ffn_conv(g_ref[...], prev, w_ref, b_ref[...])
        act_ref[...] = (gate * _sigmoid(gate) * u_ref[...]).astype(BF16)

    return pl.pallas_call(
        body, grid=(s // tt, N_CHIPS), name="ffn_act_fwd",
        in_specs=[pl.BlockSpec((tt, tc), lambda i, j: (i, 2 * j)),
                  pl.BlockSpec((HALO, tc), lambda i, j: (jnp.maximum(i * hpb - 1, 0), 2 * j)),
                  pl.BlockSpec((tt, tc), lambda i, j: (i, 2 * j + 1)),
                  pl.BlockSpec((FFN_CONV, tc), lambda i, j: (0, j)),
                  pl.BlockSpec((1, tc), lambda i, j: (0, j))],
        out_specs=pl.BlockSpec((tt, tc), lambda i, j: (i, j)),
        out_shape=jax.ShapeDtypeStruct((s, d_ff), BF16),
        compiler_params=_params(("parallel", "parallel")),
    )(gu, gu, gu, conv_w, conv_b)


def _ffn_act_bwd(gu, dact, conv_w, conv_b, tt=256):
    s = gu.shape[0]
    tt = min(tt, s)
    nt = s // tt
    d_ff = conv_w.shape[1]
    tc = d_ff // N_CHIPS
    hpb = tt // HALO

    def dgate_of(gate, up, da):
        sg = _sigmoid(gate)
        return da * up * (sg * (1.0 + gate * (1.0 - sg))), da * (gate * sg)

    def body(g_ref, gprev_ref, gnext_ref, u_ref, unext_ref, da_ref, danext_ref, w_ref, b_ref,
             dgu_ref, dw_ref, db_ref):
        i = pl.program_id(1)

        @pl.when(i == 0)
        def _():
            dw_ref[...] = jnp.zeros_like(dw_ref)
            db_ref[...] = jnp.zeros_like(db_ref)

        gp = g_ref[...]
        prev = jnp.where(i == 0, 0.0, gprev_ref[...])
        bias = b_ref[...]
        taps = [_shift_down(gp, prev, FFN_CONV - 1 - j) for j in range(FFN_CONV)]
        gate = bias + taps[0] * w_ref[0:1, :]
        for j in range(1, FFN_CONV):
            gate = gate + taps[j] * w_ref[j:j + 1, :]
        dgate, dup = dgate_of(gate, u_ref[...], da_ref[...])
        gate_n = _ffn_conv(gnext_ref[...], gp[tt - HALO:, :], w_ref, bias)
        dgate_n, _ = dgate_of(gate_n, unext_ref[...], danext_ref[...])
        dgate_n = jnp.where(i == nt - 1, 0.0, dgate_n)
        db_ref[...] += jnp.sum(dgate, axis=0, keepdims=True)
        dgp = jnp.zeros((tt, tc), F32)
        for j in range(FFN_CONV):
            dw_ref[j:j + 1, :] += jnp.sum(dgate * taps[j], axis=0, keepdims=True)
            dgp = dgp + _shift_up(dgate, dgate_n, FFN_CONV - 1 - j) * w_ref[j:j + 1, :]
        dgu_ref[:, :tc] = dgp.astype(BF16)
        dgu_ref[:, tc:] = dup.astype(BF16)

    tile = lambda half: pl.BlockSpec((tt, tc), lambda j, i, half=half: (i, 2 * j + half))
    hprev = lambda half: pl.BlockSpec((HALO, tc), lambda j, i, half=half: (jnp.maximum(i * hpb - 1, 0), 2 * j + half))
    hnext = lambda half: pl.BlockSpec(
        (HALO, tc), lambda j, i, half=half: (jnp.minimum((i + 1) * hpb, nt * hpb - 1), 2 * j + half))
    return pl.pallas_call(
        body, grid=(N_CHIPS, nt), name="ffn_act_bwd",
        in_specs=[tile(0), hprev(0), hnext(0), tile(1), hnext(1),
                  pl.BlockSpec((tt, tc), lambda j, i: (i, j)),
                  pl.BlockSpec((HALO, tc), lambda j, i: (jnp.minimum((i + 1) * hpb, nt * hpb - 1), j)),
                  pl.BlockSpec((FFN_CONV, tc), lambda j, i: (0, j)),
                  pl.BlockSpec((1, tc), lambda j, i: (0, j))],
        out_specs=[pl.BlockSpec((tt, 2 * tc), lambda j, i: (i, j)),
                   pl.BlockSpec((FFN_CONV, tc), lambda j, i: (0, j)),
                   pl.BlockSpec((1, tc), lambda j, i: (0, j))],
        out_shape=[jax.ShapeDtypeStruct((s, 2 * d_ff), BF16),
                   jax.ShapeDtypeStruct((FFN_CONV, d_ff), F32), jax.ShapeDtypeStruct((1, d_ff), F32)],
        compiler_params=_params(("parallel", "arbitrary")),
    )(gu, gu, gu, gu, gu, dact, dact, conv_w, conv_b)


def _gate_grads(dgate, dproj, tm=512):
    s, n = dgate.shape
    tm = min(tm, s)

    def body(a_ref, dproj_in_ref, o_ref, dproj_ref):
        @pl.when(pl.program_id(0) == 0)
        def _():
            o_ref[...] = jnp.zeros_like(o_ref)
        a = a_ref[...]
        o_ref[...] += jnp.sum(a, axis=0, keepdims=True)
        dproj_ref[...] = a.astype(BF16)

    return pl.pallas_call(
        body, grid=(s // tm,), name="gate_grads",
        in_specs=[pl.BlockSpec((tm, n), lambda i: (i, 0)), ANY],
        out_specs=[pl.BlockSpec((1, n), lambda i: (0, 0)),
                   pl.BlockSpec((tm, n), lambda i: (i, (_QKVO + 2 * D_LRU) // LANES))],
        out_shape=[jax.ShapeDtypeStruct((1, n), F32), jax.ShapeDtypeStruct(dproj.shape, BF16)],
        input_output_aliases={1: 1},
        compiler_params=_params(("arbitrary",)),
    )(dgate, dproj)


def _pick(n, *cands):
    for c in cands:
        if n % c == 0:
            return c
    raise ValueError(f"no tile for {n}")


def _behind(a, token):
    return a if token is None else a + token[0:1, 0:1].astype(a.dtype).reshape((1,) * a.ndim)


class _Gathered:
    def __init__(self, w):
        self.w = w

    def begin(self):
        return None

    def mid(self, grp, after):
        return None

    def end(self, grp, after):
        return self.w

    def reduce_early(self, grads):
        return None

    def reduce_early_mid(self, after):
        return None

    def reduce_late(self, grads):
        return None

    def reduce_late_mid(self, after):
        return None


def _local_step(x, target, w, comm):
    s, d = x.shape
    nc = s // CHUNK
    tm = _pick(s, 1024, 512, 256)
    tn_proj = _pick(_PROJ_PAD, 896)
    gate_col = 4 * D_MLSTM + 2 * D_LRU
    w = dict(w)

    token = comm.begin()
    n1, rstd1 = _rmsnorm_fwd("norm_mix_fwd", x, _behind(w["norm_mix_g"], token))
    comm.mid(0, n1)
    w.update(comm.end(0, None))
    proj = _mm_nn("proj_fwd", n1, w["w_in"], tm, tn_proj, d)
    token = comm.mid(1, proj)
    gates = proj[:, gate_col:gate_col + 2 * HEADS]
    gates_t = gates.reshape(nc, CHUNK, 2 * HEADS).transpose(0, 2, 1)
    bias_row = _behind(jnp.pad(w["b_gate_m"], ((0, 0), (0, LANES - 2 * HEADS))), token)
    bias_col = w["b_gate_m"].reshape(2 * HEADS, 1)
    mix, cprev, nprev, mprev = _mlstm_fwd(proj, gates_t, bias_row, bias_col, w["mlstm_norm_g"])
    mix, hsave = _lru_fwd(proj, mix, w["lru_conv_w"], w["lru_conv_b"], w["lru_wa"], w["lru_wx"],
                          w["lru_ba"], w["lru_bx"], w["lru_lambda"])
    w.update(comm.end(1, hsave))
    token = comm.mid(2, hsave)
    x1 = _mm_nn("out_fwd", mix, w["w_out"], tm, 1024, d, res=x)
    n2, rstd2 = _rmsnorm_fwd("norm_ffn_fwd", x1, _behind(w["norm_ffn_g"], token))
    w.update(comm.end(2, n2))
    token = comm.mid(3, n2)
    gu = _mm_up_fwd("up_fwd", n2, w["w_up"], tm, d)
    act = _ffn_act_fwd(gu, w["ffn_conv_w"], _behind(w["ffn_conv_b"], token))
    w.update(comm.end(3, act))
    d_ff = w["w_down"].shape[0]
    x2 = _mm_nn("down_fwd", act, w["w_down"], min(tm, 512), 1024, d_ff // 2, res=x1)
    loss, dx2, dx2b, g_norm_final = _loss_head("loss_head", x2, w["norm_final_g"], target)

    grads = {"norm_final_g": g_norm_final}
    dact = _mm_nt("down_bwd_x", dx2b, w["w_down"], tm, d_ff // N_CHIPS, d)
    grads["w_down"] = _mm_tn("down_bwd_w", act, dx2b, d_ff // N_CHIPS, 1024, 2048)
    dgu, grads["ffn_conv_w"], grads["ffn_conv_b"] = _ffn_act_bwd(gu, dact, w["ffn_conv_w"], w["ffn_conv_b"])
    dn2 = _mm_up_bwd_x("up_bwd_x", dgu, w["w_up"], tm, 1024)
    grads["w_up"] = _mm_up_bwd_w("up_bwd_w", n2, dgu, 1024, 2048)
    token = comm.reduce_early(grads)
    dx1, dx1b, grads["norm_ffn_g"] = _rmsnorm_bwd("norm_ffn_bwd", x1, rstd2, _behind(w["norm_ffn_g"], token),
                                                  dn2, dx2)
    dmix = _mm_nt("out_bwd_x", dx1b, w["w_out"], tm, 1024, d)
    grads["w_out"] = _mm_tn("out_bwd_w", mix, dx1b, 1024, 1024, 2048)
    token = comm.reduce_early_mid(grads["w_out"])
    dproj, dgate, grads["mlstm_norm_g"] = _mlstm_bwd(proj, gates_t, _behind(bias_row, token), bias_col,
                                                     w["mlstm_norm_g"], cprev, nprev, mprev, dmix)
    (dproj, grads["lru_conv_w"], grads["lru_conv_b"], grads["lru_wa"], grads["lru_wx"],
     grads["lru_ba"], grads["lru_bx"], grads["lru_lambda"]) = _lru_bwd(
        proj, hsave, dmix, dproj, w["lru_conv_w"], w["lru_conv_b"], w["lru_wa"], w["lru_wx"],
        w["lru_ba"], w["lru_bx"], w["lru_lambda"])
    gate_bias_grad, dproj = _gate_grads(dgate, dproj)
    grads["b_gate_m"] = gate_bias_grad[:, :2 * HEADS]
    grads["w_in"] = _mm_tn("proj_bwd_w", n1, dproj, 1024, tn_proj, 2048)
    token = comm.reduce_late(grads)
    dn1 = _mm_nt("proj_bwd_x", dproj, w["w_in"], tm, 512, _PROJ_PAD, after=token)
    token = comm.reduce_late_mid(dn1)
    grad_x, _, grads["norm_mix_g"] = _rmsnorm_bwd("norm_mix_bwd", x, rstd1, _behind(w["norm_mix_g"], token),
                                                  dn1, dx1)
    return loss, grad_x, grads


WEIGHT_NAMES = ("norm_mix_g", "w_in", "b_gate_m", "mlstm_norm_g", "lru_conv_w", "lru_conv_b", "lru_wa", "lru_ba",
                "lru_wx", "lru_bx", "lru_lambda", "w_out", "norm_ffn_g", "w_up", "ffn_conv_w", "ffn_conv_b",
                "w_down", "norm_final_g")
BIG = ("w_in", "w_out", "w_up", "w_down")
SMALL_SHARDED = ("mlstm_norm_g", "lru_conv_w", "ffn_conv_w")
SMALL = tuple(n for n in WEIGHT_NAMES if n not in BIG)
SMALL_REPLICATED = tuple(n for n in SMALL if n not in SMALL_SHARDED)


def _proj_segments():
    segs = [(0, 0, _QKVO), (_QKVO, _QKVO + 2 * D_LRU, _N_GATES)]
    for n in range(LRU_BLOCKS):
        segs.append((_QKVO + _N_GATES + n * LRU_BLOCK_DIM, _QKVO + 2 * n * LRU_BLOCK_DIM, LRU_BLOCK_DIM))
        segs.append((_QKVO + _N_GATES + D_LRU + n * LRU_BLOCK_DIM, _QKVO + (2 * n + 1) * LRU_BLOCK_DIM,
                     LRU_BLOCK_DIM))
    return segs


def _w_in_shards_to_local(shards):
    width = shards.shape[2]
    pieces = []
    for g0, _, n in sorted(_proj_segments(), key=lambda s: s[1]):
        at = g0
        while at < g0 + n:
            j = at // width
            stop = min(g0 + n, (j + 1) * width)
            pieces.append(shards[j][:, at - j * width:stop - j * width])
            at = stop
    pieces.append(jnp.zeros((shards.shape[1], PROJ_GATE_PAD - _N_GATES), shards.dtype))
    return jnp.concatenate(pieces, axis=1)


def _w_in_local_to_shards(w):
    width = _PROJ_COLS // N_CHIPS
    shards = []
    for j in range(N_CHIPS):
        pieces = []
        for g0, l0, n in sorted(_proj_segments()):
            lo, hi = max(g0, j * width), min(g0 + n, (j + 1) * width)
            if lo < hi:
                pieces.append(w[:, l0 + lo - g0:l0 + hi - g0])
        shards.append(jnp.concatenate(pieces, axis=1))
    return jnp.stack(shards)


def _w_in_to_global(w):
    sh = _w_in_local_to_shards(w)
    return jnp.concatenate([sh[j] for j in range(N_CHIPS)], axis=1)


def _size(shp):
    return functools.reduce(lambda a, b: a * b, shp, 1)


def _lane_dense(shp):
    return len(shp) >= 2 and shp[-1] == LANES and _size(shp) % (HALO * LANES) == 0


def _pack_rows(shapes):
    loose = sum(_size(shp) for shp in shapes if not _lane_dense(shp))
    return sum(_size(shp) // LANES for shp in shapes if _lane_dense(shp)) + -(-loose // (HALO * LANES)) * HALO


def _pack(arrs, rows):
    del rows
    parts = [a.reshape(-1, LANES).astype(F32) for a in arrs if _lane_dense(a.shape)]
    loose = [a.reshape(-1).astype(F32) for a in arrs if not _lane_dense(a.shape)]
    if loose:
        flat = jnp.concatenate(loose)
        n = -(-flat.shape[0] // (HALO * LANES)) * HALO * LANES
        parts.append(jnp.pad(flat, (0, n - flat.shape[0])).reshape(-1, LANES))
    return parts[0] if len(parts) == 1 else jnp.concatenate(parts, axis=0)


def _unpack(buf, shapes):
    out, row = {}, 0
    for i, shp in enumerate(shapes):
        if _lane_dense(shp):
            n = _size(shp) // LANES
            out[i] = buf[row:row + n].reshape(shp)
            row += n
    flat, at = buf[row:].reshape(-1), 0
    for i, shp in enumerate(shapes):
        if not _lane_dense(shp):
            out[i] = flat[at:at + _size(shp)].reshape(shp)
            at += _size(shp)
    return [out[i] for i in range(len(shapes))]


def _assemble_weights(g_in, g_out, g_up, g_down, small_sharded, replicated):
    w = dict(replicated)
    w["w_in"] = _w_in_shards_to_local(g_in)
    w["w_out"] = g_out.reshape(-1, g_out.shape[-1])
    w["w_up"] = g_up
    w["w_down"] = g_down.reshape(-1, g_down.shape[-1])
    for name, v in small_sharded.items():
        w[name] = jnp.concatenate([v[j] for j in range(N_CHIPS)], axis=1)
    return w


def _full_weights_from_global(weights):
    shard = lambda a, axis: jnp.stack(jnp.split(a, N_CHIPS, axis=axis))
    rep = {n: weights[n].reshape(1, -1) if weights[n].ndim <= 2 and n != "b_gate_m" else weights[n]
           for n in SMALL_REPLICATED}
    rep["b_gate_m"] = weights["b_gate_m"].reshape(1, -1)
    return _assemble_weights(shard(weights["w_in"], 1).astype(BF16), shard(weights["w_out"], 0).astype(BF16),
                             shard(weights["w_up"], 1).astype(BF16), shard(weights["w_down"], 0).astype(BF16),
                             {n: shard(weights[n], 1) for n in SMALL_SHARDED}, rep)


def _grads_to_global(grads):
    g = dict(grads)
    g["w_in"] = _w_in_to_global(grads["w_in"])
    g["w_up"] = jnp.concatenate([grads["w_up"][j] for j in range(N_CHIPS)], axis=1)
    return g


def _place():
    x, y, c = lax.axis_index("x"), lax.axis_index("y"), lax.axis_index("c")
    chips = [(1 - x, y), (x, 1 - y), (1 - x, 1 - y)]
    return x, y, c, 2 * x + y, chips


def _half_rows(n_rows, which):
    half = n_rows // 2
    return pl.ds(pl.multiple_of(which * half, 16), half)


def _rcopy(src, dst, send_sem, recv_sem, to):
    return pltpu.make_async_remote_copy(src_ref=src, dst_ref=dst, send_sem=send_sem, recv_sem=recv_sem,
                                        device_id=to, device_id_type=MESH)


HBM_SPEC = pl.BlockSpec(memory_space=pltpu.HBM)
SEM_SPEC = pl.BlockSpec(memory_space=pltpu.SEMAPHORE)
TOKEN_SHAPE = (8, LANES)


def _split_call(name, bufs, sems_in, sems_out_shapes, body_fn, after=None):
    nb, ni, no = len(bufs), len(sems_in), len(sems_out_shapes)
    after = [] if after is None else list(after) if isinstance(after, (list, tuple)) else [after]

    def body(*refs):
        buf_refs = refs[:nb]
        sem_in_refs = refs[nb:nb + ni]
        outs = refs[nb + ni + len(after):]
        sem_out_refs = outs[:no]
        token_ref = outs[no + nb]
        body_fn(buf_refs, sem_in_refs, sem_out_refs)
        token_ref[...] = jnp.zeros_like(token_ref)

    out_shape = ([pltpu.SemaphoreType.DMA(shp) for shp in sems_out_shapes]
                 + [pltpu.HBM(b.shape, b.dtype) for b in bufs] + [jax.ShapeDtypeStruct(TOKEN_SHAPE, F32)])
    res = pl.pallas_call(
        body, name=name, out_shape=out_shape,
        in_specs=[HBM_SPEC] * nb + [SEM_SPEC] * ni + [ANY] * len(after),
        out_specs=[SEM_SPEC] * no + [HBM_SPEC] * nb + [pl.BlockSpec(memory_space=pltpu.VMEM)],
        input_output_aliases={i: no + i for i in range(nb)},
        compiler_params=pltpu.CompilerParams(has_side_effects=pltpu.SideEffectType.DATAFLOW_SIDE_EFFECTING),
    )(*[pltpu.with_memory_space_constraint(b, pltpu.HBM) for b in bufs], *sems_in, *after)
    return list(res[:no]), list(res[no:no + nb]), res[no + nb]


def _place_own_shard(name, idx, shard, after=None):
    rows, cols = shard.shape
    tr = _row_tile(rows)

    def body(idx_ref, s_ref, *rest):
        rest[-1][...] = s_ref[...].astype(BF16)

    return pl.pallas_call(
        body, name=name, out_shape=jax.ShapeDtypeStruct((N_CHIPS, rows, cols), BF16),
        grid_spec=pltpu.PrefetchScalarGridSpec(
            num_scalar_prefetch=1, grid=(rows // tr,),
            in_specs=[pl.BlockSpec((tr, cols), lambda i, s: (i, 0))] + ([] if after is None else [ANY]),
            out_specs=pl.BlockSpec((None, tr, cols), lambda i, s: (s[1], i, 0))),
        compiler_params=_params(("parallel",)),
    )(idx, shard, *(() if after is None else (after,)))


GATHER_GROUPS = ((0, 4), (1,), (2,), (3,))


def _gather_start(name, lands, groups, after=None):
    members = [w for g in groups for w in GATHER_GROUPS[g]]

    def starts(bufs, _, sems):
        x, y, c, me, chips = _place()
        for gi, g in enumerate(groups):
            for pos, w in enumerate(GATHER_GROUPS[g]):
                buf = bufs[members.index(w)]
                part = buf.at[me] if w == 4 else buf.at[me, _half_rows(buf.shape[1], c)]
                for k, chip in enumerate(chips):
                    _rcopy(part, part, sems[2 * gi].at[3 * pos + k], sems[2 * gi + 1].at[3 * pos + k],
                           (*chip, c)).start()

    shapes = []
    for g in groups:
        shapes += [(3 * len(GATHER_GROUPS[g]),)] * 2
    sems, bufs, token = _split_call(name, [lands[w] for w in members], [], shapes, starts, after=after)
    return ({g: (sems[2 * gi], sems[2 * gi + 1]) for gi, g in enumerate(groups)},
            dict(zip(members, bufs)), token)


def _gather_mid(grp, lands, sems, after):
    members = GATHER_GROUPS[grp]
    big = [w for w in members if w != 4]

    def mid(bufs, sems_in, sems_out):
        x, y, c, me, chips = _place()
        send_sems, recv_sems = sems_in
        for pos, w in enumerate(members):
            for k, chip in enumerate(chips):
                cid = 2 * chip[0] + chip[1]
                buf = bufs[pos]
                mine = buf.at[me] if w == 4 else buf.at[me, _half_rows(buf.shape[1], c)]
                theirs = buf.at[cid] if w == 4 else buf.at[cid, _half_rows(buf.shape[1], c)]
                arrival = _rcopy(mine, theirs, send_sems.at[3 * pos + k], recv_sems.at[3 * pos + k], (*chip, c))
                arrival.wait_recv()
                arrival.wait_send()
                if w != 4:
                    _rcopy(theirs, theirs, sems_out[0].at[3 * big.index(w) + k],
                           sems_out[1].at[3 * big.index(w) + k], (x, y, 1 - c)).start()

    new_sems, bufs, token = _split_call(f"gather_mid_{grp}", [lands[w] for w in members], list(sems),
                                        [(3 * len(big),), (3 * len(big),)], mid, after=after)
    return new_sems, bufs, token


def _gather_end(grp, bufs, sems, after):
    members = GATHER_GROUPS[grp]
    big = [w for w in members if w != 4]

    def end(refs, sems_in, _):
        x, y, c, me, chips = _place()
        send_sems, recv_sems = sems_in
        for pos, w in enumerate(members):
            if w == 4:
                continue
            for k, chip in enumerate(chips):
                cid = 2 * chip[0] + chip[1]
                buf = refs[pos]
                sent = buf.at[cid, _half_rows(buf.shape[1], c)]
                landed = buf.at[cid, _half_rows(buf.shape[1], 1 - c)]
                fwd = _rcopy(sent, landed, send_sems.at[3 * big.index(w) + k], recv_sems.at[3 * big.index(w) + k],
                             (x, y, 1 - c))
                fwd.wait_recv()
                fwd.wait_send()

    _, bufs, token = _split_call(f"gather_end_{grp}", bufs, list(sems), [], end, after=after)
    return bufs, token


def _pair_start(name, grads, extra=None):
    n = len(grads)
    bufs = list(grads) + [lax.empty((g.shape[0], g.shape[1] // 2, g.shape[2]), g.dtype) for g in grads]
    if extra is not None:
        bufs += [extra, lax.empty(extra.shape, extra.dtype)]

    def starts(refs, _, sems):
        x, y, c, _, _ = _place()
        for w in range(n):
            other = _half_rows(refs[w].shape[1], 1 - c)
            _rcopy(refs[w].at[:, other], refs[n + w], sems[0].at[w], sems[1].at[w], (x, y, 1 - c)).start()
        if extra is not None:
            _rcopy(refs[2 * n], refs[2 * n + 1], sems[0].at[n], sems[1].at[n], (x, y, 1 - c)).start()

    count = n + (extra is not None)
    return _split_call(name, bufs, [], [(count,), (count,)], starts)


def _pair_wait(name, n, bufs, sems, after):
    has_extra = len(bufs) > 2 * n

    def waits(refs, sems_in, _):
        x, y, c, _, _ = _place()
        for w in range(n):
            other = _half_rows(refs[w].shape[1], 1 - c)
            cp = _rcopy(refs[w].at[:, other], refs[n + w], sems_in[0].at[w], sems_in[1].at[w], (x, y, 1 - c))
            cp.wait_recv()
            cp.wait_send()
        if has_extra:
            cp = _rcopy(refs[2 * n], refs[2 * n + 1], sems_in[0].at[n], sems_in[1].at[n], (x, y, 1 - c))
            cp.wait_recv()
            cp.wait_send()

    _, bufs, token = _split_call(name, bufs, list(sems), [], waits, after=after)
    return bufs, token


def _chip_start(name, partials, small=None):
    n = len(partials)
    bufs = list(partials) + [lax.empty(p.shape, p.dtype) for p in partials] + ([] if small is None else [small])

    def starts(refs, _, sems):
        _, _, c, me, chips = _place()
        for w in range(n):
            for k, chip in enumerate(chips):
                cid = 2 * chip[0] + chip[1]
                _rcopy(refs[w].at[cid], refs[n + w].at[me], sems[0].at[3 * w + k], sems[1].at[3 * w + k],
                       (*chip, c)).start()
        if small is not None:
            for k, chip in enumerate(chips):
                _rcopy(refs[2 * n].at[me], refs[2 * n].at[me], sems[0].at[3 * n + k], sems[1].at[3 * n + k],
                       (*chip, c)).start()

    count = 3 * (n + (small is not None))
    return _split_call(name, bufs, [], [(count,), (count,)], starts)


def _chip_wait(name, n, bufs, sems, after):
    has_small = len(bufs) > 2 * n

    def waits(refs, sems_in, _):
        _, _, c, me, chips = _place()
        for w in range(n):
            for k, chip in enumerate(chips):
                cid = 2 * chip[0] + chip[1]
                cp = _rcopy(refs[w].at[cid], refs[n + w].at[cid], sems_in[0].at[3 * w + k],
                            sems_in[1].at[3 * w + k], (*chip, c))
                cp.wait_recv()
                cp.wait_send()
        if has_small:
            for k, chip in enumerate(chips):
                cid = 2 * chip[0] + chip[1]
                cp = _rcopy(refs[2 * n].at[me], refs[2 * n].at[cid], sems_in[0].at[3 * n + k],
                            sems_in[1].at[3 * n + k], (*chip, c))
                cp.wait_recv()
                cp.wait_send()

    _, bufs, token = _split_call(name, bufs, list(sems), [], waits, after=after)
    return bufs, token


def _small_pair_sum(idx, own, recv):
    rows = own.shape[0]

    def body(idx_ref, a_ref, b_ref, o_ref):
        o_ref[...] = a_ref[...] + b_ref[...]

    blk = pl.BlockSpec((rows, LANES), lambda i, s: (0, 0))
    return pl.pallas_call(
        body, name="small_pair_sum", out_shape=jax.ShapeDtypeStruct((N_CHIPS, rows, LANES), F32),
        grid_spec=pltpu.PrefetchScalarGridSpec(
            num_scalar_prefetch=1, grid=(1,), in_specs=[blk, blk],
            out_specs=pl.BlockSpec((None, rows, LANES), lambda i, s: (s[1], 0, 0))),
        compiler_params=_params(("arbitrary",)),
    )(idx, own, recv)


def _gather_weights(shards, small):
    nb = len(shards)

    def body(*refs):
        srcs, small_ref = refs[:nb], refs[nb]
        dsts, small_out = refs[nb + 1:2 * nb + 1], refs[2 * nb + 1]
        send_sems, recv_sems, local_sems = refs[2 * nb + 2:]
        x, y, c, me, chips = _place()
        sibling = (x, y, 1 - c)
        mine = [_half_rows(s.shape[0], c) for s in srcs]
        other = [_half_rows(s.shape[0], 1 - c) for s in srcs]

        local = [pltpu.make_async_copy(srcs[w], dsts[w].at[me], local_sems.at[w]) for w in range(nb)]
        local.append(pltpu.make_async_copy(small_ref, small_out.at[me], local_sems.at[nb]))
        for cp in local:
            cp.start()
        sends = []
        for w in range(nb):
            for k, chip in enumerate(chips):
                sends.append(_rcopy(srcs[w].at[mine[w]], dsts[w].at[me, mine[w]],
                                    send_sems.at[w, k], recv_sems.at[w, k], (*chip, c)))
        for k, chip in enumerate(chips):
            sends.append(_rcopy(small_ref, small_out.at[me], send_sems.at[nb, k], recv_sems.at[nb, k], (*chip, c)))
        for cp in sends:
            cp.start()
        passed = []
        for w in range(nb):
            for k, chip in enumerate(chips):
                cid = 2 * chip[0] + chip[1]
                landed = dsts[w].at[cid, mine[w]]
                _rcopy(landed, landed, send_sems.at[w, k], recv_sems.at[w, k], (*chip, c)).wait_recv()
                fwd = _rcopy(landed, landed, send_sems.at[w, 3 + k], recv_sems.at[w, 3 + k], sibling)
                fwd.start()
                passed.append(fwd)
        for k, chip in enumerate(chips):
            cid = 2 * chip[0] + chip[1]
            _rcopy(small_ref, small_out.at[cid], send_sems.at[nb, k], recv_sems.at[nb, k], (*chip, c)).wait_recv()
        for w in range(nb):
            for k, chip in enumerate(chips):
                cid = 2 * chip[0] + chip[1]
                landed = dsts[w].at[cid, other[w]]
                _rcopy(landed, landed, send_sems.at[w, 3 + k], recv_sems.at[w, 3 + k], sibling).wait_recv()
        for cp in sends + passed:
            cp.wait_send()
        for cp in local:
            cp.wait()

    out_shape = [jax.ShapeDtypeStruct((N_CHIPS,) + s.shape, s.dtype) for s in shards]
    out_shape.append(jax.ShapeDtypeStruct((N_CHIPS,) + small.shape, small.dtype))
    return pl.pallas_call(
        body, name="gather_weights", out_shape=out_shape,
        in_specs=[ANY] * (nb + 1), out_specs=[ANY] * (nb + 1),
        scratch_shapes=[pltpu.SemaphoreType.DMA((nb + 1, 6)), pltpu.SemaphoreType.DMA((nb + 1, 6)),
                        pltpu.SemaphoreType.DMA((nb + 1,))],
    )(*shards, small)


def _pair_exchange(grads, small):
    nb = len(grads)

    def body(*refs):
        srcs, small_ref = refs[:nb], refs[nb]
        dsts, small_out = refs[nb + 1:2 * nb + 1], refs[2 * nb + 1]
        send_sems, recv_sems, small_send, small_recv, local_sem = refs[2 * nb + 2:]
        x, y, c, _, _ = _place()
        sibling = (x, y, 1 - c)
        my_id = 4 * x + 2 * y + c
        local = pltpu.make_async_copy(small_ref, small_out.at[my_id], local_sem)
        local.start()
        sends = []
        for w in range(nb):
            other = _half_rows(srcs[w].shape[1], 1 - c)
            sends.append(_rcopy(srcs[w].at[:, other], dsts[w], send_sems.at[w], recv_sems.at[w], sibling))
        for r in range(1, N_DEV):
            to = (1 - x if r & 4 else x, 1 - y if r & 2 else y, 1 - c if r & 1 else c)
            sends.append(_rcopy(small_ref, small_out.at[my_id], small_send.at[r - 1], small_recv.at[r - 1], to))
        for cp in sends:
            cp.start()
        for w in range(nb):
            _rcopy(dsts[w], dsts[w], send_sems.at[w], recv_sems.at[w], sibling).wait_recv()
        for r in range(1, N_DEV):
            frm = (1 - x if r & 4 else x, 1 - y if r & 2 else y, 1 - c if r & 1 else c)
            frm_id = 4 * frm[0] + 2 * frm[1] + frm[2]
            _rcopy(small_ref, small_out.at[frm_id], small_send.at[r - 1], small_recv.at[r - 1], frm).wait_recv()
        for cp in sends:
            cp.wait_send()
        local.wait()

    out_shape = [jax.ShapeDtypeStruct((g.shape[0], g.shape[1] // 2, g.shape[2]), g.dtype) for g in grads]
    out_shape.append(jax.ShapeDtypeStruct((N_DEV,) + small.shape, small.dtype))
    return pl.pallas_call(
        body, name="pair_exchange", out_shape=out_shape,
        in_specs=[ANY] * (nb + 1), out_specs=[ANY] * (nb + 1),
        scratch_shapes=[pltpu.SemaphoreType.DMA((nb,)), pltpu.SemaphoreType.DMA((nb,)),
                        pltpu.SemaphoreType.DMA((N_DEV - 1,)), pltpu.SemaphoreType.DMA((N_DEV - 1,)),
                        pltpu.SemaphoreType.DMA(())],
    )(*grads, small)


def _chip_exchange(partials):
    nb = len(partials)

    def body(*refs):
        srcs, dsts = refs[:nb], refs[nb:2 * nb]
        send_sems, recv_sems = refs[2 * nb:]
        _, _, c, me, chips = _place()
        sends = []
        for w in range(nb):
            for k, chip in enumerate(chips):
                cid = 2 * chip[0] + chip[1]
                sends.append(_rcopy(srcs[w].at[cid], dsts[w].at[me], send_sems.at[w, k], recv_sems.at[w, k],
                                    (*chip, c)))
        for cp in sends:
            cp.start()
        for w in range(nb):
            for k, chip in enumerate(chips):
                cid = 2 * chip[0] + chip[1]
                _rcopy(srcs[w].at[cid], dsts[w].at[cid], send_sems.at[w, k], recv_sems.at[w, k],
                       (*chip, c)).wait_recv()
        for cp in sends:
            cp.wait_send()

    return pl.pallas_call(
        body, name="chip_exchange", out_shape=[jax.ShapeDtypeStruct(p.shape, p.dtype) for p in partials],
        in_specs=[ANY] * nb, out_specs=[ANY] * nb,
        scratch_shapes=[pltpu.SemaphoreType.DMA((nb, 3)), pltpu.SemaphoreType.DMA((nb, 3))],
    )(*partials)


def _pair_share(name, shards, late=None):
    nb = len(shards)
    nl = 0 if late is None else 1

    def body(*refs):
        srcs = refs[:nb]
        dsts = refs[nb + nl:2 * nb + nl]
        send_sems, recv_sems = refs[2 * nb + 2 * nl:2 * nb + 2 * nl + 2]
        x, y, c, _, _ = _place()
        sibling = (x, y, 1 - c)
        sends = []
        for w in range(nb):
            mine = _half_rows(dsts[w].shape[0], c)
            sends.append(_rcopy(srcs[w].at[mine], dsts[w].at[mine], send_sems.at[w], recv_sems.at[w], sibling))
        if nl:
            late_ref, late_out = refs[nb], refs[2 * nb + 1]
            late_send, late_recv, local_sem = refs[2 * nb + 4:]
            my_id = 4 * x + 2 * y + c
            peer = lambda r: (1 - x if r & 4 else x, 1 - y if r & 2 else y, 1 - c if r & 1 else c)
            local = pltpu.make_async_copy(late_ref, late_out.at[my_id], local_sem)
            local.start()
            for r in range(1, N_DEV):
                sends.append(_rcopy(late_ref, late_out.at[my_id], late_send.at[r - 1], late_recv.at[r - 1],
                                    peer(r)))
        for cp in sends:
            cp.start()
        for w in range(nb):
            other = _half_rows(dsts[w].shape[0], 1 - c)
            _rcopy(srcs[w].at[other], dsts[w].at[other], send_sems.at[w], recv_sems.at[w], sibling).wait_recv()
        if nl:
            for r in range(1, N_DEV):
                frm = peer(r)
                _rcopy(late_ref, late_out.at[4 * frm[0] + 2 * frm[1] + frm[2]], late_send.at[r - 1],
                       late_recv.at[r - 1], frm).wait_recv()
        for cp in sends:
            cp.wait_send()
        if nl:
            local.wait()

    out_shape = [jax.ShapeDtypeStruct(h.shape, h.dtype) for h in shards]
    scratch = [pltpu.SemaphoreType.DMA((nb,)), pltpu.SemaphoreType.DMA((nb,))]
    if nl:
        out_shape.append(jax.ShapeDtypeStruct((N_DEV,) + late.shape, late.dtype))
        scratch += [pltpu.SemaphoreType.DMA((N_DEV - 1,)), pltpu.SemaphoreType.DMA((N_DEV - 1,)),
                    pltpu.SemaphoreType.DMA(())]
    return pl.pallas_call(
        body, name=name, out_shape=out_shape,
        in_specs=[ANY] * (nb + nl), out_specs=[ANY] * (nb + nl), scratch_shapes=scratch,
        input_output_aliases={w: w for w in range(nb)},
    )(*shards, *(() if late is None else (late,)))


def _row_tile(rows):
    return _pick(rows, 128, 64, 16, 8)


def _pair_sum(name, idx, grad, recv):
    n, half, cols = recv.shape
    tr = _row_tile(half)
    nrb = half // tr

    def body(idx_ref, g_ref, r_ref, o_ref):
        o_ref[...] = (g_ref[...] + r_ref[...]).astype(BF16)

    return pl.pallas_call(
        body, name=name, out_shape=jax.ShapeDtypeStruct(recv.shape, BF16),
        grid_spec=pltpu.PrefetchScalarGridSpec(
            num_scalar_prefetch=1, grid=(n - 1, nrb),
            in_specs=[pl.BlockSpec((None, tr, cols), lambda j, i, s: (s[2 + j], s[0] * nrb + i, 0)),
                      pl.BlockSpec((None, tr, cols), lambda j, i, s: (s[2 + j], i, 0))],
            out_specs=pl.BlockSpec((None, tr, cols), lambda j, i, s: (s[2 + j], i, 0))),
        compiler_params=_params(("parallel", "parallel")),
    )(idx, grad, recv)


def _final_sum(name, idx, grad, recv, chip_sums):
    _, half, cols = recv.shape
    tr = _row_tile(half)
    nrb = half // tr

    def body(idx_ref, g_ref, r_ref, p1_ref, p2_ref, p3_ref, o_ref):
        acc = g_ref[...] + r_ref[...]
        for p_ref in (p1_ref, p2_ref, p3_ref):
            acc = acc + p_ref[...].astype(F32)
        o_ref[...] = acc

    slot = lambda which: pl.BlockSpec((None, tr, cols), lambda i, s, which=which: (s[which], i, 0))
    return pl.pallas_call(
        body, name=name, out_shape=jax.ShapeDtypeStruct((2 * half, cols), F32),
        grid_spec=pltpu.PrefetchScalarGridSpec(
            num_scalar_prefetch=1, grid=(nrb,),
            in_specs=[pl.BlockSpec((None, tr, cols), lambda i, s: (s[1], s[0] * nrb + i, 0)),
                      slot(1), slot(2), slot(3), slot(4)],
            out_specs=pl.BlockSpec((tr, cols), lambda i, s: (s[0] * nrb + i, 0))),
        compiler_params=_params(("parallel",)),
    )(idx, grad, recv, chip_sums, chip_sums, chip_sums)


def _pair_sum_all(name, idx, grad, recv):
    _, half, cols = recv.shape
    tr = _row_tile(half)
    nrb = half // tr

    def body(idx_ref, g_ref, r_ref, o_ref):
        o_ref[...] = (g_ref[...] + r_ref[...]).astype(BF16)

    return pl.pallas_call(
        body, name=name, out_shape=jax.ShapeDtypeStruct((half, cols), BF16),
        grid_spec=pltpu.PrefetchScalarGridSpec(
            num_scalar_prefetch=1, grid=(nrb,),
            in_specs=[pl.BlockSpec((None, tr, cols), lambda i, s: (0, s[0] * nrb + i, 0)),
                      pl.BlockSpec((None, tr, cols), lambda i, s: (0, i, 0))],
            out_specs=pl.BlockSpec((tr, cols), lambda i, s: (i, 0))),
        compiler_params=_params(("parallel",)),
    )(idx, grad, recv)


def _final_sum_bf16(name, idx, partial, chip_sums):
    _, half, cols = partial.shape
    tr = _row_tile(half)
    nrb = half // tr

    def body(idx_ref, p0_ref, p1_ref, p2_ref, p3_ref, o_ref):
        acc = p0_ref[...].astype(F32)
        for p_ref in (p1_ref, p2_ref, p3_ref):
            acc = acc + p_ref[...].astype(F32)
        o_ref[...] = acc

    slot = lambda which: pl.BlockSpec((None, tr, cols), lambda i, s, which=which: (s[which], i, 0))
    return pl.pallas_call(
        body, name=name, out_shape=jax.ShapeDtypeStruct((2 * half, cols), F32),
        grid_spec=pltpu.PrefetchScalarGridSpec(
            num_scalar_prefetch=1, grid=(nrb,),
            in_specs=[slot(1), slot(2), slot(3), slot(4)],
            out_specs=pl.BlockSpec((tr, cols), lambda i, s: (s[0] * nrb + i, 0))),
        compiler_params=_params(("parallel",)),
    )(idx, partial, chip_sums, chip_sums, chip_sums)


def _small_sum(name, packs):
    n, rows, _ = packs.shape

    def body(p_ref, o_ref):
        acc = p_ref[0]
        for k in range(1, n):
            acc = acc + p_ref[k]
        o_ref[...] = acc

    return pl.pallas_call(
        body, name=name, out_shape=jax.ShapeDtypeStruct((rows, LANES), F32),
        in_specs=[pl.BlockSpec(memory_space=pltpu.VMEM)], out_specs=pl.BlockSpec(memory_space=pltpu.VMEM),
        compiler_params=pltpu.CompilerParams(vmem_limit_bytes=VMEM_LIMIT),
    )(packs)


def _adamw_math(w, g, m, v):
    m_new = ADAM_B1 * m + (1.0 - ADAM_B1) * g
    v_new = ADAM_B2 * v + (1.0 - ADAM_B2) * (g * g)
    m_hat = m_new / (1.0 - ADAM_B1 ** ADAM_STEP)
    v_hat = v_new / (1.0 - ADAM_B2 ** ADAM_STEP)
    return -ADAM_LR * (m_hat / (jnp.sqrt(v_hat) + ADAM_EPS) + ADAM_WD * w), m_new, v_new


def _adamw_many(name, ws, gs, ms, vs):
    n = len(ws)

    def body(*refs):
        for i in range(n):
            d, m_new, v_new = _adamw_math(refs[i][...], refs[n + i][...], refs[2 * n + i][...],
                                          refs[3 * n + i][...])
            refs[4 * n + i][...] = d
            refs[5 * n + i][...] = m_new
            refs[6 * n + i][...] = v_new

    vmem = pl.BlockSpec(memory_space=pltpu.VMEM)
    res = pl.pallas_call(
        body, name=name, in_specs=[vmem] * (4 * n), out_specs=[vmem] * (3 * n),
        out_shape=[jax.ShapeDtypeStruct(w.shape, F32) for w in ws] * 3,
        compiler_params=pltpu.CompilerParams(vmem_limit_bytes=VMEM_LIMIT),
    )(*ws, *gs, *ms, *vs)
    return res[:n], res[n:2 * n], res[2 * n:]


def _adamw(name, w, g, m, v):
    rows, cols = w.shape
    tr = rows if rows * cols * 4 <= (2 << 20) else _row_tile(rows)

    def body(w_ref, g_ref, m_ref, v_ref, g_out_ref, d_ref, nm_ref, nv_ref):
        gv = g_ref[...]
        g_out_ref[...] = gv
        d_ref[...], nm_ref[...], nv_ref[...] = _adamw_math(w_ref[...], gv, m_ref[...], v_ref[...])

    blk = pl.BlockSpec((tr, cols), lambda i: (i, 0))
    sds = jax.ShapeDtypeStruct((rows, cols), F32)
    return pl.pallas_call(
        body, name=name, grid=(rows // tr,), in_specs=[blk] * 4, out_specs=[blk] * 4, out_shape=[sds] * 4,
        compiler_params=_params(("parallel",)),
    )(w, g, m, v)


def _train_step(x, target, W, M, V):
    xi, yi, ci = lax.axis_index("x"), lax.axis_index("y"), lax.axis_index("c")
    me = 2 * xi + yi
    big = {n: W[n][0] for n in BIG}
    big_m = {n: M[n][0] for n in BIG}
    big_v = {n: V[n][0] for n in BIG}

    others = [jnp.where(jnp.int32(i) >= me, i + 1, i) for i in range(N_CHIPS - 1)]
    idx = jnp.stack([ci, me] + others).astype(jnp.int32)

    sharded_shapes = [W[n].shape[1:] for n in SMALL_SHARDED]
    small_pack = _pack([W[n][0] for n in SMALL_SHARDED], _pack_rows(sharded_shapes))
    small_land = lax.dynamic_update_slice(jnp.zeros((N_CHIPS,) + small_pack.shape, F32), small_pack[None],
                                          (me, 0, 0))
    replicated = {n: (W[n].reshape(1, -1) if W[n].ndim <= 2 else W[n][0]) for n in SMALL_REPLICATED}

    early = ("w_up", "w_down")
    small_late = "norm_mix_g"
    small_early = tuple(n for n in SMALL if n != small_late)
    global_shape = lambda n: ((W[n].shape[1], W[n].shape[2] * N_CHIPS) if n in SMALL_SHARDED else
                              tuple(W[n].shape) if W[n].ndim == 1 else tuple(W[n].shape[1:]))
    small_shapes = [global_shape(n) for n in small_early]

    def shard_major(n, g):
        if n == "w_in":
            return _w_in_local_to_shards(g)
        return g if g.ndim == 3 else g.reshape((N_CHIPS, -1) + g.shape[1:])

    class _SplitComm:
        def reduce_early(self, grads):
            self.e_sems, self.e_bufs, token = _pair_start("pair_start_early",
                                                          [shard_major(n, grads[n]) for n in early])
            return token

        def reduce_early_mid(self, after):
            n = len(early)
            bufs, _ = _pair_wait("pair_wait_early", n, self.e_bufs, self.e_sems, after)
            self.e_grads, self.e_recv = bufs[:n], bufs[n:2 * n]
            partial = [_pair_sum(f"pair_sum_{nm}", idx, g, r) for nm, g, r in zip(early, self.e_grads, self.e_recv)]
            self.e_sems, self.e_bufs, token = _chip_start("chip_start_early", partial)
            return token

        def reduce_late(self, grads):
            pack = _pack([grads[n] for n in small_early], _pack_rows(small_shapes))
            self.l_sems, self.l_bufs, token = _pair_start(
                "pair_start_late", [grads["w_in"][None], shard_major("w_out", grads["w_out"])], extra=pack)
            return token

        def reduce_late_mid(self, after):
            bufs, _ = _pair_wait("pair_wait_late", 2, self.l_bufs, self.l_sems, after)
            self.l_grads, self.l_recv = bufs[1], bufs[3]
            partial = [_w_in_local_to_shards(_pair_sum_all("pair_sum_w_in", idx, bufs[0], bufs[2])),
                       _pair_sum("pair_sum_w_out", idx, bufs[1], bufs[3])]
            self.l_sems, self.l_bufs, token = _chip_start("chip_start_late", partial,
                                                          small=_small_pair_sum(idx, bufs[4], bufs[5]))
            return token

        def finish_early(self, after):
            n = len(early)
            bufs, _ = _chip_wait("chip_wait_early", n, self.e_bufs, self.e_sems, after)
            halves = [_final_sum(f"final_sum_{nm}", idx, g, r, p)
                      for nm, g, r, p in zip(early, self.e_grads, self.e_recv, bufs[n:2 * n])]
            return dict(zip(early, _pair_share("pair_share_early", halves)))

        def finish_late(self, after, late):
            bufs, _ = _chip_wait("chip_wait_late", 2, self.l_bufs, self.l_sems, after)
            halves = [_final_sum_bf16("final_sum_w_in", idx, bufs[0], bufs[2]),
                      _final_sum("final_sum_w_out", idx, self.l_grads, self.l_recv, bufs[3])]
            small = dict(zip(small_early, _unpack(_small_sum("small_sum", bufs[4]), small_shapes)))
            w_in_whole, w_out_whole, late_all = _pair_share("pair_share_late", halves, late)
            return {"w_in": w_in_whole, "w_out": w_out_whole}, small, _small_sum("late_sum", late_all)

        def begin(self):
            first = {0: _place_own_shard("place_w_in", idx, big["w_in"]), 4: small_land}
            self.sems, self.lands, token = _gather_start("gather_start_0", first, (0,))
            rest = {i: _place_own_shard(f"place_{BIG[i]}", idx, big[BIG[i]], after=token) for i in (1, 2, 3)}
            sems, lands, token = _gather_start("gather_start_1", rest, (1, 2, 3), after=token)
            self.sems.update(sems)
            self.lands.update(lands)
            return token

        def mid(self, grp, after):
            if grp == 0:
                after = [after, big_m["w_in"], big_v["w_in"]]
            self.pending = _gather_mid(grp, self.lands, self.sems[grp], after)
            return self.pending[2]

        def end(self, grp, after):
            sems, bufs, _ = self.pending
            bufs, _ = _gather_end(grp, bufs, sems, after)
            if grp == 0:
                per_chip = [_unpack(bufs[1][j], sharded_shapes) for j in range(N_CHIPS)]
                out = {n: jnp.concatenate([per_chip[j][i] for j in range(N_CHIPS)], axis=1)
                       for i, n in enumerate(SMALL_SHARDED)}
                out["w_in"] = _w_in_shards_to_local(bufs[0])
                return out
            if grp == 2:
                return {"w_up": bufs[0]}
            return {("w_out" if grp == 1 else "w_down"): bufs[0].reshape(-1, bufs[0].shape[-1])}

    comm = _SplitComm()
    loss, grad_x, grads = _local_step(x[0], target[0], replicated, comm)
    loss = lax.psum(loss[0, 0], ("x", "y", "c"))
    out_g, out_d, out_m, out_v = {}, {}, {}, {}

    def update_big(n, grad):
        g, d, nm, nv = _adamw(f"adamw_{n}", big[n], grad, big_m[n], big_v[n])
        out_g[n], out_d[n], out_m[n], out_v[n] = g[None], d[None], nm[None], nv[None]
        return d

    early_grads = comm.finish_early(grad_x)
    for n in early:
        last = update_big(n, early_grads[n])
    late = _pack([grads[small_late]], _pack_rows([global_shape(small_late)]))
    late_grads, small_grads, late_sum = comm.finish_late(last, late)
    for n in ("w_in", "w_out"):
        update_big(n, late_grads[n])
    small_grads[small_late] = _unpack(late_sum, [global_shape(small_late)])[0]
    for n in SMALL_SHARDED:
        width = W[n].shape[2]
        small_grads[n] = lax.dynamic_slice_in_dim(small_grads[n], me * width, width, axis=1)

    for n in SMALL:
        out_g[n] = small_grads[n].reshape(W[n].shape)
    two_d = lambda a: a.reshape(1, -1) if a.ndim == 1 else a
    results = _adamw_many("adamw_small", *[[two_d(src[n]) for n in SMALL] for src in (W, out_g, M, V)])
    for dst, arrs in zip((out_d, out_m, out_v), results):
        dst.update({n: a.reshape(W[n].shape) for n, a in zip(SMALL, arrs)})
    return (loss, grad_x[None], *[out_g[n] for n in WEIGHT_NAMES], *[out_d[n] for n in WEIGHT_NAMES],
            *[out_m[n] for n in WEIGHT_NAMES], *[out_v[n] for n in WEIGHT_NAMES])


def kernel(x, norm_mix_g, w_in, b_gate_m, mlstm_norm_g, lru_conv_w, lru_conv_b, lru_wa, lru_ba, lru_wx, lru_bx, lru_lambda, w_out, norm_ffn_g, w_up, ffn_conv_w, ffn_conv_b, w_down, norm_final_g, loss_target, m_norm_mix_g, m_w_in, m_b_gate_m, m_mlstm_norm_g, m_lru_conv_w, m_lru_conv_b, m_lru_wa, m_lru_ba, m_lru_wx, m_lru_bx, m_lru_lambda, m_w_out, m_norm_ffn_g, m_w_up, m_ffn_conv_w, m_ffn_conv_b, m_w_down, m_norm_final_g, v_norm_mix_g, v_w_in, v_b_gate_m, v_mlstm_norm_g, v_lru_conv_w, v_lru_conv_b, v_lru_wa, v_lru_ba, v_lru_wx, v_lru_bx, v_lru_lambda, v_w_out, v_norm_ffn_g, v_w_up, v_ffn_conv_w, v_ffn_conv_b, v_w_down, v_norm_final_g):
    W = dict(zip(WEIGHT_NAMES, (norm_mix_g, w_in, b_gate_m, mlstm_norm_g, lru_conv_w, lru_conv_b, lru_wa, lru_ba,
                                lru_wx, lru_bx, lru_lambda, w_out, norm_ffn_g, w_up, ffn_conv_w, ffn_conv_b,
                                w_down, norm_final_g)))
    M = dict(zip(WEIGHT_NAMES, (m_norm_mix_g, m_w_in, m_b_gate_m, m_mlstm_norm_g, m_lru_conv_w, m_lru_conv_b,
                                m_lru_wa, m_lru_ba, m_lru_wx, m_lru_bx, m_lru_lambda, m_w_out, m_norm_ffn_g,
                                m_w_up, m_ffn_conv_w, m_ffn_conv_b, m_w_down, m_norm_final_g)))
    V = dict(zip(WEIGHT_NAMES, (v_norm_mix_g, v_w_in, v_b_gate_m, v_mlstm_norm_g, v_lru_conv_w, v_lru_conv_b,
                                v_lru_wa, v_lru_ba, v_lru_wx, v_lru_bx, v_lru_lambda, v_w_out, v_norm_ffn_g,
                                v_w_up, v_ffn_conv_w, v_ffn_conv_b, v_w_down, v_norm_final_g)))
    return _train_step(x, loss_target, W, M, V)
```

```python
import functools

import jax
import jax.numpy as jnp
from jax import lax
from jax.experimental import pallas as pl
from jax.experimental.pallas import tpu as pltpu

F32 = jnp.float32
BF16 = jnp.bfloat16
MESH = pl.DeviceIdType.MESH

EPS = 1e-6
CHUNK = 512
HEADS = 4
HEAD_DIM = 256
D_MLSTM = HEADS * HEAD_DIM
LRU_BLOCKS = 8
LRU_BLOCK_DIM = 128
D_LRU = LRU_BLOCKS * LRU_BLOCK_DIM
LRU_C = 8.0
LRU_CONV = 4
FFN_CONV = 3
ADAM_LR = 0.001
ADAM_B1 = 0.9
ADAM_B2 = 0.999
ADAM_EPS = 1e-08
ADAM_WD = 0.01
ADAM_STEP = 10

N_CHIPS = 4
N_DEV = 8
LANES = 128
HALO = 8
PROJ_GATE_PAD = LANES
_QKVO = 4 * D_MLSTM
_N_GATES = 2 * HEADS
_PROJ_COLS = _QKVO + _N_GATES + 2 * D_LRU
_PROJ_PAD = _QKVO + 2 * D_LRU + PROJ_GATE_PAD
VMEM_LIMIT = 48 * 1024 * 1024
ANY = pl.BlockSpec(memory_space=pl.ANY)


def _params(sem, vmem=VMEM_LIMIT):
    return pltpu.CompilerParams(dimension_semantics=sem, vmem_limit_bytes=vmem)


def _matmul(name, a, b, grid, a_spec, b_spec, o_spec, out_sds, contract, res=None, res_spec=None, after=None):
    nk = grid[2]
    acc_shape = tuple(d for d in o_spec.block_shape if d is not None)

    def body(*refs):
        refs = list(refs)
        a_ref, b_ref = refs[:2]
        r_ref = refs[2] if res is not None else None
        o_ref = refs[-1] if nk == 1 else refs[-2]
        acc_ref = None if nk == 1 else refs[-1]
        k = pl.program_id(2)

        def part():
            return lax.dot_general(a_ref[...], b_ref[...], (contract, ((), ())), preferred_element_type=F32)

        def finish(r):
            if r_ref is not None:
                r = r_ref[...] + r
            o_ref[...] = r.astype(o_ref.dtype)

        if nk == 1:
            finish(part())
            return

        @pl.when(k == 0)
        def _():
            acc_ref[...] = part()

        @pl.when(jnp.logical_and(k > 0, k < nk - 1))
        def _():
            acc_ref[...] += part()

        @pl.when(k == nk - 1)
        def _():
            finish(acc_ref[...] + part())

    in_specs = [a_spec, b_spec] + ([] if res is None else [res_spec]) + ([] if after is None else [ANY])
    args = (a, b) + (() if res is None else (res,)) + (() if after is None else (after,))
    if after is not None:
        inner = body
        body = lambda *refs: inner(*refs[:len(in_specs) - 1], *refs[len(in_specs):])
    return pl.pallas_call(
        body, out_shape=out_sds, grid=grid, in_specs=in_specs, out_specs=o_spec,
        scratch_shapes=[] if nk == 1 else [pltpu.VMEM(acc_shape, F32)], name=name,
        compiler_params=_params(("parallel", "parallel", "arbitrary")),
    )(*args)


NN = ((1,), (0,))
NT = ((1,), (1,))
TN = ((0,), (0,))


def _mm_nn(name, a, b, tm, tn, tk, out_dtype=F32, res=None):
    m, k = a.shape
    n = b.shape[1]
    return _matmul(name, a, b, (m // tm, n // tn, k // tk),
                   pl.BlockSpec((tm, tk), lambda i, j, kk: (i, kk)),
                   pl.BlockSpec((tk, tn), lambda i, j, kk: (kk, j)),
                   pl.BlockSpec((tm, tn), lambda i, j, kk: (i, j)),
                   jax.ShapeDtypeStruct((m, n), out_dtype), NN,
                   res=res, res_spec=pl.BlockSpec((tm, tn), lambda i, j, kk: (i, j)))


def _mm_nt(name, a, b, tm, tn, tk, out_dtype=F32, res=None, after=None):
    m, k = a.shape
    n = b.shape[0]
    return _matmul(name, a, b, (m // tm, n // tn, k // tk),
                   pl.BlockSpec((tm, tk), lambda i, j, kk: (i, kk)),
                   pl.BlockSpec((tn, tk), lambda i, j, kk: (j, kk)),
                   pl.BlockSpec((tm, tn), lambda i, j, kk: (i, j)),
                   jax.ShapeDtypeStruct((m, n), out_dtype), NT,
                   res=res, res_spec=pl.BlockSpec((tm, tn), lambda i, j, kk: (i, j)), after=after)


def _mm_tn(name, a, b, tm, tn, tk, out_dtype=F32):
    k, m = a.shape
    n = b.shape[1]
    tk = min(tk, k)
    return _matmul(name, a, b, (m // tm, n // tn, k // tk),
                   pl.BlockSpec((tk, tm), lambda i, j, kk: (kk, i)),
                   pl.BlockSpec((tk, tn), lambda i, j, kk: (kk, j)),
                   pl.BlockSpec((tm, tn), lambda i, j, kk: (i, j)),
                   jax.ShapeDtypeStruct((m, n), out_dtype), TN)


def _up_shard(n):
    return 2 * (n % 2) + (n // 2) // 2, (n // 2) % 2


def _mm_up_fwd(name, a, wg_up, tm, tk):
    m, k = a.shape
    _, _, cols = wg_up.shape
    tn = cols // 2
    return _matmul(name, a, wg_up, (m // tm, 2 * N_CHIPS, k // tk),
                   pl.BlockSpec((tm, tk), lambda i, j, kk: (i, kk)),
                   pl.BlockSpec((None, tk, tn), lambda i, j, kk: (_up_shard(j)[0], kk, _up_shard(j)[1])),
                   pl.BlockSpec((tm, tn), lambda i, j, kk: (i, j)),
                   jax.ShapeDtypeStruct((m, 2 * N_CHIPS * tn), F32), NN)


def _mm_up_bwd_x(name, dgu, wg_up, tm, tn):
    m, _ = dgu.shape
    _, d, cols = wg_up.shape
    tk = cols // 2
    nk = N_CHIPS

    def body(a_ref, bg_ref, bu_ref, o_ref, acc_ref):
        k = pl.program_id(2)

        def part():
            dims = (NT, ((), ()))
            return (lax.dot_general(a_ref[:, :tk], bg_ref[...], dims, preferred_element_type=F32)
                    + lax.dot_general(a_ref[:, tk:], bu_ref[...], dims, preferred_element_type=F32))

        @pl.when(k == 0)
        def _():
            acc_ref[...] = part()

        @pl.when(jnp.logical_and(k > 0, k < nk - 1))
        def _():
            acc_ref[...] += part()

        @pl.when(k == nk - 1)
        def _():
            o_ref[...] = acc_ref[...] + part()

    wspec = lambda half: pl.BlockSpec(
        (None, tn, tk), lambda i, j, kk: (_up_shard(2 * kk + half)[0], j, _up_shard(2 * kk + half)[1]))
    return pl.pallas_call(
        body, name=name, grid=(m // tm, d // tn, nk), out_shape=jax.ShapeDtypeStruct((m, d), F32),
        in_specs=[pl.BlockSpec((tm, 2 * tk), lambda i, j, kk: (i, kk)), wspec(0), wspec(1)],
        out_specs=pl.BlockSpec((tm, tn), lambda i, j, kk: (i, j)),
        scratch_shapes=[pltpu.VMEM((tm, tn), F32)],
        compiler_params=_params(("parallel", "parallel", "arbitrary")),
    )(dgu, wg_up, wg_up)


def _mm_up_bwd_w(name, n2, dgu, tm, tk):
    s, d = n2.shape
    tk = min(tk, s)
    tn = dgu.shape[1] // (2 * N_CHIPS)
    return _matmul(name, n2, dgu, (d // tm, 2 * N_CHIPS, s // tk),
                   pl.BlockSpec((tk, tm), lambda i, j, kk: (kk, i)),
                   pl.BlockSpec((tk, tn), lambda i, j, kk: (kk, j)),
                   pl.BlockSpec((None, tm, tn), lambda i, j, kk: (_up_shard(j)[0], i, _up_shard(j)[1])),
                   jax.ShapeDtypeStruct((N_CHIPS, d, 2 * tn), F32), TN)


def _rmsnorm_fwd(name, x, g, tm=256):
    s, d = x.shape

    def body(x_ref, g_ref, n_ref, r_ref):
        xf = x_ref[...]
        r = lax.rsqrt(jnp.mean(xf * xf, axis=-1, keepdims=True) + EPS)
        n_ref[...] = ((xf * r) * g_ref[...]).astype(BF16)
        r_ref[...] = r

    return pl.pallas_call(
        body, grid=(s // tm,), name=name,
        in_specs=[pl.BlockSpec((tm, d), lambda i: (i, 0)), pl.BlockSpec((1, d), lambda i: (0, 0))],
        out_specs=[pl.BlockSpec((tm, d), lambda i: (i, 0)), pl.BlockSpec((tm, 1), lambda i: (i, 0))],
        out_shape=[jax.ShapeDtypeStruct((s, d), BF16), jax.ShapeDtypeStruct((s, 1), F32)],
        compiler_params=_params(("parallel",)),
    )(x, g)


def _rmsnorm_bwd(name, x, rstd, g, dn, dres, tm=256):
    s, d = x.shape

    def body(x_ref, r_ref, g_ref, dn_ref, dres_ref, dx_ref, dxb_ref, dg_ref):
        @pl.when(pl.program_id(0) == 0)
        def _():
            dg_ref[...] = jnp.zeros_like(dg_ref)

        r = r_ref[...]
        xhat = x_ref[...] * r
        dn_v = dn_ref[...]
        dxhat = dn_v * g_ref[...]
        dx = dres_ref[...] + r * (dxhat - xhat * jnp.mean(dxhat * xhat, axis=-1, keepdims=True))
        dx_ref[...] = dx
        dxb_ref[...] = dx.astype(BF16)
        dg_ref[...] += jnp.sum(dn_v * xhat, axis=0, keepdims=True)

    row = pl.BlockSpec((tm, d), lambda i: (i, 0))
    vec = pl.BlockSpec((1, d), lambda i: (0, 0))
    return pl.pallas_call(
        body, grid=(s // tm,), name=name,
        in_specs=[row, pl.BlockSpec((tm, 1), lambda i: (i, 0)), vec, row, row],
        out_specs=[row, row, vec],
        out_shape=[jax.ShapeDtypeStruct((s, d), F32), jax.ShapeDtypeStruct((s, d), BF16),
                   jax.ShapeDtypeStruct((1, d), F32)],
        compiler_params=_params(("arbitrary",)),
    )(x, rstd, g, dn, dres)


def _loss_head(name, x, g, target, tm=256):
    s, d = x.shape

    def body(x_ref, g_ref, t_ref, loss_ref, dx_ref, dxb_ref, dg_ref):
        @pl.when(pl.program_id(0) == 0)
        def _():
            dg_ref[...] = jnp.zeros_like(dg_ref)
            loss_ref[...] = jnp.zeros_like(loss_ref)

        xf = x_ref[...]
        gv = g_ref[...]
        r = lax.rsqrt(jnp.mean(xf * xf, axis=-1, keepdims=True) + EPS)
        xhat = xf * r
        err = xhat * gv - t_ref[...]
        loss_ref[...] += 0.5 * jnp.sum(jnp.mean(err * err, axis=-1, keepdims=True), axis=0, keepdims=True)
        dy = err * (1.0 / d)
        dxhat = dy * gv
        dx = r * (dxhat - xhat * jnp.mean(dxhat * xhat, axis=-1, keepdims=True))
        dx_ref[...] = dx
        dxb_ref[...] = dx.astype(BF16)
        dg_ref[...] += jnp.sum(dy * xhat, axis=0, keepdims=True)

    row = pl.BlockSpec((tm, d), lambda i: (i, 0))
    vec = pl.BlockSpec((1, d), lambda i: (0, 0))
    return pl.pallas_call(
        body, grid=(s // tm,), name=name,
        in_specs=[row, vec, row],
        out_specs=[pl.BlockSpec((1, 1), lambda i: (0, 0)), row, row, vec],
        out_shape=[jax.ShapeDtypeStruct((1, 1), F32), jax.ShapeDtypeStruct((s, d), F32),
                   jax.ShapeDtypeStruct((s, d), BF16), jax.ShapeDtypeStruct((1, d), F32)],
        compiler_params=_params(("arbitrary",)),
    )(x, g, target)


def _sigmoid(v):
    return 1.0 / (1.0 + jnp.exp(-v))


def _log_sigmoid(v):
    return jnp.minimum(v, 0.0) - jnp.log1p(jnp.exp(-jnp.abs(v)))


def _softplus(v):
    return jnp.maximum(v, 0.0) + jnp.log1p(jnp.exp(-jnp.abs(v)))


def _one_minus_exp(z):
    series = -z * (1.0 + z * (0.5 + z * (1.0 / 6.0 + z * (1.0 / 24.0 + z * (1.0 / 120.0)))))
    return jnp.where(z > -0.1, series, 1.0 - jnp.exp(z))


_GELU_K = 0.7978845608028654
_GELU_C = 0.044715


def _gelu(v):
    return 0.5 * v * (1.0 + jnp.tanh(_GELU_K * (v + _GELU_C * v * v * v)))


def _gelu_grad(v):
    t = jnp.tanh(_GELU_K * (v + _GELU_C * v * v * v))
    return 0.5 * (1.0 + t) + 0.5 * v * (1.0 - t * t) * _GELU_K * (1.0 + 3.0 * _GELU_C * v * v)


def _rows(shape):
    return lax.broadcasted_iota(jnp.int32, shape, 0)


def _cols(shape):
    return lax.broadcasted_iota(jnp.int32, shape, 1)


def _shift_down(v, prev, d):
    if d == 0:
        return v
    rolled = pltpu.roll(v, d, axis=0)
    head = jnp.where(_rows((HALO, v.shape[1])) >= d, rolled[:HALO], pltpu.roll(prev, d, axis=0))
    if v.shape[0] == HALO:
        return head
    return jnp.concatenate([head, rolled[HALO:]], axis=0)


def _shift_up(v, nxt, d):
    if d == 0:
        return v
    n = v.shape[0]
    rolled = pltpu.roll(v, n - d, axis=0)
    tail = jnp.where(_rows((HALO, v.shape[1])) < HALO - d, rolled[n - HALO:], pltpu.roll(nxt, HALO - d, axis=0))
    if n == HALO:
        return tail
    return jnp.concatenate([rolled[:n - HALO], tail], axis=0)


def _dot(a, b, contract):
    return lax.dot_general(a.astype(BF16), b.astype(BF16), (contract, ((), ())), preferred_element_type=F32)


def _mlstm_chunk_common(h, q_ref, k_ref, v_ref, gcol_ref, grow_ref, brow_ref, bcol_ref, m_prev):
    L = CHUNK
    sl = slice(h * HEAD_DIM, (h + 1) * HEAD_DIM)
    qh = q_ref[:, sl]
    kh = k_ref[:, sl]
    vh = v_ref[:, sl]
    qs = qh * (HEAD_DIM ** -0.5)
    gates = gcol_ref[...] + brow_ref[...]
    lane = _cols(gates.shape)
    ic = jnp.sum(jnp.where(lane == h, gates, 0.0), axis=1, keepdims=True)
    fc = jnp.sum(jnp.where(lane == HEADS + h, gates, 0.0), axis=1, keepdims=True)
    ir = grow_ref[h:h + 1, :] + bcol_ref[h:h + 1, :]
    fr = grow_ref[HEADS + h:HEADS + h + 1, :] + bcol_ref[HEADS + h:HEADS + h + 1, :]
    logf_c = _log_sigmoid(fc)
    logf_r = _log_sigmoid(fr)
    t_i = _rows((L, L))
    s_i = _cols((L, L))
    tri = t_i >= s_i
    b_c = jnp.sum(jnp.where(tri, logf_r, 0.0), axis=1, keepdims=True)
    b_r = jnp.sum(jnp.where(t_i <= s_i, logf_c, 0.0), axis=0, keepdims=True)
    btot = jnp.sum(logf_r, axis=1, keepdims=True)
    dmat = jnp.where(tri, b_c - b_r + ir, -jnp.inf)
    m_inter = b_c + m_prev
    m_t = jnp.maximum(m_inter, jnp.max(dmat, axis=1, keepdims=True))
    e_mat = jnp.exp(dmat - m_t)
    e_inter = jnp.exp(m_inter - m_t)
    wqk = _dot(qs, kh, NT) * e_mat
    w_end_r = btot - b_r + ir
    m_loc = jnp.max(w_end_r, axis=1, keepdims=True)
    e_end_c = jnp.exp(btot - b_c + ic - m_loc)
    m_new = jnp.maximum(btot + m_prev, m_loc)
    a_dec = jnp.exp(btot + m_prev - m_new)
    c_inj = jnp.exp(m_loc - m_new)
    return dict(qh=qh, kh=kh, vh=vh, qs=qs, fc=fc, tri=tri, t_i=t_i, s_i=s_i, m_t=m_t, e_mat=e_mat,
                e_inter=e_inter, wqk=wqk, e_end_c=e_end_c, m_new=m_new, a_dec=a_dec, c_inj=c_inj)


def _mlstm_fwd(proj, gates_t, bias_row, bias_col, head_g):
    s = proj.shape[0]
    nc = s // CHUNK
    L = CHUNK

    def body(q_ref, k_ref, v_ref, o_ref, gcol_ref, grow_ref, brow_ref, bcol_ref, hg_ref,
             out_ref, cprev_ref, nprev_ref, mprev_ref, c_scr, n_scr, m_scr):
        @pl.when(pl.program_id(0) == 0)
        def _():
            c_scr[...] = jnp.zeros_like(c_scr)
            n_scr[...] = jnp.zeros_like(n_scr)
            m_scr[...] = jnp.zeros_like(m_scr)

        for h in range(HEADS):
            sl = slice(h * HEAD_DIM, (h + 1) * HEAD_DIM)
            m_prev = m_scr[h:h + 1, 0:1]
            n_prev = n_scr[h:h + 1, :]
            c_prev = c_scr[h].astype(BF16)
            q = _mlstm_chunk_common(h, q_ref, k_ref, v_ref, gcol_ref, grow_ref, brow_ref, bcol_ref, m_prev)
            num = _dot(q["wqk"], q["vh"], NN) + q["e_inter"] * _dot(q["qs"], c_prev, NN)
            den = (jnp.sum(q["wqk"], axis=1, keepdims=True)
                   + q["e_inter"] * jnp.sum(q["qs"] * n_prev, axis=1, keepdims=True))
            hh = num / jnp.maximum(jnp.abs(den), jnp.exp(-q["m_t"]))
            hn = hh * lax.rsqrt(jnp.mean(hh * hh, axis=1, keepdims=True) + EPS) * hg_ref[h:h + 1, :]
            out_ref[:, sl] = (_sigmoid(o_ref[:, sl]) * hn).astype(BF16)
            cprev_ref[h] = c_prev
            nprev_ref[h:h + 1, :] = n_prev
            mprev_ref[h:h + 1, :] = jnp.broadcast_to(m_prev, (1, LANES))
            c_loc = _dot(q["kh"], q["e_end_c"] * q["vh"], TN)
            n_loc = jnp.sum(q["e_end_c"] * q["kh"], axis=0, keepdims=True)
            c_scr[h] = q["a_dec"] * c_scr[h] + q["c_inj"] * c_loc
            n_scr[h:h + 1, :] = q["a_dec"] * n_prev + q["c_inj"] * n_loc
            m_scr[h:h + 1, :] = jnp.broadcast_to(q["m_new"], (1, LANES))

    blk = lambda j: pl.BlockSpec((L, D_MLSTM), lambda c, j=j: (c, j))
    full = lambda shp: pl.BlockSpec(shp, lambda c: tuple(0 for _ in shp))
    return pl.pallas_call(
        body, grid=(nc,), name="mlstm_fwd",
        in_specs=[blk(0), blk(1), blk(2), blk(3),
                  pl.BlockSpec((L, LANES), lambda c: (c, (4 * D_MLSTM + 2 * D_LRU) // LANES)),
                  pl.BlockSpec((None, 2 * HEADS, L), lambda c: (c, 0, 0)),
                  full((1, LANES)), full((2 * HEADS, 1)), full((HEADS, HEAD_DIM))],
        out_specs=[pl.BlockSpec((L, D_MLSTM), lambda c: (c, 0)),
                   pl.BlockSpec((None, HEADS, HEAD_DIM, HEAD_DIM), lambda c: (c, 0, 0, 0)),
                   pl.BlockSpec((None, HEADS, HEAD_DIM), lambda c: (c, 0, 0)),
                   pl.BlockSpec((None, HEADS, LANES), lambda c: (c, 0, 0))],
        out_shape=[jax.ShapeDtypeStruct((s, D_MLSTM + D_LRU), BF16),
                   jax.ShapeDtypeStruct((nc, HEADS, HEAD_DIM, HEAD_DIM), BF16),
                   jax.ShapeDtypeStruct((nc, HEADS, HEAD_DIM), F32),
                   jax.ShapeDtypeStruct((nc, HEADS, LANES), F32)],
        scratch_shapes=[pltpu.VMEM((HEADS, HEAD_DIM, HEAD_DIM), F32), pltpu.VMEM((HEADS, HEAD_DIM), F32),
                        pltpu.VMEM((HEADS, LANES), F32)],
        compiler_params=_params(("arbitrary",)),
    )(proj, proj, proj, proj, proj, gates_t, bias_row, bias_col, head_g)


def _mlstm_bwd(proj, gates_t, bias_row, bias_col, head_g, cprev, nprev, mprev, dmix):
    s = proj.shape[0]
    nc = s // CHUNK
    L = CHUNK

    def body(q_ref, k_ref, v_ref, o_ref, gcol_ref, grow_ref, brow_ref, bcol_ref, hg_ref,
             cprev_ref, nprev_ref, mprev_ref, dmix_ref,
             dqkvo_ref, dgate_ref, dhg_ref, g_scr, gn_scr):
        @pl.when(pl.program_id(0) == 0)
        def _():
            g_scr[...] = jnp.zeros_like(g_scr)
            gn_scr[...] = jnp.zeros_like(gn_scr)
            dhg_ref[...] = jnp.zeros_like(dhg_ref)

        lane = _cols((L, LANES))
        dgate = jnp.zeros((L, LANES), F32)
        for h in range(HEADS):
            sl = slice(h * HEAD_DIM, (h + 1) * HEAD_DIM)
            m_prev = mprev_ref[h:h + 1, 0:1]
            n_prev = nprev_ref[h:h + 1, :]
            c_prev = cprev_ref[h]
            q = _mlstm_chunk_common(h, q_ref, k_ref, v_ref, gcol_ref, grow_ref, brow_ref, bcol_ref, m_prev)
            qh, kh, vh, qs, wqk, e_inter = q["qh"], q["kh"], q["vh"], q["qs"], q["wqk"], q["e_inter"]
            num_state = e_inter * _dot(qs, c_prev, NN)
            den_state = e_inter * jnp.sum(qs * n_prev, axis=1, keepdims=True)
            num = _dot(wqk, vh, NN) + num_state
            den = jnp.sum(wqk, axis=1, keepdims=True) + den_state
            floor = jnp.exp(-q["m_t"])
            denom = jnp.maximum(jnp.abs(den), floor)
            hh = num / denom
            rn = lax.rsqrt(jnp.mean(hh * hh, axis=1, keepdims=True) + EPS)
            hn_pre = hh * rn
            hg = hg_ref[h:h + 1, :]
            sg = _sigmoid(o_ref[:, sl])
            dout = dmix_ref[:, sl]
            d_o = dout * (hn_pre * hg) * sg * (1.0 - sg)
            dhn = dout * sg
            dhg_ref[h:h + 1, :] += jnp.sum(dhn * hn_pre, axis=0, keepdims=True)
            dhn_pre = dhn * hg
            dhh = rn * (dhn_pre - hn_pre * jnp.mean(dhn_pre * hn_pre, axis=1, keepdims=True))
            dnum = dhh / denom
            dden = jnp.where(jnp.abs(den) >= floor,
                             -jnp.sum(hh * dhh, axis=1, keepdims=True) / denom * jnp.sign(den), 0.0)
            dwqk = _dot(dnum, vh, NT) + dden
            dv = _dot(wqk, dnum, TN)
            dp = dwqk * q["e_mat"]
            dqs = _dot(dp, kh, NN) + e_inter * (_dot(dnum, c_prev, NT) + dden * n_prev)
            dk = _dot(dp, qs, TN)
            g_next = g_scr[h]
            gn_next = gn_scr[h:h + 1, :]
            w_state = q["e_end_c"] * q["c_inj"]
            dk_state = w_state * (_dot(vh, g_next, NT) + gn_next)
            dk = dk + dk_state
            dv = dv + w_state * _dot(kh, g_next, NN)
            dq = dqs * (HEAD_DIM ** -0.5)
            eye = q["t_i"] == q["s_i"]
            to_row = lambda col: jnp.sum(jnp.where(eye, col, 0.0), axis=0, keepdims=True)
            to_col = lambda row: jnp.sum(jnp.where(eye, row, 0.0), axis=1, keepdims=True)
            g_pair = dwqk * wqk
            rs_in = jnp.sum(g_pair, axis=1, keepdims=True)
            cs_in_r = jnp.sum(g_pair, axis=0, keepdims=True)
            rs_state = (jnp.sum(dnum * num_state, axis=1, keepdims=True) + dden * den_state)
            cs_state = jnp.sum(kh * dk_state, axis=1, keepdims=True)
            di_c = to_col(cs_in_r) + cs_state
            through = q["a_dec"] * (jnp.sum(jnp.sum(g_next * c_prev.astype(F32), axis=1, keepdims=True),
                                            axis=0, keepdims=True)
                                    + jnp.sum(gn_next * n_prev, axis=1, keepdims=True))
            ends_here = to_row(rs_in + rs_state) - cs_in_r
            da_c = (jnp.sum(jnp.where(q["s_i"] >= q["t_i"], ends_here, 0.0), axis=1, keepdims=True)
                    + jnp.sum(jnp.where(q["s_i"] < q["t_i"], to_row(cs_state), 0.0), axis=1, keepdims=True)
                    + through)
            df_c = da_c * _sigmoid(-q["fc"])
            dgate = dgate + jnp.where(lane == h, di_c, 0.0) + jnp.where(lane == HEADS + h, df_c, 0.0)
            dqkvo_ref[:, sl] = dq.astype(BF16)
            dqkvo_ref[:, D_MLSTM + h * HEAD_DIM:D_MLSTM + (h + 1) * HEAD_DIM] = dk.astype(BF16)
            dqkvo_ref[:, 2 * D_MLSTM + h * HEAD_DIM:2 * D_MLSTM + (h + 1) * HEAD_DIM] = dv.astype(BF16)
            dqkvo_ref[:, 3 * D_MLSTM + h * HEAD_DIM:3 * D_MLSTM + (h + 1) * HEAD_DIM] = d_o.astype(BF16)
            g_scr[h] = q["a_dec"] * g_next + _dot(e_inter * qs, dnum, TN)
            gn_scr[h:h + 1, :] = q["a_dec"] * gn_next + jnp.sum(e_inter * qs * dden, axis=0, keepdims=True)
        dgate_ref[...] = dgate

    rev = lambda c: nc - 1 - c
    blk = lambda j: pl.BlockSpec((L, D_MLSTM), lambda c, j=j: (rev(c), j))
    full = lambda shp: pl.BlockSpec(shp, lambda c: tuple(0 for _ in shp))
    return pl.pallas_call(
        body, grid=(nc,), name="mlstm_bwd",
        in_specs=[blk(0), blk(1), blk(2), blk(3),
                  pl.BlockSpec((L, LANES), lambda c: (rev(c), (4 * D_MLSTM + 2 * D_LRU) // LANES)),
                  pl.BlockSpec((None, 2 * HEADS, L), lambda c: (rev(c), 0, 0)),
                  full((1, LANES)), full((2 * HEADS, 1)), full((HEADS, HEAD_DIM)),
                  pl.BlockSpec((None, HEADS, HEAD_DIM, HEAD_DIM), lambda c: (rev(c), 0, 0, 0)),
                  pl.BlockSpec((None, HEADS, HEAD_DIM), lambda c: (rev(c), 0, 0)),
                  pl.BlockSpec((None, HEADS, LANES), lambda c: (rev(c), 0, 0)),
                  pl.BlockSpec((L, D_MLSTM), lambda c: (rev(c), 0))],
        out_specs=[pl.BlockSpec((L, 4 * D_MLSTM), lambda c: (rev(c), 0)),
                   pl.BlockSpec((L, LANES), lambda c: (rev(c), 0)),
                   full((HEADS, HEAD_DIM))],
        out_shape=[jax.ShapeDtypeStruct((s, _PROJ_PAD), BF16),
                   jax.ShapeDtypeStruct((s, LANES), F32),
                   jax.ShapeDtypeStruct((HEADS, HEAD_DIM), F32)],
        scratch_shapes=[pltpu.VMEM((HEADS, HEAD_DIM, HEAD_DIM), F32), pltpu.VMEM((HEADS, HEAD_DIM), F32)],
        compiler_params=_params(("arbitrary",)),
    )(proj, proj, proj, proj, proj, gates_t, bias_row, bias_col, head_g, cprev, nprev, mprev, dmix)


def _lru_gates(xc, wa_ref, wx_ref, ba, bx, lam):
    r = _sigmoid(_dot(xc, wa_ref[...], NN) + ba)
    ig = _sigmoid(_dot(xc, wx_ref[...], NN) + bx)
    sp = _softplus(-lam)
    log_a = (-LRU_C * r) * sp
    a = jnp.exp(log_a)
    mult = jnp.sqrt(_one_minus_exp(2.0 * log_a))
    return r, ig, sp, a, mult


def _lru_conv(xr, prev, w_ref, b):
    xc = b + _shift_down(xr, prev, 3) * w_ref[0:1, :]
    for j in range(1, LRU_CONV):
        xc = xc + _shift_down(xr, prev, LRU_CONV - 1 - j) * w_ref[j:j + 1, :]
    return xc


def _lru_fwd(proj, mix, conv_w, conv_b, wa, wx, ba, bx, lam, tt=1024):
    s = proj.shape[0]
    tt = min(tt, s)
    nt = s // tt
    B = LRU_BLOCK_DIM
    lru_col = 4 * D_MLSTM // B
    mix_col = D_MLSTM // B

    def body(xr_ref, gr_ref, cw_ref, cb_ref, wa_ref, wx_ref, ba_ref, bx_ref, lam_ref, mix_in_ref,
             out_ref, h_ref, prev_scr, hcar_scr):
        @pl.when(pl.program_id(1) == 0)
        def _():
            prev_scr[...] = jnp.zeros_like(prev_scr)
            hcar_scr[...] = jnp.zeros_like(hcar_scr)

        xr = xr_ref[...]
        xc = _lru_conv(xr, prev_scr[...], cw_ref, cb_ref[...])
        prev_scr[...] = xr[tt - HALO:, :]
        _, ig, _, a, mult = _lru_gates(xc, wa_ref, wx_ref, ba_ref[...], bx_ref[...], lam_ref[...])
        u = mult * (ig * xc)
        rows = _rows((tt, B))
        acc_a, acc_b = a, u
        d = 1
        while d < tt:
            keep = rows >= d
            sh_a = jnp.where(keep, pltpu.roll(acc_a, d, axis=0), 1.0)
            sh_b = jnp.where(keep, pltpu.roll(acc_b, d, axis=0), 0.0)
            acc_b = acc_a * sh_b + acc_b
            acc_a = acc_a * sh_a
            d *= 2
        hv = acc_b + acc_a * hcar_scr[0:1, :]
        hcar_scr[...] = jnp.broadcast_to(hv[tt - 1:tt, :], hcar_scr.shape)
        h_ref[...] = hv
        out_ref[...] = (hv * _gelu(gr_ref[...])).astype(BF16)

    chan = lambda rws: pl.BlockSpec((rws, B), lambda n, i: (0, n))
    return pl.pallas_call(
        body, grid=(LRU_BLOCKS, nt), name="lru_fwd",
        in_specs=[pl.BlockSpec((tt, B), lambda n, i: (i, lru_col + 2 * n)),
                  pl.BlockSpec((tt, B), lambda n, i: (i, lru_col + 2 * n + 1)),
                  chan(LRU_CONV), chan(1),
                  pl.BlockSpec((None, B, B), lambda n, i: (n, 0, 0)),
                  pl.BlockSpec((None, B, B), lambda n, i: (n, 0, 0)),
                  chan(1), chan(1), chan(1), ANY],
        out_specs=[pl.BlockSpec((tt, B), lambda n, i: (i, mix_col + n)), pl.BlockSpec((tt, B), lambda n, i: (i, n))],
        out_shape=[jax.ShapeDtypeStruct(mix.shape, BF16), jax.ShapeDtypeStruct((s, D_LRU), F32)],
        scratch_shapes=[pltpu.VMEM((HALO, B), F32), pltpu.VMEM((HALO, B), F32)],
        input_output_aliases={9: 0},
        compiler_params=_params(("parallel", "arbitrary")),
    )(proj, proj, conv_w, conv_b, wa, wx, ba, bx, lam, mix)


def _lru_bwd(proj, hsave, dmix, dproj, conv_w, conv_b, wa, wx, ba, bx, lam, tt=1024):
    s = proj.shape[0]
    tt = min(tt, s)
    nt = s // tt
    B = LRU_BLOCK_DIM
    lru_col = 4 * D_MLSTM // B
    dmix_col = D_MLSTM // B
    hpb = tt // HALO

    def body(xr_ref, xprev_ref, gr_ref, h_ref, hprev_ref, dmix_ref, cw_ref, cb_ref, wa_ref, wx_ref,
             ba_ref, bx_ref, lam_ref, dproj_in_ref,
             dxg_ref, dcw_ref, dcb_ref, dwa_ref, dwx_ref, dba_ref, dbx_ref, dlam_ref,
             gcar_scr, acar_scr, dxc_scr):
        i = pl.program_id(1)
        first_tile = i == nt - 1

        @pl.when(i == 0)
        def _():
            gcar_scr[...] = jnp.zeros_like(gcar_scr)
            acar_scr[...] = jnp.zeros_like(acar_scr)
            dxc_scr[...] = jnp.zeros_like(dxc_scr)
            for ref in (dcw_ref, dcb_ref, dwa_ref, dwx_ref, dba_ref, dbx_ref, dlam_ref):
                ref[...] = jnp.zeros_like(ref)

        xr = xr_ref[...]
        xprev = jnp.where(first_tile, 0.0, xprev_ref[...])
        hprev = jnp.where(first_tile, 0.0, hprev_ref[...])
        lam = lam_ref[...]
        taps = [_shift_down(xr, xprev, LRU_CONV - 1 - j) for j in range(LRU_CONV)]
        xc = cb_ref[...] + taps[0] * cw_ref[0:1, :]
        for j in range(1, LRU_CONV):
            xc = xc + taps[j] * cw_ref[j:j + 1, :]
        r, ig, sp, a, mult = _lru_gates(xc, wa_ref, wx_ref, ba_ref[...], bx_ref[...], lam)
        gr = gr_ref[...]
        hv = h_ref[...]
        dout = dmix_ref[...]
        dxg_ref[:, B:] = (dout * hv * _gelu_grad(gr)).astype(BF16)
        dh = dout * _gelu(gr)
        rows = _rows((tt, B))
        acc_a = _shift_up(a, acar_scr[...], 1)
        acc_b = dh
        d = 1
        while d < tt:
            keep = rows < tt - d
            sh_a = jnp.where(keep, pltpu.roll(acc_a, tt - d, axis=0), 1.0)
            sh_b = jnp.where(keep, pltpu.roll(acc_b, tt - d, axis=0), 0.0)
            acc_b = acc_a * sh_b + acc_b
            acc_a = acc_a * sh_a
            d *= 2
        gv = acc_b + acc_a * gcar_scr[0:1, :]
        gcar_scr[...] = jnp.broadcast_to(gv[0:1, :], gcar_scr.shape)
        acar_scr[...] = jnp.broadcast_to(a[0:1, :], acar_scr.shape)
        h_before = _shift_down(hv, hprev, 1)
        da = gv * h_before
        dmult = gv * (ig * xc)
        dig = gv * mult * xc
        dxc = gv * mult * ig
        dlog_a = da * a - dmult * (a * a) / mult
        dr = dlog_a * (-LRU_C * sp)
        dlam_ref[...] += jnp.sum(dlog_a * (-LRU_C * r), axis=0, keepdims=True) * (-_sigmoid(-lam))
        dpre_r = dr * r * (1.0 - r)
        dpre_i = dig * ig * (1.0 - ig)
        dba_ref[...] += jnp.sum(dpre_r, axis=0, keepdims=True)
        dbx_ref[...] += jnp.sum(dpre_i, axis=0, keepdims=True)
        dwa_ref[...] += _dot(xc, dpre_r, TN)
        dwx_ref[...] += _dot(xc, dpre_i, TN)
        dxc = dxc + _dot(dpre_r, wa_ref[...], NT) + _dot(dpre_i, wx_ref[...], NT)
        dcb_ref[...] += jnp.sum(dxc, axis=0, keepdims=True)
        nxt = dxc_scr[...]
        dxr = jnp.zeros((tt, B), F32)
        for j in range(LRU_CONV):
            sft = LRU_CONV - 1 - j
            dcw_ref[j:j + 1, :] += jnp.sum(dxc * taps[j], axis=0, keepdims=True)
            dxr = dxr + _shift_up(dxc, nxt, sft) * cw_ref[j:j + 1, :]
        dxc_scr[...] = dxc[:HALO, :]
        dxg_ref[:, :B] = dxr.astype(BF16)

    rev = lambda i: nt - 1 - i
    tile = lambda col, step: pl.BlockSpec((tt, B), lambda n, i: (rev(i), col + step * n))
    halo = lambda col, step: pl.BlockSpec(
        (HALO, B), lambda n, i: (jnp.maximum(rev(i) * hpb - 1, 0), col + step * n))
    chan = lambda rws: pl.BlockSpec((rws, B), lambda n, i: (0, n))
    wblk = pl.BlockSpec((None, B, B), lambda n, i: (n, 0, 0))
    return pl.pallas_call(
        body, grid=(LRU_BLOCKS, nt), name="lru_bwd",
        in_specs=[tile(lru_col, 2), halo(lru_col, 2), tile(lru_col + 1, 2), tile(0, 1), halo(0, 1),
                  tile(dmix_col, 1), chan(LRU_CONV), chan(1), wblk, wblk, chan(1), chan(1), chan(1), ANY],
        out_specs=[pl.BlockSpec((tt, 2 * B), lambda n, i: (rev(i), lru_col // 2 + n)),
                   chan(LRU_CONV), chan(1), wblk, wblk, chan(1), chan(1), chan(1)],
        out_shape=[jax.ShapeDtypeStruct(dproj.shape, BF16),
                   jax.ShapeDtypeStruct((LRU_CONV, D_LRU), F32), jax.ShapeDtypeStruct((1, D_LRU), F32),
                   jax.ShapeDtypeStruct((LRU_BLOCKS, B, B), F32), jax.ShapeDtypeStruct((LRU_BLOCKS, B, B), F32),
                   jax.ShapeDtypeStruct((1, D_LRU), F32), jax.ShapeDtypeStruct((1, D_LRU), F32),
                   jax.ShapeDtypeStruct((1, D_LRU), F32)],
        scratch_shapes=[pltpu.VMEM((HALO, B), F32), pltpu.VMEM((HALO, B), F32), pltpu.VMEM((HALO, B), F32)],
        input_output_aliases={13: 0},
        compiler_params=_params(("parallel", "arbitrary")),
    )(proj, proj, proj, hsave, hsave, dmix, conv_w, conv_b, wa, wx, ba, bx, lam, dproj)


def _ffn_conv(gp, prev, w_ref, b):
    g = b + _shift_down(gp, prev, 2) * w_ref[0:1, :]
    for j in range(1, FFN_CONV):
        g = g + _shift_down(gp, prev, FFN_CONV - 1 - j) * w_ref[j:j + 1, :]
    return g


def _ffn_act_fwd(gu, conv_w, conv_b, tt=256):
    s = gu.shape[0]
    tt = min(tt, s)
    d_ff = conv_w.shape[1]
    tc = d_ff // N_CHIPS
    hpb = tt // HALO

    def body(g_ref, gprev_ref, u_ref, w_ref, b_ref, act_ref):
        prev = jnp.where(pl.program_id(0) == 0, 0.0, gprev_ref[...])
        gate = _ffn_conv(g_ref[...], prev, w_ref, b_ref[...])
        act_ref[...] = (gate * _sigmoid(gate) * u_ref[...]).astype(BF16)

    return pl.pallas_call(
        body, grid=(s // tt, N_CHIPS), name="ffn_act_fwd",
        in_specs=[pl.BlockSpec((tt, tc), lambda i, j: (i, 2 * j)),
                  pl.BlockSpec((HALO, tc), lambda i, j: (jnp.maximum(i * hpb - 1, 0), 2 * j)),
                  pl.BlockSpec((tt, tc), lambda i, j: (i, 2 * j + 1)),
                  pl.BlockSpec((FFN_CONV, tc), lambda i, j: (0, j)),
                  pl.BlockSpec((1, tc), lambda i, j: (0, j))],
        out_specs=pl.BlockSpec((tt, tc), lambda i, j: (i, j)),
        out_shape=jax.ShapeDtypeStruct((s, d_ff), BF16),
        compiler_params=_params(("parallel", "parallel")),
    )(gu, gu, gu, conv_w, conv_b)


def _ffn_act_bwd(gu, dact, conv_w, conv_b, tt=256):
    s = gu.shape[0]
    tt = min(tt, s)
    nt = s // tt
    d_ff = conv_w.shape[1]
    tc = d_ff // N_CHIPS
    hpb = tt // HALO

    def dgate_of(gate, up, da):
        sg = _sigmoid(gate)
        return da * up * (sg * (1.0 + gate * (1.0 - sg))), da * (gate * sg)

    def body(g_ref, gprev_ref, gnext_ref, u_ref, unext_ref, da_ref, danext_ref, w_ref, b_ref,
             dgu_ref, dw_ref, db_ref):
        i = pl.program_id(1)

        @pl.when(i == 0)
        def _():
            dw_ref[...] = jnp.zeros_like(dw_ref)
            db_ref[...] = jnp.zeros_like(db_ref)

        gp = g_ref[...]
        prev = jnp.where(i == 0, 0.0, gprev_ref[...])
        bias = b_ref[...]
        taps = [_shift_down(gp, prev, FFN_CONV - 1 - j) for j in range(FFN_CONV)]
        gate = bias + taps[0] * w_ref[0:1, :]
        for j in range(1, FFN_CONV):
            gate = gate + taps[j] * w_ref[j:j + 1, :]
        dgate, dup = dgate_of(gate, u_ref[...], da_ref[...])
        gate_n = _ffn_conv(gnext_ref[...], gp[tt - HALO:, :], w_ref, bias)
        dgate_n, _ = dgate_of(gate_n, unext_ref[...], danext_ref[...])
        dgate_n = jnp.where(i == nt - 1, 0.0, dgate_n)
        db_ref[...] += jnp.sum(dgate, axis=0, keepdims=True)
        dgp = jnp.zeros((tt, tc), F32)
        for j in range(FFN_CONV):
            dw_ref[j:j + 1, :] += jnp.sum(dgate * taps[j], axis=0, keepdims=True)
            dgp = dgp + _shift_up(dgate, dgate_n, FFN_CONV - 1 - j) * w_ref[j:j + 1, :]
        dgu_ref[:, :tc] = dgp.astype(BF16)
        dgu_ref[:, tc:] = dup.astype(BF16)

    tile = lambda half: pl.BlockSpec((tt, tc), lambda j, i, half=half: (i, 2 * j + half))
    hprev = lambda half: pl.BlockSpec((HALO, tc), lambda j, i, half=half: (jnp.maximum(i * hpb - 1, 0), 2 * j + half))
    hnext = lambda half: pl.BlockSpec(
        (HALO, tc), lambda j, i, half=half: (jnp.minimum((i + 1) * hpb, nt * hpb - 1), 2 * j + half))
    return pl.pallas_call(
        body, grid=(N_CHIPS, nt), name="ffn_act_bwd",
        in_specs=[tile(0), hprev(0), hnext(0), tile(1), hnext(1),
                  pl.BlockSpec((tt, tc), lambda j, i: (i, j)),
                  pl.BlockSpec((HALO, tc), lambda j, i: (jnp.minimum((i + 1) * hpb, nt * hpb - 1), j)),
                  pl.BlockSpec((FFN_CONV, tc), lambda j, i: (0, j)),
                  pl.BlockSpec((1, tc), lambda j, i: (0, j))],
        out_specs=[pl.BlockSpec((tt, 2 * tc), lambda j, i: (i, j)),
                   pl.BlockSpec((FFN_CONV, tc), lambda j, i: (0, j)),
                   pl.BlockSpec((1, tc), lambda j, i: (0, j))],
        out_shape=[jax.ShapeDtypeStruct((s, 2 * d_ff), BF16),
                   jax.ShapeDtypeStruct((FFN_CONV, d_ff), F32), jax.ShapeDtypeStruct((1, d_ff), F32)],
        compiler_params=_params(("parallel", "arbitrary")),
    )(gu, gu, gu, gu, gu, dact, dact, conv_w, conv_b)


def _gate_grads(dgate, dproj, tm=512):
    s, n = dgate.shape
    tm = min(tm, s)

    def body(a_ref, dproj_in_ref, o_ref, dproj_ref):
        @pl.when(pl.program_id(0) == 0)
        def _():
            o_ref[...] = jnp.zeros_like(o_ref)
        a = a_ref[...]
        o_ref[...] += jnp.sum(a, axis=0, keepdims=True)
        dproj_ref[...] = a.astype(BF16)

    return pl.pallas_call(
        body, grid=(s // tm,), name="gate_grads",
        in_specs=[pl.BlockSpec((tm, n), lambda i: (i, 0)), ANY],
        out_specs=[pl.BlockSpec((1, n), lambda i: (0, 0)),
                   pl.BlockSpec((tm, n), lambda i: (i, (_QKVO + 2 * D_LRU) // LANES))],
        out_shape=[jax.ShapeDtypeStruct((1, n), F32), jax.ShapeDtypeStruct(dproj.shape, BF16)],
        input_output_aliases={1: 1},
        compiler_params=_params(("arbitrary",)),
    )(dgate, dproj)


def _pick(n, *cands):
    for c in cands:
        if n % c == 0:
            return c
    raise ValueError(f"no tile for {n}")


def _behind(a, token):
    return a if token is None else a + token[0:1, 0:1].astype(a.dtype).reshape((1,) * a.ndim)


class _Gathered:
    def __init__(self, w):
        self.w = w

    def begin(self):
        return None

    def mid(self, grp, after):
        return None

    def end(self, grp, after):
        return self.w

    def reduce_early(self, grads):
        return None

    def reduce_early_mid(self, after):
        return None

    def reduce_late(self, grads):
        return None

    def reduce_late_mid(self, after):
        return None


def _local_step(x, target, w, comm):
    s, d = x.shape
    nc = s // CHUNK
    tm = _pick(s, 1024, 512, 256)
    tn_proj = _pick(_PROJ_PAD, 896)
    gate_col = 4 * D_MLSTM + 2 * D_LRU
    w = dict(w)

    token = comm.begin()
    n1, rstd1 = _rmsnorm_fwd("norm_mix_fwd", x, _behind(w["norm_mix_g"], token))
    comm.mid(0, n1)
    w.update(comm.end(0, None))
    proj = _mm_nn("proj_fwd", n1, w["w_in"], tm, tn_proj, d)
    token = comm.mid(1, proj)
    gates = proj[:, gate_col:gate_col + 2 * HEADS]
    gates_t = gates.reshape(nc, CHUNK, 2 * HEADS).transpose(0, 2, 1)
    bias_row = _behind(jnp.pad(w["b_gate_m"], ((0, 0), (0, LANES - 2 * HEADS))), token)
    bias_col = w["b_gate_m"].reshape(2 * HEADS, 1)
    mix, cprev, nprev, mprev = _mlstm_fwd(proj, gates_t, bias_row, bias_col, w["mlstm_norm_g"])
    mix, hsave = _lru_fwd(proj, mix, w["lru_conv_w"], w["lru_conv_b"], w["lru_wa"], w["lru_wx"],
                          w["lru_ba"], w["lru_bx"], w["lru_lambda"])
    w.update(comm.end(1, hsave))
    token = comm.mid(2, hsave)
    x1 = _mm_nn("out_fwd", mix, w["w_out"], tm, 1024, d, res=x)
    n2, rstd2 = _rmsnorm_fwd("norm_ffn_fwd", x1, _behind(w["norm_ffn_g"], token))
    w.update(comm.end(2, n2))
    token = comm.mid(3, n2)
    gu = _mm_up_fwd("up_fwd", n2, w["w_up"], tm, d)
    act = _ffn_act_fwd(gu, w["ffn_conv_w"], _behind(w["ffn_conv_b"], token))
    w.update(comm.end(3, act))
    d_ff = w["w_down"].shape[0]
    x2 = _mm_nn("down_fwd", act, w["w_down"], min(tm, 512), 1024, d_ff // 2, res=x1)
    loss, dx2, dx2b, g_norm_final = _loss_head("loss_head", x2, w["norm_final_g"], target)

    grads = {"norm_final_g": g_norm_final}
    dact = _mm_nt("down_bwd_x", dx2b, w["w_down"], tm, d_ff // N_CHIPS, d)
    grads["w_down"] = _mm_tn("down_bwd_w", act, dx2b, d_ff // N_CHIPS, 1024, 2048)
    dgu, grads["ffn_conv_w"], grads["ffn_conv_b"] = _ffn_act_bwd(gu, dact, w["ffn_conv_w"], w["ffn_conv_b"])
    dn2 = _mm_up_bwd_x("up_bwd_x", dgu, w["w_up"], tm, 1024)
    grads["w_up"] = _mm_up_bwd_w("up_bwd_w", n2, dgu, 1024, 2048)
    dx1, dx1b, grads["norm_ffn_g"] = _rmsnorm_bwd("norm_ffn_bwd", x1, rstd2, w["norm_ffn_g"], dn2, dx2)
    dmix = _mm_nt("out_bwd_x", dx1b, w["w_out"], tm, 1024, d)
    grads["w_out"] = _mm_tn("out_bwd_w", mix, dx1b, 1024, 1024, 2048)
    token = comm.reduce_early(grads)
    dproj, dgate, grads["mlstm_norm_g"] = _mlstm_bwd(proj, gates_t, _behind(bias_row, token), bias_col,
                                                     w["mlstm_norm_g"], cprev, nprev, mprev, dmix)
    token = comm.reduce_early_mid(dproj)
    (dproj, grads["lru_conv_w"], grads["lru_conv_b"], grads["lru_wa"], grads["lru_wx"],
     grads["lru_ba"], grads["lru_bx"], grads["lru_lambda"]) = _lru_bwd(
        proj, hsave, dmix, dproj, w["lru_conv_w"], _behind(w["lru_conv_b"], token), w["lru_wa"], w["lru_wx"],
        w["lru_ba"], w["lru_bx"], w["lru_lambda"])
    gate_bias_grad, dproj = _gate_grads(dgate, dproj)
    grads["b_gate_m"] = gate_bias_grad[:, :2 * HEADS]
    grads["w_in"] = _mm_tn("proj_bwd_w", n1, dproj, 1024, tn_proj, 2048)
    token = comm.reduce_late(grads)
    dn1 = _mm_nt("proj_bwd_x", dproj, w["w_in"], tm, 512, _PROJ_PAD, after=token)
    token = comm.reduce_late_mid(dn1)
    grad_x, _, grads["norm_mix_g"] = _rmsnorm_bwd("norm_mix_bwd", x, rstd1, _behind(w["norm_mix_g"], token),
                                                  dn1, dx1)
    return loss, grad_x, grads


WEIGHT_NAMES = ("norm_mix_g", "w_in", "b_gate_m", "mlstm_norm_g", "lru_conv_w", "lru_conv_b", "lru_wa", "lru_ba",
                "lru_wx", "lru_bx", "lru_lambda", "w_out", "norm_ffn_g", "w_up", "ffn_conv_w", "ffn_conv_b",
                "w_down", "norm_final_g")
BIG = ("w_in", "w_out", "w_up", "w_down")
SMALL_SHARDED = ("mlstm_norm_g", "lru_conv_w", "ffn_conv_w")
SMALL = tuple(n for n in WEIGHT_NAMES if n not in BIG)
SMALL_REPLICATED = tuple(n for n in SMALL if n not in SMALL_SHARDED)


def _proj_segments():
    segs = [(0, 0, _QKVO), (_QKVO, _QKVO + 2 * D_LRU, _N_GATES)]
    for n in range(LRU_BLOCKS):
        segs.append((_QKVO + _N_GATES + n * LRU_BLOCK_DIM, _QKVO + 2 * n * LRU_BLOCK_DIM, LRU_BLOCK_DIM))
        segs.append((_QKVO + _N_GATES + D_LRU + n * LRU_BLOCK_DIM, _QKVO + (2 * n + 1) * LRU_BLOCK_DIM,
                     LRU_BLOCK_DIM))
    return segs


def _w_in_shards_to_local(shards):
    width = shards.shape[2]
    pieces = []
    for g0, _, n in sorted(_proj_segments(), key=lambda s: s[1]):
        at = g0
        while at < g0 + n:
            j = at // width
            stop = min(g0 + n, (j + 1) * width)
            pieces.append(shards[j][:, at - j * width:stop - j * width])
            at = stop
    pieces.append(jnp.zeros((shards.shape[1], PROJ_GATE_PAD - _N_GATES), shards.dtype))
    return jnp.concatenate(pieces, axis=1)


def _w_in_local_to_shards(w):
    width = _PROJ_COLS // N_CHIPS
    shards = []
    for j in range(N_CHIPS):
        pieces = []
        for g0, l0, n in sorted(_proj_segments()):
            lo, hi = max(g0, j * width), min(g0 + n, (j + 1) * width)
            if lo < hi:
                pieces.append(w[:, l0 + lo - g0:l0 + hi - g0])
        shards.append(jnp.concatenate(pieces, axis=1))
    return jnp.stack(shards)


def _w_in_to_global(w):
    sh = _w_in_local_to_shards(w)
    return jnp.concatenate([sh[j] for j in range(N_CHIPS)], axis=1)


def _size(shp):
    return functools.reduce(lambda a, b: a * b, shp, 1)


def _lane_dense(shp):
    return len(shp) >= 2 and shp[-1] == LANES and _size(shp) % (HALO * LANES) == 0


def _pack_rows(shapes):
    loose = sum(_size(shp) for shp in shapes if not _lane_dense(shp))
    return sum(_size(shp) // LANES for shp in shapes if _lane_dense(shp)) + -(-loose // (HALO * LANES)) * HALO


def _pack(arrs, rows):
    del rows
    parts = [a.reshape(-1, LANES).astype(F32) for a in arrs if _lane_dense(a.shape)]
    loose = [a.reshape(-1).astype(F32) for a in arrs if not _lane_dense(a.shape)]
    if loose:
        flat = jnp.concatenate(loose)
        n = -(-flat.shape[0] // (HALO * LANES)) * HALO * LANES
        parts.append(jnp.pad(flat, (0, n - flat.shape[0])).reshape(-1, LANES))
    return parts[0] if len(parts) == 1 else jnp.concatenate(parts, axis=0)


def _unpack(buf, shapes):
    out, row = {}, 0
    for i, shp in enumerate(shapes):
        if _lane_dense(shp):
            n = _size(shp) // LANES
            out[i] = buf[row:row + n].reshape(shp)
            row += n
    flat, at = buf[row:].reshape(-1), 0
    for i, shp in enumerate(shapes):
        if not _lane_dense(shp):
            out[i] = flat[at:at + _size(shp)].reshape(shp)
            at += _size(shp)
    return [out[i] for i in range(len(shapes))]


def _assemble_weights(g_in, g_out, g_up, g_down, small_sharded, replicated):
    w = dict(replicated)
    w["w_in"] = _w_in_shards_to_local(g_in)
    w["w_out"] = g_out.reshape(-1, g_out.shape[-1])
    w["w_up"] = g_up
    w["w_down"] = g_down.reshape(-1, g_down.shape[-1])
    for name, v in small_sharded.items():
        w[name] = jnp.concatenate([v[j] for j in range(N_CHIPS)], axis=1)
    return w


def _full_weights_from_global(weights):
    shard = lambda a, axis: jnp.stack(jnp.split(a, N_CHIPS, axis=axis))
    rep = {n: weights[n].reshape(1, -1) if weights[n].ndim <= 2 and n != "b_gate_m" else weights[n]
           for n in SMALL_REPLICATED}
    rep["b_gate_m"] = weights["b_gate_m"].reshape(1, -1)
    return _assemble_weights(shard(weights["w_in"], 1).astype(BF16), shard(weights["w_out"], 0).astype(BF16),
                             shard(weights["w_up"], 1).astype(BF16), shard(weights["w_down"], 0).astype(BF16),
                             {n: shard(weights[n], 1) for n in SMALL_SHARDED}, rep)


def _grads_to_global(grads):
    g = dict(grads)
    g["w_in"] = _w_in_to_global(grads["w_in"])
    g["w_up"] = jnp.concatenate([grads["w_up"][j] for j in range(N_CHIPS)], axis=1)
    return g


def _place():
    x, y, c = lax.axis_index("x"), lax.axis_index("y"), lax.axis_index("c")
    chips = [(1 - x, y), (x, 1 - y), (1 - x, 1 - y)]
    return x, y, c, 2 * x + y, chips


def _half_rows(n_rows, which):
    half = n_rows // 2
    return pl.ds(pl.multiple_of(which * half, 16), half)


def _rcopy(src, dst, send_sem, recv_sem, to):
    return pltpu.make_async_remote_copy(src_ref=src, dst_ref=dst, send_sem=send_sem, recv_sem=recv_sem,
                                        device_id=to, device_id_type=MESH)


HBM_SPEC = pl.BlockSpec(memory_space=pltpu.HBM)
SEM_SPEC = pl.BlockSpec(memory_space=pltpu.SEMAPHORE)
TOKEN_SHAPE = (8, LANES)


def _split_call(name, bufs, sems_in, sems_out_shapes, body_fn, after=None):
    nb, ni, no = len(bufs), len(sems_in), len(sems_out_shapes)
    after = [] if after is None else list(after) if isinstance(after, (list, tuple)) else [after]

    def body(*refs):
        buf_refs = refs[:nb]
        sem_in_refs = refs[nb:nb + ni]
        outs = refs[nb + ni + len(after):]
        sem_out_refs = outs[:no]
        token_ref = outs[no + nb]
        body_fn(buf_refs, sem_in_refs, sem_out_refs)
        token_ref[...] = jnp.zeros_like(token_ref)

    out_shape = ([pltpu.SemaphoreType.DMA(shp) for shp in sems_out_shapes]
                 + [pltpu.HBM(b.shape, b.dtype) for b in bufs] + [jax.ShapeDtypeStruct(TOKEN_SHAPE, F32)])
    res = pl.pallas_call(
        body, name=name, out_shape=out_shape,
        in_specs=[HBM_SPEC] * nb + [SEM_SPEC] * ni + [ANY] * len(after),
        out_specs=[SEM_SPEC] * no + [HBM_SPEC] * nb + [pl.BlockSpec(memory_space=pltpu.VMEM)],
        input_output_aliases={i: no + i for i in range(nb)},
        compiler_params=pltpu.CompilerParams(has_side_effects=pltpu.SideEffectType.DATAFLOW_SIDE_EFFECTING),
    )(*[pltpu.with_memory_space_constraint(b, pltpu.HBM) for b in bufs], *sems_in, *after)
    return list(res[:no]), list(res[no:no + nb]), res[no + nb]


def _place_own_shard(name, idx, shard, after=None):
    rows, cols = shard.shape
    tr = _row_tile(rows)

    def body(idx_ref, s_ref, *rest):
        rest[-1][...] = s_ref[...].astype(BF16)

    return pl.pallas_call(
        body, name=name, out_shape=jax.ShapeDtypeStruct((N_CHIPS, rows, cols), BF16),
        grid_spec=pltpu.PrefetchScalarGridSpec(
            num_scalar_prefetch=1, grid=(rows // tr,),
            in_specs=[pl.BlockSpec((tr, cols), lambda i, s: (i, 0))] + ([] if after is None else [ANY]),
            out_specs=pl.BlockSpec((None, tr, cols), lambda i, s: (s[1], i, 0))),
        compiler_params=_params(("parallel",)),
    )(idx, shard, *(() if after is None else (after,)))


GATHER_GROUPS = ((0, 4), (1,), (2,), (3,))


def _gather_start(name, lands, groups, after=None):
    members = [w for g in groups for w in GATHER_GROUPS[g]]

    def starts(bufs, _, sems):
        x, y, c, me, chips = _place()
        for gi, g in enumerate(groups):
            for pos, w in enumerate(GATHER_GROUPS[g]):
                buf = bufs[members.index(w)]
                part = buf.at[me] if w == 4 else buf.at[me, _half_rows(buf.shape[1], c)]
                for k, chip in enumerate(chips):
                    _rcopy(part, part, sems[2 * gi].at[3 * pos + k], sems[2 * gi + 1].at[3 * pos + k],
                           (*chip, c)).start()

    shapes = []
    for g in groups:
        shapes += [(3 * len(GATHER_GROUPS[g]),)] * 2
    sems, bufs, token = _split_call(name, [lands[w] for w in members], [], shapes, starts, after=after)
    return ({g: (sems[2 * gi], sems[2 * gi + 1]) for gi, g in enumerate(groups)},
            dict(zip(members, bufs)), token)


def _gather_mid(grp, lands, sems, after):
    members = GATHER_GROUPS[grp]
    big = [w for w in members if w != 4]

    def mid(bufs, sems_in, sems_out):
        x, y, c, me, chips = _place()
        send_sems, recv_sems = sems_in
        for pos, w in enumerate(members):
            for k, chip in enumerate(chips):
                cid = 2 * chip[0] + chip[1]
                buf = bufs[pos]
                mine = buf.at[me] if w == 4 else buf.at[me, _half_rows(buf.shape[1], c)]
                theirs = buf.at[cid] if w == 4 else buf.at[cid, _half_rows(buf.shape[1], c)]
                arrival = _rcopy(mine, theirs, send_sems.at[3 * pos + k], recv_sems.at[3 * pos + k], (*chip, c))
                arrival.wait_recv()
                arrival.wait_send()
                if w != 4:
                    _rcopy(theirs, theirs, sems_out[0].at[3 * big.index(w) + k],
                           sems_out[1].at[3 * big.index(w) + k], (x, y, 1 - c)).start()

    new_sems, bufs, token = _split_call(f"gather_mid_{grp}", [lands[w] for w in members], list(sems),
                                        [(3 * len(big),), (3 * len(big),)], mid, after=after)
    return new_sems, bufs, token


def _gather_end(grp, bufs, sems, after):
    members = GATHER_GROUPS[grp]
    big = [w for w in members if w != 4]

    def end(refs, sems_in, _):
        x, y, c, me, chips = _place()
        send_sems, recv_sems = sems_in
        for pos, w in enumerate(members):
            if w == 4:
                continue
            for k, chip in enumerate(chips):
                cid = 2 * chip[0] + chip[1]
                buf = refs[pos]
                sent = buf.at[cid, _half_rows(buf.shape[1], c)]
                landed = buf.at[cid, _half_rows(buf.shape[1], 1 - c)]
                fwd = _rcopy(sent, landed, send_sems.at[3 * big.index(w) + k], recv_sems.at[3 * big.index(w) + k],
                             (x, y, 1 - c))
                fwd.wait_recv()
                fwd.wait_send()

    _, bufs, token = _split_call(f"gather_end_{grp}", bufs, list(sems), [], end, after=after)
    return bufs, token


def _pair_start(name, grads, extra=None):
    n = len(grads)
    bufs = list(grads) + [lax.empty((g.shape[0], g.shape[1] // 2, g.shape[2]), g.dtype) for g in grads]
    if extra is not None:
        bufs += [extra, lax.empty(extra.shape, extra.dtype)]

    def starts(refs, _, sems):
        x, y, c, _, _ = _place()
        for w in range(n):
            other = _half_rows(refs[w].shape[1], 1 - c)
            _rcopy(refs[w].at[:, other], refs[n + w], sems[0].at[w], sems[1].at[w], (x, y, 1 - c)).start()
        if extra is not None:
            _rcopy(refs[2 * n], refs[2 * n + 1], sems[0].at[n], sems[1].at[n], (x, y, 1 - c)).start()

    count = n + (extra is not None)
    return _split_call(name, bufs, [], [(count,), (count,)], starts)


def _pair_wait(name, n, bufs, sems, after):
    has_extra = len(bufs) > 2 * n

    def waits(refs, sems_in, _):
        x, y, c, _, _ = _place()
        for w in range(n):
            other = _half_rows(refs[w].shape[1], 1 - c)
            cp = _rcopy(refs[w].at[:, other], refs[n + w], sems_in[0].at[w], sems_in[1].at[w], (x, y, 1 - c))
            cp.wait_recv()
            cp.wait_send()
        if has_extra:
            cp = _rcopy(refs[2 * n], refs[2 * n + 1], sems_in[0].at[n], sems_in[1].at[n], (x, y, 1 - c))
            cp.wait_recv()
            cp.wait_send()

    _, bufs, token = _split_call(name, bufs, list(sems), [], waits, after=after)
    return bufs, token


def _chip_start(name, partials, small=None):
    n = len(partials)
    bufs = list(partials) + [lax.empty(p.shape, p.dtype) for p in partials] + ([] if small is None else [small])

    def starts(refs, _, sems):
        _, _, c, me, chips = _place()
        for w in range(n):
            for k, chip in enumerate(chips):
                cid = 2 * chip[0] + chip[1]
                _rcopy(refs[w].at[cid], refs[n + w].at[me], sems[0].at[3 * w + k], sems[1].at[3 * w + k],
                       (*chip, c)).start()
        if small is not None:
            for k, chip in enumerate(chips):
                _rcopy(refs[2 * n].at[me], refs[2 * n].at[me], sems[0].at[3 * n + k], sems[1].at[3 * n + k],
                       (*chip, c)).start()

    count = 3 * (n + (small is not None))
    return _split_call(name, bufs, [], [(count,), (count,)], starts)


def _chip_wait(name, n, bufs, sems, after):
    has_small = len(bufs) > 2 * n

    def waits(refs, sems_in, _):
        _, _, c, me, chips = _place()
        for w in range(n):
            for k, chip in enumerate(chips):
                cid = 2 * chip[0] + chip[1]
                cp = _rcopy(refs[w].at[cid], refs[n + w].at[cid], sems_in[0].at[3 * w + k],
                            sems_in[1].at[3 * w + k], (*chip, c))
                cp.wait_recv()
                cp.wait_send()
        if has_small:
            for k, chip in enumerate(chips):
                cid = 2 * chip[0] + chip[1]
                cp = _rcopy(refs[2 * n].at[me], refs[2 * n].at[cid], sems_in[0].at[3 * n + k],
                            sems_in[1].at[3 * n + k], (*chip, c))
                cp.wait_recv()
                cp.wait_send()

    _, bufs, token = _split_call(name, bufs, list(sems), [], waits, after=after)
    return bufs, token


def _small_pair_sum(idx, own, recv):
    rows = own.shape[0]

    def body(idx_ref, a_ref, b_ref, o_ref):
        o_ref[...] = a_ref[...] + b_ref[...]

    blk = pl.BlockSpec((rows, LANES), lambda i, s: (0, 0))
    return pl.pallas_call(
        body, name="small_pair_sum", out_shape=jax.ShapeDtypeStruct((N_CHIPS, rows, LANES), F32),
        grid_spec=pltpu.PrefetchScalarGridSpec(
            num_scalar_prefetch=1, grid=(1,), in_specs=[blk, blk],
            out_specs=pl.BlockSpec((None, rows, LANES), lambda i, s: (s[1], 0, 0))),
        compiler_params=_params(("arbitrary",)),
    )(idx, own, recv)


def _gather_weights(shards, small):
    nb = len(shards)

    def body(*refs):
        srcs, small_ref = refs[:nb], refs[nb]
        dsts, small_out = refs[nb + 1:2 * nb + 1], refs[2 * nb + 1]
        send_sems, recv_sems, local_sems = refs[2 * nb + 2:]
        x, y, c, me, chips = _place()
        sibling = (x, y, 1 - c)
        mine = [_half_rows(s.shape[0], c) for s in srcs]
        other = [_half_rows(s.shape[0], 1 - c) for s in srcs]

        local = [pltpu.make_async_copy(srcs[w], dsts[w].at[me], local_sems.at[w]) for w in range(nb)]
        local.append(pltpu.make_async_copy(small_ref, small_out.at[me], local_sems.at[nb]))
        for cp in local:
            cp.start()
        sends = []
        for w in range(nb):
            for k, chip in enumerate(chips):
                sends.append(_rcopy(srcs[w].at[mine[w]], dsts[w].at[me, mine[w]],
                                    send_sems.at[w, k], recv_sems.at[w, k], (*chip, c)))
        for k, chip in enumerate(chips):
            sends.append(_rcopy(small_ref, small_out.at[me], send_sems.at[nb, k], recv_sems.at[nb, k], (*chip, c)))
        for cp in sends:
            cp.start()
        passed = []
        for w in range(nb):
            for k, chip in enumerate(chips):
                cid = 2 * chip[0] + chip[1]
                landed = dsts[w].at[cid, mine[w]]
                _rcopy(landed, landed, send_sems.at[w, k], recv_sems.at[w, k], (*chip, c)).wait_recv()
                fwd = _rcopy(landed, landed, send_sems.at[w, 3 + k], recv_sems.at[w, 3 + k], sibling)
                fwd.start()
                passed.append(fwd)
        for k, chip in enumerate(chips):
            cid = 2 * chip[0] + chip[1]
            _rcopy(small_ref, small_out.at[cid], send_sems.at[nb, k], recv_sems.at[nb, k], (*chip, c)).wait_recv()
        for w in range(nb):
            for k, chip in enumerate(chips):
                cid = 2 * chip[0] + chip[1]
                landed = dsts[w].at[cid, other[w]]
                _rcopy(landed, landed, send_sems.at[w, 3 + k], recv_sems.at[w, 3 + k], sibling).wait_recv()
        for cp in sends + passed:
            cp.wait_send()
        for cp in local:
            cp.wait()

    out_shape = [jax.ShapeDtypeStruct((N_CHIPS,) + s.shape, s.dtype) for s in shards]
    out_shape.append(jax.ShapeDtypeStruct((N_CHIPS,) + small.shape, small.dtype))
    return pl.pallas_call(
        body, name="gather_weights", out_shape=out_shape,
        in_specs=[ANY] * (nb + 1), out_specs=[ANY] * (nb + 1),
        scratch_shapes=[pltpu.SemaphoreType.DMA((nb + 1, 6)), pltpu.SemaphoreType.DMA((nb + 1, 6)),
                        pltpu.SemaphoreType.DMA((nb + 1,))],
    )(*shards, small)


def _pair_exchange(grads, small):
    nb = len(grads)

    def body(*refs):
        srcs, small_ref = refs[:nb], refs[nb]
        dsts, small_out = refs[nb + 1:2 * nb + 1], refs[2 * nb + 1]
        send_sems, recv_sems, small_send, small_recv, local_sem = refs[2 * nb + 2:]
        x, y, c, _, _ = _place()
        sibling = (x, y, 1 - c)
        my_id = 4 * x + 2 * y + c
        local = pltpu.make_async_copy(small_ref, small_out.at[my_id], local_sem)
        local.start()
        sends = []
        for w in range(nb):
            other = _half_rows(srcs[w].shape[1], 1 - c)
            sends.append(_rcopy(srcs[w].at[:, other], dsts[w], send_sems.at[w], recv_sems.at[w], sibling))
        for r in range(1, N_DEV):
            to = (1 - x if r & 4 else x, 1 - y if r & 2 else y, 1 - c if r & 1 else c)
            sends.append(_rcopy(small_ref, small_out.at[my_id], small_send.at[r - 1], small_recv.at[r - 1], to))
        for cp in sends:
            cp.start()
        for w in range(nb):
            _rcopy(dsts[w], dsts[w], send_sems.at[w], recv_sems.at[w], sibling).wait_recv()
        for r in range(1, N_DEV):
            frm = (1 - x if r & 4 else x, 1 - y if r & 2 else y, 1 - c if r & 1 else c)
            frm_id = 4 * frm[0] + 2 * frm[1] + frm[2]
            _rcopy(small_ref, small_out.at[frm_id], small_send.at[r - 1], small_recv.at[r - 1], frm).wait_recv()
        for cp in sends:
            cp.wait_send()
        local.wait()

    out_shape = [jax.ShapeDtypeStruct((g.shape[0], g.shape[1] // 2, g.shape[2]), g.dtype) for g in grads]
    out_shape.append(jax.ShapeDtypeStruct((N_DEV,) + small.shape, small.dtype))
    return pl.pallas_call(
        body, name="pair_exchange", out_shape=out_shape,
        in_specs=[ANY] * (nb + 1), out_specs=[ANY] * (nb + 1),
        scratch_shapes=[pltpu.SemaphoreType.DMA((nb,)), pltpu.SemaphoreType.DMA((nb,)),
                        pltpu.SemaphoreType.DMA((N_DEV - 1,)), pltpu.SemaphoreType.DMA((N_DEV - 1,)),
                        pltpu.SemaphoreType.DMA(())],
    )(*grads, small)


def _chip_exchange(partials):
    nb = len(partials)

    def body(*refs):
        srcs, dsts = refs[:nb], refs[nb:2 * nb]
        send_sems, recv_sems = refs[2 * nb:]
        _, _, c, me, chips = _place()
        sends = []
        for w in range(nb):
            for k, chip in enumerate(chips):
                cid = 2 * chip[0] + chip[1]
                sends.append(_rcopy(srcs[w].at[cid], dsts[w].at[me], send_sems.at[w, k], recv_sems.at[w, k],
                                    (*chip, c)))
        for cp in sends:
            cp.start()
        for w in range(nb):
            for k, chip in enumerate(chips):
                cid = 2 * chip[0] + chip[1]
                _rcopy(srcs[w].at[cid], dsts[w].at[cid], send_sems.at[w, k], recv_sems.at[w, k],
                       (*chip, c)).wait_recv()
        for cp in sends:
            cp.wait_send()

    return pl.pallas_call(
        body, name="chip_exchange", out_shape=[jax.ShapeDtypeStruct(p.shape, p.dtype) for p in partials],
        in_specs=[ANY] * nb, out_specs=[ANY] * nb,
        scratch_shapes=[pltpu.SemaphoreType.DMA((nb, 3)), pltpu.SemaphoreType.DMA((nb, 3))],
    )(*partials)


def _pair_share(name, shards, late=None):
    nb = len(shards)
    nl = 0 if late is None else 1

    def body(*refs):
        srcs = refs[:nb]
        dsts = refs[nb + nl:2 * nb + nl]
        send_sems, recv_sems = refs[2 * nb + 2 * nl:2 * nb + 2 * nl + 2]
        x, y, c, _, _ = _place()
        sibling = (x, y, 1 - c)
        sends = []
        for w in range(nb):
            mine = _half_rows(dsts[w].shape[0], c)
            sends.append(_rcopy(srcs[w].at[mine], dsts[w].at[mine], send_sems.at[w], recv_sems.at[w], sibling))
        if nl:
            late_ref, late_out = refs[nb], refs[2 * nb + 1]
            late_send, late_recv, local_sem = refs[2 * nb + 4:]
            my_id = 4 * x + 2 * y + c
            peer = lambda r: (1 - x if r & 4 else x, 1 - y if r & 2 else y, 1 - c if r & 1 else c)
            local = pltpu.make_async_copy(late_ref, late_out.at[my_id], local_sem)
            local.start()
            for r in range(1, N_DEV):
                sends.append(_rcopy(late_ref, late_out.at[my_id], late_send.at[r - 1], late_recv.at[r - 1],
                                    peer(r)))
        for cp in sends:
            cp.start()
        for w in range(nb):
            other = _half_rows(dsts[w].shape[0], 1 - c)
            _rcopy(srcs[w].at[other], dsts[w].at[other], send_sems.at[w], recv_sems.at[w], sibling).wait_recv()
        if nl:
            for r in range(1, N_DEV):
                frm = peer(r)
                _rcopy(late_ref, late_out.at[4 * frm[0] + 2 * frm[1] + frm[2]], late_send.at[r - 1],
                       late_recv.at[r - 1], frm).wait_recv()
        for cp in sends:
            cp.wait_send()
        if nl:
            local.wait()

    out_shape = [jax.ShapeDtypeStruct(h.shape, h.dtype) for h in shards]
    scratch = [pltpu.SemaphoreType.DMA((nb,)), pltpu.SemaphoreType.DMA((nb,))]
    if nl:
        out_shape.append(jax.ShapeDtypeStruct((N_DEV,) + late.shape, late.dtype))
        scratch += [pltpu.SemaphoreType.DMA((N_DEV - 1,)), pltpu.SemaphoreType.DMA((N_DEV - 1,)),
                    pltpu.SemaphoreType.DMA(())]
    return pl.pallas_call(
        body, name=name, out_shape=out_shape,
        in_specs=[ANY] * (nb + nl), out_specs=[ANY] * (nb + nl), scratch_shapes=scratch,
        input_output_aliases={w: w for w in range(nb)},
    )(*shards, *(() if late is None else (late,)))


def _row_tile(rows):
    return _pick(rows, 128, 64, 16, 8)


def _pair_sum(name, idx, grad, recv):
    n, half, cols = recv.shape
    tr = _row_tile(half)
    nrb = half // tr

    def body(idx_ref, g_ref, r_ref, o_ref):
        o_ref[...] = (g_ref[...] + r_ref[...]).astype(BF16)

    return pl.pallas_call(
        body, name=name, out_shape=jax.ShapeDtypeStruct(recv.shape, BF16),
        grid_spec=pltpu.PrefetchScalarGridSpec(
            num_scalar_prefetch=1, grid=(n - 1, nrb),
            in_specs=[pl.BlockSpec((None, tr, cols), lambda j, i, s: (s[2 + j], s[0] * nrb + i, 0)),
                      pl.BlockSpec((None, tr, cols), lambda j, i, s: (s[2 + j], i, 0))],
            out_specs=pl.BlockSpec((None, tr, cols), lambda j, i, s: (s[2 + j], i, 0))),
        compiler_params=_params(("parallel", "parallel")),
    )(idx, grad, recv)


def _final_sum(name, idx, grad, recv, chip_sums):
    _, half, cols = recv.shape
    tr = _row_tile(half)
    nrb = half // tr

    def body(idx_ref, g_ref, r_ref, p1_ref, p2_ref, p3_ref, o_ref):
        acc = g_ref[...] + r_ref[...]
        for p_ref in (p1_ref, p2_ref, p3_ref):
            acc = acc + p_ref[...].astype(F32)
        o_ref[...] = acc

    slot = lambda which: pl.BlockSpec((None, tr, cols), lambda i, s, which=which: (s[which], i, 0))
    return pl.pallas_call(
        body, name=name, out_shape=jax.ShapeDtypeStruct((2 * half, cols), F32),
        grid_spec=pltpu.PrefetchScalarGridSpec(
            num_scalar_prefetch=1, grid=(nrb,),
            in_specs=[pl.BlockSpec((None, tr, cols), lambda i, s: (s[1], s[0] * nrb + i, 0)),
                      slot(1), slot(2), slot(3), slot(4)],
            out_specs=pl.BlockSpec((tr, cols), lambda i, s: (s[0] * nrb + i, 0))),
        compiler_params=_params(("parallel",)),
    )(idx, grad, recv, chip_sums, chip_sums, chip_sums)


def _pair_sum_all(name, idx, grad, recv):
    _, half, cols = recv.shape
    tr = _row_tile(half)
    nrb = half // tr

    def body(idx_ref, g_ref, r_ref, o_ref):
        o_ref[...] = (g_ref[...] + r_ref[...]).astype(BF16)

    return pl.pallas_call(
        body, name=name, out_shape=jax.ShapeDtypeStruct((half, cols), BF16),
        grid_spec=pltpu.PrefetchScalarGridSpec(
            num_scalar_prefetch=1, grid=(nrb,),
            in_specs=[pl.BlockSpec((None, tr, cols), lambda i, s: (0, s[0] * nrb + i, 0)),
                      pl.BlockSpec((None, tr, cols), lambda i, s: (0, i, 0))],
            out_specs=pl.BlockSpec((tr, cols), lambda i, s: (i, 0))),
        compiler_params=_params(("parallel",)),
    )(idx, grad, recv)


def _final_sum_bf16(name, idx, partial, chip_sums):
    _, half, cols = partial.shape
    tr = _row_tile(half)
    nrb = half // tr

    def body(idx_ref, p0_ref, p1_ref, p2_ref, p3_ref, o_ref):
        acc = p0_ref[...].astype(F32)
        for p_ref in (p1_ref, p2_ref, p3_ref):
            acc = acc + p_ref[...].astype(F32)
        o_ref[...] = acc

    slot = lambda which: pl.BlockSpec((None, tr, cols), lambda i, s, which=which: (s[which], i, 0))
    return pl.pallas_call(
        body, name=name, out_shape=jax.ShapeDtypeStruct((2 * half, cols), F32),
        grid_spec=pltpu.PrefetchScalarGridSpec(
            num_scalar_prefetch=1, grid=(nrb,),
            in_specs=[slot(1), slot(2), slot(3), slot(4)],
            out_specs=pl.BlockSpec((tr, cols), lambda i, s: (s[0] * nrb + i, 0))),
        compiler_params=_params(("parallel",)),
    )(idx, partial, chip_sums, chip_sums, chip_sums)


def _small_sum(name, packs):
    n, rows, _ = packs.shape

    def body(p_ref, o_ref):
        acc = p_ref[0]
        for k in range(1, n):
            acc = acc + p_ref[k]
        o_ref[...] = acc

    return pl.pallas_call(
        body, name=name, out_shape=jax.ShapeDtypeStruct((rows, LANES), F32),
        in_specs=[pl.BlockSpec(memory_space=pltpu.VMEM)], out_specs=pl.BlockSpec(memory_space=pltpu.VMEM),
        compiler_params=pltpu.CompilerParams(vmem_limit_bytes=VMEM_LIMIT),
    )(packs)


def _adamw_math(w, g, m, v):
    m_new = ADAM_B1 * m + (1.0 - ADAM_B1) * g
    v_new = ADAM_B2 * v + (1.0 - ADAM_B2) * (g * g)
    m_hat = m_new / (1.0 - ADAM_B1 ** ADAM_STEP)
    v_hat = v_new / (1.0 - ADAM_B2 ** ADAM_STEP)
    return -ADAM_LR * (m_hat / (jnp.sqrt(v_hat) + ADAM_EPS) + ADAM_WD * w), m_new, v_new


def _adamw_many(name, ws, gs, ms, vs):
    n = len(ws)

    def body(*refs):
        for i in range(n):
            d, m_new, v_new = _adamw_math(refs[i][...], refs[n + i][...], refs[2 * n + i][...],
                                          refs[3 * n + i][...])
            refs[4 * n + i][...] = d
            refs[5 * n + i][...] = m_new
            refs[6 * n + i][...] = v_new

    vmem = pl.BlockSpec(memory_space=pltpu.VMEM)
    res = pl.pallas_call(
        body, name=name, in_specs=[vmem] * (4 * n), out_specs=[vmem] * (3 * n),
        out_shape=[jax.ShapeDtypeStruct(w.shape, F32) for w in ws] * 3,
        compiler_params=pltpu.CompilerParams(vmem_limit_bytes=VMEM_LIMIT),
    )(*ws, *gs, *ms, *vs)
    return res[:n], res[n:2 * n], res[2 * n:]


def _adamw(name, w, g, m, v):
    rows, cols = w.shape
    tr = rows if rows * cols * 4 <= (2 << 20) else _row_tile(rows)

    def body(w_ref, g_ref, m_ref, v_ref, g_out_ref, d_ref, nm_ref, nv_ref):
        gv = g_ref[...]
        g_out_ref[...] = gv
        d_ref[...], nm_ref[...], nv_ref[...] = _adamw_math(w_ref[...], gv, m_ref[...], v_ref[...])

    blk = pl.BlockSpec((tr, cols), lambda i: (i, 0))
    sds = jax.ShapeDtypeStruct((rows, cols), F32)
    return pl.pallas_call(
        body, name=name, grid=(rows // tr,), in_specs=[blk] * 4, out_specs=[blk] * 4, out_shape=[sds] * 4,
        compiler_params=_params(("parallel",)),
    )(w, g, m, v)


def _train_step(x, target, W, M, V):
    xi, yi, ci = lax.axis_index("x"), lax.axis_index("y"), lax.axis_index("c")
    me = 2 * xi + yi
    big = {n: W[n][0] for n in BIG}
    big_m = {n: M[n][0] for n in BIG}
    big_v = {n: V[n][0] for n in BIG}

    others = [jnp.where(jnp.int32(i) >= me, i + 1, i) for i in range(N_CHIPS - 1)]
    idx = jnp.stack([ci, me] + others).astype(jnp.int32)

    sharded_shapes = [W[n].shape[1:] for n in SMALL_SHARDED]
    small_pack = _pack([W[n][0] for n in SMALL_SHARDED], _pack_rows(sharded_shapes))
    small_land = lax.dynamic_update_slice(jnp.zeros((N_CHIPS,) + small_pack.shape, F32), small_pack[None],
                                          (me, 0, 0))
    replicated = {n: (W[n].reshape(1, -1) if W[n].ndim <= 2 else W[n][0]) for n in SMALL_REPLICATED}

    early = ("w_out", "w_up", "w_down")
    small_late = "norm_mix_g"
    small_early = tuple(n for n in SMALL if n != small_late)
    global_shape = lambda n: ((W[n].shape[1], W[n].shape[2] * N_CHIPS) if n in SMALL_SHARDED else
                              tuple(W[n].shape) if W[n].ndim == 1 else tuple(W[n].shape[1:]))
    small_shapes = [global_shape(n) for n in small_early]

    def shard_major(n, g):
        if n == "w_in":
            return _w_in_local_to_shards(g)
        return g if g.ndim == 3 else g.reshape((N_CHIPS, -1) + g.shape[1:])

    class _SplitComm:
        def reduce_early(self, grads):
            self.e_sems, self.e_bufs, token = _pair_start("pair_start_early",
                                                          [shard_major(n, grads[n]) for n in early])
            return token

        def reduce_early_mid(self, after):
            n = len(early)
            bufs, _ = _pair_wait("pair_wait_early", n, self.e_bufs, self.e_sems, after)
            self.e_grads, self.e_recv = bufs[:n], bufs[n:2 * n]
            partial = [_pair_sum(f"pair_sum_{nm}", idx, g, r) for nm, g, r in zip(early, self.e_grads, self.e_recv)]
            self.e_sems, self.e_bufs, token = _chip_start("chip_start_early", partial)
            return token

        def reduce_late(self, grads):
            pack = _pack([grads[n] for n in small_early], _pack_rows(small_shapes))
            self.l_sems, self.l_bufs, token = _pair_start("pair_start_late", [grads["w_in"][None]], extra=pack)
            return token

        def reduce_late_mid(self, after):
            bufs, _ = _pair_wait("pair_wait_late", 1, self.l_bufs, self.l_sems, after)
            partial = _w_in_local_to_shards(_pair_sum_all("pair_sum_w_in", idx, bufs[0], bufs[1]))
            self.l_sems, self.l_bufs, token = _chip_start("chip_start_late", [partial],
                                                          small=_small_pair_sum(idx, bufs[2], bufs[3]))
            return token

        def finish_early(self, after):
            n = len(early)
            bufs, _ = _chip_wait("chip_wait_early", n, self.e_bufs, self.e_sems, after)
            halves = [_final_sum(f"final_sum_{nm}", idx, g, r, p)
                      for nm, g, r, p in zip(early, self.e_grads, self.e_recv, bufs[n:2 * n])]
            return dict(zip(early, _pair_share("pair_share_early", halves)))

        def finish_late(self, after, late):
            bufs, _ = _chip_wait("chip_wait_late", 1, self.l_bufs, self.l_sems, after)
            half = _final_sum_bf16("final_sum_w_in", idx, bufs[0], bufs[1])
            small = dict(zip(small_early, _unpack(_small_sum("small_sum", bufs[2]), small_shapes)))
            whole, late_all = _pair_share("pair_share_late", [half], late)
            return whole, small, _small_sum("late_sum", late_all)

        def begin(self):
            first = {0: _place_own_shard("place_w_in", idx, big["w_in"]), 4: small_land}
            self.sems, self.lands, token = _gather_start("gather_start_0", first, (0,))
            rest = {i: _place_own_shard(f"place_{BIG[i]}", idx, big[BIG[i]], after=token) for i in (1, 2, 3)}
            sems, lands, token = _gather_start("gather_start_1", rest, (1, 2, 3), after=token)
            self.sems.update(sems)
            self.lands.update(lands)
            return token

        def mid(self, grp, after):
            if grp == 0:
                after = [after, big_m["w_in"], big_v["w_in"]]
            self.pending = _gather_mid(grp, self.lands, self.sems[grp], after)
            return self.pending[2]

        def end(self, grp, after):
            sems, bufs, _ = self.pending
            bufs, _ = _gather_end(grp, bufs, sems, after)
            if grp == 0:
                per_chip = [_unpack(bufs[1][j], sharded_shapes) for j in range(N_CHIPS)]
                out = {n: jnp.concatenate([per_chip[j][i] for j in range(N_CHIPS)], axis=1)
                       for i, n in enumerate(SMALL_SHARDED)}
                out["w_in"] = _w_in_shards_to_local(bufs[0])
                return out
            if grp == 2:
                return {"w_up": bufs[0]}
            return {("w_out" if grp == 1 else "w_down"): bufs[0].reshape(-1, bufs[0].shape[-1])}

    comm = _SplitComm()
    loss, grad_x, grads = _local_step(x[0], target[0], replicated, comm)
    loss = lax.psum(loss[0, 0], ("x", "y", "c"))
    out_g, out_d, out_m, out_v = {}, {}, {}, {}

    def update_big(n, grad):
        g, d, nm, nv = _adamw(f"adamw_{n}", big[n], grad, big_m[n], big_v[n])
        out_g[n], out_d[n], out_m[n], out_v[n] = g[None], d[None], nm[None], nv[None]
        return d

    early_grads = comm.finish_early([grad_x, loss.reshape(1, 1)])
    done = [update_big(n, early_grads[n]) for n in early]
    late = _pack([grads[small_late]], _pack_rows([global_shape(small_late)]))
    w_in_grad, small_grads, late_sum = comm.finish_late(done, late)
    update_big("w_in", w_in_grad)
    small_grads[small_late] = _unpack(late_sum, [global_shape(small_late)])[0]
    for n in SMALL_SHARDED:
        width = W[n].shape[2]
        small_grads[n] = lax.dynamic_slice_in_dim(small_grads[n], me * width, width, axis=1)

    for n in SMALL:
        out_g[n] = small_grads[n].reshape(W[n].shape)
    two_d = lambda a: a.reshape(1, -1) if a.ndim == 1 else a
    results = _adamw_many("adamw_small", *[[two_d(src[n]) for n in SMALL] for src in (W, out_g, M, V)])
    for dst, arrs in zip((out_d, out_m, out_v), results):
        dst.update({n: a.reshape(W[n].shape) for n, a in zip(SMALL, arrs)})
    return (loss, grad_x[None], *[out_g[n] for n in WEIGHT_NAMES], *[out_d[n] for n in WEIGHT_NAMES],
            *[out_m[n] for n in WEIGHT_NAMES], *[out_v[n] for n in WEIGHT_NAMES])


def kernel(x, norm_mix_g, w_in, b_gate_m, mlstm_norm_g, lru_conv_w, lru_conv_b, lru_wa, lru_ba, lru_wx, lru_bx, lru_lambda, w_out, norm_ffn_g, w_up, ffn_conv_w, ffn_conv_b, w_down, norm_final_g, loss_target, m_norm_mix_g, m_w_in, m_b_gate_m, m_mlstm_norm_g, m_lru_conv_w, m_lru_conv_b, m_lru_wa, m_lru_ba, m_lru_wx, m_lru_bx, m_lru_lambda, m_w_out, m_norm_ffn_g, m_w_up, m_ffn_conv_w, m_ffn_conv_b, m_w_down, m_norm_final_g, v_norm_mix_g, v_w_in, v_b_gate_m, v_mlstm_norm_g, v_lru_conv_w, v_lru_conv_b, v_lru_wa, v_lru_ba, v_lru_wx, v_lru_bx, v_lru_lambda, v_w_out, v_norm_ffn_g, v_w_up, v_ffn_conv_w, v_ffn_conv_b, v_w_down, v_norm_final_g):
    W = dict(zip(WEIGHT_NAMES, (norm_mix_g, w_in, b_gate_m, mlstm_norm_g, lru_conv_w, lru_conv_b, lru_wa, lru_ba,
                                lru_wx, lru_bx, lru_lambda, w_out, norm_ffn_g, w_up, ffn_conv_w, ffn_conv_b,
                                w_down, norm_final_g)))
    M = dict(zip(WEIGHT_NAMES, (m_norm_mix_g, m_w_in, m_b_gate_m, m_mlstm_norm_g, m_lru_conv_w, m_lru_conv_b,
                                m_lru_wa, m_lru_ba, m_lru_wx, m_lru_bx, m_lru_lambda, m_w_out, m_norm_ffn_g,
                                m_w_up, m_ffn_conv_w, m_ffn_conv_b, m_w_down, m_norm_final_g)))
    V = dict(zip(WEIGHT_NAMES, (v_norm_mix_g, v_w_in, v_b_gate_m, v_mlstm_norm_g, v_lru_conv_w, v_lru_conv_b,
                                v_lru_wa, v_lru_ba, v_lru_wx, v_lru_bx, v_lru_lambda, v_w_out, v_norm_ffn_g,
                                v_w_up, v_ffn_conv_w, v_ffn_conv_b, v_w_down, v_norm_final_g)))
    return _train_step(x, loss_target, W, M, V)
```

```python
import functools

import jax
import jax.numpy as jnp
from jax import lax
from jax.experimental import pallas as pl
from jax.experimental.pallas import tpu as pltpu

F32 = jnp.float32
BF16 = jnp.bfloat16
MESH = pl.DeviceIdType.MESH

EPS = 1e-6
CHUNK = 512
HEADS = 4
HEAD_DIM = 256
D_MLSTM = HEADS * HEAD_DIM
LRU_BLOCKS = 8
LRU_BLOCK_DIM = 128
D_LRU = LRU_BLOCKS * LRU_BLOCK_DIM
LRU_C = 8.0
LRU_CONV = 4
FFN_CONV = 3
ADAM_LR = 0.001
ADAM_B1 = 0.9
ADAM_B2 = 0.999
ADAM_EPS = 1e-08
ADAM_WD = 0.01
ADAM_STEP = 10

N_CHIPS = 4
N_DEV = 8
LANES = 128
HALO = 8
PROJ_GATE_PAD = LANES
_QKVO = 4 * D_MLSTM
_N_GATES = 2 * HEADS
_PROJ_COLS = _QKVO + _N_GATES + 2 * D_LRU
_PROJ_PAD = _QKVO + 2 * D_LRU + PROJ_GATE_PAD
VMEM_LIMIT = 48 * 1024 * 1024
ANY = pl.BlockSpec(memory_space=pl.ANY)


def _params(sem, vmem=VMEM_LIMIT):
    return pltpu.CompilerParams(dimension_semantics=sem, vmem_limit_bytes=vmem)


def _matmul(name, a, b, grid, a_spec, b_spec, o_spec, out_sds, contract, res=None, res_spec=None, after=None):
    nk = grid[2]
    acc_shape = tuple(d for d in o_spec.block_shape if d is not None)

    def body(*refs):
        refs = list(refs)
        a_ref, b_ref = refs[:2]
        r_ref = refs[2] if res is not None else None
        o_ref = refs[-1] if nk == 1 else refs[-2]
        acc_ref = None if nk == 1 else refs[-1]
        k = pl.program_id(2)

        def part():
            return lax.dot_general(a_ref[...], b_ref[...], (contract, ((), ())), preferred_element_type=F32)

        def finish(r):
            if r_ref is not None:
                r = r_ref[...] + r
            o_ref[...] = r.astype(o_ref.dtype)

        if nk == 1:
            finish(part())
            return

        @pl.when(k == 0)
        def _():
            acc_ref[...] = part()

        @pl.when(jnp.logical_and(k > 0, k < nk - 1))
        def _():
            acc_ref[...] += part()

        @pl.when(k == nk - 1)
        def _():
            finish(acc_ref[...] + part())

    in_specs = [a_spec, b_spec] + ([] if res is None else [res_spec]) + ([] if after is None else [ANY])
    args = (a, b) + (() if res is None else (res,)) + (() if after is None else (after,))
    if after is not None:
        inner = body
        body = lambda *refs: inner(*refs[:len(in_specs) - 1], *refs[len(in_specs):])
    return pl.pallas_call(
        body, out_shape=out_sds, grid=grid, in_specs=in_specs, out_specs=o_spec,
        scratch_shapes=[] if nk == 1 else [pltpu.VMEM(acc_shape, F32)], name=name,
        compiler_params=_params(("parallel", "parallel", "arbitrary")),
    )(*args)


NN = ((1,), (0,))
NT = ((1,), (1,))
TN = ((0,), (0,))


def _mm_nn(name, a, b, tm, tn, tk, out_dtype=F32, res=None):
    m, k = a.shape
    n = b.shape[1]
    return _matmul(name, a, b, (m // tm, n // tn, k // tk),
                   pl.BlockSpec((tm, tk), lambda i, j, kk: (i, kk)),
                   pl.BlockSpec((tk, tn), lambda i, j, kk: (kk, j)),
                   pl.BlockSpec((tm, tn), lambda i, j, kk: (i, j)),
                   jax.ShapeDtypeStruct((m, n), out_dtype), NN,
                   res=res, res_spec=pl.BlockSpec((tm, tn), lambda i, j, kk: (i, j)))


def _mm_nt(name, a, b, tm, tn, tk, out_dtype=F32, res=None, after=None):
    m, k = a.shape
    n = b.shape[0]
    return _matmul(name, a, b, (m // tm, n // tn, k // tk),
                   pl.BlockSpec((tm, tk), lambda i, j, kk: (i, kk)),
                   pl.BlockSpec((tn, tk), lambda i, j, kk: (j, kk)),
                   pl.BlockSpec((tm, tn), lambda i, j, kk: (i, j)),
                   jax.ShapeDtypeStruct((m, n), out_dtype), NT,
                   res=res, res_spec=pl.BlockSpec((tm, tn), lambda i, j, kk: (i, j)), after=after)


def _mm_tn(name, a, b, tm, tn, tk, out_dtype=F32):
    k, m = a.shape
    n = b.shape[1]
    tk = min(tk, k)
    return _matmul(name, a, b, (m // tm, n // tn, k // tk),
                   pl.BlockSpec((tk, tm), lambda i, j, kk: (kk, i)),
                   pl.BlockSpec((tk, tn), lambda i, j, kk: (kk, j)),
                   pl.BlockSpec((tm, tn), lambda i, j, kk: (i, j)),
                   jax.ShapeDtypeStruct((m, n), out_dtype), TN)


def _up_shard(n):
    return 2 * (n % 2) + (n // 2) // 2, (n // 2) % 2


def _mm_up_fwd(name, a, wg_up, tm, tk):
    m, k = a.shape
    _, _, cols = wg_up.shape
    tn = cols // 2
    return _matmul(name, a, wg_up, (m // tm, 2 * N_CHIPS, k // tk),
                   pl.BlockSpec((tm, tk), lambda i, j, kk: (i, kk)),
                   pl.BlockSpec((None, tk, tn), lambda i, j, kk: (_up_shard(j)[0], kk, _up_shard(j)[1])),
                   pl.BlockSpec((tm, tn), lambda i, j, kk: (i, j)),
                   jax.ShapeDtypeStruct((m, 2 * N_CHIPS * tn), F32), NN)


def _mm_up_bwd_x(name, dgu, wg_up, tm, tn):
    m, _ = dgu.shape
    _, d, cols = wg_up.shape
    tk = cols // 2
    nk = N_CHIPS

    def body(a_ref, bg_ref, bu_ref, o_ref, acc_ref):
        k = pl.program_id(2)

        def part():
            dims = (NT, ((), ()))
            return (lax.dot_general(a_ref[:, :tk], bg_ref[...], dims, preferred_element_type=F32)
                    + lax.dot_general(a_ref[:, tk:], bu_ref[...], dims, preferred_element_type=F32))

        @pl.when(k == 0)
        def _():
            acc_ref[...] = part()

        @pl.when(jnp.logical_and(k > 0, k < nk - 1))
        def _():
            acc_ref[...] += part()

        @pl.when(k == nk - 1)
        def _():
            o_ref[...] = acc_ref[...] + part()

    wspec = lambda half: pl.BlockSpec(
        (None, tn, tk), lambda i, j, kk: (_up_shard(2 * kk + half)[0], j, _up_shard(2 * kk + half)[1]))
    return pl.pallas_call(
        body, name=name, grid=(m // tm, d // tn, nk), out_shape=jax.ShapeDtypeStruct((m, d), F32),
        in_specs=[pl.BlockSpec((tm, 2 * tk), lambda i, j, kk: (i, kk)), wspec(0), wspec(1)],
        out_specs=pl.BlockSpec((tm, tn), lambda i, j, kk: (i, j)),
        scratch_shapes=[pltpu.VMEM((tm, tn), F32)],
        compiler_params=_params(("parallel", "parallel", "arbitrary")),
    )(dgu, wg_up, wg_up)


def _mm_up_bwd_w(name, n2, dgu, tm, tk):
    s, d = n2.shape
    tk = min(tk, s)
    tn = dgu.shape[1] // (2 * N_CHIPS)
    return _matmul(name, n2, dgu, (d // tm, 2 * N_CHIPS, s // tk),
                   pl.BlockSpec((tk, tm), lambda i, j, kk: (kk, i)),
                   pl.BlockSpec((tk, tn), lambda i, j, kk: (kk, j)),
                   pl.BlockSpec((None, tm, tn), lambda i, j, kk: (_up_shard(j)[0], i, _up_shard(j)[1])),
                   jax.ShapeDtypeStruct((N_CHIPS, d, 2 * tn), F32), TN)


def _rmsnorm_fwd(name, x, g, tm=256):
    s, d = x.shape

    def body(x_ref, g_ref, n_ref, r_ref):
        xf = x_ref[...]
        r = lax.rsqrt(jnp.mean(xf * xf, axis=-1, keepdims=True) + EPS)
        n_ref[...] = ((xf * r) * g_ref[...]).astype(BF16)
        r_ref[...] = r

    return pl.pallas_call(
        body, grid=(s // tm,), name=name,
        in_specs=[pl.BlockSpec((tm, d), lambda i: (i, 0)), pl.BlockSpec((1, d), lambda i: (0, 0))],
        out_specs=[pl.BlockSpec((tm, d), lambda i: (i, 0)), pl.BlockSpec((tm, 1), lambda i: (i, 0))],
        out_shape=[jax.ShapeDtypeStruct((s, d), BF16), jax.ShapeDtypeStruct((s, 1), F32)],
        compiler_params=_params(("parallel",)),
    )(x, g)


def _rmsnorm_bwd(name, x, rstd, g, dn, dres, tm=256):
    s, d = x.shape

    def body(x_ref, r_ref, g_ref, dn_ref, dres_ref, dx_ref, dxb_ref, dg_ref):
        @pl.when(pl.program_id(0) == 0)
        def _():
            dg_ref[...] = jnp.zeros_like(dg_ref)

        r = r_ref[...]
        xhat = x_ref[...] * r
        dn_v = dn_ref[...]
        dxhat = dn_v * g_ref[...]
        dx = dres_ref[...] + r * (dxhat - xhat * jnp.mean(dxhat * xhat, axis=-1, keepdims=True))
        dx_ref[...] = dx
        dxb_ref[...] = dx.astype(BF16)
        dg_ref[...] += jnp.sum(dn_v * xhat, axis=0, keepdims=True)

    row = pl.BlockSpec((tm, d), lambda i: (i, 0))
    vec = pl.BlockSpec((1, d), lambda i: (0, 0))
    return pl.pallas_call(
        body, grid=(s // tm,), name=name,
        in_specs=[row, pl.BlockSpec((tm, 1), lambda i: (i, 0)), vec, row, row],
        out_specs=[row, row, vec],
        out_shape=[jax.ShapeDtypeStruct((s, d), F32), jax.ShapeDtypeStruct((s, d), BF16),
                   jax.ShapeDtypeStruct((1, d), F32)],
        compiler_params=_params(("arbitrary",)),
    )(x, rstd, g, dn, dres)


def _loss_head(name, x, g, target, tm=256):
    s, d = x.shape

    def body(x_ref, g_ref, t_ref, loss_ref, dx_ref, dxb_ref, dg_ref):
        @pl.when(pl.program_id(0) == 0)
        def _():
            dg_ref[...] = jnp.zeros_like(dg_ref)
            loss_ref[...] = jnp.zeros_like(loss_ref)

        xf = x_ref[...]
        gv = g_ref[...]
        r = lax.rsqrt(jnp.mean(xf * xf, axis=-1, keepdims=True) + EPS)
        xhat = xf * r
        err = xhat * gv - t_ref[...]
        loss_ref[...] += 0.5 * jnp.sum(jnp.mean(err * err, axis=-1, keepdims=True), axis=0, keepdims=True)
        dy = err * (1.0 / d)
        dxhat = dy * gv
        dx = r * (dxhat - xhat * jnp.mean(dxhat * xhat, axis=-1, keepdims=True))
        dx_ref[...] = dx
        dxb_ref[...] = dx.astype(BF16)
        dg_ref[...] += jnp.sum(dy * xhat, axis=0, keepdims=True)

    row = pl.BlockSpec((tm, d), lambda i: (i, 0))
    vec = pl.BlockSpec((1, d), lambda i: (0, 0))
    return pl.pallas_call(
        body, grid=(s // tm,), name=name,
        in_specs=[row, vec, row],
        out_specs=[pl.BlockSpec((1, 1), lambda i: (0, 0)), row, row, vec],
        out_shape=[jax.ShapeDtypeStruct((1, 1), F32), jax.ShapeDtypeStruct((s, d), F32),
                   jax.ShapeDtypeStruct((s, d), BF16), jax.ShapeDtypeStruct((1, d), F32)],
        compiler_params=_params(("arbitrary",)),
    )(x, g, target)


def _sigmoid(v):
    return 1.0 / (1.0 + jnp.exp(-v))


def _log_sigmoid(v):
    return jnp.minimum(v, 0.0) - jnp.log1p(jnp.exp(-jnp.abs(v)))


def _softplus(v):
    return jnp.maximum(v, 0.0) + jnp.log1p(jnp.exp(-jnp.abs(v)))


def _one_minus_exp(z):
    series = -z * (1.0 + z * (0.5 + z * (1.0 / 6.0 + z * (1.0 / 24.0 + z * (1.0 / 120.0)))))
    return jnp.where(z > -0.1, series, 1.0 - jnp.exp(z))


_GELU_K = 0.7978845608028654
_GELU_C = 0.044715


def _gelu(v):
    return 0.5 * v * (1.0 + jnp.tanh(_GELU_K * (v + _GELU_C * v * v * v)))


def _gelu_grad(v):
    t = jnp.tanh(_GELU_K * (v + _GELU_C * v * v * v))
    return 0.5 * (1.0 + t) + 0.5 * v * (1.0 - t * t) * _GELU_K * (1.0 + 3.0 * _GELU_C * v * v)


def _rows(shape):
    return lax.broadcasted_iota(jnp.int32, shape, 0)


def _cols(shape):
    return lax.broadcasted_iota(jnp.int32, shape, 1)


def _shift_down(v, prev, d):
    if d == 0:
        return v
    rolled = pltpu.roll(v, d, axis=0)
    head = jnp.where(_rows((HALO, v.shape[1])) >= d, rolled[:HALO], pltpu.roll(prev, d, axis=0))
    if v.shape[0] == HALO:
        return head
    return jnp.concatenate([head, rolled[HALO:]], axis=0)


def _shift_up(v, nxt, d):
    if d == 0:
        return v
    n = v.shape[0]
    rolled = pltpu.roll(v, n - d, axis=0)
    tail = jnp.where(_rows((HALO, v.shape[1])) < HALO - d, rolled[n - HALO:], pltpu.roll(nxt, HALO - d, axis=0))
    if n == HALO:
        return tail
    return jnp.concatenate([rolled[:n - HALO], tail], axis=0)


def _dot(a, b, contract):
    return lax.dot_general(a.astype(BF16), b.astype(BF16), (contract, ((), ())), preferred_element_type=F32)


def _mlstm_chunk_common(h, q_ref, k_ref, v_ref, gcol_ref, grow_ref, brow_ref, bcol_ref, m_prev):
    L = CHUNK
    sl = slice(h * HEAD_DIM, (h + 1) * HEAD_DIM)
    qh = q_ref[:, sl]
    kh = k_ref[:, sl]
    vh = v_ref[:, sl]
    qs = qh * (HEAD_DIM ** -0.5)
    gates = gcol_ref[...] + brow_ref[...]
    lane = _cols(gates.shape)
    ic = jnp.sum(jnp.where(lane == h, gates, 0.0), axis=1, keepdims=True)
    fc = jnp.sum(jnp.where(lane == HEADS + h, gates, 0.0), axis=1, keepdims=True)
    ir = grow_ref[h:h + 1, :] + bcol_ref[h:h + 1, :]
    fr = grow_ref[HEADS + h:HEADS + h + 1, :] + bcol_ref[HEADS + h:HEADS + h + 1, :]
    logf_c = _log_sigmoid(fc)
    logf_r = _log_sigmoid(fr)
    t_i = _rows((L, L))
    s_i = _cols((L, L))
    tri = t_i >= s_i
    b_c = jnp.sum(jnp.where(tri, logf_r, 0.0), axis=1, keepdims=True)
    b_r = jnp.sum(jnp.where(t_i <= s_i, logf_c, 0.0), axis=0, keepdims=True)
    btot = jnp.sum(logf_r, axis=1, keepdims=True)
    dmat = jnp.where(tri, b_c - b_r + ir, -jnp.inf)
    m_inter = b_c + m_prev
    m_t = jnp.maximum(m_inter, jnp.max(dmat, axis=1, keepdims=True))
    e_mat = jnp.exp(dmat - m_t)
    e_inter = jnp.exp(m_inter - m_t)
    wqk = _dot(qs, kh, NT) * e_mat
    w_end_r = btot - b_r + ir
    m_loc = jnp.max(w_end_r, axis=1, keepdims=True)
    e_end_c = jnp.exp(btot - b_c + ic - m_loc)
    m_new = jnp.maximum(btot + m_prev, m_loc)
    a_dec = jnp.exp(btot + m_prev - m_new)
    c_inj = jnp.exp(m_loc - m_new)
    return dict(qh=qh, kh=kh, vh=vh, qs=qs, fc=fc, tri=tri, t_i=t_i, s_i=s_i, m_t=m_t, e_mat=e_mat,
                e_inter=e_inter, wqk=wqk, e_end_c=e_end_c, m_new=m_new, a_dec=a_dec, c_inj=c_inj)


def _mlstm_fwd(proj, gates_t, bias_row, bias_col, head_g):
    s = proj.shape[0]
    nc = s // CHUNK
    L = CHUNK

    def body(q_ref, k_ref, v_ref, o_ref, gcol_ref, grow_ref, brow_ref, bcol_ref, hg_ref,
             out_ref, cprev_ref, nprev_ref, mprev_ref, c_scr, n_scr, m_scr):
        @pl.when(pl.program_id(0) == 0)
        def _():
            c_scr[...] = jnp.zeros_like(c_scr)
            n_scr[...] = jnp.zeros_like(n_scr)
            m_scr[...] = jnp.zeros_like(m_scr)

        for h in range(HEADS):
            sl = slice(h * HEAD_DIM, (h + 1) * HEAD_DIM)
            m_prev = m_scr[h:h + 1, 0:1]
            n_prev = n_scr[h:h + 1, :]
            c_prev = c_scr[h].astype(BF16)
            q = _mlstm_chunk_common(h, q_ref, k_ref, v_ref, gcol_ref, grow_ref, brow_ref, bcol_ref, m_prev)
            num = _dot(q["wqk"], q["vh"], NN) + q["e_inter"] * _dot(q["qs"], c_prev, NN)
            den = (jnp.sum(q["wqk"], axis=1, keepdims=True)
                   + q["e_inter"] * jnp.sum(q["qs"] * n_prev, axis=1, keepdims=True))
            hh = num / jnp.maximum(jnp.abs(den), jnp.exp(-q["m_t"]))
            hn = hh * lax.rsqrt(jnp.mean(hh * hh, axis=1, keepdims=True) + EPS) * hg_ref[h:h + 1, :]
            out_ref[:, sl] = (_sigmoid(o_ref[:, sl]) * hn).astype(BF16)
            cprev_ref[h] = c_prev
            nprev_ref[h:h + 1, :] = n_prev
            mprev_ref[h:h + 1, :] = jnp.broadcast_to(m_prev, (1, LANES))
            c_loc = _dot(q["kh"], q["e_end_c"] * q["vh"], TN)
            n_loc = jnp.sum(q["e_end_c"] * q["kh"], axis=0, keepdims=True)
            c_scr[h] = q["a_dec"] * c_scr[h] + q["c_inj"] * c_loc
            n_scr[h:h + 1, :] = q["a_dec"] * n_prev + q["c_inj"] * n_loc
            m_scr[h:h + 1, :] = jnp.broadcast_to(q["m_new"], (1, LANES))

    blk = lambda j: pl.BlockSpec((L, D_MLSTM), lambda c, j=j: (c, j))
    full = lambda shp: pl.BlockSpec(shp, lambda c: tuple(0 for _ in shp))
    return pl.pallas_call(
        body, grid=(nc,), name="mlstm_fwd",
        in_specs=[blk(0), blk(1), blk(2), blk(3),
                  pl.BlockSpec((L, LANES), lambda c: (c, (4 * D_MLSTM + 2 * D_LRU) // LANES)),
                  pl.BlockSpec((None, 2 * HEADS, L), lambda c: (c, 0, 0)),
                  full((1, LANES)), full((2 * HEADS, 1)), full((HEADS, HEAD_DIM))],
        out_specs=[pl.BlockSpec((L, D_MLSTM), lambda c: (c, 0)),
                   pl.BlockSpec((None, HEADS, HEAD_DIM, HEAD_DIM), lambda c: (c, 0, 0, 0)),
                   pl.BlockSpec((None, HEADS, HEAD_DIM), lambda c: (c, 0, 0)),
                   pl.BlockSpec((None, HEADS, LANES), lambda c: (c, 0, 0))],
        out_shape=[jax.ShapeDtypeStruct((s, D_MLSTM + D_LRU), BF16),
                   jax.ShapeDtypeStruct((nc, HEADS, HEAD_DIM, HEAD_DIM), BF16),
                   jax.ShapeDtypeStruct((nc, HEADS, HEAD_DIM), F32),
                   jax.ShapeDtypeStruct((nc, HEADS, LANES), F32)],
        scratch_shapes=[pltpu.VMEM((HEADS, HEAD_DIM, HEAD_DIM), F32), pltpu.VMEM((HEADS, HEAD_DIM), F32),
                        pltpu.VMEM((HEADS, LANES), F32)],
        compiler_params=_params(("arbitrary",)),
    )(proj, proj, proj, proj, proj, gates_t, bias_row, bias_col, head_g)


def _mlstm_bwd(proj, gates_t, bias_row, bias_col, head_g, cprev, nprev, mprev, dmix):
    s = proj.shape[0]
    nc = s // CHUNK
    L = CHUNK

    def body(q_ref, k_ref, v_ref, o_ref, gcol_ref, grow_ref, brow_ref, bcol_ref, hg_ref,
             cprev_ref, nprev_ref, mprev_ref, dmix_ref,
             dqkvo_ref, dgate_ref, dhg_ref, g_scr, gn_scr):
        @pl.when(pl.program_id(0) == 0)
        def _():
            g_scr[...] = jnp.zeros_like(g_scr)
            gn_scr[...] = jnp.zeros_like(gn_scr)
            dhg_ref[...] = jnp.zeros_like(dhg_ref)

        lane = _cols((L, LANES))
        dgate = jnp.zeros((L, LANES), F32)
        for h in range(HEADS):
            sl = slice(h * HEAD_DIM, (h + 1) * HEAD_DIM)
            m_prev = mprev_ref[h:h + 1, 0:1]
            n_prev = nprev_ref[h:h + 1, :]
            c_prev = cprev_ref[h]
            q = _mlstm_chunk_common(h, q_ref, k_ref, v_ref, gcol_ref, grow_ref, brow_ref, bcol_ref, m_prev)
            qh, kh, vh, qs, wqk, e_inter = q["qh"], q["kh"], q["vh"], q["qs"], q["wqk"], q["e_inter"]
            num_state = e_inter * _dot(qs, c_prev, NN)
            den_state = e_inter * jnp.sum(qs * n_prev, axis=1, keepdims=True)
            num = _dot(wqk, vh, NN) + num_state
            den = jnp.sum(wqk, axis=1, keepdims=True) + den_state
            floor = jnp.exp(-q["m_t"])
            denom = jnp.maximum(jnp.abs(den), floor)
            hh = num / denom
            rn = lax.rsqrt(jnp.mean(hh * hh, axis=1, keepdims=True) + EPS)
            hn_pre = hh * rn
            hg = hg_ref[h:h + 1, :]
            sg = _sigmoid(o_ref[:, sl])
            dout = dmix_ref[:, sl]
            d_o = dout * (hn_pre * hg) * sg * (1.0 - sg)
            dhn = dout * sg
            dhg_ref[h:h + 1, :] += jnp.sum(dhn * hn_pre, axis=0, keepdims=True)
            dhn_pre = dhn * hg
            dhh = rn * (dhn_pre - hn_pre * jnp.mean(dhn_pre * hn_pre, axis=1, keepdims=True))
            dnum = dhh / denom
            dden = jnp.where(jnp.abs(den) >= floor,
                             -jnp.sum(hh * dhh, axis=1, keepdims=True) / denom * jnp.sign(den), 0.0)
            dwqk = _dot(dnum, vh, NT) + dden
            dv = _dot(wqk, dnum, TN)
            dp = dwqk * q["e_mat"]
            dqs = _dot(dp, kh, NN) + e_inter * (_dot(dnum, c_prev, NT) + dden * n_prev)
            dk = _dot(dp, qs, TN)
            g_next = g_scr[h]
            gn_next = gn_scr[h:h + 1, :]
            w_state = q["e_end_c"] * q["c_inj"]
            dk_state = w_state * (_dot(vh, g_next, NT) + gn_next)
            dk = dk + dk_state
            dv = dv + w_state * _dot(kh, g_next, NN)
            dq = dqs * (HEAD_DIM ** -0.5)
            eye = q["t_i"] == q["s_i"]
            to_row = lambda col: jnp.sum(jnp.where(eye, col, 0.0), axis=0, keepdims=True)
            to_col = lambda row: jnp.sum(jnp.where(eye, row, 0.0), axis=1, keepdims=True)
            g_pair = dwqk * wqk
            rs_in = jnp.sum(g_pair, axis=1, keepdims=True)
            cs_in_r = jnp.sum(g_pair, axis=0, keepdims=True)
            rs_state = (jnp.sum(dnum * num_state, axis=1, keepdims=True) + dden * den_state)
            cs_state = jnp.sum(kh * dk_state, axis=1, keepdims=True)
            di_c = to_col(cs_in_r) + cs_state
            through = q["a_dec"] * (jnp.sum(jnp.sum(g_next * c_prev.astype(F32), axis=1, keepdims=True),
                                            axis=0, keepdims=True)
                                    + jnp.sum(gn_next * n_prev, axis=1, keepdims=True))
            ends_here = to_row(rs_in + rs_state) - cs_in_r
            da_c = (jnp.sum(jnp.where(q["s_i"] >= q["t_i"], ends_here, 0.0), axis=1, keepdims=True)
                    + jnp.sum(jnp.where(q["s_i"] < q["t_i"], to_row(cs_state), 0.0), axis=1, keepdims=True)
                    + through)
            df_c = da_c * _sigmoid(-q["fc"])
            dgate = dgate + jnp.where(lane == h, di_c, 0.0) + jnp.where(lane == HEADS + h, df_c, 0.0)
            dqkvo_ref[:, sl] = dq.astype(BF16)
            dqkvo_ref[:, D_MLSTM + h * HEAD_DIM:D_MLSTM + (h + 1) * HEAD_DIM] = dk.astype(BF16)
            dqkvo_ref[:, 2 * D_MLSTM + h * HEAD_DIM:2 * D_MLSTM + (h + 1) * HEAD_DIM] = dv.astype(BF16)
            dqkvo_ref[:, 3 * D_MLSTM + h * HEAD_DIM:3 * D_MLSTM + (h + 1) * HEAD_DIM] = d_o.astype(BF16)
            g_scr[h] = q["a_dec"] * g_next + _dot(e_inter * qs, dnum, TN)
            gn_scr[h:h + 1, :] = q["a_dec"] * gn_next + jnp.sum(e_inter * qs * dden, axis=0, keepdims=True)
        dgate_ref[...] = dgate

    rev = lambda c: nc - 1 - c
    blk = lambda j: pl.BlockSpec((L, D_MLSTM), lambda c, j=j: (rev(c), j))
    full = lambda shp: pl.BlockSpec(shp, lambda c: tuple(0 for _ in shp))
    return pl.pallas_call(
        body, grid=(nc,), name="mlstm_bwd",
        in_specs=[blk(0), blk(1), blk(2), blk(3),
                  pl.BlockSpec((L, LANES), lambda c: (rev(c), (4 * D_MLSTM + 2 * D_LRU) // LANES)),
                  pl.BlockSpec((None, 2 * HEADS, L), lambda c: (rev(c), 0, 0)),
                  full((1, LANES)), full((2 * HEADS, 1)), full((HEADS, HEAD_DIM)),
                  pl.BlockSpec((None, HEADS, HEAD_DIM, HEAD_DIM), lambda c: (rev(c), 0, 0, 0)),
                  pl.BlockSpec((None, HEADS, HEAD_DIM), lambda c: (rev(c), 0, 0)),
                  pl.BlockSpec((None, HEADS, LANES), lambda c: (rev(c), 0, 0)),
                  pl.BlockSpec((L, D_MLSTM), lambda c: (rev(c), 0))],
        out_specs=[pl.BlockSpec((L, 4 * D_MLSTM), lambda c: (rev(c), 0)),
                   pl.BlockSpec((L, LANES), lambda c: (rev(c), 0)),
                   full((HEADS, HEAD_DIM))],
        out_shape=[jax.ShapeDtypeStruct((s, _PROJ_PAD), BF16),
                   jax.ShapeDtypeStruct((s, LANES), F32),
                   jax.ShapeDtypeStruct((HEADS, HEAD_DIM), F32)],
        scratch_shapes=[pltpu.VMEM((HEADS, HEAD_DIM, HEAD_DIM), F32), pltpu.VMEM((HEADS, HEAD_DIM), F32)],
        compiler_params=_params(("arbitrary",)),
    )(proj, proj, proj, proj, proj, gates_t, bias_row, bias_col, head_g, cprev, nprev, mprev, dmix)


def _lru_gates(xc, wa_ref, wx_ref, ba, bx, lam):
    r = _sigmoid(_dot(xc, wa_ref[...], NN) + ba)
    ig = _sigmoid(_dot(xc, wx_ref[...], NN) + bx)
    sp = _softplus(-lam)
    log_a = (-LRU_C * r) * sp
    a = jnp.exp(log_a)
    mult = jnp.sqrt(_one_minus_exp(2.0 * log_a))
    return r, ig, sp, a, mult


def _lru_conv(xr, prev, w_ref, b):
    xc = b + _shift_down(xr, prev, 3) * w_ref[0:1, :]
    for j in range(1, LRU_CONV):
        xc = xc + _shift_down(xr, prev, LRU_CONV - 1 - j) * w_ref[j:j + 1, :]
    return xc


def _lru_fwd(proj, mix, conv_w, conv_b, wa, wx, ba, bx, lam, tt=1024):
    s = proj.shape[0]
    tt = min(tt, s)
    nt = s // tt
    B = LRU_BLOCK_DIM
    lru_col = 4 * D_MLSTM // B
    mix_col = D_MLSTM // B

    def body(xr_ref, gr_ref, cw_ref, cb_ref, wa_ref, wx_ref, ba_ref, bx_ref, lam_ref, mix_in_ref,
             out_ref, h_ref, prev_scr, hcar_scr):
        @pl.when(pl.program_id(1) == 0)
        def _():
            prev_scr[...] = jnp.zeros_like(prev_scr)
            hcar_scr[...] = jnp.zeros_like(hcar_scr)

        xr = xr_ref[...]
        xc = _lru_conv(xr, prev_scr[...], cw_ref, cb_ref[...])
        prev_scr[...] = xr[tt - HALO:, :]
        _, ig, _, a, mult = _lru_gates(xc, wa_ref, wx_ref, ba_ref[...], bx_ref[...], lam_ref[...])
        u = mult * (ig * xc)
        rows = _rows((tt, B))
        acc_a, acc_b = a, u
        d = 1
        while d < tt:
            if d < HALO:
                keep = rows >= d
                sh_a = jnp.where(keep, pltpu.roll(acc_a, d, axis=0), 1.0)
                sh_b = jnp.where(keep, pltpu.roll(acc_b, d, axis=0), 0.0)
            else:
                sh_a = jnp.concatenate([jnp.ones((d, B), F32), acc_a[:tt - d]], axis=0)
                sh_b = jnp.concatenate([jnp.zeros((d, B), F32), acc_b[:tt - d]], axis=0)
            acc_b = acc_a * sh_b + acc_b
            acc_a = acc_a * sh_a
            d *= 2
        hv = acc_b + acc_a * hcar_scr[0:1, :]
        hcar_scr[...] = jnp.broadcast_to(hv[tt - 1:tt, :], hcar_scr.shape)
        h_ref[...] = hv
        out_ref[...] = (hv * _gelu(gr_ref[...])).astype(BF16)

    chan = lambda rws: pl.BlockSpec((rws, B), lambda n, i: (0, n))
    return pl.pallas_call(
        body, grid=(LRU_BLOCKS, nt), name="lru_fwd",
        in_specs=[pl.BlockSpec((tt, B), lambda n, i: (i, lru_col + 2 * n)),
                  pl.BlockSpec((tt, B), lambda n, i: (i, lru_col + 2 * n + 1)),
                  chan(LRU_CONV), chan(1),
                  pl.BlockSpec((None, B, B), lambda n, i: (n, 0, 0)),
                  pl.BlockSpec((None, B, B), lambda n, i: (n, 0, 0)),
                  chan(1), chan(1), chan(1), ANY],
        out_specs=[pl.BlockSpec((tt, B), lambda n, i: (i, mix_col + n)), pl.BlockSpec((tt, B), lambda n, i: (i, n))],
        out_shape=[jax.ShapeDtypeStruct(mix.shape, BF16), jax.ShapeDtypeStruct((s, D_LRU), F32)],
        scratch_shapes=[pltpu.VMEM((HALO, B), F32), pltpu.VMEM((HALO, B), F32)],
        input_output_aliases={9: 0},
        compiler_params=_params(("parallel", "arbitrary")),
    )(proj, proj, conv_w, conv_b, wa, wx, ba, bx, lam, mix)


def _lru_bwd(proj, hsave, dmix, dproj, conv_w, conv_b, wa, wx, ba, bx, lam, tt=1024):
    s = proj.shape[0]
    tt = min(tt, s)
    nt = s // tt
    B = LRU_BLOCK_DIM
    lru_col = 4 * D_MLSTM // B
    dmix_col = D_MLSTM // B
    hpb = tt // HALO

    def body(xr_ref, xprev_ref, gr_ref, h_ref, hprev_ref, dmix_ref, cw_ref, cb_ref, wa_ref, wx_ref,
             ba_ref, bx_ref, lam_ref, dproj_in_ref,
             dxg_ref, dcw_ref, dcb_ref, dwa_ref, dwx_ref, dba_ref, dbx_ref, dlam_ref,
             gcar_scr, acar_scr, dxc_scr):
        i = pl.program_id(1)
        first_tile = i == nt - 1

        @pl.when(i == 0)
        def _():
            gcar_scr[...] = jnp.zeros_like(gcar_scr)
            acar_scr[...] = jnp.zeros_like(acar_scr)
            dxc_scr[...] = jnp.zeros_like(dxc_scr)
            for ref in (dcw_ref, dcb_ref, dwa_ref, dwx_ref, dba_ref, dbx_ref, dlam_ref):
                ref[...] = jnp.zeros_like(ref)

        xr = xr_ref[...]
        xprev = jnp.where(first_tile, 0.0, xprev_ref[...])
        hprev = jnp.where(first_tile, 0.0, hprev_ref[...])
        lam = lam_ref[...]
        taps = [_shift_down(xr, xprev, LRU_CONV - 1 - j) for j in range(LRU_CONV)]
        xc = cb_ref[...] + taps[0] * cw_ref[0:1, :]
        for j in range(1, LRU_CONV):
            xc = xc + taps[j] * cw_ref[j:j + 1, :]
        r, ig, sp, a, mult = _lru_gates(xc, wa_ref, wx_ref, ba_ref[...], bx_ref[...], lam)
        gr = gr_ref[...]
        hv = h_ref[...]
        dout = dmix_ref[...]
        dxg_ref[:, B:] = (dout * hv * _gelu_grad(gr)).astype(BF16)
        dh = dout * _gelu(gr)
        rows = _rows((tt, B))
        acc_a = _shift_up(a, acar_scr[...], 1)
        acc_b = dh
        d = 1
        while d < tt:
            if d < HALO:
                keep = rows < tt - d
                sh_a = jnp.where(keep, pltpu.roll(acc_a, tt - d, axis=0), 1.0)
                sh_b = jnp.where(keep, pltpu.roll(acc_b, tt - d, axis=0), 0.0)
            else:
                sh_a = jnp.concatenate([acc_a[d:], jnp.ones((d, B), F32)], axis=0)
                sh_b = jnp.concatenate([acc_b[d:], jnp.zeros((d, B), F32)], axis=0)
            acc_b = acc_a * sh_b + acc_b
            acc_a = acc_a * sh_a
            d *= 2
        gv = acc_b + acc_a * gcar_scr[0:1, :]
        gcar_scr[...] = jnp.broadcast_to(gv[0:1, :], gcar_scr.shape)
        acar_scr[...] = jnp.broadcast_to(a[0:1, :], acar_scr.shape)
        h_before = _shift_down(hv, hprev, 1)
        da = gv * h_before
        dmult = gv * (ig * xc)
        dig = gv * mult * xc
        dxc = gv * mult * ig
        dlog_a = da * a - dmult * (a * a) / mult
        dr = dlog_a * (-LRU_C * sp)
        dlam_ref[...] += jnp.sum(dlog_a * (-LRU_C * r), axis=0, keepdims=True) * (-_sigmoid(-lam))
        dpre_r = dr * r * (1.0 - r)
        dpre_i = dig * ig * (1.0 - ig)
        dba_ref[...] += jnp.sum(dpre_r, axis=0, keepdims=True)
        dbx_ref[...] += jnp.sum(dpre_i, axis=0, keepdims=True)
        dwa_ref[...] += _dot(xc, dpre_r, TN)
        dwx_ref[...] += _dot(xc, dpre_i, TN)
        dxc = dxc + _dot(dpre_r, wa_ref[...], NT) + _dot(dpre_i, wx_ref[...], NT)
        dcb_ref[...] += jnp.sum(dxc, axis=0, keepdims=True)
        nxt = dxc_scr[...]
        dxr = jnp.zeros((tt, B), F32)
        for j in range(LRU_CONV):
            sft = LRU_CONV - 1 - j
            dcw_ref[j:j + 1, :] += jnp.sum(dxc * taps[j], axis=0, keepdims=True)
            dxr = dxr + _shift_up(dxc, nxt, sft) * cw_ref[j:j + 1, :]
        dxc_scr[...] = dxc[:HALO, :]
        dxg_ref[:, :B] = dxr.astype(BF16)

    rev = lambda i: nt - 1 - i
    tile = lambda col, step: pl.BlockSpec((tt, B), lambda n, i: (rev(i), col + step * n))
    halo = lambda col, step: pl.BlockSpec(
        (HALO, B), lambda n, i: (jnp.maximum(rev(i) * hpb - 1, 0), col + step * n))
    chan = lambda rws: pl.BlockSpec((rws, B), lambda n, i: (0, n))
    wblk = pl.BlockSpec((None, B, B), lambda n, i: (n, 0, 0))
    return pl.pallas_call(
        body, grid=(LRU_BLOCKS, nt), name="lru_bwd",
        in_specs=[tile(lru_col, 2), halo(lru_col, 2), tile(lru_col + 1, 2), tile(0, 1), halo(0, 1),
                  tile(dmix_col, 1), chan(LRU_CONV), chan(1), wblk, wblk, chan(1), chan(1), chan(1), ANY],
        out_specs=[pl.BlockSpec((tt, 2 * B), lambda n, i: (rev(i), lru_col // 2 + n)),
                   chan(LRU_CONV), chan(1), wblk, wblk, chan(1), chan(1), chan(1)],
        out_shape=[jax.ShapeDtypeStruct(dproj.shape, BF16),
                   jax.ShapeDtypeStruct((LRU_CONV, D_LRU), F32), jax.ShapeDtypeStruct((1, D_LRU), F32),
                   jax.ShapeDtypeStruct((LRU_BLOCKS, B, B), F32), jax.ShapeDtypeStruct((LRU_BLOCKS, B, B), F32),
                   jax.ShapeDtypeStruct((1, D_LRU), F32), jax.ShapeDtypeStruct((1, D_LRU), F32),
                   jax.ShapeDtypeStruct((1, D_LRU), F32)],
        scratch_shapes=[pltpu.VMEM((HALO, B), F32), pltpu.VMEM((HALO, B), F32), pltpu.VMEM((HALO, B), F32)],
        input_output_aliases={13: 0},
        compiler_params=_params(("parallel", "arbitrary")),
    )(proj, proj, proj, hsave, hsave, dmix, conv_w, conv_b, wa, wx, ba, bx, lam, dproj)


def _ffn_conv(gp, prev, w_ref, b):
    g = b + _shift_down(gp, prev, 2) * w_ref[0:1, :]
    for j in range(1, FFN_CONV):
        g = g + _shift_down(gp, prev, FFN_CONV - 1 - j) * w_ref[j:j + 1, :]
    return g


def _ffn_act_fwd(gu, conv_w, conv_b, tt=256):
    s = gu.shape[0]
    tt = min(tt, s)
    d_ff = conv_w.shape[1]
    tc = d_ff // N_CHIPS
    hpb = tt // HALO

    def body(g_ref, gprev_ref, u_ref, w_ref, b_ref, act_ref):
        prev = jnp.where(pl.program_id(0) == 0, 0.0, gprev_ref[...])
        gate = _ffn_conv(g_ref[...], prev, w_ref, b_ref[...])
        act_ref[...] = (gate * _sigmoid(gate) * u_ref[...]).astype(BF16)

    return pl.pallas_call(
        body, grid=(s // tt, N_CHIPS), name="ffn_act_fwd",
        in_specs=[pl.BlockSpec((tt, tc), lambda i, j: (i, 2 * j)),
                  pl.BlockSpec((HALO, tc), lambda i, j: (jnp.maximum(i * hpb - 1, 0), 2 * j)),
                  pl.BlockSpec((tt, tc), lambda i, j: (i, 2 * j + 1)),
                  pl.BlockSpec((FFN_CONV, tc), lambda i, j: (0, j)),
                  pl.BlockSpec((1, tc), lambda i, j: (0, j))],
        out_specs=pl.BlockSpec((tt, tc), lambda i, j: (i, j)),
        out_shape=jax.ShapeDtypeStruct((s, d_ff), BF16),
        compiler_params=_params(("parallel", "parallel")),
    )(gu, gu, gu, conv_w, conv_b)


def _ffn_act_bwd(gu, dact, conv_w, conv_b, tt=256):
    s = gu.shape[0]
    tt = min(tt, s)
    nt = s // tt
    d_ff = conv_w.shape[1]
    tc = d_ff // N_CHIPS
    hpb = tt // HALO

    def dgate_of(gate, up, da):
        sg = _sigmoid(gate)
        return da * up * (sg * (1.0 + gate * (1.0 - sg))), da * (gate * sg)

    def body(g_ref, gprev_ref, gnext_ref, u_ref, unext_ref, da_ref, danext_ref, w_ref, b_ref,
             dgu_ref, dw_ref, db_ref):
        i = pl.program_id(1)

        @pl.when(i == 0)
        def _():
            dw_ref[...] = jnp.zeros_like(dw_ref)
            db_ref[...] = jnp.zeros_like(db_ref)

        gp = g_ref[...]
        prev = jnp.where(i == 0, 0.0, gprev_ref[...])
        bias = b_ref[...]
        taps = [_shift_down(gp, prev, FFN_CONV - 1 - j) for j in range(FFN_CONV)]
        gate = bias + taps[0] * w_ref[0:1, :]
        for j in range(1, FFN_CONV):
            gate = gate + taps[j] * w_ref[j:j + 1, :]
        dgate, dup = dgate_of(gate, u_ref[...], da_ref[...])
        gate_n = _ffn_conv(gnext_ref[...], gp[tt - HALO:, :], w_ref, bias)
        dgate_n, _ = dgate_of(gate_n, unext_ref[...], danext_ref[...])
        dgate_n = jnp.where(i == nt - 1, 0.0, dgate_n)
        db_ref[...] += jnp.sum(dgate, axis=0, keepdims=True)
        dgp = jnp.zeros((tt, tc), F32)
        for j in range(FFN_CONV):
            dw_ref[j:j + 1, :] += jnp.sum(dgate * taps[j], axis=0, keepdims=True)
            dgp = dgp + _shift_up(dgate, dgate_n, FFN_CONV - 1 - j) * w_ref[j:j + 1, :]
        dgu_ref[:, :tc] = dgp.astype(BF16)
        dgu_ref[:, tc:] = dup.astype(BF16)

    tile = lambda half: pl.BlockSpec((tt, tc), lambda j, i, half=half: (i, 2 * j + half))
    hprev = lambda half: pl.BlockSpec((HALO, tc), lambda j, i, half=half: (jnp.maximum(i * hpb - 1, 0), 2 * j + half))
    hnext = lambda half: pl.BlockSpec(
        (HALO, tc), lambda j, i, half=half: (jnp.minimum((i + 1) * hpb, nt * hpb - 1), 2 * j + half))
    return pl.pallas_call(
        body, grid=(N_CHIPS, nt), name="ffn_act_bwd",
        in_specs=[tile(0), hprev(0), hnext(0), tile(1), hnext(1),
                  pl.BlockSpec((tt, tc), lambda j, i: (i, j)),
                  pl.BlockSpec((HALO, tc), lambda j, i: (jnp.minimum((i + 1) * hpb, nt * hpb - 1), j)),
                  pl.BlockSpec((FFN_CONV, tc), lambda j, i: (0, j)),
                  pl.BlockSpec((1, tc), lambda j, i: (0, j))],
        out_specs=[pl.BlockSpec((tt, 2 * tc), lambda j, i: (i, j)),
                   pl.BlockSpec((FFN_CONV, tc), lambda j, i: (0, j)),
                   pl.BlockSpec((1, tc), lambda j, i: (0, j))],
        out_shape=[jax.ShapeDtypeStruct((s, 2 * d_ff), BF16),
                   jax.ShapeDtypeStruct((FFN_CONV, d_ff), F32), jax.ShapeDtypeStruct((1, d_ff), F32)],
        compiler_params=_params(("parallel", "arbitrary")),
    )(gu, gu, gu, gu, gu, dact, dact, conv_w, conv_b)


def _gate_grads(dgate, dproj, tm=512):
    s, n = dgate.shape
    tm = min(tm, s)

    def body(a_ref, dproj_in_ref, o_ref, dproj_ref):
        @pl.when(pl.program_id(0) == 0)
        def _():
            o_ref[...] = jnp.zeros_like(o_ref)
        a = a_ref[...]
        o_ref[...] += jnp.sum(a, axis=0, keepdims=True)
        dproj_ref[...] = a.astype(BF16)

    return pl.pallas_call(
        body, grid=(s // tm,), name="gate_grads",
        in_specs=[pl.BlockSpec((tm, n), lambda i: (i, 0)), ANY],
        out_specs=[pl.BlockSpec((1, n), lambda i: (0, 0)),
                   pl.BlockSpec((tm, n), lambda i: (i, (_QKVO + 2 * D_LRU) // LANES))],
        out_shape=[jax.ShapeDtypeStruct((1, n), F32), jax.ShapeDtypeStruct(dproj.shape, BF16)],
        input_output_aliases={1: 1},
        compiler_params=_params(("arbitrary",)),
    )(dgate, dproj)


def _pick(n, *cands):
    for c in cands:
        if n % c == 0:
            return c
    raise ValueError(f"no tile for {n}")


def _behind(a, token):
    return a if token is None else a + token[0:1, 0:1].astype(a.dtype).reshape((1,) * a.ndim)


class _Gathered:
    def __init__(self, w):
        self.w = w

    def begin(self):
        return None

    def mid(self, grp, after):
        return None

    def end(self, grp, after):
        return self.w

    def reduce_early(self, grads):
        return None

    def reduce_early_mid(self, after):
        return None

    def reduce_late(self, grads):
        return None

    def reduce_late_mid(self, after):
        return None


def _local_step(x, target, w, comm):
    s, d = x.shape
    nc = s // CHUNK
    tm = _pick(s, 1024, 512, 256)
    tn_proj = _pick(_PROJ_PAD, 896)
    gate_col = 4 * D_MLSTM + 2 * D_LRU
    w = dict(w)

    token = comm.begin()
    n1, rstd1 = _rmsnorm_fwd("norm_mix_fwd", x, _behind(w["norm_mix_g"], token))
    comm.mid(0, n1)
    w.update(comm.end(0, None))
    proj = _mm_nn("proj_fwd", n1, w["w_in"], tm, tn_proj, d)
    token = comm.mid(1, proj)
    gates = proj[:, gate_col:gate_col + 2 * HEADS]
    gates_t = gates.reshape(nc, CHUNK, 2 * HEADS).transpose(0, 2, 1)
    bias_row = _behind(jnp.pad(w["b_gate_m"], ((0, 0), (0, LANES - 2 * HEADS))), token)
    bias_col = w["b_gate_m"].reshape(2 * HEADS, 1)
    mix, cprev, nprev, mprev = _mlstm_fwd(proj, gates_t, bias_row, bias_col, w["mlstm_norm_g"])
    mix, hsave = _lru_fwd(proj, mix, w["lru_conv_w"], w["lru_conv_b"], w["lru_wa"], w["lru_wx"],
                          w["lru_ba"], w["lru_bx"], w["lru_lambda"])
    w.update(comm.end(1, hsave))
    token = comm.mid(2, hsave)
    x1 = _mm_nn("out_fwd", mix, w["w_out"], tm, 1024, d, res=x)
    n2, rstd2 = _rmsnorm_fwd("norm_ffn_fwd", x1, _behind(w["norm_ffn_g"], token))
    w.update(comm.end(2, n2))
    token = comm.mid(3, n2)
    gu = _mm_up_fwd("up_fwd", n2, w["w_up"], tm, d)
    act = _ffn_act_fwd(gu, w["ffn_conv_w"], _behind(w["ffn_conv_b"], token))
    w.update(comm.end(3, act))
    d_ff = w["w_down"].shape[0]
    x2 = _mm_nn("down_fwd", act, w["w_down"], min(tm, 512), 1024, d_ff // 2, res=x1)
    loss, dx2, dx2b, g_norm_final = _loss_head("loss_head", x2, w["norm_final_g"], target)

    grads = {"norm_final_g": g_norm_final}
    dact = _mm_nt("down_bwd_x", dx2b, w["w_down"], tm, d_ff // N_CHIPS, d)
    grads["w_down"] = _mm_tn("down_bwd_w", act, dx2b, d_ff // N_CHIPS, 1024, 2048)
    dgu, grads["ffn_conv_w"], grads["ffn_conv_b"] = _ffn_act_bwd(gu, dact, w["ffn_conv_w"], w["ffn_conv_b"])
    dn2 = _mm_up_bwd_x("up_bwd_x", dgu, w["w_up"], tm, 1024)
    grads["w_up"] = _mm_up_bwd_w("up_bwd_w", n2, dgu, 1024, 2048)
    dx1, dx1b, grads["norm_ffn_g"] = _rmsnorm_bwd("norm_ffn_bwd", x1, rstd2, w["norm_ffn_g"], dn2, dx2)
    dmix = _mm_nt("out_bwd_x", dx1b, w["w_out"], tm, 1024, d)
    grads["w_out"] = _mm_tn("out_bwd_w", mix, dx1b, 1024, 1024, 2048)
    token = comm.reduce_early(grads)
    dproj, dgate, grads["mlstm_norm_g"] = _mlstm_bwd(proj, gates_t, _behind(bias_row, token), bias_col,
                                                     w["mlstm_norm_g"], cprev, nprev, mprev, dmix)
    token = comm.reduce_early_mid(dproj)
    (dproj, grads["lru_conv_w"], grads["lru_conv_b"], grads["lru_wa"], grads["lru_wx"],
     grads["lru_ba"], grads["lru_bx"], grads["lru_lambda"]) = _lru_bwd(
        proj, hsave, dmix, dproj, w["lru_conv_w"], _behind(w["lru_conv_b"], token), w["lru_wa"], w["lru_wx"],
        w["lru_ba"], w["lru_bx"], w["lru_lambda"])
    gate_bias_grad, dproj = _gate_grads(dgate, dproj)
    grads["b_gate_m"] = gate_bias_grad[:, :2 * HEADS]
    grads["w_in"] = _mm_tn("proj_bwd_w", n1, dproj, 1024, tn_proj, 2048)
    token = comm.reduce_late(grads)
    dn1 = _mm_nt("proj_bwd_x", dproj, w["w_in"], tm, 512, _PROJ_PAD, after=token)
    token = comm.reduce_late_mid(dn1)
    grad_x, _, grads["norm_mix_g"] = _rmsnorm_bwd("norm_mix_bwd", x, rstd1, _behind(w["norm_mix_g"], token),
                                                  dn1, dx1)
    return loss, grad_x, grads


WEIGHT_NAMES = ("norm_mix_g", "w_in", "b_gate_m", "mlstm_norm_g", "lru_conv_w", "lru_conv_b", "lru_wa", "lru_ba",
                "lru_wx", "lru_bx", "lru_lambda", "w_out", "norm_ffn_g", "w_up", "ffn_conv_w", "ffn_conv_b",
                "w_down", "norm_final_g")
BIG = ("w_in", "w_out", "w_up", "w_down")
SMALL_SHARDED = ("mlstm_norm_g", "lru_conv_w", "ffn_conv_w")
SMALL = tuple(n for n in WEIGHT_NAMES if n not in BIG)
SMALL_REPLICATED = tuple(n for n in SMALL if n not in SMALL_SHARDED)


def _proj_segments():
    segs = [(0, 0, _QKVO), (_QKVO, _QKVO + 2 * D_LRU, _N_GATES)]
    for n in range(LRU_BLOCKS):
        segs.append((_QKVO + _N_GATES + n * LRU_BLOCK_DIM, _QKVO + 2 * n * LRU_BLOCK_DIM, LRU_BLOCK_DIM))
        segs.append((_QKVO + _N_GATES + D_LRU + n * LRU_BLOCK_DIM, _QKVO + (2 * n + 1) * LRU_BLOCK_DIM,
                     LRU_BLOCK_DIM))
    return segs


def _w_in_shards_to_local(shards):
    width = shards.shape[2]
    pieces = []
    for g0, _, n in sorted(_proj_segments(), key=lambda s: s[1]):
        at = g0
        while at < g0 + n:
            j = at // width
            stop = min(g0 + n, (j + 1) * width)
            pieces.append(shards[j][:, at - j * width:stop - j * width])
            at = stop
    pieces.append(jnp.zeros((shards.shape[1], PROJ_GATE_PAD - _N_GATES), shards.dtype))
    return jnp.concatenate(pieces, axis=1)


def _w_in_local_to_shards(w):
    width = _PROJ_COLS // N_CHIPS
    shards = []
    for j in range(N_CHIPS):
        pieces = []
        for g0, l0, n in sorted(_proj_segments()):
            lo, hi = max(g0, j * width), min(g0 + n, (j + 1) * width)
            if lo < hi:
                pieces.append(w[:, l0 + lo - g0:l0 + hi - g0])
        shards.append(jnp.concatenate(pieces, axis=1))
    return jnp.stack(shards)


def _w_in_to_global(w):
    sh = _w_in_local_to_shards(w)
    return jnp.concatenate([sh[j] for j in range(N_CHIPS)], axis=1)


def _size(shp):
    return functools.reduce(lambda a, b: a * b, shp, 1)


def _lane_dense(shp):
    return len(shp) >= 2 and shp[-1] == LANES and _size(shp) % (HALO * LANES) == 0


def _pack_rows(shapes):
    loose = sum(_size(shp) for shp in shapes if not _lane_dense(shp))
    return sum(_size(shp) // LANES for shp in shapes if _lane_dense(shp)) + -(-loose // (HALO * LANES)) * HALO


def _pack(arrs, rows):
    del rows
    parts = [a.reshape(-1, LANES).astype(F32) for a in arrs if _lane_dense(a.shape)]
    loose = [a.reshape(-1).astype(F32) for a in arrs if not _lane_dense(a.shape)]
    if loose:
        flat = jnp.concatenate(loose)
        n = -(-flat.shape[0] // (HALO * LANES)) * HALO * LANES
        parts.append(jnp.pad(flat, (0, n - flat.shape[0])).reshape(-1, LANES))
    return parts[0] if len(parts) == 1 else jnp.concatenate(parts, axis=0)


def _unpack(buf, shapes):
    out, row = {}, 0
    for i, shp in enumerate(shapes):
        if _lane_dense(shp):
            n = _size(shp) // LANES
            out[i] = buf[row:row + n].reshape(shp)
            row += n
    flat, at = buf[row:].reshape(-1), 0
    for i, shp in enumerate(shapes):
        if not _lane_dense(shp):
            out[i] = flat[at:at + _size(shp)].reshape(shp)
            at += _size(shp)
    return [out[i] for i in range(len(shapes))]


def _assemble_weights(g_in, g_out, g_up, g_down, small_sharded, replicated):
    w = dict(replicated)
    w["w_in"] = _w_in_shards_to_local(g_in)
    w["w_out"] = g_out.reshape(-1, g_out.shape[-1])
    w["w_up"] = g_up
    w["w_down"] = g_down.reshape(-1, g_down.shape[-1])
    for name, v in small_sharded.items():
        w[name] = jnp.concatenate([v[j] for j in range(N_CHIPS)], axis=1)
    return w


def _full_weights_from_global(weights):
    shard = lambda a, axis: jnp.stack(jnp.split(a, N_CHIPS, axis=axis))
    rep = {n: weights[n].reshape(1, -1) if weights[n].ndim <= 2 and n != "b_gate_m" else weights[n]
           for n in SMALL_REPLICATED}
    rep["b_gate_m"] = weights["b_gate_m"].reshape(1, -1)
    return _assemble_weights(shard(weights["w_in"], 1).astype(BF16), shard(weights["w_out"], 0).astype(BF16),
                             shard(weights["w_up"], 1).astype(BF16), shard(weights["w_down"], 0).astype(BF16),
                             {n: shard(weights[n], 1) for n in SMALL_SHARDED}, rep)


def _grads_to_global(grads):
    g = dict(grads)
    g["w_in"] = _w_in_to_global(grads["w_in"])
    g["w_up"] = jnp.concatenate([grads["w_up"][j] for j in range(N_CHIPS)], axis=1)
    return g


def _place():
    x, y, c = lax.axis_index("x"), lax.axis_index("y"), lax.axis_index("c")
    chips = [(1 - x, y), (x, 1 - y), (1 - x, 1 - y)]
    return x, y, c, 2 * x + y, chips


def _half_rows(n_rows, which):
    half = n_rows // 2
    return pl.ds(pl.multiple_of(which * half, 16), half)


def _rcopy(src, dst, send_sem, recv_sem, to):
    return pltpu.make_async_remote_copy(src_ref=src, dst_ref=dst, send_sem=send_sem, recv_sem=recv_sem,
                                        device_id=to, device_id_type=MESH)


HBM_SPEC = pl.BlockSpec(memory_space=pltpu.HBM)
SEM_SPEC = pl.BlockSpec(memory_space=pltpu.SEMAPHORE)
TOKEN_SHAPE = (8, LANES)


def _split_call(name, bufs, sems_in, sems_out_shapes, body_fn, after=None):
    nb, ni, no = len(bufs), len(sems_in), len(sems_out_shapes)
    after = [] if after is None else list(after) if isinstance(after, (list, tuple)) else [after]

    def body(*refs):
        buf_refs = refs[:nb]
        sem_in_refs = refs[nb:nb + ni]
        outs = refs[nb + ni + len(after):]
        sem_out_refs = outs[:no]
        token_ref = outs[no + nb]
        body_fn(buf_refs, sem_in_refs, sem_out_refs)
        token_ref[...] = jnp.zeros_like(token_ref)

    out_shape = ([pltpu.SemaphoreType.DMA(shp) for shp in sems_out_shapes]
                 + [pltpu.HBM(b.shape, b.dtype) for b in bufs] + [jax.ShapeDtypeStruct(TOKEN_SHAPE, F32)])
    res = pl.pallas_call(
        body, name=name, out_shape=out_shape,
        in_specs=[HBM_SPEC] * nb + [SEM_SPEC] * ni + [ANY] * len(after),
        out_specs=[SEM_SPEC] * no + [HBM_SPEC] * nb + [pl.BlockSpec(memory_space=pltpu.VMEM)],
        input_output_aliases={i: no + i for i in range(nb)},
        compiler_params=pltpu.CompilerParams(has_side_effects=pltpu.SideEffectType.DATAFLOW_SIDE_EFFECTING),
    )(*[pltpu.with_memory_space_constraint(b, pltpu.HBM) for b in bufs], *sems_in, *after)
    return list(res[:no]), list(res[no:no + nb]), res[no + nb]


def _place_own_shard(name, idx, shard, after=None):
    rows, cols = shard.shape
    tr = _row_tile(rows)

    def body(idx_ref, s_ref, *rest):
        rest[-1][...] = s_ref[...].astype(BF16)

    return pl.pallas_call(
        body, name=name, out_shape=jax.ShapeDtypeStruct((N_CHIPS, rows, cols), BF16),
        grid_spec=pltpu.PrefetchScalarGridSpec(
            num_scalar_prefetch=1, grid=(rows // tr,),
            in_specs=[pl.BlockSpec((tr, cols), lambda i, s: (i, 0))] + ([] if after is None else [ANY]),
            out_specs=pl.BlockSpec((None, tr, cols), lambda i, s: (s[1], i, 0))),
        compiler_params=_params(("parallel",)),
    )(idx, shard, *(() if after is None else (after,)))


GATHER_GROUPS = ((0, 4), (1,), (2,), (3,))


def _gather_start(name, lands, groups, after=None):
    members = [w for g in groups for w in GATHER_GROUPS[g]]

    def starts(bufs, _, sems):
        x, y, c, me, chips = _place()
        for gi, g in enumerate(groups):
            for pos, w in enumerate(GATHER_GROUPS[g]):
                buf = bufs[members.index(w)]
                part = buf.at[me] if w == 4 else buf.at[me, _half_rows(buf.shape[1], c)]
                for k, chip in enumerate(chips):
                    _rcopy(part, part, sems[2 * gi].at[3 * pos + k], sems[2 * gi + 1].at[3 * pos + k],
                           (*chip, c)).start()

    shapes = []
    for g in groups:
        shapes += [(3 * len(GATHER_GROUPS[g]),)] * 2
    sems, bufs, token = _split_call(name, [lands[w] for w in members], [], shapes, starts, after=after)
    return ({g: (sems[2 * gi], sems[2 * gi + 1]) for gi, g in enumerate(groups)},
            dict(zip(members, bufs)), token)


def _gather_mid(grp, lands, sems, after):
    members = GATHER_GROUPS[grp]
    big = [w for w in members if w != 4]

    def mid(bufs, sems_in, sems_out):
        x, y, c, me, chips = _place()
        send_sems, recv_sems = sems_in
        for pos, w in enumerate(members):
            for k, chip in enumerate(chips):
                cid = 2 * chip[0] + chip[1]
                buf = bufs[pos]
                mine = buf.at[me] if w == 4 else buf.at[me, _half_rows(buf.shape[1], c)]
                theirs = buf.at[cid] if w == 4 else buf.at[cid, _half_rows(buf.shape[1], c)]
                arrival = _rcopy(mine, theirs, send_sems.at[3 * pos + k], recv_sems.at[3 * pos + k], (*chip, c))
                arrival.wait_recv()
                arrival.wait_send()
                if w != 4:
                    _rcopy(theirs, theirs, sems_out[0].at[3 * big.index(w) + k],
                           sems_out[1].at[3 * big.index(w) + k], (x, y, 1 - c)).start()

    new_sems, bufs, token = _split_call(f"gather_mid_{grp}", [lands[w] for w in members], list(sems),
                                        [(3 * len(big),), (3 * len(big),)], mid, after=after)
    return new_sems, bufs, token


def _gather_end(grp, bufs, sems, after):
    members = GATHER_GROUPS[grp]
    big = [w for w in members if w != 4]

    def end(refs, sems_in, _):
        x, y, c, me, chips = _place()
        send_sems, recv_sems = sems_in
        for pos, w in enumerate(members):
            if w == 4:
                continue
            for k, chip in enumerate(chips):
                cid = 2 * chip[0] + chip[1]
                buf = refs[pos]
                sent = buf.at[cid, _half_rows(buf.shape[1], c)]
                landed = buf.at[cid, _half_rows(buf.shape[1], 1 - c)]
                fwd = _rcopy(sent, landed, send_sems.at[3 * big.index(w) + k], recv_sems.at[3 * big.index(w) + k],
                             (x, y, 1 - c))
                fwd.wait_recv()
                fwd.wait_send()

    _, bufs, token = _split_call(f"gather_end_{grp}", bufs, list(sems), [], end, after=after)
    return bufs, token


def _pair_start(name, grads, extra=None):
    n = len(grads)
    bufs = list(grads) + [lax.empty((g.shape[0], g.shape[1] // 2, g.shape[2]), g.dtype) for g in grads]
    if extra is not None:
        bufs += [extra, lax.empty(extra.shape, extra.dtype)]

    def starts(refs, _, sems):
        x, y, c, _, _ = _place()
        for w in range(n):
            other = _half_rows(refs[w].shape[1], 1 - c)
            _rcopy(refs[w].at[:, other], refs[n + w], sems[0].at[w], sems[1].at[w], (x, y, 1 - c)).start()
        if extra is not None:
            _rcopy(refs[2 * n], refs[2 * n + 1], sems[0].at[n], sems[1].at[n], (x, y, 1 - c)).start()

    count = n + (extra is not None)
    return _split_call(name, bufs, [], [(count,), (count,)], starts)


def _pair_wait(name, n, bufs, sems, after):
    has_extra = len(bufs) > 2 * n

    def waits(refs, sems_in, _):
        x, y, c, _, _ = _place()
        for w in range(n):
            other = _half_rows(refs[w].shape[1], 1 - c)
            cp = _rcopy(refs[w].at[:, other], refs[n + w], sems_in[0].at[w], sems_in[1].at[w], (x, y, 1 - c))
            cp.wait_recv()
            cp.wait_send()
        if has_extra:
            cp = _rcopy(refs[2 * n], refs[2 * n + 1], sems_in[0].at[n], sems_in[1].at[n], (x, y, 1 - c))
            cp.wait_recv()
            cp.wait_send()

    _, bufs, token = _split_call(name, bufs, list(sems), [], waits, after=after)
    return bufs, token


def _chip_start(name, partials, small=None):
    n = len(partials)
    bufs = list(partials) + [lax.empty(p.shape, p.dtype) for p in partials] + ([] if small is None else [small])

    def starts(refs, _, sems):
        _, _, c, me, chips = _place()
        for w in range(n):
            for k, chip in enumerate(chips):
                cid = 2 * chip[0] + chip[1]
                _rcopy(refs[w].at[cid], refs[n + w].at[me], sems[0].at[3 * w + k], sems[1].at[3 * w + k],
                       (*chip, c)).start()
        if small is not None:
            for k, chip in enumerate(chips):
                _rcopy(refs[2 * n].at[me], refs[2 * n].at[me], sems[0].at[3 * n + k], sems[1].at[3 * n + k],
                       (*chip, c)).start()

    count = 3 * (n + (small is not None))
    return _split_call(name, bufs, [], [(count,), (count,)], starts)


def _chip_wait(name, n, bufs, sems, after):
    has_small = len(bufs) > 2 * n

    def waits(refs, sems_in, _):
        _, _, c, me, chips = _place()
        for w in range(n):
            for k, chip in enumerate(chips):
                cid = 2 * chip[0] + chip[1]
                cp = _rcopy(refs[w].at[cid], refs[n + w].at[cid], sems_in[0].at[3 * w + k],
                            sems_in[1].at[3 * w + k], (*chip, c))
                cp.wait_recv()
                cp.wait_send()
        if has_small:
            for k, chip in enumerate(chips):
                cid = 2 * chip[0] + chip[1]
                cp = _rcopy(refs[2 * n].at[me], refs[2 * n].at[cid], sems_in[0].at[3 * n + k],
                            sems_in[1].at[3 * n + k], (*chip, c))
                cp.wait_recv()
                cp.wait_send()

    _, bufs, token = _split_call(name, bufs, list(sems), [], waits, after=after)
    return bufs, token


def _small_pair_sum(idx, own, recv):
    rows = own.shape[0]

    def body(idx_ref, a_ref, b_ref, o_ref):
        o_ref[...] = a_ref[...] + b_ref[...]

    blk = pl.BlockSpec((rows, LANES), lambda i, s: (0, 0))
    return pl.pallas_call(
        body, name="small_pair_sum", out_shape=jax.ShapeDtypeStruct((N_CHIPS, rows, LANES), F32),
        grid_spec=pltpu.PrefetchScalarGridSpec(
            num_scalar_prefetch=1, grid=(1,), in_specs=[blk, blk],
            out_specs=pl.BlockSpec((None, rows, LANES), lambda i, s: (s[1], 0, 0))),
        compiler_params=_params(("arbitrary",)),
    )(idx, own, recv)


def _pair_share(name, shards, late=None):
    nb = len(shards)
    nl = 0 if late is None else 1

    def body(*refs):
        srcs = refs[:nb]
        dsts = refs[nb + nl:2 * nb + nl]
        send_sems, recv_sems = refs[2 * nb + 2 * nl:2 * nb + 2 * nl + 2]
        x, y, c, _, _ = _place()
        sibling = (x, y, 1 - c)
        sends = []
        for w in range(nb):
            mine = _half_rows(dsts[w].shape[0], c)
            sends.append(_rcopy(srcs[w].at[mine], dsts[w].at[mine], send_sems.at[w], recv_sems.at[w], sibling))
        if nl:
            late_ref, late_out = refs[nb], refs[2 * nb + 1]
            late_send, late_recv, local_sem = refs[2 * nb + 4:]
            my_id = 4 * x + 2 * y + c
            peer = lambda r: (1 - x if r & 4 else x, 1 - y if r & 2 else y, 1 - c if r & 1 else c)
            local = pltpu.make_async_copy(late_ref, late_out.at[my_id], local_sem)
            local.start()
            for r in range(1, N_DEV):
                sends.append(_rcopy(late_ref, late_out.at[my_id], late_send.at[r - 1], late_recv.at[r - 1],
                                    peer(r)))
        for cp in sends:
            cp.start()
        for w in range(nb):
            other = _half_rows(dsts[w].shape[0], 1 - c)
            _rcopy(srcs[w].at[other], dsts[w].at[other], send_sems.at[w], recv_sems.at[w], sibling).wait_recv()
        if nl:
            for r in range(1, N_DEV):
                frm = peer(r)
                _rcopy(late_ref, late_out.at[4 * frm[0] + 2 * frm[1] + frm[2]], late_send.at[r - 1],
                       late_recv.at[r - 1], frm).wait_recv()
        for cp in sends:
            cp.wait_send()
        if nl:
            local.wait()

    out_shape = [jax.ShapeDtypeStruct(h.shape, h.dtype) for h in shards]
    scratch = [pltpu.SemaphoreType.DMA((nb,)), pltpu.SemaphoreType.DMA((nb,))]
    if nl:
        out_shape.append(jax.ShapeDtypeStruct((N_DEV,) + late.shape, late.dtype))
        scratch += [pltpu.SemaphoreType.DMA((N_DEV - 1,)), pltpu.SemaphoreType.DMA((N_DEV - 1,)),
                    pltpu.SemaphoreType.DMA(())]
    return pl.pallas_call(
        body, name=name, out_shape=out_shape,
        in_specs=[ANY] * (nb + nl), out_specs=[ANY] * (nb + nl), scratch_shapes=scratch,
        input_output_aliases={w: w for w in range(nb)},
    )(*shards, *(() if late is None else (late,)))


def _row_tile(rows):
    return _pick(rows, 128, 64, 16, 8)


def _pair_sum(name, idx, grad, recv):
    n, half, cols = recv.shape
    tr = _row_tile(half)
    nrb = half // tr

    def body(idx_ref, g_ref, r_ref, o_ref):
        o_ref[...] = (g_ref[...] + r_ref[...]).astype(BF16)

    return pl.pallas_call(
        body, name=name, out_shape=jax.ShapeDtypeStruct(recv.shape, BF16),
        grid_spec=pltpu.PrefetchScalarGridSpec(
            num_scalar_prefetch=1, grid=(n - 1, nrb),
            in_specs=[pl.BlockSpec((None, tr, cols), lambda j, i, s: (s[2 + j], s[0] * nrb + i, 0)),
                      pl.BlockSpec((None, tr, cols), lambda j, i, s: (s[2 + j], i, 0))],
            out_specs=pl.BlockSpec((None, tr, cols), lambda j, i, s: (s[2 + j], i, 0))),
        compiler_params=_params(("parallel", "parallel")),
    )(idx, grad, recv)


def _final_sum(name, idx, grad, recv, chip_sums):
    _, half, cols = recv.shape
    tr = _row_tile(half)
    nrb = half // tr

    def body(idx_ref, g_ref, r_ref, p1_ref, p2_ref, p3_ref, o_ref):
        acc = g_ref[...] + r_ref[...]
        for p_ref in (p1_ref, p2_ref, p3_ref):
            acc = acc + p_ref[...].astype(F32)
        o_ref[...] = acc

    slot = lambda which: pl.BlockSpec((None, tr, cols), lambda i, s, which=which: (s[which], i, 0))
    return pl.pallas_call(
        body, name=name, out_shape=jax.ShapeDtypeStruct((2 * half, cols), F32),
        grid_spec=pltpu.PrefetchScalarGridSpec(
            num_scalar_prefetch=1, grid=(nrb,),
            in_specs=[pl.BlockSpec((None, tr, cols), lambda i, s: (s[1], s[0] * nrb + i, 0)),
                      slot(1), slot(2), slot(3), slot(4)],
            out_specs=pl.BlockSpec((tr, cols), lambda i, s: (s[0] * nrb + i, 0))),
        compiler_params=_params(("parallel",)),
    )(idx, grad, recv, chip_sums, chip_sums, chip_sums)


def _pair_sum_all(name, idx, grad, recv):
    _, half, cols = recv.shape
    tr = _row_tile(half)
    nrb = half // tr

    def body(idx_ref, g_ref, r_ref, o_ref):
        o_ref[...] = (g_ref[...] + r_ref[...]).astype(BF16)

    return pl.pallas_call(
        body, name=name, out_shape=jax.ShapeDtypeStruct((half, cols), BF16),
        grid_spec=pltpu.PrefetchScalarGridSpec(
            num_scalar_prefetch=1, grid=(nrb,),
            in_specs=[pl.BlockSpec((None, tr, cols), lambda i, s: (0, s[0] * nrb + i, 0)),
                      pl.BlockSpec((None, tr, cols), lambda i, s: (0, i, 0))],
            out_specs=pl.BlockSpec((tr, cols), lambda i, s: (i, 0))),
        compiler_params=_params(("parallel",)),
    )(idx, grad, recv)


def _final_sum_bf16(name, idx, partial, chip_sums):
    _, half, cols = partial.shape
    tr = _row_tile(half)
    nrb = half // tr

    def body(idx_ref, p0_ref, p1_ref, p2_ref, p3_ref, o_ref):
        acc = p0_ref[...].astype(F32)
        for p_ref in (p1_ref, p2_ref, p3_ref):
            acc = acc + p_ref[...].astype(F32)
        o_ref[...] = acc

    slot = lambda which: pl.BlockSpec((None, tr, cols), lambda i, s, which=which: (s[which], i, 0))
    return pl.pallas_call(
        body, name=name, out_shape=jax.ShapeDtypeStruct((2 * half, cols), F32),
        grid_spec=pltpu.PrefetchScalarGridSpec(
            num_scalar_prefetch=1, grid=(nrb,),
            in_specs=[slot(1), slot(2), slot(3), slot(4)],
            out_specs=pl.BlockSpec((tr, cols), lambda i, s: (s[0] * nrb + i, 0))),
        compiler_params=_params(("parallel",)),
    )(idx, partial, chip_sums, chip_sums, chip_sums)


def _small_sum(name, packs):
    n, rows, _ = packs.shape

    def body(p_ref, o_ref):
        acc = p_ref[0]
        for k in range(1, n):
            acc = acc + p_ref[k]
        o_ref[...] = acc

    return pl.pallas_call(
        body, name=name, out_shape=jax.ShapeDtypeStruct((rows, LANES), F32),
        in_specs=[pl.BlockSpec(memory_space=pltpu.VMEM)], out_specs=pl.BlockSpec(memory_space=pltpu.VMEM),
        compiler_params=pltpu.CompilerParams(vmem_limit_bytes=VMEM_LIMIT),
    )(packs)


def _adamw_math(w, g, m, v):
    m_new = ADAM_B1 * m + (1.0 - ADAM_B1) * g
    v_new = ADAM_B2 * v + (1.0 - ADAM_B2) * (g * g)
    m_hat = m_new / (1.0 - ADAM_B1 ** ADAM_STEP)
    v_hat = v_new / (1.0 - ADAM_B2 ** ADAM_STEP)
    return -ADAM_LR * (m_hat / (jnp.sqrt(v_hat) + ADAM_EPS) + ADAM_WD * w), m_new, v_new


def _adamw_many(name, ws, gs, ms, vs):
    n = len(ws)

    def body(*refs):
        for i in range(n):
            d, m_new, v_new = _adamw_math(refs[i][...], refs[n + i][...], refs[2 * n + i][...],
                                          refs[3 * n + i][...])
            refs[4 * n + i][...] = d
            refs[5 * n + i][...] = m_new
            refs[6 * n + i][...] = v_new

    vmem = pl.BlockSpec(memory_space=pltpu.VMEM)
    res = pl.pallas_call(
        body, name=name, in_specs=[vmem] * (4 * n), out_specs=[vmem] * (3 * n),
        out_shape=[jax.ShapeDtypeStruct(w.shape, F32) for w in ws] * 3,
        compiler_params=pltpu.CompilerParams(vmem_limit_bytes=VMEM_LIMIT),
    )(*ws, *gs, *ms, *vs)
    return res[:n], res[n:2 * n], res[2 * n:]


def _adamw(name, w, g, m, v):
    rows, cols = w.shape
    tr = rows if rows * cols * 4 <= (2 << 20) else _row_tile(rows)

    def body(w_ref, g_ref, m_ref, v_ref, g_out_ref, d_ref, nm_ref, nv_ref):
        gv = g_ref[...]
        g_out_ref[...] = gv
        d_ref[...], nm_ref[...], nv_ref[...] = _adamw_math(w_ref[...], gv, m_ref[...], v_ref[...])

    blk = pl.BlockSpec((tr, cols), lambda i: (i, 0))
    sds = jax.ShapeDtypeStruct((rows, cols), F32)
    return pl.pallas_call(
        body, name=name, grid=(rows // tr,), in_specs=[blk] * 4, out_specs=[blk] * 4, out_shape=[sds] * 4,
        compiler_params=_params(("parallel",)),
    )(w, g, m, v)


def _train_step(x, target, W, M, V):
    xi, yi, ci = lax.axis_index("x"), lax.axis_index("y"), lax.axis_index("c")
    me = 2 * xi + yi
    big = {n: W[n][0] for n in BIG}
    big_m = {n: M[n][0] for n in BIG}
    big_v = {n: V[n][0] for n in BIG}

    others = [jnp.where(jnp.int32(i) >= me, i + 1, i) for i in range(N_CHIPS - 1)]
    idx = jnp.stack([ci, me] + others).astype(jnp.int32)

    sharded_shapes = [W[n].shape[1:] for n in SMALL_SHARDED]
    small_pack = _pack([W[n][0] for n in SMALL_SHARDED], _pack_rows(sharded_shapes))
    small_land = lax.dynamic_update_slice(jnp.zeros((N_CHIPS,) + small_pack.shape, F32), small_pack[None],
                                          (me, 0, 0))
    replicated = {n: (W[n].reshape(1, -1) if W[n].ndim <= 2 else W[n][0]) for n in SMALL_REPLICATED}

    early = ("w_out", "w_up", "w_down")
    small_late = "norm_mix_g"
    small_early = tuple(n for n in SMALL if n != small_late)
    global_shape = lambda n: ((W[n].shape[1], W[n].shape[2] * N_CHIPS) if n in SMALL_SHARDED else
                              tuple(W[n].shape) if W[n].ndim == 1 else tuple(W[n].shape[1:]))
    small_shapes = [global_shape(n) for n in small_early]

    def shard_major(n, g):
        if n == "w_in":
            return _w_in_local_to_shards(g)
        return g if g.ndim == 3 else g.reshape((N_CHIPS, -1) + g.shape[1:])

    class _SplitComm:
        def reduce_early(self, grads):
            self.e_sems, self.e_bufs, token = _pair_start("pair_start_early",
                                                          [shard_major(n, grads[n]) for n in early])
            return token

        def reduce_early_mid(self, after):
            n = len(early)
            bufs, _ = _pair_wait("pair_wait_early", n, self.e_bufs, self.e_sems, after)
            self.e_grads, self.e_recv = bufs[:n], bufs[n:2 * n]
            partial = [_pair_sum(f"pair_sum_{nm}", idx, g, r) for nm, g, r in zip(early, self.e_grads, self.e_recv)]
            self.e_sems, self.e_bufs, token = _chip_start("chip_start_early", partial)
            return token

        def reduce_late(self, grads):
            pack = _pack([grads[n] for n in small_early], _pack_rows(small_shapes))
            self.l_sems, self.l_bufs, token = _pair_start("pair_start_late", [grads["w_in"][None]], extra=pack)
            return token

        def reduce_late_mid(self, after):
            bufs, _ = _pair_wait("pair_wait_late", 1, self.l_bufs, self.l_sems, after)
            partial = _w_in_local_to_shards(_pair_sum_all("pair_sum_w_in", idx, bufs[0], bufs[1]))
            self.l_sems, self.l_bufs, token = _chip_start("chip_start_late", [partial],
                                                          small=_small_pair_sum(idx, bufs[2], bufs[3]))
            return token

        def finish_early(self, after):
            n = len(early)
            bufs, _ = _chip_wait("chip_wait_early", n, self.e_bufs, self.e_sems, after)
            halves = [_final_sum(f"final_sum_{nm}", idx, g, r, p)
                      for nm, g, r, p in zip(early, self.e_grads, self.e_recv, bufs[n:2 * n])]
            return dict(zip(early, _pair_share("pair_share_early", halves)))

        def finish_late(self, after, late):
            bufs, _ = _chip_wait("chip_wait_late", 1, self.l_bufs, self.l_sems, after)
            half = _final_sum_bf16("final_sum_w_in", idx, bufs[0], bufs[1])
            small = dict(zip(small_early, _unpack(_small_sum("small_sum", bufs[2]), small_shapes)))
            whole, late_all = _pair_share("pair_share_late", [half], late)
            return whole, small, _small_sum("late_sum", late_all)

        def begin(self):
            first = {0: _place_own_shard("place_w_in", idx, big["w_in"]), 4: small_land}
            self.sems, self.lands, token = _gather_start("gather_start_0", first, (0,))
            rest = {i: _place_own_shard(f"place_{BIG[i]}", idx, big[BIG[i]], after=token) for i in (1, 2, 3)}
            sems, lands, token = _gather_start("gather_start_1", rest, (1, 2, 3), after=token)
            self.sems.update(sems)
            self.lands.update(lands)
            return token

        def mid(self, grp, after):
            if grp == 0:
                after = [after, big_m["w_in"], big_v["w_in"]]
            self.pending = _gather_mid(grp, self.lands, self.sems[grp], after)
            return self.pending[2]

        def end(self, grp, after):
            sems, bufs, _ = self.pending
            bufs, _ = _gather_end(grp, bufs, sems, after)
            if grp == 0:
                per_chip = [_unpack(bufs[1][j], sharded_shapes) for j in range(N_CHIPS)]
                out = {n: jnp.concatenate([per_chip[j][i] for j in range(N_CHIPS)], axis=1)
                       for i, n in enumerate(SMALL_SHARDED)}
                out["w_in"] = _w_in_shards_to_local(bufs[0])
                return out
            if grp == 2:
                return {"w_up": bufs[0]}
            return {("w_out" if grp == 1 else "w_down"): bufs[0].reshape(-1, bufs[0].shape[-1])}

    comm = _SplitComm()
    loss, grad_x, grads = _local_step(x[0], target[0], replicated, comm)
    loss = lax.psum(loss[0, 0], ("x", "y", "c"))
    out_g, out_d, out_m, out_v = {}, {}, {}, {}

    def update_big(n, grad):
        g, d, nm, nv = _adamw(f"adamw_{n}", big[n], grad, big_m[n], big_v[n])
        out_g[n], out_d[n], out_m[n], out_v[n] = g[None], d[None], nm[None], nv[None]
        return d

    early_grads = comm.finish_early(grad_x)
    done = [update_big(n, early_grads[n]) for n in early]
    late = _pack([grads[small_late]], _pack_rows([global_shape(small_late)]))
    w_in_grad, small_grads, late_sum = comm.finish_late(done, late)
    update_big("w_in", w_in_grad)
    small_grads[small_late] = _unpack(late_sum, [global_shape(small_late)])[0]
    for n in SMALL_SHARDED:
        width = W[n].shape[2]
        small_grads[n] = lax.dynamic_slice_in_dim(small_grads[n], me * width, width, axis=1)

    for n in SMALL:
        out_g[n] = small_grads[n].reshape(W[n].shape)
    two_d = lambda a: a.reshape(1, -1) if a.ndim == 1 else a
    results = _adamw_many("adamw_small", *[[two_d(src[n]) for n in SMALL] for src in (W, out_g, M, V)])
    for dst, arrs in zip((out_d, out_m, out_v), results):
        dst.update({n: a.reshape(W[n].shape) for n, a in zip(SMALL, arrs)})
    return (loss, grad_x[None], *[out_g[n] for n in WEIGHT_NAMES], *[out_d[n] for n in WEIGHT_NAMES],
            *[out_m[n] for n in WEIGHT_NAMES], *[out_v[n] for n in WEIGHT_NAMES])


def kernel(x, norm_mix_g, w_in, b_gate_m, mlstm_norm_g, lru_conv_w, lru_conv_b, lru_wa, lru_ba, lru_wx, lru_bx, lru_lambda, w_out, norm_ffn_g, w_up, ffn_conv_w, ffn_conv_b, w_down, norm_final_g, loss_target, m_norm_mix_g, m_w_in, m_b_gate_m, m_mlstm_norm_g, m_lru_conv_w, m_lru_conv_b, m_lru_wa, m_lru_ba, m_lru_wx, m_lru_bx, m_lru_lambda, m_w_out, m_norm_ffn_g, m_w_up, m_ffn_conv_w, m_ffn_conv_b, m_w_down, m_norm_final_g, v_norm_mix_g, v_w_in, v_b_gate_m, v_mlstm_norm_g, v_lru_conv_w, v_lru_conv_b, v_lru_wa, v_lru_ba, v_lru_wx, v_lru_bx, v_lru_lambda, v_w_out, v_norm_ffn_g, v_w_up, v_ffn_conv_w, v_ffn_conv_b, v_w_down, v_norm_final_g):
    W = dict(zip(WEIGHT_NAMES, (norm_mix_g, w_in, b_gate_m, mlstm_norm_g, lru_conv_w, lru_conv_b, lru_wa, lru_ba,
                                lru_wx, lru_bx, lru_lambda, w_out, norm_ffn_g, w_up, ffn_conv_w, ffn_conv_b,
                                w_down, norm_final_g)))
    M = dict(zip(WEIGHT_NAMES, (m_norm_mix_g, m_w_in, m_b_gate_m, m_mlstm_norm_g, m_lru_conv_w, m_lru_conv_b,
                                m_lru_wa, m_lru_ba, m_lru_wx, m_lru_bx, m_lru_lambda, m_w_out, m_norm_ffn_g,
                                m_w_up, m_ffn_conv_w, m_ffn_conv_b, m_w_down, m_norm_final_g)))
    V = dict(zip(WEIGHT_NAMES, (v_norm_mix_g, v_w_in, v_b_gate_m, v_mlstm_norm_g, v_lru_conv_w, v_lru_conv_b,
                                v_lru_wa, v_lru_ba, v_lru_wx, v_lru_bx, v_lru_lambda, v_w_out, v_norm_ffn_g,
                                v_w_up, v_ffn_conv_w, v_ffn_conv_b, v_w_down, v_norm_final_g)))
    return _train_step(x, loss_target, W, M, V)
```

```python
import functools

import jax
import jax.numpy as jnp
from jax import lax
from jax.experimental import pallas as pl
from jax.experimental.pallas import tpu as pltpu

F32 = jnp.float32
BF16 = jnp.bfloat16
MESH = pl.DeviceIdType.MESH

EPS = 1e-6
CHUNK = 512
HEADS = 4
HEAD_DIM = 256
D_MLSTM = HEADS * HEAD_DIM
LRU_BLOCKS = 8
LRU_BLOCK_DIM = 128
D_LRU = LRU_BLOCKS * LRU_BLOCK_DIM
LRU_C = 8.0
LRU_CONV = 4
FFN_CONV = 3
ADAM_LR = 0.001
ADAM_B1 = 0.9
ADAM_B2 = 0.999
ADAM_EPS = 1e-08
ADAM_WD = 0.01
ADAM_STEP = 10

N_CHIPS = 4
N_DEV = 8
LANES = 128
HALO = 8
PROJ_GATE_PAD = LANES
_QKVO = 4 * D_MLSTM
_N_GATES = 2 * HEADS
_PROJ_COLS = _QKVO + _N_GATES + 2 * D_LRU
_PROJ_PAD = _QKVO + 2 * D_LRU + PROJ_GATE_PAD
VMEM_LIMIT = 48 * 1024 * 1024
ANY = pl.BlockSpec(memory_space=pl.ANY)


def _params(sem, vmem=VMEM_LIMIT):
    return pltpu.CompilerParams(dimension_semantics=sem, vmem_limit_bytes=vmem)


def _matmul(name, a, b, grid, a_spec, b_spec, o_spec, out_sds, contract, res=None, res_spec=None, after=None):
    nk = grid[2]
    acc_shape = tuple(d for d in o_spec.block_shape if d is not None)

    def body(*refs):
        refs = list(refs)
        a_ref, b_ref = refs[:2]
        r_ref = refs[2] if res is not None else None
        o_ref = refs[-1] if nk == 1 else refs[-2]
        acc_ref = None if nk == 1 else refs[-1]
        k = pl.program_id(2)

        def part():
            return lax.dot_general(a_ref[...], b_ref[...], (contract, ((), ())), preferred_element_type=F32)

        def finish(r):
            if r_ref is not None:
                r = r_ref[...] + r
            o_ref[...] = r.astype(o_ref.dtype)

        if nk == 1:
            finish(part())
            return

        @pl.when(k == 0)
        def _():
            acc_ref[...] = part()

        @pl.when(jnp.logical_and(k > 0, k < nk - 1))
        def _():
            acc_ref[...] += part()

        @pl.when(k == nk - 1)
        def _():
            finish(acc_ref[...] + part())

    in_specs = [a_spec, b_spec] + ([] if res is None else [res_spec]) + ([] if after is None else [ANY])
    args = (a, b) + (() if res is None else (res,)) + (() if after is None else (after,))
    if after is not None:
        inner = body
        body = lambda *refs: inner(*refs[:len(in_specs) - 1], *refs[len(in_specs):])
    return pl.pallas_call(
        body, out_shape=out_sds, grid=grid, in_specs=in_specs, out_specs=o_spec,
        scratch_shapes=[] if nk == 1 else [pltpu.VMEM(acc_shape, F32)], name=name,
        compiler_params=_params(("parallel", "parallel", "arbitrary")),
    )(*args)


NN = ((1,), (0,))
NT = ((1,), (1,))
TN = ((0,), (0,))


def _mm_nn(name, a, b, tm, tn, tk, out_dtype=F32, res=None):
    m, k = a.shape
    n = b.shape[1]
    return _matmul(name, a, b, (m // tm, n // tn, k // tk),
                   pl.BlockSpec((tm, tk), lambda i, j, kk: (i, kk)),
                   pl.BlockSpec((tk, tn), lambda i, j, kk: (kk, j)),
                   pl.BlockSpec((tm, tn), lambda i, j, kk: (i, j)),
                   jax.ShapeDtypeStruct((m, n), out_dtype), NN,
                   res=res, res_spec=pl.BlockSpec((tm, tn), lambda i, j, kk: (i, j)))


def _mm_nt(name, a, b, tm, tn, tk, out_dtype=F32, res=None, after=None):
    m, k = a.shape
    n = b.shape[0]
    return _matmul(name, a, b, (m // tm, n // tn, k // tk),
                   pl.BlockSpec((tm, tk), lambda i, j, kk: (i, kk)),
                   pl.BlockSpec((tn, tk), lambda i, j, kk: (j, kk)),
                   pl.BlockSpec((tm, tn), lambda i, j, kk: (i, j)),
                   jax.ShapeDtypeStruct((m, n), out_dtype), NT,
                   res=res, res_spec=pl.BlockSpec((tm, tn), lambda i, j, kk: (i, j)), after=after)


def _mm_tn(name, a, b, tm, tn, tk, out_dtype=F32):
    k, m = a.shape
    n = b.shape[1]
    tk = min(tk, k)
    return _matmul(name, a, b, (m // tm, n // tn, k // tk),
                   pl.BlockSpec((tk, tm), lambda i, j, kk: (kk, i)),
                   pl.BlockSpec((tk, tn), lambda i, j, kk: (kk, j)),
                   pl.BlockSpec((tm, tn), lambda i, j, kk: (i, j)),
                   jax.ShapeDtypeStruct((m, n), out_dtype), TN)


def _up_shard(n):
    return 2 * (n % 2) + (n // 2) // 2, (n // 2) % 2


def _mm_up_fwd(name, a, wg_up, tm, tk):
    m, k = a.shape
    _, _, cols = wg_up.shape
    tn = cols // 2
    return _matmul(name, a, wg_up, (m // tm, 2 * N_CHIPS, k // tk),
                   pl.BlockSpec((tm, tk), lambda i, j, kk: (i, kk)),
                   pl.BlockSpec((None, tk, tn), lambda i, j, kk: (_up_shard(j)[0], kk, _up_shard(j)[1])),
                   pl.BlockSpec((tm, tn), lambda i, j, kk: (i, j)),
                   jax.ShapeDtypeStruct((m, 2 * N_CHIPS * tn), F32), NN)


def _mm_up_bwd_x(name, dgu, wg_up, tm, tn):
    m, _ = dgu.shape
    _, d, cols = wg_up.shape
    tk = cols // 2
    nk = N_CHIPS

    def body(a_ref, bg_ref, bu_ref, o_ref, acc_ref):
        k = pl.program_id(2)

        def part():
            dims = (NT, ((), ()))
            return (lax.dot_general(a_ref[:, :tk], bg_ref[...], dims, preferred_element_type=F32)
                    + lax.dot_general(a_ref[:, tk:], bu_ref[...], dims, preferred_element_type=F32))

        @pl.when(k == 0)
        def _():
            acc_ref[...] = part()

        @pl.when(jnp.logical_and(k > 0, k < nk - 1))
        def _():
            acc_ref[...] += part()

        @pl.when(k == nk - 1)
        def _():
            o_ref[...] = acc_ref[...] + part()

    wspec = lambda half: pl.BlockSpec(
        (None, tn, tk), lambda i, j, kk: (_up_shard(2 * kk + half)[0], j, _up_shard(2 * kk + half)[1]))
    return pl.pallas_call(
        body, name=name, grid=(m // tm, d // tn, nk), out_shape=jax.ShapeDtypeStruct((m, d), F32),
        in_specs=[pl.BlockSpec((tm, 2 * tk), lambda i, j, kk: (i, kk)), wspec(0), wspec(1)],
        out_specs=pl.BlockSpec((tm, tn), lambda i, j, kk: (i, j)),
        scratch_shapes=[pltpu.VMEM((tm, tn), F32)],
        compiler_params=_params(("parallel", "parallel", "arbitrary")),
    )(dgu, wg_up, wg_up)


def _mm_up_bwd_w(name, n2, dgu, tm, tk):
    s, d = n2.shape
    tk = min(tk, s)
    tn = dgu.shape[1] // (2 * N_CHIPS)
    return _matmul(name, n2, dgu, (d // tm, 2 * N_CHIPS, s // tk),
                   pl.BlockSpec((tk, tm), lambda i, j, kk: (kk, i)),
                   pl.BlockSpec((tk, tn), lambda i, j, kk: (kk, j)),
                   pl.BlockSpec((None, tm, tn), lambda i, j, kk: (_up_shard(j)[0], i, _up_shard(j)[1])),
                   jax.ShapeDtypeStruct((N_CHIPS, d, 2 * tn), F32), TN)


def _rmsnorm_fwd(name, x, g, tm=256):
    s, d = x.shape

    def body(x_ref, g_ref, n_ref, r_ref):
        xf = x_ref[...]
        r = lax.rsqrt(jnp.mean(xf * xf, axis=-1, keepdims=True) + EPS)
        n_ref[...] = ((xf * r) * g_ref[...]).astype(BF16)
        r_ref[...] = r

    return pl.pallas_call(
        body, grid=(s // tm,), name=name,
        in_specs=[pl.BlockSpec((tm, d), lambda i: (i, 0)), pl.BlockSpec((1, d), lambda i: (0, 0))],
        out_specs=[pl.BlockSpec((tm, d), lambda i: (i, 0)), pl.BlockSpec((tm, 1), lambda i: (i, 0))],
        out_shape=[jax.ShapeDtypeStruct((s, d), BF16), jax.ShapeDtypeStruct((s, 1), F32)],
        compiler_params=_params(("parallel",)),
    )(x, g)


def _rmsnorm_bwd(name, x, rstd, g, dn, dres, tm=256):
    s, d = x.shape

    def body(x_ref, r_ref, g_ref, dn_ref, dres_ref, dx_ref, dxb_ref, dg_ref):
        @pl.when(pl.program_id(0) == 0)
        def _():
            dg_ref[...] = jnp.zeros_like(dg_ref)

        r = r_ref[...]
        xhat = x_ref[...] * r
        dn_v = dn_ref[...]
        dxhat = dn_v * g_ref[...]
        dx = dres_ref[...] + r * (dxhat - xhat * jnp.mean(dxhat * xhat, axis=-1, keepdims=True))
        dx_ref[...] = dx
        dxb_ref[...] = dx.astype(BF16)
        dg_ref[...] += jnp.sum(dn_v * xhat, axis=0, keepdims=True)

    row = pl.BlockSpec((tm, d), lambda i: (i, 0))
    vec = pl.BlockSpec((1, d), lambda i: (0, 0))
    return pl.pallas_call(
        body, grid=(s // tm,), name=name,
        in_specs=[row, pl.BlockSpec((tm, 1), lambda i: (i, 0)), vec, row, row],
        out_specs=[row, row, vec],
        out_shape=[jax.ShapeDtypeStruct((s, d), F32), jax.ShapeDtypeStruct((s, d), BF16),
                   jax.ShapeDtypeStruct((1, d), F32)],
        compiler_params=_params(("arbitrary",)),
    )(x, rstd, g, dn, dres)


def _loss_head(name, x, g, target, tm=256):
    s, d = x.shape

    def body(x_ref, g_ref, t_ref, loss_ref, dx_ref, dxb_ref, dg_ref):
        @pl.when(pl.program_id(0) == 0)
        def _():
            dg_ref[...] = jnp.zeros_like(dg_ref)
            loss_ref[...] = jnp.zeros_like(loss_ref)

        xf = x_ref[...]
        gv = g_ref[...]
        r = lax.rsqrt(jnp.mean(xf * xf, axis=-1, keepdims=True) + EPS)
        xhat = xf * r
        err = xhat * gv - t_ref[...]
        loss_ref[...] += 0.5 * jnp.sum(jnp.mean(err * err, axis=-1, keepdims=True), axis=0, keepdims=True)
        dy = err * (1.0 / d)
        dxhat = dy * gv
        dx = r * (dxhat - xhat * jnp.mean(dxhat * xhat, axis=-1, keepdims=True))
        dx_ref[...] = dx
        dxb_ref[...] = dx.astype(BF16)
        dg_ref[...] += jnp.sum(dy * xhat, axis=0, keepdims=True)

    row = pl.BlockSpec((tm, d), lambda i: (i, 0))
    vec = pl.BlockSpec((1, d), lambda i: (0, 0))
    return pl.pallas_call(
        body, grid=(s // tm,), name=name,
        in_specs=[row, vec, row],
        out_specs=[pl.BlockSpec((1, 1), lambda i: (0, 0)), row, row, vec],
        out_shape=[jax.ShapeDtypeStruct((1, 1), F32), jax.ShapeDtypeStruct((s, d), F32),
                   jax.ShapeDtypeStruct((s, d), BF16), jax.ShapeDtypeStruct((1, d), F32)],
        compiler_params=_params(("arbitrary",)),
    )(x, g, target)


def _sigmoid(v):
    return 0.5 * jnp.tanh(0.5 * v) + 0.5


def _log_sigmoid(v):
    return jnp.minimum(v, 0.0) - jnp.log1p(jnp.exp(-jnp.abs(v)))


def _softplus(v):
    return jnp.maximum(v, 0.0) + jnp.log1p(jnp.exp(-jnp.abs(v)))


def _one_minus_exp(z):
    series = -z * (1.0 + z * (0.5 + z * (1.0 / 6.0 + z * (1.0 / 24.0 + z * (1.0 / 120.0)))))
    return jnp.where(z > -0.1, series, 1.0 - jnp.exp(z))


_GELU_K = 0.7978845608028654
_GELU_C = 0.044715


def _gelu(v):
    return 0.5 * v * (1.0 + jnp.tanh(_GELU_K * (v + _GELU_C * v * v * v)))


def _gelu_grad(v):
    t = jnp.tanh(_GELU_K * (v + _GELU_C * v * v * v))
    return 0.5 * (1.0 + t) + 0.5 * v * (1.0 - t * t) * _GELU_K * (1.0 + 3.0 * _GELU_C * v * v)


def _rows(shape):
    return lax.broadcasted_iota(jnp.int32, shape, 0)


def _cols(shape):
    return lax.broadcasted_iota(jnp.int32, shape, 1)


def _shift_down(v, prev, d):
    if d == 0:
        return v
    rolled = pltpu.roll(v, d, axis=0)
    head = jnp.where(_rows((HALO, v.shape[1])) >= d, rolled[:HALO], pltpu.roll(prev, d, axis=0))
    if v.shape[0] == HALO:
        return head
    return jnp.concatenate([head, rolled[HALO:]], axis=0)


def _shift_up(v, nxt, d):
    if d == 0:
        return v
    n = v.shape[0]
    rolled = pltpu.roll(v, n - d, axis=0)
    tail = jnp.where(_rows((HALO, v.shape[1])) < HALO - d, rolled[n - HALO:], pltpu.roll(nxt, HALO - d, axis=0))
    if n == HALO:
        return tail
    return jnp.concatenate([rolled[:n - HALO], tail], axis=0)


def _dot(a, b, contract):
    return lax.dot_general(a.astype(BF16), b.astype(BF16), (contract, ((), ())), preferred_element_type=F32)


def _mlstm_chunk_common(h, q_ref, k_ref, v_ref, gcol_ref, grow_ref, brow_ref, bcol_ref, m_prev):
    L = CHUNK
    sl = slice(h * HEAD_DIM, (h + 1) * HEAD_DIM)
    qh = q_ref[:, sl]
    kh = k_ref[:, sl]
    vh = v_ref[:, sl]
    qs = qh * (HEAD_DIM ** -0.5)
    gates = gcol_ref[...] + brow_ref[...]
    lane = _cols(gates.shape)
    ic = jnp.sum(jnp.where(lane == h, gates, 0.0), axis=1, keepdims=True)
    fc = jnp.sum(jnp.where(lane == HEADS + h, gates, 0.0), axis=1, keepdims=True)
    ir = grow_ref[h:h + 1, :] + bcol_ref[h:h + 1, :]
    fr = grow_ref[HEADS + h:HEADS + h + 1, :] + bcol_ref[HEADS + h:HEADS + h + 1, :]
    logf_c = _log_sigmoid(fc)
    logf_r = _log_sigmoid(fr)
    t_i = _rows((L, L))
    s_i = _cols((L, L))
    tri = t_i >= s_i
    b_c = jnp.sum(jnp.where(tri, logf_r, 0.0), axis=1, keepdims=True)
    b_r = jnp.sum(jnp.where(t_i <= s_i, logf_c, 0.0), axis=0, keepdims=True)
    btot = jnp.sum(logf_r, axis=1, keepdims=True)
    dmat = jnp.where(tri, b_c - b_r + ir, -jnp.inf)
    m_inter = b_c + m_prev
    m_t = jnp.maximum(m_inter, jnp.max(dmat, axis=1, keepdims=True))
    e_mat = jnp.exp(dmat - m_t)
    e_inter = jnp.exp(m_inter - m_t)
    wqk = _dot(qs, kh, NT) * e_mat
    w_end_r = btot - b_r + ir
    m_loc = jnp.max(w_end_r, axis=1, keepdims=True)
    e_end_c = jnp.exp(btot - b_c + ic - m_loc)
    m_new = jnp.maximum(btot + m_prev, m_loc)
    a_dec = jnp.exp(btot + m_prev - m_new)
    c_inj = jnp.exp(m_loc - m_new)
    return dict(qh=qh, kh=kh, vh=vh, qs=qs, fc=fc, tri=tri, t_i=t_i, s_i=s_i, m_t=m_t, e_mat=e_mat,
                e_inter=e_inter, wqk=wqk, e_end_c=e_end_c, m_new=m_new, a_dec=a_dec, c_inj=c_inj)


def _mlstm_fwd(proj, gates_t, bias_row, bias_col, head_g):
    s = proj.shape[0]
    nc = s // CHUNK
    L = CHUNK

    def body(q_ref, k_ref, v_ref, o_ref, gcol_ref, grow_ref, brow_ref, bcol_ref, hg_ref,
             out_ref, cprev_ref, nprev_ref, mprev_ref, c_scr, n_scr, m_scr):
        @pl.when(pl.program_id(0) == 0)
        def _():
            c_scr[...] = jnp.zeros_like(c_scr)
            n_scr[...] = jnp.zeros_like(n_scr)
            m_scr[...] = jnp.zeros_like(m_scr)

        for h in range(HEADS):
            sl = slice(h * HEAD_DIM, (h + 1) * HEAD_DIM)
            m_prev = m_scr[h:h + 1, 0:1]
            n_prev = n_scr[h:h + 1, :]
            c_prev = c_scr[h].astype(BF16)
            q = _mlstm_chunk_common(h, q_ref, k_ref, v_ref, gcol_ref, grow_ref, brow_ref, bcol_ref, m_prev)
            num = _dot(q["wqk"], q["vh"], NN) + q["e_inter"] * _dot(q["qs"], c_prev, NN)
            den = (jnp.sum(q["wqk"], axis=1, keepdims=True)
                   + q["e_inter"] * jnp.sum(q["qs"] * n_prev, axis=1, keepdims=True))
            hh = num / jnp.maximum(jnp.abs(den), jnp.exp(-q["m_t"]))
            hn = hh * lax.rsqrt(jnp.mean(hh * hh, axis=1, keepdims=True) + EPS) * hg_ref[h:h + 1, :]
            out_ref[:, sl] = (_sigmoid(o_ref[:, sl]) * hn).astype(BF16)
            cprev_ref[h] = c_prev
            nprev_ref[h:h + 1, :] = n_prev
            mprev_ref[h:h + 1, :] = jnp.broadcast_to(m_prev, (1, LANES))
            c_loc = _dot(q["kh"], q["e_end_c"] * q["vh"], TN)
            n_loc = jnp.sum(q["e_end_c"] * q["kh"], axis=0, keepdims=True)
            c_scr[h] = q["a_dec"] * c_scr[h] + q["c_inj"] * c_loc
            n_scr[h:h + 1, :] = q["a_dec"] * n_prev + q["c_inj"] * n_loc
            m_scr[h:h + 1, :] = jnp.broadcast_to(q["m_new"], (1, LANES))

    blk = lambda j: pl.BlockSpec((L, D_MLSTM), lambda c, j=j: (c, j))
    full = lambda shp: pl.BlockSpec(shp, lambda c: tuple(0 for _ in shp))
    return pl.pallas_call(
        body, grid=(nc,), name="mlstm_fwd",
        in_specs=[blk(0), blk(1), blk(2), blk(3),
                  pl.BlockSpec((L, LANES), lambda c: (c, (4 * D_MLSTM + 2 * D_LRU) // LANES)),
                  pl.BlockSpec((None, 2 * HEADS, L), lambda c: (c, 0, 0)),
                  full((1, LANES)), full((2 * HEADS, 1)), full((HEADS, HEAD_DIM))],
        out_specs=[pl.BlockSpec((L, D_MLSTM), lambda c: (c, 0)),
                   pl.BlockSpec((None, HEADS, HEAD_DIM, HEAD_DIM), lambda c: (c, 0, 0, 0)),
                   pl.BlockSpec((None, HEADS, HEAD_DIM), lambda c: (c, 0, 0)),
                   pl.BlockSpec((None, HEADS, LANES), lambda c: (c, 0, 0))],
        out_shape=[jax.ShapeDtypeStruct((s, D_MLSTM + D_LRU), BF16),
                   jax.ShapeDtypeStruct((nc, HEADS, HEAD_DIM, HEAD_DIM), BF16),
                   jax.ShapeDtypeStruct((nc, HEADS, HEAD_DIM), F32),
                   jax.ShapeDtypeStruct((nc, HEADS, LANES), F32)],
        scratch_shapes=[pltpu.VMEM((HEADS, HEAD_DIM, HEAD_DIM), F32), pltpu.VMEM((HEADS, HEAD_DIM), F32),
                        pltpu.VMEM((HEADS, LANES), F32)],
        compiler_params=_params(("arbitrary",)),
    )(proj, proj, proj, proj, proj, gates_t, bias_row, bias_col, head_g)


def _mlstm_bwd(proj, gates_t, bias_row, bias_col, head_g, cprev, nprev, mprev, dmix):
    s = proj.shape[0]
    nc = s // CHUNK
    L = CHUNK

    def body(q_ref, k_ref, v_ref, o_ref, gcol_ref, grow_ref, brow_ref, bcol_ref, hg_ref,
             cprev_ref, nprev_ref, mprev_ref, dmix_ref,
             dqkvo_ref, dgate_ref, dhg_ref, g_scr, gn_scr):
        @pl.when(pl.program_id(0) == 0)
        def _():
            g_scr[...] = jnp.zeros_like(g_scr)
            gn_scr[...] = jnp.zeros_like(gn_scr)
            dhg_ref[...] = jnp.zeros_like(dhg_ref)

        lane = _cols((L, LANES))
        dgate = jnp.zeros((L, LANES), F32)
        for h in range(HEADS):
            sl = slice(h * HEAD_DIM, (h + 1) * HEAD_DIM)
            m_prev = mprev_ref[h:h + 1, 0:1]
            n_prev = nprev_ref[h:h + 1, :]
            c_prev = cprev_ref[h]
            q = _mlstm_chunk_common(h, q_ref, k_ref, v_ref, gcol_ref, grow_ref, brow_ref, bcol_ref, m_prev)
            qh, kh, vh, qs, wqk, e_inter = q["qh"], q["kh"], q["vh"], q["qs"], q["wqk"], q["e_inter"]
            num_state = e_inter * _dot(qs, c_prev, NN)
            den_state = e_inter * jnp.sum(qs * n_prev, axis=1, keepdims=True)
            num = _dot(wqk, vh, NN) + num_state
            den = jnp.sum(wqk, axis=1, keepdims=True) + den_state
            floor = jnp.exp(-q["m_t"])
            denom = jnp.maximum(jnp.abs(den), floor)
            hh = num / denom
            rn = lax.rsqrt(jnp.mean(hh * hh, axis=1, keepdims=True) + EPS)
            hn_pre = hh * rn
            hg = hg_ref[h:h + 1, :]
            sg = _sigmoid(o_ref[:, sl])
            dout = dmix_ref[:, sl]
            d_o = dout * (hn_pre * hg) * sg * (1.0 - sg)
            dhn = dout * sg
            dhg_ref[h:h + 1, :] += jnp.sum(dhn * hn_pre, axis=0, keepdims=True)
            dhn_pre = dhn * hg
            dhh = rn * (dhn_pre - hn_pre * jnp.mean(dhn_pre * hn_pre, axis=1, keepdims=True))
            dnum = dhh / denom
            dden = jnp.where(jnp.abs(den) >= floor,
                             -jnp.sum(hh * dhh, axis=1, keepdims=True) / denom * jnp.sign(den), 0.0)
            dwqk = _dot(dnum, vh, NT) + dden
            dv = _dot(wqk, dnum, TN)
            dp = dwqk * q["e_mat"]
            dqs = _dot(dp, kh, NN) + e_inter * (_dot(dnum, c_prev, NT) + dden * n_prev)
            dk = _dot(dp, qs, TN)
            g_next = g_scr[h]
            gn_next = gn_scr[h:h + 1, :]
            w_state = q["e_end_c"] * q["c_inj"]
            dk_state = w_state * (_dot(vh, g_next, NT) + gn_next)
            dk = dk + dk_state
            dv = dv + w_state * _dot(kh, g_next, NN)
            dq = dqs * (HEAD_DIM ** -0.5)
            eye = q["t_i"] == q["s_i"]
            to_row = lambda col: jnp.sum(jnp.where(eye, col, 0.0), axis=0, keepdims=True)
            to_col = lambda row: jnp.sum(jnp.where(eye, row, 0.0), axis=1, keepdims=True)
            g_pair = dwqk * wqk
            rs_in = jnp.sum(g_pair, axis=1, keepdims=True)
            cs_in_r = jnp.sum(g_pair, axis=0, keepdims=True)
            rs_state = (jnp.sum(dnum * num_state, axis=1, keepdims=True) + dden * den_state)
            cs_state = jnp.sum(kh * dk_state, axis=1, keepdims=True)
            di_c = to_col(cs_in_r) + cs_state
            through = q["a_dec"] * (jnp.sum(jnp.sum(g_next * c_prev.astype(F32), axis=1, keepdims=True),
                                            axis=0, keepdims=True)
                                    + jnp.sum(gn_next * n_prev, axis=1, keepdims=True))
            ends_here = to_row(rs_in + rs_state) - cs_in_r
            da_c = (jnp.sum(jnp.where(q["s_i"] >= q["t_i"], ends_here, 0.0), axis=1, keepdims=True)
                    + jnp.sum(jnp.where(q["s_i"] < q["t_i"], to_row(cs_state), 0.0), axis=1, keepdims=True)
                    + through)
            df_c = da_c * _sigmoid(-q["fc"])
            dgate = dgate + jnp.where(lane == h, di_c, 0.0) + jnp.where(lane == HEADS + h, df_c, 0.0)
            dqkvo_ref[:, sl] = dq.astype(BF16)
            dqkvo_ref[:, D_MLSTM + h * HEAD_DIM:D_MLSTM + (h + 1) * HEAD_DIM] = dk.astype(BF16)
            dqkvo_ref[:, 2 * D_MLSTM + h * HEAD_DIM:2 * D_MLSTM + (h + 1) * HEAD_DIM] = dv.astype(BF16)
            dqkvo_ref[:, 3 * D_MLSTM + h * HEAD_DIM:3 * D_MLSTM + (h + 1) * HEAD_DIM] = d_o.astype(BF16)
            g_scr[h] = q["a_dec"] * g_next + _dot(e_inter * qs, dnum, TN)
            gn_scr[h:h + 1, :] = q["a_dec"] * gn_next + jnp.sum(e_inter * qs * dden, axis=0, keepdims=True)
        dgate_ref[...] = dgate

    rev = lambda c: nc - 1 - c
    blk = lambda j: pl.BlockSpec((L, D_MLSTM), lambda c, j=j: (rev(c), j))
    full = lambda shp: pl.BlockSpec(shp, lambda c: tuple(0 for _ in shp))
    return pl.pallas_call(
        body, grid=(nc,), name="mlstm_bwd",
        in_specs=[blk(0), blk(1), blk(2), blk(3),
                  pl.BlockSpec((L, LANES), lambda c: (rev(c), (4 * D_MLSTM + 2 * D_LRU) // LANES)),
                  pl.BlockSpec((None, 2 * HEADS, L), lambda c: (rev(c), 0, 0)),
                  full((1, LANES)), full((2 * HEADS, 1)), full((HEADS, HEAD_DIM)),
                  pl.BlockSpec((None, HEADS, HEAD_DIM, HEAD_DIM), lambda c: (rev(c), 0, 0, 0)),
                  pl.BlockSpec((None, HEADS, HEAD_DIM), lambda c: (rev(c), 0, 0)),
                  pl.BlockSpec((None, HEADS, LANES), lambda c: (rev(c), 0, 0)),
                  pl.BlockSpec((L, D_MLSTM), lambda c: (rev(c), 0))],
        out_specs=[pl.BlockSpec((L, 4 * D_MLSTM), lambda c: (rev(c), 0)),
                   pl.BlockSpec((L, LANES), lambda c: (rev(c), 0)),
                   full((HEADS, HEAD_DIM))],
        out_shape=[jax.ShapeDtypeStruct((s, _PROJ_PAD), BF16),
                   jax.ShapeDtypeStruct((s, LANES), F32),
                   jax.ShapeDtypeStruct((HEADS, HEAD_DIM), F32)],
        scratch_shapes=[pltpu.VMEM((HEADS, HEAD_DIM, HEAD_DIM), F32), pltpu.VMEM((HEADS, HEAD_DIM), F32)],
        compiler_params=_params(("arbitrary",)),
    )(proj, proj, proj, proj, proj, gates_t, bias_row, bias_col, head_g, cprev, nprev, mprev, dmix)


def _lru_gates(xc, wa_ref, wx_ref, ba, bx, lam):
    r = _sigmoid(_dot(xc, wa_ref[...], NN) + ba)
    ig = _sigmoid(_dot(xc, wx_ref[...], NN) + bx)
    sp = _softplus(-lam)
    log_a = (-LRU_C * r) * sp
    a = jnp.exp(log_a)
    mult = jnp.sqrt(_one_minus_exp(2.0 * log_a))
    return r, ig, sp, a, mult


def _lru_conv(xr, prev, w_ref, b):
    xc = b + _shift_down(xr, prev, 3) * w_ref[0:1, :]
    for j in range(1, LRU_CONV):
        xc = xc + _shift_down(xr, prev, LRU_CONV - 1 - j) * w_ref[j:j + 1, :]
    return xc


def _lru_fwd(proj, mix, conv_w, conv_b, wa, wx, ba, bx, lam, tt=1024):
    s = proj.shape[0]
    tt = min(tt, s)
    nt = s // tt
    B = LRU_BLOCK_DIM
    lru_col = 4 * D_MLSTM // B
    mix_col = D_MLSTM // B

    def body(xr_ref, gr_ref, cw_ref, cb_ref, wa_ref, wx_ref, ba_ref, bx_ref, lam_ref, mix_in_ref,
             out_ref, h_ref, prev_scr, hcar_scr):
        @pl.when(pl.program_id(1) == 0)
        def _():
            prev_scr[...] = jnp.zeros_like(prev_scr)
            hcar_scr[...] = jnp.zeros_like(hcar_scr)

        xr = xr_ref[...]
        xc = _lru_conv(xr, prev_scr[...], cw_ref, cb_ref[...])
        prev_scr[...] = xr[tt - HALO:, :]
        _, ig, _, a, mult = _lru_gates(xc, wa_ref, wx_ref, ba_ref[...], bx_ref[...], lam_ref[...])
        u = mult * (ig * xc)
        rows = _rows((tt, B))
        acc_a, acc_b = a, u
        d = 1
        while d < tt:
            if d < HALO:
                keep = rows >= d
                sh_a = jnp.where(keep, pltpu.roll(acc_a, d, axis=0), 1.0)
                sh_b = jnp.where(keep, pltpu.roll(acc_b, d, axis=0), 0.0)
            else:
                sh_a = jnp.concatenate([jnp.ones((d, B), F32), acc_a[:tt - d]], axis=0)
                sh_b = jnp.concatenate([jnp.zeros((d, B), F32), acc_b[:tt - d]], axis=0)
            acc_b = acc_a * sh_b + acc_b
            acc_a = acc_a * sh_a
            d *= 2
        hv = acc_b + acc_a * hcar_scr[0:1, :]
        hcar_scr[...] = jnp.broadcast_to(hv[tt - 1:tt, :], hcar_scr.shape)
        h_ref[...] = hv
        out_ref[...] = (hv * _gelu(gr_ref[...])).astype(BF16)

    chan = lambda rws: pl.BlockSpec((rws, B), lambda n, i: (0, n))
    return pl.pallas_call(
        body, grid=(LRU_BLOCKS, nt), name="lru_fwd",
        in_specs=[pl.BlockSpec((tt, B), lambda n, i: (i, lru_col + 2 * n)),
                  pl.BlockSpec((tt, B), lambda n, i: (i, lru_col + 2 * n + 1)),
                  chan(LRU_CONV), chan(1),
                  pl.BlockSpec((None, B, B), lambda n, i: (n, 0, 0)),
                  pl.BlockSpec((None, B, B), lambda n, i: (n, 0, 0)),
                  chan(1), chan(1), chan(1), ANY],
        out_specs=[pl.BlockSpec((tt, B), lambda n, i: (i, mix_col + n)), pl.BlockSpec((tt, B), lambda n, i: (i, n))],
        out_shape=[jax.ShapeDtypeStruct(mix.shape, BF16), jax.ShapeDtypeStruct((s, D_LRU), F32)],
        scratch_shapes=[pltpu.VMEM((HALO, B), F32), pltpu.VMEM((HALO, B), F32)],
        input_output_aliases={9: 0},
        compiler_params=_params(("parallel", "arbitrary")),
    )(proj, proj, conv_w, conv_b, wa, wx, ba, bx, lam, mix)


def _lru_bwd(proj, hsave, dmix, dproj, conv_w, conv_b, wa, wx, ba, bx, lam, tt=1024):
    s = proj.shape[0]
    tt = min(tt, s)
    nt = s // tt
    B = LRU_BLOCK_DIM
    lru_col = 4 * D_MLSTM // B
    dmix_col = D_MLSTM // B
    hpb = tt // HALO

    def body(xr_ref, xprev_ref, gr_ref, h_ref, hprev_ref, dmix_ref, cw_ref, cb_ref, wa_ref, wx_ref,
             ba_ref, bx_ref, lam_ref, dproj_in_ref,
             dxg_ref, dcw_ref, dcb_ref, dwa_ref, dwx_ref, dba_ref, dbx_ref, dlam_ref,
             gcar_scr, acar_scr, dxc_scr):
        i = pl.program_id(1)
        first_tile = i == nt - 1

        @pl.when(i == 0)
        def _():
            gcar_scr[...] = jnp.zeros_like(gcar_scr)
            acar_scr[...] = jnp.zeros_like(acar_scr)
            dxc_scr[...] = jnp.zeros_like(dxc_scr)
            for ref in (dcw_ref, dcb_ref, dwa_ref, dwx_ref, dba_ref, dbx_ref, dlam_ref):
                ref[...] = jnp.zeros_like(ref)

        xr = xr_ref[...]
        xprev = jnp.where(first_tile, 0.0, xprev_ref[...])
        hprev = jnp.where(first_tile, 0.0, hprev_ref[...])
        lam = lam_ref[...]
        taps = [_shift_down(xr, xprev, LRU_CONV - 1 - j) for j in range(LRU_CONV)]
        xc = cb_ref[...] + taps[0] * cw_ref[0:1, :]
        for j in range(1, LRU_CONV):
            xc = xc + taps[j] * cw_ref[j:j + 1, :]
        r, ig, sp, a, mult = _lru_gates(xc, wa_ref, wx_ref, ba_ref[...], bx_ref[...], lam)
        gr = gr_ref[...]
        hv = h_ref[...]
        dout = dmix_ref[...]
        dxg_ref[:, B:] = (dout * hv * _gelu_grad(gr)).astype(BF16)
        dh = dout * _gelu(gr)
        rows = _rows((tt, B))
        acc_a = _shift_up(a, acar_scr[...], 1)
        acc_b = dh
        d = 1
        while d < tt:
            if d < HALO:
                keep = rows < tt - d
                sh_a = jnp.where(keep, pltpu.roll(acc_a, tt - d, axis=0), 1.0)
                sh_b = jnp.where(keep, pltpu.roll(acc_b, tt - d, axis=0), 0.0)
            else:
                sh_a = jnp.concatenate([acc_a[d:], jnp.ones((d, B), F32)], axis=0)
                sh_b = jnp.concatenate([acc_b[d:], jnp.zeros((d, B), F32)], axis=0)
            acc_b = acc_a * sh_b + acc_b
            acc_a = acc_a * sh_a
            d *= 2
        gv = acc_b + acc_a * gcar_scr[0:1, :]
        gcar_scr[...] = jnp.broadcast_to(gv[0:1, :], gcar_scr.shape)
        acar_scr[...] = jnp.broadcast_to(a[0:1, :], acar_scr.shape)
        h_before = _shift_down(hv, hprev, 1)
        da = gv * h_before
        dmult = gv * (ig * xc)
        dig = gv * mult * xc
        dxc = gv * mult * ig
        dlog_a = da * a - dmult * (a * a) / mult
        dr = dlog_a * (-LRU_C * sp)
        dlam_ref[...] += jnp.sum(dlog_a * (-LRU_C * r), axis=0, keepdims=True) * (-_sigmoid(-lam))
        dpre_r = dr * r * (1.0 - r)
        dpre_i = dig * ig * (1.0 - ig)
        dba_ref[...] += jnp.sum(dpre_r, axis=0, keepdims=True)
        dbx_ref[...] += jnp.sum(dpre_i, axis=0, keepdims=True)
        dwa_ref[...] += _dot(xc, dpre_r, TN)
        dwx_ref[...] += _dot(xc, dpre_i, TN)
        dxc = dxc + _dot(dpre_r, wa_ref[...], NT) + _dot(dpre_i, wx_ref[...], NT)
        dcb_ref[...] += jnp.sum(dxc, axis=0, keepdims=True)
        nxt = dxc_scr[...]
        dxr = jnp.zeros((tt, B), F32)
        for j in range(LRU_CONV):
            sft = LRU_CONV - 1 - j
            dcw_ref[j:j + 1, :] += jnp.sum(dxc * taps[j], axis=0, keepdims=True)
            dxr = dxr + _shift_up(dxc, nxt, sft) * cw_ref[j:j + 1, :]
        dxc_scr[...] = dxc[:HALO, :]
        dxg_ref[:, :B] = dxr.astype(BF16)

    rev = lambda i: nt - 1 - i
    tile = lambda col, step: pl.BlockSpec((tt, B), lambda n, i: (rev(i), col + step * n))
    halo = lambda col, step: pl.BlockSpec(
        (HALO, B), lambda n, i: (jnp.maximum(rev(i) * hpb - 1, 0), col + step * n))
    chan = lambda rws: pl.BlockSpec((rws, B), lambda n, i: (0, n))
    wblk = pl.BlockSpec((None, B, B), lambda n, i: (n, 0, 0))
    return pl.pallas_call(
        body, grid=(LRU_BLOCKS, nt), name="lru_bwd",
        in_specs=[tile(lru_col, 2), halo(lru_col, 2), tile(lru_col + 1, 2), tile(0, 1), halo(0, 1),
                  tile(dmix_col, 1), chan(LRU_CONV), chan(1), wblk, wblk, chan(1), chan(1), chan(1), ANY],
        out_specs=[pl.BlockSpec((tt, 2 * B), lambda n, i: (rev(i), lru_col // 2 + n)),
                   chan(LRU_CONV), chan(1), wblk, wblk, chan(1), chan(1), chan(1)],
        out_shape=[jax.ShapeDtypeStruct(dproj.shape, BF16),
                   jax.ShapeDtypeStruct((LRU_CONV, D_LRU), F32), jax.ShapeDtypeStruct((1, D_LRU), F32),
                   jax.ShapeDtypeStruct((LRU_BLOCKS, B, B), F32), jax.ShapeDtypeStruct((LRU_BLOCKS, B, B), F32),
                   jax.ShapeDtypeStruct((1, D_LRU), F32), jax.ShapeDtypeStruct((1, D_LRU), F32),
                   jax.ShapeDtypeStruct((1, D_LRU), F32)],
        scratch_shapes=[pltpu.VMEM((HALO, B), F32), pltpu.VMEM((HALO, B), F32), pltpu.VMEM((HALO, B), F32)],
        input_output_aliases={13: 0},
        compiler_params=_params(("parallel", "arbitrary")),
    )(proj, proj, proj, hsave, hsave, dmix, conv_w, conv_b, wa, wx, ba, bx, lam, dproj)


def _ffn_conv(gp, prev, w_ref, b):
    g = b + _shift_down(gp, prev, 2) * w_ref[0:1, :]
    for j in range(1, FFN_CONV):
        g = g + _shift_down(gp, prev, FFN_CONV - 1 - j) * w_ref[j:j + 1, :]
    return g


def _ffn_act_fwd(gu, conv_w, conv_b, tt=256):
    s = gu.shape[0]
    tt = min(tt, s)
    d_ff = conv_w.shape[1]
    tc = d_ff // N_CHIPS
    hpb = tt // HALO

    def body(g_ref, gprev_ref, u_ref, w_ref, b_ref, act_ref):
        prev = jnp.where(pl.program_id(0) == 0, 0.0, gprev_ref[...])
        gate = _ffn_conv(g_ref[...], prev, w_ref, b_ref[...])
        act_ref[...] = (gate * _sigmoid(gate) * u_ref[...]).astype(BF16)

    return pl.pallas_call(
        body, grid=(s // tt, N_CHIPS), name="ffn_act_fwd",
        in_specs=[pl.BlockSpec((tt, tc), lambda i, j: (i, 2 * j)),
                  pl.BlockSpec((HALO, tc), lambda i, j: (jnp.maximum(i * hpb - 1, 0), 2 * j)),
                  pl.BlockSpec((tt, tc), lambda i, j: (i, 2 * j + 1)),
                  pl.BlockSpec((FFN_CONV, tc), lambda i, j: (0, j)),
                  pl.BlockSpec((1, tc), lambda i, j: (0, j))],
        out_specs=pl.BlockSpec((tt, tc), lambda i, j: (i, j)),
        out_shape=jax.ShapeDtypeStruct((s, d_ff), BF16),
        compiler_params=_params(("parallel", "parallel")),
    )(gu, gu, gu, conv_w, conv_b)


def _ffn_act_bwd(gu, dact, conv_w, conv_b, tt=256):
    s = gu.shape[0]
    tt = min(tt, s)
    nt = s // tt
    d_ff = conv_w.shape[1]
    tc = d_ff // N_CHIPS
    hpb = tt // HALO

    def dgate_of(gate, up, da):
        sg = _sigmoid(gate)
        return da * up * (sg * (1.0 + gate * (1.0 - sg))), da * (gate * sg)

    def body(g_ref, gprev_ref, gnext_ref, u_ref, unext_ref, da_ref, danext_ref, w_ref, b_ref,
             dgu_ref, dw_ref, db_ref):
        i = pl.program_id(1)

        @pl.when(i == 0)
        def _():
            dw_ref[...] = jnp.zeros_like(dw_ref)
            db_ref[...] = jnp.zeros_like(db_ref)

        gp = g_ref[...]
        prev = jnp.where(i == 0, 0.0, gprev_ref[...])
        bias = b_ref[...]
        taps = [_shift_down(gp, prev, FFN_CONV - 1 - j) for j in range(FFN_CONV)]
        gate = bias + taps[0] * w_ref[0:1, :]
        for j in range(1, FFN_CONV):
            gate = gate + taps[j] * w_ref[j:j + 1, :]
        dgate, dup = dgate_of(gate, u_ref[...], da_ref[...])
        gate_n = _ffn_conv(gnext_ref[...], gp[tt - HALO:, :], w_ref, bias)
        dgate_n, _ = dgate_of(gate_n, unext_ref[...], danext_ref[...])
        dgate_n = jnp.where(i == nt - 1, 0.0, dgate_n)
        db_ref[...] += jnp.sum(dgate, axis=0, keepdims=True)
        dgp = jnp.zeros((tt, tc), F32)
        for j in range(FFN_CONV):
            dw_ref[j:j + 1, :] += jnp.sum(dgate * taps[j], axis=0, keepdims=True)
            dgp = dgp + _shift_up(dgate, dgate_n, FFN_CONV - 1 - j) * w_ref[j:j + 1, :]
        dgu_ref[:, :tc] = dgp.astype(BF16)
        dgu_ref[:, tc:] = dup.astype(BF16)

    tile = lambda half: pl.BlockSpec((tt, tc), lambda j, i, half=half: (i, 2 * j + half))
    hprev = lambda half: pl.BlockSpec((HALO, tc), lambda j, i, half=half: (jnp.maximum(i * hpb - 1, 0), 2 * j + half))
    hnext = lambda half: pl.BlockSpec(
        (HALO, tc), lambda j, i, half=half: (jnp.minimum((i + 1) * hpb, nt * hpb - 1), 2 * j + half))
    return pl.pallas_call(
        body, grid=(N_CHIPS, nt), name="ffn_act_bwd",
        in_specs=[tile(0), hprev(0), hnext(0), tile(1), hnext(1),
                  pl.BlockSpec((tt, tc), lambda j, i: (i, j)),
                  pl.BlockSpec((HALO, tc), lambda j, i: (jnp.minimum((i + 1) * hpb, nt * hpb - 1), j)),
                  pl.BlockSpec((FFN_CONV, tc), lambda j, i: (0, j)),
                  pl.BlockSpec((1, tc), lambda j, i: (0, j))],
        out_specs=[pl.BlockSpec((tt, 2 * tc), lambda j, i: (i, j)),
                   pl.BlockSpec((FFN_CONV, tc), lambda j, i: (0, j)),
                   pl.BlockSpec((1, tc), lambda j, i: (0, j))],
        out_shape=[jax.ShapeDtypeStruct((s, 2 * d_ff), BF16),
                   jax.ShapeDtypeStruct((FFN_CONV, d_ff), F32), jax.ShapeDtypeStruct((1, d_ff), F32)],
        compiler_params=_params(("parallel", "arbitrary")),
    )(gu, gu, gu, gu, gu, dact, dact, conv_w, conv_b)


def _gate_grads(dgate, dproj, tm=512):
    s, n = dgate.shape
    tm = min(tm, s)

    def body(a_ref, dproj_in_ref, o_ref, dproj_ref):
        @pl.when(pl.program_id(0) == 0)
        def _():
            o_ref[...] = jnp.zeros_like(o_ref)
        a = a_ref[...]
        o_ref[...] += jnp.sum(a, axis=0, keepdims=True)
        dproj_ref[...] = a.astype(BF16)

    return pl.pallas_call(
        body, grid=(s // tm,), name="gate_grads",
        in_specs=[pl.BlockSpec((tm, n), lambda i: (i, 0)), ANY],
        out_specs=[pl.BlockSpec((1, n), lambda i: (0, 0)),
                   pl.BlockSpec((tm, n), lambda i: (i, (_QKVO + 2 * D_LRU) // LANES))],
        out_shape=[jax.ShapeDtypeStruct((1, n), F32), jax.ShapeDtypeStruct(dproj.shape, BF16)],
        input_output_aliases={1: 1},
        compiler_params=_params(("arbitrary",)),
    )(dgate, dproj)


def _pick(n, *cands):
    for c in cands:
        if n % c == 0:
            return c
    raise ValueError(f"no tile for {n}")


def _behind(a, token):
    return a if token is None else a + token[0:1, 0:1].astype(a.dtype).reshape((1,) * a.ndim)


class _Gathered:
    def __init__(self, w):
        self.w = w

    def begin(self):
        return None

    def mid(self, grp, after):
        return None

    def end(self, grp, after):
        return self.w

    def reduce_early(self, grads):
        return None

    def reduce_early_mid(self, after):
        return None

    def reduce_late(self, grads):
        return None

    def reduce_late_mid(self, after):
        return None


def _local_step(x, target, w, comm):
    s, d = x.shape
    nc = s // CHUNK
    tm = _pick(s, 1024, 512, 256)
    tn_proj = _pick(_PROJ_PAD, 896)
    gate_col = 4 * D_MLSTM + 2 * D_LRU
    w = dict(w)

    token = comm.begin()
    n1, rstd1 = _rmsnorm_fwd("norm_mix_fwd", x, _behind(w["norm_mix_g"], token))
    comm.mid(0, n1)
    w.update(comm.end(0, None))
    proj = _mm_nn("proj_fwd", n1, w["w_in"], tm, tn_proj, d)
    token = comm.mid(1, proj)
    gates = proj[:, gate_col:gate_col + 2 * HEADS]
    gates_t = gates.reshape(nc, CHUNK, 2 * HEADS).transpose(0, 2, 1)
    bias_row = _behind(jnp.pad(w["b_gate_m"], ((0, 0), (0, LANES - 2 * HEADS))), token)
    bias_col = w["b_gate_m"].reshape(2 * HEADS, 1)
    mix, cprev, nprev, mprev = _mlstm_fwd(proj, gates_t, bias_row, bias_col, w["mlstm_norm_g"])
    mix, hsave = _lru_fwd(proj, mix, w["lru_conv_w"], w["lru_conv_b"], w["lru_wa"], w["lru_wx"],
                          w["lru_ba"], w["lru_bx"], w["lru_lambda"])
    w.update(comm.end(1, hsave))
    token = comm.mid(2, hsave)
    x1 = _mm_nn("out_fwd", mix, w["w_out"], tm, 1024, d, res=x)
    n2, rstd2 = _rmsnorm_fwd("norm_ffn_fwd", x1, _behind(w["norm_ffn_g"], token))
    w.update(comm.end(2, n2))
    token = comm.mid(3, n2)
    gu = _mm_up_fwd("up_fwd", n2, w["w_up"], tm, d)
    act = _ffn_act_fwd(gu, w["ffn_conv_w"], _behind(w["ffn_conv_b"], token))
    w.update(comm.end(3, act))
    d_ff = w["w_down"].shape[0]
    x2 = _mm_nn("down_fwd", act, w["w_down"], min(tm, 512), 1024, d_ff // 2, res=x1)
    loss, dx2, dx2b, g_norm_final = _loss_head("loss_head", x2, w["norm_final_g"], target)

    grads = {"norm_final_g": g_norm_final}
    dact = _mm_nt("down_bwd_x", dx2b, w["w_down"], tm, d_ff // N_CHIPS, d)
    grads["w_down"] = _mm_tn("down_bwd_w", act, dx2b, d_ff // N_CHIPS, 1024, 2048)
    dgu, grads["ffn_conv_w"], grads["ffn_conv_b"] = _ffn_act_bwd(gu, dact, w["ffn_conv_w"], w["ffn_conv_b"])
    dn2 = _mm_up_bwd_x("up_bwd_x", dgu, w["w_up"], tm, 1024)
    grads["w_up"] = _mm_up_bwd_w("up_bwd_w", n2, dgu, 1024, 2048)
    dx1, dx1b, grads["norm_ffn_g"] = _rmsnorm_bwd("norm_ffn_bwd", x1, rstd2, w["norm_ffn_g"], dn2, dx2)
    dmix = _mm_nt("out_bwd_x", dx1b, w["w_out"], tm, 1024, d)
    grads["w_out"] = _mm_tn("out_bwd_w", mix, dx1b, 1024, 1024, 2048)
    token = comm.reduce_early(grads)
    dproj, dgate, grads["mlstm_norm_g"] = _mlstm_bwd(proj, gates_t, _behind(bias_row, token), bias_col,
                                                     w["mlstm_norm_g"], cprev, nprev, mprev, dmix)
    token = comm.reduce_early_mid(dproj)
    (dproj, grads["lru_conv_w"], grads["lru_conv_b"], grads["lru_wa"], grads["lru_wx"],
     grads["lru_ba"], grads["lru_bx"], grads["lru_lambda"]) = _lru_bwd(
        proj, hsave, dmix, dproj, w["lru_conv_w"], _behind(w["lru_conv_b"], token), w["lru_wa"], w["lru_wx"],
        w["lru_ba"], w["lru_bx"], w["lru_lambda"])
    gate_bias_grad, dproj = _gate_grads(dgate, dproj)
    grads["b_gate_m"] = gate_bias_grad[:, :2 * HEADS]
    grads["w_in"] = _mm_tn("proj_bwd_w", n1, dproj, 1024, tn_proj, 2048)
    token = comm.reduce_late(grads)
    dn1 = _mm_nt("proj_bwd_x", dproj, w["w_in"], tm, 512, _PROJ_PAD, after=token)
    token = comm.reduce_late_mid(dn1)
    grad_x, _, grads["norm_mix_g"] = _rmsnorm_bwd("norm_mix_bwd", x, rstd1, _behind(w["norm_mix_g"], token),
                                                  dn1, dx1)
    return loss, grad_x, grads


WEIGHT_NAMES = ("norm_mix_g", "w_in", "b_gate_m", "mlstm_norm_g", "lru_conv_w", "lru_conv_b", "lru_wa", "lru_ba",
                "lru_wx", "lru_bx", "lru_lambda", "w_out", "norm_ffn_g", "w_up", "ffn_conv_w", "ffn_conv_b",
                "w_down", "norm_final_g")
BIG = ("w_in", "w_out", "w_up", "w_down")
SMALL_SHARDED = ("mlstm_norm_g", "lru_conv_w", "ffn_conv_w")
SMALL = tuple(n for n in WEIGHT_NAMES if n not in BIG)
SMALL_REPLICATED = tuple(n for n in SMALL if n not in SMALL_SHARDED)


def _proj_segments():
    segs = [(0, 0, _QKVO), (_QKVO, _QKVO + 2 * D_LRU, _N_GATES)]
    for n in range(LRU_BLOCKS):
        segs.append((_QKVO + _N_GATES + n * LRU_BLOCK_DIM, _QKVO + 2 * n * LRU_BLOCK_DIM, LRU_BLOCK_DIM))
        segs.append((_QKVO + _N_GATES + D_LRU + n * LRU_BLOCK_DIM, _QKVO + (2 * n + 1) * LRU_BLOCK_DIM,
                     LRU_BLOCK_DIM))
    return segs


def _w_in_shards_to_local(shards):
    width = shards.shape[2]
    pieces = []
    for g0, _, n in sorted(_proj_segments(), key=lambda s: s[1]):
        at = g0
        while at < g0 + n:
            j = at // width
            stop = min(g0 + n, (j + 1) * width)
            pieces.append(shards[j][:, at - j * width:stop - j * width])
            at = stop
    pieces.append(jnp.zeros((shards.shape[1], PROJ_GATE_PAD - _N_GATES), shards.dtype))
    return jnp.concatenate(pieces, axis=1)


def _w_in_local_to_shards(w):
    width = _PROJ_COLS // N_CHIPS
    shards = []
    for j in range(N_CHIPS):
        pieces = []
        for g0, l0, n in sorted(_proj_segments()):
            lo, hi = max(g0, j * width), min(g0 + n, (j + 1) * width)
            if lo < hi:
                pieces.append(w[:, l0 + lo - g0:l0 + hi - g0])
        shards.append(jnp.concatenate(pieces, axis=1))
    return jnp.stack(shards)


def _w_in_to_global(w):
    sh = _w_in_local_to_shards(w)
    return jnp.concatenate([sh[j] for j in range(N_CHIPS)], axis=1)


def _size(shp):
    return functools.reduce(lambda a, b: a * b, shp, 1)


def _lane_dense(shp):
    return len(shp) >= 2 and shp[-1] == LANES and _size(shp) % (HALO * LANES) == 0


def _pack_rows(shapes):
    loose = sum(_size(shp) for shp in shapes if not _lane_dense(shp))
    return sum(_size(shp) // LANES for shp in shapes if _lane_dense(shp)) + -(-loose // (HALO * LANES)) * HALO


def _pack(arrs, rows):
    del rows
    parts = [a.reshape(-1, LANES).astype(F32) for a in arrs if _lane_dense(a.shape)]
    loose = [a.reshape(-1).astype(F32) for a in arrs if not _lane_dense(a.shape)]
    if loose:
        flat = jnp.concatenate(loose)
        n = -(-flat.shape[0] // (HALO * LANES)) * HALO * LANES
        parts.append(jnp.pad(flat, (0, n - flat.shape[0])).reshape(-1, LANES))
    return parts[0] if len(parts) == 1 else jnp.concatenate(parts, axis=0)


def _unpack(buf, shapes):
    out, row = {}, 0
    for i, shp in enumerate(shapes):
        if _lane_dense(shp):
            n = _size(shp) // LANES
            out[i] = buf[row:row + n].reshape(shp)
            row += n
    flat, at = buf[row:].reshape(-1), 0
    for i, shp in enumerate(shapes):
        if not _lane_dense(shp):
            out[i] = flat[at:at + _size(shp)].reshape(shp)
            at += _size(shp)
    return [out[i] for i in range(len(shapes))]


def _assemble_weights(g_in, g_out, g_up, g_down, small_sharded, replicated):
    w = dict(replicated)
    w["w_in"] = _w_in_shards_to_local(g_in)
    w["w_out"] = g_out.reshape(-1, g_out.shape[-1])
    w["w_up"] = g_up
    w["w_down"] = g_down.reshape(-1, g_down.shape[-1])
    for name, v in small_sharded.items():
        w[name] = jnp.concatenate([v[j] for j in range(N_CHIPS)], axis=1)
    return w


def _full_weights_from_global(weights):
    shard = lambda a, axis: jnp.stack(jnp.split(a, N_CHIPS, axis=axis))
    rep = {n: weights[n].reshape(1, -1) if weights[n].ndim <= 2 and n != "b_gate_m" else weights[n]
           for n in SMALL_REPLICATED}
    rep["b_gate_m"] = weights["b_gate_m"].reshape(1, -1)
    return _assemble_weights(shard(weights["w_in"], 1).astype(BF16), shard(weights["w_out"], 0).astype(BF16),
                             shard(weights["w_up"], 1).astype(BF16), shard(weights["w_down"], 0).astype(BF16),
                             {n: shard(weights[n], 1) for n in SMALL_SHARDED}, rep)


def _grads_to_global(grads):
    g = dict(grads)
    g["w_in"] = _w_in_to_global(grads["w_in"])
    g["w_up"] = jnp.concatenate([grads["w_up"][j] for j in range(N_CHIPS)], axis=1)
    return g


def _place():
    x, y, c = lax.axis_index("x"), lax.axis_index("y"), lax.axis_index("c")
    chips = [(1 - x, y), (x, 1 - y), (1 - x, 1 - y)]
    return x, y, c, 2 * x + y, chips


def _half_rows(n_rows, which):
    half = n_rows // 2
    return pl.ds(pl.multiple_of(which * half, 16), half)


def _rcopy(src, dst, send_sem, recv_sem, to):
    return pltpu.make_async_remote_copy(src_ref=src, dst_ref=dst, send_sem=send_sem, recv_sem=recv_sem,
                                        device_id=to, device_id_type=MESH)


HBM_SPEC = pl.BlockSpec(memory_space=pltpu.HBM)
SEM_SPEC = pl.BlockSpec(memory_space=pltpu.SEMAPHORE)
TOKEN_SHAPE = (8, LANES)


def _split_call(name, bufs, sems_in, sems_out_shapes, body_fn, after=None):
    nb, ni, no = len(bufs), len(sems_in), len(sems_out_shapes)
    after = [] if after is None else list(after) if isinstance(after, (list, tuple)) else [after]

    def body(*refs):
        buf_refs = refs[:nb]
        sem_in_refs = refs[nb:nb + ni]
        outs = refs[nb + ni + len(after):]
        sem_out_refs = outs[:no]
        token_ref = outs[no + nb]
        body_fn(buf_refs, sem_in_refs, sem_out_refs)
        token_ref[...] = jnp.zeros_like(token_ref)

    out_shape = ([pltpu.SemaphoreType.DMA(shp) for shp in sems_out_shapes]
                 + [pltpu.HBM(b.shape, b.dtype) for b in bufs] + [jax.ShapeDtypeStruct(TOKEN_SHAPE, F32)])
    res = pl.pallas_call(
        body, name=name, out_shape=out_shape,
        in_specs=[HBM_SPEC] * nb + [SEM_SPEC] * ni + [ANY] * len(after),
        out_specs=[SEM_SPEC] * no + [HBM_SPEC] * nb + [pl.BlockSpec(memory_space=pltpu.VMEM)],
        input_output_aliases={i: no + i for i in range(nb)},
        compiler_params=pltpu.CompilerParams(has_side_effects=pltpu.SideEffectType.DATAFLOW_SIDE_EFFECTING),
    )(*[pltpu.with_memory_space_constraint(b, pltpu.HBM) for b in bufs], *sems_in, *after)
    return list(res[:no]), list(res[no:no + nb]), res[no + nb]


def _place_own_shard(name, idx, shard, after=None):
    rows, cols = shard.shape
    tr = _row_tile(rows)

    def body(idx_ref, s_ref, *rest):
        rest[-1][...] = s_ref[...].astype(BF16)

    return pl.pallas_call(
        body, name=name, out_shape=jax.ShapeDtypeStruct((N_CHIPS, rows, cols), BF16),
        grid_spec=pltpu.PrefetchScalarGridSpec(
            num_scalar_prefetch=1, grid=(rows // tr,),
            in_specs=[pl.BlockSpec((tr, cols), lambda i, s: (i, 0))] + ([] if after is None else [ANY]),
            out_specs=pl.BlockSpec((None, tr, cols), lambda i, s: (s[1], i, 0))),
        compiler_params=_params(("parallel",)),
    )(idx, shard, *(() if after is None else (after,)))


GATHER_GROUPS = ((0, 4), (1,), (2,), (3,))


def _gather_start(name, lands, groups, after=None):
    members = [w for g in groups for w in GATHER_GROUPS[g]]

    def starts(bufs, _, sems):
        x, y, c, me, chips = _place()
        for gi, g in enumerate(groups):
            for pos, w in enumerate(GATHER_GROUPS[g]):
                buf = bufs[members.index(w)]
                part = buf.at[me] if w == 4 else buf.at[me, _half_rows(buf.shape[1], c)]
                for k, chip in enumerate(chips):
                    _rcopy(part, part, sems[2 * gi].at[3 * pos + k], sems[2 * gi + 1].at[3 * pos + k],
                           (*chip, c)).start()

    shapes = []
    for g in groups:
        shapes += [(3 * len(GATHER_GROUPS[g]),)] * 2
    sems, bufs, token = _split_call(name, [lands[w] for w in members], [], shapes, starts, after=after)
    return ({g: (sems[2 * gi], sems[2 * gi + 1]) for gi, g in enumerate(groups)},
            dict(zip(members, bufs)), token)


def _gather_mid(grp, lands, sems, after):
    members = GATHER_GROUPS[grp]
    big = [w for w in members if w != 4]

    def mid(bufs, sems_in, sems_out):
        x, y, c, me, chips = _place()
        send_sems, recv_sems = sems_in
        for pos, w in enumerate(members):
            for k, chip in enumerate(chips):
                cid = 2 * chip[0] + chip[1]
                buf = bufs[pos]
                mine = buf.at[me] if w == 4 else buf.at[me, _half_rows(buf.shape[1], c)]
                theirs = buf.at[cid] if w == 4 else buf.at[cid, _half_rows(buf.shape[1], c)]
                arrival = _rcopy(mine, theirs, send_sems.at[3 * pos + k], recv_sems.at[3 * pos + k], (*chip, c))
                arrival.wait_recv()
                arrival.wait_send()
                if w != 4:
                    _rcopy(theirs, theirs, sems_out[0].at[3 * big.index(w) + k],
                           sems_out[1].at[3 * big.index(w) + k], (x, y, 1 - c)).start()

    new_sems, bufs, token = _split_call(f"gather_mid_{grp}", [lands[w] for w in members], list(sems),
                                        [(3 * len(big),), (3 * len(big),)], mid, after=after)
    return new_sems, bufs, token


def _gather_end(grp, bufs, sems, after):
    members = GATHER_GROUPS[grp]
    big = [w for w in members if w != 4]

    def end(refs, sems_in, _):
        x, y, c, me, chips = _place()
        send_sems, recv_sems = sems_in
        for pos, w in enumerate(members):
            if w == 4:
                continue
            for k, chip in enumerate(chips):
                cid = 2 * chip[0] + chip[1]
                buf = refs[pos]
                sent = buf.at[cid, _half_rows(buf.shape[1], c)]
                landed = buf.at[cid, _half_rows(buf.shape[1], 1 - c)]
                fwd = _rcopy(sent, landed, send_sems.at[3 * big.index(w) + k], recv_sems.at[3 * big.index(w) + k],
                             (x, y, 1 - c))
                fwd.wait_recv()
                fwd.wait_send()

    _, bufs, token = _split_call(f"gather_end_{grp}", bufs, list(sems), [], end, after=after)
    return bufs, token


def _pair_start(name, grads, extra=None):
    n = len(grads)
    bufs = list(grads) + [lax.empty((g.shape[0], g.shape[1] // 2, g.shape[2]), g.dtype) for g in grads]
    if extra is not None:
        bufs += [extra, lax.empty(extra.shape, extra.dtype)]

    def starts(refs, _, sems):
        x, y, c, _, _ = _place()
        for w in range(n):
            other = _half_rows(refs[w].shape[1], 1 - c)
            _rcopy(refs[w].at[:, other], refs[n + w], sems[0].at[w], sems[1].at[w], (x, y, 1 - c)).start()
        if extra is not None:
            _rcopy(refs[2 * n], refs[2 * n + 1], sems[0].at[n], sems[1].at[n], (x, y, 1 - c)).start()

    count = n + (extra is not None)
    return _split_call(name, bufs, [], [(count,), (count,)], starts)


def _pair_wait(name, n, bufs, sems, after):
    has_extra = len(bufs) > 2 * n

    def waits(refs, sems_in, _):
        x, y, c, _, _ = _place()
        for w in range(n):
            other = _half_rows(refs[w].shape[1], 1 - c)
            cp = _rcopy(refs[w].at[:, other], refs[n + w], sems_in[0].at[w], sems_in[1].at[w], (x, y, 1 - c))
            cp.wait_recv()
            cp.wait_send()
        if has_extra:
            cp = _rcopy(refs[2 * n], refs[2 * n + 1], sems_in[0].at[n], sems_in[1].at[n], (x, y, 1 - c))
            cp.wait_recv()
            cp.wait_send()

    _, bufs, token = _split_call(name, bufs, list(sems), [], waits, after=after)
    return bufs, token


def _chip_start(name, partials, small=None):
    n = len(partials)
    bufs = list(partials) + [lax.empty(p.shape, p.dtype) for p in partials] + ([] if small is None else [small])

    def starts(refs, _, sems):
        _, _, c, me, chips = _place()
        for w in range(n):
            for k, chip in enumerate(chips):
                cid = 2 * chip[0] + chip[1]
                _rcopy(refs[w].at[cid], refs[n + w].at[me], sems[0].at[3 * w + k], sems[1].at[3 * w + k],
                       (*chip, c)).start()
        if small is not None:
            for k, chip in enumerate(chips):
                _rcopy(refs[2 * n].at[me], refs[2 * n].at[me], sems[0].at[3 * n + k], sems[1].at[3 * n + k],
                       (*chip, c)).start()

    count = 3 * (n + (small is not None))
    return _split_call(name, bufs, [], [(count,), (count,)], starts)


def _chip_wait(name, n, bufs, sems, after):
    has_small = len(bufs) > 2 * n

    def waits(refs, sems_in, _):
        _, _, c, me, chips = _place()
        for w in range(n):
            for k, chip in enumerate(chips):
                cid = 2 * chip[0] + chip[1]
                cp = _rcopy(refs[w].at[cid], refs[n + w].at[cid], sems_in[0].at[3 * w + k],
                            sems_in[1].at[3 * w + k], (*chip, c))
                cp.wait_recv()
                cp.wait_send()
        if has_small:
            for k, chip in enumerate(chips):
                cid = 2 * chip[0] + chip[1]
                cp = _rcopy(refs[2 * n].at[me], refs[2 * n].at[cid], sems_in[0].at[3 * n + k],
                            sems_in[1].at[3 * n + k], (*chip, c))
                cp.wait_recv()
                cp.wait_send()

    _, bufs, token = _split_call(name, bufs, list(sems), [], waits, after=after)
    return bufs, token


def _small_pair_sum(idx, own, recv):
    rows = own.shape[0]

    def body(idx_ref, a_ref, b_ref, o_ref):
        o_ref[...] = a_ref[...] + b_ref[...]

    blk = pl.BlockSpec((rows, LANES), lambda i, s: (0, 0))
    return pl.pallas_call(
        body, name="small_pair_sum", out_shape=jax.ShapeDtypeStruct((N_CHIPS, rows, LANES), F32),
        grid_spec=pltpu.PrefetchScalarGridSpec(
            num_scalar_prefetch=1, grid=(1,), in_specs=[blk, blk],
            out_specs=pl.BlockSpec((None, rows, LANES), lambda i, s: (s[1], 0, 0))),
        compiler_params=_params(("arbitrary",)),
    )(idx, own, recv)


def _pair_share(name, shards, late=None):
    nb = len(shards)
    nl = 0 if late is None else 1

    def body(*refs):
        srcs = refs[:nb]
        dsts = refs[nb + nl:2 * nb + nl]
        send_sems, recv_sems = refs[2 * nb + 2 * nl:2 * nb + 2 * nl + 2]
        x, y, c, _, _ = _place()
        sibling = (x, y, 1 - c)
        sends = []
        for w in range(nb):
            mine = _half_rows(dsts[w].shape[0], c)
            sends.append(_rcopy(srcs[w].at[mine], dsts[w].at[mine], send_sems.at[w], recv_sems.at[w], sibling))
        if nl:
            late_ref, late_out = refs[nb], refs[2 * nb + 1]
            late_send, late_recv, local_sem = refs[2 * nb + 4:]
            my_id = 4 * x + 2 * y + c
            peer = lambda r: (1 - x if r & 4 else x, 1 - y if r & 2 else y, 1 - c if r & 1 else c)
            local = pltpu.make_async_copy(late_ref, late_out.at[my_id], local_sem)
            local.start()
            for r in range(1, N_DEV):
                sends.append(_rcopy(late_ref, late_out.at[my_id], late_send.at[r - 1], late_recv.at[r - 1],
                                    peer(r)))
        for cp in sends:
            cp.start()
        for w in range(nb):
            other = _half_rows(dsts[w].shape[0], 1 - c)
            _rcopy(srcs[w].at[other], dsts[w].at[other], send_sems.at[w], recv_sems.at[w], sibling).wait_recv()
        if nl:
            for r in range(1, N_DEV):
                frm = peer(r)
                _rcopy(late_ref, late_out.at[4 * frm[0] + 2 * frm[1] + frm[2]], late_send.at[r - 1],
                       late_recv.at[r - 1], frm).wait_recv()
        for cp in sends:
            cp.wait_send()
        if nl:
            local.wait()

    out_shape = [jax.ShapeDtypeStruct(h.shape, h.dtype) for h in shards]
    scratch = [pltpu.SemaphoreType.DMA((nb,)), pltpu.SemaphoreType.DMA((nb,))]
    if nl:
        out_shape.append(jax.ShapeDtypeStruct((N_DEV,) + late.shape, late.dtype))
        scratch += [pltpu.SemaphoreType.DMA((N_DEV - 1,)), pltpu.SemaphoreType.DMA((N_DEV - 1,)),
                    pltpu.SemaphoreType.DMA(())]
    return pl.pallas_call(
        body, name=name, out_shape=out_shape,
        in_specs=[ANY] * (nb + nl), out_specs=[ANY] * (nb + nl), scratch_shapes=scratch,
        input_output_aliases={w: w for w in range(nb)},
    )(*shards, *(() if late is None else (late,)))


def _row_tile(rows):
    return _pick(rows, 128, 64, 16, 8)


def _pair_sum(name, idx, grad, recv):
    n, half, cols = recv.shape
    tr = _row_tile(half)
    nrb = half // tr

    def body(idx_ref, g_ref, r_ref, o_ref):
        o_ref[...] = (g_ref[...] + r_ref[...]).astype(BF16)

    return pl.pallas_call(
        body, name=name, out_shape=jax.ShapeDtypeStruct(recv.shape, BF16),
        grid_spec=pltpu.PrefetchScalarGridSpec(
            num_scalar_prefetch=1, grid=(n - 1, nrb),
            in_specs=[pl.BlockSpec((None, tr, cols), lambda j, i, s: (s[2 + j], s[0] * nrb + i, 0)),
                      pl.BlockSpec((None, tr, cols), lambda j, i, s: (s[2 + j], i, 0))],
            out_specs=pl.BlockSpec((None, tr, cols), lambda j, i, s: (s[2 + j], i, 0))),
        compiler_params=_params(("parallel", "parallel")),
    )(idx, grad, recv)


def _final_sum(name, idx, grad, recv, chip_sums):
    _, half, cols = recv.shape
    tr = _row_tile(half)
    nrb = half // tr

    def body(idx_ref, g_ref, r_ref, p1_ref, p2_ref, p3_ref, o_ref):
        acc = g_ref[...] + r_ref[...]
        for p_ref in (p1_ref, p2_ref, p3_ref):
            acc = acc + p_ref[...].astype(F32)
        o_ref[...] = acc

    slot = lambda which: pl.BlockSpec((None, tr, cols), lambda i, s, which=which: (s[which], i, 0))
    return pl.pallas_call(
        body, name=name, out_shape=jax.ShapeDtypeStruct((2 * half, cols), F32),
        grid_spec=pltpu.PrefetchScalarGridSpec(
            num_scalar_prefetch=1, grid=(nrb,),
            in_specs=[pl.BlockSpec((None, tr, cols), lambda i, s: (s[1], s[0] * nrb + i, 0)),
                      slot(1), slot(2), slot(3), slot(4)],
            out_specs=pl.BlockSpec((tr, cols), lambda i, s: (s[0] * nrb + i, 0))),
        compiler_params=_params(("parallel",)),
    )(idx, grad, recv, chip_sums, chip_sums, chip_sums)


def _pair_sum_all(name, idx, grad, recv):
    _, half, cols = recv.shape
    tr = _row_tile(half)
    nrb = half // tr

    def body(idx_ref, g_ref, r_ref, o_ref):
        o_ref[...] = (g_ref[...] + r_ref[...]).astype(BF16)

    return pl.pallas_call(
        body, name=name, out_shape=jax.ShapeDtypeStruct((half, cols), BF16),
        grid_spec=pltpu.PrefetchScalarGridSpec(
            num_scalar_prefetch=1, grid=(nrb,),
            in_specs=[pl.BlockSpec((None, tr, cols), lambda i, s: (0, s[0] * nrb + i, 0)),
                      pl.BlockSpec((None, tr, cols), lambda i, s: (0, i, 0))],
            out_specs=pl.BlockSpec((tr, cols), lambda i, s: (i, 0))),
        compiler_params=_params(("parallel",)),
    )(idx, grad, recv)


def _final_sum_bf16(name, idx, partial, chip_sums):
    _, half, cols = partial.shape
    tr = _row_tile(half)
    nrb = half // tr

    def body(idx_ref, p0_ref, p1_ref, p2_ref, p3_ref, o_ref):
        acc = p0_ref[...].astype(F32)
        for p_ref in (p1_ref, p2_ref, p3_ref):
            acc = acc + p_ref[...].astype(F32)
        o_ref[...] = acc

    slot = lambda which: pl.BlockSpec((None, tr, cols), lambda i, s, which=which: (s[which], i, 0))
    return pl.pallas_call(
        body, name=name, out_shape=jax.ShapeDtypeStruct((2 * half, cols), F32),
        grid_spec=pltpu.PrefetchScalarGridSpec(
            num_scalar_prefetch=1, grid=(nrb,),
            in_specs=[slot(1), slot(2), slot(3), slot(4)],
            out_specs=pl.BlockSpec((tr, cols), lambda i, s: (s[0] * nrb + i, 0))),
        compiler_params=_params(("parallel",)),
    )(idx, partial, chip_sums, chip_sums, chip_sums)


def _small_sum(name, packs):
    n, rows, _ = packs.shape

    def body(p_ref, o_ref):
        acc = p_ref[0]
        for k in range(1, n):
            acc = acc + p_ref[k]
        o_ref[...] = acc

    return pl.pallas_call(
        body, name=name, out_shape=jax.ShapeDtypeStruct((rows, LANES), F32),
        in_specs=[pl.BlockSpec(memory_space=pltpu.VMEM)], out_specs=pl.BlockSpec(memory_space=pltpu.VMEM),
        compiler_params=pltpu.CompilerParams(vmem_limit_bytes=VMEM_LIMIT),
    )(packs)


def _adamw_math(w, g, m, v):
    m_new = ADAM_B1 * m + (1.0 - ADAM_B1) * g
    v_new = ADAM_B2 * v + (1.0 - ADAM_B2) * (g * g)
    m_hat = m_new / (1.0 - ADAM_B1 ** ADAM_STEP)
    v_hat = v_new / (1.0 - ADAM_B2 ** ADAM_STEP)
    return -ADAM_LR * (m_hat / (jnp.sqrt(v_hat) + ADAM_EPS) + ADAM_WD * w), m_new, v_new


def _adamw_many(name, ws, gs, ms, vs):
    n = len(ws)

    def body(*refs):
        for i in range(n):
            d, m_new, v_new = _adamw_math(refs[i][...], refs[n + i][...], refs[2 * n + i][...],
                                          refs[3 * n + i][...])
            refs[4 * n + i][...] = d
            refs[5 * n + i][...] = m_new
            refs[6 * n + i][...] = v_new

    vmem = pl.BlockSpec(memory_space=pltpu.VMEM)
    res = pl.pallas_call(
        body, name=name, in_specs=[vmem] * (4 * n), out_specs=[vmem] * (3 * n),
        out_shape=[jax.ShapeDtypeStruct(w.shape, F32) for w in ws] * 3,
        compiler_params=pltpu.CompilerParams(vmem_limit_bytes=VMEM_LIMIT),
    )(*ws, *gs, *ms, *vs)
    return res[:n], res[n:2 * n], res[2 * n:]


def _adamw(name, w, g, m, v):
    rows, cols = w.shape
    tr = rows if rows * cols * 4 <= (2 << 20) else _row_tile(rows)

    def body(w_ref, g_ref, m_ref, v_ref, g_out_ref, d_ref, nm_ref, nv_ref):
        gv = g_ref[...]
        g_out_ref[...] = gv
        d_ref[...], nm_ref[...], nv_ref[...] = _adamw_math(w_ref[...], gv, m_ref[...], v_ref[...])

    blk = pl.BlockSpec((tr, cols), lambda i: (i, 0))
    sds = jax.ShapeDtypeStruct((rows, cols), F32)
    return pl.pallas_call(
        body, name=name, grid=(rows // tr,), in_specs=[blk] * 4, out_specs=[blk] * 4, out_shape=[sds] * 4,
        compiler_params=_params(("parallel",)),
    )(w, g, m, v)


def _train_step(x, target, W, M, V):
    xi, yi, ci = lax.axis_index("x"), lax.axis_index("y"), lax.axis_index("c")
    me = 2 * xi + yi
    big = {n: W[n][0] for n in BIG}
    big_m = {n: M[n][0] for n in BIG}
    big_v = {n: V[n][0] for n in BIG}

    others = [jnp.where(jnp.int32(i) >= me, i + 1, i) for i in range(N_CHIPS - 1)]
    idx = jnp.stack([ci, me] + others).astype(jnp.int32)

    sharded_shapes = [W[n].shape[1:] for n in SMALL_SHARDED]
    small_pack = _pack([W[n][0] for n in SMALL_SHARDED], _pack_rows(sharded_shapes))
    small_land = lax.dynamic_update_slice(jnp.zeros((N_CHIPS,) + small_pack.shape, F32), small_pack[None],
                                          (me, 0, 0))
    replicated = {n: (W[n].reshape(1, -1) if W[n].ndim <= 2 else W[n][0]) for n in SMALL_REPLICATED}

    early = ("w_out", "w_up", "w_down")
    small_late = "norm_mix_g"
    small_early = tuple(n for n in SMALL if n != small_late)
    global_shape = lambda n: ((W[n].shape[1], W[n].shape[2] * N_CHIPS) if n in SMALL_SHARDED else
                              tuple(W[n].shape) if W[n].ndim == 1 else tuple(W[n].shape[1:]))
    small_shapes = [global_shape(n) for n in small_early]

    def shard_major(n, g):
        if n == "w_in":
            return _w_in_local_to_shards(g)
        return g if g.ndim == 3 else g.reshape((N_CHIPS, -1) + g.shape[1:])

    class _SplitComm:
        def reduce_early(self, grads):
            self.e_sems, self.e_bufs, token = _pair_start("pair_start_early",
                                                          [shard_major(n, grads[n]) for n in early])
            return token

        def reduce_early_mid(self, after):
            n = len(early)
            bufs, _ = _pair_wait("pair_wait_early", n, self.e_bufs, self.e_sems, after)
            self.e_grads, self.e_recv = bufs[:n], bufs[n:2 * n]
            partial = [_pair_sum(f"pair_sum_{nm}", idx, g, r) for nm, g, r in zip(early, self.e_grads, self.e_recv)]
            self.e_sems, self.e_bufs, token = _chip_start("chip_start_early", partial)
            return token

        def reduce_late(self, grads):
            pack = _pack([grads[n] for n in small_early], _pack_rows(small_shapes))
            self.l_sems, self.l_bufs, token = _pair_start("pair_start_late", [grads["w_in"][None]], extra=pack)
            return token

        def reduce_late_mid(self, after):
            bufs, _ = _pair_wait("pair_wait_late", 1, self.l_bufs, self.l_sems, after)
            partial = _w_in_local_to_shards(_pair_sum_all("pair_sum_w_in", idx, bufs[0], bufs[1]))
            self.l_sems, self.l_bufs, token = _chip_start("chip_start_late", [partial],
                                                          small=_small_pair_sum(idx, bufs[2], bufs[3]))
            return token

        def finish_early(self, after):
            n = len(early)
            bufs, _ = _chip_wait("chip_wait_early", n, self.e_bufs, self.e_sems, after)
            halves = [_final_sum(f"final_sum_{nm}", idx, g, r, p)
                      for nm, g, r, p in zip(early, self.e_grads, self.e_recv, bufs[n:2 * n])]
            return dict(zip(early, _pair_share("pair_share_early", halves)))

        def finish_late(self, after, late):
            bufs, _ = _chip_wait("chip_wait_late", 1, self.l_bufs, self.l_sems, after)
            half = _final_sum_bf16("final_sum_w_in", idx, bufs[0], bufs[1])
            small = dict(zip(small_early, _unpack(_small_sum("small_sum", bufs[2]), small_shapes)))
            whole, late_all = _pair_share("pair_share_late", [half], late)
            return whole, small, _small_sum("late_sum", late_all)

        def begin(self):
            first = {0: _place_own_shard("place_w_in", idx, big["w_in"]), 4: small_land}
            self.sems, self.lands, token = _gather_start("gather_start_0", first, (0,))
            rest = {i: _place_own_shard(f"place_{BIG[i]}", idx, big[BIG[i]], after=token) for i in (1, 2, 3)}
            sems, lands, token = _gather_start("gather_start_1", rest, (1, 2, 3), after=token)
            self.sems.update(sems)
            self.lands.update(lands)
            return token

        def mid(self, grp, after):
            if grp == 0:
                after = [after, big_m["w_in"], big_v["w_in"]]
            self.pending = _gather_mid(grp, self.lands, self.sems[grp], after)
            return self.pending[2]

        def end(self, grp, after):
            sems, bufs, _ = self.pending
            bufs, _ = _gather_end(grp, bufs, sems, after)
            if grp == 0:
                per_chip = [_unpack(bufs[1][j], sharded_shapes) for j in range(N_CHIPS)]
                out = {n: jnp.concatenate([per_chip[j][i] for j in range(N_CHIPS)], axis=1)
                       for i, n in enumerate(SMALL_SHARDED)}
                out["w_in"] = _w_in_shards_to_local(bufs[0])
                return out
            if grp == 2:
                return {"w_up": bufs[0]}
            return {("w_out" if grp == 1 else "w_down"): bufs[0].reshape(-1, bufs[0].shape[-1])}

    comm = _SplitComm()
    loss, grad_x, grads = _local_step(x[0], target[0], replicated, comm)
    loss = lax.psum(loss[0, 0], ("x", "y", "c"))
    out_g, out_d, out_m, out_v = {}, {}, {}, {}

    def update_big(n, grad):
        g, d, nm, nv = _adamw(f"adamw_{n}", big[n], grad, big_m[n], big_v[n])
        out_g[n], out_d[n], out_m[n], out_v[n] = g[None], d[None], nm[None], nv[None]
        return d

    early_grads = comm.finish_early(grad_x)
    done = [update_big(n, early_grads[n]) for n in early]
    late = _pack([grads[small_late]], _pack_rows([global_shape(small_late)]))
    w_in_grad, small_grads, late_sum = comm.finish_late(done, late)
    update_big("w_in", w_in_grad)
    small_grads[small_late] = _unpack(late_sum, [global_shape(small_late)])[0]
    for n in SMALL_SHARDED:
        width = W[n].shape[2]
        small_grads[n] = lax.dynamic_slice_in_dim(small_grads[n], me * width, width, axis=1)

    for n in SMALL:
        out_g[n] = small_grads[n].reshape(W[n].shape)
    two_d = lambda a: a.reshape(1, -1) if a.ndim == 1 else a
    results = _adamw_many("adamw_small", *[[two_d(src[n]) for n in SMALL] for src in (W, out_g, M, V)])
    for dst, arrs in zip((out_d, out_m, out_v), results):
        dst.update({n: a.reshape(W[n].shape) for n, a in zip(SMALL, arrs)})
    return (loss, grad_x[None], *[out_g[n] for n in WEIGHT_NAMES], *[out_d[n] for n in WEIGHT_NAMES],
            *[out_m[n] for n in WEIGHT_NAMES], *[out_v[n] for n in WEIGHT_NAMES])


def kernel(x, norm_mix_g, w_in, b_gate_m, mlstm_norm_g, lru_conv_w, lru_conv_b, lru_wa, lru_ba, lru_wx, lru_bx, lru_lambda, w_out, norm_ffn_g, w_up, ffn_conv_w, ffn_conv_b, w_down, norm_final_g, loss_target, m_norm_mix_g, m_w_in, m_b_gate_m, m_mlstm_norm_g, m_lru_conv_w, m_lru_conv_b, m_lru_wa, m_lru_ba, m_lru_wx, m_lru_bx, m_lru_lambda, m_w_out, m_norm_ffn_g, m_w_up, m_ffn_conv_w, m_ffn_conv_b, m_w_down, m_norm_final_g, v_norm_mix_g, v_w_in, v_b_gate_m, v_mlstm_norm_g, v_lru_conv_w, v_lru_conv_b, v_lru_wa, v_lru_ba, v_lru_wx, v_lru_bx, v_lru_lambda, v_w_out, v_norm_ffn_g, v_w_up, v_ffn_conv_w, v_ffn_conv_b, v_w_down, v_norm_final_g):
    W = dict(zip(WEIGHT_NAMES, (norm_mix_g, w_in, b_gate_m, mlstm_norm_g, lru_conv_w, lru_conv_b, lru_wa, lru_ba,
                                lru_wx, lru_bx, lru_lambda, w_out, norm_ffn_g, w_up, ffn_conv_w, ffn_conv_b,
                                w_down, norm_final_g)))
    M = dict(zip(WEIGHT_NAMES, (m_norm_mix_g, m_w_in, m_b_gate_m, m_mlstm_norm_g, m_lru_conv_w, m_lru_conv_b,
                                m_lru_wa, m_lru_ba, m_lru_wx, m_lru_bx, m_lru_lambda, m_w_out, m_norm_ffn_g,
                                m_w_up, m_ffn_conv_w, m_ffn_conv_b, m_w_down, m_norm_final_g)))
    V = dict(zip(WEIGHT_NAMES, (v_norm_mix_g, v_w_in, v_b_gate_m, v_mlstm_norm_g, v_lru_conv_w, v_lru_conv_b,
                                v_lru_wa, v_lru_ba, v_lru_wx, v_lru_bx, v_lru_lambda, v_w_out, v_norm_ffn_g,
                                v_w_up, v_ffn_conv_w, v_ffn_conv_b, v_w_down, v_norm_final_g)))
    return _train_step(x, loss_target, W, M, V)
```

```python
import functools

import jax
import jax.numpy as jnp
from jax import lax
from jax.experimental import pallas as pl
from jax.experimental.pallas import tpu as pltpu

F32 = jnp.float32
BF16 = jnp.bfloat16
MESH = pl.DeviceIdType.MESH

EPS = 1e-6
CHUNK = 512
HEADS = 4
HEAD_DIM = 256
D_MLSTM = HEADS * HEAD_DIM
LRU_BLOCKS = 8
LRU_BLOCK_DIM = 128
D_LRU = LRU_BLOCKS * LRU_BLOCK_DIM
LRU_C = 8.0
LRU_CONV = 4
FFN_CONV = 3
ADAM_LR = 0.001
ADAM_B1 = 0.9
ADAM_B2 = 0.999
ADAM_EPS = 1e-08
ADAM_WD = 0.01
ADAM_STEP = 10

N_CHIPS = 4
N_DEV = 8
LANES = 128
HALO = 8
PROJ_GATE_PAD = LANES
_QKVO = 4 * D_MLSTM
_N_GATES = 2 * HEADS
_PROJ_COLS = _QKVO + _N_GATES + 2 * D_LRU
_PROJ_PAD = _QKVO + 2 * D_LRU + PROJ_GATE_PAD
VMEM_LIMIT = 48 * 1024 * 1024
ANY = pl.BlockSpec(memory_space=pl.ANY)


def _params(sem, vmem=VMEM_LIMIT):
    return pltpu.CompilerParams(dimension_semantics=sem, vmem_limit_bytes=vmem)


def _matmul(name, a, b, grid, a_spec, b_spec, o_spec, out_sds, contract, res=None, res_spec=None, after=None):
    nk = grid[2]
    acc_shape = tuple(d for d in o_spec.block_shape if d is not None)

    def body(*refs):
        refs = list(refs)
        a_ref, b_ref = refs[:2]
        r_ref = refs[2] if res is not None else None
        o_ref = refs[-1] if nk == 1 else refs[-2]
        acc_ref = None if nk == 1 else refs[-1]
        k = pl.program_id(2)

        def part():
            return lax.dot_general(a_ref[...], b_ref[...], (contract, ((), ())), preferred_element_type=F32)

        def finish(r):
            if r_ref is not None:
                r = r_ref[...] + r
            o_ref[...] = r.astype(o_ref.dtype)

        if nk == 1:
            finish(part())
            return

        @pl.when(k == 0)
        def _():
            acc_ref[...] = part()

        @pl.when(jnp.logical_and(k > 0, k < nk - 1))
        def _():
            acc_ref[...] += part()

        @pl.when(k == nk - 1)
        def _():
            finish(acc_ref[...] + part())

    in_specs = [a_spec, b_spec] + ([] if res is None else [res_spec]) + ([] if after is None else [ANY])
    args = (a, b) + (() if res is None else (res,)) + (() if after is None else (after,))
    if after is not None:
        inner = body
        body = lambda *refs: inner(*refs[:len(in_specs) - 1], *refs[len(in_specs):])
    return pl.pallas_call(
        body, out_shape=out_sds, grid=grid, in_specs=in_specs, out_specs=o_spec,
        scratch_shapes=[] if nk == 1 else [pltpu.VMEM(acc_shape, F32)], name=name,
        compiler_params=_params(("parallel", "parallel", "arbitrary")),
    )(*args)


NN = ((1,), (0,))
NT = ((1,), (1,))
TN = ((0,), (0,))


def _mm_nn(name, a, b, tm, tn, tk, out_dtype=F32, res=None):
    m, k = a.shape
    n = b.shape[1]
    return _matmul(name, a, b, (m // tm, n // tn, k // tk),
                   pl.BlockSpec((tm, tk), lambda i, j, kk: (i, kk)),
                   pl.BlockSpec((tk, tn), lambda i, j, kk: (kk, j)),
                   pl.BlockSpec((tm, tn), lambda i, j, kk: (i, j)),
                   jax.ShapeDtypeStruct((m, n), out_dtype), NN,
                   res=res, res_spec=pl.BlockSpec((tm, tn), lambda i, j, kk: (i, j)))


def _mm_nt(name, a, b, tm, tn, tk, out_dtype=F32, res=None, after=None):
    m, k = a.shape
    n = b.shape[0]
    return _matmul(name, a, b, (m // tm, n // tn, k // tk),
                   pl.BlockSpec((tm, tk), lambda i, j, kk: (i, kk)),
                   pl.BlockSpec((tn, tk), lambda i, j, kk: (j, kk)),
                   pl.BlockSpec((tm, tn), lambda i, j, kk: (i, j)),
                   jax.ShapeDtypeStruct((m, n), out_dtype), NT,
                   res=res, res_spec=pl.BlockSpec((tm, tn), lambda i, j, kk: (i, j)), after=after)


def _mm_tn(name, a, b, tm, tn, tk, out_dtype=F32):
    k, m = a.shape
    n = b.shape[1]
    tk = min(tk, k)
    return _matmul(name, a, b, (m // tm, n // tn, k // tk),
                   pl.BlockSpec((tk, tm), lambda i, j, kk: (kk, i)),
                   pl.BlockSpec((tk, tn), lambda i, j, kk: (kk, j)),
                   pl.BlockSpec((tm, tn), lambda i, j, kk: (i, j)),
                   jax.ShapeDtypeStruct((m, n), out_dtype), TN)


def _up_shard(n):
    return 2 * (n % 2) + (n // 2) // 2, (n // 2) % 2


def _mm_up_fwd(name, a, wg_up, tm, tk):
    m, k = a.shape
    _, _, cols = wg_up.shape
    tn = cols // 2
    return _matmul(name, a, wg_up, (m // tm, 2 * N_CHIPS, k // tk),
                   pl.BlockSpec((tm, tk), lambda i, j, kk: (i, kk)),
                   pl.BlockSpec((None, tk, tn), lambda i, j, kk: (_up_shard(j)[0], kk, _up_shard(j)[1])),
                   pl.BlockSpec((tm, tn), lambda i, j, kk: (i, j)),
                   jax.ShapeDtypeStruct((m, 2 * N_CHIPS * tn), F32), NN)


def _mm_up_bwd_x(name, dgu, wg_up, tm, tn):
    m, _ = dgu.shape
    _, d, cols = wg_up.shape
    tk = cols // 2
    nk = N_CHIPS

    def body(a_ref, bg_ref, bu_ref, o_ref, acc_ref):
        k = pl.program_id(2)

        def part():
            dims = (NT, ((), ()))
            return (lax.dot_general(a_ref[:, :tk], bg_ref[...], dims, preferred_element_type=F32)
                    + lax.dot_general(a_ref[:, tk:], bu_ref[...], dims, preferred_element_type=F32))

        @pl.when(k == 0)
        def _():
            acc_ref[...] = part()

        @pl.when(jnp.logical_and(k > 0, k < nk - 1))
        def _():
            acc_ref[...] += part()

        @pl.when(k == nk - 1)
        def _():
            o_ref[...] = acc_ref[...] + part()

    wspec = lambda half: pl.BlockSpec(
        (None, tn, tk), lambda i, j, kk: (_up_shard(2 * kk + half)[0], j, _up_shard(2 * kk + half)[1]))
    return pl.pallas_call(
        body, name=name, grid=(m // tm, d // tn, nk), out_shape=jax.ShapeDtypeStruct((m, d), F32),
        in_specs=[pl.BlockSpec((tm, 2 * tk), lambda i, j, kk: (i, kk)), wspec(0), wspec(1)],
        out_specs=pl.BlockSpec((tm, tn), lambda i, j, kk: (i, j)),
        scratch_shapes=[pltpu.VMEM((tm, tn), F32)],
        compiler_params=_params(("parallel", "parallel", "arbitrary")),
    )(dgu, wg_up, wg_up)


def _mm_up_bwd_w(name, n2, dgu, tm, tk):
    s, d = n2.shape
    tk = min(tk, s)
    tn = dgu.shape[1] // (2 * N_CHIPS)
    return _matmul(name, n2, dgu, (d // tm, 2 * N_CHIPS, s // tk),
                   pl.BlockSpec((tk, tm), lambda i, j, kk: (kk, i)),
                   pl.BlockSpec((tk, tn), lambda i, j, kk: (kk, j)),
                   pl.BlockSpec((None, tm, tn), lambda i, j, kk: (_up_shard(j)[0], i, _up_shard(j)[1])),
                   jax.ShapeDtypeStruct((N_CHIPS, d, 2 * tn), F32), TN)


def _rmsnorm_fwd(name, x, g, tm=256):
    s, d = x.shape

    def body(x_ref, g_ref, n_ref, r_ref):
        xf = x_ref[...]
        r = lax.rsqrt(jnp.mean(xf * xf, axis=-1, keepdims=True) + EPS)
        n_ref[...] = ((xf * r) * g_ref[...]).astype(BF16)
        r_ref[...] = r

    return pl.pallas_call(
        body, grid=(s // tm,), name=name,
        in_specs=[pl.BlockSpec((tm, d), lambda i: (i, 0)), pl.BlockSpec((1, d), lambda i: (0, 0))],
        out_specs=[pl.BlockSpec((tm, d), lambda i: (i, 0)), pl.BlockSpec((tm, 1), lambda i: (i, 0))],
        out_shape=[jax.ShapeDtypeStruct((s, d), BF16), jax.ShapeDtypeStruct((s, 1), F32)],
        compiler_params=_params(("parallel",)),
    )(x, g)


def _rmsnorm_bwd(name, x, rstd, g, dn, dres, tm=256):
    s, d = x.shape

    def body(x_ref, r_ref, g_ref, dn_ref, dres_ref, dx_ref, dxb_ref, dg_ref):
        @pl.when(pl.program_id(0) == 0)
        def _():
            dg_ref[...] = jnp.zeros_like(dg_ref)

        r = r_ref[...]
        xhat = x_ref[...] * r
        dn_v = dn_ref[...]
        dxhat = dn_v * g_ref[...]
        dx = dres_ref[...] + r * (dxhat - xhat * jnp.mean(dxhat * xhat, axis=-1, keepdims=True))
        dx_ref[...] = dx
        dxb_ref[...] = dx.astype(BF16)
        dg_ref[...] += jnp.sum(dn_v * xhat, axis=0, keepdims=True)

    row = pl.BlockSpec((tm, d), lambda i: (i, 0))
    vec = pl.BlockSpec((1, d), lambda i: (0, 0))
    return pl.pallas_call(
        body, grid=(s // tm,), name=name,
        in_specs=[row, pl.BlockSpec((tm, 1), lambda i: (i, 0)), vec, row, row],
        out_specs=[row, row, vec],
        out_shape=[jax.ShapeDtypeStruct((s, d), F32), jax.ShapeDtypeStruct((s, d), BF16),
                   jax.ShapeDtypeStruct((1, d), F32)],
        compiler_params=_params(("arbitrary",)),
    )(x, rstd, g, dn, dres)


def _loss_head(name, x, g, target, tm=256):
    s, d = x.shape

    def body(x_ref, g_ref, t_ref, loss_ref, dx_ref, dxb_ref, dg_ref):
        @pl.when(pl.program_id(0) == 0)
        def _():
            dg_ref[...] = jnp.zeros_like(dg_ref)
            loss_ref[...] = jnp.zeros_like(loss_ref)

        xf = x_ref[...]
        gv = g_ref[...]
        r = lax.rsqrt(jnp.mean(xf * xf, axis=-1, keepdims=True) + EPS)
        xhat = xf * r
        err = xhat * gv - t_ref[...]
        loss_ref[...] += 0.5 * jnp.sum(jnp.mean(err * err, axis=-1, keepdims=True), axis=0, keepdims=True)
        dy = err * (1.0 / d)
        dxhat = dy * gv
        dx = r * (dxhat - xhat * jnp.mean(dxhat * xhat, axis=-1, keepdims=True))
        dx_ref[...] = dx
        dxb_ref[...] = dx.astype(BF16)
        dg_ref[...] += jnp.sum(dy * xhat, axis=0, keepdims=True)

    row = pl.BlockSpec((tm, d), lambda i: (i, 0))
    vec = pl.BlockSpec((1, d), lambda i: (0, 0))
    return pl.pallas_call(
        body, grid=(s // tm,), name=name,
        in_specs=[row, vec, row],
        out_specs=[pl.BlockSpec((1, 1), lambda i: (0, 0)), row, row, vec],
        out_shape=[jax.ShapeDtypeStruct((1, 1), F32), jax.ShapeDtypeStruct((s, d), F32),
                   jax.ShapeDtypeStruct((s, d), BF16), jax.ShapeDtypeStruct((1, d), F32)],
        compiler_params=_params(("arbitrary",)),
    )(x, g, target)


def _sigmoid(v):
    return 1.0 / (1.0 + jnp.exp(-v))


def _log_sigmoid(v):
    return jnp.minimum(v, 0.0) - jnp.log1p(jnp.exp(-jnp.abs(v)))


def _softplus(v):
    return jnp.maximum(v, 0.0) + jnp.log1p(jnp.exp(-jnp.abs(v)))


def _one_minus_exp(z):
    series = -z * (1.0 + z * (0.5 + z * (1.0 / 6.0 + z * (1.0 / 24.0 + z * (1.0 / 120.0)))))
    return jnp.where(z > -0.1, series, 1.0 - jnp.exp(z))


_GELU_K = 0.7978845608028654
_GELU_C = 0.044715


def _gelu(v):
    return 0.5 * v * (1.0 + jnp.tanh(_GELU_K * (v + _GELU_C * v * v * v)))


def _gelu_grad(v):
    t = jnp.tanh(_GELU_K * (v + _GELU_C * v * v * v))
    return 0.5 * (1.0 + t) + 0.5 * v * (1.0 - t * t) * _GELU_K * (1.0 + 3.0 * _GELU_C * v * v)


def _rows(shape):
    return lax.broadcasted_iota(jnp.int32, shape, 0)


def _cols(shape):
    return lax.broadcasted_iota(jnp.int32, shape, 1)


def _shift_down(v, prev, d):
    if d == 0:
        return v
    rolled = pltpu.roll(v, d, axis=0)
    head = jnp.where(_rows((HALO, v.shape[1])) >= d, rolled[:HALO], pltpu.roll(prev, d, axis=0))
    if v.shape[0] == HALO:
        return head
    return jnp.concatenate([head, rolled[HALO:]], axis=0)


def _shift_up(v, nxt, d):
    if d == 0:
        return v
    n = v.shape[0]
    rolled = pltpu.roll(v, n - d, axis=0)
    tail = jnp.where(_rows((HALO, v.shape[1])) < HALO - d, rolled[n - HALO:], pltpu.roll(nxt, HALO - d, axis=0))
    if n == HALO:
        return tail
    return jnp.concatenate([rolled[:n - HALO], tail], axis=0)


def _dot(a, b, contract):
    return lax.dot_general(a.astype(BF16), b.astype(BF16), (contract, ((), ())), preferred_element_type=F32)


def _mlstm_chunk_common(h, q_ref, k_ref, v_ref, gcol_ref, grow_ref, brow_ref, bcol_ref, m_prev):
    L = CHUNK
    sl = slice(h * HEAD_DIM, (h + 1) * HEAD_DIM)
    qh = q_ref[:, sl]
    kh = k_ref[:, sl]
    vh = v_ref[:, sl]
    qs = qh * (HEAD_DIM ** -0.5)
    gates = gcol_ref[...] + brow_ref[...]
    lane = _cols(gates.shape)
    ic = jnp.sum(jnp.where(lane == h, gates, 0.0), axis=1, keepdims=True)
    fc = jnp.sum(jnp.where(lane == HEADS + h, gates, 0.0), axis=1, keepdims=True)
    ir = grow_ref[h:h + 1, :] + bcol_ref[h:h + 1, :]
    fr = grow_ref[HEADS + h:HEADS + h + 1, :] + bcol_ref[HEADS + h:HEADS + h + 1, :]
    logf_c = _log_sigmoid(fc)
    logf_r = _log_sigmoid(fr)
    t_i = _rows((L, L))
    s_i = _cols((L, L))
    tri = t_i >= s_i
    b_c = jnp.sum(jnp.where(tri, logf_r, 0.0), axis=1, keepdims=True)
    b_r = jnp.sum(jnp.where(t_i <= s_i, logf_c, 0.0), axis=0, keepdims=True)
    btot = jnp.sum(logf_r, axis=1, keepdims=True)
    dmat = jnp.where(tri, b_c - b_r + ir, -jnp.inf)
    m_inter = b_c + m_prev
    m_t = jnp.maximum(m_inter, jnp.max(dmat, axis=1, keepdims=True))
    e_mat = jnp.exp(dmat - m_t)
    e_inter = jnp.exp(m_inter - m_t)
    wqk = _dot(qs, kh, NT) * e_mat
    w_end_r = btot - b_r + ir
    m_loc = jnp.max(w_end_r, axis=1, keepdims=True)
    e_end_c = jnp.exp(btot - b_c + ic - m_loc)
    m_new = jnp.maximum(btot + m_prev, m_loc)
    a_dec = jnp.exp(btot + m_prev - m_new)
    c_inj = jnp.exp(m_loc - m_new)
    return dict(qh=qh, kh=kh, vh=vh, qs=qs, fc=fc, tri=tri, t_i=t_i, s_i=s_i, m_t=m_t, e_mat=e_mat,
                e_inter=e_inter, wqk=wqk, e_end_c=e_end_c, m_new=m_new, a_dec=a_dec, c_inj=c_inj)


def _mlstm_fwd(proj, gates_t, bias_row, bias_col, head_g):
    s = proj.shape[0]
    nc = s // CHUNK
    L = CHUNK

    def body(q_ref, k_ref, v_ref, o_ref, gcol_ref, grow_ref, brow_ref, bcol_ref, hg_ref,
             out_ref, cprev_ref, nprev_ref, mprev_ref, c_scr, n_scr, m_scr):
        @pl.when(pl.program_id(0) == 0)
        def _():
            c_scr[...] = jnp.zeros_like(c_scr)
            n_scr[...] = jnp.zeros_like(n_scr)
            m_scr[...] = jnp.zeros_like(m_scr)

        for h in range(HEADS):
            sl = slice(h * HEAD_DIM, (h + 1) * HEAD_DIM)
            m_prev = m_scr[h:h + 1, 0:1]
            n_prev = n_scr[h:h + 1, :]
            c_prev = c_scr[h].astype(BF16)
            q = _mlstm_chunk_common(h, q_ref, k_ref, v_ref, gcol_ref, grow_ref, brow_ref, bcol_ref, m_prev)
            num = _dot(q["wqk"], q["vh"], NN) + q["e_inter"] * _dot(q["qs"], c_prev, NN)
            den = (jnp.sum(q["wqk"], axis=1, keepdims=True)
                   + q["e_inter"] * jnp.sum(q["qs"] * n_prev, axis=1, keepdims=True))
            hh = num / jnp.maximum(jnp.abs(den), jnp.exp(-q["m_t"]))
            hn = hh * lax.rsqrt(jnp.mean(hh * hh, axis=1, keepdims=True) + EPS) * hg_ref[h:h + 1, :]
            out_ref[:, sl] = (_sigmoid(o_ref[:, sl]) * hn).astype(BF16)
            cprev_ref[h] = c_prev
            nprev_ref[h:h + 1, :] = n_prev
            mprev_ref[h:h + 1, :] = jnp.broadcast_to(m_prev, (1, LANES))
            c_loc = _dot(q["kh"], q["e_end_c"] * q["vh"], TN)
            n_loc = jnp.sum(q["e_end_c"] * q["kh"], axis=0, keepdims=True)
            c_scr[h] = q["a_dec"] * c_scr[h] + q["c_inj"] * c_loc
            n_scr[h:h + 1, :] = q["a_dec"] * n_prev + q["c_inj"] * n_loc
            m_scr[h:h + 1, :] = jnp.broadcast_to(q["m_new"], (1, LANES))

    blk = lambda j: pl.BlockSpec((L, D_MLSTM), lambda c, j=j: (c, j))
    full = lambda shp: pl.BlockSpec(shp, lambda c: tuple(0 for _ in shp))
    return pl.pallas_call(
        body, grid=(nc,), name="mlstm_fwd",
        in_specs=[blk(0), blk(1), blk(2), blk(3),
                  pl.BlockSpec((L, LANES), lambda c: (c, (4 * D_MLSTM + 2 * D_LRU) // LANES)),
                  pl.BlockSpec((None, 2 * HEADS, L), lambda c: (c, 0, 0)),
                  full((1, LANES)), full((2 * HEADS, 1)), full((HEADS, HEAD_DIM))],
        out_specs=[pl.BlockSpec((L, D_MLSTM), lambda c: (c, 0)),
                   pl.BlockSpec((None, HEADS, HEAD_DIM, HEAD_DIM), lambda c: (c, 0, 0, 0)),
                   pl.BlockSpec((None, HEADS, HEAD_DIM), lambda c: (c, 0, 0)),
                   pl.BlockSpec((None, HEADS, LANES), lambda c: (c, 0, 0))],
        out_shape=[jax.ShapeDtypeStruct((s, D_MLSTM + D_LRU), BF16),
                   jax.ShapeDtypeStruct((nc, HEADS, HEAD_DIM, HEAD_DIM), BF16),
                   jax.ShapeDtypeStruct((nc, HEADS, HEAD_DIM), F32),
                   jax.ShapeDtypeStruct((nc, HEADS, LANES), F32)],
        scratch_shapes=[pltpu.VMEM((HEADS, HEAD_DIM, HEAD_DIM), F32), pltpu.VMEM((HEADS, HEAD_DIM), F32),
                        pltpu.VMEM((HEADS, LANES), F32)],
        compiler_params=_params(("arbitrary",)),
    )(proj, proj, proj, proj, proj, gates_t, bias_row, bias_col, head_g)


def _mlstm_bwd(proj, gates_t, bias_row, bias_col, head_g, cprev, nprev, mprev, dmix):
    s = proj.shape[0]
    nc = s // CHUNK
    L = CHUNK

    def body(q_ref, k_ref, v_ref, o_ref, gcol_ref, grow_ref, brow_ref, bcol_ref, hg_ref,
             cprev_ref, nprev_ref, mprev_ref, dmix_ref,
             dqkvo_ref, dgate_ref, dhg_ref, g_scr, gn_scr):
        @pl.when(pl.program_id(0) == 0)
        def _():
            g_scr[...] = jnp.zeros_like(g_scr)
            gn_scr[...] = jnp.zeros_like(gn_scr)
            dhg_ref[...] = jnp.zeros_like(dhg_ref)

        lane = _cols((L, LANES))
        dgate = jnp.zeros((L, LANES), F32)
        for h in range(HEADS):
            sl = slice(h * HEAD_DIM, (h + 1) * HEAD_DIM)
            m_prev = mprev_ref[h:h + 1, 0:1]
            n_prev = nprev_ref[h:h + 1, :]
            c_prev = cprev_ref[h]
            q = _mlstm_chunk_common(h, q_ref, k_ref, v_ref, gcol_ref, grow_ref, brow_ref, bcol_ref, m_prev)
            qh, kh, vh, qs, wqk, e_inter = q["qh"], q["kh"], q["vh"], q["qs"], q["wqk"], q["e_inter"]
            num_state = e_inter * _dot(qs, c_prev, NN)
            den_state = e_inter * jnp.sum(qs * n_prev, axis=1, keepdims=True)
            num = _dot(wqk, vh, NN) + num_state
            den = jnp.sum(wqk, axis=1, keepdims=True) + den_state
            floor = jnp.exp(-q["m_t"])
            denom = jnp.maximum(jnp.abs(den), floor)
            hh = num / denom
            rn = lax.rsqrt(jnp.mean(hh * hh, axis=1, keepdims=True) + EPS)
            hn_pre = hh * rn
            hg = hg_ref[h:h + 1, :]
            sg = _sigmoid(o_ref[:, sl])
            dout = dmix_ref[:, sl]
            d_o = dout * (hn_pre * hg) * sg * (1.0 - sg)
            dhn = dout * sg
            dhg_ref[h:h + 1, :] += jnp.sum(dhn * hn_pre, axis=0, keepdims=True)
            dhn_pre = dhn * hg
            dhh = rn * (dhn_pre - hn_pre * jnp.mean(dhn_pre * hn_pre, axis=1, keepdims=True))
            dnum = dhh / denom
            dden = jnp.where(jnp.abs(den) >= floor,
                             -jnp.sum(hh * dhh, axis=1, keepdims=True) / denom * jnp.sign(den), 0.0)
            dwqk = _dot(dnum, vh, NT) + dden
            dv = _dot(wqk, dnum, TN)
            dp = dwqk * q["e_mat"]
            dqs = _dot(dp, kh, NN) + e_inter * (_dot(dnum, c_prev, NT) + dden * n_prev)
            dk = _dot(dp, qs, TN)
            g_next = g_scr[h]
            gn_next = gn_scr[h:h + 1, :]
            w_state = q["e_end_c"] * q["c_inj"]
            dk_state = w_state * (_dot(vh, g_next, NT) + gn_next)
            dk = dk + dk_state
            dv = dv + w_state * _dot(kh, g_next, NN)
            dq = dqs * (HEAD_DIM ** -0.5)
            eye = q["t_i"] == q["s_i"]
            to_row = lambda col: jnp.sum(jnp.where(eye, col, 0.0), axis=0, keepdims=True)
            to_col = lambda row: jnp.sum(jnp.where(eye, row, 0.0), axis=1, keepdims=True)
            g_pair = dwqk * wqk
            rs_in = jnp.sum(g_pair, axis=1, keepdims=True)
            cs_in_r = jnp.sum(g_pair, axis=0, keepdims=True)
            rs_state = (jnp.sum(dnum * num_state, axis=1, keepdims=True) + dden * den_state)
            cs_state = jnp.sum(kh * dk_state, axis=1, keepdims=True)
            di_c = to_col(cs_in_r) + cs_state
            through = q["a_dec"] * (jnp.sum(jnp.sum(g_next * c_prev.astype(F32), axis=1, keepdims=True),
                                            axis=0, keepdims=True)
                                    + jnp.sum(gn_next * n_prev, axis=1, keepdims=True))
            ends_here = to_row(rs_in + rs_state) - cs_in_r
            da_c = (jnp.sum(jnp.where(q["s_i"] >= q["t_i"], ends_here, 0.0), axis=1, keepdims=True)
                    + jnp.sum(jnp.where(q["s_i"] < q["t_i"], to_row(cs_state), 0.0), axis=1, keepdims=True)
                    + through)
            df_c = da_c * _sigmoid(-q["fc"])
            dgate = dgate + jnp.where(lane == h, di_c, 0.0) + jnp.where(lane == HEADS + h, df_c, 0.0)
            dqkvo_ref[:, sl] = dq.astype(BF16)
            dqkvo_ref[:, D_MLSTM + h * HEAD_DIM:D_MLSTM + (h + 1) * HEAD_DIM] = dk.astype(BF16)
            dqkvo_ref[:, 2 * D_MLSTM + h * HEAD_DIM:2 * D_MLSTM + (h + 1) * HEAD_DIM] = dv.astype(BF16)
            dqkvo_ref[:, 3 * D_MLSTM + h * HEAD_DIM:3 * D_MLSTM + (h + 1) * HEAD_DIM] = d_o.astype(BF16)
            g_scr[h] = q["a_dec"] * g_next + _dot(e_inter * qs, dnum, TN)
            gn_scr[h:h + 1, :] = q["a_dec"] * gn_next + jnp.sum(e_inter * qs * dden, axis=0, keepdims=True)
        dgate_ref[...] = dgate

    rev = lambda c: nc - 1 - c
    blk = lambda j: pl.BlockSpec((L, D_MLSTM), lambda c, j=j: (rev(c), j))
    full = lambda shp: pl.BlockSpec(shp, lambda c: tuple(0 for _ in shp))
    return pl.pallas_call(
        body, grid=(nc,), name="mlstm_bwd",
        in_specs=[blk(0), blk(1), blk(2), blk(3),
                  pl.BlockSpec((L, LANES), lambda c: (rev(c), (4 * D_MLSTM + 2 * D_LRU) // LANES)),
                  pl.BlockSpec((None, 2 * HEADS, L), lambda c: (rev(c), 0, 0)),
                  full((1, LANES)), full((2 * HEADS, 1)), full((HEADS, HEAD_DIM)),
                  pl.BlockSpec((None, HEADS, HEAD_DIM, HEAD_DIM), lambda c: (rev(c), 0, 0, 0)),
                  pl.BlockSpec((None, HEADS, HEAD_DIM), lambda c: (rev(c), 0, 0)),
                  pl.BlockSpec((None, HEADS, LANES), lambda c: (rev(c), 0, 0)),
                  pl.BlockSpec((L, D_MLSTM), lambda c: (rev(c), 0))],
        out_specs=[pl.BlockSpec((L, 4 * D_MLSTM), lambda c: (rev(c), 0)),
                   pl.BlockSpec((L, LANES), lambda c: (rev(c), 0)),
                   full((HEADS, HEAD_DIM))],
        out_shape=[jax.ShapeDtypeStruct((s, _PROJ_PAD), BF16),
                   jax.ShapeDtypeStruct((s, LANES), F32),
                   jax.ShapeDtypeStruct((HEADS, HEAD_DIM), F32)],
        scratch_shapes=[pltpu.VMEM((HEADS, HEAD_DIM, HEAD_DIM), F32), pltpu.VMEM((HEADS, HEAD_DIM), F32)],
        compiler_params=_params(("arbitrary",)),
    )(proj, proj, proj, proj, proj, gates_t, bias_row, bias_col, head_g, cprev, nprev, mprev, dmix)


def _lru_gates(xc, wa_ref, wx_ref, ba, bx, lam):
    r = _sigmoid(_dot(xc, wa_ref[...], NN) + ba)
    ig = _sigmoid(_dot(xc, wx_ref[...], NN) + bx)
    sp = _softplus(-lam)
    log_a = (-LRU_C * r) * sp
    a = jnp.exp(log_a)
    mult = jnp.sqrt(_one_minus_exp(2.0 * log_a))
    return r, ig, sp, a, mult


def _lru_conv(xr, prev, w_ref, b):
    xc = b + _shift_down(xr, prev, 3) * w_ref[0:1, :]
    for j in range(1, LRU_CONV):
        xc = xc + _shift_down(xr, prev, LRU_CONV - 1 - j) * w_ref[j:j + 1, :]
    return xc


def _lru_fwd(proj, mix, conv_w, conv_b, wa, wx, ba, bx, lam, tt=1024):
    s = proj.shape[0]
    tt = min(tt, s)
    nt = s // tt
    B = LRU_BLOCK_DIM
    lru_col = 4 * D_MLSTM // B
    mix_col = D_MLSTM // B

    def body(xr_ref, gr_ref, cw_ref, cb_ref, wa_ref, wx_ref, ba_ref, bx_ref, lam_ref, mix_in_ref,
             out_ref, h_ref, prev_scr, hcar_scr):
        @pl.when(pl.program_id(1) == 0)
        def _():
            prev_scr[...] = jnp.zeros_like(prev_scr)
            hcar_scr[...] = jnp.zeros_like(hcar_scr)

        xr = xr_ref[...]
        xc = _lru_conv(xr, prev_scr[...], cw_ref, cb_ref[...])
        prev_scr[...] = xr[tt - HALO:, :]
        _, ig, _, a, mult = _lru_gates(xc, wa_ref, wx_ref, ba_ref[...], bx_ref[...], lam_ref[...])
        u = mult * (ig * xc)
        rows = _rows((tt, B))
        acc_a, acc_b = a, u
        d = 1
        while d < tt:
            if d < HALO:
                keep = rows >= d
                sh_a = jnp.where(keep, pltpu.roll(acc_a, d, axis=0), 1.0)
                sh_b = jnp.where(keep, pltpu.roll(acc_b, d, axis=0), 0.0)
            else:
                sh_a = jnp.concatenate([jnp.ones((d, B), F32), acc_a[:tt - d]], axis=0)
                sh_b = jnp.concatenate([jnp.zeros((d, B), F32), acc_b[:tt - d]], axis=0)
            acc_b = acc_a * sh_b + acc_b
            acc_a = acc_a * sh_a
            d *= 2
        hv = acc_b + acc_a * hcar_scr[0:1, :]
        hcar_scr[...] = jnp.broadcast_to(hv[tt - 1:tt, :], hcar_scr.shape)
        h_ref[...] = hv
        out_ref[...] = (hv * _gelu(gr_ref[...])).astype(BF16)

    chan = lambda rws: pl.BlockSpec((rws, B), lambda n, i: (0, n))
    return pl.pallas_call(
        body, grid=(LRU_BLOCKS, nt), name="lru_fwd",
        in_specs=[pl.BlockSpec((tt, B), lambda n, i: (i, lru_col + 2 * n)),
                  pl.BlockSpec((tt, B), lambda n, i: (i, lru_col + 2 * n + 1)),
                  chan(LRU_CONV), chan(1),
                  pl.BlockSpec((None, B, B), lambda n, i: (n, 0, 0)),
                  pl.BlockSpec((None, B, B), lambda n, i: (n, 0, 0)),
                  chan(1), chan(1), chan(1), ANY],
        out_specs=[pl.BlockSpec((tt, B), lambda n, i: (i, mix_col + n)), pl.BlockSpec((tt, B), lambda n, i: (i, n))],
        out_shape=[jax.ShapeDtypeStruct(mix.shape, BF16), jax.ShapeDtypeStruct((s, D_LRU), F32)],
        scratch_shapes=[pltpu.VMEM((HALO, B), F32), pltpu.VMEM((HALO, B), F32)],
        input_output_aliases={9: 0},
        compiler_params=_params(("parallel", "arbitrary")),
    )(proj, proj, conv_w, conv_b, wa, wx, ba, bx, lam, mix)


def _lru_bwd(proj, hsave, dmix, dproj, conv_w, conv_b, wa, wx, ba, bx, lam, tt=1024):
    s = proj.shape[0]
    tt = min(tt, s)
    nt = s // tt
    B = LRU_BLOCK_DIM
    lru_col = 4 * D_MLSTM // B
    dmix_col = D_MLSTM // B
    hpb = tt // HALO

    def body(xr_ref, xprev_ref, gr_ref, h_ref, hprev_ref, dmix_ref, cw_ref, cb_ref, wa_ref, wx_ref,
             ba_ref, bx_ref, lam_ref, dproj_in_ref,
             dxg_ref, dcw_ref, dcb_ref, dwa_ref, dwx_ref, dba_ref, dbx_ref, dlam_ref,
             gcar_scr, acar_scr, dxc_scr):
        i = pl.program_id(1)
        first_tile = i == nt - 1

        @pl.when(i == 0)
        def _():
            gcar_scr[...] = jnp.zeros_like(gcar_scr)
            acar_scr[...] = jnp.zeros_like(acar_scr)
            dxc_scr[...] = jnp.zeros_like(dxc_scr)
            for ref in (dcw_ref, dcb_ref, dwa_ref, dwx_ref, dba_ref, dbx_ref, dlam_ref):
                ref[...] = jnp.zeros_like(ref)

        xr = xr_ref[...]
        xprev = jnp.where(first_tile, 0.0, xprev_ref[...])
        hprev = jnp.where(first_tile, 0.0, hprev_ref[...])
        lam = lam_ref[...]
        taps = [_shift_down(xr, xprev, LRU_CONV - 1 - j) for j in range(LRU_CONV)]
        xc = cb_ref[...] + taps[0] * cw_ref[0:1, :]
        for j in range(1, LRU_CONV):
            xc = xc + taps[j] * cw_ref[j:j + 1, :]
        r, ig, sp, a, mult = _lru_gates(xc, wa_ref, wx_ref, ba_ref[...], bx_ref[...], lam)
        gr = gr_ref[...]
        hv = h_ref[...]
        dout = dmix_ref[...]
        dxg_ref[:, B:] = (dout * hv * _gelu_grad(gr)).astype(BF16)
        dh = dout * _gelu(gr)
        rows = _rows((tt, B))
        acc_a = _shift_up(a, acar_scr[...], 1)
        acc_b = dh
        d = 1
        while d < tt:
            if d < HALO:
                keep = rows < tt - d
                sh_a = jnp.where(keep, pltpu.roll(acc_a, tt - d, axis=0), 1.0)
                sh_b = jnp.where(keep, pltpu.roll(acc_b, tt - d, axis=0), 0.0)
            else:
                sh_a = jnp.concatenate([acc_a[d:], jnp.ones((d, B), F32)], axis=0)
                sh_b = jnp.concatenate([acc_b[d:], jnp.zeros((d, B), F32)], axis=0)
            acc_b = acc_a * sh_b + acc_b
            acc_a = acc_a * sh_a
            d *= 2
        gv = acc_b + acc_a * gcar_scr[0:1, :]
        gcar_scr[...] = jnp.broadcast_to(gv[0:1, :], gcar_scr.shape)
        acar_scr[...] = jnp.broadcast_to(a[0:1, :], acar_scr.shape)
        h_before = _shift_down(hv, hprev, 1)
        da = gv * h_before
        dmult = gv * (ig * xc)
        dig = gv * mult * xc
        dxc = gv * mult * ig
        dlog_a = da * a - dmult * (a * a) / mult
        dr = dlog_a * (-LRU_C * sp)
        dlam_ref[...] += jnp.sum(dlog_a * (-LRU_C * r), axis=0, keepdims=True) * (-_sigmoid(-lam))
        dpre_r = dr * r * (1.0 - r)
        dpre_i = dig * ig * (1.0 - ig)
        dba_ref[...] += jnp.sum(dpre_r, axis=0, keepdims=True)
        dbx_ref[...] += jnp.sum(dpre_i, axis=0, keepdims=True)
        dwa_ref[...] += _dot(xc, dpre_r, TN)
        dwx_ref[...] += _dot(xc, dpre_i, TN)
        dxc = dxc + _dot(dpre_r, wa_ref[...], NT) + _dot(dpre_i, wx_ref[...], NT)
        dcb_ref[...] += jnp.sum(dxc, axis=0, keepdims=True)
        nxt = dxc_scr[...]
        dxr = jnp.zeros((tt, B), F32)
        for j in range(LRU_CONV):
            sft = LRU_CONV - 1 - j
            dcw_ref[j:j + 1, :] += jnp.sum(dxc * taps[j], axis=0, keepdims=True)
            dxr = dxr + _shift_up(dxc, nxt, sft) * cw_ref[j:j + 1, :]
        dxc_scr[...] = dxc[:HALO, :]
        dxg_ref[:, :B] = dxr.astype(BF16)

    rev = lambda i: nt - 1 - i
    tile = lambda col, step: pl.BlockSpec((tt, B), lambda n, i: (rev(i), col + step * n))
    halo = lambda col, step: pl.BlockSpec(
        (HALO, B), lambda n, i: (jnp.maximum(rev(i) * hpb - 1, 0), col + step * n))
    chan = lambda rws: pl.BlockSpec((rws, B), lambda n, i: (0, n))
    wblk = pl.BlockSpec((None, B, B), lambda n, i: (n, 0, 0))
    return pl.pallas_call(
        body, grid=(LRU_BLOCKS, nt), name="lru_bwd",
        in_specs=[tile(lru_col, 2), halo(lru_col, 2), tile(lru_col + 1, 2), tile(0, 1), halo(0, 1),
                  tile(dmix_col, 1), chan(LRU_CONV), chan(1), wblk, wblk, chan(1), chan(1), chan(1), ANY],
        out_specs=[pl.BlockSpec((tt, 2 * B), lambda n, i: (rev(i), lru_col // 2 + n)),
                   chan(LRU_CONV), chan(1), wblk, wblk, chan(1), chan(1), chan(1)],
        out_shape=[jax.ShapeDtypeStruct(dproj.shape, BF16),
                   jax.ShapeDtypeStruct((LRU_CONV, D_LRU), F32), jax.ShapeDtypeStruct((1, D_LRU), F32),
                   jax.ShapeDtypeStruct((LRU_BLOCKS, B, B), F32), jax.ShapeDtypeStruct((LRU_BLOCKS, B, B), F32),
                   jax.ShapeDtypeStruct((1, D_LRU), F32), jax.ShapeDtypeStruct((1, D_LRU), F32),
                   jax.ShapeDtypeStruct((1, D_LRU), F32)],
        scratch_shapes=[pltpu.VMEM((HALO, B), F32), pltpu.VMEM((HALO, B), F32), pltpu.VMEM((HALO, B), F32)],
        input_output_aliases={13: 0},
        compiler_params=_params(("parallel", "arbitrary")),
    )(proj, proj, proj, hsave, hsave, dmix, conv_w, conv_b, wa, wx, ba, bx, lam, dproj)


def _ffn_conv(gp, prev, w_ref, b):
    g = b + _shift_down(gp, prev, 2) * w_ref[0:1, :]
    for j in range(1, FFN_CONV):
        g = g + _shift_down(gp, prev, FFN_CONV - 1 - j) * w_ref[j:j + 1, :]
    return g


def _ffn_act_fwd(gu, conv_w, conv_b, tt=512):
    s = gu.shape[0]
    tt = min(tt, s)
    d_ff = conv_w.shape[1]
    tc = d_ff // N_CHIPS
    hpb = tt // HALO

    def body(g_ref, gprev_ref, u_ref, w_ref, b_ref, act_ref):
        prev = jnp.where(pl.program_id(0) == 0, 0.0, gprev_ref[...])
        gate = _ffn_conv(g_ref[...], prev, w_ref, b_ref[...])
        act_ref[...] = (gate * _sigmoid(gate) * u_ref[...]).astype(BF16)

    return pl.pallas_call(
        body, grid=(s // tt, N_CHIPS), name="ffn_act_fwd",
        in_specs=[pl.BlockSpec((tt, tc), lambda i, j: (i, 2 * j)),
                  pl.BlockSpec((HALO, tc), lambda i, j: (jnp.maximum(i * hpb - 1, 0), 2 * j)),
                  pl.BlockSpec((tt, tc), lambda i, j: (i, 2 * j + 1)),
                  pl.BlockSpec((FFN_CONV, tc), lambda i, j: (0, j)),
                  pl.BlockSpec((1, tc), lambda i, j: (0, j))],
        out_specs=pl.BlockSpec((tt, tc), lambda i, j: (i, j)),
        out_shape=jax.ShapeDtypeStruct((s, d_ff), BF16),
        compiler_params=_params(("parallel", "parallel")),
    )(gu, gu, gu, conv_w, conv_b)


def _ffn_act_bwd(gu, dact, conv_w, conv_b, tt=512):
    s = gu.shape[0]
    tt = min(tt, s)
    nt = s // tt
    d_ff = conv_w.shape[1]
    tc = d_ff // N_CHIPS
    hpb = tt // HALO

    def dgate_of(gate, up, da):
        sg = _sigmoid(gate)
        return da * up * (sg * (1.0 + gate * (1.0 - sg))), da * (gate * sg)

    def body(g_ref, gprev_ref, gnext_ref, u_ref, unext_ref, da_ref, danext_ref, w_ref, b_ref,
             dgu_ref, dw_ref, db_ref):
        i = pl.program_id(1)

        @pl.when(i == 0)
        def _():
            dw_ref[...] = jnp.zeros_like(dw_ref)
            db_ref[...] = jnp.zeros_like(db_ref)

        gp = g_ref[...]
        prev = jnp.where(i == 0, 0.0, gprev_ref[...])
        bias = b_ref[...]
        taps = [_shift_down(gp, prev, FFN_CONV - 1 - j) for j in range(FFN_CONV)]
        gate = bias + taps[0] * w_ref[0:1, :]
        for j in range(1, FFN_CONV):
            gate = gate + taps[j] * w_ref[j:j + 1, :]
        dgate, dup = dgate_of(gate, u_ref[...], da_ref[...])
        gate_n = _ffn_conv(gnext_ref[...], gp[tt - HALO:, :], w_ref, bias)
        dgate_n, _ = dgate_of(gate_n, unext_ref[...], danext_ref[...])
        dgate_n = jnp.where(i == nt - 1, 0.0, dgate_n)
        db_ref[...] += jnp.sum(dgate, axis=0, keepdims=True)
        dgp = jnp.zeros((tt, tc), F32)
        for j in range(FFN_CONV):
            dw_ref[j:j + 1, :] += jnp.sum(dgate * taps[j], axis=0, keepdims=True)
            dgp = dgp + _shift_up(dgate, dgate_n, FFN_CONV - 1 - j) * w_ref[j:j + 1, :]
        dgu_ref[:, :tc] = dgp.astype(BF16)
        dgu_ref[:, tc:] = dup.astype(BF16)

    tile = lambda half: pl.BlockSpec((tt, tc), lambda j, i, half=half: (i, 2 * j + half))
    hprev = lambda half: pl.BlockSpec((HALO, tc), lambda j, i, half=half: (jnp.maximum(i * hpb - 1, 0), 2 * j + half))
    hnext = lambda half: pl.BlockSpec(
        (HALO, tc), lambda j, i, half=half: (jnp.minimum((i + 1) * hpb, nt * hpb - 1), 2 * j + half))
    return pl.pallas_call(
        body, grid=(N_CHIPS, nt), name="ffn_act_bwd",
        in_specs=[tile(0), hprev(0), hnext(0), tile(1), hnext(1),
                  pl.BlockSpec((tt, tc), lambda j, i: (i, j)),
                  pl.BlockSpec((HALO, tc), lambda j, i: (jnp.minimum((i + 1) * hpb, nt * hpb - 1), j)),
                  pl.BlockSpec((FFN_CONV, tc), lambda j, i: (0, j)),
                  pl.BlockSpec((1, tc), lambda j, i: (0, j))],
        out_specs=[pl.BlockSpec((tt, 2 * tc), lambda j, i: (i, j)),
                   pl.BlockSpec((FFN_CONV, tc), lambda j, i: (0, j)),
                   pl.BlockSpec((1, tc), lambda j, i: (0, j))],
        out_shape=[jax.ShapeDtypeStruct((s, 2 * d_ff), BF16),
                   jax.ShapeDtypeStruct((FFN_CONV, d_ff), F32), jax.ShapeDtypeStruct((1, d_ff), F32)],
        compiler_params=_params(("parallel", "arbitrary")),
    )(gu, gu, gu, gu, gu, dact, dact, conv_w, conv_b)


def _gate_grads(dgate, dproj, tm=512):
    s, n = dgate.shape
    tm = min(tm, s)

    def body(a_ref, dproj_in_ref, o_ref, dproj_ref):
        @pl.when(pl.program_id(0) == 0)
        def _():
            o_ref[...] = jnp.zeros_like(o_ref)
        a = a_ref[...]
        o_ref[...] += jnp.sum(a, axis=0, keepdims=True)
        dproj_ref[...] = a.astype(BF16)

    return pl.pallas_call(
        body, grid=(s // tm,), name="gate_grads",
        in_specs=[pl.BlockSpec((tm, n), lambda i: (i, 0)), ANY],
        out_specs=[pl.BlockSpec((1, n), lambda i: (0, 0)),
                   pl.BlockSpec((tm, n), lambda i: (i, (_QKVO + 2 * D_LRU) // LANES))],
        out_shape=[jax.ShapeDtypeStruct((1, n), F32), jax.ShapeDtypeStruct(dproj.shape, BF16)],
        input_output_aliases={1: 1},
        compiler_params=_params(("arbitrary",)),
    )(dgate, dproj)


def _pick(n, *cands):
    for c in cands:
        if n % c == 0:
            return c
    raise ValueError(f"no tile for {n}")


def _behind(a, token):
    return a if token is None else a + token[0:1, 0:1].astype(a.dtype).reshape((1,) * a.ndim)


class _Gathered:
    def __init__(self, w):
        self.w = w

    def begin(self):
        return None

    def mid(self, grp, after):
        return None

    def end(self, grp, after):
        return self.w

    def reduce_early(self, grads):
        return None

    def reduce_early_mid(self, after):
        return None

    def reduce_late(self, grads):
        return None

    def reduce_late_mid(self, after):
        return None


def _local_step(x, target, w, comm):
    s, d = x.shape
    nc = s // CHUNK
    tm = _pick(s, 1024, 512, 256)
    tn_proj = _pick(_PROJ_PAD, 896)
    gate_col = 4 * D_MLSTM + 2 * D_LRU
    w = dict(w)

    token = comm.begin()
    n1, rstd1 = _rmsnorm_fwd("norm_mix_fwd", x, _behind(w["norm_mix_g"], token))
    comm.mid(0, n1)
    w.update(comm.end(0, None))
    proj = _mm_nn("proj_fwd", n1, w["w_in"], tm, tn_proj, d)
    token = comm.mid(1, proj)
    gates = proj[:, gate_col:gate_col + 2 * HEADS]
    gates_t = gates.reshape(nc, CHUNK, 2 * HEADS).transpose(0, 2, 1)
    bias_row = _behind(jnp.pad(w["b_gate_m"], ((0, 0), (0, LANES - 2 * HEADS))), token)
    bias_col = w["b_gate_m"].reshape(2 * HEADS, 1)
    mix, cprev, nprev, mprev = _mlstm_fwd(proj, gates_t, bias_row, bias_col, w["mlstm_norm_g"])
    mix, hsave = _lru_fwd(proj, mix, w["lru_conv_w"], w["lru_conv_b"], w["lru_wa"], w["lru_wx"],
                          w["lru_ba"], w["lru_bx"], w["lru_lambda"])
    w.update(comm.end(1, hsave))
    token = comm.mid(2, hsave)
    x1 = _mm_nn("out_fwd", mix, w["w_out"], tm, 1024, d, res=x)
    n2, rstd2 = _rmsnorm_fwd("norm_ffn_fwd", x1, _behind(w["norm_ffn_g"], token))
    w.update(comm.end(2, n2))
    token = comm.mid(3, n2)
    gu = _mm_up_fwd("up_fwd", n2, w["w_up"], tm, d)
    act = _ffn_act_fwd(gu, w["ffn_conv_w"], _behind(w["ffn_conv_b"], token))
    w.update(comm.end(3, act))
    d_ff = w["w_down"].shape[0]
    x2 = _mm_nn("down_fwd", act, w["w_down"], min(tm, 512), 1024, d_ff, res=x1)
    loss, dx2, dx2b, g_norm_final = _loss_head("loss_head", x2, w["norm_final_g"], target)

    grads = {"norm_final_g": g_norm_final}
    dact = _mm_nt("down_bwd_x", dx2b, w["w_down"], tm, d_ff // N_CHIPS, d)
    grads["w_down"] = _mm_tn("down_bwd_w", act, dx2b, d_ff // N_CHIPS, 1024, 2048)
    dgu, grads["ffn_conv_w"], grads["ffn_conv_b"] = _ffn_act_bwd(gu, dact, w["ffn_conv_w"], w["ffn_conv_b"])
    dn2 = _mm_up_bwd_x("up_bwd_x", dgu, w["w_up"], tm, 1024)
    grads["w_up"] = _mm_up_bwd_w("up_bwd_w", n2, dgu, 1024, 2048)
    dx1, dx1b, grads["norm_ffn_g"] = _rmsnorm_bwd("norm_ffn_bwd", x1, rstd2, w["norm_ffn_g"], dn2, dx2)
    dmix = _mm_nt("out_bwd_x", dx1b, w["w_out"], tm, 1024, d)
    grads["w_out"] = _mm_tn("out_bwd_w", mix, dx1b, 1024, 1024, 2048)
    token = comm.reduce_early(grads)
    dproj, dgate, grads["mlstm_norm_g"] = _mlstm_bwd(proj, gates_t, _behind(bias_row, token), bias_col,
                                                     w["mlstm_norm_g"], cprev, nprev, mprev, dmix)
    token = comm.reduce_early_mid(dproj)
    (dproj, grads["lru_conv_w"], grads["lru_conv_b"], grads["lru_wa"], grads["lru_wx"],
     grads["lru_ba"], grads["lru_bx"], grads["lru_lambda"]) = _lru_bwd(
        proj, hsave, dmix, dproj, w["lru_conv_w"], _behind(w["lru_conv_b"], token), w["lru_wa"], w["lru_wx"],
        w["lru_ba"], w["lru_bx"], w["lru_lambda"])
    gate_bias_grad, dproj = _gate_grads(dgate, dproj)
    grads["b_gate_m"] = gate_bias_grad[:, :2 * HEADS]
    grads["w_in"] = _mm_tn("proj_bwd_w", n1, dproj, 1024, tn_proj, 2048)
    token = comm.reduce_late(grads)
    dn1 = _mm_nt("proj_bwd_x", dproj, w["w_in"], tm, 512, _PROJ_PAD, after=token)
    token = comm.reduce_late_mid(dn1)
    grad_x, _, grads["norm_mix_g"] = _rmsnorm_bwd("norm_mix_bwd", x, rstd1, _behind(w["norm_mix_g"], token),
                                                  dn1, dx1)
    return loss, grad_x, grads


WEIGHT_NAMES = ("norm_mix_g", "w_in", "b_gate_m", "mlstm_norm_g", "lru_conv_w", "lru_conv_b", "lru_wa", "lru_ba",
                "lru_wx", "lru_bx", "lru_lambda", "w_out", "norm_ffn_g", "w_up", "ffn_conv_w", "ffn_conv_b",
                "w_down", "norm_final_g")
BIG = ("w_in", "w_out", "w_up", "w_down")
SMALL_SHARDED = ("mlstm_norm_g", "lru_conv_w", "ffn_conv_w")
SMALL = tuple(n for n in WEIGHT_NAMES if n not in BIG)
SMALL_REPLICATED = tuple(n for n in SMALL if n not in SMALL_SHARDED)


def _proj_segments():
    segs = [(0, 0, _QKVO), (_QKVO, _QKVO + 2 * D_LRU, _N_GATES)]
    for n in range(LRU_BLOCKS):
        segs.append((_QKVO + _N_GATES + n * LRU_BLOCK_DIM, _QKVO + 2 * n * LRU_BLOCK_DIM, LRU_BLOCK_DIM))
        segs.append((_QKVO + _N_GATES + D_LRU + n * LRU_BLOCK_DIM, _QKVO + (2 * n + 1) * LRU_BLOCK_DIM,
                     LRU_BLOCK_DIM))
    return segs


def _w_in_shards_to_local(shards):
    width = shards.shape[2]
    pieces = []
    for g0, _, n in sorted(_proj_segments(), key=lambda s: s[1]):
        at = g0
        while at < g0 + n:
            j = at // width
            stop = min(g0 + n, (j + 1) * width)
            pieces.append(shards[j][:, at - j * width:stop - j * width])
            at = stop
    pieces.append(jnp.zeros((shards.shape[1], PROJ_GATE_PAD - _N_GATES), shards.dtype))
    return jnp.concatenate(pieces, axis=1)


def _w_in_local_to_shards(w):
    width = _PROJ_COLS // N_CHIPS
    shards = []
    for j in range(N_CHIPS):
        pieces = []
        for g0, l0, n in sorted(_proj_segments()):
            lo, hi = max(g0, j * width), min(g0 + n, (j + 1) * width)
            if lo < hi:
                pieces.append(w[:, l0 + lo - g0:l0 + hi - g0])
        shards.append(jnp.concatenate(pieces, axis=1))
    return jnp.stack(shards)


def _w_in_to_global(w):
    sh = _w_in_local_to_shards(w)
    return jnp.concatenate([sh[j] for j in range(N_CHIPS)], axis=1)


def _size(shp):
    return functools.reduce(lambda a, b: a * b, shp, 1)


def _lane_dense(shp):
    return len(shp) >= 2 and shp[-1] == LANES and _size(shp) % (HALO * LANES) == 0


def _pack_rows(shapes):
    loose = sum(_size(shp) for shp in shapes if not _lane_dense(shp))
    return sum(_size(shp) // LANES for shp in shapes if _lane_dense(shp)) + -(-loose // (HALO * LANES)) * HALO


def _pack(arrs, rows):
    del rows
    parts = [a.reshape(-1, LANES).astype(F32) for a in arrs if _lane_dense(a.shape)]
    loose = [a.reshape(-1).astype(F32) for a in arrs if not _lane_dense(a.shape)]
    if loose:
        flat = jnp.concatenate(loose)
        n = -(-flat.shape[0] // (HALO * LANES)) * HALO * LANES
        parts.append(jnp.pad(flat, (0, n - flat.shape[0])).reshape(-1, LANES))
    return parts[0] if len(parts) == 1 else jnp.concatenate(parts, axis=0)


def _unpack(buf, shapes):
    out, row = {}, 0
    for i, shp in enumerate(shapes):
        if _lane_dense(shp):
            n = _size(shp) // LANES
            out[i] = buf[row:row + n].reshape(shp)
            row += n
    flat, at = buf[row:].reshape(-1), 0
    for i, shp in enumerate(shapes):
        if not _lane_dense(shp):
            out[i] = flat[at:at + _size(shp)].reshape(shp)
            at += _size(shp)
    return [out[i] for i in range(len(shapes))]


def _assemble_weights(g_in, g_out, g_up, g_down, small_sharded, replicated):
    w = dict(replicated)
    w["w_in"] = _w_in_shards_to_local(g_in)
    w["w_out"] = g_out.reshape(-1, g_out.shape[-1])
    w["w_up"] = g_up
    w["w_down"] = g_down.reshape(-1, g_down.shape[-1])
    for name, v in small_sharded.items():
        w[name] = jnp.concatenate([v[j] for j in range(N_CHIPS)], axis=1)
    return w


def _full_weights_from_global(weights):
    shard = lambda a, axis: jnp.stack(jnp.split(a, N_CHIPS, axis=axis))
    rep = {n: weights[n].reshape(1, -1) if weights[n].ndim <= 2 and n != "b_gate_m" else weights[n]
           for n in SMALL_REPLICATED}
    rep["b_gate_m"] = weights["b_gate_m"].reshape(1, -1)
    return _assemble_weights(shard(weights["w_in"], 1).astype(BF16), shard(weights["w_out"], 0).astype(BF16),
                             shard(weights["w_up"], 1).astype(BF16), shard(weights["w_down"], 0).astype(BF16),
                             {n: shard(weights[n], 1) for n in SMALL_SHARDED}, rep)


def _grads_to_global(grads):
    g = dict(grads)
    g["w_in"] = _w_in_to_global(grads["w_in"])
    g["w_up"] = jnp.concatenate([grads["w_up"][j] for j in range(N_CHIPS)], axis=1)
    return g


def _place():
    x, y, c = lax.axis_index("x"), lax.axis_index("y"), lax.axis_index("c")
    chips = [(1 - x, y), (x, 1 - y), (1 - x, 1 - y)]
    return x, y, c, 2 * x + y, chips


def _half_rows(n_rows, which):
    half = n_rows // 2
    return pl.ds(pl.multiple_of(which * half, 16), half)


def _rcopy(src, dst, send_sem, recv_sem, to):
    return pltpu.make_async_remote_copy(src_ref=src, dst_ref=dst, send_sem=send_sem, recv_sem=recv_sem,
                                        device_id=to, device_id_type=MESH)


HBM_SPEC = pl.BlockSpec(memory_space=pltpu.HBM)
SEM_SPEC = pl.BlockSpec(memory_space=pltpu.SEMAPHORE)
TOKEN_SHAPE = (8, LANES)


def _split_call(name, bufs, sems_in, sems_out_shapes, body_fn, after=None):
    nb, ni, no = len(bufs), len(sems_in), len(sems_out_shapes)
    after = [] if after is None else list(after) if isinstance(after, (list, tuple)) else [after]

    def body(*refs):
        buf_refs = refs[:nb]
        sem_in_refs = refs[nb:nb + ni]
        outs = refs[nb + ni + len(after):]
        sem_out_refs = outs[:no]
        token_ref = outs[no + nb]
        body_fn(buf_refs, sem_in_refs, sem_out_refs)
        token_ref[...] = jnp.zeros_like(token_ref)

    out_shape = ([pltpu.SemaphoreType.DMA(shp) for shp in sems_out_shapes]
                 + [pltpu.HBM(b.shape, b.dtype) for b in bufs] + [jax.ShapeDtypeStruct(TOKEN_SHAPE, F32)])
    res = pl.pallas_call(
        body, name=name, out_shape=out_shape,
        in_specs=[HBM_SPEC] * nb + [SEM_SPEC] * ni + [ANY] * len(after),
        out_specs=[SEM_SPEC] * no + [HBM_SPEC] * nb + [pl.BlockSpec(memory_space=pltpu.VMEM)],
        input_output_aliases={i: no + i for i in range(nb)},
        compiler_params=pltpu.CompilerParams(has_side_effects=pltpu.SideEffectType.DATAFLOW_SIDE_EFFECTING),
    )(*[pltpu.with_memory_space_constraint(b, pltpu.HBM) for b in bufs], *sems_in, *after)
    return list(res[:no]), list(res[no:no + nb]), res[no + nb]


def _place_own_shard(name, idx, shard, after=None):
    rows, cols = shard.shape
    tr = _row_tile(rows)

    def body(idx_ref, s_ref, *rest):
        rest[-1][...] = s_ref[...].astype(BF16)

    return pl.pallas_call(
        body, name=name, out_shape=jax.ShapeDtypeStruct((N_CHIPS, rows, cols), BF16),
        grid_spec=pltpu.PrefetchScalarGridSpec(
            num_scalar_prefetch=1, grid=(rows // tr,),
            in_specs=[pl.BlockSpec((tr, cols), lambda i, s: (i, 0))] + ([] if after is None else [ANY]),
            out_specs=pl.BlockSpec((None, tr, cols), lambda i, s: (s[1], i, 0))),
        compiler_params=_params(("parallel",)),
    )(idx, shard, *(() if after is None else (after,)))


GATHER_GROUPS = ((0, 4), (1,), (2,), (3,))


def _gather_start(name, lands, groups, after=None):
    members = [w for g in groups for w in GATHER_GROUPS[g]]

    def starts(bufs, _, sems):
        x, y, c, me, chips = _place()
        for gi, g in enumerate(groups):
            for pos, w in enumerate(GATHER_GROUPS[g]):
                buf = bufs[members.index(w)]
                part = buf.at[me] if w == 4 else buf.at[me, _half_rows(buf.shape[1], c)]
                for k, chip in enumerate(chips):
                    _rcopy(part, part, sems[2 * gi].at[3 * pos + k], sems[2 * gi + 1].at[3 * pos + k],
                           (*chip, c)).start()

    shapes = []
    for g in groups:
        shapes += [(3 * len(GATHER_GROUPS[g]),)] * 2
    sems, bufs, token = _split_call(name, [lands[w] for w in members], [], shapes, starts, after=after)
    return ({g: (sems[2 * gi], sems[2 * gi + 1]) for gi, g in enumerate(groups)},
            dict(zip(members, bufs)), token)


def _gather_mid(grp, lands, sems, after):
    members = GATHER_GROUPS[grp]
    big = [w for w in members if w != 4]

    def mid(bufs, sems_in, sems_out):
        x, y, c, me, chips = _place()
        send_sems, recv_sems = sems_in
        for pos, w in enumerate(members):
            for k, chip in enumerate(chips):
                cid = 2 * chip[0] + chip[1]
                buf = bufs[pos]
                mine = buf.at[me] if w == 4 else buf.at[me, _half_rows(buf.shape[1], c)]
                theirs = buf.at[cid] if w == 4 else buf.at[cid, _half_rows(buf.shape[1], c)]
                arrival = _rcopy(mine, theirs, send_sems.at[3 * pos + k], recv_sems.at[3 * pos + k], (*chip, c))
                arrival.wait_recv()
                arrival.wait_send()
                if w != 4:
                    _rcopy(theirs, theirs, sems_out[0].at[3 * big.index(w) + k],
                           sems_out[1].at[3 * big.index(w) + k], (x, y, 1 - c)).start()

    new_sems, bufs, token = _split_call(f"gather_mid_{grp}", [lands[w] for w in members], list(sems),
                                        [(3 * len(big),), (3 * len(big),)], mid, after=after)
    return new_sems, bufs, token


def _gather_end(grp, bufs, sems, after):
    members = GATHER_GROUPS[grp]
    big = [w for w in members if w != 4]

    def end(refs, sems_in, _):
        x, y, c, me, chips = _place()
        send_sems, recv_sems = sems_in
        for pos, w in enumerate(members):
            if w == 4:
                continue
            for k, chip in enumerate(chips):
                cid = 2 * chip[0] + chip[1]
                buf = refs[pos]
                sent = buf.at[cid, _half_rows(buf.shape[1], c)]
                landed = buf.at[cid, _half_rows(buf.shape[1], 1 - c)]
                fwd = _rcopy(sent, landed, send_sems.at[3 * big.index(w) + k], recv_sems.at[3 * big.index(w) + k],
                             (x, y, 1 - c))
                fwd.wait_recv()
                fwd.wait_send()

    _, bufs, token = _split_call(f"gather_end_{grp}", bufs, list(sems), [], end, after=after)
    return bufs, token


def _pair_start(name, grads, extra=None):
    n = len(grads)
    bufs = list(grads) + [lax.empty((g.shape[0], g.shape[1] // 2, g.shape[2]), g.dtype) for g in grads]
    if extra is not None:
        bufs += [extra, lax.empty(extra.shape, extra.dtype)]

    def starts(refs, _, sems):
        x, y, c, _, _ = _place()
        for w in range(n):
            other = _half_rows(refs[w].shape[1], 1 - c)
            _rcopy(refs[w].at[:, other], refs[n + w], sems[0].at[w], sems[1].at[w], (x, y, 1 - c)).start()
        if extra is not None:
            _rcopy(refs[2 * n], refs[2 * n + 1], sems[0].at[n], sems[1].at[n], (x, y, 1 - c)).start()

    count = n + (extra is not None)
    return _split_call(name, bufs, [], [(count,), (count,)], starts)


def _pair_wait(name, n, bufs, sems, after):
    has_extra = len(bufs) > 2 * n

    def waits(refs, sems_in, _):
        x, y, c, _, _ = _place()
        for w in range(n):
            other = _half_rows(refs[w].shape[1], 1 - c)
            cp = _rcopy(refs[w].at[:, other], refs[n + w], sems_in[0].at[w], sems_in[1].at[w], (x, y, 1 - c))
            cp.wait_recv()
            cp.wait_send()
        if has_extra:
            cp = _rcopy(refs[2 * n], refs[2 * n + 1], sems_in[0].at[n], sems_in[1].at[n], (x, y, 1 - c))
            cp.wait_recv()
            cp.wait_send()

    _, bufs, token = _split_call(name, bufs, list(sems), [], waits, after=after)
    return bufs, token


def _chip_start(name, partials, small=None):
    n = len(partials)
    bufs = list(partials) + [lax.empty(p.shape, p.dtype) for p in partials] + ([] if small is None else [small])

    def starts(refs, _, sems):
        _, _, c, me, chips = _place()
        for w in range(n):
            for k, chip in enumerate(chips):
                cid = 2 * chip[0] + chip[1]
                _rcopy(refs[w].at[cid], refs[n + w].at[me], sems[0].at[3 * w + k], sems[1].at[3 * w + k],
                       (*chip, c)).start()
        if small is not None:
            for k, chip in enumerate(chips):
                _rcopy(refs[2 * n].at[me], refs[2 * n].at[me], sems[0].at[3 * n + k], sems[1].at[3 * n + k],
                       (*chip, c)).start()

    count = 3 * (n + (small is not None))
    return _split_call(name, bufs, [], [(count,), (count,)], starts)


def _chip_wait(name, n, bufs, sems, after):
    has_small = len(bufs) > 2 * n

    def waits(refs, sems_in, _):
        _, _, c, me, chips = _place()
        for w in range(n):
            for k, chip in enumerate(chips):
                cid = 2 * chip[0] + chip[1]
                cp = _rcopy(refs[w].at[cid], refs[n + w].at[cid], sems_in[0].at[3 * w + k],
                            sems_in[1].at[3 * w + k], (*chip, c))
                cp.wait_recv()
                cp.wait_send()
        if has_small:
            for k, chip in enumerate(chips):
                cid = 2 * chip[0] + chip[1]
                cp = _rcopy(refs[2 * n].at[me], refs[2 * n].at[cid], sems_in[0].at[3 * n + k],
                            sems_in[1].at[3 * n + k], (*chip, c))
                cp.wait_recv()
                cp.wait_send()

    _, bufs, token = _split_call(name, bufs, list(sems), [], waits, after=after)
    return bufs, token


def _small_pair_sum(idx, own, recv):
    rows = own.shape[0]

    def body(idx_ref, a_ref, b_ref, o_ref):
        o_ref[...] = a_ref[...] + b_ref[...]

    blk = pl.BlockSpec((rows, LANES), lambda i, s: (0, 0))
    return pl.pallas_call(
        body, name="small_pair_sum", out_shape=jax.ShapeDtypeStruct((N_CHIPS, rows, LANES), F32),
        grid_spec=pltpu.PrefetchScalarGridSpec(
            num_scalar_prefetch=1, grid=(1,), in_specs=[blk, blk],
            out_specs=pl.BlockSpec((None, rows, LANES), lambda i, s: (s[1], 0, 0))),
        compiler_params=_params(("arbitrary",)),
    )(idx, own, recv)


def _pair_share(name, shards, late=None):
    nb = len(shards)
    nl = 0 if late is None else 1

    def body(*refs):
        srcs = refs[:nb]
        dsts = refs[nb + nl:2 * nb + nl]
        send_sems, recv_sems = refs[2 * nb + 2 * nl:2 * nb + 2 * nl + 2]
        x, y, c, _, _ = _place()
        sibling = (x, y, 1 - c)
        sends = []
        for w in range(nb):
            mine = _half_rows(dsts[w].shape[0], c)
            sends.append(_rcopy(srcs[w].at[mine], dsts[w].at[mine], send_sems.at[w], recv_sems.at[w], sibling))
        if nl:
            late_ref, late_out = refs[nb], refs[2 * nb + 1]
            late_send, late_recv, local_sem = refs[2 * nb + 4:]
            my_id = 4 * x + 2 * y + c
            peer = lambda r: (1 - x if r & 4 else x, 1 - y if r & 2 else y, 1 - c if r & 1 else c)
            local = pltpu.make_async_copy(late_ref, late_out.at[my_id], local_sem)
            local.start()
            for r in range(1, N_DEV):
                sends.append(_rcopy(late_ref, late_out.at[my_id], late_send.at[r - 1], late_recv.at[r - 1],
                                    peer(r)))
        for cp in sends:
            cp.start()
        for w in range(nb):
            other = _half_rows(dsts[w].shape[0], 1 - c)
            _rcopy(srcs[w].at[other], dsts[w].at[other], send_sems.at[w], recv_sems.at[w], sibling).wait_recv()
        if nl:
            for r in range(1, N_DEV):
                frm = peer(r)
                _rcopy(late_ref, late_out.at[4 * frm[0] + 2 * frm[1] + frm[2]], late_send.at[r - 1],
                       late_recv.at[r - 1], frm).wait_recv()
        for cp in sends:
            cp.wait_send()
        if nl:
            local.wait()

    out_shape = [jax.ShapeDtypeStruct(h.shape, h.dtype) for h in shards]
    scratch = [pltpu.SemaphoreType.DMA((nb,)), pltpu.SemaphoreType.DMA((nb,))]
    if nl:
        out_shape.append(jax.ShapeDtypeStruct((N_DEV,) + late.shape, late.dtype))
        scratch += [pltpu.SemaphoreType.DMA((N_DEV - 1,)), pltpu.SemaphoreType.DMA((N_DEV - 1,)),
                    pltpu.SemaphoreType.DMA(())]
    return pl.pallas_call(
        body, name=name, out_shape=out_shape,
        in_specs=[ANY] * (nb + nl), out_specs=[ANY] * (nb + nl), scratch_shapes=scratch,
        input_output_aliases={w: w for w in range(nb)},
    )(*shards, *(() if late is None else (late,)))


def _row_tile(rows):
    return _pick(rows, 128, 64, 16, 8)


def _pair_sum(name, idx, grad, recv):
    n, half, cols = recv.shape
    tr = _row_tile(half)
    nrb = half // tr

    def body(idx_ref, g_ref, r_ref, o_ref):
        o_ref[...] = (g_ref[...] + r_ref[...]).astype(BF16)

    return pl.pallas_call(
        body, name=name, out_shape=jax.ShapeDtypeStruct(recv.shape, BF16),
        grid_spec=pltpu.PrefetchScalarGridSpec(
            num_scalar_prefetch=1, grid=(n - 1, nrb),
            in_specs=[pl.BlockSpec((None, tr, cols), lambda j, i, s: (s[2 + j], s[0] * nrb + i, 0)),
                      pl.BlockSpec((None, tr, cols), lambda j, i, s: (s[2 + j], i, 0))],
            out_specs=pl.BlockSpec((None, tr, cols), lambda j, i, s: (s[2 + j], i, 0))),
        compiler_params=_params(("parallel", "parallel")),
    )(idx, grad, recv)


def _final_sum(name, idx, grad, recv, chip_sums):
    _, half, cols = recv.shape
    tr = _row_tile(half)
    nrb = half // tr

    def body(idx_ref, g_ref, r_ref, p1_ref, p2_ref, p3_ref, o_ref):
        acc = g_ref[...] + r_ref[...]
        for p_ref in (p1_ref, p2_ref, p3_ref):
            acc = acc + p_ref[...].astype(F32)
        o_ref[...] = acc

    slot = lambda which: pl.BlockSpec((None, tr, cols), lambda i, s, which=which: (s[which], i, 0))
    return pl.pallas_call(
        body, name=name, out_shape=jax.ShapeDtypeStruct((2 * half, cols), F32),
        grid_spec=pltpu.PrefetchScalarGridSpec(
            num_scalar_prefetch=1, grid=(nrb,),
            in_specs=[pl.BlockSpec((None, tr, cols), lambda i, s: (s[1], s[0] * nrb + i, 0)),
                      slot(1), slot(2), slot(3), slot(4)],
            out_specs=pl.BlockSpec((tr, cols), lambda i, s: (s[0] * nrb + i, 0))),
        compiler_params=_params(("parallel",)),
    )(idx, grad, recv, chip_sums, chip_sums, chip_sums)


def _pair_sum_all(name, idx, grad, recv):
    _, half, cols = recv.shape
    tr = _row_tile(half)
    nrb = half // tr

    def body(idx_ref, g_ref, r_ref, o_ref):
        o_ref[...] = (g_ref[...] + r_ref[...]).astype(BF16)

    return pl.pallas_call(
        body, name=name, out_shape=jax.ShapeDtypeStruct((half, cols), BF16),
        grid_spec=pltpu.PrefetchScalarGridSpec(
            num_scalar_prefetch=1, grid=(nrb,),
            in_specs=[pl.BlockSpec((None, tr, cols), lambda i, s: (0, s[0] * nrb + i, 0)),
                      pl.BlockSpec((None, tr, cols), lambda i, s: (0, i, 0))],
            out_specs=pl.BlockSpec((tr, cols), lambda i, s: (i, 0))),
        compiler_params=_params(("parallel",)),
    )(idx, grad, recv)


def _final_sum_bf16(name, idx, partial, chip_sums):
    _, half, cols = partial.shape
    tr = _row_tile(half)
    nrb = half // tr

    def body(idx_ref, p0_ref, p1_ref, p2_ref, p3_ref, o_ref):
        acc = p0_ref[...].astype(F32)
        for p_ref in (p1_ref, p2_ref, p3_ref):
            acc = acc + p_ref[...].astype(F32)
        o_ref[...] = acc

    slot = lambda which: pl.BlockSpec((None, tr, cols), lambda i, s, which=which: (s[which], i, 0))
    return pl.pallas_call(
        body, name=name, out_shape=jax.ShapeDtypeStruct((2 * half, cols), F32),
        grid_spec=pltpu.PrefetchScalarGridSpec(
            num_scalar_prefetch=1, grid=(nrb,),
            in_specs=[slot(1), slot(2), slot(3), slot(4)],
            out_specs=pl.BlockSpec((tr, cols), lambda i, s: (s[0] * nrb + i, 0))),
        compiler_params=_params(("parallel",)),
    )(idx, partial, chip_sums, chip_sums, chip_sums)


def _small_sum(name, packs):
    n, rows, _ = packs.shape

    def body(p_ref, o_ref):
        acc = p_ref[0]
        for k in range(1, n):
            acc = acc + p_ref[k]
        o_ref[...] = acc

    return pl.pallas_call(
        body, name=name, out_shape=jax.ShapeDtypeStruct((rows, LANES), F32),
        in_specs=[pl.BlockSpec(memory_space=pltpu.VMEM)], out_specs=pl.BlockSpec(memory_space=pltpu.VMEM),
        compiler_params=pltpu.CompilerParams(vmem_limit_bytes=VMEM_LIMIT),
    )(packs)


def _adamw_math(w, g, m, v):
    m_new = ADAM_B1 * m + (1.0 - ADAM_B1) * g
    v_new = ADAM_B2 * v + (1.0 - ADAM_B2) * (g * g)
    m_hat = m_new / (1.0 - ADAM_B1 ** ADAM_STEP)
    v_hat = v_new / (1.0 - ADAM_B2 ** ADAM_STEP)
    return -ADAM_LR * (m_hat / (jnp.sqrt(v_hat) + ADAM_EPS) + ADAM_WD * w), m_new, v_new


def _adamw_many(name, ws, gs, ms, vs):
    n = len(ws)

    def body(*refs):
        for i in range(n):
            d, m_new, v_new = _adamw_math(refs[i][...], refs[n + i][...], refs[2 * n + i][...],
                                          refs[3 * n + i][...])
            refs[4 * n + i][...] = d
            refs[5 * n + i][...] = m_new
            refs[6 * n + i][...] = v_new

    vmem = pl.BlockSpec(memory_space=pltpu.VMEM)
    res = pl.pallas_call(
        body, name=name, in_specs=[vmem] * (4 * n), out_specs=[vmem] * (3 * n),
        out_shape=[jax.ShapeDtypeStruct(w.shape, F32) for w in ws] * 3,
        compiler_params=pltpu.CompilerParams(vmem_limit_bytes=VMEM_LIMIT),
    )(*ws, *gs, *ms, *vs)
    return res[:n], res[n:2 * n], res[2 * n:]


def _adamw(name, w, g, m, v):
    rows, cols = w.shape
    tr = rows if rows * cols * 4 <= (2 << 20) else _row_tile(rows)

    def body(w_ref, g_ref, m_ref, v_ref, g_out_ref, d_ref, nm_ref, nv_ref):
        gv = g_ref[...]
        g_out_ref[...] = gv
        d_ref[...], nm_ref[...], nv_ref[...] = _adamw_math(w_ref[...], gv, m_ref[...], v_ref[...])

    blk = pl.BlockSpec((tr, cols), lambda i: (i, 0))
    sds = jax.ShapeDtypeStruct((rows, cols), F32)
    return pl.pallas_call(
        body, name=name, grid=(rows // tr,), in_specs=[blk] * 4, out_specs=[blk] * 4, out_shape=[sds] * 4,
        compiler_params=_params(("parallel",)),
    )(w, g, m, v)


def _train_step(x, target, W, M, V):
    xi, yi, ci = lax.axis_index("x"), lax.axis_index("y"), lax.axis_index("c")
    me = 2 * xi + yi
    big = {n: W[n][0] for n in BIG}
    big_m = {n: M[n][0] for n in BIG}
    big_v = {n: V[n][0] for n in BIG}

    others = [jnp.where(jnp.int32(i) >= me, i + 1, i) for i in range(N_CHIPS - 1)]
    idx = jnp.stack([ci, me] + others).astype(jnp.int32)

    sharded_shapes = [W[n].shape[1:] for n in SMALL_SHARDED]
    small_pack = _pack([W[n][0] for n in SMALL_SHARDED], _pack_rows(sharded_shapes))
    small_land = lax.dynamic_update_slice(jnp.zeros((N_CHIPS,) + small_pack.shape, F32), small_pack[None],
                                          (me, 0, 0))
    replicated = {n: (W[n].reshape(1, -1) if W[n].ndim <= 2 else W[n][0]) for n in SMALL_REPLICATED}

    early = ("w_out", "w_up", "w_down")
    small_late = "norm_mix_g"
    small_early = tuple(n for n in SMALL if n != small_late)
    global_shape = lambda n: ((W[n].shape[1], W[n].shape[2] * N_CHIPS) if n in SMALL_SHARDED else
                              tuple(W[n].shape) if W[n].ndim == 1 else tuple(W[n].shape[1:]))
    small_shapes = [global_shape(n) for n in small_early]

    def shard_major(n, g):
        if n == "w_in":
            return _w_in_local_to_shards(g)
        return g if g.ndim == 3 else g.reshape((N_CHIPS, -1) + g.shape[1:])

    class _SplitComm:
        def reduce_early(self, grads):
            self.e_sems, self.e_bufs, token = _pair_start("pair_start_early",
                                                          [shard_major(n, grads[n]) for n in early])
            return token

        def reduce_early_mid(self, after):
            n = len(early)
            bufs, _ = _pair_wait("pair_wait_early", n, self.e_bufs, self.e_sems, after)
            self.e_grads, self.e_recv = bufs[:n], bufs[n:2 * n]
            partial = [_pair_sum(f"pair_sum_{nm}", idx, g, r) for nm, g, r in zip(early, self.e_grads, self.e_recv)]
            self.e_sems, self.e_bufs, token = _chip_start("chip_start_early", partial)
            return token

        def reduce_late(self, grads):
            pack = _pack([grads[n] for n in small_early], _pack_rows(small_shapes))
            self.l_sems, self.l_bufs, token = _pair_start("pair_start_late", [grads["w_in"][None]], extra=pack)
            return token

        def reduce_late_mid(self, after):
            bufs, _ = _pair_wait("pair_wait_late", 1, self.l_bufs, self.l_sems, after)
            partial = _w_in_local_to_shards(_pair_sum_all("pair_sum_w_in", idx, bufs[0], bufs[1]))
            self.l_sems, self.l_bufs, token = _chip_start("chip_start_late", [partial],
                                                          small=_small_pair_sum(idx, bufs[2], bufs[3]))
            return token

        def finish_early(self, after):
            n = len(early)
            bufs, _ = _chip_wait("chip_wait_early", n, self.e_bufs, self.e_sems, after)
            halves = [_final_sum(f"final_sum_{nm}", idx, g, r, p)
                      for nm, g, r, p in zip(early, self.e_grads, self.e_recv, bufs[n:2 * n])]
            return dict(zip(early, _pair_share("pair_share_early", halves)))

        def finish_late(self, after, late):
            bufs, _ = _chip_wait("chip_wait_late", 1, self.l_bufs, self.l_sems, after)
            half = _final_sum_bf16("final_sum_w_in", idx, bufs[0], bufs[1])
            small = dict(zip(small_early, _unpack(_small_sum("small_sum", bufs[2]), small_shapes)))
            whole, late_all = _pair_share("pair_share_late", [half], late)
            return whole, small, _small_sum("late_sum", late_all)

        def begin(self):
            first = {0: _place_own_shard("place_w_in", idx, big["w_in"]), 4: small_land}
            self.sems, self.lands, token = _gather_start("gather_start_0", first, (0,))
            rest = {i: _place_own_shard(f"place_{BIG[i]}", idx, big[BIG[i]], after=token) for i in (1, 2, 3)}
            sems, lands, token = _gather_start("gather_start_1", rest, (1, 2, 3), after=token)
            self.sems.update(sems)
            self.lands.update(lands)
            return token

        def mid(self, grp, after):
            if grp == 0:
                after = [after, big_m["w_in"], big_v["w_in"]]
            self.pending = _gather_mid(grp, self.lands, self.sems[grp], after)
            return self.pending[2]

        def end(self, grp, after):
            sems, bufs, _ = self.pending
            bufs, _ = _gather_end(grp, bufs, sems, after)
            if grp == 0:
                per_chip = [_unpack(bufs[1][j], sharded_shapes) for j in range(N_CHIPS)]
                out = {n: jnp.concatenate([per_chip[j][i] for j in range(N_CHIPS)], axis=1)
                       for i, n in enumerate(SMALL_SHARDED)}
                out["w_in"] = _w_in_shards_to_local(bufs[0])
                return out
            if grp == 2:
                return {"w_up": bufs[0]}
            return {("w_out" if grp == 1 else "w_down"): bufs[0].reshape(-1, bufs[0].shape[-1])}

    comm = _SplitComm()
    loss, grad_x, grads = _local_step(x[0], target[0], replicated, comm)
    loss = lax.psum(loss[0, 0], ("x", "y", "c"))
    out_g, out_d, out_m, out_v = {}, {}, {}, {}

    def update_big(n, grad):
        g, d, nm, nv = _adamw(f"adamw_{n}", big[n], grad, big_m[n], big_v[n])
        out_g[n], out_d[n], out_m[n], out_v[n] = g[None], d[None], nm[None], nv[None]
        return d

    early_grads = comm.finish_early(grad_x)
    done = [update_big(n, early_grads[n]) for n in early]
    late = _pack([grads[small_late]], _pack_rows([global_shape(small_late)]))
    w_in_grad, small_grads, late_sum = comm.finish_late(done, late)
    update_big("w_in", w_in_grad)
    small_grads[small_late] = _unpack(late_sum, [global_shape(small_late)])[0]
    for n in SMALL_SHARDED:
        width = W[n].shape[2]
        small_grads[n] = lax.dynamic_slice_in_dim(small_grads[n], me * width, width, axis=1)

    for n in SMALL:
        out_g[n] = small_grads[n].reshape(W[n].shape)
    two_d = lambda a: a.reshape(1, -1) if a.ndim == 1 else a
    results = _adamw_many("adamw_small", *[[two_d(src[n]) for n in SMALL] for src in (W, out_g, M, V)])
    for dst, arrs in zip((out_d, out_m, out_v), results):
        dst.update({n: a.reshape(W[n].shape) for n, a in zip(SMALL, arrs)})
    return (loss, grad_x[None], *[out_g[n] for n in WEIGHT_NAMES], *[out_d[n] for n in WEIGHT_NAMES],
            *[out_m[n] for n in WEIGHT_NAMES], *[out_v[n] for n in WEIGHT_NAMES])


def kernel(x, norm_mix_g, w_in, b_gate_m, mlstm_norm_g, lru_conv_w, lru_conv_b, lru_wa, lru_ba, lru_wx, lru_bx, lru_lambda, w_out, norm_ffn_g, w_up, ffn_conv_w, ffn_conv_b, w_down, norm_final_g, loss_target, m_norm_mix_g, m_w_in, m_b_gate_m, m_mlstm_norm_g, m_lru_conv_w, m_lru_conv_b, m_lru_wa, m_lru_ba, m_lru_wx, m_lru_bx, m_lru_lambda, m_w_out, m_norm_ffn_g, m_w_up, m_ffn_conv_w, m_ffn_conv_b, m_w_down, m_norm_final_g, v_norm_mix_g, v_w_in, v_b_gate_m, v_mlstm_norm_g, v_lru_conv_w, v_lru_conv_b, v_lru_wa, v_lru_ba, v_lru_wx, v_lru_bx, v_lru_lambda, v_w_out, v_norm_ffn_g, v_w_up, v_ffn_conv_w, v_ffn_conv_b, v_w_down, v_norm_final_g):
    W = dict(zip(WEIGHT_NAMES, (norm_mix_g, w_in, b_gate_m, mlstm_norm_g, lru_conv_w, lru_conv_b, lru_wa, lru_ba,
                                lru_wx, lru_bx, lru_lambda, w_out, norm_ffn_g, w_up, ffn_conv_w, ffn_conv_b,
                                w_down, norm_final_g)))
    M = dict(zip(WEIGHT_NAMES, (m_norm_mix_g, m_w_in, m_b_gate_m, m_mlstm_norm_g, m_lru_conv_w, m_lru_conv_b,
                                m_lru_wa, m_lru_ba, m_lru_wx, m_lru_bx, m_lru_lambda, m_w_out, m_norm_ffn_g,
                                m_w_up, m_ffn_conv_w, m_ffn_conv_b, m_w_down, m_norm_final_g)))
    V = dict(zip(WEIGHT_NAMES, (v_norm_mix_g, v_w_in, v_b_gate_m, v_mlstm_norm_g, v_lru_conv_w, v_lru_conv_b,
                                v_lru_wa, v_lru_ba, v_lru_wx, v_lru_bx, v_lru_lambda, v_w_out, v_norm_ffn_g,
                                v_w_up, v_ffn_conv_w, v_ffn_conv_b, v_w_down, v_norm_final_g)))
    return _train_step(x, loss_target, W, M, V)
```

```python
import functools

import jax
import jax.numpy as jnp
from jax import lax
from jax.experimental import pallas as pl
from jax.experimental.pallas import tpu as pltpu

F32 = jnp.float32
BF16 = jnp.bfloat16
MESH = pl.DeviceIdType.MESH

EPS = 1e-6
CHUNK = 512
HEADS = 4
HEAD_DIM = 256
D_MLSTM = HEADS * HEAD_DIM
LRU_BLOCKS = 8
LRU_BLOCK_DIM = 128
D_LRU = LRU_BLOCKS * LRU_BLOCK_DIM
LRU_C = 8.0
LRU_CONV = 4
FFN_CONV = 3
ADAM_LR = 0.001
ADAM_B1 = 0.9
ADAM_B2 = 0.999
ADAM_EPS = 1e-08
ADAM_WD = 0.01
ADAM_STEP = 10

N_CHIPS = 4
N_DEV = 8
LANES = 128
HALO = 8
PROJ_GATE_PAD = LANES
_QKVO = 4 * D_MLSTM
_N_GATES = 2 * HEADS
_PROJ_COLS = _QKVO + _N_GATES + 2 * D_LRU
_PROJ_PAD = _QKVO + 2 * D_LRU + PROJ_GATE_PAD
VMEM_LIMIT = 48 * 1024 * 1024
ANY = pl.BlockSpec(memory_space=pl.ANY)


def _params(sem, vmem=VMEM_LIMIT):
    return pltpu.CompilerParams(dimension_semantics=sem, vmem_limit_bytes=vmem)


def _matmul(name, a, b, grid, a_spec, b_spec, o_spec, out_sds, contract, res=None, res_spec=None, after=None):
    nk = grid[2]
    acc_shape = tuple(d for d in o_spec.block_shape if d is not None)

    def body(*refs):
        refs = list(refs)
        a_ref, b_ref = refs[:2]
        r_ref = refs[2] if res is not None else None
        o_ref = refs[-1] if nk == 1 else refs[-2]
        acc_ref = None if nk == 1 else refs[-1]
        k = pl.program_id(2)

        def part():
            return lax.dot_general(a_ref[...], b_ref[...], (contract, ((), ())), preferred_element_type=F32)

        def finish(r):
            if r_ref is not None:
                r = r_ref[...] + r
            o_ref[...] = r.astype(o_ref.dtype)

        if nk == 1:
            finish(part())
            return

        @pl.when(k == 0)
        def _():
            acc_ref[...] = part()

        @pl.when(jnp.logical_and(k > 0, k < nk - 1))
        def _():
            acc_ref[...] += part()

        @pl.when(k == nk - 1)
        def _():
            finish(acc_ref[...] + part())

    in_specs = [a_spec, b_spec] + ([] if res is None else [res_spec]) + ([] if after is None else [ANY])
    args = (a, b) + (() if res is None else (res,)) + (() if after is None else (after,))
    if after is not None:
        inner = body
        body = lambda *refs: inner(*refs[:len(in_specs) - 1], *refs[len(in_specs):])
    return pl.pallas_call(
        body, out_shape=out_sds, grid=grid, in_specs=in_specs, out_specs=o_spec,
        scratch_shapes=[] if nk == 1 else [pltpu.VMEM(acc_shape, F32)], name=name,
        compiler_params=_params(("parallel", "parallel", "arbitrary")),
    )(*args)


NN = ((1,), (0,))
NT = ((1,), (1,))
TN = ((0,), (0,))


def _mm_nn(name, a, b, tm, tn, tk, out_dtype=F32, res=None):
    m, k = a.shape
    n = b.shape[1]
    return _matmul(name, a, b, (m // tm, n // tn, k // tk),
                   pl.BlockSpec((tm, tk), lambda i, j, kk: (i, kk)),
                   pl.BlockSpec((tk, tn), lambda i, j, kk: (kk, j)),
                   pl.BlockSpec((tm, tn), lambda i, j, kk: (i, j)),
                   jax.ShapeDtypeStruct((m, n), out_dtype), NN,
                   res=res, res_spec=pl.BlockSpec((tm, tn), lambda i, j, kk: (i, j)))


def _mm_nt(name, a, b, tm, tn, tk, out_dtype=F32, res=None, after=None):
    m, k = a.shape
    n = b.shape[0]
    return _matmul(name, a, b, (m // tm, n // tn, k // tk),
                   pl.BlockSpec((tm, tk), lambda i, j, kk: (i, kk)),
                   pl.BlockSpec((tn, tk), lambda i, j, kk: (j, kk)),
                   pl.BlockSpec((tm, tn), lambda i, j, kk: (i, j)),
                   jax.ShapeDtypeStruct((m, n), out_dtype), NT,
                   res=res, res_spec=pl.BlockSpec((tm, tn), lambda i, j, kk: (i, j)), after=after)


def _mm_tn(name, a, b, tm, tn, tk, out_dtype=F32):
    k, m = a.shape
    n = b.shape[1]
    tk = min(tk, k)
    return _matmul(name, a, b, (m // tm, n // tn, k // tk),
                   pl.BlockSpec((tk, tm), lambda i, j, kk: (kk, i)),
                   pl.BlockSpec((tk, tn), lambda i, j, kk: (kk, j)),
                   pl.BlockSpec((tm, tn), lambda i, j, kk: (i, j)),
                   jax.ShapeDtypeStruct((m, n), out_dtype), TN)


def _up_shard(n):
    return 2 * (n % 2) + (n // 2) // 2, (n // 2) % 2


def _mm_up_fwd(name, a, wg_up, tm, tk):
    m, k = a.shape
    _, _, cols = wg_up.shape
    tn = cols // 2
    return _matmul(name, a, wg_up, (m // tm, 2 * N_CHIPS, k // tk),
                   pl.BlockSpec((tm, tk), lambda i, j, kk: (i, kk)),
                   pl.BlockSpec((None, tk, tn), lambda i, j, kk: (_up_shard(j)[0], kk, _up_shard(j)[1])),
                   pl.BlockSpec((tm, tn), lambda i, j, kk: (i, j)),
                   jax.ShapeDtypeStruct((m, 2 * N_CHIPS * tn), F32), NN)


def _mm_up_bwd_x(name, dgu, wg_up, tm, tn):
    m, _ = dgu.shape
    _, d, cols = wg_up.shape
    tk = cols // 2
    nk = N_CHIPS

    def body(a_ref, bg_ref, bu_ref, o_ref, acc_ref):
        k = pl.program_id(2)

        def part():
            dims = (NT, ((), ()))
            return (lax.dot_general(a_ref[:, :tk], bg_ref[...], dims, preferred_element_type=F32)
                    + lax.dot_general(a_ref[:, tk:], bu_ref[...], dims, preferred_element_type=F32))

        @pl.when(k == 0)
        def _():
            acc_ref[...] = part()

        @pl.when(jnp.logical_and(k > 0, k < nk - 1))
        def _():
            acc_ref[...] += part()

        @pl.when(k == nk - 1)
        def _():
            o_ref[...] = acc_ref[...] + part()

    wspec = lambda half: pl.BlockSpec(
        (None, tn, tk), lambda i, j, kk: (_up_shard(2 * kk + half)[0], j, _up_shard(2 * kk + half)[1]))
    return pl.pallas_call(
        body, name=name, grid=(m // tm, d // tn, nk), out_shape=jax.ShapeDtypeStruct((m, d), F32),
        in_specs=[pl.BlockSpec((tm, 2 * tk), lambda i, j, kk: (i, kk)), wspec(0), wspec(1)],
        out_specs=pl.BlockSpec((tm, tn), lambda i, j, kk: (i, j)),
        scratch_shapes=[pltpu.VMEM((tm, tn), F32)],
        compiler_params=_params(("parallel", "parallel", "arbitrary")),
    )(dgu, wg_up, wg_up)


def _mm_up_bwd_w(name, n2, dgu, tm, tk):
    s, d = n2.shape
    tk = min(tk, s)
    tn = dgu.shape[1] // (2 * N_CHIPS)
    return _matmul(name, n2, dgu, (d // tm, 2 * N_CHIPS, s // tk),
                   pl.BlockSpec((tk, tm), lambda i, j, kk: (kk, i)),
                   pl.BlockSpec((tk, tn), lambda i, j, kk: (kk, j)),
                   pl.BlockSpec((None, tm, tn), lambda i, j, kk: (_up_shard(j)[0], i, _up_shard(j)[1])),
                   jax.ShapeDtypeStruct((N_CHIPS, d, 2 * tn), F32), TN)


def _rmsnorm_fwd(name, x, g, tm=512):
    s, d = x.shape
    tm = min(tm, s)

    def body(x_ref, g_ref, n_ref, r_ref):
        xf = x_ref[...]
        r = lax.rsqrt(jnp.mean(xf * xf, axis=-1, keepdims=True) + EPS)
        n_ref[...] = ((xf * r) * g_ref[...]).astype(BF16)
        r_ref[...] = r

    return pl.pallas_call(
        body, grid=(s // tm,), name=name,
        in_specs=[pl.BlockSpec((tm, d), lambda i: (i, 0)), pl.BlockSpec((1, d), lambda i: (0, 0))],
        out_specs=[pl.BlockSpec((tm, d), lambda i: (i, 0)), pl.BlockSpec((tm, 1), lambda i: (i, 0))],
        out_shape=[jax.ShapeDtypeStruct((s, d), BF16), jax.ShapeDtypeStruct((s, 1), F32)],
        compiler_params=_params(("parallel",)),
    )(x, g)


def _rmsnorm_bwd(name, x, rstd, g, dn, dres, tm=512):
    s, d = x.shape
    tm = min(tm, s)

    def body(x_ref, r_ref, g_ref, dn_ref, dres_ref, dx_ref, dxb_ref, dg_ref):
        @pl.when(pl.program_id(0) == 0)
        def _():
            dg_ref[...] = jnp.zeros_like(dg_ref)

        r = r_ref[...]
        xhat = x_ref[...] * r
        dn_v = dn_ref[...]
        dxhat = dn_v * g_ref[...]
        dx = dres_ref[...] + r * (dxhat - xhat * jnp.mean(dxhat * xhat, axis=-1, keepdims=True))
        dx_ref[...] = dx
        dxb_ref[...] = dx.astype(BF16)
        dg_ref[...] += jnp.sum(dn_v * xhat, axis=0, keepdims=True)

    row = pl.BlockSpec((tm, d), lambda i: (i, 0))
    vec = pl.BlockSpec((1, d), lambda i: (0, 0))
    return pl.pallas_call(
        body, grid=(s // tm,), name=name,
        in_specs=[row, pl.BlockSpec((tm, 1), lambda i: (i, 0)), vec, row, row],
        out_specs=[row, row, vec],
        out_shape=[jax.ShapeDtypeStruct((s, d), F32), jax.ShapeDtypeStruct((s, d), BF16),
                   jax.ShapeDtypeStruct((1, d), F32)],
        compiler_params=_params(("arbitrary",)),
    )(x, rstd, g, dn, dres)


def _loss_head(name, x, g, target, tm=512):
    s, d = x.shape
    tm = min(tm, s)

    def body(x_ref, g_ref, t_ref, loss_ref, dx_ref, dxb_ref, dg_ref):
        @pl.when(pl.program_id(0) == 0)
        def _():
            dg_ref[...] = jnp.zeros_like(dg_ref)
            loss_ref[...] = jnp.zeros_like(loss_ref)

        xf = x_ref[...]
        gv = g_ref[...]
        r = lax.rsqrt(jnp.mean(xf * xf, axis=-1, keepdims=True) + EPS)
        xhat = xf * r
        err = xhat * gv - t_ref[...]
        loss_ref[...] += 0.5 * jnp.sum(jnp.mean(err * err, axis=-1, keepdims=True), axis=0, keepdims=True)
        dy = err * (1.0 / d)
        dxhat = dy * gv
        dx = r * (dxhat - xhat * jnp.mean(dxhat * xhat, axis=-1, keepdims=True))
        dx_ref[...] = dx
        dxb_ref[...] = dx.astype(BF16)
        dg_ref[...] += jnp.sum(dy * xhat, axis=0, keepdims=True)

    row = pl.BlockSpec((tm, d), lambda i: (i, 0))
    vec = pl.BlockSpec((1, d), lambda i: (0, 0))
    return pl.pallas_call(
        body, grid=(s // tm,), name=name,
        in_specs=[row, vec, row],
        out_specs=[pl.BlockSpec((1, 1), lambda i: (0, 0)), row, row, vec],
        out_shape=[jax.ShapeDtypeStruct((1, 1), F32), jax.ShapeDtypeStruct((s, d), F32),
                   jax.ShapeDtypeStruct((s, d), BF16), jax.ShapeDtypeStruct((1, d), F32)],
        compiler_params=_params(("arbitrary",)),
    )(x, g, target)


def _sigmoid(v):
    return 1.0 / (1.0 + jnp.exp(-v))


def _log_sigmoid(v):
    return jnp.minimum(v, 0.0) - jnp.log1p(jnp.exp(-jnp.abs(v)))


def _softplus(v):
    return jnp.maximum(v, 0.0) + jnp.log1p(jnp.exp(-jnp.abs(v)))


def _one_minus_exp(z):
    series = -z * (1.0 + z * (0.5 + z * (1.0 / 6.0 + z * (1.0 / 24.0 + z * (1.0 / 120.0)))))
    return jnp.where(z > -0.1, series, 1.0 - jnp.exp(z))


_GELU_K = 0.7978845608028654
_GELU_C = 0.044715


def _gelu(v):
    return 0.5 * v * (1.0 + jnp.tanh(_GELU_K * (v + _GELU_C * v * v * v)))


def _gelu_grad(v):
    t = jnp.tanh(_GELU_K * (v + _GELU_C * v * v * v))
    return 0.5 * (1.0 + t) + 0.5 * v * (1.0 - t * t) * _GELU_K * (1.0 + 3.0 * _GELU_C * v * v)


def _rows(shape):
    return lax.broadcasted_iota(jnp.int32, shape, 0)


def _cols(shape):
    return lax.broadcasted_iota(jnp.int32, shape, 1)


def _shift_down(v, prev, d):
    if d == 0:
        return v
    rolled = pltpu.roll(v, d, axis=0)
    head = jnp.where(_rows((HALO, v.shape[1])) >= d, rolled[:HALO], pltpu.roll(prev, d, axis=0))
    if v.shape[0] == HALO:
        return head
    return jnp.concatenate([head, rolled[HALO:]], axis=0)


def _shift_up(v, nxt, d):
    if d == 0:
        return v
    n = v.shape[0]
    rolled = pltpu.roll(v, n - d, axis=0)
    tail = jnp.where(_rows((HALO, v.shape[1])) < HALO - d, rolled[n - HALO:], pltpu.roll(nxt, HALO - d, axis=0))
    if n == HALO:
        return tail
    return jnp.concatenate([rolled[:n - HALO], tail], axis=0)


def _dot(a, b, contract):
    return lax.dot_general(a.astype(BF16), b.astype(BF16), (contract, ((), ())), preferred_element_type=F32)


def _mlstm_chunk_common(h, q_ref, k_ref, v_ref, gcol_ref, grow_ref, brow_ref, bcol_ref, m_prev):
    L = CHUNK
    sl = slice(h * HEAD_DIM, (h + 1) * HEAD_DIM)
    qh = q_ref[:, sl]
    kh = k_ref[:, sl]
    vh = v_ref[:, sl]
    qs = qh * (HEAD_DIM ** -0.5)
    gates = gcol_ref[...] + brow_ref[...]
    lane = _cols(gates.shape)
    ic = jnp.sum(jnp.where(lane == h, gates, 0.0), axis=1, keepdims=True)
    fc = jnp.sum(jnp.where(lane == HEADS + h, gates, 0.0), axis=1, keepdims=True)
    ir = grow_ref[h:h + 1, :] + bcol_ref[h:h + 1, :]
    fr = grow_ref[HEADS + h:HEADS + h + 1, :] + bcol_ref[HEADS + h:HEADS + h + 1, :]
    logf_c = _log_sigmoid(fc)
    logf_r = _log_sigmoid(fr)
    t_i = _rows((L, L))
    s_i = _cols((L, L))
    tri = t_i >= s_i
    b_c = jnp.sum(jnp.where(tri, logf_r, 0.0), axis=1, keepdims=True)
    b_r = jnp.sum(jnp.where(t_i <= s_i, logf_c, 0.0), axis=0, keepdims=True)
    btot = jnp.sum(logf_r, axis=1, keepdims=True)
    dmat = jnp.where(tri, b_c - b_r + ir, -jnp.inf)
    m_inter = b_c + m_prev
    m_t = jnp.maximum(m_inter, jnp.max(dmat, axis=1, keepdims=True))
    e_mat = jnp.exp(dmat - m_t)
    e_inter = jnp.exp(m_inter - m_t)
    wqk = _dot(qs, kh, NT) * e_mat
    w_end_r = btot - b_r + ir
    m_loc = jnp.max(w_end_r, axis=1, keepdims=True)
    e_end_c = jnp.exp(btot - b_c + ic - m_loc)
    m_new = jnp.maximum(btot + m_prev, m_loc)
    a_dec = jnp.exp(btot + m_prev - m_new)
    c_inj = jnp.exp(m_loc - m_new)
    return dict(qh=qh, kh=kh, vh=vh, qs=qs, fc=fc, tri=tri, t_i=t_i, s_i=s_i, m_t=m_t, e_mat=e_mat,
                e_inter=e_inter, wqk=wqk, e_end_c=e_end_c, m_new=m_new, a_dec=a_dec, c_inj=c_inj)


def _mlstm_fwd(proj, gates_t, bias_row, bias_col, head_g):
    s = proj.shape[0]
    nc = s // CHUNK
    L = CHUNK

    def body(q_ref, k_ref, v_ref, o_ref, gcol_ref, grow_ref, brow_ref, bcol_ref, hg_ref,
             out_ref, cprev_ref, nprev_ref, mprev_ref, c_scr, n_scr, m_scr):
        @pl.when(pl.program_id(0) == 0)
        def _():
            c_scr[...] = jnp.zeros_like(c_scr)
            n_scr[...] = jnp.zeros_like(n_scr)
            m_scr[...] = jnp.zeros_like(m_scr)

        for h in range(HEADS):
            sl = slice(h * HEAD_DIM, (h + 1) * HEAD_DIM)
            m_prev = m_scr[h:h + 1, 0:1]
            n_prev = n_scr[h:h + 1, :]
            c_prev = c_scr[h].astype(BF16)
            q = _mlstm_chunk_common(h, q_ref, k_ref, v_ref, gcol_ref, grow_ref, brow_ref, bcol_ref, m_prev)
            num = _dot(q["wqk"], q["vh"], NN) + q["e_inter"] * _dot(q["qs"], c_prev, NN)
            den = (jnp.sum(q["wqk"], axis=1, keepdims=True)
                   + q["e_inter"] * jnp.sum(q["qs"] * n_prev, axis=1, keepdims=True))
            hh = num / jnp.maximum(jnp.abs(den), jnp.exp(-q["m_t"]))
            hn = hh * lax.rsqrt(jnp.mean(hh * hh, axis=1, keepdims=True) + EPS) * hg_ref[h:h + 1, :]
            out_ref[:, sl] = (_sigmoid(o_ref[:, sl]) * hn).astype(BF16)
            cprev_ref[h] = c_prev
            nprev_ref[h:h + 1, :] = n_prev
            mprev_ref[h:h + 1, :] = jnp.broadcast_to(m_prev, (1, LANES))
            c_loc = _dot(q["kh"], q["e_end_c"] * q["vh"], TN)
            n_loc = jnp.sum(q["e_end_c"] * q["kh"], axis=0, keepdims=True)
            c_scr[h] = q["a_dec"] * c_scr[h] + q["c_inj"] * c_loc
            n_scr[h:h + 1, :] = q["a_dec"] * n_prev + q["c_inj"] * n_loc
            m_scr[h:h + 1, :] = jnp.broadcast_to(q["m_new"], (1, LANES))

    blk = lambda j: pl.BlockSpec((L, D_MLSTM), lambda c, j=j: (c, j))
    full = lambda shp: pl.BlockSpec(shp, lambda c: tuple(0 for _ in shp))
    return pl.pallas_call(
        body, grid=(nc,), name="mlstm_fwd",
        in_specs=[blk(0), blk(1), blk(2), blk(3),
                  pl.BlockSpec((L, LANES), lambda c: (c, (4 * D_MLSTM + 2 * D_LRU) // LANES)),
                  pl.BlockSpec((None, 2 * HEADS, L), lambda c: (c, 0, 0)),
                  full((1, LANES)), full((2 * HEADS, 1)), full((HEADS, HEAD_DIM))],
        out_specs=[pl.BlockSpec((L, D_MLSTM), lambda c: (c, 0)),
                   pl.BlockSpec((None, HEADS, HEAD_DIM, HEAD_DIM), lambda c: (c, 0, 0, 0)),
                   pl.BlockSpec((None, HEADS, HEAD_DIM), lambda c: (c, 0, 0)),
                   pl.BlockSpec((None, HEADS, LANES), lambda c: (c, 0, 0))],
        out_shape=[jax.ShapeDtypeStruct((s, D_MLSTM + D_LRU), BF16),
                   jax.ShapeDtypeStruct((nc, HEADS, HEAD_DIM, HEAD_DIM), BF16),
                   jax.ShapeDtypeStruct((nc, HEADS, HEAD_DIM), F32),
                   jax.ShapeDtypeStruct((nc, HEADS, LANES), F32)],
        scratch_shapes=[pltpu.VMEM((HEADS, HEAD_DIM, HEAD_DIM), F32), pltpu.VMEM((HEADS, HEAD_DIM), F32),
                        pltpu.VMEM((HEADS, LANES), F32)],
        compiler_params=_params(("arbitrary",)),
    )(proj, proj, proj, proj, proj, gates_t, bias_row, bias_col, head_g)


def _mlstm_bwd(proj, gates_t, bias_row, bias_col, head_g, cprev, nprev, mprev, dmix):
    s = proj.shape[0]
    nc = s // CHUNK
    L = CHUNK

    def body(q_ref, k_ref, v_ref, o_ref, gcol_ref, grow_ref, brow_ref, bcol_ref, hg_ref,
             cprev_ref, nprev_ref, mprev_ref, dmix_ref,
             dqkvo_ref, dgate_ref, dhg_ref, g_scr, gn_scr):
        @pl.when(pl.program_id(0) == 0)
        def _():
            g_scr[...] = jnp.zeros_like(g_scr)
            gn_scr[...] = jnp.zeros_like(gn_scr)
            dhg_ref[...] = jnp.zeros_like(dhg_ref)

        lane = _cols((L, LANES))
        dgate = jnp.zeros((L, LANES), F32)
        for h in range(HEADS):
            sl = slice(h * HEAD_DIM, (h + 1) * HEAD_DIM)
            m_prev = mprev_ref[h:h + 1, 0:1]
            n_prev = nprev_ref[h:h + 1, :]
            c_prev = cprev_ref[h]
            q = _mlstm_chunk_common(h, q_ref, k_ref, v_ref, gcol_ref, grow_ref, brow_ref, bcol_ref, m_prev)
            qh, kh, vh, qs, wqk, e_inter = q["qh"], q["kh"], q["vh"], q["qs"], q["wqk"], q["e_inter"]
            num_state = e_inter * _dot(qs, c_prev, NN)
            den_state = e_inter * jnp.sum(qs * n_prev, axis=1, keepdims=True)
            num = _dot(wqk, vh, NN) + num_state
            den = jnp.sum(wqk, axis=1, keepdims=True) + den_state
            floor = jnp.exp(-q["m_t"])
            denom = jnp.maximum(jnp.abs(den), floor)
            hh = num / denom
            rn = lax.rsqrt(jnp.mean(hh * hh, axis=1, keepdims=True) + EPS)
            hn_pre = hh * rn
            hg = hg_ref[h:h + 1, :]
            sg = _sigmoid(o_ref[:, sl])
            dout = dmix_ref[:, sl]
            d_o = dout * (hn_pre * hg) * sg * (1.0 - sg)
            dhn = dout * sg
            dhg_ref[h:h + 1, :] += jnp.sum(dhn * hn_pre, axis=0, keepdims=True)
            dhn_pre = dhn * hg
            dhh = rn * (dhn_pre - hn_pre * jnp.mean(dhn_pre * hn_pre, axis=1, keepdims=True))
            dnum = dhh / denom
            dden = jnp.where(jnp.abs(den) >= floor,
                             -jnp.sum(hh * dhh, axis=1, keepdims=True) / denom * jnp.sign(den), 0.0)
            dwqk = _dot(dnum, vh, NT) + dden
            dv = _dot(wqk, dnum, TN)
            dp = dwqk * q["e_mat"]
            dqs = _dot(dp, kh, NN) + e_inter * (_dot(dnum, c_prev, NT) + dden * n_prev)
            dk = _dot(dp, qs, TN)
            g_next = g_scr[h]
            gn_next = gn_scr[h:h + 1, :]
            w_state = q["e_end_c"] * q["c_inj"]
            dk_state = w_state * (_dot(vh, g_next, NT) + gn_next)
            dk = dk + dk_state
            dv = dv + w_state * _dot(kh, g_next, NN)
            dq = dqs * (HEAD_DIM ** -0.5)
            eye = q["t_i"] == q["s_i"]
            to_row = lambda col: jnp.sum(jnp.where(eye, col, 0.0), axis=0, keepdims=True)
            to_col = lambda row: jnp.sum(jnp.where(eye, row, 0.0), axis=1, keepdims=True)
            g_pair = dwqk * wqk
            rs_in = jnp.sum(g_pair, axis=1, keepdims=True)
            cs_in_r = jnp.sum(g_pair, axis=0, keepdims=True)
            rs_state = (jnp.sum(dnum * num_state, axis=1, keepdims=True) + dden * den_state)
            cs_state = jnp.sum(kh * dk_state, axis=1, keepdims=True)
            di_c = to_col(cs_in_r) + cs_state
            through = q["a_dec"] * (jnp.sum(jnp.sum(g_next * c_prev.astype(F32), axis=1, keepdims=True),
                                            axis=0, keepdims=True)
                                    + jnp.sum(gn_next * n_prev, axis=1, keepdims=True))
            ends_here = to_row(rs_in + rs_state) - cs_in_r
            da_c = (jnp.sum(jnp.where(q["s_i"] >= q["t_i"], ends_here, 0.0), axis=1, keepdims=True)
                    + jnp.sum(jnp.where(q["s_i"] < q["t_i"], to_row(cs_state), 0.0), axis=1, keepdims=True)
                    + through)
            df_c = da_c * _sigmoid(-q["fc"])
            dgate = dgate + jnp.where(lane == h, di_c, 0.0) + jnp.where(lane == HEADS + h, df_c, 0.0)
            dqkvo_ref[:, sl] = dq.astype(BF16)
            dqkvo_ref[:, D_MLSTM + h * HEAD_DIM:D_MLSTM + (h + 1) * HEAD_DIM] = dk.astype(BF16)
            dqkvo_ref[:, 2 * D_MLSTM + h * HEAD_DIM:2 * D_MLSTM + (h + 1) * HEAD_DIM] = dv.astype(BF16)
            dqkvo_ref[:, 3 * D_MLSTM + h * HEAD_DIM:3 * D_MLSTM + (h + 1) * HEAD_DIM] = d_o.astype(BF16)
            g_scr[h] = q["a_dec"] * g_next + _dot(e_inter * qs, dnum, TN)
            gn_scr[h:h + 1, :] = q["a_dec"] * gn_next + jnp.sum(e_inter * qs * dden, axis=0, keepdims=True)
        dgate_ref[...] = dgate

    rev = lambda c: nc - 1 - c
    blk = lambda j: pl.BlockSpec((L, D_MLSTM), lambda c, j=j: (rev(c), j))
    full = lambda shp: pl.BlockSpec(shp, lambda c: tuple(0 for _ in shp))
    return pl.pallas_call(
        body, grid=(nc,), name="mlstm_bwd",
        in_specs=[blk(0), blk(1), blk(2), blk(3),
                  pl.BlockSpec((L, LANES), lambda c: (rev(c), (4 * D_MLSTM + 2 * D_LRU) // LANES)),
                  pl.BlockSpec((None, 2 * HEADS, L), lambda c: (rev(c), 0, 0)),
                  full((1, LANES)), full((2 * HEADS, 1)), full((HEADS, HEAD_DIM)),
                  pl.BlockSpec((None, HEADS, HEAD_DIM, HEAD_DIM), lambda c: (rev(c), 0, 0, 0)),
                  pl.BlockSpec((None, HEADS, HEAD_DIM), lambda c: (rev(c), 0, 0)),
                  pl.BlockSpec((None, HEADS, LANES), lambda c: (rev(c), 0, 0)),
                  pl.BlockSpec((L, D_MLSTM), lambda c: (rev(c), 0))],
        out_specs=[pl.BlockSpec((L, 4 * D_MLSTM), lambda c: (rev(c), 0)),
                   pl.BlockSpec((L, LANES), lambda c: (rev(c), 0)),
                   full((HEADS, HEAD_DIM))],
        out_shape=[jax.ShapeDtypeStruct((s, _PROJ_PAD), BF16),
                   jax.ShapeDtypeStruct((s, LANES), F32),
                   jax.ShapeDtypeStruct((HEADS, HEAD_DIM), F32)],
        scratch_shapes=[pltpu.VMEM((HEADS, HEAD_DIM, HEAD_DIM), F32), pltpu.VMEM((HEADS, HEAD_DIM), F32)],
        compiler_params=_params(("arbitrary",)),
    )(proj, proj, proj, proj, proj, gates_t, bias_row, bias_col, head_g, cprev, nprev, mprev, dmix)


def _lru_gates(xc, wa_ref, wx_ref, ba, bx, lam):
    r = _sigmoid(_dot(xc, wa_ref[...], NN) + ba)
    ig = _sigmoid(_dot(xc, wx_ref[...], NN) + bx)
    sp = _softplus(-lam)
    log_a = (-LRU_C * r) * sp
    a = jnp.exp(log_a)
    mult = jnp.sqrt(_one_minus_exp(2.0 * log_a))
    return r, ig, sp, a, mult


def _lru_conv(xr, prev, w_ref, b):
    xc = b + _shift_down(xr, prev, 3) * w_ref[0:1, :]
    for j in range(1, LRU_CONV):
        xc = xc + _shift_down(xr, prev, LRU_CONV - 1 - j) * w_ref[j:j + 1, :]
    return xc


def _lru_fwd(proj, mix, conv_w, conv_b, wa, wx, ba, bx, lam, tt=1024):
    s = proj.shape[0]
    tt = min(tt, s)
    nt = s // tt
    B = LRU_BLOCK_DIM
    lru_col = 4 * D_MLSTM // B
    mix_col = D_MLSTM // B

    def body(xr_ref, gr_ref, cw_ref, cb_ref, wa_ref, wx_ref, ba_ref, bx_ref, lam_ref, mix_in_ref,
             out_ref, h_ref, prev_scr, hcar_scr):
        @pl.when(pl.program_id(1) == 0)
        def _():
            prev_scr[...] = jnp.zeros_like(prev_scr)
            hcar_scr[...] = jnp.zeros_like(hcar_scr)

        xr = xr_ref[...]
        xc = _lru_conv(xr, prev_scr[...], cw_ref, cb_ref[...])
        prev_scr[...] = xr[tt - HALO:, :]
        _, ig, _, a, mult = _lru_gates(xc, wa_ref, wx_ref, ba_ref[...], bx_ref[...], lam_ref[...])
        u = mult * (ig * xc)
        rows = _rows((tt, B))
        acc_a, acc_b = a, u
        d = 1
        while d < tt:
            if d < HALO:
                keep = rows >= d
                sh_a = jnp.where(keep, pltpu.roll(acc_a, d, axis=0), 1.0)
                sh_b = jnp.where(keep, pltpu.roll(acc_b, d, axis=0), 0.0)
            else:
                sh_a = jnp.concatenate([jnp.ones((d, B), F32), acc_a[:tt - d]], axis=0)
                sh_b = jnp.concatenate([jnp.zeros((d, B), F32), acc_b[:tt - d]], axis=0)
            acc_b = acc_a * sh_b + acc_b
            acc_a = acc_a * sh_a
            d *= 2
        hv = acc_b + acc_a * hcar_scr[0:1, :]
        hcar_scr[...] = jnp.broadcast_to(hv[tt - 1:tt, :], hcar_scr.shape)
        h_ref[...] = hv
        out_ref[...] = (hv * _gelu(gr_ref[...])).astype(BF16)

    chan = lambda rws: pl.BlockSpec((rws, B), lambda n, i: (0, n))
    return pl.pallas_call(
        body, grid=(LRU_BLOCKS, nt), name="lru_fwd",
        in_specs=[pl.BlockSpec((tt, B), lambda n, i: (i, lru_col + 2 * n)),
                  pl.BlockSpec((tt, B), lambda n, i: (i, lru_col + 2 * n + 1)),
                  chan(LRU_CONV), chan(1),
                  pl.BlockSpec((None, B, B), lambda n, i: (n, 0, 0)),
                  pl.BlockSpec((None, B, B), lambda n, i: (n, 0, 0)),
                  chan(1), chan(1), chan(1), ANY],
        out_specs=[pl.BlockSpec((tt, B), lambda n, i: (i, mix_col + n)), pl.BlockSpec((tt, B), lambda n, i: (i, n))],
        out_shape=[jax.ShapeDtypeStruct(mix.shape, BF16), jax.ShapeDtypeStruct((s, D_LRU), F32)],
        scratch_shapes=[pltpu.VMEM((HALO, B), F32), pltpu.VMEM((HALO, B), F32)],
        input_output_aliases={9: 0},
        compiler_params=_params(("parallel", "arbitrary")),
    )(proj, proj, conv_w, conv_b, wa, wx, ba, bx, lam, mix)


def _lru_bwd(proj, hsave, dmix, dproj, conv_w, conv_b, wa, wx, ba, bx, lam, tt=1024):
    s = proj.shape[0]
    tt = min(tt, s)
    nt = s // tt
    B = LRU_BLOCK_DIM
    lru_col = 4 * D_MLSTM // B
    dmix_col = D_MLSTM // B
    hpb = tt // HALO

    def body(xr_ref, xprev_ref, gr_ref, h_ref, hprev_ref, dmix_ref, cw_ref, cb_ref, wa_ref, wx_ref,
             ba_ref, bx_ref, lam_ref, dproj_in_ref,
             dxg_ref, dcw_ref, dcb_ref, dwa_ref, dwx_ref, dba_ref, dbx_ref, dlam_ref,
             gcar_scr, acar_scr, dxc_scr):
        i = pl.program_id(1)
        first_tile = i == nt - 1

        @pl.when(i == 0)
        def _():
            gcar_scr[...] = jnp.zeros_like(gcar_scr)
            acar_scr[...] = jnp.zeros_like(acar_scr)
            dxc_scr[...] = jnp.zeros_like(dxc_scr)
            for ref in (dcw_ref, dcb_ref, dwa_ref, dwx_ref, dba_ref, dbx_ref, dlam_ref):
                ref[...] = jnp.zeros_like(ref)

        xr = xr_ref[...]
        xprev = jnp.where(first_tile, 0.0, xprev_ref[...])
        hprev = jnp.where(first_tile, 0.0, hprev_ref[...])
        lam = lam_ref[...]
        taps = [_shift_down(xr, xprev, LRU_CONV - 1 - j) for j in range(LRU_CONV)]
        xc = cb_ref[...] + taps[0] * cw_ref[0:1, :]
        for j in range(1, LRU_CONV):
            xc = xc + taps[j] * cw_ref[j:j + 1, :]
        r, ig, sp, a, mult = _lru_gates(xc, wa_ref, wx_ref, ba_ref[...], bx_ref[...], lam)
        gr = gr_ref[...]
        hv = h_ref[...]
        dout = dmix_ref[...]
        dxg_ref[:, B:] = (dout * hv * _gelu_grad(gr)).astype(BF16)
        dh = dout * _gelu(gr)
        rows = _rows((tt, B))
        acc_a = _shift_up(a, acar_scr[...], 1)
        acc_b = dh
        d = 1
        while d < tt:
            if d < HALO:
                keep = rows < tt - d
                sh_a = jnp.where(keep, pltpu.roll(acc_a, tt - d, axis=0), 1.0)
                sh_b = jnp.where(keep, pltpu.roll(acc_b, tt - d, axis=0), 0.0)
            else:
                sh_a = jnp.concatenate([acc_a[d:], jnp.ones((d, B), F32)], axis=0)
                sh_b = jnp.concatenate([acc_b[d:], jnp.zeros((d, B), F32)], axis=0)
            acc_b = acc_a * sh_b + acc_b
            acc_a = acc_a * sh_a
            d *= 2
        gv = acc_b + acc_a * gcar_scr[0:1, :]
        gcar_scr[...] = jnp.broadcast_to(gv[0:1, :], gcar_scr.shape)
        acar_scr[...] = jnp.broadcast_to(a[0:1, :], acar_scr.shape)
        h_before = _shift_down(hv, hprev, 1)
        da = gv * h_before
        dmult = gv * (ig * xc)
        dig = gv * mult * xc
        dxc = gv * mult * ig
        dlog_a = da * a - dmult * (a * a) / mult
        dr = dlog_a * (-LRU_C * sp)
        dlam_ref[...] += jnp.sum(dlog_a * (-LRU_C * r), axis=0, keepdims=True) * (-_sigmoid(-lam))
        dpre_r = dr * r * (1.0 - r)
        dpre_i = dig * ig * (1.0 - ig)
        dba_ref[...] += jnp.sum(dpre_r, axis=0, keepdims=True)
        dbx_ref[...] += jnp.sum(dpre_i, axis=0, keepdims=True)
        dwa_ref[...] += _dot(xc, dpre_r, TN)
        dwx_ref[...] += _dot(xc, dpre_i, TN)
        dxc = dxc + _dot(dpre_r, wa_ref[...], NT) + _dot(dpre_i, wx_ref[...], NT)
        dcb_ref[...] += jnp.sum(dxc, axis=0, keepdims=True)
        nxt = dxc_scr[...]
        dxr = jnp.zeros((tt, B), F32)
        for j in range(LRU_CONV):
            sft = LRU_CONV - 1 - j
            dcw_ref[j:j + 1, :] += jnp.sum(dxc * taps[j], axis=0, keepdims=True)
            dxr = dxr + _shift_up(dxc, nxt, sft) * cw_ref[j:j + 1, :]
        dxc_scr[...] = dxc[:HALO, :]
        dxg_ref[:, :B] = dxr.astype(BF16)

    rev = lambda i: nt - 1 - i
    tile = lambda col, step: pl.BlockSpec((tt, B), lambda n, i: (rev(i), col + step * n))
    halo = lambda col, step: pl.BlockSpec(
        (HALO, B), lambda n, i: (jnp.maximum(rev(i) * hpb - 1, 0), col + step * n))
    chan = lambda rws: pl.BlockSpec((rws, B), lambda n, i: (0, n))
    wblk = pl.BlockSpec((None, B, B), lambda n, i: (n, 0, 0))
    return pl.pallas_call(
        body, grid=(LRU_BLOCKS, nt), name="lru_bwd",
        in_specs=[tile(lru_col, 2), halo(lru_col, 2), tile(lru_col + 1, 2), tile(0, 1), halo(0, 1),
                  tile(dmix_col, 1), chan(LRU_CONV), chan(1), wblk, wblk, chan(1), chan(1), chan(1), ANY],
        out_specs=[pl.BlockSpec((tt, 2 * B), lambda n, i: (rev(i), lru_col // 2 + n)),
                   chan(LRU_CONV), chan(1), wblk, wblk, chan(1), chan(1), chan(1)],
        out_shape=[jax.ShapeDtypeStruct(dproj.shape, BF16),
                   jax.ShapeDtypeStruct((LRU_CONV, D_LRU), F32), jax.ShapeDtypeStruct((1, D_LRU), F32),
                   jax.ShapeDtypeStruct((LRU_BLOCKS, B, B), F32), jax.ShapeDtypeStruct((LRU_BLOCKS, B, B), F32),
                   jax.ShapeDtypeStruct((1, D_LRU), F32), jax.ShapeDtypeStruct((1, D_LRU), F32),
                   jax.ShapeDtypeStruct((1, D_LRU), F32)],
        scratch_shapes=[pltpu.VMEM((HALO, B), F32), pltpu.VMEM((HALO, B), F32), pltpu.VMEM((HALO, B), F32)],
        input_output_aliases={13: 0},
        compiler_params=_params(("parallel", "arbitrary")),
    )(proj, proj, proj, hsave, hsave, dmix, conv_w, conv_b, wa, wx, ba, bx, lam, dproj)


def _ffn_conv(gp, prev, w_ref, b):
    g = b + _shift_down(gp, prev, 2) * w_ref[0:1, :]
    for j in range(1, FFN_CONV):
        g = g + _shift_down(gp, prev, FFN_CONV - 1 - j) * w_ref[j:j + 1, :]
    return g


def _ffn_act_fwd(gu, conv_w, conv_b, tt=1024):
    s = gu.shape[0]
    tt = min(tt, s)
    d_ff = conv_w.shape[1]
    tc = d_ff // N_CHIPS
    hpb = tt // HALO

    def body(g_ref, gprev_ref, u_ref, w_ref, b_ref, act_ref):
        prev = jnp.where(pl.program_id(0) == 0, 0.0, gprev_ref[...])
        gate = _ffn_conv(g_ref[...], prev, w_ref, b_ref[...])
        act_ref[...] = (gate * _sigmoid(gate) * u_ref[...]).astype(BF16)

    return pl.pallas_call(
        body, grid=(s // tt, N_CHIPS), name="ffn_act_fwd",
        in_specs=[pl.BlockSpec((tt, tc), lambda i, j: (i, 2 * j)),
                  pl.BlockSpec((HALO, tc), lambda i, j: (jnp.maximum(i * hpb - 1, 0), 2 * j)),
                  pl.BlockSpec((tt, tc), lambda i, j: (i, 2 * j + 1)),
                  pl.BlockSpec((FFN_CONV, tc), lambda i, j: (0, j)),
                  pl.BlockSpec((1, tc), lambda i, j: (0, j))],
        out_specs=pl.BlockSpec((tt, tc), lambda i, j: (i, j)),
        out_shape=jax.ShapeDtypeStruct((s, d_ff), BF16),
        compiler_params=_params(("parallel", "parallel")),
    )(gu, gu, gu, conv_w, conv_b)


def _ffn_act_bwd(gu, dact, conv_w, conv_b, tt=512):
    s = gu.shape[0]
    tt = min(tt, s)
    nt = s // tt
    d_ff = conv_w.shape[1]
    tc = d_ff // N_CHIPS
    hpb = tt // HALO

    def dgate_of(gate, up, da):
        sg = _sigmoid(gate)
        return da * up * (sg * (1.0 + gate * (1.0 - sg))), da * (gate * sg)

    def body(g_ref, gprev_ref, gnext_ref, u_ref, unext_ref, da_ref, danext_ref, w_ref, b_ref,
             dgu_ref, dw_ref, db_ref):
        i = pl.program_id(1)

        @pl.when(i == 0)
        def _():
            dw_ref[...] = jnp.zeros_like(dw_ref)
            db_ref[...] = jnp.zeros_like(db_ref)

        gp = g_ref[...]
        prev = jnp.where(i == 0, 0.0, gprev_ref[...])
        bias = b_ref[...]
        taps = [_shift_down(gp, prev, FFN_CONV - 1 - j) for j in range(FFN_CONV)]
        gate = bias + taps[0] * w_ref[0:1, :]
        for j in range(1, FFN_CONV):
            gate = gate + taps[j] * w_ref[j:j + 1, :]
        dgate, dup = dgate_of(gate, u_ref[...], da_ref[...])
        gate_n = _ffn_conv(gnext_ref[...], gp[tt - HALO:, :], w_ref, bias)
        dgate_n, _ = dgate_of(gate_n, unext_ref[...], danext_ref[...])
        dgate_n = jnp.where(i == nt - 1, 0.0, dgate_n)
        db_ref[...] += jnp.sum(dgate, axis=0, keepdims=True)
        dgp = jnp.zeros((tt, tc), F32)
        for j in range(FFN_CONV):
            dw_ref[j:j + 1, :] += jnp.sum(dgate * taps[j], axis=0, keepdims=True)
            dgp = dgp + _shift_up(dgate, dgate_n, FFN_CONV - 1 - j) * w_ref[j:j + 1, :]
        dgu_ref[:, :tc] = dgp.astype(BF16)
        dgu_ref[:, tc:] = dup.astype(BF16)

    tile = lambda half: pl.BlockSpec((tt, tc), lambda j, i, half=half: (i, 2 * j + half))
    hprev = lambda half: pl.BlockSpec((HALO, tc), lambda j, i, half=half: (jnp.maximum(i * hpb - 1, 0), 2 * j + half))
    hnext = lambda half: pl.BlockSpec(
        (HALO, tc), lambda j, i, half=half: (jnp.minimum((i + 1) * hpb, nt * hpb - 1), 2 * j + half))
    return pl.pallas_call(
        body, grid=(N_CHIPS, nt), name="ffn_act_bwd",
        in_specs=[tile(0), hprev(0), hnext(0), tile(1), hnext(1),
                  pl.BlockSpec((tt, tc), lambda j, i: (i, j)),
                  pl.BlockSpec((HALO, tc), lambda j, i: (jnp.minimum((i + 1) * hpb, nt * hpb - 1), j)),
                  pl.BlockSpec((FFN_CONV, tc), lambda j, i: (0, j)),
                  pl.BlockSpec((1, tc), lambda j, i: (0, j))],
        out_specs=[pl.BlockSpec((tt, 2 * tc), lambda j, i: (i, j)),
                   pl.BlockSpec((FFN_CONV, tc), lambda j, i: (0, j)),
                   pl.BlockSpec((1, tc), lambda j, i: (0, j))],
        out_shape=[jax.ShapeDtypeStruct((s, 2 * d_ff), BF16),
                   jax.ShapeDtypeStruct((FFN_CONV, d_ff), F32), jax.ShapeDtypeStruct((1, d_ff), F32)],
        compiler_params=_params(("parallel", "arbitrary")),
    )(gu, gu, gu, gu, gu, dact, dact, conv_w, conv_b)


def _gate_grads(dgate, dproj, tm=512):
    s, n = dgate.shape
    tm = min(tm, s)

    def body(a_ref, dproj_in_ref, o_ref, dproj_ref):
        @pl.when(pl.program_id(0) == 0)
        def _():
            o_ref[...] = jnp.zeros_like(o_ref)
        a = a_ref[...]
        o_ref[...] += jnp.sum(a, axis=0, keepdims=True)
        dproj_ref[...] = a.astype(BF16)

    return pl.pallas_call(
        body, grid=(s // tm,), name="gate_grads",
        in_specs=[pl.BlockSpec((tm, n), lambda i: (i, 0)), ANY],
        out_specs=[pl.BlockSpec((1, n), lambda i: (0, 0)),
                   pl.BlockSpec((tm, n), lambda i: (i, (_QKVO + 2 * D_LRU) // LANES))],
        out_shape=[jax.ShapeDtypeStruct((1, n), F32), jax.ShapeDtypeStruct(dproj.shape, BF16)],
        input_output_aliases={1: 1},
        compiler_params=_params(("arbitrary",)),
    )(dgate, dproj)


def _pick(n, *cands):
    for c in cands:
        if n % c == 0:
            return c
    raise ValueError(f"no tile for {n}")


def _behind(a, token):
    return a if token is None else a + token[0:1, 0:1].astype(a.dtype).reshape((1,) * a.ndim)


class _Gathered:
    def __init__(self, w):
        self.w = w

    def begin(self):
        return None

    def mid(self, grp, after):
        return None

    def end(self, grp, after):
        return self.w

    def reduce_early(self, grads):
        return None

    def reduce_early_mid(self, after):
        return None

    def reduce_late(self, grads):
        return None

    def reduce_late_mid(self, after):
        return None


def _local_step(x, target, w, comm):
    s, d = x.shape
    nc = s // CHUNK
    tm = _pick(s, 1024, 512, 256)
    tn_proj = _pick(_PROJ_PAD, 896)
    gate_col = 4 * D_MLSTM + 2 * D_LRU
    w = dict(w)

    token = comm.begin()
    n1, rstd1 = _rmsnorm_fwd("norm_mix_fwd", x, _behind(w["norm_mix_g"], token))
    comm.mid(0, n1)
    w.update(comm.end(0, None))
    proj = _mm_nn("proj_fwd", n1, w["w_in"], tm, tn_proj, d)
    token = comm.mid(1, proj)
    gates = proj[:, gate_col:gate_col + 2 * HEADS]
    gates_t = gates.reshape(nc, CHUNK, 2 * HEADS).transpose(0, 2, 1)
    bias_row = _behind(jnp.pad(w["b_gate_m"], ((0, 0), (0, LANES - 2 * HEADS))), token)
    bias_col = w["b_gate_m"].reshape(2 * HEADS, 1)
    mix, cprev, nprev, mprev = _mlstm_fwd(proj, gates_t, bias_row, bias_col, w["mlstm_norm_g"])
    mix, hsave = _lru_fwd(proj, mix, w["lru_conv_w"], w["lru_conv_b"], w["lru_wa"], w["lru_wx"],
                          w["lru_ba"], w["lru_bx"], w["lru_lambda"])
    w.update(comm.end(1, hsave))
    token = comm.mid(2, hsave)
    x1 = _mm_nn("out_fwd", mix, w["w_out"], tm, 1024, d, res=x)
    n2, rstd2 = _rmsnorm_fwd("norm_ffn_fwd", x1, _behind(w["norm_ffn_g"], token))
    w.update(comm.end(2, n2))
    token = comm.mid(3, n2)
    gu = _mm_up_fwd("up_fwd", n2, w["w_up"], tm, d)
    act = _ffn_act_fwd(gu, w["ffn_conv_w"], _behind(w["ffn_conv_b"], token))
    w.update(comm.end(3, act))
    d_ff = w["w_down"].shape[0]
    x2 = _mm_nn("down_fwd", act, w["w_down"], min(tm, 512), 1024, d_ff, res=x1)
    loss, dx2, dx2b, g_norm_final = _loss_head("loss_head", x2, w["norm_final_g"], target)

    grads = {"norm_final_g": g_norm_final}
    dact = _mm_nt("down_bwd_x", dx2b, w["w_down"], tm, d_ff // N_CHIPS, d)
    grads["w_down"] = _mm_tn("down_bwd_w", act, dx2b, d_ff // N_CHIPS, 1024, 2048)
    dgu, grads["ffn_conv_w"], grads["ffn_conv_b"] = _ffn_act_bwd(gu, dact, w["ffn_conv_w"], w["ffn_conv_b"])
    dn2 = _mm_up_bwd_x("up_bwd_x", dgu, w["w_up"], tm, 1024)
    grads["w_up"] = _mm_up_bwd_w("up_bwd_w", n2, dgu, 1024, 2048)
    dx1, dx1b, grads["norm_ffn_g"] = _rmsnorm_bwd("norm_ffn_bwd", x1, rstd2, w["norm_ffn_g"], dn2, dx2)
    dmix = _mm_nt("out_bwd_x", dx1b, w["w_out"], tm, 1024, d)
    grads["w_out"] = _mm_tn("out_bwd_w", mix, dx1b, 1024, 1024, 2048)
    token = comm.reduce_early(grads)
    dproj, dgate, grads["mlstm_norm_g"] = _mlstm_bwd(proj, gates_t, _behind(bias_row, token), bias_col,
                                                     w["mlstm_norm_g"], cprev, nprev, mprev, dmix)
    token = comm.reduce_early_mid(dproj)
    (dproj, grads["lru_conv_w"], grads["lru_conv_b"], grads["lru_wa"], grads["lru_wx"],
     grads["lru_ba"], grads["lru_bx"], grads["lru_lambda"]) = _lru_bwd(
        proj, hsave, dmix, dproj, w["lru_conv_w"], _behind(w["lru_conv_b"], token), w["lru_wa"], w["lru_wx"],
        w["lru_ba"], w["lru_bx"], w["lru_lambda"])
    gate_bias_grad, dproj = _gate_grads(dgate, dproj)
    grads["b_gate_m"] = gate_bias_grad[:, :2 * HEADS]
    grads["w_in"] = _mm_tn("proj_bwd_w", n1, dproj, 1024, tn_proj, 2048)
    token = comm.reduce_late(grads)
    dn1 = _mm_nt("proj_bwd_x", dproj, w["w_in"], tm, 512, _PROJ_PAD, after=token)
    token = comm.reduce_late_mid(dn1)
    grad_x, _, grads["norm_mix_g"] = _rmsnorm_bwd("norm_mix_bwd", x, rstd1, _behind(w["norm_mix_g"], token),
                                                  dn1, dx1)
    return loss, grad_x, grads


WEIGHT_NAMES = ("norm_mix_g", "w_in", "b_gate_m", "mlstm_norm_g", "lru_conv_w", "lru_conv_b", "lru_wa", "lru_ba",
                "lru_wx", "lru_bx", "lru_lambda", "w_out", "norm_ffn_g", "w_up", "ffn_conv_w", "ffn_conv_b",
                "w_down", "norm_final_g")
BIG = ("w_in", "w_out", "w_up", "w_down")
SMALL_SHARDED = ("mlstm_norm_g", "lru_conv_w", "ffn_conv_w")
SMALL = tuple(n for n in WEIGHT_NAMES if n not in BIG)
SMALL_REPLICATED = tuple(n for n in SMALL if n not in SMALL_SHARDED)


def _proj_segments():
    segs = [(0, 0, _QKVO), (_QKVO, _QKVO + 2 * D_LRU, _N_GATES)]
    for n in range(LRU_BLOCKS):
        segs.append((_QKVO + _N_GATES + n * LRU_BLOCK_DIM, _QKVO + 2 * n * LRU_BLOCK_DIM, LRU_BLOCK_DIM))
        segs.append((_QKVO + _N_GATES + D_LRU + n * LRU_BLOCK_DIM, _QKVO + (2 * n + 1) * LRU_BLOCK_DIM,
                     LRU_BLOCK_DIM))
    return segs


def _w_in_shards_to_local(shards):
    width = shards.shape[2]
    pieces = []
    for g0, _, n in sorted(_proj_segments(), key=lambda s: s[1]):
        at = g0
        while at < g0 + n:
            j = at // width
            stop = min(g0 + n, (j + 1) * width)
            pieces.append(shards[j][:, at - j * width:stop - j * width])
            at = stop
    pieces.append(jnp.zeros((shards.shape[1], PROJ_GATE_PAD - _N_GATES), shards.dtype))
    return jnp.concatenate(pieces, axis=1)


def _w_in_local_to_shards(w):
    width = _PROJ_COLS // N_CHIPS
    shards = []
    for j in range(N_CHIPS):
        pieces = []
        for g0, l0, n in sorted(_proj_segments()):
            lo, hi = max(g0, j * width), min(g0 + n, (j + 1) * width)
            if lo < hi:
                pieces.append(w[:, l0 + lo - g0:l0 + hi - g0])
        shards.append(jnp.concatenate(pieces, axis=1))
    return jnp.stack(shards)


def _w_in_to_global(w):
    sh = _w_in_local_to_shards(w)
    return jnp.concatenate([sh[j] for j in range(N_CHIPS)], axis=1)


def _size(shp):
    return functools.reduce(lambda a, b: a * b, shp, 1)


def _lane_dense(shp):
    return len(shp) >= 2 and shp[-1] == LANES and _size(shp) % (HALO * LANES) == 0


def _pack_rows(shapes):
    loose = sum(_size(shp) for shp in shapes if not _lane_dense(shp))
    return sum(_size(shp) // LANES for shp in shapes if _lane_dense(shp)) + -(-loose // (HALO * LANES)) * HALO


def _pack(arrs, rows):
    del rows
    parts = [a.reshape(-1, LANES).astype(F32) for a in arrs if _lane_dense(a.shape)]
    loose = [a.reshape(-1).astype(F32) for a in arrs if not _lane_dense(a.shape)]
    if loose:
        flat = jnp.concatenate(loose)
        n = -(-flat.shape[0] // (HALO * LANES)) * HALO * LANES
        parts.append(jnp.pad(flat, (0, n - flat.shape[0])).reshape(-1, LANES))
    return parts[0] if len(parts) == 1 else jnp.concatenate(parts, axis=0)


def _unpack(buf, shapes):
    out, row = {}, 0
    for i, shp in enumerate(shapes):
        if _lane_dense(shp):
            n = _size(shp) // LANES
            out[i] = buf[row:row + n].reshape(shp)
            row += n
    flat, at = buf[row:].reshape(-1), 0
    for i, shp in enumerate(shapes):
        if not _lane_dense(shp):
            out[i] = flat[at:at + _size(shp)].reshape(shp)
            at += _size(shp)
    return [out[i] for i in range(len(shapes))]


def _assemble_weights(g_in, g_out, g_up, g_down, small_sharded, replicated):
    w = dict(replicated)
    w["w_in"] = _w_in_shards_to_local(g_in)
    w["w_out"] = g_out.reshape(-1, g_out.shape[-1])
    w["w_up"] = g_up
    w["w_down"] = g_down.reshape(-1, g_down.shape[-1])
    for name, v in small_sharded.items():
        w[name] = jnp.concatenate([v[j] for j in range(N_CHIPS)], axis=1)
    return w


def _full_weights_from_global(weights):
    shard = lambda a, axis: jnp.stack(jnp.split(a, N_CHIPS, axis=axis))
    rep = {n: weights[n].reshape(1, -1) if weights[n].ndim <= 2 and n != "b_gate_m" else weights[n]
           for n in SMALL_REPLICATED}
    rep["b_gate_m"] = weights["b_gate_m"].reshape(1, -1)
    return _assemble_weights(shard(weights["w_in"], 1).astype(BF16), shard(weights["w_out"], 0).astype(BF16),
                             shard(weights["w_up"], 1).astype(BF16), shard(weights["w_down"], 0).astype(BF16),
                             {n: shard(weights[n], 1) for n in SMALL_SHARDED}, rep)


def _grads_to_global(grads):
    g = dict(grads)
    g["w_in"] = _w_in_to_global(grads["w_in"])
    g["w_up"] = jnp.concatenate([grads["w_up"][j] for j in range(N_CHIPS)], axis=1)
    return g


def _place():
    x, y, c = lax.axis_index("x"), lax.axis_index("y"), lax.axis_index("c")
    chips = [(1 - x, y), (x, 1 - y), (1 - x, 1 - y)]
    return x, y, c, 2 * x + y, chips


def _half_rows(n_rows, which):
    half = n_rows // 2
    return pl.ds(pl.multiple_of(which * half, 16), half)


def _rcopy(src, dst, send_sem, recv_sem, to):
    return pltpu.make_async_remote_copy(src_ref=src, dst_ref=dst, send_sem=send_sem, recv_sem=recv_sem,
                                        device_id=to, device_id_type=MESH)


HBM_SPEC = pl.BlockSpec(memory_space=pltpu.HBM)
SEM_SPEC = pl.BlockSpec(memory_space=pltpu.SEMAPHORE)
TOKEN_SHAPE = (8, LANES)


def _split_call(name, bufs, sems_in, sems_out_shapes, body_fn, after=None):
    nb, ni, no = len(bufs), len(sems_in), len(sems_out_shapes)
    after = [] if after is None else list(after) if isinstance(after, (list, tuple)) else [after]

    def body(*refs):
        buf_refs = refs[:nb]
        sem_in_refs = refs[nb:nb + ni]
        outs = refs[nb + ni + len(after):]
        sem_out_refs = outs[:no]
        token_ref = outs[no + nb]
        body_fn(buf_refs, sem_in_refs, sem_out_refs)
        token_ref[...] = jnp.zeros_like(token_ref)

    out_shape = ([pltpu.SemaphoreType.DMA(shp) for shp in sems_out_shapes]
                 + [pltpu.HBM(b.shape, b.dtype) for b in bufs] + [jax.ShapeDtypeStruct(TOKEN_SHAPE, F32)])
    res = pl.pallas_call(
        body, name=name, out_shape=out_shape,
        in_specs=[HBM_SPEC] * nb + [SEM_SPEC] * ni + [ANY] * len(after),
        out_specs=[SEM_SPEC] * no + [HBM_SPEC] * nb + [pl.BlockSpec(memory_space=pltpu.VMEM)],
        input_output_aliases={i: no + i for i in range(nb)},
        compiler_params=pltpu.CompilerParams(has_side_effects=pltpu.SideEffectType.DATAFLOW_SIDE_EFFECTING),
    )(*[pltpu.with_memory_space_constraint(b, pltpu.HBM) for b in bufs], *sems_in, *after)
    return list(res[:no]), list(res[no:no + nb]), res[no + nb]


def _place_own_shard(name, idx, shard, after=None):
    rows, cols = shard.shape
    tr = _row_tile(rows)

    def body(idx_ref, s_ref, *rest):
        rest[-1][...] = s_ref[...].astype(BF16)

    return pl.pallas_call(
        body, name=name, out_shape=jax.ShapeDtypeStruct((N_CHIPS, rows, cols), BF16),
        grid_spec=pltpu.PrefetchScalarGridSpec(
            num_scalar_prefetch=1, grid=(rows // tr,),
            in_specs=[pl.BlockSpec((tr, cols), lambda i, s: (i, 0))] + ([] if after is None else [ANY]),
            out_specs=pl.BlockSpec((None, tr, cols), lambda i, s: (s[1], i, 0))),
        compiler_params=_params(("parallel",)),
    )(idx, shard, *(() if after is None else (after,)))


GATHER_GROUPS = ((0, 4), (1,), (2,), (3,))


def _gather_start(name, lands, groups, after=None):
    members = [w for g in groups for w in GATHER_GROUPS[g]]

    def starts(bufs, _, sems):
        x, y, c, me, chips = _place()
        for gi, g in enumerate(groups):
            for pos, w in enumerate(GATHER_GROUPS[g]):
                buf = bufs[members.index(w)]
                part = buf.at[me] if w == 4 else buf.at[me, _half_rows(buf.shape[1], c)]
                for k, chip in enumerate(chips):
                    _rcopy(part, part, sems[2 * gi].at[3 * pos + k], sems[2 * gi + 1].at[3 * pos + k],
                           (*chip, c)).start()

    shapes = []
    for g in groups:
        shapes += [(3 * len(GATHER_GROUPS[g]),)] * 2
    sems, bufs, token = _split_call(name, [lands[w] for w in members], [], shapes, starts, after=after)
    return ({g: (sems[2 * gi], sems[2 * gi + 1]) for gi, g in enumerate(groups)},
            dict(zip(members, bufs)), token)


def _gather_mid(grp, lands, sems, after):
    members = GATHER_GROUPS[grp]
    big = [w for w in members if w != 4]

    def mid(bufs, sems_in, sems_out):
        x, y, c, me, chips = _place()
        send_sems, recv_sems = sems_in
        for pos, w in enumerate(members):
            for k, chip in enumerate(chips):
                cid = 2 * chip[0] + chip[1]
                buf = bufs[pos]
                mine = buf.at[me] if w == 4 else buf.at[me, _half_rows(buf.shape[1], c)]
                theirs = buf.at[cid] if w == 4 else buf.at[cid, _half_rows(buf.shape[1], c)]
                arrival = _rcopy(mine, theirs, send_sems.at[3 * pos + k], recv_sems.at[3 * pos + k], (*chip, c))
                arrival.wait_recv()
                arrival.wait_send()
                if w != 4:
                    _rcopy(theirs, theirs, sems_out[0].at[3 * big.index(w) + k],
                           sems_out[1].at[3 * big.index(w) + k], (x, y, 1 - c)).start()

    new_sems, bufs, token = _split_call(f"gather_mid_{grp}", [lands[w] for w in members], list(sems),
                                        [(3 * len(big),), (3 * len(big),)], mid, after=after)
    return new_sems, bufs, token


def _gather_end(grp, bufs, sems, after):
    members = GATHER_GROUPS[grp]
    big = [w for w in members if w != 4]

    def end(refs, sems_in, _):
        x, y, c, me, chips = _place()
        send_sems, recv_sems = sems_in
        for pos, w in enumerate(members):
            if w == 4:
                continue
            for k, chip in enumerate(chips):
                cid = 2 * chip[0] + chip[1]
                buf = refs[pos]
                sent = buf.at[cid, _half_rows(buf.shape[1], c)]
                landed = buf.at[cid, _half_rows(buf.shape[1], 1 - c)]
                fwd = _rcopy(sent, landed, send_sems.at[3 * big.index(w) + k], recv_sems.at[3 * big.index(w) + k],
                             (x, y, 1 - c))
                fwd.wait_recv()
                fwd.wait_send()

    _, bufs, token = _split_call(f"gather_end_{grp}", bufs, list(sems), [], end, after=after)
    return bufs, token


def _pair_start(name, grads, extra=None):
    n = len(grads)
    bufs = list(grads) + [lax.empty((g.shape[0], g.shape[1] // 2, g.shape[2]), g.dtype) for g in grads]
    if extra is not None:
        bufs += [extra, lax.empty(extra.shape, extra.dtype)]

    def starts(refs, _, sems):
        x, y, c, _, _ = _place()
        for w in range(n):
            other = _half_rows(refs[w].shape[1], 1 - c)
            _rcopy(refs[w].at[:, other], refs[n + w], sems[0].at[w], sems[1].at[w], (x, y, 1 - c)).start()
        if extra is not None:
            _rcopy(refs[2 * n], refs[2 * n + 1], sems[0].at[n], sems[1].at[n], (x, y, 1 - c)).start()

    count = n + (extra is not None)
    return _split_call(name, bufs, [], [(count,), (count,)], starts)


def _pair_wait(name, n, bufs, sems, after):
    has_extra = len(bufs) > 2 * n

    def waits(refs, sems_in, _):
        x, y, c, _, _ = _place()
        for w in range(n):
            other = _half_rows(refs[w].shape[1], 1 - c)
            cp = _rcopy(refs[w].at[:, other], refs[n + w], sems_in[0].at[w], sems_in[1].at[w], (x, y, 1 - c))
            cp.wait_recv()
            cp.wait_send()
        if has_extra:
            cp = _rcopy(refs[2 * n], refs[2 * n + 1], sems_in[0].at[n], sems_in[1].at[n], (x, y, 1 - c))
            cp.wait_recv()
            cp.wait_send()

    _, bufs, token = _split_call(name, bufs, list(sems), [], waits, after=after)
    return bufs, token


def _chip_start(name, partials, small=None):
    n = len(partials)
    bufs = list(partials) + [lax.empty(p.shape, p.dtype) for p in partials] + ([] if small is None else [small])

    def starts(refs, _, sems):
        _, _, c, me, chips = _place()
        for w in range(n):
            for k, chip in enumerate(chips):
                cid = 2 * chip[0] + chip[1]
                _rcopy(refs[w].at[cid], refs[n + w].at[me], sems[0].at[3 * w + k], sems[1].at[3 * w + k],
                       (*chip, c)).start()
        if small is not None:
            for k, chip in enumerate(chips):
                _rcopy(refs[2 * n].at[me], refs[2 * n].at[me], sems[0].at[3 * n + k], sems[1].at[3 * n + k],
                       (*chip, c)).start()

    count = 3 * (n + (small is not None))
    return _split_call(name, bufs, [], [(count,), (count,)], starts)


def _chip_wait(name, n, bufs, sems, after):
    has_small = len(bufs) > 2 * n

    def waits(refs, sems_in, _):
        _, _, c, me, chips = _place()
        for w in range(n):
            for k, chip in enumerate(chips):
                cid = 2 * chip[0] + chip[1]
                cp = _rcopy(refs[w].at[cid], refs[n + w].at[cid], sems_in[0].at[3 * w + k],
                            sems_in[1].at[3 * w + k], (*chip, c))
                cp.wait_recv()
                cp.wait_send()
        if has_small:
            for k, chip in enumerate(chips):
                cid = 2 * chip[0] + chip[1]
                cp = _rcopy(refs[2 * n].at[me], refs[2 * n].at[cid], sems_in[0].at[3 * n + k],
                            sems_in[1].at[3 * n + k], (*chip, c))
                cp.wait_recv()
                cp.wait_send()

    _, bufs, token = _split_call(name, bufs, list(sems), [], waits, after=after)
    return bufs, token


def _small_pair_sum(idx, own, recv):
    rows = own.shape[0]

    def body(idx_ref, a_ref, b_ref, o_ref):
        o_ref[...] = a_ref[...] + b_ref[...]

    blk = pl.BlockSpec((rows, LANES), lambda i, s: (0, 0))
    return pl.pallas_call(
        body, name="small_pair_sum", out_shape=jax.ShapeDtypeStruct((N_CHIPS, rows, LANES), F32),
        grid_spec=pltpu.PrefetchScalarGridSpec(
            num_scalar_prefetch=1, grid=(1,), in_specs=[blk, blk],
            out_specs=pl.BlockSpec((None, rows, LANES), lambda i, s: (s[1], 0, 0))),
        compiler_params=_params(("arbitrary",)),
    )(idx, own, recv)


def _pair_share(name, shards, late=None):
    nb = len(shards)
    nl = 0 if late is None else 1

    def body(*refs):
        srcs = refs[:nb]
        dsts = refs[nb + nl:2 * nb + nl]
        send_sems, recv_sems = refs[2 * nb + 2 * nl:2 * nb + 2 * nl + 2]
        x, y, c, _, _ = _place()
        sibling = (x, y, 1 - c)
        sends = []
        for w in range(nb):
            mine = _half_rows(dsts[w].shape[0], c)
            sends.append(_rcopy(srcs[w].at[mine], dsts[w].at[mine], send_sems.at[w], recv_sems.at[w], sibling))
        if nl:
            late_ref, late_out = refs[nb], refs[2 * nb + 1]
            late_send, late_recv, local_sem = refs[2 * nb + 4:]
            my_id = 4 * x + 2 * y + c
            peer = lambda r: (1 - x if r & 4 else x, 1 - y if r & 2 else y, 1 - c if r & 1 else c)
            local = pltpu.make_async_copy(late_ref, late_out.at[my_id], local_sem)
            local.start()
            for r in range(1, N_DEV):
                sends.append(_rcopy(late_ref, late_out.at[my_id], late_send.at[r - 1], late_recv.at[r - 1],
                                    peer(r)))
        for cp in sends:
            cp.start()
        for w in range(nb):
            other = _half_rows(dsts[w].shape[0], 1 - c)
            _rcopy(srcs[w].at[other], dsts[w].at[other], send_sems.at[w], recv_sems.at[w], sibling).wait_recv()
        if nl:
            for r in range(1, N_DEV):
                frm = peer(r)
                _rcopy(late_ref, late_out.at[4 * frm[0] + 2 * frm[1] + frm[2]], late_send.at[r - 1],
                       late_recv.at[r - 1], frm).wait_recv()
        for cp in sends:
            cp.wait_send()
        if nl:
            local.wait()

    out_shape = [jax.ShapeDtypeStruct(h.shape, h.dtype) for h in shards]
    scratch = [pltpu.SemaphoreType.DMA((nb,)), pltpu.SemaphoreType.DMA((nb,))]
    if nl:
        out_shape.append(jax.ShapeDtypeStruct((N_DEV,) + late.shape, late.dtype))
        scratch += [pltpu.SemaphoreType.DMA((N_DEV - 1,)), pltpu.SemaphoreType.DMA((N_DEV - 1,)),
                    pltpu.SemaphoreType.DMA(())]
    return pl.pallas_call(
        body, name=name, out_shape=out_shape,
        in_specs=[ANY] * (nb + nl), out_specs=[ANY] * (nb + nl), scratch_shapes=scratch,
        input_output_aliases={w: w for w in range(nb)},
    )(*shards, *(() if late is None else (late,)))


def _row_tile(rows):
    return _pick(rows, 128, 64, 16, 8)


def _pair_sum(name, idx, grad, recv):
    n, half, cols = recv.shape
    tr = _row_tile(half)
    nrb = half // tr

    def body(idx_ref, g_ref, r_ref, o_ref):
        o_ref[...] = (g_ref[...] + r_ref[...]).astype(BF16)

    return pl.pallas_call(
        body, name=name, out_shape=jax.ShapeDtypeStruct(recv.shape, BF16),
        grid_spec=pltpu.PrefetchScalarGridSpec(
            num_scalar_prefetch=1, grid=(n - 1, nrb),
            in_specs=[pl.BlockSpec((None, tr, cols), lambda j, i, s: (s[2 + j], s[0] * nrb + i, 0)),
                      pl.BlockSpec((None, tr, cols), lambda j, i, s: (s[2 + j], i, 0))],
            out_specs=pl.BlockSpec((None, tr, cols), lambda j, i, s: (s[2 + j], i, 0))),
        compiler_params=_params(("parallel", "parallel")),
    )(idx, grad, recv)


def _final_sum(name, idx, grad, recv, chip_sums):
    _, half, cols = recv.shape
    tr = _row_tile(half)
    nrb = half // tr

    def body(idx_ref, g_ref, r_ref, p1_ref, p2_ref, p3_ref, o_ref):
        acc = g_ref[...] + r_ref[...]
        for p_ref in (p1_ref, p2_ref, p3_ref):
            acc = acc + p_ref[...].astype(F32)
        o_ref[...] = acc

    slot = lambda which: pl.BlockSpec((None, tr, cols), lambda i, s, which=which: (s[which], i, 0))
    return pl.pallas_call(
        body, name=name, out_shape=jax.ShapeDtypeStruct((2 * half, cols), F32),
        grid_spec=pltpu.PrefetchScalarGridSpec(
            num_scalar_prefetch=1, grid=(nrb,),
            in_specs=[pl.BlockSpec((None, tr, cols), lambda i, s: (s[1], s[0] * nrb + i, 0)),
                      slot(1), slot(2), slot(3), slot(4)],
            out_specs=pl.BlockSpec((tr, cols), lambda i, s: (s[0] * nrb + i, 0))),
        compiler_params=_params(("parallel",)),
    )(idx, grad, recv, chip_sums, chip_sums, chip_sums)


def _pair_sum_all(name, idx, grad, recv):
    _, half, cols = recv.shape
    tr = _row_tile(half)
    nrb = half // tr

    def body(idx_ref, g_ref, r_ref, o_ref):
        o_ref[...] = (g_ref[...] + r_ref[...]).astype(BF16)

    return pl.pallas_call(
        body, name=name, out_shape=jax.ShapeDtypeStruct((half, cols), BF16),
        grid_spec=pltpu.PrefetchScalarGridSpec(
            num_scalar_prefetch=1, grid=(nrb,),
            in_specs=[pl.BlockSpec((None, tr, cols), lambda i, s: (0, s[0] * nrb + i, 0)),
                      pl.BlockSpec((None, tr, cols), lambda i, s: (0, i, 0))],
            out_specs=pl.BlockSpec((tr, cols), lambda i, s: (i, 0))),
        compiler_params=_params(("parallel",)),
    )(idx, grad, recv)


def _final_sum_bf16(name, idx, partial, chip_sums):
    _, half, cols = partial.shape
    tr = _row_tile(half)
    nrb = half // tr

    def body(idx_ref, p0_ref, p1_ref, p2_ref, p3_ref, o_ref):
        acc = p0_ref[...].astype(F32)
        for p_ref in (p1_ref, p2_ref, p3_ref):
            acc = acc + p_ref[...].astype(F32)
        o_ref[...] = acc

    slot = lambda which: pl.BlockSpec((None, tr, cols), lambda i, s, which=which: (s[which], i, 0))
    return pl.pallas_call(
        body, name=name, out_shape=jax.ShapeDtypeStruct((2 * half, cols), F32),
        grid_spec=pltpu.PrefetchScalarGridSpec(
            num_scalar_prefetch=1, grid=(nrb,),
            in_specs=[slot(1), slot(2), slot(3), slot(4)],
            out_specs=pl.BlockSpec((tr, cols), lambda i, s: (s[0] * nrb + i, 0))),
        compiler_params=_params(("parallel",)),
    )(idx, partial, chip_sums, chip_sums, chip_sums)


def _small_sum(name, packs):
    n, rows, _ = packs.shape

    def body(p_ref, o_ref):
        acc = p_ref[0]
        for k in range(1, n):
            acc = acc + p_ref[k]
        o_ref[...] = acc

    return pl.pallas_call(
        body, name=name, out_shape=jax.ShapeDtypeStruct((rows, LANES), F32),
        in_specs=[pl.BlockSpec(memory_space=pltpu.VMEM)], out_specs=pl.BlockSpec(memory_space=pltpu.VMEM),
        compiler_params=pltpu.CompilerParams(vmem_limit_bytes=VMEM_LIMIT),
    )(packs)


def _adamw_math(w, g, m, v):
    m_new = ADAM_B1 * m + (1.0 - ADAM_B1) * g
    v_new = ADAM_B2 * v + (1.0 - ADAM_B2) * (g * g)
    m_hat = m_new / (1.0 - ADAM_B1 ** ADAM_STEP)
    v_hat = v_new / (1.0 - ADAM_B2 ** ADAM_STEP)
    return -ADAM_LR * (m_hat / (jnp.sqrt(v_hat) + ADAM_EPS) + ADAM_WD * w), m_new, v_new


def _adamw_many(name, ws, gs, ms, vs):
    n = len(ws)

    def body(*refs):
        for i in range(n):
            d, m_new, v_new = _adamw_math(refs[i][...], refs[n + i][...], refs[2 * n + i][...],
                                          refs[3 * n + i][...])
            refs[4 * n + i][...] = d
            refs[5 * n + i][...] = m_new
            refs[6 * n + i][...] = v_new

    vmem = pl.BlockSpec(memory_space=pltpu.VMEM)
    res = pl.pallas_call(
        body, name=name, in_specs=[vmem] * (4 * n), out_specs=[vmem] * (3 * n),
        out_shape=[jax.ShapeDtypeStruct(w.shape, F32) for w in ws] * 3,
        compiler_params=pltpu.CompilerParams(vmem_limit_bytes=VMEM_LIMIT),
    )(*ws, *gs, *ms, *vs)
    return res[:n], res[n:2 * n], res[2 * n:]


def _adamw(name, w, g, m, v):
    rows, cols = w.shape
    tr = rows if rows * cols * 4 <= (2 << 20) else _row_tile(rows)

    def body(w_ref, g_ref, m_ref, v_ref, g_out_ref, d_ref, nm_ref, nv_ref):
        gv = g_ref[...]
        g_out_ref[...] = gv
        d_ref[...], nm_ref[...], nv_ref[...] = _adamw_math(w_ref[...], gv, m_ref[...], v_ref[...])

    blk = pl.BlockSpec((tr, cols), lambda i: (i, 0))
    sds = jax.ShapeDtypeStruct((rows, cols), F32)
    return pl.pallas_call(
        body, name=name, grid=(rows // tr,), in_specs=[blk] * 4, out_specs=[blk] * 4, out_shape=[sds] * 4,
        compiler_params=_params(("parallel",)),
    )(w, g, m, v)


def _train_step(x, target, W, M, V):
    xi, yi, ci = lax.axis_index("x"), lax.axis_index("y"), lax.axis_index("c")
    me = 2 * xi + yi
    big = {n: W[n][0] for n in BIG}
    big_m = {n: M[n][0] for n in BIG}
    big_v = {n: V[n][0] for n in BIG}

    others = [jnp.where(jnp.int32(i) >= me, i + 1, i) for i in range(N_CHIPS - 1)]
    idx = jnp.stack([ci, me] + others).astype(jnp.int32)

    sharded_shapes = [W[n].shape[1:] for n in SMALL_SHARDED]
    small_pack = _pack([W[n][0] for n in SMALL_SHARDED], _pack_rows(sharded_shapes))
    small_land = lax.dynamic_update_slice(jnp.zeros((N_CHIPS,) + small_pack.shape, F32), small_pack[None],
                                          (me, 0, 0))
    replicated = {n: (W[n].reshape(1, -1) if W[n].ndim <= 2 else W[n][0]) for n in SMALL_REPLICATED}

    early = ("w_out", "w_up", "w_down")
    small_late = "norm_mix_g"
    small_early = tuple(n for n in SMALL if n != small_late)
    global_shape = lambda n: ((W[n].shape[1], W[n].shape[2] * N_CHIPS) if n in SMALL_SHARDED else
                              tuple(W[n].shape) if W[n].ndim == 1 else tuple(W[n].shape[1:]))
    small_shapes = [global_shape(n) for n in small_early]

    def shard_major(n, g):
        if n == "w_in":
            return _w_in_local_to_shards(g)
        return g if g.ndim == 3 else g.reshape((N_CHIPS, -1) + g.shape[1:])

    class _SplitComm:
        def reduce_early(self, grads):
            self.e_sems, self.e_bufs, token = _pair_start("pair_start_early",
                                                          [shard_major(n, grads[n]) for n in early])
            return token

        def reduce_early_mid(self, after):
            n = len(early)
            bufs, _ = _pair_wait("pair_wait_early", n, self.e_bufs, self.e_sems, after)
            self.e_grads, self.e_recv = bufs[:n], bufs[n:2 * n]
            partial = [_pair_sum(f"pair_sum_{nm}", idx, g, r) for nm, g, r in zip(early, self.e_grads, self.e_recv)]
            self.e_sems, self.e_bufs, token = _chip_start("chip_start_early", partial)
            return token

        def reduce_late(self, grads):
            pack = _pack([grads[n] for n in small_early], _pack_rows(small_shapes))
            self.l_sems, self.l_bufs, token = _pair_start("pair_start_late", [grads["w_in"][None]], extra=pack)
            return token

        def reduce_late_mid(self, after):
            bufs, _ = _pair_wait("pair_wait_late", 1, self.l_bufs, self.l_sems, after)
            partial = _w_in_local_to_shards(_pair_sum_all("pair_sum_w_in", idx, bufs[0], bufs[1]))
            self.l_sems, self.l_bufs, token = _chip_start("chip_start_late", [partial],
                                                          small=_small_pair_sum(idx, bufs[2], bufs[3]))
            return token

        def finish_early(self, after):
            n = len(early)
            bufs, _ = _chip_wait("chip_wait_early", n, self.e_bufs, self.e_sems, after)
            halves = [_final_sum(f"final_sum_{nm}", idx, g, r, p)
                      for nm, g, r, p in zip(early, self.e_grads, self.e_recv, bufs[n:2 * n])]
            return dict(zip(early, _pair_share("pair_share_early", halves)))

        def finish_late(self, after, late):
            bufs, _ = _chip_wait("chip_wait_late", 1, self.l_bufs, self.l_sems, after)
            half = _final_sum_bf16("final_sum_w_in", idx, bufs[0], bufs[1])
            small = dict(zip(small_early, _unpack(_small_sum("small_sum", bufs[2]), small_shapes)))
            whole, late_all = _pair_share("pair_share_late", [half], late)
            return whole, small, _small_sum("late_sum", late_all)

        def begin(self):
            first = {0: _place_own_shard("place_w_in", idx, big["w_in"]), 4: small_land}
            self.sems, self.lands, token = _gather_start("gather_start_0", first, (0,))
            rest = {i: _place_own_shard(f"place_{BIG[i]}", idx, big[BIG[i]], after=token) for i in (1, 2, 3)}
            sems, lands, token = _gather_start("gather_start_1", rest, (1, 2, 3), after=token)
            self.sems.update(sems)
            self.lands.update(lands)
            return token

        def mid(self, grp, after):
            if grp == 0:
                after = [after, big_m["w_in"], big_v["w_in"]]
            self.pending = _gather_mid(grp, self.lands, self.sems[grp], after)
            return self.pending[2]

        def end(self, grp, after):
            sems, bufs, _ = self.pending
            bufs, _ = _gather_end(grp, bufs, sems, after)
            if grp == 0:
                per_chip = [_unpack(bufs[1][j], sharded_shapes) for j in range(N_CHIPS)]
                out = {n: jnp.concatenate([per_chip[j][i] for j in range(N_CHIPS)], axis=1)
                       for i, n in enumerate(SMALL_SHARDED)}
                out["w_in"] = _w_in_shards_to_local(bufs[0])
                return out
            if grp == 2:
                return {"w_up": bufs[0]}
            return {("w_out" if grp == 1 else "w_down"): bufs[0].reshape(-1, bufs[0].shape[-1])}

    comm = _SplitComm()
    loss, grad_x, grads = _local_step(x[0], target[0], replicated, comm)
    loss = lax.psum(loss[0, 0], ("x", "y", "c"))
    out_g, out_d, out_m, out_v = {}, {}, {}, {}

    def update_big(n, grad):
        g, d, nm, nv = _adamw(f"adamw_{n}", big[n], grad, big_m[n], big_v[n])
        out_g[n], out_d[n], out_m[n], out_v[n] = g[None], d[None], nm[None], nv[None]
        return d

    early_grads = comm.finish_early(grad_x)
    done = [update_big(n, early_grads[n]) for n in early]
    late = _pack([grads[small_late]], _pack_rows([global_shape(small_late)]))
    w_in_grad, small_grads, late_sum = comm.finish_late(done, late)
    update_big("w_in", w_in_grad)
    small_grads[small_late] = _unpack(late_sum, [global_shape(small_late)])[0]
    for n in SMALL_SHARDED:
        width = W[n].shape[2]
        small_grads[n] = lax.dynamic_slice_in_dim(small_grads[n], me * width, width, axis=1)

    for n in SMALL:
        out_g[n] = small_grads[n].reshape(W[n].shape)
    two_d = lambda a: a.reshape(1, -1) if a.ndim == 1 else a
    results = _adamw_many("adamw_small", *[[two_d(src[n]) for n in SMALL] for src in (W, out_g, M, V)])
    for dst, arrs in zip((out_d, out_m, out_v), results):
        dst.update({n: a.reshape(W[n].shape) for n, a in zip(SMALL, arrs)})
    return (loss, grad_x[None], *[out_g[n] for n in WEIGHT_NAMES], *[out_d[n] for n in WEIGHT_NAMES],
            *[out_m[n] for n in WEIGHT_NAMES], *[out_v[n] for n in WEIGHT_NAMES])


def kernel(x, norm_mix_g, w_in, b_gate_m, mlstm_norm_g, lru_conv_w, lru_conv_b, lru_wa, lru_ba, lru_wx, lru_bx, lru_lambda, w_out, norm_ffn_g, w_up, ffn_conv_w, ffn_conv_b, w_down, norm_final_g, loss_target, m_norm_mix_g, m_w_in, m_b_gate_m, m_mlstm_norm_g, m_lru_conv_w, m_lru_conv_b, m_lru_wa, m_lru_ba, m_lru_wx, m_lru_bx, m_lru_lambda, m_w_out, m_norm_ffn_g, m_w_up, m_ffn_conv_w, m_ffn_conv_b, m_w_down, m_norm_final_g, v_norm_mix_g, v_w_in, v_b_gate_m, v_mlstm_norm_g, v_lru_conv_w, v_lru_conv_b, v_lru_wa, v_lru_ba, v_lru_wx, v_lru_bx, v_lru_lambda, v_w_out, v_norm_ffn_g, v_w_up, v_ffn_conv_w, v_ffn_conv_b, v_w_down, v_norm_final_g):
    W = dict(zip(WEIGHT_NAMES, (norm_mix_g, w_in, b_gate_m, mlstm_norm_g, lru_conv_w, lru_conv_b, lru_wa, lru_ba,
                                lru_wx, lru_bx, lru_lambda, w_out, norm_ffn_g, w_up, ffn_conv_w, ffn_conv_b,
                                w_down, norm_final_g)))
    M = dict(zip(WEIGHT_NAMES, (m_norm_mix_g, m_w_in, m_b_gate_m, m_mlstm_norm_g, m_lru_conv_w, m_lru_conv_b,
                                m_lru_wa, m_lru_ba, m_lru_wx, m_lru_bx, m_lru_lambda, m_w_out, m_norm_ffn_g,
                                m_w_up, m_ffn_conv_w, m_ffn_conv_b, m_w_down, m_norm_final_g)))
    V = dict(zip(WEIGHT_NAMES, (v_norm_mix_g, v_w_in, v_b_gate_m, v_mlstm_norm_g, v_lru_conv_w, v_lru_conv_b,
                                v_lru_wa, v_lru_ba, v_lru_wx, v_lru_bx, v_lru_lambda, v_w_out, v_norm_ffn_g,
                                v_w_up, v_ffn_conv_w, v_ffn_conv_b, v_w_down, v_norm_final_g)))
    return _train_step(x, loss_target, W, M, V)
```

```python
import functools

import jax
import jax.numpy as jnp
from jax import lax
from jax.experimental import pallas as pl
from jax.experimental.pallas import tpu as pltpu

F32 = jnp.float32
BF16 = jnp.bfloat16
MESH = pl.DeviceIdType.MESH

EPS = 1e-6
CHUNK = 512
HEADS = 4
HEAD_DIM = 256
D_MLSTM = HEADS * HEAD_DIM
LRU_BLOCKS = 8
LRU_BLOCK_DIM = 128
D_LRU = LRU_BLOCKS * LRU_BLOCK_DIM
LRU_C = 8.0
LRU_CONV = 4
FFN_CONV = 3
ADAM_LR = 0.001
ADAM_B1 = 0.9
ADAM_B2 = 0.999
ADAM_EPS = 1e-08
ADAM_WD = 0.01
ADAM_STEP = 10

N_CHIPS = 4
N_DEV = 8
LANES = 128
HALO = 8
PROJ_GATE_PAD = LANES
_QKVO = 4 * D_MLSTM
_N_GATES = 2 * HEADS
_PROJ_COLS = _QKVO + _N_GATES + 2 * D_LRU
_PROJ_PAD = _QKVO + 2 * D_LRU + PROJ_GATE_PAD
VMEM_LIMIT = 48 * 1024 * 1024
ANY = pl.BlockSpec(memory_space=pl.ANY)


def _params(sem, vmem=VMEM_LIMIT):
    return pltpu.CompilerParams(dimension_semantics=sem, vmem_limit_bytes=vmem)


def _matmul(name, a, b, grid, a_spec, b_spec, o_spec, out_sds, contract, res=None, res_spec=None, after=None):
    nk = grid[2]
    acc_shape = tuple(d for d in o_spec.block_shape if d is not None)

    def body(*refs):
        refs = list(refs)
        a_ref, b_ref = refs[:2]
        r_ref = refs[2] if res is not None else None
        o_ref = refs[-1] if nk == 1 else refs[-2]
        acc_ref = None if nk == 1 else refs[-1]
        k = pl.program_id(2)

        def part():
            return lax.dot_general(a_ref[...], b_ref[...], (contract, ((), ())), preferred_element_type=F32)

        def finish(r):
            if r_ref is not None:
                r = r_ref[...] + r
            o_ref[...] = r.astype(o_ref.dtype)

        if nk == 1:
            finish(part())
            return

        @pl.when(k == 0)
        def _():
            acc_ref[...] = part()

        @pl.when(jnp.logical_and(k > 0, k < nk - 1))
        def _():
            acc_ref[...] += part()

        @pl.when(k == nk - 1)
        def _():
            finish(acc_ref[...] + part())

    in_specs = [a_spec, b_spec] + ([] if res is None else [res_spec]) + ([] if after is None else [ANY])
    args = (a, b) + (() if res is None else (res,)) + (() if after is None else (after,))
    if after is not None:
        inner = body
        body = lambda *refs: inner(*refs[:len(in_specs) - 1], *refs[len(in_specs):])
    return pl.pallas_call(
        body, out_shape=out_sds, grid=grid, in_specs=in_specs, out_specs=o_spec,
        scratch_shapes=[] if nk == 1 else [pltpu.VMEM(acc_shape, F32)], name=name,
        compiler_params=_params(("parallel", "parallel", "arbitrary")),
    )(*args)


NN = ((1,), (0,))
NT = ((1,), (1,))
TN = ((0,), (0,))


def _mm_nn(name, a, b, tm, tn, tk, out_dtype=F32, res=None):
    m, k = a.shape
    n = b.shape[1]
    return _matmul(name, a, b, (m // tm, n // tn, k // tk),
                   pl.BlockSpec((tm, tk), lambda i, j, kk: (i, kk)),
                   pl.BlockSpec((tk, tn), lambda i, j, kk: (kk, j)),
                   pl.BlockSpec((tm, tn), lambda i, j, kk: (i, j)),
                   jax.ShapeDtypeStruct((m, n), out_dtype), NN,
                   res=res, res_spec=pl.BlockSpec((tm, tn), lambda i, j, kk: (i, j)))


def _mm_nt(name, a, b, tm, tn, tk, out_dtype=F32, res=None, after=None):
    m, k = a.shape
    n = b.shape[0]
    return _matmul(name, a, b, (m // tm, n // tn, k // tk),
                   pl.BlockSpec((tm, tk), lambda i, j, kk: (i, kk)),
                   pl.BlockSpec((tn, tk), lambda i, j, kk: (j, kk)),
                   pl.BlockSpec((tm, tn), lambda i, j, kk: (i, j)),
                   jax.ShapeDtypeStruct((m, n), out_dtype), NT,
                   res=res, res_spec=pl.BlockSpec((tm, tn), lambda i, j, kk: (i, j)), after=after)


def _mm_tn(name, a, b, tm, tn, tk, out_dtype=F32):
    k, m = a.shape
    n = b.shape[1]
    tk = min(tk, k)
    return _matmul(name, a, b, (m // tm, n // tn, k // tk),
                   pl.BlockSpec((tk, tm), lambda i, j, kk: (kk, i)),
                   pl.BlockSpec((tk, tn), lambda i, j, kk: (kk, j)),
                   pl.BlockSpec((tm, tn), lambda i, j, kk: (i, j)),
                   jax.ShapeDtypeStruct((m, n), out_dtype), TN)


def _up_shard(n):
    return 2 * (n % 2) + (n // 2) // 2, (n // 2) % 2


def _mm_up_fwd(name, a, wg_up, tm, tk):
    m, k = a.shape
    _, _, cols = wg_up.shape
    tn = cols // 2
    return _matmul(name, a, wg_up, (m // tm, 2 * N_CHIPS, k // tk),
                   pl.BlockSpec((tm, tk), lambda i, j, kk: (i, kk)),
                   pl.BlockSpec((None, tk, tn), lambda i, j, kk: (_up_shard(j)[0], kk, _up_shard(j)[1])),
                   pl.BlockSpec((tm, tn), lambda i, j, kk: (i, j)),
                   jax.ShapeDtypeStruct((m, 2 * N_CHIPS * tn), F32), NN)


def _mm_up_bwd_x(name, dgu, wg_up, tm, tn):
    m, _ = dgu.shape
    _, d, cols = wg_up.shape
    tk = cols // 2
    nk = N_CHIPS

    def body(a_ref, bg_ref, bu_ref, o_ref, acc_ref):
        k = pl.program_id(2)

        def part():
            dims = (NT, ((), ()))
            return (lax.dot_general(a_ref[:, :tk], bg_ref[...], dims, preferred_element_type=F32)
                    + lax.dot_general(a_ref[:, tk:], bu_ref[...], dims, preferred_element_type=F32))

        @pl.when(k == 0)
        def _():
            acc_ref[...] = part()

        @pl.when(jnp.logical_and(k > 0, k < nk - 1))
        def _():
            acc_ref[...] += part()

        @pl.when(k == nk - 1)
        def _():
            o_ref[...] = acc_ref[...] + part()

    wspec = lambda half: pl.BlockSpec(
        (None, tn, tk), lambda i, j, kk: (_up_shard(2 * kk + half)[0], j, _up_shard(2 * kk + half)[1]))
    return pl.pallas_call(
        body, name=name, grid=(m // tm, d // tn, nk), out_shape=jax.ShapeDtypeStruct((m, d), F32),
        in_specs=[pl.BlockSpec((tm, 2 * tk), lambda i, j, kk: (i, kk)), wspec(0), wspec(1)],
        out_specs=pl.BlockSpec((tm, tn), lambda i, j, kk: (i, j)),
        scratch_shapes=[pltpu.VMEM((tm, tn), F32)],
        compiler_params=_params(("parallel", "parallel", "arbitrary")),
    )(dgu, wg_up, wg_up)


def _mm_up_bwd_w(name, n2, dgu, tm, tk):
    s, d = n2.shape
    tk = min(tk, s)
    tn = dgu.shape[1] // (2 * N_CHIPS)
    return _matmul(name, n2, dgu, (d // tm, 2 * N_CHIPS, s // tk),
                   pl.BlockSpec((tk, tm), lambda i, j, kk: (kk, i)),
                   pl.BlockSpec((tk, tn), lambda i, j, kk: (kk, j)),
                   pl.BlockSpec((None, tm, tn), lambda i, j, kk: (_up_shard(j)[0], i, _up_shard(j)[1])),
                   jax.ShapeDtypeStruct((N_CHIPS, d, 2 * tn), F32), TN)


def _rmsnorm_fwd(name, x, g, tm=512):
    s, d = x.shape
    tm = min(tm, s)

    def body(x_ref, g_ref, n_ref, r_ref):
        xf = x_ref[...]
        r = lax.rsqrt(jnp.mean(xf * xf, axis=-1, keepdims=True) + EPS)
        n_ref[...] = ((xf * r) * g_ref[...]).astype(BF16)
        r_ref[...] = r

    return pl.pallas_call(
        body, grid=(s // tm,), name=name,
        in_specs=[pl.BlockSpec((tm, d), lambda i: (i, 0)), pl.BlockSpec((1, d), lambda i: (0, 0))],
        out_specs=[pl.BlockSpec((tm, d), lambda i: (i, 0)), pl.BlockSpec((tm, 1), lambda i: (i, 0))],
        out_shape=[jax.ShapeDtypeStruct((s, d), BF16), jax.ShapeDtypeStruct((s, 1), F32)],
        compiler_params=_params(("parallel",)),
    )(x, g)


def _rmsnorm_bwd(name, x, rstd, g, dn, dres, tm=512):
    s, d = x.shape
    tm = min(tm, s)

    def body(x_ref, r_ref, g_ref, dn_ref, dres_ref, dx_ref, dxb_ref, dg_ref):
        @pl.when(pl.program_id(0) == 0)
        def _():
            dg_ref[...] = jnp.zeros_like(dg_ref)

        r = r_ref[...]
        xhat = x_ref[...] * r
        dn_v = dn_ref[...]
        dxhat = dn_v * g_ref[...]
        dx = dres_ref[...] + r * (dxhat - xhat * jnp.mean(dxhat * xhat, axis=-1, keepdims=True))
        dx_ref[...] = dx
        dxb_ref[...] = dx.astype(BF16)
        dg_ref[...] += jnp.sum(dn_v * xhat, axis=0, keepdims=True)

    row = pl.BlockSpec((tm, d), lambda i: (i, 0))
    vec = pl.BlockSpec((1, d), lambda i: (0, 0))
    return pl.pallas_call(
        body, grid=(s // tm,), name=name,
        in_specs=[row, pl.BlockSpec((tm, 1), lambda i: (i, 0)), vec, row, row],
        out_specs=[row, row, vec],
        out_shape=[jax.ShapeDtypeStruct((s, d), F32), jax.ShapeDtypeStruct((s, d), BF16),
                   jax.ShapeDtypeStruct((1, d), F32)],
        compiler_params=_params(("arbitrary",)),
    )(x, rstd, g, dn, dres)


def _loss_head(name, x, g, target, tm=512):
    s, d = x.shape
    tm = min(tm, s)

    def body(x_ref, g_ref, t_ref, loss_ref, dx_ref, dxb_ref, dg_ref):
        @pl.when(pl.program_id(0) == 0)
        def _():
            dg_ref[...] = jnp.zeros_like(dg_ref)
            loss_ref[...] = jnp.zeros_like(loss_ref)

        xf = x_ref[...]
        gv = g_ref[...]
        r = lax.rsqrt(jnp.mean(xf * xf, axis=-1, keepdims=True) + EPS)
        xhat = xf * r
        err = xhat * gv - t_ref[...]
        loss_ref[...] += 0.5 * jnp.sum(jnp.mean(err * err, axis=-1, keepdims=True), axis=0, keepdims=True)
        dy = err * (1.0 / d)
        dxhat = dy * gv
        dx = r * (dxhat - xhat * jnp.mean(dxhat * xhat, axis=-1, keepdims=True))
        dx_ref[...] = dx
        dxb_ref[...] = dx.astype(BF16)
        dg_ref[...] += jnp.sum(dy * xhat, axis=0, keepdims=True)

    row = pl.BlockSpec((tm, d), lambda i: (i, 0))
    vec = pl.BlockSpec((1, d), lambda i: (0, 0))
    return pl.pallas_call(
        body, grid=(s // tm,), name=name,
        in_specs=[row, vec, row],
        out_specs=[pl.BlockSpec((1, 1), lambda i: (0, 0)), row, row, vec],
        out_shape=[jax.ShapeDtypeStruct((1, 1), F32), jax.ShapeDtypeStruct((s, d), F32),
                   jax.ShapeDtypeStruct((s, d), BF16), jax.ShapeDtypeStruct((1, d), F32)],
        compiler_params=_params(("arbitrary",)),
    )(x, g, target)


def _sigmoid(v):
    return 1.0 / (1.0 + jnp.exp(-v))


def _log_sigmoid(v):
    return jnp.minimum(v, 0.0) - jnp.log1p(jnp.exp(-jnp.abs(v)))


def _softplus(v):
    return jnp.maximum(v, 0.0) + jnp.log1p(jnp.exp(-jnp.abs(v)))


def _one_minus_exp(z):
    series = -z * (1.0 + z * (0.5 + z * (1.0 / 6.0 + z * (1.0 / 24.0 + z * (1.0 / 120.0)))))
    return jnp.where(z > -0.1, series, 1.0 - jnp.exp(z))


_GELU_K = 0.7978845608028654
_GELU_C = 0.044715


def _gelu(v):
    return 0.5 * v * (1.0 + jnp.tanh(_GELU_K * (v + _GELU_C * v * v * v)))


def _gelu_grad(v):
    t = jnp.tanh(_GELU_K * (v + _GELU_C * v * v * v))
    return 0.5 * (1.0 + t) + 0.5 * v * (1.0 - t * t) * _GELU_K * (1.0 + 3.0 * _GELU_C * v * v)


def _rows(shape):
    return lax.broadcasted_iota(jnp.int32, shape, 0)


def _cols(shape):
    return lax.broadcasted_iota(jnp.int32, shape, 1)


def _shift_down(v, prev, d):
    if d == 0:
        return v
    rolled = pltpu.roll(v, d, axis=0)
    head = jnp.where(_rows((HALO, v.shape[1])) >= d, rolled[:HALO], pltpu.roll(prev, d, axis=0))
    if v.shape[0] == HALO:
        return head
    return jnp.concatenate([head, rolled[HALO:]], axis=0)


def _shift_up(v, nxt, d):
    if d == 0:
        return v
    n = v.shape[0]
    rolled = pltpu.roll(v, n - d, axis=0)
    tail = jnp.where(_rows((HALO, v.shape[1])) < HALO - d, rolled[n - HALO:], pltpu.roll(nxt, HALO - d, axis=0))
    if n == HALO:
        return tail
    return jnp.concatenate([rolled[:n - HALO], tail], axis=0)


def _dot(a, b, contract):
    return lax.dot_general(a.astype(BF16), b.astype(BF16), (contract, ((), ())), preferred_element_type=F32)


def _mlstm_chunk_common(h, q_ref, k_ref, v_ref, gcol_ref, grow_ref, brow_ref, bcol_ref, m_prev):
    L = CHUNK
    sl = slice(h * HEAD_DIM, (h + 1) * HEAD_DIM)
    qh = q_ref[:, sl]
    kh = k_ref[:, sl]
    vh = v_ref[:, sl]
    qs = qh * (HEAD_DIM ** -0.5)
    gates = gcol_ref[...] + brow_ref[...]
    lane = _cols(gates.shape)
    ic = jnp.sum(jnp.where(lane == h, gates, 0.0), axis=1, keepdims=True)
    fc = jnp.sum(jnp.where(lane == HEADS + h, gates, 0.0), axis=1, keepdims=True)
    ir = grow_ref[h:h + 1, :] + bcol_ref[h:h + 1, :]
    fr = grow_ref[HEADS + h:HEADS + h + 1, :] + bcol_ref[HEADS + h:HEADS + h + 1, :]
    logf_c = _log_sigmoid(fc)
    logf_r = _log_sigmoid(fr)
    t_i = _rows((L, L))
    s_i = _cols((L, L))
    tri = t_i >= s_i
    b_c = jnp.sum(jnp.where(tri, logf_r, 0.0), axis=1, keepdims=True)
    b_r = jnp.sum(jnp.where(t_i <= s_i, logf_c, 0.0), axis=0, keepdims=True)
    btot = jnp.sum(logf_r, axis=1, keepdims=True)
    dmat = jnp.where(tri, b_c - b_r + ir, -jnp.inf)
    m_inter = b_c + m_prev
    m_t = jnp.maximum(m_inter, jnp.max(dmat, axis=1, keepdims=True))
    e_mat = jnp.exp(dmat - m_t)
    e_inter = jnp.exp(m_inter - m_t)
    wqk = _dot(qs, kh, NT) * e_mat
    w_end_r = btot - b_r + ir
    m_loc = jnp.max(w_end_r, axis=1, keepdims=True)
    e_end_c = jnp.exp(btot - b_c + ic - m_loc)
    m_new = jnp.maximum(btot + m_prev, m_loc)
    a_dec = jnp.exp(btot + m_prev - m_new)
    c_inj = jnp.exp(m_loc - m_new)
    return dict(qh=qh, kh=kh, vh=vh, qs=qs, fc=fc, tri=tri, t_i=t_i, s_i=s_i, m_t=m_t, e_mat=e_mat,
                e_inter=e_inter, wqk=wqk, e_end_c=e_end_c, m_new=m_new, a_dec=a_dec, c_inj=c_inj)


def _mlstm_fwd(proj, gates_t, bias_row, bias_col, head_g):
    s = proj.shape[0]
    nc = s // CHUNK
    L = CHUNK

    def body(q_ref, k_ref, v_ref, o_ref, gcol_ref, grow_ref, brow_ref, bcol_ref, hg_ref,
             out_ref, cprev_ref, nprev_ref, mprev_ref, c_scr, n_scr, m_scr):
        @pl.when(pl.program_id(0) == 0)
        def _():
            c_scr[...] = jnp.zeros_like(c_scr)
            n_scr[...] = jnp.zeros_like(n_scr)
            m_scr[...] = jnp.zeros_like(m_scr)

        for h in range(HEADS):
            sl = slice(h * HEAD_DIM, (h + 1) * HEAD_DIM)
            m_prev = m_scr[h:h + 1, 0:1]
            n_prev = n_scr[h:h + 1, :]
            c_prev = c_scr[h].astype(BF16)
            q = _mlstm_chunk_common(h, q_ref, k_ref, v_ref, gcol_ref, grow_ref, brow_ref, bcol_ref, m_prev)
            num = _dot(q["wqk"], q["vh"], NN) + q["e_inter"] * _dot(q["qs"], c_prev, NN)
            den = (jnp.sum(q["wqk"], axis=1, keepdims=True)
                   + q["e_inter"] * jnp.sum(q["qs"] * n_prev, axis=1, keepdims=True))
            hh = num / jnp.maximum(jnp.abs(den), jnp.exp(-q["m_t"]))
            hn = hh * lax.rsqrt(jnp.mean(hh * hh, axis=1, keepdims=True) + EPS) * hg_ref[h:h + 1, :]
            out_ref[:, sl] = (_sigmoid(o_ref[:, sl]) * hn).astype(BF16)
            cprev_ref[h] = c_prev
            nprev_ref[h:h + 1, :] = n_prev
            mprev_ref[h:h + 1, :] = jnp.broadcast_to(m_prev, (1, LANES))
            c_loc = _dot(q["kh"], q["e_end_c"] * q["vh"], TN)
            n_loc = jnp.sum(q["e_end_c"] * q["kh"], axis=0, keepdims=True)
            c_scr[h] = q["a_dec"] * c_scr[h] + q["c_inj"] * c_loc
            n_scr[h:h + 1, :] = q["a_dec"] * n_prev + q["c_inj"] * n_loc
            m_scr[h:h + 1, :] = jnp.broadcast_to(q["m_new"], (1, LANES))

    blk = lambda j: pl.BlockSpec((L, D_MLSTM), lambda c, j=j: (c, j))
    full = lambda shp: pl.BlockSpec(shp, lambda c: tuple(0 for _ in shp))
    return pl.pallas_call(
        body, grid=(nc,), name="mlstm_fwd",
        in_specs=[blk(0), blk(1), blk(2), blk(3),
                  pl.BlockSpec((L, LANES), lambda c: (c, (4 * D_MLSTM + 2 * D_LRU) // LANES)),
                  pl.BlockSpec((None, 2 * HEADS, L), lambda c: (c, 0, 0)),
                  full((1, LANES)), full((2 * HEADS, 1)), full((HEADS, HEAD_DIM))],
        out_specs=[pl.BlockSpec((L, D_MLSTM), lambda c: (c, 0)),
                   pl.BlockSpec((None, HEADS, HEAD_DIM, HEAD_DIM), lambda c: (c, 0, 0, 0)),
                   pl.BlockSpec((None, HEADS, HEAD_DIM), lambda c: (c, 0, 0)),
                   pl.BlockSpec((None, HEADS, LANES), lambda c: (c, 0, 0))],
        out_shape=[jax.ShapeDtypeStruct((s, D_MLSTM + D_LRU), BF16),
                   jax.ShapeDtypeStruct((nc, HEADS, HEAD_DIM, HEAD_DIM), BF16),
                   jax.ShapeDtypeStruct((nc, HEADS, HEAD_DIM), F32),
                   jax.ShapeDtypeStruct((nc, HEADS, LANES), F32)],
        scratch_shapes=[pltpu.VMEM((HEADS, HEAD_DIM, HEAD_DIM), F32), pltpu.VMEM((HEADS, HEAD_DIM), F32),
                        pltpu.VMEM((HEADS, LANES), F32)],
        compiler_params=_params(("arbitrary",)),
    )(proj, proj, proj, proj, proj, gates_t, bias_row, bias_col, head_g)


def _mlstm_bwd(proj, gates_t, bias_row, bias_col, head_g, cprev, nprev, mprev, dmix):
    s = proj.shape[0]
    nc = s // CHUNK
    L = CHUNK

    def body(q_ref, k_ref, v_ref, o_ref, gcol_ref, grow_ref, brow_ref, bcol_ref, hg_ref,
             cprev_ref, nprev_ref, mprev_ref, dmix_ref,
             dqkvo_ref, dgate_ref, dhg_ref, g_scr, gn_scr):
        @pl.when(pl.program_id(0) == 0)
        def _():
            g_scr[...] = jnp.zeros_like(g_scr)
            gn_scr[...] = jnp.zeros_like(gn_scr)
            dhg_ref[...] = jnp.zeros_like(dhg_ref)

        lane = _cols((L, LANES))
        dgate = jnp.zeros((L, LANES), F32)
        for h in range(HEADS):
            sl = slice(h * HEAD_DIM, (h + 1) * HEAD_DIM)
            m_prev = mprev_ref[h:h + 1, 0:1]
            n_prev = nprev_ref[h:h + 1, :]
            c_prev = cprev_ref[h]
            q = _mlstm_chunk_common(h, q_ref, k_ref, v_ref, gcol_ref, grow_ref, brow_ref, bcol_ref, m_prev)
            qh, kh, vh, qs, wqk, e_inter = q["qh"], q["kh"], q["vh"], q["qs"], q["wqk"], q["e_inter"]
            num_state = e_inter * _dot(qs, c_prev, NN)
            den_state = e_inter * jnp.sum(qs * n_prev, axis=1, keepdims=True)
            num = _dot(wqk, vh, NN) + num_state
            den = jnp.sum(wqk, axis=1, keepdims=True) + den_state
            floor = jnp.exp(-q["m_t"])
            denom = jnp.maximum(jnp.abs(den), floor)
            hh = num / denom
            rn = lax.rsqrt(jnp.mean(hh * hh, axis=1, keepdims=True) + EPS)
            hn_pre = hh * rn
            hg = hg_ref[h:h + 1, :]
            sg = _sigmoid(o_ref[:, sl])
            dout = dmix_ref[:, sl]
            d_o = dout * (hn_pre * hg) * sg * (1.0 - sg)
            dhn = dout * sg
            dhg_ref[h:h + 1, :] += jnp.sum(dhn * hn_pre, axis=0, keepdims=True)
            dhn_pre = dhn * hg
            dhh = rn * (dhn_pre - hn_pre * jnp.mean(dhn_pre * hn_pre, axis=1, keepdims=True))
            dnum = dhh / denom
            dden = jnp.where(jnp.abs(den) >= floor,
                             -jnp.sum(hh * dhh, axis=1, keepdims=True) / denom * jnp.sign(den), 0.0)
            dwqk = _dot(dnum, vh, NT) + dden
            dv = _dot(wqk, dnum, TN)
            dp = dwqk * q["e_mat"]
            dqs = _dot(dp, kh, NN) + e_inter * (_dot(dnum, c_prev, NT) + dden * n_prev)
            dk = _dot(dp, qs, TN)
            g_next = g_scr[h]
            gn_next = gn_scr[h:h + 1, :]
            w_state = q["e_end_c"] * q["c_inj"]
            dk_state = w_state * (_dot(vh, g_next, NT) + gn_next)
            dk = dk + dk_state
            dv = dv + w_state * _dot(kh, g_next, NN)
            dq = dqs * (HEAD_DIM ** -0.5)
            eye = q["t_i"] == q["s_i"]
            to_row = lambda col: jnp.sum(jnp.where(eye, col, 0.0), axis=0, keepdims=True)
            to_col = lambda row: jnp.sum(jnp.where(eye, row, 0.0), axis=1, keepdims=True)
            g_pair = dwqk * wqk
            rs_in = jnp.sum(g_pair, axis=1, keepdims=True)
            cs_in_r = jnp.sum(g_pair, axis=0, keepdims=True)
            rs_state = (jnp.sum(dnum * num_state, axis=1, keepdims=True) + dden * den_state)
            cs_state = jnp.sum(kh * dk_state, axis=1, keepdims=True)
            di_c = to_col(cs_in_r) + cs_state
            through = q["a_dec"] * (jnp.sum(jnp.sum(g_next * c_prev.astype(F32), axis=1, keepdims=True),
                                            axis=0, keepdims=True)
                                    + jnp.sum(gn_next * n_prev, axis=1, keepdims=True))
            ends_here = to_row(rs_in + rs_state) - cs_in_r
            da_c = (jnp.sum(jnp.where(q["s_i"] >= q["t_i"], ends_here, 0.0), axis=1, keepdims=True)
                    + jnp.sum(jnp.where(q["s_i"] < q["t_i"], to_row(cs_state), 0.0), axis=1, keepdims=True)
                    + through)
            df_c = da_c * _sigmoid(-q["fc"])
            dgate = dgate + jnp.where(lane == h, di_c, 0.0) + jnp.where(lane == HEADS + h, df_c, 0.0)
            dqkvo_ref[:, sl] = dq.astype(BF16)
            dqkvo_ref[:, D_MLSTM + h * HEAD_DIM:D_MLSTM + (h + 1) * HEAD_DIM] = dk.astype(BF16)
            dqkvo_ref[:, 2 * D_MLSTM + h * HEAD_DIM:2 * D_MLSTM + (h + 1) * HEAD_DIM] = dv.astype(BF16)
            dqkvo_ref[:, 3 * D_MLSTM + h * HEAD_DIM:3 * D_MLSTM + (h + 1) * HEAD_DIM] = d_o.astype(BF16)
            g_scr[h] = q["a_dec"] * g_next + _dot(e_inter * qs, dnum, TN)
            gn_scr[h:h + 1, :] = q["a_dec"] * gn_next + jnp.sum(e_inter * qs * dden, axis=0, keepdims=True)
        dgate_ref[...] = dgate

    rev = lambda c: nc - 1 - c
    blk = lambda j: pl.BlockSpec((L, D_MLSTM), lambda c, j=j: (rev(c), j))
    full = lambda shp: pl.BlockSpec(shp, lambda c: tuple(0 for _ in shp))
    return pl.pallas_call(
        body, grid=(nc,), name="mlstm_bwd",
        in_specs=[blk(0), blk(1), blk(2), blk(3),
                  pl.BlockSpec((L, LANES), lambda c: (rev(c), (4 * D_MLSTM + 2 * D_LRU) // LANES)),
                  pl.BlockSpec((None, 2 * HEADS, L), lambda c: (rev(c), 0, 0)),
                  full((1, LANES)), full((2 * HEADS, 1)), full((HEADS, HEAD_DIM)),
                  pl.BlockSpec((None, HEADS, HEAD_DIM, HEAD_DIM), lambda c: (rev(c), 0, 0, 0)),
                  pl.BlockSpec((None, HEADS, HEAD_DIM), lambda c: (rev(c), 0, 0)),
                  pl.BlockSpec((None, HEADS, LANES), lambda c: (rev(c), 0, 0)),
                  pl.BlockSpec((L, D_MLSTM), lambda c: (rev(c), 0))],
        out_specs=[pl.BlockSpec((L, 4 * D_MLSTM), lambda c: (rev(c), 0)),
                   pl.BlockSpec((L, LANES), lambda c: (rev(c), 0)),
                   full((HEADS, HEAD_DIM))],
        out_shape=[jax.ShapeDtypeStruct((s, _PROJ_PAD), BF16),
                   jax.ShapeDtypeStruct((s, LANES), F32),
                   jax.ShapeDtypeStruct((HEADS, HEAD_DIM), F32)],
        scratch_shapes=[pltpu.VMEM((HEADS, HEAD_DIM, HEAD_DIM), F32), pltpu.VMEM((HEADS, HEAD_DIM), F32)],
        compiler_params=_params(("arbitrary",)),
    )(proj, proj, proj, proj, proj, gates_t, bias_row, bias_col, head_g, cprev, nprev, mprev, dmix)


def _lru_gates(xc, wa_ref, wx_ref, ba, bx, lam):
    r = _sigmoid(_dot(xc, wa_ref[...], NN) + ba)
    ig = _sigmoid(_dot(xc, wx_ref[...], NN) + bx)
    sp = _softplus(-lam)
    log_a = (-LRU_C * r) * sp
    a = jnp.exp(log_a)
    mult = jnp.sqrt(_one_minus_exp(2.0 * log_a))
    return r, ig, sp, a, mult


def _lru_conv(xr, prev, w_ref, b):
    xc = b + _shift_down(xr, prev, 3) * w_ref[0:1, :]
    for j in range(1, LRU_CONV):
        xc = xc + _shift_down(xr, prev, LRU_CONV - 1 - j) * w_ref[j:j + 1, :]
    return xc


def _lru_fwd(proj, mix, conv_w, conv_b, wa, wx, ba, bx, lam, tt=1024):
    s = proj.shape[0]
    tt = min(tt, s)
    nt = s // tt
    B = LRU_BLOCK_DIM
    lru_col = 4 * D_MLSTM // B
    mix_col = D_MLSTM // B

    def body(xr_ref, gr_ref, cw_ref, cb_ref, wa_ref, wx_ref, ba_ref, bx_ref, lam_ref, mix_in_ref,
             out_ref, h_ref, prev_scr, hcar_scr):
        @pl.when(pl.program_id(1) == 0)
        def _():
            prev_scr[...] = jnp.zeros_like(prev_scr)
            hcar_scr[...] = jnp.zeros_like(hcar_scr)

        xr = xr_ref[...]
        xc = _lru_conv(xr, prev_scr[...], cw_ref, cb_ref[...])
        prev_scr[...] = xr[tt - HALO:, :]
        _, ig, _, a, mult = _lru_gates(xc, wa_ref, wx_ref, ba_ref[...], bx_ref[...], lam_ref[...])
        u = mult * (ig * xc)
        rows = _rows((tt, B))
        acc_a, acc_b = a, u
        d = 1
        while d < tt:
            if d < HALO:
                keep = rows >= d
                sh_a = jnp.where(keep, pltpu.roll(acc_a, d, axis=0), 1.0)
                sh_b = jnp.where(keep, pltpu.roll(acc_b, d, axis=0), 0.0)
            else:
                sh_a = jnp.concatenate([jnp.ones((d, B), F32), acc_a[:tt - d]], axis=0)
                sh_b = jnp.concatenate([jnp.zeros((d, B), F32), acc_b[:tt - d]], axis=0)
            acc_b = acc_a * sh_b + acc_b
            acc_a = acc_a * sh_a
            d *= 2
        hv = acc_b + acc_a * hcar_scr[0:1, :]
        hcar_scr[...] = jnp.broadcast_to(hv[tt - 1:tt, :], hcar_scr.shape)
        h_ref[...] = hv
        out_ref[...] = (hv * _gelu(gr_ref[...])).astype(BF16)

    chan = lambda rws: pl.BlockSpec((rws, B), lambda n, i: (0, n))
    return pl.pallas_call(
        body, grid=(LRU_BLOCKS, nt), name="lru_fwd",
        in_specs=[pl.BlockSpec((tt, B), lambda n, i: (i, lru_col + 2 * n)),
                  pl.BlockSpec((tt, B), lambda n, i: (i, lru_col + 2 * n + 1)),
                  chan(LRU_CONV), chan(1),
                  pl.BlockSpec((None, B, B), lambda n, i: (n, 0, 0)),
                  pl.BlockSpec((None, B, B), lambda n, i: (n, 0, 0)),
                  chan(1), chan(1), chan(1), ANY],
        out_specs=[pl.BlockSpec((tt, B), lambda n, i: (i, mix_col + n)), pl.BlockSpec((tt, B), lambda n, i: (i, n))],
        out_shape=[jax.ShapeDtypeStruct(mix.shape, BF16), jax.ShapeDtypeStruct((s, D_LRU), F32)],
        scratch_shapes=[pltpu.VMEM((HALO, B), F32), pltpu.VMEM((HALO, B), F32)],
        input_output_aliases={9: 0},
        compiler_params=_params(("parallel", "arbitrary")),
    )(proj, proj, conv_w, conv_b, wa, wx, ba, bx, lam, mix)


def _lru_bwd(proj, hsave, dmix, dproj, conv_w, conv_b, wa, wx, ba, bx, lam, tt=1024):
    s = proj.shape[0]
    tt = min(tt, s)
    nt = s // tt
    B = LRU_BLOCK_DIM
    lru_col = 4 * D_MLSTM // B
    dmix_col = D_MLSTM // B
    hpb = tt // HALO

    def body(xr_ref, xprev_ref, gr_ref, h_ref, hprev_ref, dmix_ref, cw_ref, cb_ref, wa_ref, wx_ref,
             ba_ref, bx_ref, lam_ref, dproj_in_ref,
             dxg_ref, dcw_ref, dcb_ref, dwa_ref, dwx_ref, dba_ref, dbx_ref, dlam_ref,
             gcar_scr, acar_scr, dxc_scr):
        i = pl.program_id(1)
        first_tile = i == nt - 1

        @pl.when(i == 0)
        def _():
            gcar_scr[...] = jnp.zeros_like(gcar_scr)
            acar_scr[...] = jnp.zeros_like(acar_scr)
            dxc_scr[...] = jnp.zeros_like(dxc_scr)
            for ref in (dcw_ref, dcb_ref, dwa_ref, dwx_ref, dba_ref, dbx_ref, dlam_ref):
                ref[...] = jnp.zeros_like(ref)

        xr = xr_ref[...]
        xprev = jnp.where(first_tile, 0.0, xprev_ref[...])
        hprev = jnp.where(first_tile, 0.0, hprev_ref[...])
        lam = lam_ref[...]
        taps = [_shift_down(xr, xprev, LRU_CONV - 1 - j) for j in range(LRU_CONV)]
        xc = cb_ref[...] + taps[0] * cw_ref[0:1, :]
        for j in range(1, LRU_CONV):
            xc = xc + taps[j] * cw_ref[j:j + 1, :]
        r, ig, sp, a, mult = _lru_gates(xc, wa_ref, wx_ref, ba_ref[...], bx_ref[...], lam)
        gr = gr_ref[...]
        hv = h_ref[...]
        dout = dmix_ref[...]
        dxg_ref[:, B:] = (dout * hv * _gelu_grad(gr)).astype(BF16)
        dh = dout * _gelu(gr)
        rows = _rows((tt, B))
        acc_a = _shift_up(a, acar_scr[...], 1)
        acc_b = dh
        d = 1
        while d < tt:
            if d < HALO:
                keep = rows < tt - d
                sh_a = jnp.where(keep, pltpu.roll(acc_a, tt - d, axis=0), 1.0)
                sh_b = jnp.where(keep, pltpu.roll(acc_b, tt - d, axis=0), 0.0)
            else:
                sh_a = jnp.concatenate([acc_a[d:], jnp.ones((d, B), F32)], axis=0)
                sh_b = jnp.concatenate([acc_b[d:], jnp.zeros((d, B), F32)], axis=0)
            acc_b = acc_a * sh_b + acc_b
            acc_a = acc_a * sh_a
            d *= 2
        gv = acc_b + acc_a * gcar_scr[0:1, :]
        gcar_scr[...] = jnp.broadcast_to(gv[0:1, :], gcar_scr.shape)
        acar_scr[...] = jnp.broadcast_to(a[0:1, :], acar_scr.shape)
        h_before = _shift_down(hv, hprev, 1)
        da = gv * h_before
        dmult = gv * (ig * xc)
        dig = gv * mult * xc
        dxc = gv * mult * ig
        dlog_a = da * a - dmult * (a * a) / mult
        dr = dlog_a * (-LRU_C * sp)
        dlam_ref[...] += jnp.sum(dlog_a * (-LRU_C * r), axis=0, keepdims=True) * (-_sigmoid(-lam))
        dpre_r = dr * r * (1.0 - r)
        dpre_i = dig * ig * (1.0 - ig)
        dba_ref[...] += jnp.sum(dpre_r, axis=0, keepdims=True)
        dbx_ref[...] += jnp.sum(dpre_i, axis=0, keepdims=True)
        dwa_ref[...] += _dot(xc, dpre_r, TN)
        dwx_ref[...] += _dot(xc, dpre_i, TN)
        dxc = dxc + _dot(dpre_r, wa_ref[...], NT) + _dot(dpre_i, wx_ref[...], NT)
        dcb_ref[...] += jnp.sum(dxc, axis=0, keepdims=True)
        nxt = dxc_scr[...]
        dxr = jnp.zeros((tt, B), F32)
        for j in range(LRU_CONV):
            sft = LRU_CONV - 1 - j
            dcw_ref[j:j + 1, :] += jnp.sum(dxc * taps[j], axis=0, keepdims=True)
            dxr = dxr + _shift_up(dxc, nxt, sft) * cw_ref[j:j + 1, :]
        dxc_scr[...] = dxc[:HALO, :]
        dxg_ref[:, :B] = dxr.astype(BF16)

    rev = lambda i: nt - 1 - i
    tile = lambda col, step: pl.BlockSpec((tt, B), lambda n, i: (rev(i), col + step * n))
    halo = lambda col, step: pl.BlockSpec(
        (HALO, B), lambda n, i: (jnp.maximum(rev(i) * hpb - 1, 0), col + step * n))
    chan = lambda rws: pl.BlockSpec((rws, B), lambda n, i: (0, n))
    wblk = pl.BlockSpec((None, B, B), lambda n, i: (n, 0, 0))
    return pl.pallas_call(
        body, grid=(LRU_BLOCKS, nt), name="lru_bwd",
        in_specs=[tile(lru_col, 2), halo(lru_col, 2), tile(lru_col + 1, 2), tile(0, 1), halo(0, 1),
                  tile(dmix_col, 1), chan(LRU_CONV), chan(1), wblk, wblk, chan(1), chan(1), chan(1), ANY],
        out_specs=[pl.BlockSpec((tt, 2 * B), lambda n, i: (rev(i), lru_col // 2 + n)),
                   chan(LRU_CONV), chan(1), wblk, wblk, chan(1), chan(1), chan(1)],
        out_shape=[jax.ShapeDtypeStruct(dproj.shape, BF16),
                   jax.ShapeDtypeStruct((LRU_CONV, D_LRU), F32), jax.ShapeDtypeStruct((1, D_LRU), F32),
                   jax.ShapeDtypeStruct((LRU_BLOCKS, B, B), F32), jax.ShapeDtypeStruct((LRU_BLOCKS, B, B), F32),
                   jax.ShapeDtypeStruct((1, D_LRU), F32), jax.ShapeDtypeStruct((1, D_LRU), F32),
                   jax.ShapeDtypeStruct((1, D_LRU), F32)],
        scratch_shapes=[pltpu.VMEM((HALO, B), F32), pltpu.VMEM((HALO, B), F32), pltpu.VMEM((HALO, B), F32)],
        input_output_aliases={13: 0},
        compiler_params=_params(("parallel", "arbitrary")),
    )(proj, proj, proj, hsave, hsave, dmix, conv_w, conv_b, wa, wx, ba, bx, lam, dproj)


def _ffn_conv(gp, prev, w_ref, b):
    g = b + _shift_down(gp, prev, 2) * w_ref[0:1, :]
    for j in range(1, FFN_CONV):
        g = g + _shift_down(gp, prev, FFN_CONV - 1 - j) * w_ref[j:j + 1, :]
    return g


def _ffn_act_fwd(gu, conv_w, conv_b, tt=1024):
    s = gu.shape[0]
    tt = min(tt, s)
    d_ff = conv_w.shape[1]
    tc = d_ff // N_CHIPS
    hpb = tt // HALO

    def body(g_ref, gprev_ref, u_ref, w_ref, b_ref, act_ref):
        prev = jnp.where(pl.program_id(0) == 0, 0.0, gprev_ref[...])
        gate = _ffn_conv(g_ref[...], prev, w_ref, b_ref[...])
        act_ref[...] = (gate * _sigmoid(gate) * u_ref[...]).astype(BF16)

    return pl.pallas_call(
        body, grid=(s // tt, N_CHIPS), name="ffn_act_fwd",
        in_specs=[pl.BlockSpec((tt, tc), lambda i, j: (i, 2 * j)),
                  pl.BlockSpec((HALO, tc), lambda i, j: (jnp.maximum(i * hpb - 1, 0), 2 * j)),
                  pl.BlockSpec((tt, tc), lambda i, j: (i, 2 * j + 1)),
                  pl.BlockSpec((FFN_CONV, tc), lambda i, j: (0, j)),
                  pl.BlockSpec((1, tc), lambda i, j: (0, j))],
        out_specs=pl.BlockSpec((tt, tc), lambda i, j: (i, j)),
        out_shape=jax.ShapeDtypeStruct((s, d_ff), BF16),
        compiler_params=_params(("parallel", "parallel")),
    )(gu, gu, gu, conv_w, conv_b)


def _ffn_act_bwd(gu, dact, conv_w, conv_b, tt=512):
    s = gu.shape[0]
    tt = min(tt, s)
    nt = s // tt
    d_ff = conv_w.shape[1]
    tc = d_ff // N_CHIPS
    hpb = tt // HALO

    def dgate_of(gate, up, da):
        sg = _sigmoid(gate)
        return da * up * (sg * (1.0 + gate * (1.0 - sg))), da * (gate * sg)

    def body(g_ref, gprev_ref, gnext_ref, u_ref, unext_ref, da_ref, danext_ref, w_ref, b_ref,
             dgu_ref, dw_ref, db_ref):
        i = pl.program_id(1)

        @pl.when(i == 0)
        def _():
            dw_ref[...] = jnp.zeros_like(dw_ref)
            db_ref[...] = jnp.zeros_like(db_ref)

        gp = g_ref[...]
        prev = jnp.where(i == 0, 0.0, gprev_ref[...])
        bias = b_ref[...]
        taps = [_shift_down(gp, prev, FFN_CONV - 1 - j) for j in range(FFN_CONV)]
        gate = bias + taps[0] * w_ref[0:1, :]
        for j in range(1, FFN_CONV):
            gate = gate + taps[j] * w_ref[j:j + 1, :]
        dgate, dup = dgate_of(gate, u_ref[...], da_ref[...])
        gate_n = _ffn_conv(gnext_ref[...], gp[tt - HALO:, :], w_ref, bias)
        dgate_n, _ = dgate_of(gate_n, unext_ref[...], danext_ref[...])
        dgate_n = jnp.where(i == nt - 1, 0.0, dgate_n)
        db_ref[...] += jnp.sum(dgate, axis=0, keepdims=True)
        dgp = jnp.zeros((tt, tc), F32)
        for j in range(FFN_CONV):
            dw_ref[j:j + 1, :] += jnp.sum(dgate * taps[j], axis=0, keepdims=True)
            dgp = dgp + _shift_up(dgate, dgate_n, FFN_CONV - 1 - j) * w_ref[j:j + 1, :]
        dgu_ref[:, :tc] = dgp.astype(BF16)
        dgu_ref[:, tc:] = dup.astype(BF16)

    tile = lambda half: pl.BlockSpec((tt, tc), lambda j, i, half=half: (i, 2 * j + half))
    hprev = lambda half: pl.BlockSpec((HALO, tc), lambda j, i, half=half: (jnp.maximum(i * hpb - 1, 0), 2 * j + half))
    hnext = lambda half: pl.BlockSpec(
        (HALO, tc), lambda j, i, half=half: (jnp.minimum((i + 1) * hpb, nt * hpb - 1), 2 * j + half))
    return pl.pallas_call(
        body, grid=(N_CHIPS, nt), name="ffn_act_bwd",
        in_specs=[tile(0), hprev(0), hnext(0), tile(1), hnext(1),
                  pl.BlockSpec((tt, tc), lambda j, i: (i, j)),
                  pl.BlockSpec((HALO, tc), lambda j, i: (jnp.minimum((i + 1) * hpb, nt * hpb - 1), j)),
                  pl.BlockSpec((FFN_CONV, tc), lambda j, i: (0, j)),
                  pl.BlockSpec((1, tc), lambda j, i: (0, j))],
        out_specs=[pl.BlockSpec((tt, 2 * tc), lambda j, i: (i, j)),
                   pl.BlockSpec((FFN_CONV, tc), lambda j, i: (0, j)),
                   pl.BlockSpec((1, tc), lambda j, i: (0, j))],
        out_shape=[jax.ShapeDtypeStruct((s, 2 * d_ff), BF16),
                   jax.ShapeDtypeStruct((FFN_CONV, d_ff), F32), jax.ShapeDtypeStruct((1, d_ff), F32)],
        compiler_params=_params(("parallel", "arbitrary")),
    )(gu, gu, gu, gu, gu, dact, dact, conv_w, conv_b)


def _gate_grads(dgate, dproj, tm=512):
    s, n = dgate.shape
    tm = min(tm, s)

    def body(a_ref, dproj_in_ref, o_ref, dproj_ref):
        @pl.when(pl.program_id(0) == 0)
        def _():
            o_ref[...] = jnp.zeros_like(o_ref)
        a = a_ref[...]
        o_ref[...] += jnp.sum(a, axis=0, keepdims=True)
        dproj_ref[...] = a.astype(BF16)

    return pl.pallas_call(
        body, grid=(s // tm,), name="gate_grads",
        in_specs=[pl.BlockSpec((tm, n), lambda i: (i, 0)), ANY],
        out_specs=[pl.BlockSpec((1, n), lambda i: (0, 0)),
                   pl.BlockSpec((tm, n), lambda i: (i, (_QKVO + 2 * D_LRU) // LANES))],
        out_shape=[jax.ShapeDtypeStruct((1, n), F32), jax.ShapeDtypeStruct(dproj.shape, BF16)],
        input_output_aliases={1: 1},
        compiler_params=_params(("arbitrary",)),
    )(dgate, dproj)


def _pick(n, *cands):
    for c in cands:
        if n % c == 0:
            return c
    raise ValueError(f"no tile for {n}")


def _behind(a, token):
    return a if token is None else a + token[0:1, 0:1].astype(a.dtype).reshape((1,) * a.ndim)


class _Gathered:
    def __init__(self, w):
        self.w = w

    def begin(self):
        return None

    def mid(self, grp, after):
        return None

    def end(self, grp, after):
        return self.w

    def reduce_early(self, grads):
        return None

    def reduce_early_mid(self, after):
        return None

    def reduce_late(self, grads):
        return None

    def reduce_late_mid(self, after):
        return None


def _local_step(x, target, w, comm):
    s, d = x.shape
    nc = s // CHUNK
    tm = _pick(s, 1024, 512, 256)
    tn_proj = _pick(_PROJ_PAD, 896)
    gate_col = 4 * D_MLSTM + 2 * D_LRU
    w = dict(w)

    token = comm.begin()
    n1, rstd1 = _rmsnorm_fwd("norm_mix_fwd", x, _behind(w["norm_mix_g"], token))
    comm.mid(0, n1)
    w.update(comm.end(0, None))
    proj = _mm_nn("proj_fwd", n1, w["w_in"], tm, tn_proj, d)
    token = comm.mid(1, proj)
    gates = proj[:, gate_col:gate_col + 2 * HEADS]
    gates_t = gates.reshape(nc, CHUNK, 2 * HEADS).transpose(0, 2, 1)
    bias_row = _behind(jnp.pad(w["b_gate_m"], ((0, 0), (0, LANES - 2 * HEADS))), token)
    bias_col = w["b_gate_m"].reshape(2 * HEADS, 1)
    mix, cprev, nprev, mprev = _mlstm_fwd(proj, gates_t, bias_row, bias_col, w["mlstm_norm_g"])
    mix, hsave = _lru_fwd(proj, mix, w["lru_conv_w"], w["lru_conv_b"], w["lru_wa"], w["lru_wx"],
                          w["lru_ba"], w["lru_bx"], w["lru_lambda"])
    w.update(comm.end(1, hsave))
    token = comm.mid(2, hsave)
    x1 = _mm_nn("out_fwd", mix, w["w_out"], tm, 1024, d, res=x)
    n2, rstd2 = _rmsnorm_fwd("norm_ffn_fwd", x1, _behind(w["norm_ffn_g"], token))
    w.update(comm.end(2, n2))
    token = comm.mid(3, n2)
    gu = _mm_up_fwd("up_fwd", n2, w["w_up"], tm, d)
    act = _ffn_act_fwd(gu, w["ffn_conv_w"], _behind(w["ffn_conv_b"], token))
    w.update(comm.end(3, act))
    d_ff = w["w_down"].shape[0]
    x2 = _mm_nn("down_fwd", act, w["w_down"], min(tm, 512), 1024, d_ff, res=x1)
    loss, dx2, dx2b, g_norm_final = _loss_head("loss_head", x2, w["norm_final_g"], target)

    grads = {"norm_final_g": g_norm_final}
    dact = _mm_nt("down_bwd_x", dx2b, w["w_down"], tm, d_ff // N_CHIPS, d)
    grads["w_down"] = _mm_tn("down_bwd_w", act, dx2b, d_ff // N_CHIPS, 1024, 2048)
    dgu, grads["ffn_conv_w"], grads["ffn_conv_b"] = _ffn_act_bwd(gu, dact, w["ffn_conv_w"], w["ffn_conv_b"])
    dn2 = _mm_up_bwd_x("up_bwd_x", dgu, w["w_up"], tm, 1024)
    grads["w_up"] = _mm_up_bwd_w("up_bwd_w", n2, dgu, 1024, 2048)
    dx1, dx1b, grads["norm_ffn_g"] = _rmsnorm_bwd("norm_ffn_bwd", x1, rstd2, w["norm_ffn_g"], dn2, dx2)
    dmix = _mm_nt("out_bwd_x", dx1b, w["w_out"], tm, 1024, d)
    grads["w_out"] = _mm_tn("out_bwd_w", mix, dx1b, 1024, 1024, 2048)
    token = comm.reduce_early(grads)
    dproj, dgate, grads["mlstm_norm_g"] = _mlstm_bwd(proj, gates_t, _behind(bias_row, token), bias_col,
                                                     w["mlstm_norm_g"], cprev, nprev, mprev, dmix)
    token = comm.reduce_early_mid(dproj)
    (dproj, grads["lru_conv_w"], grads["lru_conv_b"], grads["lru_wa"], grads["lru_wx"],
     grads["lru_ba"], grads["lru_bx"], grads["lru_lambda"]) = _lru_bwd(
        proj, hsave, dmix, dproj, w["lru_conv_w"], _behind(w["lru_conv_b"], token), w["lru_wa"], w["lru_wx"],
        w["lru_ba"], w["lru_bx"], w["lru_lambda"])
    gate_bias_grad, dproj = _gate_grads(dgate, dproj)
    grads["b_gate_m"] = gate_bias_grad[:, :2 * HEADS]
    grads["w_in"] = _mm_tn("proj_bwd_w", n1, dproj, 1024, tn_proj, 2048)
    token = comm.reduce_late(grads)
    dn1 = _mm_nt("proj_bwd_x", dproj, w["w_in"], tm, 512, _PROJ_PAD, after=token)
    token = comm.reduce_late_mid(dn1)
    grad_x, _, grads["norm_mix_g"] = _rmsnorm_bwd("norm_mix_bwd", x, rstd1, _behind(w["norm_mix_g"], token),
                                                  dn1, dx1)
    return loss, grad_x, grads


WEIGHT_NAMES = ("norm_mix_g", "w_in", "b_gate_m", "mlstm_norm_g", "lru_conv_w", "lru_conv_b", "lru_wa", "lru_ba",
                "lru_wx", "lru_bx", "lru_lambda", "w_out", "norm_ffn_g", "w_up", "ffn_conv_w", "ffn_conv_b",
                "w_down", "norm_final_g")
BIG = ("w_in", "w_out", "w_up", "w_down")
SMALL_SHARDED = ("mlstm_norm_g", "lru_conv_w", "ffn_conv_w")
SMALL = tuple(n for n in WEIGHT_NAMES if n not in BIG)
SMALL_REPLICATED = tuple(n for n in SMALL if n not in SMALL_SHARDED)


def _proj_segments():
    segs = [(0, 0, _QKVO), (_QKVO, _QKVO + 2 * D_LRU, _N_GATES)]
    for n in range(LRU_BLOCKS):
        segs.append((_QKVO + _N_GATES + n * LRU_BLOCK_DIM, _QKVO + 2 * n * LRU_BLOCK_DIM, LRU_BLOCK_DIM))
        segs.append((_QKVO + _N_GATES + D_LRU + n * LRU_BLOCK_DIM, _QKVO + (2 * n + 1) * LRU_BLOCK_DIM,
                     LRU_BLOCK_DIM))
    return segs


def _w_in_shards_to_local(shards):
    width = shards.shape[2]
    pieces = []
    for g0, _, n in sorted(_proj_segments(), key=lambda s: s[1]):
        at = g0
        while at < g0 + n:
            j = at // width
            stop = min(g0 + n, (j + 1) * width)
            pieces.append(shards[j][:, at - j * width:stop - j * width])
            at = stop
    pieces.append(jnp.zeros((shards.shape[1], PROJ_GATE_PAD - _N_GATES), shards.dtype))
    return jnp.concatenate(pieces, axis=1)


def _w_in_local_to_shards(w):
    width = _PROJ_COLS // N_CHIPS
    shards = []
    for j in range(N_CHIPS):
        pieces = []
        for g0, l0, n in sorted(_proj_segments()):
            lo, hi = max(g0, j * width), min(g0 + n, (j + 1) * width)
            if lo < hi:
                pieces.append(w[:, l0 + lo - g0:l0 + hi - g0])
        shards.append(jnp.concatenate(pieces, axis=1))
    return jnp.stack(shards)


def _w_in_to_global(w):
    sh = _w_in_local_to_shards(w)
    return jnp.concatenate([sh[j] for j in range(N_CHIPS)], axis=1)


def _size(shp):
    return functools.reduce(lambda a, b: a * b, shp, 1)


def _lane_dense(shp):
    return len(shp) >= 2 and shp[-1] == LANES and _size(shp) % (HALO * LANES) == 0


def _pack_rows(shapes):
    loose = sum(_size(shp) for shp in shapes if not _lane_dense(shp))
    return sum(_size(shp) // LANES for shp in shapes if _lane_dense(shp)) + -(-loose // (HALO * LANES)) * HALO


def _pack(arrs, rows):
    del rows
    parts = [a.reshape(-1, LANES).astype(F32) for a in arrs if _lane_dense(a.shape)]
    loose = [a.reshape(-1).astype(F32) for a in arrs if not _lane_dense(a.shape)]
    if loose:
        flat = jnp.concatenate(loose)
        n = -(-flat.shape[0] // (HALO * LANES)) * HALO * LANES
        parts.append(jnp.pad(flat, (0, n - flat.shape[0])).reshape(-1, LANES))
    return parts[0] if len(parts) == 1 else jnp.concatenate(parts, axis=0)


def _unpack(buf, shapes):
    out, row = {}, 0
    for i, shp in enumerate(shapes):
        if _lane_dense(shp):
            n = _size(shp) // LANES
            out[i] = buf[row:row + n].reshape(shp)
            row += n
    flat, at = buf[row:].reshape(-1), 0
    for i, shp in enumerate(shapes):
        if not _lane_dense(shp):
            out[i] = flat[at:at + _size(shp)].reshape(shp)
            at += _size(shp)
    return [out[i] for i in range(len(shapes))]


def _assemble_weights(g_in, g_out, g_up, g_down, small_sharded, replicated):
    w = dict(replicated)
    w["w_in"] = _w_in_shards_to_local(g_in)
    w["w_out"] = g_out.reshape(-1, g_out.shape[-1])
    w["w_up"] = g_up
    w["w_down"] = g_down.reshape(-1, g_down.shape[-1])
    for name, v in small_sharded.items():
        w[name] = jnp.concatenate([v[j] for j in range(N_CHIPS)], axis=1)
    return w


def _full_weights_from_global(weights):
    shard = lambda a, axis: jnp.stack(jnp.split(a, N_CHIPS, axis=axis))
    rep = {n: weights[n].reshape(1, -1) if weights[n].ndim <= 2 and n != "b_gate_m" else weights[n]
           for n in SMALL_REPLICATED}
    rep["b_gate_m"] = weights["b_gate_m"].reshape(1, -1)
    return _assemble_weights(shard(weights["w_in"], 1).astype(BF16), shard(weights["w_out"], 0).astype(BF16),
                             shard(weights["w_up"], 1).astype(BF16), shard(weights["w_down"], 0).astype(BF16),
                             {n: shard(weights[n], 1) for n in SMALL_SHARDED}, rep)


def _grads_to_global(grads):
    g = dict(grads)
    g["w_in"] = _w_in_to_global(grads["w_in"])
    g["w_up"] = jnp.concatenate([grads["w_up"][j] for j in range(N_CHIPS)], axis=1)
    return g


def _place():
    x, y, c = lax.axis_index("x"), lax.axis_index("y"), lax.axis_index("c")
    chips = [(1 - x, y), (x, 1 - y), (1 - x, 1 - y)]
    return x, y, c, 2 * x + y, chips


def _half_rows(n_rows, which):
    half = n_rows // 2
    return pl.ds(pl.multiple_of(which * half, 16), half)


def _rcopy(src, dst, send_sem, recv_sem, to):
    return pltpu.make_async_remote_copy(src_ref=src, dst_ref=dst, send_sem=send_sem, recv_sem=recv_sem,
                                        device_id=to, device_id_type=MESH)


HBM_SPEC = pl.BlockSpec(memory_space=pltpu.HBM)
SEM_SPEC = pl.BlockSpec(memory_space=pltpu.SEMAPHORE)
TOKEN_SHAPE = (8, LANES)


def _split_call(name, bufs, sems_in, sems_out_shapes, body_fn, after=None):
    nb, ni, no = len(bufs), len(sems_in), len(sems_out_shapes)
    after = [] if after is None else list(after) if isinstance(after, (list, tuple)) else [after]

    def body(*refs):
        buf_refs = refs[:nb]
        sem_in_refs = refs[nb:nb + ni]
        outs = refs[nb + ni + len(after):]
        sem_out_refs = outs[:no]
        token_ref = outs[no + nb]
        body_fn(buf_refs, sem_in_refs, sem_out_refs)
        token_ref[...] = jnp.zeros_like(token_ref)

    out_shape = ([pltpu.SemaphoreType.DMA(shp) for shp in sems_out_shapes]
                 + [pltpu.HBM(b.shape, b.dtype) for b in bufs] + [jax.ShapeDtypeStruct(TOKEN_SHAPE, F32)])
    res = pl.pallas_call(
        body, name=name, out_shape=out_shape,
        in_specs=[HBM_SPEC] * nb + [SEM_SPEC] * ni + [ANY] * len(after),
        out_specs=[SEM_SPEC] * no + [HBM_SPEC] * nb + [pl.BlockSpec(memory_space=pltpu.VMEM)],
        input_output_aliases={i: no + i for i in range(nb)},
        compiler_params=pltpu.CompilerParams(has_side_effects=pltpu.SideEffectType.DATAFLOW_SIDE_EFFECTING),
    )(*[pltpu.with_memory_space_constraint(b, pltpu.HBM) for b in bufs], *sems_in, *after)
    return list(res[:no]), list(res[no:no + nb]), res[no + nb]


def _place_own_shard(name, idx, shard, after=None):
    rows, cols = shard.shape
    tr = _row_tile(rows)

    def body(idx_ref, s_ref, *rest):
        rest[-1][...] = s_ref[...].astype(BF16)

    return pl.pallas_call(
        body, name=name, out_shape=jax.ShapeDtypeStruct((N_CHIPS, rows, cols), BF16),
        grid_spec=pltpu.PrefetchScalarGridSpec(
            num_scalar_prefetch=1, grid=(rows // tr,),
            in_specs=[pl.BlockSpec((tr, cols), lambda i, s: (i, 0))] + ([] if after is None else [ANY]),
            out_specs=pl.BlockSpec((None, tr, cols), lambda i, s: (s[1], i, 0))),
        compiler_params=_params(("parallel",)),
    )(idx, shard, *(() if after is None else (after,)))


GATHER_GROUPS = ((0, 4), (1,), (2,), (3,))


def _gather_start(name, lands, groups, after=None):
    members = [w for g in groups for w in GATHER_GROUPS[g]]

    def starts(bufs, _, sems):
        x, y, c, me, chips = _place()
        for gi, g in enumerate(groups):
            for pos, w in enumerate(GATHER_GROUPS[g]):
                buf = bufs[members.index(w)]
                part = buf.at[me] if w == 4 else buf.at[me, _half_rows(buf.shape[1], c)]
                for k, chip in enumerate(chips):
                    _rcopy(part, part, sems[2 * gi].at[3 * pos + k], sems[2 * gi + 1].at[3 * pos + k],
                           (*chip, c)).start()

    shapes = []
    for g in groups:
        shapes += [(3 * len(GATHER_GROUPS[g]),)] * 2
    sems, bufs, token = _split_call(name, [lands[w] for w in members], [], shapes, starts, after=after)
    return ({g: (sems[2 * gi], sems[2 * gi + 1]) for gi, g in enumerate(groups)},
            dict(zip(members, bufs)), token)


def _gather_mid(grp, lands, sems, after):
    members = GATHER_GROUPS[grp]
    big = [w for w in members if w != 4]

    def mid(bufs, sems_in, sems_out):
        x, y, c, me, chips = _place()
        send_sems, recv_sems = sems_in
        for pos, w in enumerate(members):
            for k, chip in enumerate(chips):
                cid = 2 * chip[0] + chip[1]
                buf = bufs[pos]
                mine = buf.at[me] if w == 4 else buf.at[me, _half_rows(buf.shape[1], c)]
                theirs = buf.at[cid] if w == 4 else buf.at[cid, _half_rows(buf.shape[1], c)]
                arrival = _rcopy(mine, theirs, send_sems.at[3 * pos + k], recv_sems.at[3 * pos + k], (*chip, c))
                arrival.wait_recv()
                arrival.wait_send()
                if w != 4:
                    _rcopy(theirs, theirs, sems_out[0].at[3 * big.index(w) + k],
                           sems_out[1].at[3 * big.index(w) + k], (x, y, 1 - c)).start()

    new_sems, bufs, token = _split_call(f"gather_mid_{grp}", [lands[w] for w in members], list(sems),
                                        [(3 * len(big),), (3 * len(big),)], mid, after=after)
    return new_sems, bufs, token


def _gather_end(grp, bufs, sems, after):
    members = GATHER_GROUPS[grp]
    big = [w for w in members if w != 4]

    def end(refs, sems_in, _):
        x, y, c, me, chips = _place()
        send_sems, recv_sems = sems_in
        for pos, w in enumerate(members):
            if w == 4:
                continue
            for k, chip in enumerate(chips):
                cid = 2 * chip[0] + chip[1]
                buf = refs[pos]
                sent = buf.at[cid, _half_rows(buf.shape[1], c)]
                landed = buf.at[cid, _half_rows(buf.shape[1], 1 - c)]
                fwd = _rcopy(sent, landed, send_sems.at[3 * big.index(w) + k], recv_sems.at[3 * big.index(w) + k],
                             (x, y, 1 - c))
                fwd.wait_recv()
                fwd.wait_send()

    _, bufs, token = _split_call(f"gather_end_{grp}", bufs, list(sems), [], end, after=after)
    return bufs, token


def _pair_start(name, grads, extra=None):
    n = len(grads)
    bufs = list(grads) + [lax.empty((g.shape[0], g.shape[1] // 2, g.shape[2]), g.dtype) for g in grads]
    if extra is not None:
        bufs += [extra, lax.empty(extra.shape, extra.dtype)]

    def starts(refs, _, sems):
        x, y, c, _, _ = _place()
        for w in range(n):
            other = _half_rows(refs[w].shape[1], 1 - c)
            _rcopy(refs[w].at[:, other], refs[n + w], sems[0].at[w], sems[1].at[w], (x, y, 1 - c)).start()
        if extra is not None:
            _rcopy(refs[2 * n], refs[2 * n + 1], sems[0].at[n], sems[1].at[n], (x, y, 1 - c)).start()

    count = n + (extra is not None)
    return _split_call(name, bufs, [], [(count,), (count,)], starts)


def _pair_wait(name, n, bufs, sems, after):
    has_extra = len(bufs) > 2 * n

    def waits(refs, sems_in, _):
        x, y, c, _, _ = _place()
        for w in range(n):
            other = _half_rows(refs[w].shape[1], 1 - c)
            cp = _rcopy(refs[w].at[:, other], refs[n + w], sems_in[0].at[w], sems_in[1].at[w], (x, y, 1 - c))
            cp.wait_recv()
            cp.wait_send()
        if has_extra:
            cp = _rcopy(refs[2 * n], refs[2 * n + 1], sems_in[0].at[n], sems_in[1].at[n], (x, y, 1 - c))
            cp.wait_recv()
            cp.wait_send()

    _, bufs, token = _split_call(name, bufs, list(sems), [], waits, after=after)
    return bufs, token


def _chip_start(name, partials, small=None):
    n = len(partials)
    bufs = list(partials) + [lax.empty(p.shape, p.dtype) for p in partials] + ([] if small is None else [small])

    def starts(refs, _, sems):
        _, _, c, me, chips = _place()
        for w in range(n):
            for k, chip in enumerate(chips):
                cid = 2 * chip[0] + chip[1]
                _rcopy(refs[w].at[cid], refs[n + w].at[me], sems[0].at[3 * w + k], sems[1].at[3 * w + k],
                       (*chip, c)).start()
        if small is not None:
            for k, chip in enumerate(chips):
                _rcopy(refs[2 * n].at[me], refs[2 * n].at[me], sems[0].at[3 * n + k], sems[1].at[3 * n + k],
                       (*chip, c)).start()

    count = 3 * (n + (small is not None))
    return _split_call(name, bufs, [], [(count,), (count,)], starts)


def _chip_wait(name, n, bufs, sems, after):
    has_small = len(bufs) > 2 * n

    def waits(refs, sems_in, _):
        _, _, c, me, chips = _place()
        for w in range(n):
            for k, chip in enumerate(chips):
                cid = 2 * chip[0] + chip[1]
                cp = _rcopy(refs[w].at[cid], refs[n + w].at[cid], sems_in[0].at[3 * w + k],
                            sems_in[1].at[3 * w + k], (*chip, c))
                cp.wait_recv()
                cp.wait_send()
        if has_small:
            for k, chip in enumerate(chips):
                cid = 2 * chip[0] + chip[1]
                cp = _rcopy(refs[2 * n].at[me], refs[2 * n].at[cid], sems_in[0].at[3 * n + k],
                            sems_in[1].at[3 * n + k], (*chip, c))
                cp.wait_recv()
                cp.wait_send()

    _, bufs, token = _split_call(name, bufs, list(sems), [], waits, after=after)
    return bufs, token


def _share_start(name, shards):
    n = len(shards)

    def starts(refs, _, sems):
        x, y, c, _, _ = _place()
        for w in range(n):
            mine = refs[w].at[_half_rows(refs[w].shape[0], c)]
            _rcopy(mine, mine, sems[0].at[w], sems[1].at[w], (x, y, 1 - c)).start()

    return _split_call(name, list(shards), [], [(n,), (n,)], starts)


def _share_wait(name, bufs, sems, after):
    def waits(refs, sems_in, _):
        x, y, c, _, _ = _place()
        for w in range(len(refs)):
            cp = _rcopy(refs[w].at[_half_rows(refs[w].shape[0], c)], refs[w].at[_half_rows(refs[w].shape[0], 1 - c)],
                        sems_in[0].at[w], sems_in[1].at[w], (x, y, 1 - c))
            cp.wait_recv()
            cp.wait_send()

    _, bufs, token = _split_call(name, bufs, list(sems), [], waits, after=after)
    return bufs, token


def _small_pair_sum(idx, own, recv):
    rows = own.shape[0]

    def body(idx_ref, a_ref, b_ref, o_ref):
        o_ref[...] = a_ref[...] + b_ref[...]

    blk = pl.BlockSpec((rows, LANES), lambda i, s: (0, 0))
    return pl.pallas_call(
        body, name="small_pair_sum", out_shape=jax.ShapeDtypeStruct((N_CHIPS, rows, LANES), F32),
        grid_spec=pltpu.PrefetchScalarGridSpec(
            num_scalar_prefetch=1, grid=(1,), in_specs=[blk, blk],
            out_specs=pl.BlockSpec((None, rows, LANES), lambda i, s: (s[1], 0, 0))),
        compiler_params=_params(("arbitrary",)),
    )(idx, own, recv)


def _pair_share(name, shards, late=None):
    nb = len(shards)
    nl = 0 if late is None else 1

    def body(*refs):
        srcs = refs[:nb]
        dsts = refs[nb + nl:2 * nb + nl]
        send_sems, recv_sems = refs[2 * nb + 2 * nl:2 * nb + 2 * nl + 2]
        x, y, c, _, _ = _place()
        sibling = (x, y, 1 - c)
        sends = []
        for w in range(nb):
            mine = _half_rows(dsts[w].shape[0], c)
            sends.append(_rcopy(srcs[w].at[mine], dsts[w].at[mine], send_sems.at[w], recv_sems.at[w], sibling))
        if nl:
            late_ref, late_out = refs[nb], refs[2 * nb + 1]
            late_send, late_recv, local_sem = refs[2 * nb + 4:]
            my_id = 4 * x + 2 * y + c
            peer = lambda r: (1 - x if r & 4 else x, 1 - y if r & 2 else y, 1 - c if r & 1 else c)
            local = pltpu.make_async_copy(late_ref, late_out.at[my_id], local_sem)
            local.start()
            for r in range(1, N_DEV):
                sends.append(_rcopy(late_ref, late_out.at[my_id], late_send.at[r - 1], late_recv.at[r - 1],
                                    peer(r)))
        for cp in sends:
            cp.start()
        for w in range(nb):
            other = _half_rows(dsts[w].shape[0], 1 - c)
            _rcopy(srcs[w].at[other], dsts[w].at[other], send_sems.at[w], recv_sems.at[w], sibling).wait_recv()
        if nl:
            for r in range(1, N_DEV):
                frm = peer(r)
                _rcopy(late_ref, late_out.at[4 * frm[0] + 2 * frm[1] + frm[2]], late_send.at[r - 1],
                       late_recv.at[r - 1], frm).wait_recv()
        for cp in sends:
            cp.wait_send()
        if nl:
            local.wait()

    out_shape = [jax.ShapeDtypeStruct(h.shape, h.dtype) for h in shards]
    scratch = [pltpu.SemaphoreType.DMA((nb,)), pltpu.SemaphoreType.DMA((nb,))]
    if nl:
        out_shape.append(jax.ShapeDtypeStruct((N_DEV,) + late.shape, late.dtype))
        scratch += [pltpu.SemaphoreType.DMA((N_DEV - 1,)), pltpu.SemaphoreType.DMA((N_DEV - 1,)),
                    pltpu.SemaphoreType.DMA(())]
    return pl.pallas_call(
        body, name=name, out_shape=out_shape,
        in_specs=[ANY] * (nb + nl), out_specs=[ANY] * (nb + nl), scratch_shapes=scratch,
        input_output_aliases={w: w for w in range(nb)},
    )(*shards, *(() if late is None else (late,)))


def _row_tile(rows):
    return _pick(rows, 128, 64, 16, 8)


def _pair_sum(name, idx, grad, recv):
    n, half, cols = recv.shape
    tr = _row_tile(half)
    nrb = half // tr

    def body(idx_ref, g_ref, r_ref, o_ref):
        o_ref[...] = (g_ref[...] + r_ref[...]).astype(BF16)

    return pl.pallas_call(
        body, name=name, out_shape=jax.ShapeDtypeStruct(recv.shape, BF16),
        grid_spec=pltpu.PrefetchScalarGridSpec(
            num_scalar_prefetch=1, grid=(n - 1, nrb),
            in_specs=[pl.BlockSpec((None, tr, cols), lambda j, i, s: (s[2 + j], s[0] * nrb + i, 0)),
                      pl.BlockSpec((None, tr, cols), lambda j, i, s: (s[2 + j], i, 0))],
            out_specs=pl.BlockSpec((None, tr, cols), lambda j, i, s: (s[2 + j], i, 0))),
        compiler_params=_params(("parallel", "parallel")),
    )(idx, grad, recv)


def _final_sum(name, idx, grad, recv, chip_sums):
    _, half, cols = recv.shape
    tr = _row_tile(half)
    nrb = half // tr

    def body(idx_ref, g_ref, r_ref, p1_ref, p2_ref, p3_ref, o_ref):
        acc = g_ref[...] + r_ref[...]
        for p_ref in (p1_ref, p2_ref, p3_ref):
            acc = acc + p_ref[...].astype(F32)
        o_ref[...] = acc

    slot = lambda which: pl.BlockSpec((None, tr, cols), lambda i, s, which=which: (s[which], i, 0))
    return pl.pallas_call(
        body, name=name, out_shape=jax.ShapeDtypeStruct((2 * half, cols), F32),
        grid_spec=pltpu.PrefetchScalarGridSpec(
            num_scalar_prefetch=1, grid=(nrb,),
            in_specs=[pl.BlockSpec((None, tr, cols), lambda i, s: (s[1], s[0] * nrb + i, 0)),
                      slot(1), slot(2), slot(3), slot(4)],
            out_specs=pl.BlockSpec((tr, cols), lambda i, s: (s[0] * nrb + i, 0))),
        compiler_params=_params(("parallel",)),
    )(idx, grad, recv, chip_sums, chip_sums, chip_sums)


def _pair_sum_all(name, idx, grad, recv):
    _, half, cols = recv.shape
    tr = _row_tile(half)
    nrb = half // tr

    def body(idx_ref, g_ref, r_ref, o_ref):
        o_ref[...] = (g_ref[...] + r_ref[...]).astype(BF16)

    return pl.pallas_call(
        body, name=name, out_shape=jax.ShapeDtypeStruct((half, cols), BF16),
        grid_spec=pltpu.PrefetchScalarGridSpec(
            num_scalar_prefetch=1, grid=(nrb,),
            in_specs=[pl.BlockSpec((None, tr, cols), lambda i, s: (0, s[0] * nrb + i, 0)),
                      pl.BlockSpec((None, tr, cols), lambda i, s: (0, i, 0))],
            out_specs=pl.BlockSpec((tr, cols), lambda i, s: (i, 0))),
        compiler_params=_params(("parallel",)),
    )(idx, grad, recv)


def _final_sum_bf16(name, idx, partial, chip_sums):
    _, half, cols = partial.shape
    tr = _row_tile(half)
    nrb = half // tr

    def body(idx_ref, p0_ref, p1_ref, p2_ref, p3_ref, o_ref):
        acc = p0_ref[...].astype(F32)
        for p_ref in (p1_ref, p2_ref, p3_ref):
            acc = acc + p_ref[...].astype(F32)
        o_ref[...] = acc

    slot = lambda which: pl.BlockSpec((None, tr, cols), lambda i, s, which=which: (s[which], i, 0))
    return pl.pallas_call(
        body, name=name, out_shape=jax.ShapeDtypeStruct((2 * half, cols), F32),
        grid_spec=pltpu.PrefetchScalarGridSpec(
            num_scalar_prefetch=1, grid=(nrb,),
            in_specs=[slot(1), slot(2), slot(3), slot(4)],
            out_specs=pl.BlockSpec((tr, cols), lambda i, s: (s[0] * nrb + i, 0))),
        compiler_params=_params(("parallel",)),
    )(idx, partial, chip_sums, chip_sums, chip_sums)


def _small_sum(name, packs):
    n, rows, _ = packs.shape

    def body(p_ref, o_ref):
        acc = p_ref[0]
        for k in range(1, n):
            acc = acc + p_ref[k]
        o_ref[...] = acc

    return pl.pallas_call(
        body, name=name, out_shape=jax.ShapeDtypeStruct((rows, LANES), F32),
        in_specs=[pl.BlockSpec(memory_space=pltpu.VMEM)], out_specs=pl.BlockSpec(memory_space=pltpu.VMEM),
        compiler_params=pltpu.CompilerParams(vmem_limit_bytes=VMEM_LIMIT),
    )(packs)


def _adamw_math(w, g, m, v):
    m_new = ADAM_B1 * m + (1.0 - ADAM_B1) * g
    v_new = ADAM_B2 * v + (1.0 - ADAM_B2) * (g * g)
    m_hat = m_new / (1.0 - ADAM_B1 ** ADAM_STEP)
    v_hat = v_new / (1.0 - ADAM_B2 ** ADAM_STEP)
    return -ADAM_LR * (m_hat / (jnp.sqrt(v_hat) + ADAM_EPS) + ADAM_WD * w), m_new, v_new


def _adamw_many(name, ws, gs, ms, vs):
    n = len(ws)

    def body(*refs):
        for i in range(n):
            d, m_new, v_new = _adamw_math(refs[i][...], refs[n + i][...], refs[2 * n + i][...],
                                          refs[3 * n + i][...])
            refs[4 * n + i][...] = d
            refs[5 * n + i][...] = m_new
            refs[6 * n + i][...] = v_new

    vmem = pl.BlockSpec(memory_space=pltpu.VMEM)
    res = pl.pallas_call(
        body, name=name, in_specs=[vmem] * (4 * n), out_specs=[vmem] * (3 * n),
        out_shape=[jax.ShapeDtypeStruct(w.shape, F32) for w in ws] * 3,
        compiler_params=pltpu.CompilerParams(vmem_limit_bytes=VMEM_LIMIT),
    )(*ws, *gs, *ms, *vs)
    return res[:n], res[n:2 * n], res[2 * n:]


def _adamw(name, w, g, m, v):
    rows, cols = w.shape
    tr = rows if rows * cols * 4 <= (2 << 20) else _row_tile(rows)

    def body(w_ref, g_ref, m_ref, v_ref, g_out_ref, d_ref, nm_ref, nv_ref):
        gv = g_ref[...]
        g_out_ref[...] = gv
        d_ref[...], nm_ref[...], nv_ref[...] = _adamw_math(w_ref[...], gv, m_ref[...], v_ref[...])

    blk = pl.BlockSpec((tr, cols), lambda i: (i, 0))
    sds = jax.ShapeDtypeStruct((rows, cols), F32)
    return pl.pallas_call(
        body, name=name, grid=(rows // tr,), in_specs=[blk] * 4, out_specs=[blk] * 4, out_shape=[sds] * 4,
        compiler_params=_params(("parallel",)),
    )(w, g, m, v)


def _train_step(x, target, W, M, V):
    xi, yi, ci = lax.axis_index("x"), lax.axis_index("y"), lax.axis_index("c")
    me = 2 * xi + yi
    big = {n: W[n][0] for n in BIG}
    big_m = {n: M[n][0] for n in BIG}
    big_v = {n: V[n][0] for n in BIG}

    others = [jnp.where(jnp.int32(i) >= me, i + 1, i) for i in range(N_CHIPS - 1)]
    idx = jnp.stack([ci, me] + others).astype(jnp.int32)

    sharded_shapes = [W[n].shape[1:] for n in SMALL_SHARDED]
    small_pack = _pack([W[n][0] for n in SMALL_SHARDED], _pack_rows(sharded_shapes))
    small_land = lax.dynamic_update_slice(jnp.zeros((N_CHIPS,) + small_pack.shape, F32), small_pack[None],
                                          (me, 0, 0))
    replicated = {n: (W[n].reshape(1, -1) if W[n].ndim <= 2 else W[n][0]) for n in SMALL_REPLICATED}

    early = ("w_out", "w_up", "w_down")
    small_late = "norm_mix_g"
    small_early = tuple(n for n in SMALL if n != small_late)
    global_shape = lambda n: ((W[n].shape[1], W[n].shape[2] * N_CHIPS) if n in SMALL_SHARDED else
                              tuple(W[n].shape) if W[n].ndim == 1 else tuple(W[n].shape[1:]))
    small_shapes = [global_shape(n) for n in small_early]

    def shard_major(n, g):
        if n == "w_in":
            return _w_in_local_to_shards(g)
        return g if g.ndim == 3 else g.reshape((N_CHIPS, -1) + g.shape[1:])

    class _SplitComm:
        def reduce_early(self, grads):
            self.e_sems, self.e_bufs, token = _pair_start("pair_start_early",
                                                          [shard_major(n, grads[n]) for n in early])
            return token

        def reduce_early_mid(self, after):
            n = len(early)
            bufs, _ = _pair_wait("pair_wait_early", n, self.e_bufs, self.e_sems, after)
            self.e_grads, self.e_recv = bufs[:n], bufs[n:2 * n]
            partial = [_pair_sum(f"pair_sum_{nm}", idx, g, r) for nm, g, r in zip(early, self.e_grads, self.e_recv)]
            self.e_sems, self.e_bufs, token = _chip_start("chip_start_early", partial)
            return token

        def reduce_late(self, grads):
            pack = _pack([grads[n] for n in small_early], _pack_rows(small_shapes))
            self.l_sems, self.l_bufs, token = _pair_start("pair_start_late", [grads["w_in"][None]], extra=pack)
            return token

        def reduce_late_mid(self, after):
            bufs, _ = _pair_wait("pair_wait_late", 1, self.l_bufs, self.l_sems, after)
            partial = _w_in_local_to_shards(_pair_sum_all("pair_sum_w_in", idx, bufs[0], bufs[1]))
            self.l_sems, self.l_bufs, token = _chip_start("chip_start_late", [partial],
                                                          small=_small_pair_sum(idx, bufs[2], bufs[3]))
            n = len(early)
            ebufs, _ = _chip_wait("chip_wait_early", n, self.e_bufs, self.e_sems, token)
            halves = [_final_sum(f"final_sum_{nm}", idx, g, r, p)
                      for nm, g, r, p in zip(early, self.e_grads, self.e_recv, ebufs[n:2 * n])]
            self.s_sems, self.s_bufs, token = _share_start("share_start_early", halves)
            return token

        def finish_early(self, after):
            bufs, _ = _share_wait("share_wait_early", self.s_bufs, self.s_sems, after)
            return dict(zip(early, bufs))

        def finish_late(self, after, late):
            bufs, _ = _chip_wait("chip_wait_late", 1, self.l_bufs, self.l_sems, after)
            half = _final_sum_bf16("final_sum_w_in", idx, bufs[0], bufs[1])
            small = dict(zip(small_early, _unpack(_small_sum("small_sum", bufs[2]), small_shapes)))
            whole, late_all = _pair_share("pair_share_late", [half], late)
            return whole, small, _small_sum("late_sum", late_all)

        def begin(self):
            first = {0: _place_own_shard("place_w_in", idx, big["w_in"]), 4: small_land}
            self.sems, self.lands, token = _gather_start("gather_start_0", first, (0,))
            rest = {i: _place_own_shard(f"place_{BIG[i]}", idx, big[BIG[i]], after=token) for i in (1, 2, 3)}
            sems, lands, token = _gather_start("gather_start_1", rest, (1, 2, 3), after=token)
            self.sems.update(sems)
            self.lands.update(lands)
            return token

        def mid(self, grp, after):
            if grp == 0:
                after = [after, big_m["w_in"], big_v["w_in"]]
            self.pending = _gather_mid(grp, self.lands, self.sems[grp], after)
            return self.pending[2]

        def end(self, grp, after):
            sems, bufs, _ = self.pending
            bufs, _ = _gather_end(grp, bufs, sems, after)
            if grp == 0:
                per_chip = [_unpack(bufs[1][j], sharded_shapes) for j in range(N_CHIPS)]
                out = {n: jnp.concatenate([per_chip[j][i] for j in range(N_CHIPS)], axis=1)
                       for i, n in enumerate(SMALL_SHARDED)}
                out["w_in"] = _w_in_shards_to_local(bufs[0])
                return out
            if grp == 2:
                return {"w_up": bufs[0]}
            return {("w_out" if grp == 1 else "w_down"): bufs[0].reshape(-1, bufs[0].shape[-1])}

    comm = _SplitComm()
    loss, grad_x, grads = _local_step(x[0], target[0], replicated, comm)
    loss = lax.psum(loss[0, 0], ("x", "y", "c"))
    out_g, out_d, out_m, out_v = {}, {}, {}, {}

    def update_big(n, grad):
        g, d, nm, nv = _adamw(f"adamw_{n}", big[n], grad, big_m[n], big_v[n])
        out_g[n], out_d[n], out_m[n], out_v[n] = g[None], d[None], nm[None], nv[None]
        return d

    early_grads = comm.finish_early(grad_x)
    done = [update_big(n, early_grads[n]) for n in early]
    late = _pack([grads[small_late]], _pack_rows([global_shape(small_late)]))
    w_in_grad, small_grads, late_sum = comm.finish_late(done, late)
    update_big("w_in", w_in_grad)
    small_grads[small_late] = _unpack(late_sum, [global_shape(small_late)])[0]
    for n in SMALL_SHARDED:
        width = W[n].shape[2]
        small_grads[n] = lax.dynamic_slice_in_dim(small_grads[n], me * width, width, axis=1)

    for n in SMALL:
        out_g[n] = small_grads[n].reshape(W[n].shape)
    two_d = lambda a: a.reshape(1, -1) if a.ndim == 1 else a
    results = _adamw_many("adamw_small", *[[two_d(src[n]) for n in SMALL] for src in (W, out_g, M, V)])
    for dst, arrs in zip((out_d, out_m, out_v), results):
        dst.update({n: a.reshape(W[n].shape) for n, a in zip(SMALL, arrs)})
    return (loss, grad_x[None], *[out_g[n] for n in WEIGHT_NAMES], *[out_d[n] for n in WEIGHT_NAMES],
            *[out_m[n] for n in WEIGHT_NAMES], *[out_v[n] for n in WEIGHT_NAMES])


def kernel(x, norm_mix_g, w_in, b_gate_m, mlstm_norm_g, lru_conv_w, lru_conv_b, lru_wa, lru_ba, lru_wx, lru_bx, lru_lambda, w_out, norm_ffn_g, w_up, ffn_conv_w, ffn_conv_b, w_down, norm_final_g, loss_target, m_norm_mix_g, m_w_in, m_b_gate_m, m_mlstm_norm_g, m_lru_conv_w, m_lru_conv_b, m_lru_wa, m_lru_ba, m_lru_wx, m_lru_bx, m_lru_lambda, m_w_out, m_norm_ffn_g, m_w_up, m_ffn_conv_w, m_ffn_conv_b, m_w_down, m_norm_final_g, v_norm_mix_g, v_w_in, v_b_gate_m, v_mlstm_norm_g, v_lru_conv_w, v_lru_conv_b, v_lru_wa, v_lru_ba, v_lru_wx, v_lru_bx, v_lru_lambda, v_w_out, v_norm_ffn_g, v_w_up, v_ffn_conv_w, v_ffn_conv_b, v_w_down, v_norm_final_g):
    W = dict(zip(WEIGHT_NAMES, (norm_mix_g, w_in, b_gate_m, mlstm_norm_g, lru_conv_w, lru_conv_b, lru_wa, lru_ba,
                                lru_wx, lru_bx, lru_lambda, w_out, norm_ffn_g, w_up, ffn_conv_w, ffn_conv_b,
                                w_down, norm_final_g)))
    M = dict(zip(WEIGHT_NAMES, (m_norm_mix_g, m_w_in, m_b_gate_m, m_mlstm_norm_g, m_lru_conv_w, m_lru_conv_b,
                                m_lru_wa, m_lru_ba, m_lru_wx, m_lru_bx, m_lru_lambda, m_w_out, m_norm_ffn_g,
                                m_w_up, m_ffn_conv_w, m_ffn_conv_b, m_w_down, m_norm_final_g)))
    V = dict(zip(WEIGHT_NAMES, (v_norm_mix_g, v_w_in, v_b_gate_m, v_mlstm_norm_g, v_lru_conv_w, v_lru_conv_b,
                                v_lru_wa, v_lru_ba, v_lru_wx, v_lru_bx, v_lru_lambda, v_w_out, v_norm_ffn_g,
                                v_w_up, v_ffn_conv_w, v_ffn_conv_b, v_w_down, v_norm_final_g)))
    return _train_step(x, loss_target, W, M, V)
```
